```python
import jax, jax.numpy as jnp
from jax import lax
import numpy as np

D_MODEL = 1024
BATCH = 8
SEQ = 8192
DEPTH = 1

CHUNK = 64
PLE_DIM = 256
HG_HEADS = 4
HG_DK = 128
HG_DV = (D_MODEL // 2) // HG_HEADS
HG_WIDTH = HG_HEADS * HG_DV
ML_HEADS = 4
ML_DV = (D_MODEL // 2) // ML_HEADS
ML_DQK = ML_DV // 2
ML_WIDTH = ML_HEADS * ML_DV
MIX_WIDTH = HG_WIDTH + ML_WIDTH
CONV_K = 4
D_FF = ((8 * D_MODEL + 3 * 256 - 1) // (3 * 256)) * 256
ALPHA = float((2 * DEPTH) ** 0.25)
BETA = float((8 * DEPTH) ** -0.25)
LN_EPS = 1e-5
RMS_EPS = 1e-6
ML_I_BIAS = -2.0
ML_F_BIAS_LO = 3.0
ML_F_BIAS_HI = 6.0
PROJ_SIZES = (
    HG_HEADS * HG_DK,
    HG_HEADS * HG_DK,
    HG_WIDTH,
    HG_WIDTH,
    ML_HEADS * ML_DQK,
    ML_HEADS * ML_DQK,
    ML_WIDTH,
    ML_WIDTH,
    ML_HEADS,
    ML_HEADS,
)
PROJ_WIDTH = sum(PROJ_SIZES)

kernel_name = "hymba_hgrn2_mlstm_deepnorm_block"


def _split_cols(u):
    idx, acc = [], 0
    for s in PROJ_SIZES[:-1]:
        acc += s
        idx.append(acc)
    return jnp.split(u, idx, axis=-1)


def layer_norm(x, g, b):
    xf = x.astype(jnp.float32)
    mu = jnp.mean(xf, -1, keepdims=True)
    var = jnp.mean(jnp.square(xf - mu), -1, keepdims=True)
    return ((xf - mu) * lax.rsqrt(var + LN_EPS)).astype(x.dtype) * g + b


def head_rms_norm(h, g):
    hf = h.astype(jnp.float32)
    hf = hf * lax.rsqrt(jnp.mean(hf * hf, -1, keepdims=True) + RMS_EPS)
    B, S, H, Dh = h.shape
    return hf.reshape(B, S, H * Dh).astype(g.dtype) * g


def causal_conv(x, w, b):
    S = x.shape[1]
    xp = jnp.pad(x, ((0, 0), (CONV_K - 1, 0), (0, 0)))
    out = xp[:, 0:S] * w[0]
    for k in range(1, CONV_K):
        out = out + xp[:, k:k + S] * w[k]
    return out + b


def to_chunks(t):
    B, S, H, D = t.shape
    return t.reshape(B, S // CHUNK, CHUNK, H, D).transpose(1, 0, 3, 2, 4)


def gate_chunks(t):
    B, S, H = t.shape
    return t.reshape(B, S // CHUNK, CHUNK, H).transpose(1, 0, 3, 2)


def from_chunks(t):
    N, B, H, C, D = t.shape
    return t.transpose(1, 0, 3, 2, 4).reshape(B, N * C, H, D)


def hgrn2_mixer(q, log_f, k, v):
    B, S, H, DK = q.shape
    DV = v.shape[-1]
    mask = jnp.tril(jnp.ones((CHUNK, CHUNK), dtype=bool))[:, :, None]

    def step(state, inp):
        q_, g_, k_, v_ = inp
        b = jnp.cumsum(g_, axis=2)
        diff = b[:, :, :, None, :] - b[:, :, None, :, :]
        decay = jnp.exp(jnp.where(mask, diff, -jnp.inf))
        scores = jnp.einsum('bhtd,bhsd,bhtsd->bhts', q_, k_, decay)
        o_intra = jnp.einsum('bhts,bhsv->bhtv', scores, v_)
        o_inter = jnp.einsum('bhtd,bhdv->bhtv', q_ * jnp.exp(b), state)
        b_last = b[:, :, -1:, :]
        k_dec = k_ * jnp.exp(b_last - b)
        new_state = jnp.exp(b_last[:, :, 0, :])[..., None] * state + jnp.einsum('bhsd,bhsv->bhdv', k_dec, v_)
        return new_state, o_intra + o_inter

    state0 = jnp.zeros((B, H, DK, DV), jnp.float32)
    _, o = lax.scan(step, state0, (to_chunks(q), to_chunks(log_f), to_chunks(k), to_chunks(v)))
    return from_chunks(o).astype(v.dtype)


def mlstm_mixer(q, k, v, i_pre, log_f):
    B, S, H, DQK = q.shape
    DV = v.shape[-1]
    q = q * (DQK ** -0.5)
    mask = jnp.tril(jnp.ones((CHUNK, CHUNK), dtype=bool))

    def step(carry, inp):
        C_st, n_st, m_st = carry
        q_, k_, v_, ig, lf = inp
        g = jnp.cumsum(lf, axis=-1)
        dmat = g[..., :, None] - g[..., None, :] + ig[..., None, :]
        dmat = jnp.where(mask, dmat, -jnp.inf)
        m_inter = g + m_st[..., None]
        m_t = jnp.maximum(m_inter, jnp.max(dmat, -1))
        w_intra = jnp.exp(dmat - m_t[..., None])
        w_inter = jnp.exp(m_inter - m_t)
        qk = jnp.einsum('bhtd,bhsd->bhts', q_, k_) * w_intra
        num = jnp.einsum('bhts,bhsv->bhtv', qk, v_) + w_inter[..., None] * jnp.einsum('bhtd,bhdv->bhtv', q_, C_st)
        den = jnp.sum(qk, -1) + w_inter * jnp.einsum('bhtd,bhd->bht', q_, n_st)
        h = num / jnp.maximum(jnp.abs(den), jnp.exp(-m_t))[..., None]
        g_last = g[..., -1]
        a = g_last[..., None] - g + ig
        m_new = jnp.maximum(g_last + m_st, jnp.max(a, -1))
        ws = jnp.exp(a - m_new[..., None])
        w_old = jnp.exp(g_last + m_st - m_new)
        C_new = w_old[..., None, None] * C_st + jnp.einsum('bhs,bhsd,bhsv->bhdv', ws, k_, v_)
        n_new = w_old[..., None] * n_st + jnp.einsum('bhs,bhsd->bhd', ws, k_)
        return (C_new, n_new, m_new), h

    carry0 = (jnp.zeros((B, H, DQK, DV), jnp.float32),
              jnp.zeros((B, H, DQK), jnp.float32),
              jnp.zeros((B, H), jnp.float32))
    _, h = lax.scan(step, carry0, (to_chunks(q), to_chunks(k), to_chunks(v),
                                   gate_chunks(i_pre), gate_chunks(log_f)))
    return from_chunks(h).astype(v.dtype)


def _fwd_setup_inputs(seed: int = 0) -> dict:
    key = jax.random.key(seed)
    ks = jax.random.split(key, 20)
    f32 = jnp.float32
    nrm = lambda k, shape, scale: jax.random.normal(k, shape, f32) * scale
    x = nrm(ks[0], (BATCH, SEQ, D_MODEL), 1.0)
    p = nrm(ks[1], (DEPTH, BATCH, SEQ, PLE_DIM), 1.0)
    w_in = nrm(ks[2], (DEPTH, D_MODEL, PROJ_WIDTH), D_MODEL ** -0.5)
    b_in = nrm(ks[3], (DEPTH, PROJ_WIDTH), 0.02)
    ig_off = PROJ_WIDTH - 2 * ML_HEADS
    fg_off = PROJ_WIDTH - ML_HEADS
    b_in = b_in.at[:, ig_off:fg_off].add(ML_I_BIAS)
    b_in = b_in.at[:, fg_off:].add(jnp.linspace(ML_F_BIAS_LO, ML_F_BIAS_HI, ML_HEADS, dtype=f32))
    hg_lb_logits = nrm(ks[4], (DEPTH + 1, HG_HEADS * HG_DK), 0.5)
    ml_conv_w = nrm(ks[5], (DEPTH, CONV_K, 2 * ML_HEADS * ML_DQK), CONV_K ** -0.5)
    ml_conv_b = nrm(ks[6], (DEPTH, 2 * ML_HEADS * ML_DQK), 0.02)
    hg_norm_g = 1.0 + nrm(ks[7], (DEPTH, HG_WIDTH), 0.02)
    ml_norm_g = 1.0 + nrm(ks[8], (DEPTH, ML_WIDTH), 0.02)
    w_out = nrm(ks[9], (DEPTH, MIX_WIDTH, D_MODEL), BETA * MIX_WIDTH ** -0.5)
    ln1_g = 1.0 + nrm(ks[10], (DEPTH, D_MODEL), 0.02)
    ln1_b = nrm(ks[11], (DEPTH, D_MODEL), 0.02)
    w_ffn_gate = nrm(ks[12], (DEPTH, D_MODEL, D_FF), D_MODEL ** -0.5)
    w_ffn_up = nrm(ks[13], (DEPTH, D_MODEL, D_FF), D_MODEL ** -0.5)
    w_ffn_down = nrm(ks[14], (DEPTH, D_FF, D_MODEL), BETA * D_FF ** -0.5)
    ln2_g = 1.0 + nrm(ks[15], (DEPTH, D_MODEL), 0.02)
    ln2_b = nrm(ks[16], (DEPTH, D_MODEL), 0.02)
    ple_w_proj = nrm(ks[17], (DEPTH, PLE_DIM, D_MODEL), PLE_DIM ** -0.5)
    ple_w_gate = nrm(ks[18], (DEPTH, D_MODEL, D_MODEL), D_MODEL ** -0.5)
    ple_b_gate = nrm(ks[19], (DEPTH, D_MODEL), 0.02)
    return {"x": x, "p": p, "w_in": w_in, "b_in": b_in, "hg_lb_logits": hg_lb_logits,
            "ml_conv_w": ml_conv_w, "ml_conv_b": ml_conv_b, "hg_norm_g": hg_norm_g,
            "ml_norm_g": ml_norm_g, "w_out": w_out, "ln1_g": ln1_g, "ln1_b": ln1_b,
            "w_ffn_gate": w_ffn_gate, "w_ffn_up": w_ffn_up, "w_ffn_down": w_ffn_down,
            "ln2_g": ln2_g, "ln2_b": ln2_b, "ple_w_proj": ple_w_proj,
            "ple_w_gate": ple_w_gate, "ple_b_gate": ple_b_gate}


def _fwd_reference(x, p, w_in, b_in, hg_lb_logits, ml_conv_w, ml_conv_b, hg_norm_g, ml_norm_g,
              w_out, ln1_g, ln1_b, w_ffn_gate, w_ffn_up, w_ffn_down, ln2_g, ln2_b,
              ple_w_proj, ple_w_gate, ple_b_gate):
    B, S, _ = x.shape
    lower_bounds = jnp.cumsum(jax.nn.softmax(hg_lb_logits.astype(jnp.float32), axis=0), axis=0)
    for i in range(DEPTH):
        u = x @ w_in[i] + b_in[i]
        hq, hf, hv, hgate, mq, mk, mv, mo, mig, mfg = _split_cols(u)

        lb = lower_bounds[i]
        log_f = jnp.logaddexp(jnp.log(lb), jnp.log1p(-lb) + jax.nn.log_sigmoid(hf.astype(jnp.float32)))
        k_hg = -jnp.expm1(log_f)
        o_hg = hgrn2_mixer(jax.nn.silu(hq).reshape(B, S, HG_HEADS, HG_DK),
                           log_f.reshape(B, S, HG_HEADS, HG_DK),
                           k_hg.reshape(B, S, HG_HEADS, HG_DK),
                           hv.reshape(B, S, HG_HEADS, HG_DV))
        o_hg = head_rms_norm(o_hg, hg_norm_g[i]) * jax.nn.silu(hgate)

        qk_c = jax.nn.silu(causal_conv(jnp.concatenate([mq, mk], -1), ml_conv_w[i], ml_conv_b[i]))
        mq_c, mk_c = jnp.split(qk_c, 2, axis=-1)
        h_ml = mlstm_mixer(mq_c.reshape(B, S, ML_HEADS, ML_DQK),
                           mk_c.reshape(B, S, ML_HEADS, ML_DQK),
                           mv.reshape(B, S, ML_HEADS, ML_DV),
                           mig.astype(jnp.float32),
                           jax.nn.log_sigmoid(mfg.astype(jnp.float32)))
        o_ml = head_rms_norm(h_ml, ml_norm_g[i]) * jax.nn.sigmoid(mo)

        mix = jnp.concatenate([o_hg, o_ml], -1) @ w_out[i]
        x = layer_norm(ALPHA * x + mix, ln1_g[i], ln1_b[i])

        ffn = (jax.nn.silu(x @ w_ffn_gate[i]) * (x @ w_ffn_up[i])) @ w_ffn_down[i]
        x = layer_norm(ALPHA * x + ffn, ln2_g[i], ln2_b[i])

        x = x + jax.nn.sigmoid(x @ ple_w_gate[i] + ple_b_gate[i]) * (p[i] @ ple_w_proj[i])
    return x


import jax as _jax
import jax.numpy as _jnp

TWIN_FORMAT = 'train_step'
FWD_PARAMS = ['x', 'p', 'w_in', 'b_in', 'hg_lb_logits', 'ml_conv_w', 'ml_conv_b', 'hg_norm_g', 'ml_norm_g', 'w_out', 'ln1_g', 'ln1_b', 'w_ffn_gate', 'w_ffn_up', 'w_ffn_down', 'ln2_g', 'ln2_b', 'ple_w_proj', 'ple_w_gate', 'ple_b_gate']
TWIN_WEIGHTS = ['w_in', 'b_in', 'hg_lb_logits', 'ml_conv_w', 'ml_conv_b', 'hg_norm_g', 'ml_norm_g', 'w_out', 'ln1_g', 'ln1_b', 'w_ffn_gate', 'w_ffn_up', 'w_ffn_down', 'ln2_g', 'ln2_b', 'ple_w_proj', 'ple_w_gate', 'ple_b_gate']
TWIN_DIFF_INPUT = 'x'
TWIN_INPUTS = ['x', 'p', 'w_in', 'b_in', 'hg_lb_logits', 'ml_conv_w', 'ml_conv_b', 'hg_norm_g', 'ml_norm_g', 'w_out', 'ln1_g', 'ln1_b', 'w_ffn_gate', 'w_ffn_up', 'w_ffn_down', 'ln2_g', 'ln2_b', 'ple_w_proj', 'ple_w_gate', 'ple_b_gate', 'loss_target', 'm_w_in', 'm_b_in', 'm_hg_lb_logits', 'm_ml_conv_w', 'm_ml_conv_b', 'm_hg_norm_g', 'm_ml_norm_g', 'm_w_out', 'm_ln1_g', 'm_ln1_b', 'm_w_ffn_gate', 'm_w_ffn_up', 'm_w_ffn_down', 'm_ln2_g', 'm_ln2_b', 'm_ple_w_proj', 'm_ple_w_gate', 'm_ple_b_gate', 'v_w_in', 'v_b_in', 'v_hg_lb_logits', 'v_ml_conv_w', 'v_ml_conv_b', 'v_hg_norm_g', 'v_ml_norm_g', 'v_w_out', 'v_ln1_g', 'v_ln1_b', 'v_w_ffn_gate', 'v_w_ffn_up', 'v_w_ffn_down', 'v_ln2_g', 'v_ln2_b', 'v_ple_w_proj', 'v_ple_w_gate', 'v_ple_b_gate']
TWIN_OUTPUTS = ['loss', 'grad_x', 'grad_w_in', 'grad_b_in', 'grad_hg_lb_logits', 'grad_ml_conv_w', 'grad_ml_conv_b', 'grad_hg_norm_g', 'grad_ml_norm_g', 'grad_w_out', 'grad_ln1_g', 'grad_ln1_b', 'grad_w_ffn_gate', 'grad_w_ffn_up', 'grad_w_ffn_down', 'grad_ln2_g', 'grad_ln2_b', 'grad_ple_w_proj', 'grad_ple_w_gate', 'grad_ple_b_gate', 'delta_w_in', 'delta_b_in', 'delta_hg_lb_logits', 'delta_ml_conv_w', 'delta_ml_conv_b', 'delta_hg_norm_g', 'delta_ml_norm_g', 'delta_w_out', 'delta_ln1_g', 'delta_ln1_b', 'delta_w_ffn_gate', 'delta_w_ffn_up', 'delta_w_ffn_down', 'delta_ln2_g', 'delta_ln2_b', 'delta_ple_w_proj', 'delta_ple_w_gate', 'delta_ple_b_gate', 'new_m_w_in', 'new_m_b_in', 'new_m_hg_lb_logits', 'new_m_ml_conv_w', 'new_m_ml_conv_b', 'new_m_hg_norm_g', 'new_m_ml_norm_g', 'new_m_w_out', 'new_m_ln1_g', 'new_m_ln1_b', 'new_m_w_ffn_gate', 'new_m_w_ffn_up', 'new_m_w_ffn_down', 'new_m_ln2_g', 'new_m_ln2_b', 'new_m_ple_w_proj', 'new_m_ple_w_gate', 'new_m_ple_b_gate', 'new_v_w_in', 'new_v_b_in', 'new_v_hg_lb_logits', 'new_v_ml_conv_w', 'new_v_ml_conv_b', 'new_v_hg_norm_g', 'new_v_ml_norm_g', 'new_v_w_out', 'new_v_ln1_g', 'new_v_ln1_b', 'new_v_w_ffn_gate', 'new_v_w_ffn_up', 'new_v_w_ffn_down', 'new_v_ln2_g', 'new_v_ln2_b', 'new_v_ple_w_proj', 'new_v_ple_w_gate', 'new_v_ple_b_gate']
TWIN_LEAF_KINDS = {'loss': 'loss', 'grad_x': 'grad_x', 'grad_w_in': 'grad_w', 'grad_b_in': 'grad_w', 'grad_hg_lb_logits': 'grad_w', 'grad_ml_conv_w': 'grad_w', 'grad_ml_conv_b': 'grad_w', 'grad_hg_norm_g': 'grad_w', 'grad_ml_norm_g': 'grad_w', 'grad_w_out': 'grad_w', 'grad_ln1_g': 'grad_w', 'grad_ln1_b': 'grad_w', 'grad_w_ffn_gate': 'grad_w', 'grad_w_ffn_up': 'grad_w', 'grad_w_ffn_down': 'grad_w', 'grad_ln2_g': 'grad_w', 'grad_ln2_b': 'grad_w', 'grad_ple_w_proj': 'grad_w', 'grad_ple_w_gate': 'grad_w', 'grad_ple_b_gate': 'grad_w', 'delta_w_in': 'delta_w', 'delta_b_in': 'delta_w', 'delta_hg_lb_logits': 'delta_w', 'delta_ml_conv_w': 'delta_w', 'delta_ml_conv_b': 'delta_w', 'delta_hg_norm_g': 'delta_w', 'delta_ml_norm_g': 'delta_w', 'delta_w_out': 'delta_w', 'delta_ln1_g': 'delta_w', 'delta_ln1_b': 'delta_w', 'delta_w_ffn_gate': 'delta_w', 'delta_w_ffn_up': 'delta_w', 'delta_w_ffn_down': 'delta_w', 'delta_ln2_g': 'delta_w', 'delta_ln2_b': 'delta_w', 'delta_ple_w_proj': 'delta_w', 'delta_ple_w_gate': 'delta_w', 'delta_ple_b_gate': 'delta_w', 'new_m_w_in': 'new_m', 'new_m_b_in': 'new_m', 'new_m_hg_lb_logits': 'new_m', 'new_m_ml_conv_w': 'new_m', 'new_m_ml_conv_b': 'new_m', 'new_m_hg_norm_g': 'new_m', 'new_m_ml_norm_g': 'new_m', 'new_m_w_out': 'new_m', 'new_m_ln1_g': 'new_m', 'new_m_ln1_b': 'new_m', 'new_m_w_ffn_gate': 'new_m', 'new_m_w_ffn_up': 'new_m', 'new_m_w_ffn_down': 'new_m', 'new_m_ln2_g': 'new_m', 'new_m_ln2_b': 'new_m', 'new_m_ple_w_proj': 'new_m', 'new_m_ple_w_gate': 'new_m', 'new_m_ple_b_gate': 'new_m', 'new_v_w_in': 'new_v', 'new_v_b_in': 'new_v', 'new_v_hg_lb_logits': 'new_v', 'new_v_ml_conv_w': 'new_v', 'new_v_ml_conv_b': 'new_v', 'new_v_hg_norm_g': 'new_v', 'new_v_ml_norm_g': 'new_v', 'new_v_w_out': 'new_v', 'new_v_ln1_g': 'new_v', 'new_v_ln1_b': 'new_v', 'new_v_w_ffn_gate': 'new_v', 'new_v_w_ffn_up': 'new_v', 'new_v_w_ffn_down': 'new_v', 'new_v_ln2_g': 'new_v', 'new_v_ln2_b': 'new_v', 'new_v_ple_w_proj': 'new_v', 'new_v_ple_w_gate': 'new_v', 'new_v_ple_b_gate': 'new_v'}


def _forward(args):
    return _fwd_reference(*[args[k] for k in FWD_PARAMS])


def _output_shape():
    def fwd():
        inp = _fwd_setup_inputs(0)
        return _fwd_reference(*[inp[k] for k in FWD_PARAMS])
    out = _jax.eval_shape(fwd)
    return out.shape, out.dtype

N_MICROBATCH = 1
ADAM_LR = 0.001
ADAM_B1 = 0.9
ADAM_B2 = 0.999
ADAM_EPS = 1e-08
ADAM_WD = 0.01
ADAM_STEP = 10
PER_EXAMPLE_BATCH_AXIS = {'x': 0, 'p': 1, 'loss_target': 0}
SHARED_INPUTS = []
_WEIGHT_DTYPES = {'w_in': _jnp.float32, 'b_in': _jnp.float32, 'hg_lb_logits': _jnp.float32, 'ml_conv_w': _jnp.float32, 'ml_conv_b': _jnp.float32, 'hg_norm_g': _jnp.float32, 'ml_norm_g': _jnp.float32, 'w_out': _jnp.float32, 'ln1_g': _jnp.float32, 'ln1_b': _jnp.float32, 'w_ffn_gate': _jnp.float32, 'w_ffn_up': _jnp.float32, 'w_ffn_down': _jnp.float32, 'ln2_g': _jnp.float32, 'ln2_b': _jnp.float32, 'ple_w_proj': _jnp.float32, 'ple_w_gate': _jnp.float32, 'ple_b_gate': _jnp.float32}
MOMENT_SCALE = {'w_in': 1.172553e-01, 'b_in': 2.570770e+00, 'hg_lb_logits': 8.072805e-03, 'ml_conv_w': 1.108862e-01, 'ml_conv_b': 1.195009e-01, 'hg_norm_g': 9.192105e-02, 'ml_norm_g': 2.443201e-01, 'w_out': 2.808007e-01, 'ln1_g': 2.034220e+00, 'ln1_b': 6.365389e+00, 'w_ffn_gate': 5.548969e-02, 'w_ffn_up': 5.467355e-02, 'w_ffn_down': 1.521878e-01, 'ln2_g': 6.592736e+01, 'ln2_b': 6.899133e+00, 'ple_w_proj': 9.172822e-01, 'ple_w_gate': 2.054810e-01, 'ple_b_gate': 6.632431e+00}


def _to_microbatches(a, axis):
    t = _jnp.moveaxis(a, axis, 0)
    t = t.reshape((N_MICROBATCH, t.shape[0] // N_MICROBATCH) + t.shape[1:])
    return _jnp.moveaxis(t, 1, axis + 1)


def setup_inputs(seed: int = 0) -> dict:
    inp = _fwd_setup_inputs(seed)
    key = _jax.random.fold_in(_jax.random.key(seed), 7919)
    shape, _ = _output_shape()
    out = dict(inp)
    out["loss_target"] = _jax.random.normal(_jax.random.fold_in(key, 0), shape, _jnp.float32)
    for i, name in enumerate(TWIN_WEIGHTS):
        w = inp[name].astype(_jnp.float32)
        if MOMENT_SCALE is None:
            s = _jnp.sqrt(_jnp.mean(_jnp.square(w)) + 1e-30)
        else:
            s = MOMENT_SCALE[name]
        km, kv = _jax.random.split(_jax.random.fold_in(key, i + 1))
        out[name] = w
        out["m_" + name] = s * _jax.random.normal(km, w.shape, _jnp.float32)
        out["v_" + name] = (s * s) * _jax.random.uniform(kv, w.shape, _jnp.float32, 0.5, 1.5)
    if N_MICROBATCH > 1:
        for name, axis in PER_EXAMPLE_BATCH_AXIS.items():
            out[name] = _to_microbatches(out[name], axis)
    return {'x': out['x'], 'p': out['p'], 'w_in': out['w_in'], 'b_in': out['b_in'], 'hg_lb_logits': out['hg_lb_logits'], 'ml_conv_w': out['ml_conv_w'], 'ml_conv_b': out['ml_conv_b'], 'hg_norm_g': out['hg_norm_g'], 'ml_norm_g': out['ml_norm_g'], 'w_out': out['w_out'], 'ln1_g': out['ln1_g'], 'ln1_b': out['ln1_b'], 'w_ffn_gate': out['w_ffn_gate'], 'w_ffn_up': out['w_ffn_up'], 'w_ffn_down': out['w_ffn_down'], 'ln2_g': out['ln2_g'], 'ln2_b': out['ln2_b'], 'ple_w_proj': out['ple_w_proj'], 'ple_w_gate': out['ple_w_gate'], 'ple_b_gate': out['ple_b_gate'], 'loss_target': out['loss_target'], 'm_w_in': out['m_w_in'], 'm_b_in': out['m_b_in'], 'm_hg_lb_logits': out['m_hg_lb_logits'], 'm_ml_conv_w': out['m_ml_conv_w'], 'm_ml_conv_b': out['m_ml_conv_b'], 'm_hg_norm_g': out['m_hg_norm_g'], 'm_ml_norm_g': out['m_ml_norm_g'], 'm_w_out': out['m_w_out'], 'm_ln1_g': out['m_ln1_g'], 'm_ln1_b': out['m_ln1_b'], 'm_w_ffn_gate': out['m_w_ffn_gate'], 'm_w_ffn_up': out['m_w_ffn_up'], 'm_w_ffn_down': out['m_w_ffn_down'], 'm_ln2_g': out['m_ln2_g'], 'm_ln2_b': out['m_ln2_b'], 'm_ple_w_proj': out['m_ple_w_proj'], 'm_ple_w_gate': out['m_ple_w_gate'], 'm_ple_b_gate': out['m_ple_b_gate'], 'v_w_in': out['v_w_in'], 'v_b_in': out['v_b_in'], 'v_hg_lb_logits': out['v_hg_lb_logits'], 'v_ml_conv_w': out['v_ml_conv_w'], 'v_ml_conv_b': out['v_ml_conv_b'], 'v_hg_norm_g': out['v_hg_norm_g'], 'v_ml_norm_g': out['v_ml_norm_g'], 'v_w_out': out['v_w_out'], 'v_ln1_g': out['v_ln1_g'], 'v_ln1_b': out['v_ln1_b'], 'v_w_ffn_gate': out['v_w_ffn_gate'], 'v_w_ffn_up': out['v_w_ffn_up'], 'v_w_ffn_down': out['v_w_ffn_down'], 'v_ln2_g': out['v_ln2_g'], 'v_ln2_b': out['v_ln2_b'], 'v_ple_w_proj': out['v_ple_w_proj'], 'v_ple_w_gate': out['v_ple_w_gate'], 'v_ple_b_gate': out['v_ple_b_gate']}


def _loss(weights, diff, rest, loss_target):
    with _jax.named_scope("forward"):
        args = {**rest, TWIN_DIFF_INPUT: diff, **{k: w.astype(_WEIGHT_DTYPES[k]) for k, w in weights.items()}}
        y = _forward(args)
    with _jax.named_scope("loss_head"):
        err = _jnp.square(y.astype(_jnp.float32) - loss_target)
        return 0.5 * _jnp.sum(_jnp.mean(err, axis=-1)) if err.ndim else 0.5 * err


def _adamw(w, g, m, v):
    m = ADAM_B1 * m + (1.0 - ADAM_B1) * g
    v = ADAM_B2 * v + (1.0 - ADAM_B2) * _jnp.square(g)
    m_hat = m / (1.0 - ADAM_B1 ** ADAM_STEP)
    v_hat = v / (1.0 - ADAM_B2 ** ADAM_STEP)
    delta = -ADAM_LR * (m_hat / (_jnp.sqrt(v_hat) + ADAM_EPS) + ADAM_WD * w)
    return delta, m, v


def reference(x, p, w_in, b_in, hg_lb_logits, ml_conv_w, ml_conv_b, hg_norm_g, ml_norm_g, w_out, ln1_g, ln1_b, w_ffn_gate, w_ffn_up, w_ffn_down, ln2_g, ln2_b, ple_w_proj, ple_w_gate, ple_b_gate, loss_target, m_w_in, m_b_in, m_hg_lb_logits, m_ml_conv_w, m_ml_conv_b, m_hg_norm_g, m_ml_norm_g, m_w_out, m_ln1_g, m_ln1_b, m_w_ffn_gate, m_w_ffn_up, m_w_ffn_down, m_ln2_g, m_ln2_b, m_ple_w_proj, m_ple_w_gate, m_ple_b_gate, v_w_in, v_b_in, v_hg_lb_logits, v_ml_conv_w, v_ml_conv_b, v_hg_norm_g, v_ml_norm_g, v_w_out, v_ln1_g, v_ln1_b, v_w_ffn_gate, v_w_ffn_up, v_w_ffn_down, v_ln2_g, v_ln2_b, v_ple_w_proj, v_ple_w_gate, v_ple_b_gate):
    given = dict(x=x, p=p, w_in=w_in, b_in=b_in, hg_lb_logits=hg_lb_logits, ml_conv_w=ml_conv_w, ml_conv_b=ml_conv_b, hg_norm_g=hg_norm_g, ml_norm_g=ml_norm_g, w_out=w_out, ln1_g=ln1_g, ln1_b=ln1_b, w_ffn_gate=w_ffn_gate, w_ffn_up=w_ffn_up, w_ffn_down=w_ffn_down, ln2_g=ln2_g, ln2_b=ln2_b, ple_w_proj=ple_w_proj, ple_w_gate=ple_w_gate, ple_b_gate=ple_b_gate, loss_target=loss_target, m_w_in=m_w_in, m_b_in=m_b_in, m_hg_lb_logits=m_hg_lb_logits, m_ml_conv_w=m_ml_conv_w, m_ml_conv_b=m_ml_conv_b, m_hg_norm_g=m_hg_norm_g, m_ml_norm_g=m_ml_norm_g, m_w_out=m_w_out, m_ln1_g=m_ln1_g, m_ln1_b=m_ln1_b, m_w_ffn_gate=m_w_ffn_gate, m_w_ffn_up=m_w_ffn_up, m_w_ffn_down=m_w_ffn_down, m_ln2_g=m_ln2_g, m_ln2_b=m_ln2_b, m_ple_w_proj=m_ple_w_proj, m_ple_w_gate=m_ple_w_gate, m_ple_b_gate=m_ple_b_gate, v_w_in=v_w_in, v_b_in=v_b_in, v_hg_lb_logits=v_hg_lb_logits, v_ml_conv_w=v_ml_conv_w, v_ml_conv_b=v_ml_conv_b, v_hg_norm_g=v_hg_norm_g, v_ml_norm_g=v_ml_norm_g, v_w_out=v_w_out, v_ln1_g=v_ln1_g, v_ln1_b=v_ln1_b, v_w_ffn_gate=v_w_ffn_gate, v_w_ffn_up=v_w_ffn_up, v_w_ffn_down=v_w_ffn_down, v_ln2_g=v_ln2_g, v_ln2_b=v_ln2_b, v_ple_w_proj=v_ple_w_proj, v_ple_w_gate=v_ple_w_gate, v_ple_b_gate=v_ple_b_gate)
    weights = {n: given[n] for n in TWIN_WEIGHTS}
    shared = {n: given[n] for n in SHARED_INPUTS}
    per_example = {n: given[n] for n in ['x', 'p']}
    grad_fn = _jax.value_and_grad(_loss, argnums=(0, 1))

    def one_microbatch(ex, loss_target):
        ex = dict(ex)
        diff = ex.pop(TWIN_DIFF_INPUT)
        return grad_fn(weights, diff, {**shared, **ex}, loss_target)

    if N_MICROBATCH == 1:
        loss, (grad_w, grad_x) = one_microbatch(per_example, given["loss_target"])
    else:
        def body(carry, xs):
            loss_sum, grad_sum = carry
            l_k, (gw_k, gx_k) = one_microbatch(xs[0], xs[1])
            with _jax.named_scope("update"):
                return (loss_sum + l_k, _jax.tree.map(_jnp.add, grad_sum, gw_k)), gx_k

        init = (_jnp.zeros((), _jnp.float32), _jax.tree.map(_jnp.zeros_like, weights))
        (loss, grad_w), grad_x = _jax.lax.scan(body, init, (per_example, given["loss_target"]))
    with _jax.named_scope("update"):
        delta_w, new_m, new_v = {}, {}, {}
        for n in TWIN_WEIGHTS:
            delta_w[n], new_m[n], new_v[n] = _adamw(weights[n], grad_w[n], given["m_" + n], given["v_" + n])
    return (loss, grad_x, *[grad_w[n] for n in TWIN_WEIGHTS], *[delta_w[n] for n in TWIN_WEIGHTS],
            *[new_m[n] for n in TWIN_WEIGHTS], *[new_v[n] for n in TWIN_WEIGHTS])
```

```python
import functools
import math

import jax
import jax.numpy as jnp
from jax import lax
from jax.experimental import pallas as pl
from jax.experimental.pallas import tpu as pltpu

F32 = jnp.float32
_MXU = jnp.bfloat16

D_MODEL = 1024
CHUNK = 64
SUB = 16
PLE_DIM = 256
HEADS = 4
HG_DK = 128
ML_DQK = 64
HALF = 512
D_FF = 2816
PROJ_W = 3592
PROJ_WP = 3712
ALPHA = float(2 ** 0.25)
LN_EPS = 1e-5
RMS_EPS = 1e-6
ML_SCALE = ML_DQK ** -0.5
N_DEV = 8
LR, B1, B2, EPS, WD, STEP = 0.001, 0.9, 0.999, 1e-08, 0.01, 10
NEG = -1e30

C_HQ, C_HF, C_HV, C_HGATE, C_MQK, C_MV, C_MO, C_GATES = 0, 4, 8, 12, 16, 20, 24, 28

VMEM_LIMIT = 52 * 1024 * 1024

NN = (((1,), (0,)), ((), ()))
NT = (((1,), (1,)), ((), ()))
TN = (((0,), (0,)), ((), ()))


def _dot(a, b, dims=NN):
    return lax.dot_general(a.astype(_MXU), b.astype(_MXU), dims, preferred_element_type=F32)


def _dotx(a, b, dims=NN):
    return lax.dot_general(a, b, dims, precision=lax.Precision.HIGHEST, preferred_element_type=F32)


def _sig(x):
    return jax.nn.sigmoid(x)


def _cp(*sem):
    return pltpu.CompilerParams(dimension_semantics=sem, vmem_limit_bytes=VMEM_LIMIT)


def _row(tm, c, blk=0):
    return pl.BlockSpec((tm, c), lambda i, blk=blk: (i, blk))


def _full(shape):
    nd = len(shape)
    return pl.BlockSpec(tuple(shape), lambda i, nd=nd: (0,) * nd)


def _sds(shape):
    return jax.ShapeDtypeStruct(tuple(shape), F32)


def _iota(shape, axis):
    return lax.broadcasted_iota(jnp.int32, shape, axis)


def _colsum(x):
    return jnp.sum(x, axis=0, keepdims=True)


def _rowsum(x):
    return jnp.sum(x, axis=1, keepdims=True)


def _ln_fwd(z, g, b):
    mu = jnp.mean(z, axis=-1, keepdims=True)
    zc = z - mu
    var = jnp.mean(zc * zc, axis=-1, keepdims=True)
    rstd = lax.rsqrt(var + LN_EPS)
    xhat = zc * rstd
    return xhat * g + b, xhat, rstd


def _ln_bwd(dy, xhat, rstd, g):
    dxh = dy * g
    m1 = jnp.mean(dxh, axis=-1, keepdims=True)
    m2 = jnp.mean(dxh * xhat, axis=-1, keepdims=True)
    return rstd * (dxh - m1 - xhat * m2)


def _dsilu(x, s):
    return s * (1.0 + x * (1.0 - s))


def _in_proj(x, w, b, tm):
    T = x.shape[0]

    def body(x_ref, w_ref, b_ref, o_ref):
        o_ref[...] = _dot(x_ref[...], w_ref[...]) + b_ref[...]

    return pl.pallas_call(
        body, name="in_proj", grid=(T // tm,),
        in_specs=[_row(tm, D_MODEL), _full(w.shape), _full(b.shape)],
        out_specs=_row(tm, PROJ_WP), out_shape=_sds((T, PROJ_WP)),
        compiler_params=_cp("parallel"))(x, w, b)


def _shift_rows(x, halo, j, rowi):
    r = pltpu.roll(x, j, 0)
    top = jnp.where(rowi < j, pltpu.roll(halo, j, 0), r[:8])
    return jnp.concatenate([top, r[8:]], axis=0)


def _shift_rows_up(x, halo, j, rowi):
    n = x.shape[0]
    r = pltpu.roll(x, n - j, 0)
    bot = jnp.where(rowi >= 8 - j, pltpu.roll(halo, 8 - j, 0), r[n - 8:])
    return jnp.concatenate([r[:n - 8], bot], axis=0)


def _conv_fwd(u, cw, cb, tm):
    T = u.shape[0]
    hb = tm // 8

    def body(x_ref, halo_ref, w_ref, b_ref, pre_ref, out_ref):
        i = pl.program_id(0)
        x = x_ref[...]
        halo = jnp.where(i > 0, halo_ref[...], 0.0)
        rowi = _iota((8, HALF), 0)
        acc = x * w_ref[3:4, :] + b_ref[...]
        for j in (1, 2, 3):
            acc = acc + _shift_rows(x, halo, j, rowi) * w_ref[3 - j:4 - j, :]
        pre_ref[...] = acc
        out_ref[...] = acc * _sig(acc)

    return pl.pallas_call(
        body, name="conv_fwd", grid=(T // tm,),
        in_specs=[pl.BlockSpec((tm, HALF), lambda i: (i, C_MQK // 4)),
                  pl.BlockSpec((8, HALF), lambda i: (jnp.maximum(i * hb - 1, 0), C_MQK // 4)),
                  _full(cw.shape), _full(cb.shape)],
        out_specs=[_row(tm, HALF), _row(tm, HALF)], out_shape=[_sds((T, HALF)), _sds((T, HALF))],
        compiler_params=_cp("parallel"))(u, u, cw, cb)


def _hg_gates(hq, hf, lb):
    sg = _sig(hf)
    nsg = _sig(-hf)
    f = lb + (1.0 - lb) * sg
    g = jnp.log(f)
    k = (1.0 - lb) * nsg
    sq = _sig(hq)
    return hq * sq, g, k, f, sg, nsg, sq


def _tri(n, lower):
    r, c = _iota((n, n), 0), _iota((n, n), 1)
    return jnp.where((r >= c) if lower else (c >= r), 1.0, 0.0).astype(F32)


def _hg_diag_tiles(b_sc, r0, rowi):
    bi = b_sc[r0:r0 + SUB, :]
    return [jnp.exp(jnp.where(rowi >= s, bi - b_sc[r0 + s:r0 + s + 1, :], NEG)) for s in range(SUB)]


def _hgrn2_fwd(u, lb_logits):
    T = u.shape[0]
    N = T // CHUNK

    def body(hq_ref, hf_ref, hv_ref, lg_ref, o_ref, st_ref, S_ref, b_sc, k_sc):
        n = pl.program_id(0)

        @pl.when(n == 0)
        def _():
            S_ref[...] = jnp.zeros_like(S_ref)

        lb_all = _sig(lg_ref[0:1, :] - lg_ref[1:2, :])
        tril = _tri(CHUNK, True)
        ones = jnp.ones((128, 128), F32)
        rowi = _iota((SUB, 128), 0)
        for h in range(HEADS):
            sl = slice(128 * h, 128 * h + 128)
            q, g, k, _, _, _, _ = _hg_gates(hq_ref[:, sl], hf_ref[:, sl], lb_all[:, sl])
            b = _dotx(tril, g)
            b_sc[...] = b
            k_sc[...] = k
            ST = S_ref[h]
            st_ref[0, h] = ST
            o = _dot(q * jnp.exp(b), ST, NT)
            outs = []
            for i in range(CHUNK // SUB):
                r0 = SUB * i
                qi = q[r0:r0 + SUB]
                oi = o[r0:r0 + SUB]
                if i > 0:
                    r = b_sc[r0 - 1:r0, :]
                    qe = qi * jnp.exp(b[r0:r0 + SUB] - r)
                    ke = k[:r0] * jnp.exp(r - b[:r0])
                    oi = oi + _dot(_dot(qe, ke, NT), hv_ref[0:r0, sl])
                tiles = _hg_diag_tiles(b_sc, r0, rowi)
                ms = [qi * (k_sc[r0 + s:r0 + s + 1, :] * tiles[s]) for s in range(SUB)]
                R = _dot(jnp.concatenate(ms, axis=0), ones)
                for s in range(SUB):
                    oi = oi + R[SUB * s:SUB * s + SUB] * hv_ref[r0 + s:r0 + s + 1, sl]
                outs.append(oi)
            o_ref[:, sl] = jnp.concatenate(outs, axis=0)
            bl = b_sc[CHUNK - 1:CHUNK, :]
            S_ref[h] = ST * jnp.exp(bl) + _dot(hv_ref[:, sl], k * jnp.exp(bl - b), TN)

    blk = lambda c: pl.BlockSpec((CHUNK, HALF), lambda n, c=c: (n, c // 4))
    return pl.pallas_call(
        body, name="hgrn2_fwd", grid=(N,),
        in_specs=[blk(C_HQ), blk(C_HF), blk(C_HV), _full(lb_logits.shape)],
        out_specs=[pl.BlockSpec((CHUNK, HALF), lambda n: (n, 0)),
                   pl.BlockSpec((1, HEADS, 128, 128), lambda n: (n, 0, 0, 0))],
        out_shape=[_sds((T, HALF)), _sds((N, HEADS, 128, 128))],
        scratch_shapes=[pltpu.VMEM((HEADS, 128, 128), F32), pltpu.VMEM((CHUNK, 128), F32),
                        pltpu.VMEM((CHUNK, 128), F32)],
        compiler_params=_cp("arbitrary"))(u, u, u, lb_logits)


def _lane_col(x, c, lane):
    return _rowsum(jnp.where(lane == c, x, 0.0))


def _sub_row(x, r, sub):
    return _colsum(jnp.where(sub == r, x, 0.0))


def _ml_chunk(q, k, v, gates, h, C, nrow, mprev, consts):
    tril, onehot8, lane, sub8, causal = consts
    lf = jnp.minimum(gates, 0.0) - jnp.log(1.0 + jnp.exp(-jnp.abs(gates)))
    g_all = _dotx(tril, lf)
    g_rows = _dotx(onehot8, g_all, NT)
    i_rows = _dotx(onehot8, gates, NT)
    gcol = _lane_col(g_all, 4 + h, lane)
    icol = _lane_col(gates, h, lane)
    grow = _sub_row(g_rows, 4 + h, sub8)
    irow = _sub_row(i_rows, h, sub8)
    dmat = jnp.where(causal, gcol - grow + irow, NEG)
    m_inter = gcol + mprev
    m_t = jnp.maximum(m_inter, jnp.max(dmat, axis=1, keepdims=True))
    wi = jnp.exp(dmat - m_t)
    wn = jnp.exp(m_inter - m_t)
    s_mat = _dot(q, k, NT) * wi
    qc = _dot(q, C)
    qn = _rowsum(q * nrow)
    num = _dot(s_mat, v) + wn * qc
    den = _rowsum(s_mat) + wn * qn
    floor = jnp.exp(-m_t)
    nrm = jnp.maximum(jnp.abs(den), floor)
    gl = _sub_row(gcol, CHUNK - 1, _iota((CHUNK, 1), 0))
    a_row = gl - grow + irow
    m_new = jnp.maximum(gl + mprev, jnp.max(a_row, axis=1, keepdims=True))
    ws_col = jnp.exp(gl - gcol + icol - m_new)
    wo = jnp.exp(gl + mprev - m_new)
    return dict(wi=wi, wn=wn, s=s_mat, qc=qc, qn=qn, num=num, den=den, floor=floor, nrm=nrm,
                m_new=m_new, ws=ws_col, wo=wo)


def _ml_consts():
    lane = _iota((CHUNK, 128), 1)
    sub8 = _iota((8, CHUNK), 0)
    onehot8 = jnp.where(_iota((8, 128), 0) == _iota((8, 128), 1), 1.0, 0.0).astype(F32)
    causal = _iota((CHUNK, CHUNK), 0) >= _iota((CHUNK, CHUNK), 1)
    return _tri(CHUNK, True), onehot8, lane, sub8, causal


def _mlstm_fwd(qkc, u):
    T = u.shape[0]
    N = T // CHUNK

    def body(qk_ref, v_ref, g_ref, h_ref, cst_ref, nst_ref, mst_ref, C_ref, n_ref, m_ref):
        n = pl.program_id(0)

        @pl.when(n == 0)
        def _():
            C_ref[...] = jnp.zeros_like(C_ref)
            n_ref[...] = jnp.zeros_like(n_ref)
            m_ref[...] = jnp.zeros_like(m_ref)

        consts = _ml_consts()
        gates = g_ref[...]
        for h in range(HEADS):
            q = qk_ref[:, ML_DQK * h:ML_DQK * (h + 1)] * ML_SCALE
            k = qk_ref[:, 256 + ML_DQK * h:256 + ML_DQK * (h + 1)]
            v = v_ref[:, 128 * h:128 * (h + 1)]
            C, nrow, mprev = C_ref[h], n_ref[h], m_ref[h]
            cst_ref[0, h] = C
            nst_ref[0, h] = nrow
            mst_ref[0, h] = mprev
            r = _ml_chunk(q, k, v, gates, h, C, nrow, mprev, consts)
            h_ref[:, 128 * h:128 * (h + 1)] = r["num"] / r["nrm"]
            kw = k * r["ws"]
            C_ref[h] = r["wo"] * C + _dot(kw, v, TN)
            n_ref[h] = r["wo"] * nrow + _colsum(kw)
            m_ref[h] = r["m_new"]

    return pl.pallas_call(
        body, name="mlstm_fwd", grid=(N,),
        in_specs=[pl.BlockSpec((CHUNK, HALF), lambda n: (n, 0)),
                  pl.BlockSpec((CHUNK, HALF), lambda n: (n, C_MV // 4)),
                  pl.BlockSpec((CHUNK, 128), lambda n: (n, C_GATES))],
        out_specs=[pl.BlockSpec((CHUNK, HALF), lambda n: (n, 0)),
                   pl.BlockSpec((1, HEADS, ML_DQK, 128), lambda n: (n, 0, 0, 0)),
                   pl.BlockSpec((1, HEADS, 1, ML_DQK), lambda n: (n, 0, 0, 0)),
                   pl.BlockSpec((1, HEADS, 1, 1), lambda n: (n, 0, 0, 0))],
        out_shape=[_sds((T, HALF)), _sds((N, HEADS, ML_DQK, 128)), _sds((N, HEADS, 1, ML_DQK)),
                   _sds((N, HEADS, 1, 1))],
        scratch_shapes=[pltpu.VMEM((HEADS, ML_DQK, 128), F32), pltpu.VMEM((HEADS, 1, ML_DQK), F32),
                        pltpu.VMEM((HEADS, 1, 1), F32)],
        compiler_params=_cp("arbitrary"))(qkc, u, u)


def _head_norm(o, g):
    rs_parts, r_parts = [], []
    for h in range(HEADS):
        oh = o[:, 128 * h:128 * (h + 1)]
        rs = lax.rsqrt(jnp.mean(oh * oh, axis=-1, keepdims=True) + RMS_EPS)
        rs_parts.append(rs)
        r_parts.append(oh * rs)
    return jnp.concatenate(r_parts, axis=1), rs_parts


def _mix_in(o_hg, h_ml, hgate, mo, g_hg, g_ml):
    r_hg, _ = _head_norm(o_hg, g_hg)
    r_ml, _ = _head_norm(h_ml, g_ml)
    a = r_hg * g_hg * (hgate * _sig(hgate))
    b = r_ml * g_ml * _sig(mo)
    return jnp.concatenate([a, b], axis=1)


def _out_proj_ln(x, u, o_hg, h_ml, g_hg, g_ml, w_out, ln_g, ln_b, tm):
    T = x.shape[0]

    def body(x_ref, hgate_ref, mo_ref, ohg_ref, hml_ref, ghg_ref, gml_ref, w_ref, g_ref, b_ref,
             m_ref, z_ref, x1_ref):
        m = _mix_in(ohg_ref[...], hml_ref[...], hgate_ref[...], mo_ref[...], ghg_ref[...], gml_ref[...])
        m_ref[...] = m
        z = ALPHA * x_ref[...] + _dot(m, w_ref[...])
        z_ref[...] = z
        x1_ref[...] = _ln_fwd(z, g_ref[...], b_ref[...])[0]

    return pl.pallas_call(
        body, name="out_proj_ln1", grid=(T // tm,),
        in_specs=[_row(tm, D_MODEL), _row(tm, HALF, C_HGATE // 4), _row(tm, HALF, C_MO // 4),
                  _row(tm, HALF), _row(tm, HALF), _full(g_hg.shape), _full(g_ml.shape),
                  _full(w_out.shape), _full(ln_g.shape), _full(ln_b.shape)],
        out_specs=[_row(tm, D_MODEL)] * 3, out_shape=[_sds((T, D_MODEL))] * 3,
        compiler_params=_cp("parallel"))(x, u, u, o_hg, h_ml, g_hg, g_ml, w_out, ln_g, ln_b)


def _ffn_ln(x1, wg, wu, wd, ln_g, ln_b, tm):
    T = x1.shape[0]

    def body(x_ref, wg_ref, wu_ref, wd_ref, g_ref, b_ref, z_ref, x2_ref):
        x = x_ref[...]
        a = _dot(x, wg_ref[...])
        hh = a * _sig(a) * _dot(x, wu_ref[...])
        z = ALPHA * x + _dot(hh, wd_ref[...])
        z_ref[...] = z
        x2_ref[...] = _ln_fwd(z, g_ref[...], b_ref[...])[0]

    return pl.pallas_call(
        body, name="ffn_ln2", grid=(T // tm,),
        in_specs=[_row(tm, D_MODEL), _full(wg.shape), _full(wu.shape), _full(wd.shape),
                  _full(ln_g.shape), _full(ln_b.shape)],
        out_specs=[_row(tm, D_MODEL)] * 2, out_shape=[_sds((T, D_MODEL))] * 2,
        compiler_params=_cp("parallel"))(x1, wg, wu, wd, ln_g, ln_b)


def _ple_loss_ln2_bwd(x2, z2, p, tgt, wpg, bpg, wpp, ln_g, ln_b, tm):
    T = x2.shape[0]

    def body(x2_ref, z_ref, p_ref, t_ref, wpg_ref, bpg_ref, wpp_ref, g_ref, b_ref,
             de_ref, dgp_ref, dz_ref, loss_ref, dbpg_ref, dg_ref, db_ref):
        i = pl.program_id(0)

        @pl.when(i == 0)
        def _():
            for r in (loss_ref, dbpg_ref, dg_ref, db_ref):
                r[...] = jnp.zeros_like(r)

        x2 = x2_ref[...]
        gate = _sig(_dot(x2, wpg_ref[...]) + bpg_ref[...])
        e = _dot(p_ref[...], wpp_ref[...])
        err = x2 + gate * e - t_ref[...]
        loss_ref[...] += _colsum(err * err)
        dy = err * (1.0 / D_MODEL)
        de_ref[...] = dy * gate
        dgp = dy * e * gate * (1.0 - gate)
        dgp_ref[...] = dgp
        dbpg_ref[...] += _colsum(dgp)
        dx2 = dy + _dot(dgp, wpg_ref[...], NT)
        _, xhat, rstd = _ln_fwd(z_ref[...], g_ref[...], b_ref[...])
        dg_ref[...] += _colsum(dx2 * xhat)
        db_ref[...] += _colsum(dx2)
        dz_ref[...] = _ln_bwd(dx2, xhat, rstd, g_ref[...])

    vec = _full((1, D_MODEL))
    return pl.pallas_call(
        body, name="ple_loss_ln2_bwd", grid=(T // tm,),
        in_specs=[_row(tm, D_MODEL), _row(tm, D_MODEL), _row(tm, PLE_DIM), _row(tm, D_MODEL),
                  _full(wpg.shape), vec, _full(wpp.shape), vec, vec],
        out_specs=[_row(tm, D_MODEL)] * 3 + [vec] * 4,
        out_shape=[_sds((T, D_MODEL))] * 3 + [_sds((1, D_MODEL))] * 4,
        compiler_params=_cp("arbitrary"))(x2, z2, p, tgt, wpg, bpg, wpp, ln_g, ln_b)


def _ffn_bwd_ln1_bwd(x1, z1, dz2, wg, wu, wd, ln_g, ln_b, tm):
    T = x1.shape[0]

    def body(x_ref, z_ref, dz2_ref, wg_ref, wu_ref, wd_ref, g_ref, b_ref,
             h_ref, da_ref, dbb_ref, dz1_ref, dg_ref, db_ref):
        i = pl.program_id(0)

        @pl.when(i == 0)
        def _():
            dg_ref[...] = jnp.zeros_like(dg_ref)
            db_ref[...] = jnp.zeros_like(db_ref)

        x = x_ref[...]
        dz2 = dz2_ref[...]
        a = _dot(x, wg_ref[...])
        bb = _dot(x, wu_ref[...])
        sa = _sig(a)
        act = a * sa
        h_ref[...] = act * bb
        dh = _dot(dz2, wd_ref[...], NT)
        da = dh * bb * _dsilu(a, sa)
        dbb = dh * act
        da_ref[...] = da
        dbb_ref[...] = dbb
        dx1 = ALPHA * dz2 + _dot(da, wg_ref[...], NT) + _dot(dbb, wu_ref[...], NT)
        _, xhat, rstd = _ln_fwd(z_ref[...], g_ref[...], b_ref[...])
        dg_ref[...] += _colsum(dx1 * xhat)
        db_ref[...] += _colsum(dx1)
        dz1_ref[...] = _ln_bwd(dx1, xhat, rstd, g_ref[...])

    vec = _full((1, D_MODEL))
    return pl.pallas_call(
        body, name="ffn_bwd_ln1_bwd", grid=(T // tm,),
        in_specs=[_row(tm, D_MODEL)] * 3 + [_full(wg.shape), _full(wu.shape), _full(wd.shape), vec, vec],
        out_specs=[_row(tm, D_FF)] * 3 + [_row(tm, D_MODEL), vec, vec],
        out_shape=[_sds((T, D_FF))] * 3 + [_sds((T, D_MODEL)), _sds((1, D_MODEL)), _sds((1, D_MODEL))],
        compiler_params=_cp("arbitrary"))(x1, z1, dz2, wg, wu, wd, ln_g, ln_b)


def _out_proj_bwd(dz1, u, o_hg, h_ml, g_hg, g_ml, w_out, tm):
    T = dz1.shape[0]

    def body(dz_ref, hgate_ref, mo_ref, ohg_ref, hml_ref, ghg_ref, gml_ref, w_ref,
             dohg_ref, dhml_ref, dhgate_ref, dmo_ref, dghg_ref, dgml_ref):
        i = pl.program_id(0)

        @pl.when(i == 0)
        def _():
            dghg_ref[...] = jnp.zeros_like(dghg_ref)
            dgml_ref[...] = jnp.zeros_like(dgml_ref)

        dm = _dot(dz_ref[...], w_ref[...], NT)

        def half(dmh, o, gvec, gate_val, dgate_fac, do_ref, dgate_ref, dgvec_ref):
            r, rs = _head_norm(o, gvec)
            nrm = r * gvec
            dgate_ref[...] = dmh * nrm * dgate_fac
            dn = dmh * gate_val
            dgvec_ref[...] += _colsum(dn * r)
            dr = dn * gvec
            parts = []
            for h in range(HEADS):
                sl = slice(128 * h, 128 * (h + 1))
                parts.append(rs[h] * (dr[:, sl] - r[:, sl] * jnp.mean(dr[:, sl] * r[:, sl], axis=-1, keepdims=True)))
            do_ref[...] = jnp.concatenate(parts, axis=1)

        hg = hgate_ref[...]
        shg = _sig(hg)
        half(dm[:, :HALF], ohg_ref[...], ghg_ref[...], hg * shg, _dsilu(hg, shg), dohg_ref, dhgate_ref, dghg_ref)
        smo = _sig(mo_ref[...])
        half(dm[:, HALF:], hml_ref[...], gml_ref[...], smo, smo * (1.0 - smo), dhml_ref, dmo_ref, dgml_ref)

    vec = _full((1, HALF))
    return pl.pallas_call(
        body, name="out_proj_bwd", grid=(T // tm,),
        in_specs=[_row(tm, D_MODEL), _row(tm, HALF, C_HGATE // 4), _row(tm, HALF, C_MO // 4),
                  _row(tm, HALF), _row(tm, HALF), vec, vec, _full(w_out.shape)],
        out_specs=[_row(tm, HALF)] * 4 + [vec, vec],
        out_shape=[_sds((T, HALF))] * 4 + [_sds((1, HALF))] * 2,
        compiler_params=_cp("arbitrary"))(dz1, u, u, o_hg, h_ml, g_hg, g_ml, w_out)


def _hgrn2_bwd(u, lb_logits, do, states):
    T = u.shape[0]
    N = T // CHUNK

    def body(hq_ref, hf_ref, hv_ref, lg_ref, do_ref, st_ref, dhq_ref, dhf_ref, dhv_ref, dlb_ref,
             dS_ref, b_sc, k_sc):
        i = pl.program_id(0)

        @pl.when(i == 0)
        def _():
            dS_ref[...] = jnp.zeros_like(dS_ref)
            dlb_ref[...] = jnp.zeros_like(dlb_ref)

        lb_all = _sig(lg_ref[0:1, :] - lg_ref[1:2, :])
        tril = _tri(CHUNK, True)
        triu = _tri(CHUNK, False)
        ones = jnp.ones((128, 128), F32)
        rowi = _iota((SUB, 128), 0)
        row64 = _iota((CHUNK, 128), 0)
        lane = _iota((SUB, 128), 1)
        for h in range(HEADS):
            sl = slice(128 * h, 128 * h + 128)
            lb = lb_all[:, sl]
            hq = hq_ref[:, sl]
            q, g, k, f, sg, nsg, sq = _hg_gates(hq, hf_ref[:, sl], lb)
            v = hv_ref[:, sl]
            do_h = do_ref[:, sl]
            b = _dotx(tril, g)
            b_sc[...] = b
            k_sc[...] = k
            ST = st_ref[0, h]
            dST = dS_ref[h]
            bl = b_sc[CHUNK - 1:CHUNK, :]
            eb = jnp.exp(b)
            ebl = jnp.exp(bl - b)
            qt = q * eb
            kl = k * ebl
            dqt = _dot(do_h, ST)
            dkl = _dot(v, dST)
            dv_acc = _dot(kl, dST, NT)
            dq_parts = []
            dk_in = jnp.zeros((CHUNK, 128), F32)
            for i_s in range(CHUNK // SUB):
                r0 = SUB * i_s
                qi = q[r0:r0 + SUB]
                doi = do_h[r0:r0 + SUB]
                dqi = jnp.zeros((SUB, 128), F32)
                if i_s > 0:
                    r = b_sc[r0 - 1:r0, :]
                    eq = jnp.exp(b[r0:r0 + SUB] - r)
                    ek = jnp.exp(r - b[:r0])
                    qe = qi * eq
                    ke = k[:r0] * ek
                    a_off = _dot(qe, ke, NT)
                    p_off = _dot(doi, v[:r0], NT)
                    pad = jnp.zeros((CHUNK - r0, 128), F32)
                    dv_acc = dv_acc + jnp.concatenate([_dot(a_off, doi, TN), pad], axis=0)
                    dqi = dqi + _dot(p_off, ke) * eq
                    dk_in = dk_in + jnp.concatenate([_dot(p_off, qe, TN) * ek, pad], axis=0)
                tiles = _hg_diag_tiles(b_sc, r0, rowi)
                ms = [qi * (k_sc[r0 + s:r0 + s + 1, :] * tiles[s]) for s in range(SUB)]
                ps = [doi * hv_ref[r0 + s:r0 + s + 1, sl] for s in range(SUB)]
                R = _dot(jnp.concatenate(ms + ps, axis=0), ones)
                dv_rows, dk_rows = [], []
                for s in range(SUB):
                    a_s = R[SUB * s:SUB * s + SUB]
                    p_s = R[SUB * (SUB + s):SUB * (SUB + s) + SUB]
                    pt = p_s * tiles[s]
                    dqi = dqi + pt * k_sc[r0 + s:r0 + s + 1, :]
                    dk_rows.append(_colsum(pt * qi))
                    dv_rows.append(_colsum(a_s * doi))
                pad_lo = jnp.zeros((r0, 128), F32)
                pad_hi = jnp.zeros((CHUNK - r0 - SUB, 128), F32)
                dk_in = dk_in + jnp.concatenate([pad_lo] * (r0 > 0) + dk_rows + [pad_hi] * (r0 + SUB < CHUNK), axis=0)
                dv_acc = dv_acc + jnp.concatenate([pad_lo] * (r0 > 0) + dv_rows + [pad_hi] * (r0 + SUB < CHUNK), axis=0)
                dq_parts.append(dqi)
            dq_in = jnp.concatenate(dq_parts, axis=0)
            db = qt * dqt + q * dq_in - k * dk_in - kl * dkl
            last = _colsum(kl * dkl) + jnp.exp(bl) * _colsum(ST * dST)
            db = db + jnp.where(row64 == CHUNK - 1, last, 0.0)
            dg = _dotx(triu, db)
            dq_tot = dqt * eb + dq_in
            dk_tot = dkl * ebl + dk_in
            common = dg / f - dk_tot
            dhf_ref[:, sl] = (1.0 - lb) * sg * nsg * common
            dl0 = _colsum(nsg * common) * lb * (1.0 - lb)
            dlb_ref[0:1, sl] += dl0
            dlb_ref[1:2, sl] -= dl0
            dhq_ref[:, sl] = dq_tot * _dsilu(hq, sq)
            dhv_ref[:, sl] = dv_acc
            dS_ref[h] = dST * jnp.exp(bl) + _dot(do_h, qt, TN)

    rev = lambda c: pl.BlockSpec((CHUNK, HALF), lambda i, c=c: (N - 1 - i, c // 4))
    rev0 = pl.BlockSpec((CHUNK, HALF), lambda i: (N - 1 - i, 0))
    return pl.pallas_call(
        body, name="hgrn2_bwd", grid=(N,),
        in_specs=[rev(C_HQ), rev(C_HF), rev(C_HV), _full(lb_logits.shape), rev0,
                  pl.BlockSpec((1, HEADS, 128, 128), lambda i: (N - 1 - i, 0, 0, 0))],
        out_specs=[rev0, rev0, rev0, _full((2, HALF))],
        out_shape=[_sds((T, HALF))] * 3 + [_sds((2, HALF))],
        scratch_shapes=[pltpu.VMEM((HEADS, 128, 128), F32), pltpu.VMEM((CHUNK, 128), F32),
                        pltpu.VMEM((CHUNK, 128), F32)],
        compiler_params=_cp("arbitrary"))(u, u, u, lb_logits, do, states)


def _mlstm_bwd(qkc, u, dh, cst, nst, mst):
    T = u.shape[0]
    N = T // CHUNK

    def body(qk_ref, v_ref, g_ref, dh_ref, cst_ref, nst_ref, mst_ref, dqk_ref, dv_ref, dgt_ref,
             dC_ref, dn_ref):
        i = pl.program_id(0)

        @pl.when(i == 0)
        def _():
            dC_ref[...] = jnp.zeros_like(dC_ref)
            dn_ref[...] = jnp.zeros_like(dn_ref)

        consts = _ml_consts()
        lane = consts[2]
        triu = _tri(CHUNK, False)
        ones = jnp.ones((CHUNK, 128), F32)
        rowc = _iota((CHUNK, 1), 0)
        gates = g_ref[...]
        dg_all = jnp.zeros((CHUNK, 128), F32)
        di_all = jnp.zeros((CHUNK, 128), F32)
        for h in range(HEADS):
            q = qk_ref[:, ML_DQK * h:ML_DQK * (h + 1)] * ML_SCALE
            k = qk_ref[:, 256 + ML_DQK * h:256 + ML_DQK * (h + 1)]
            v = v_ref[:, 128 * h:128 * (h + 1)]
            dh_h = dh_ref[:, 128 * h:128 * (h + 1)]
            C, nrow, mprev = cst_ref[0, h], nst_ref[0, h], mst_ref[0, h]
            dC, dn = dC_ref[h], dn_ref[h]
            r = _ml_chunk(q, k, v, gates, h, C, nrow, mprev, consts)
            wn, ws, wo, s_mat = r["wn"], r["ws"], r["wo"], r["s"]
            inv = 1.0 / r["nrm"]
            dnum = dh_h * inv
            hh = r["num"] * inv
            dnrm = -_rowsum(dh_h * hh) * inv
            dden = jnp.where(jnp.abs(r["den"]) > r["floor"], dnrm * jnp.sign(r["den"]), 0.0)
            ds = _dot(dnum, v, NT) + dden
            dqk = ds * r["wi"]
            dd = ds * s_mat
            dq = _dot(dqk, k) + wn * (_dot(dnum, C, NT) + dden * nrow)
            dk_st = ws * (_dot(v, dC, NT) + dn)
            dk = _dot(dqk, q, TN) + dk_st
            dv_ref[:, 128 * h:128 * (h + 1)] = _dot(s_mat, dnum, TN) + ws * _dot(k, dC)
            dqk_ref[:, ML_DQK * h:ML_DQK * (h + 1)] = dq * ML_SCALE
            dqk_ref[:, 256 + ML_DQK * h:256 + ML_DQK * (h + 1)] = dk
            dC_ref[h] = wo * dC + _dot(q * wn, dnum, TN)
            dn_ref[h] = wo * dn + _colsum(q * (wn * dden))
            e_col = wn * (_rowsum(dnum * r["qc"]) + dden * r["qn"])
            c_col = _rowsum(k * dk_st)
            z = wo * (jnp.sum(dC * C, keepdims=True) + jnp.sum(dn * nrow, keepdims=True))
            dd_cols = _dotx(dd, ones, TN)[:, 0:1]
            dg_col = _rowsum(dd) - dd_cols + e_col - c_col
            dg_col = dg_col + jnp.where(rowc == CHUNK - 1, jnp.sum(c_col, keepdims=True) + z, 0.0)
            di_col = dd_cols + c_col
            dg_all = dg_all + jnp.where(lane == 4 + h, dg_col, 0.0)
            di_all = di_all + jnp.where(lane == h, di_col, 0.0)
        dlf = _dotx(triu, dg_all)
        dgt_ref[...] = di_all + dlf * _sig(-gates)

    rev0 = pl.BlockSpec((CHUNK, HALF), lambda i: (N - 1 - i, 0))
    st = lambda a, b: pl.BlockSpec((1, HEADS, a, b), lambda i: (N - 1 - i, 0, 0, 0))
    return pl.pallas_call(
        body, name="mlstm_bwd", grid=(N,),
        in_specs=[rev0, pl.BlockSpec((CHUNK, HALF), lambda i: (N - 1 - i, C_MV // 4)),
                  pl.BlockSpec((CHUNK, 128), lambda i: (N - 1 - i, C_GATES)), rev0,
                  st(ML_DQK, 128), st(1, ML_DQK), st(1, 1)],
        out_specs=[rev0, rev0, pl.BlockSpec((CHUNK, 128), lambda i: (N - 1 - i, 0))],
        out_shape=[_sds((T, HALF)), _sds((T, HALF)), _sds((T, 128))],
        scratch_shapes=[pltpu.VMEM((HEADS, ML_DQK, 128), F32), pltpu.VMEM((HEADS, 1, ML_DQK), F32)],
        compiler_params=_cp("arbitrary"))(qkc, u, u, dh, cst, nst, mst)


def _conv_bwd(u, pre, dqkc, cw, tm):
    T = u.shape[0]
    hb = tm // 8
    nb = T // 8

    def body(x_ref, xh_ref, pre_ref, preh_ref, d_ref, dh_ref, w_ref, dx_ref, dw_ref, db_ref):
        i = pl.program_id(0)

        @pl.when(i == 0)
        def _():
            dw_ref[...] = jnp.zeros_like(dw_ref)
            db_ref[...] = jnp.zeros_like(db_ref)

        def dpre_of(pre, d):
            s = _sig(pre)
            return d * _dsilu(pre, s)

        rowi = _iota((8, HALF), 0)
        dpre = dpre_of(pre_ref[...], d_ref[...])
        dpre_next = jnp.where(i < pl.num_programs(0) - 1, dpre_of(preh_ref[...], dh_ref[...]), 0.0)
        x = x_ref[...]
        xprev = jnp.where(i > 0, xh_ref[...], 0.0)
        dx = dpre * w_ref[3:4, :]
        db_ref[...] += _colsum(dpre)
        dws = [None] * 4
        dws[3] = _colsum(dpre * x)
        for j in (1, 2, 3):
            dx = dx + _shift_rows_up(dpre, dpre_next, j, rowi) * w_ref[3 - j:4 - j, :]
            dws[3 - j] = _colsum(dpre * _shift_rows(x, xprev, j, rowi))
        dx_ref[...] = dx
        dw_ref[...] += jnp.concatenate(dws, axis=0)

    cur = lambda blk: pl.BlockSpec((tm, HALF), lambda i, blk=blk: (i, blk))
    nxt = pl.BlockSpec((8, HALF), lambda i: (jnp.minimum((i + 1) * hb, nb - 1), 0))
    return pl.pallas_call(
        body, name="conv_bwd", grid=(T // tm,),
        in_specs=[cur(C_MQK // 4), pl.BlockSpec((8, HALF), lambda i: (jnp.maximum(i * hb - 1, 0), C_MQK // 4)),
                  cur(0), nxt, cur(0), nxt, _full(cw.shape)],
        out_specs=[cur(0), _full((4, HALF)), _full((1, HALF))],
        out_shape=[_sds((T, HALF)), _sds((4, HALF)), _sds((1, HALF))],
        compiler_params=_cp("arbitrary"))(u, u, pre, pre, dqkc, dqkc, cw)


def _in_proj_bwd(dz1, du, w, tm):
    T = dz1.shape[0]

    def body(dz_ref, du_ref, w_ref, dx_ref, db_ref):
        i = pl.program_id(0)

        @pl.when(i == 0)
        def _():
            db_ref[...] = jnp.zeros_like(db_ref)

        du_t = du_ref[...]
        db_ref[...] += _colsum(du_t)
        dx_ref[...] = ALPHA * dz_ref[...] + _dot(du_t, w_ref[...], NT)

    return pl.pallas_call(
        body, name="in_proj_bwd", grid=(T // tm,),
        in_specs=[_row(tm, D_MODEL), _row(tm, PROJ_WP), _full(w.shape)],
        out_specs=[_row(tm, D_MODEL), _full((1, PROJ_WP))],
        out_shape=[_sds((T, D_MODEL)), _sds((1, PROJ_WP))],
        compiler_params=_cp("arbitrary"))(dz1, du, w)


def _wgrad(a, b, name, tm, tn, tk):
    T, M = a.shape
    N = b.shape[1]
    tm, tn, tk = min(tm, M), min(tn, N), min(tk, T)
    nk = T // tk

    def body(a_ref, b_ref, o_ref, acc_ref):
        kk = pl.program_id(2)

        @pl.when(kk == 0)
        def _():
            acc_ref[...] = jnp.zeros_like(acc_ref)

        acc_ref[...] += _dot(a_ref[...], b_ref[...], TN)

        @pl.when(kk == nk - 1)
        def _():
            o_ref[...] = acc_ref[...]

    return pl.pallas_call(
        body, name=name, grid=(M // tm, N // tn, nk),
        in_specs=[pl.BlockSpec((tk, tm), lambda i, j, kk: (kk, i)), pl.BlockSpec((tk, tn), lambda i, j, kk: (kk, j))],
        out_specs=pl.BlockSpec((tm, tn), lambda i, j, kk: (i, j)), out_shape=_sds((M, N)),
        scratch_shapes=[pltpu.VMEM((tm, tn), F32)],
        compiler_params=_cp("parallel", "parallel", "arbitrary"))(a, b)


def _local_step(x, p, tgt, w_in, b_in, lb_logits, conv_w, conv_b, g_hg, g_ml, w_out, ln1_g, ln1_b,
                wg, wu, wd, ln2_g, ln2_b, wpp, wpg, bpg):
    T = x.shape[0]
    tm = min(256, T)
    u = _in_proj(x, w_in, b_in, tm)
    pre, qkc = _conv_fwd(u, conv_w, conv_b, tm)
    o_hg, hg_states = _hgrn2_fwd(u, lb_logits)
    h_ml, cst, nst, mst = _mlstm_fwd(qkc, u)
    m_in, z1, x1 = _out_proj_ln(x, u, o_hg, h_ml, g_hg, g_ml, w_out, ln1_g, ln1_b, tm)
    z2, x2 = _ffn_ln(x1, wg, wu, wd, ln2_g, ln2_b, tm)
    de, dgp, dz2, loss_vec, d_bpg, d_ln2g, d_ln2b = _ple_loss_ln2_bwd(x2, z2, p, tgt, wpg, bpg, wpp, ln2_g, ln2_b, tm)
    hh, da, dbb, dz1, d_ln1g, d_ln1b = _ffn_bwd_ln1_bwd(x1, z1, dz2, wg, wu, wd, ln1_g, ln1_b, tm)
    d_ohg, d_hml, d_hgate, d_mo, d_ghg, d_gml = _out_proj_bwd(dz1, u, o_hg, h_ml, g_hg, g_ml, w_out, tm)
    d_hq, d_hf, d_hv, d_lb = _hgrn2_bwd(u, lb_logits, d_ohg, hg_states)
    d_qkc, d_mv, d_gates = _mlstm_bwd(qkc, u, d_hml, cst, nst, mst)
    d_mqk, d_convw, d_convb = _conv_bwd(u, pre, d_qkc, conv_w, tm)
    du = jnp.concatenate([d_hq, d_hf, d_hv, d_hgate, d_mqk, d_mv, d_mo, d_gates], axis=1)
    dx, d_bin = _in_proj_bwd(dz1, du, w_in, tm)
    wk = dict(tm=512, tn=D_MODEL, tk=512)
    grads = dict(
        w_in=_wgrad(x, du, "wgrad_w_in", 512, PROJ_WP, 512),
        w_out=_wgrad(m_in, dz1, "wgrad_w_out", **wk),
        w_ffn_gate=_wgrad(x1, da, "wgrad_ffn_gate", 512, D_FF, 512),
        w_ffn_up=_wgrad(x1, dbb, "wgrad_ffn_up", 512, D_FF, 512),
        w_ffn_down=_wgrad(hh, dz2, "wgrad_ffn_down", D_FF, D_MODEL, 512),
        ple_w_proj=_wgrad(p, de, "wgrad_ple_proj", **wk),
        ple_w_gate=_wgrad(x2, dgp, "wgrad_ple_gate", **wk),
        b_in=d_bin, hg_lb_logits=d_lb, ml_conv_w=d_convw, ml_conv_b=d_convb, hg_norm_g=d_ghg, ml_norm_g=d_gml,
        ln1_g=d_ln1g, ln1_b=d_ln1b, ln2_g=d_ln2g, ln2_b=d_ln2b, ple_b_gate=d_bpg)
    return loss_vec, dx, grads


W_IN_S, FF_S, OUT_S, PP_S = PROJ_W // N_DEV, D_FF // N_DEV, D_MODEL // N_DEV, D_MODEL // N_DEV
PK_COLS = 1280
C_IN, C_GATE, C_UP = 0, 512, 896
R_OUT, R_DOWN, R_PG, R_PP, PK_ROWS = 1024, 1152, 1504, 1632, 1664
PP_ROWS = PLE_DIM * PP_S // D_MODEL

BIG = ("w_in", "w_ffn_gate", "w_ffn_up", "w_out", "w_ffn_down", "ple_w_gate", "ple_w_proj")
SMALL = dict(b_in=(8, 32, PROJ_WP), hg_lb_logits=(40, 8, 1024), ml_conv_w=(48, 16, 2048), ml_conv_b=(64, 8, 512),
             hg_norm_g=(72, 8, 512), ml_norm_g=(80, 8, 512), ln1_g=(88, 8, 1024), ln1_b=(96, 8, 1024),
             ln2_g=(104, 8, 1024), ln2_b=(112, 8, 1024), ple_b_gate=(120, 8, 1024))
SM_ROWS = 128


def _padc(a, n):
    return jnp.pad(a, [(0, 0)] * (a.ndim - 1) + [(0, n - a.shape[-1])])


def _pack_blocks(w_in, gate, up, w_out, down, pg, pp):
    top = jnp.concatenate([_padc(w_in, C_GATE - C_IN), _padc(gate, C_UP - C_GATE), _padc(up, PK_COLS - C_UP)], axis=-1)
    rest = [_padc(a, PK_COLS) for a in (w_out, down, pg, pp)]
    return jnp.concatenate([top] + rest, axis=-2)


def _unpack_blocks(pk):
    return dict(w_in=pk[..., :D_MODEL, C_IN:C_IN + W_IN_S], w_ffn_gate=pk[..., :D_MODEL, C_GATE:C_GATE + FF_S],
                w_ffn_up=pk[..., :D_MODEL, C_UP:C_UP + FF_S], w_out=pk[..., R_OUT:R_DOWN, :D_MODEL],
                w_ffn_down=pk[..., R_DOWN:R_PG, :D_MODEL], ple_w_gate=pk[..., R_PG:R_PP, :D_MODEL],
                ple_w_proj=pk[..., R_PP:PK_ROWS, :D_MODEL])


def _pack_shard(d):
    return _pack_blocks(d["w_in"], d["w_ffn_gate"], d["w_ffn_up"], d["w_out"], d["w_ffn_down"], d["ple_w_gate"],
                        d["ple_w_proj"].reshape(PP_ROWS, D_MODEL))


def _unpack_shard(pk):
    d = _unpack_blocks(pk)
    d["ple_w_proj"] = d["ple_w_proj"].reshape(PLE_DIM, PP_S)
    return d


def _split_cols(a, n):
    return a.reshape(a.shape[0], N_DEV, n).transpose(1, 0, 2)


def _pack_full(g):
    pp = _split_cols(g["ple_w_proj"], PP_S).reshape(N_DEV, PP_ROWS, D_MODEL)
    return _pack_blocks(_split_cols(g["w_in"][:, :PROJ_W], W_IN_S), _split_cols(g["w_ffn_gate"], FF_S),
                        _split_cols(g["w_ffn_up"], FF_S), g["w_out"].reshape(N_DEV, OUT_S, D_MODEL),
                        g["w_ffn_down"].reshape(N_DEV, FF_S, D_MODEL), g["ple_w_gate"].reshape(N_DEV, OUT_S, D_MODEL), pp)


def _unpack_gathered(gk):
    d = _unpack_blocks(gk)
    join_cols = lambda a: a.transpose(1, 0, 2).reshape(a.shape[1], -1)
    join_rows = lambda a: a.reshape(-1, a.shape[2])
    pp = d["ple_w_proj"].reshape(N_DEV, PLE_DIM, PP_S)
    return dict(w_in=_padc(join_cols(d["w_in"]), PROJ_WP), w_ffn_gate=join_cols(d["w_ffn_gate"]),
                w_ffn_up=join_cols(d["w_ffn_up"]), w_out=join_rows(d["w_out"]), w_ffn_down=join_rows(d["w_ffn_down"]),
                ple_w_gate=join_rows(d["ple_w_gate"]), ple_w_proj=join_cols(pp))


def _pack_small(d, loss_vec=None):
    first = jnp.zeros((8, 128), F32) if loss_vec is None else loss_vec.reshape(8, 128)
    parts = [first]
    for name, (_, rows, n) in SMALL.items():
        parts.append(jnp.pad(d[name].reshape(-1), (0, rows * 128 - n)).reshape(rows, 128))
    return jnp.concatenate(parts, axis=0)


def _unpack_small(slab, shapes):
    return {name: slab[r0:r0 + rows].reshape(-1)[:n].reshape(shapes[name]) for name, (r0, rows, n) in SMALL.items()}


MESH = pl.DeviceIdType.MESH
ANY = pl.BlockSpec(memory_space=pl.ANY)


def _flip(v, bit):
    return 1 - v if bit else v


def _gather_blocks(shard, name):
    def body(x_ref, out_ref, send_sems, recv_sems, local_sem):
        x, y, c = lax.axis_index("x"), lax.axis_index("y"), lax.axis_index("c")
        me, sibling = (x, y, c), (x, y, 1 - c)
        chips = [(1 - x, y), (x, 1 - y), (1 - x, 1 - y)]

        def slab(px, py, pc):
            return out_ref.at[4 * px + 2 * py + pc]

        def copy(k, block, to, src=None):
            return pltpu.make_async_remote_copy(
                src_ref=slab(*block) if src is None else src, dst_ref=slab(*block),
                send_sem=send_sems.at[k], recv_sem=recv_sems.at[k], device_id=to, device_id_type=MESH)

        mine = pltpu.make_async_copy(x_ref, slab(*me), local_sem)
        mine.start()
        first = [copy(0, me, sibling, src=x_ref)]
        first += [copy(1 + j, me, (*chip, c), src=x_ref) for j, chip in enumerate(chips)]
        for cp in first:
            cp.start()
        passed = [copy(4 + j, (*chip, c), sibling) for j, chip in enumerate(chips)]
        for j, chip in enumerate(chips):
            copy(1 + j, (*chip, c), me).wait_recv()
            passed[j].start()
        copy(0, sibling, me).wait_recv()
        for j, chip in enumerate(chips):
            copy(4 + j, (*chip, 1 - c), me).wait_recv()
        for cp in first + passed:
            cp.wait_send()
        mine.wait()

    return pl.pallas_call(
        body, name=name, out_shape=jax.ShapeDtypeStruct((N_DEV,) + shard.shape, shard.dtype),
        in_specs=[ANY], out_specs=ANY,
        scratch_shapes=[pltpu.SemaphoreType.DMA((7,)), pltpu.SemaphoreType.DMA((7,)), pltpu.SemaphoreType.DMA])(shard)


def _exchange_grads(big, small):
    def body(big_ref, small_ref, rb_ref, sg_ref, send_b, recv_b, send_s, recv_s, local_sems):
        x, y, c = lax.axis_index("x"), lax.axis_index("y"), lax.axis_index("c")
        me = 4 * x + 2 * y + c
        loc_b = pltpu.make_async_copy(big_ref.at[me], rb_ref.at[me], local_sems.at[0])
        loc_s = pltpu.make_async_copy(small_ref, sg_ref.at[me], local_sems.at[1])
        loc_b.start()
        loc_s.start()
        copies = []
        for k in range(1, N_DEV):
            px, py, pc = _flip(x, k & 4), _flip(y, k & 2), _flip(c, k & 1)
            peer = 4 * px + 2 * py + pc
            copies.append(pltpu.make_async_remote_copy(
                src_ref=big_ref.at[peer], dst_ref=rb_ref.at[me], send_sem=send_b.at[k - 1], recv_sem=recv_b.at[k - 1],
                device_id=(px, py, pc), device_id_type=MESH))
            copies.append(pltpu.make_async_remote_copy(
                src_ref=small_ref, dst_ref=sg_ref.at[me], send_sem=send_s.at[k - 1], recv_sem=recv_s.at[k - 1],
                device_id=(px, py, pc), device_id_type=MESH))
        for cp in copies:
            cp.start()
        for cp in copies:
            cp.wait()
        loc_b.wait()
        loc_s.wait()

    return pl.pallas_call(
        body, name="exchange_grads",
        out_shape=[jax.ShapeDtypeStruct(big.shape, big.dtype), jax.ShapeDtypeStruct((N_DEV,) + small.shape, small.dtype)],
        in_specs=[ANY, ANY], out_specs=[ANY, ANY],
        scratch_shapes=[pltpu.SemaphoreType.DMA((7,))] * 4 + [pltpu.SemaphoreType.DMA((2,))])(big, small)


def _adamw(w, g, m, v):
    m = B1 * m + (1.0 - B1) * g
    v = B2 * v + (1.0 - B2) * jnp.square(g)
    m_hat = m / (1.0 - B1 ** STEP)
    v_hat = v / (1.0 - B2 ** STEP)
    return -LR * (m_hat / (jnp.sqrt(v_hat) + EPS) + WD * w), m, v


def _sum_slabs(ref):
    g = ref[0].astype(F32)
    for j in range(1, N_DEV):
        g = g + ref[j].astype(F32)
    return g


def _adamw_big(rb, w, m, v, tr):
    R, C = w.shape

    def body(rb_ref, w_ref, m_ref, v_ref, g_ref, d_ref, m2_ref, v2_ref):
        g = _sum_slabs(rb_ref)
        g_ref[...] = g
        d_ref[...], m2_ref[...], v2_ref[...] = _adamw(w_ref[...], g, m_ref[...], v_ref[...])

    blk = pl.BlockSpec((tr, C), lambda i: (i, 0))
    return pl.pallas_call(
        body, name="adamw_big", grid=(R // tr,),
        in_specs=[pl.BlockSpec((N_DEV, tr, C), lambda i: (0, i, 0)), blk, blk, blk],
        out_specs=[blk] * 4, out_shape=[_sds((R, C))] * 4, compiler_params=_cp("parallel"))(rb, w, m, v)


def _adamw_small(sg, w, m, v):
    def body(sg_ref, w_ref, m_ref, v_ref, loss_ref, g_ref, d_ref, m2_ref, v2_ref):
        g = _sum_slabs(sg_ref)
        loss_ref[...] = (0.5 / D_MODEL) * jnp.sum(g[0:8], keepdims=True)
        g_ref[...] = g
        d_ref[...], m2_ref[...], v2_ref[...] = _adamw(w_ref[...], g, m_ref[...], v_ref[...])

    return pl.pallas_call(
        body, name="adamw_small", out_shape=[_sds((1, 1))] + [_sds((SM_ROWS, 128))] * 4)(sg, w, m, v)


WEIGHTS = ("w_in", "b_in", "hg_lb_logits", "ml_conv_w", "ml_conv_b", "hg_norm_g", "ml_norm_g", "w_out", "ln1_g", "ln1_b",
           "w_ffn_gate", "w_ffn_up", "w_ffn_down", "ln2_g", "ln2_b", "ple_w_proj", "ple_w_gate", "ple_b_gate")
CONV_S = HALF // N_DEV


def kernel(x, p, w_in, b_in, hg_lb_logits, ml_conv_w, ml_conv_b, hg_norm_g, ml_norm_g, w_out, ln1_g, ln1_b, w_ffn_gate, w_ffn_up, w_ffn_down, ln2_g, ln2_b, ple_w_proj, ple_w_gate, ple_b_gate, loss_target, m_w_in, m_b_in, m_hg_lb_logits, m_ml_conv_w, m_ml_conv_b, m_hg_norm_g, m_ml_norm_g, m_w_out, m_ln1_g, m_ln1_b, m_w_ffn_gate, m_w_ffn_up, m_w_ffn_down, m_ln2_g, m_ln2_b, m_ple_w_proj, m_ple_w_gate, m_ple_b_gate, v_w_in, v_b_in, v_hg_lb_logits, v_ml_conv_w, v_ml_conv_b, v_hg_norm_g, v_ml_norm_g, v_w_out, v_ln1_g, v_ln1_b, v_w_ffn_gate, v_w_ffn_up, v_w_ffn_down, v_ln2_g, v_ln2_b, v_ple_w_proj, v_ple_w_gate, v_ple_b_gate):
    args = locals()
    me = 4 * lax.axis_index("x") + 2 * lax.axis_index("y") + lax.axis_index("c")
    shapes = {n: args[n].shape for n in WEIGHTS}
    drop = lambda n, a: a[0] if n in BIG or n == "ml_conv_w" else a
    W = {n: drop(n, args[n]) for n in WEIGHTS}
    M = {n: drop(n, args["m_" + n]) for n in WEIGHTS}
    V = {n: drop(n, args["v_" + n]) for n in WEIGHTS}

    gathered = _unpack_gathered(_gather_blocks(_pack_shard(W).astype(_MXU), "gather_weights"))
    conv_full = _gather_blocks(jnp.pad(W["ml_conv_w"], ((0, 4), (0, 128 - CONV_S))), "gather_conv")
    conv_full = conv_full[:, :4, :CONV_S].transpose(1, 0, 2).reshape(4, HALF)

    loss_vec, dx, g = _local_step(
        x[0], p[0, 0], loss_target[0], gathered["w_in"], _padc(b_in, PROJ_WP), hg_lb_logits, conv_full, ml_conv_b,
        hg_norm_g, ml_norm_g, gathered["w_out"], ln1_g, ln1_b, gathered["w_ffn_gate"], gathered["w_ffn_up"],
        gathered["w_ffn_down"], ln2_g, ln2_b, gathered["ple_w_proj"], gathered["ple_w_gate"], ple_b_gate)

    rb, sg = _exchange_grads(_pack_full(g), _pack_small(g, loss_vec))

    gb, db, mb, vb = _adamw_big(rb, _pack_shard(W), _pack_shard(M), _pack_shard(V), 128)
    place = lambda d: {**d, "b_in": _padc(d["b_in"], PROJ_WP),
                       "ml_conv_w": lax.dynamic_update_slice(jnp.zeros((4, HALF), F32), d["ml_conv_w"], (0, me * CONV_S))}
    loss, gs, ds, ms, vs = _adamw_small(sg, _pack_small(place(W)), _pack_small(place(M)), _pack_small(place(V)))

    outs = []
    sm_shapes = {**{n: shapes[n] for n in SMALL}, "b_in": (1, PROJ_WP), "ml_conv_w": (4, HALF)}
    for big_pk, small_pk in ((gb, gs), (db, ds), (mb, ms), (vb, vs)):
        bigs = _unpack_shard(big_pk)
        smalls = _unpack_small(small_pk, sm_shapes)
        smalls["b_in"] = smalls["b_in"][:, :PROJ_W]
        smalls["ml_conv_w"] = lax.dynamic_slice(smalls["ml_conv_w"], (0, me * CONV_S), (4, CONV_S))
        for n in WEIGHTS:
            outs.append((bigs[n] if n in BIG else smalls[n]).reshape(shapes[n]))
    return (loss.reshape(()), dx.reshape(x.shape), *outs)
```

```python
import functools
import math

import jax
import jax.numpy as jnp
from jax import lax
from jax.experimental import pallas as pl
from jax.experimental.pallas import tpu as pltpu

F32 = jnp.float32
_MXU = jnp.bfloat16

D_MODEL = 1024
CHUNK = 64
SUB = 16
PLE_DIM = 256
HEADS = 4
HG_DK = 128
ML_DQK = 64
HALF = 512
D_FF = 2816
PROJ_W = 3592
PROJ_WP = 3712
ALPHA = float(2 ** 0.25)
LN_EPS = 1e-5
RMS_EPS = 1e-6
ML_SCALE = ML_DQK ** -0.5
N_DEV = 8
LR, B1, B2, EPS, WD, STEP = 0.001, 0.9, 0.999, 1e-08, 0.01, 10
NEG = -1e30

C_HQ, C_HF, C_HV, C_HGATE, C_MQK, C_MV, C_MO, C_GATES = 0, 4, 8, 12, 16, 20, 24, 28

VMEM_LIMIT = 52 * 1024 * 1024

NN = (((1,), (0,)), ((), ()))
NT = (((1,), (1,)), ((), ()))
TN = (((0,), (0,)), ((), ()))


def _dot(a, b, dims=NN):
    return lax.dot_general(a.astype(_MXU), b.astype(_MXU), dims, preferred_element_type=F32)


def _dotx(a, b, dims=NN):
    return lax.dot_general(a, b, dims, precision=lax.Precision.HIGHEST, preferred_element_type=F32)


def _sig(x):
    return jax.nn.sigmoid(x)


def _cp(*sem):
    return pltpu.CompilerParams(dimension_semantics=sem, vmem_limit_bytes=VMEM_LIMIT)


def _row(tm, c, blk=0):
    return pl.BlockSpec((tm, c), lambda i, blk=blk: (i, blk))


def _full(shape):
    nd = len(shape)
    return pl.BlockSpec(tuple(shape), lambda i, nd=nd: (0,) * nd)


def _sds(shape):
    return jax.ShapeDtypeStruct(tuple(shape), F32)


def _iota(shape, axis):
    return lax.broadcasted_iota(jnp.int32, shape, axis)


def _colsum(x):
    return jnp.sum(x, axis=0, keepdims=True)


def _rowsum(x):
    return jnp.sum(x, axis=1, keepdims=True)


def _ln_fwd(z, g, b):
    mu = jnp.mean(z, axis=-1, keepdims=True)
    zc = z - mu
    var = jnp.mean(zc * zc, axis=-1, keepdims=True)
    rstd = lax.rsqrt(var + LN_EPS)
    xhat = zc * rstd
    return xhat * g + b, xhat, rstd


def _ln_bwd(dy, xhat, rstd, g):
    dxh = dy * g
    m1 = jnp.mean(dxh, axis=-1, keepdims=True)
    m2 = jnp.mean(dxh * xhat, axis=-1, keepdims=True)
    return rstd * (dxh - m1 - xhat * m2)


def _dsilu(x, s):
    return s * (1.0 + x * (1.0 - s))


MESH = pl.DeviceIdType.MESH
ANY = pl.BlockSpec(memory_space=pl.ANY)


def _flip(v, bit):
    return 1 - v if bit else v


class _Comm:
    def __init__(self, kind, srcs):
        self.kind, self.srcs, self.n = kind, list(srcs), len(srcs)

    def out_shape(self):
        lead = (N_DEV,) if self.kind == "gather" else ()
        return [jax.ShapeDtypeStruct(lead + s.shape, s.dtype) for s in self.srcs]

    def scratch(self):
        return [pltpu.SemaphoreType.DMA((7 * self.n,)), pltpu.SemaphoreType.DMA((7 * self.n,)),
                pltpu.SemaphoreType.DMA((self.n,))]

    def copies(self, srcs, dsts, send_sems, recv_sems, local_sems):
        x, y, c = lax.axis_index("x"), lax.axis_index("y"), lax.axis_index("c")
        me = 4 * x + 2 * y + c
        pick = (lambda s, j: s) if self.kind == "gather" else (lambda s, j: s.at[j])
        out = []
        for i, (s, d) in enumerate(zip(srcs, dsts)):
            out.append(pltpu.make_async_copy(pick(s, me), d.at[me], local_sems.at[i]))
            for k in range(1, N_DEV):
                px, py, pc = _flip(x, k & 4), _flip(y, k & 2), _flip(c, k & 1)
                out.append(pltpu.make_async_remote_copy(
                    src_ref=pick(s, 4 * px + 2 * py + pc), dst_ref=d.at[me], send_sem=send_sems.at[7 * i + k - 1],
                    recv_sem=recv_sems.at[7 * i + k - 1], device_id=(px, py, pc), device_id_type=MESH))
        return out


def _comm_call(comms, name):
    counts = [cm.n for cm in comms]

    def body(*refs):
        total = sum(counts)
        srcs, dsts, sems = refs[:total], refs[total:2 * total], refs[2 * total:]
        copies, o = [], 0
        for j, cm in enumerate(comms):
            copies += cm.copies(srcs[o:o + cm.n], dsts[o:o + cm.n], *sems[3 * j:3 * j + 3])
            o += cm.n
        for cp in copies:
            cp.start()
        for cp in copies:
            cp.wait()

    total = sum(counts)
    res = pl.pallas_call(
        body, name=name, in_specs=[ANY] * total, out_specs=[ANY] * total,
        out_shape=[s for cm in comms for s in cm.out_shape()],
        scratch_shapes=[s for cm in comms for s in cm.scratch()])(*[a for cm in comms for a in cm.srcs])
    out, o = [], 0
    for cm in comms:
        out.append(res[o:o + cm.n])
        o += cm.n
    return out


def _hosted_call(body, comm, *, name, grid, in_specs, out_specs, out_shape, scratch_shapes, args):
    if comm is None:
        res = pl.pallas_call(body, name=name, grid=grid, in_specs=in_specs, out_specs=out_specs, out_shape=out_shape,
                             scratch_shapes=scratch_shapes, compiler_params=_cp("arbitrary"))(*args)
        return list(res), []
    n_in, n_out, n_sc, nc = len(in_specs), len(out_specs), len(scratch_shapes), comm.n
    last = grid[0] - 1

    def hosted(*refs):
        ins, csrc = refs[:n_in], refs[n_in:n_in + nc]
        o0 = n_in + nc
        outs, cdst = refs[o0:o0 + n_out], refs[o0 + n_out:o0 + n_out + nc]
        s0 = o0 + n_out + nc
        scr, sems = refs[s0:s0 + n_sc], refs[s0 + n_sc:]
        i = pl.program_id(0)

        @pl.when(i == 0)
        def _():
            for cp in comm.copies(csrc, cdst, *sems):
                cp.start()

        body(*ins, *outs, *scr)

        @pl.when(i == last)
        def _():
            for cp in comm.copies(csrc, cdst, *sems):
                cp.wait()

    res = pl.pallas_call(
        hosted, name=name, grid=grid, in_specs=list(in_specs) + [ANY] * nc, out_specs=list(out_specs) + [ANY] * nc,
        out_shape=list(out_shape) + comm.out_shape(), scratch_shapes=list(scratch_shapes) + comm.scratch(),
        compiler_params=_cp("arbitrary"))(*args, *comm.srcs)
    return list(res[:n_out]), list(res[n_out:])


def _in_proj(x, w, b, tm):
    T = x.shape[0]

    def body(x_ref, w_ref, b_ref, o_ref):
        o_ref[...] = _dot(x_ref[...], w_ref[...]) + b_ref[...]

    return pl.pallas_call(
        body, name="in_proj", grid=(T // tm,),
        in_specs=[_row(tm, D_MODEL), _full(w.shape), _full(b.shape)],
        out_specs=_row(tm, PROJ_WP), out_shape=_sds((T, PROJ_WP)),
        compiler_params=_cp("parallel"))(x, w, b)


def _shift_rows(x, halo, j, rowi):
    r = pltpu.roll(x, j, 0)
    top = jnp.where(rowi < j, pltpu.roll(halo, j, 0), r[:8])
    return jnp.concatenate([top, r[8:]], axis=0)


def _shift_rows_up(x, halo, j, rowi):
    n = x.shape[0]
    r = pltpu.roll(x, n - j, 0)
    bot = jnp.where(rowi >= 8 - j, pltpu.roll(halo, 8 - j, 0), r[n - 8:])
    return jnp.concatenate([r[:n - 8], bot], axis=0)


def _conv_fwd(u, cw, cb, tm):
    T = u.shape[0]
    hb = tm // 8

    def body(x_ref, halo_ref, w_ref, b_ref, pre_ref, out_ref):
        i = pl.program_id(0)
        x = x_ref[...]
        halo = jnp.where(i > 0, halo_ref[...], 0.0)
        rowi = _iota((8, HALF), 0)
        acc = x * w_ref[3:4, :] + b_ref[...]
        for j in (1, 2, 3):
            acc = acc + _shift_rows(x, halo, j, rowi) * w_ref[3 - j:4 - j, :]
        pre_ref[...] = acc
        out_ref[...] = acc * _sig(acc)

    return pl.pallas_call(
        body, name="conv_fwd", grid=(T // tm,),
        in_specs=[pl.BlockSpec((tm, HALF), lambda i: (i, C_MQK // 4)),
                  pl.BlockSpec((8, HALF), lambda i: (jnp.maximum(i * hb - 1, 0), C_MQK // 4)),
                  _full(cw.shape), _full(cb.shape)],
        out_specs=[_row(tm, HALF), _row(tm, HALF)], out_shape=[_sds((T, HALF)), _sds((T, HALF))],
        compiler_params=_cp("parallel"))(u, u, cw, cb)


def _hg_gates(hq, hf, lb):
    sg = _sig(hf)
    nsg = _sig(-hf)
    f = lb + (1.0 - lb) * sg
    g = jnp.log(f)
    k = (1.0 - lb) * nsg
    sq = _sig(hq)
    return hq * sq, g, k, f, sg, nsg, sq


def _tri(n, lower):
    r, c = _iota((n, n), 0), _iota((n, n), 1)
    return jnp.where((r >= c) if lower else (c >= r), 1.0, 0.0).astype(F32)


def _hg_diag_tiles(b_sc, r0, rowi):
    bi = b_sc[r0:r0 + SUB, :]
    return [jnp.exp(jnp.where(rowi >= s, bi - b_sc[r0 + s:r0 + s + 1, :], NEG)) for s in range(SUB)]


def _hgrn2_fwd(u, lb_logits, comm=None):
    T = u.shape[0]
    N = T // CHUNK

    def body(hq_ref, hf_ref, hv_ref, lg_ref, o_ref, st_ref, S_ref, b_sc, k_sc):
        n = pl.program_id(0)

        @pl.when(n == 0)
        def _():
            S_ref[...] = jnp.zeros_like(S_ref)

        lb_all = _sig(lg_ref[0:1, :] - lg_ref[1:2, :])
        tril = _tri(CHUNK, True)
        ones = jnp.ones((128, 128), F32)
        rowi = _iota((SUB, 128), 0)
        for h in range(HEADS):
            sl = slice(128 * h, 128 * h + 128)
            q, g, k, _, _, _, _ = _hg_gates(hq_ref[:, sl], hf_ref[:, sl], lb_all[:, sl])
            b = _dotx(tril, g)
            b_sc[...] = b
            k_sc[...] = k
            ST = S_ref[h]
            st_ref[0, h] = ST
            o = _dot(q * jnp.exp(b), ST, NT)
            outs = []
            for i in range(CHUNK // SUB):
                r0 = SUB * i
                qi = q[r0:r0 + SUB]
                oi = o[r0:r0 + SUB]
                if i > 0:
                    r = b_sc[r0 - 1:r0, :]
                    qe = qi * jnp.exp(b[r0:r0 + SUB] - r)
                    ke = k[:r0] * jnp.exp(r - b[:r0])
                    oi = oi + _dot(_dot(qe, ke, NT), hv_ref[0:r0, sl])
                tiles = _hg_diag_tiles(b_sc, r0, rowi)
                ms = [qi * (k_sc[r0 + s:r0 + s + 1, :] * tiles[s]) for s in range(SUB)]
                R = _dot(jnp.concatenate(ms, axis=0), ones)
                for s in range(SUB):
                    oi = oi + R[SUB * s:SUB * s + SUB] * hv_ref[r0 + s:r0 + s + 1, sl]
                outs.append(oi)
            o_ref[:, sl] = jnp.concatenate(outs, axis=0)
            bl = b_sc[CHUNK - 1:CHUNK, :]
            S_ref[h] = ST * jnp.exp(bl) + _dot(hv_ref[:, sl], k * jnp.exp(bl - b), TN)

    blk = lambda c: pl.BlockSpec((CHUNK, HALF), lambda n, c=c: (n, c // 4))
    return _hosted_call(
        body, comm, name="hgrn2_fwd", grid=(N,),
        in_specs=[blk(C_HQ), blk(C_HF), blk(C_HV), _full(lb_logits.shape)],
        out_specs=[pl.BlockSpec((CHUNK, HALF), lambda n: (n, 0)),
                   pl.BlockSpec((1, HEADS, 128, 128), lambda n: (n, 0, 0, 0))],
        out_shape=[_sds((T, HALF)), _sds((N, HEADS, 128, 128))],
        scratch_shapes=[pltpu.VMEM((HEADS, 128, 128), F32), pltpu.VMEM((CHUNK, 128), F32),
                        pltpu.VMEM((CHUNK, 128), F32)],
        args=(u, u, u, lb_logits))


def _lane_col(x, c, lane):
    return _rowsum(jnp.where(lane == c, x, 0.0))


def _sub_row(x, r, sub):
    return _colsum(jnp.where(sub == r, x, 0.0))


def _ml_chunk(q, k, v, gates, h, C, nrow, mprev, consts):
    tril, onehot8, lane, sub8, causal = consts
    lf = jnp.minimum(gates, 0.0) - jnp.log(1.0 + jnp.exp(-jnp.abs(gates)))
    g_all = _dotx(tril, lf)
    g_rows = _dotx(onehot8, g_all, NT)
    i_rows = _dotx(onehot8, gates, NT)
    gcol = _lane_col(g_all, 4 + h, lane)
    icol = _lane_col(gates, h, lane)
    grow = _sub_row(g_rows, 4 + h, sub8)
    irow = _sub_row(i_rows, h, sub8)
    dmat = jnp.where(causal, gcol - grow + irow, NEG)
    m_inter = gcol + mprev
    m_t = jnp.maximum(m_inter, jnp.max(dmat, axis=1, keepdims=True))
    wi = jnp.exp(dmat - m_t)
    wn = jnp.exp(m_inter - m_t)
    s_mat = _dot(q, k, NT) * wi
    qc = _dot(q, C)
    qn = _rowsum(q * nrow)
    num = _dot(s_mat, v) + wn * qc
    den = _rowsum(s_mat) + wn * qn
    floor = jnp.exp(-m_t)
    nrm = jnp.maximum(jnp.abs(den), floor)
    gl = _sub_row(gcol, CHUNK - 1, _iota((CHUNK, 1), 0))
    a_row = gl - grow + irow
    m_new = jnp.maximum(gl + mprev, jnp.max(a_row, axis=1, keepdims=True))
    ws_col = jnp.exp(gl - gcol + icol - m_new)
    wo = jnp.exp(gl + mprev - m_new)
    return dict(wi=wi, wn=wn, s=s_mat, qc=qc, qn=qn, num=num, den=den, floor=floor, nrm=nrm,
                m_new=m_new, ws=ws_col, wo=wo)


def _ml_consts():
    lane = _iota((CHUNK, 128), 1)
    sub8 = _iota((8, CHUNK), 0)
    onehot8 = jnp.where(_iota((8, 128), 0) == _iota((8, 128), 1), 1.0, 0.0).astype(F32)
    causal = _iota((CHUNK, CHUNK), 0) >= _iota((CHUNK, CHUNK), 1)
    return _tri(CHUNK, True), onehot8, lane, sub8, causal


def _mlstm_fwd(qkc, u):
    T = u.shape[0]
    N = T // CHUNK

    def body(qk_ref, v_ref, g_ref, h_ref, cst_ref, nst_ref, mst_ref, C_ref, n_ref, m_ref):
        n = pl.program_id(0)

        @pl.when(n == 0)
        def _():
            C_ref[...] = jnp.zeros_like(C_ref)
            n_ref[...] = jnp.zeros_like(n_ref)
            m_ref[...] = jnp.zeros_like(m_ref)

        consts = _ml_consts()
        gates = g_ref[...]
        for h in range(HEADS):
            q = qk_ref[:, ML_DQK * h:ML_DQK * (h + 1)] * ML_SCALE
            k = qk_ref[:, 256 + ML_DQK * h:256 + ML_DQK * (h + 1)]
            v = v_ref[:, 128 * h:128 * (h + 1)]
            C, nrow, mprev = C_ref[h], n_ref[h], m_ref[h]
            cst_ref[0, h] = C
            nst_ref[0, h] = nrow
            mst_ref[0, h] = mprev
            r = _ml_chunk(q, k, v, gates, h, C, nrow, mprev, consts)
            h_ref[:, 128 * h:128 * (h + 1)] = r["num"] / r["nrm"]
            kw = k * r["ws"]
            C_ref[h] = r["wo"] * C + _dot(kw, v, TN)
            n_ref[h] = r["wo"] * nrow + _colsum(kw)
            m_ref[h] = r["m_new"]

    return pl.pallas_call(
        body, name="mlstm_fwd", grid=(N,),
        in_specs=[pl.BlockSpec((CHUNK, HALF), lambda n: (n, 0)),
                  pl.BlockSpec((CHUNK, HALF), lambda n: (n, C_MV // 4)),
                  pl.BlockSpec((CHUNK, 128), lambda n: (n, C_GATES))],
        out_specs=[pl.BlockSpec((CHUNK, HALF), lambda n: (n, 0)),
                   pl.BlockSpec((1, HEADS, ML_DQK, 128), lambda n: (n, 0, 0, 0)),
                   pl.BlockSpec((1, HEADS, 1, ML_DQK), lambda n: (n, 0, 0, 0)),
                   pl.BlockSpec((1, HEADS, 1, 1), lambda n: (n, 0, 0, 0))],
        out_shape=[_sds((T, HALF)), _sds((N, HEADS, ML_DQK, 128)), _sds((N, HEADS, 1, ML_DQK)),
                   _sds((N, HEADS, 1, 1))],
        scratch_shapes=[pltpu.VMEM((HEADS, ML_DQK, 128), F32), pltpu.VMEM((HEADS, 1, ML_DQK), F32),
                        pltpu.VMEM((HEADS, 1, 1), F32)],
        compiler_params=_cp("arbitrary"))(qkc, u, u)


def _head_norm(o, g):
    rs_parts, r_parts = [], []
    for h in range(HEADS):
        oh = o[:, 128 * h:128 * (h + 1)]
        rs = lax.rsqrt(jnp.mean(oh * oh, axis=-1, keepdims=True) + RMS_EPS)
        rs_parts.append(rs)
        r_parts.append(oh * rs)
    return jnp.concatenate(r_parts, axis=1), rs_parts


def _mix_in(o_hg, h_ml, hgate, mo, g_hg, g_ml):
    r_hg, _ = _head_norm(o_hg, g_hg)
    r_ml, _ = _head_norm(h_ml, g_ml)
    a = r_hg * g_hg * (hgate * _sig(hgate))
    b = r_ml * g_ml * _sig(mo)
    return jnp.concatenate([a, b], axis=1)


def _out_proj_ln(x, u, o_hg, h_ml, g_hg, g_ml, w_out, ln_g, ln_b, tm):
    T = x.shape[0]

    def body(x_ref, hgate_ref, mo_ref, ohg_ref, hml_ref, ghg_ref, gml_ref, w_ref, g_ref, b_ref,
             m_ref, z_ref, x1_ref):
        m = _mix_in(ohg_ref[...], hml_ref[...], hgate_ref[...], mo_ref[...], ghg_ref[...], gml_ref[...])
        m_ref[...] = m
        z = ALPHA * x_ref[...] + _dot(m, w_ref[...])
        z_ref[...] = z
        x1_ref[...] = _ln_fwd(z, g_ref[...], b_ref[...])[0]

    return pl.pallas_call(
        body, name="out_proj_ln1", grid=(T // tm,),
        in_specs=[_row(tm, D_MODEL), _row(tm, HALF, C_HGATE // 4), _row(tm, HALF, C_MO // 4),
                  _row(tm, HALF), _row(tm, HALF), _full(g_hg.shape), _full(g_ml.shape),
                  _full(w_out.shape), _full(ln_g.shape), _full(ln_b.shape)],
        out_specs=[_row(tm, D_MODEL)] * 3, out_shape=[_sds((T, D_MODEL))] * 3,
        compiler_params=_cp("parallel"))(x, u, u, o_hg, h_ml, g_hg, g_ml, w_out, ln_g, ln_b)


def _ffn_ln(x1, wg, wu, wd, ln_g, ln_b, tm):
    T = x1.shape[0]

    def body(x_ref, wg_ref, wu_ref, wd_ref, g_ref, b_ref, z_ref, x2_ref):
        x = x_ref[...]
        a = _dot(x, wg_ref[...])
        hh = a * _sig(a) * _dot(x, wu_ref[...])
        z = ALPHA * x + _dot(hh, wd_ref[...])
        z_ref[...] = z
        x2_ref[...] = _ln_fwd(z, g_ref[...], b_ref[...])[0]

    return pl.pallas_call(
        body, name="ffn_ln2", grid=(T // tm,),
        in_specs=[_row(tm, D_MODEL), _full(wg.shape), _full(wu.shape), _full(wd.shape),
                  _full(ln_g.shape), _full(ln_b.shape)],
        out_specs=[_row(tm, D_MODEL)] * 2, out_shape=[_sds((T, D_MODEL))] * 2,
        compiler_params=_cp("parallel"))(x1, wg, wu, wd, ln_g, ln_b)


def _ple_loss_ln2_bwd(x2, z2, p, tgt, wpg, bpg, wpp, ln_g, ln_b, tm):
    T = x2.shape[0]

    def body(x2_ref, z_ref, p_ref, t_ref, wpg_ref, bpg_ref, wpp_ref, g_ref, b_ref,
             de_ref, dgp_ref, dz_ref, loss_ref, dbpg_ref, dg_ref, db_ref):
        i = pl.program_id(0)

        @pl.when(i == 0)
        def _():
            for r in (loss_ref, dbpg_ref, dg_ref, db_ref):
                r[...] = jnp.zeros_like(r)

        x2 = x2_ref[...]
        gate = _sig(_dot(x2, wpg_ref[...]) + bpg_ref[...])
        e = _dot(p_ref[...], wpp_ref[...])
        err = x2 + gate * e - t_ref[...]
        loss_ref[...] += _colsum(err * err)
        dy = err * (1.0 / D_MODEL)
        de_ref[...] = dy * gate
        dgp = dy * e * gate * (1.0 - gate)
        dgp_ref[...] = dgp
        dbpg_ref[...] += _colsum(dgp)
        dx2 = dy + _dot(dgp, wpg_ref[...], NT)
        _, xhat, rstd = _ln_fwd(z_ref[...], g_ref[...], b_ref[...])
        dg_ref[...] += _colsum(dx2 * xhat)
        db_ref[...] += _colsum(dx2)
        dz_ref[...] = _ln_bwd(dx2, xhat, rstd, g_ref[...])

    vec = _full((1, D_MODEL))
    return pl.pallas_call(
        body, name="ple_loss_ln2_bwd", grid=(T // tm,),
        in_specs=[_row(tm, D_MODEL), _row(tm, D_MODEL), _row(tm, PLE_DIM), _row(tm, D_MODEL),
                  _full(wpg.shape), vec, _full(wpp.shape), vec, vec],
        out_specs=[_row(tm, D_MODEL)] * 3 + [vec] * 4,
        out_shape=[_sds((T, D_MODEL))] * 3 + [_sds((1, D_MODEL))] * 4,
        compiler_params=_cp("arbitrary"))(x2, z2, p, tgt, wpg, bpg, wpp, ln_g, ln_b)


def _ffn_bwd_ln1_bwd(x1, z1, dz2, wg, wu, wd, ln_g, ln_b, tm):
    T = x1.shape[0]

    def body(x_ref, z_ref, dz2_ref, wg_ref, wu_ref, wd_ref, g_ref, b_ref,
             h_ref, da_ref, dbb_ref, dz1_ref, dg_ref, db_ref):
        i = pl.program_id(0)

        @pl.when(i == 0)
        def _():
            dg_ref[...] = jnp.zeros_like(dg_ref)
            db_ref[...] = jnp.zeros_like(db_ref)

        x = x_ref[...]
        dz2 = dz2_ref[...]
        a = _dot(x, wg_ref[...])
        bb = _dot(x, wu_ref[...])
        sa = _sig(a)
        act = a * sa
        h_ref[...] = act * bb
        dh = _dot(dz2, wd_ref[...], NT)
        da = dh * bb * _dsilu(a, sa)
        dbb = dh * act
        da_ref[...] = da
        dbb_ref[...] = dbb
        dx1 = ALPHA * dz2 + _dot(da, wg_ref[...], NT) + _dot(dbb, wu_ref[...], NT)
        _, xhat, rstd = _ln_fwd(z_ref[...], g_ref[...], b_ref[...])
        dg_ref[...] += _colsum(dx1 * xhat)
        db_ref[...] += _colsum(dx1)
        dz1_ref[...] = _ln_bwd(dx1, xhat, rstd, g_ref[...])

    vec = _full((1, D_MODEL))
    return pl.pallas_call(
        body, name="ffn_bwd_ln1_bwd", grid=(T // tm,),
        in_specs=[_row(tm, D_MODEL)] * 3 + [_full(wg.shape), _full(wu.shape), _full(wd.shape), vec, vec],
        out_specs=[_row(tm, D_FF)] * 3 + [_row(tm, D_MODEL), vec, vec],
        out_shape=[_sds((T, D_FF))] * 3 + [_sds((T, D_MODEL)), _sds((1, D_MODEL)), _sds((1, D_MODEL))],
        compiler_params=_cp("arbitrary"))(x1, z1, dz2, wg, wu, wd, ln_g, ln_b)


def _out_proj_bwd(dz1, u, o_hg, h_ml, g_hg, g_ml, w_out, tm):
    T = dz1.shape[0]

    def body(dz_ref, hgate_ref, mo_ref, ohg_ref, hml_ref, ghg_ref, gml_ref, w_ref,
             dohg_ref, dhml_ref, dhgate_ref, dmo_ref, dghg_ref, dgml_ref):
        i = pl.program_id(0)

        @pl.when(i == 0)
        def _():
            dghg_ref[...] = jnp.zeros_like(dghg_ref)
            dgml_ref[...] = jnp.zeros_like(dgml_ref)

        dm = _dot(dz_ref[...], w_ref[...], NT)

        def half(dmh, o, gvec, gate_val, dgate_fac, do_ref, dgate_ref, dgvec_ref):
            r, rs = _head_norm(o, gvec)
            nrm = r * gvec
            dgate_ref[...] = dmh * nrm * dgate_fac
            dn = dmh * gate_val
            dgvec_ref[...] += _colsum(dn * r)
            dr = dn * gvec
            parts = []
            for h in range(HEADS):
                sl = slice(128 * h, 128 * (h + 1))
                parts.append(rs[h] * (dr[:, sl] - r[:, sl] * jnp.mean(dr[:, sl] * r[:, sl], axis=-1, keepdims=True)))
            do_ref[...] = jnp.concatenate(parts, axis=1)

        hg = hgate_ref[...]
        shg = _sig(hg)
        half(dm[:, :HALF], ohg_ref[...], ghg_ref[...], hg * shg, _dsilu(hg, shg), dohg_ref, dhgate_ref, dghg_ref)
        smo = _sig(mo_ref[...])
        half(dm[:, HALF:], hml_ref[...], gml_ref[...], smo, smo * (1.0 - smo), dhml_ref, dmo_ref, dgml_ref)

    vec = _full((1, HALF))
    return pl.pallas_call(
        body, name="out_proj_bwd", grid=(T // tm,),
        in_specs=[_row(tm, D_MODEL), _row(tm, HALF, C_HGATE // 4), _row(tm, HALF, C_MO // 4),
                  _row(tm, HALF), _row(tm, HALF), vec, vec, _full(w_out.shape)],
        out_specs=[_row(tm, HALF)] * 4 + [vec, vec],
        out_shape=[_sds((T, HALF))] * 4 + [_sds((1, HALF))] * 2,
        compiler_params=_cp("arbitrary"))(dz1, u, u, o_hg, h_ml, g_hg, g_ml, w_out)


def _hgrn2_bwd(u, lb_logits, do, states, comm=None):
    T = u.shape[0]
    N = T // CHUNK

    def body(hq_ref, hf_ref, hv_ref, lg_ref, do_ref, st_ref, dhq_ref, dhf_ref, dhv_ref, dlb_ref,
             dS_ref, b_sc, k_sc):
        i = pl.program_id(0)

        @pl.when(i == 0)
        def _():
            dS_ref[...] = jnp.zeros_like(dS_ref)
            dlb_ref[...] = jnp.zeros_like(dlb_ref)

        lb_all = _sig(lg_ref[0:1, :] - lg_ref[1:2, :])
        tril = _tri(CHUNK, True)
        triu = _tri(CHUNK, False)
        ones = jnp.ones((128, 128), F32)
        rowi = _iota((SUB, 128), 0)
        row64 = _iota((CHUNK, 128), 0)
        lane = _iota((SUB, 128), 1)
        for h in range(HEADS):
            sl = slice(128 * h, 128 * h + 128)
            lb = lb_all[:, sl]
            hq = hq_ref[:, sl]
            q, g, k, f, sg, nsg, sq = _hg_gates(hq, hf_ref[:, sl], lb)
            v = hv_ref[:, sl]
            do_h = do_ref[:, sl]
            b = _dotx(tril, g)
            b_sc[...] = b
            k_sc[...] = k
            ST = st_ref[0, h]
            dST = dS_ref[h]
            bl = b_sc[CHUNK - 1:CHUNK, :]
            eb = jnp.exp(b)
            ebl = jnp.exp(bl - b)
            qt = q * eb
            kl = k * ebl
            dqt = _dot(do_h, ST)
            dkl = _dot(v, dST)
            dv_acc = _dot(kl, dST, NT)
            dq_parts = []
            dk_in = jnp.zeros((CHUNK, 128), F32)
            for i_s in range(CHUNK // SUB):
                r0 = SUB * i_s
                qi = q[r0:r0 + SUB]
                doi = do_h[r0:r0 + SUB]
                dqi = jnp.zeros((SUB, 128), F32)
                if i_s > 0:
                    r = b_sc[r0 - 1:r0, :]
                    eq = jnp.exp(b[r0:r0 + SUB] - r)
                    ek = jnp.exp(r - b[:r0])
                    qe = qi * eq
                    ke = k[:r0] * ek
                    a_off = _dot(qe, ke, NT)
                    p_off = _dot(doi, v[:r0], NT)
                    pad = jnp.zeros((CHUNK - r0, 128), F32)
                    dv_acc = dv_acc + jnp.concatenate([_dot(a_off, doi, TN), pad], axis=0)
                    dqi = dqi + _dot(p_off, ke) * eq
                    dk_in = dk_in + jnp.concatenate([_dot(p_off, qe, TN) * ek, pad], axis=0)
                tiles = _hg_diag_tiles(b_sc, r0, rowi)
                ms = [qi * (k_sc[r0 + s:r0 + s + 1, :] * tiles[s]) for s in range(SUB)]
                ps = [doi * hv_ref[r0 + s:r0 + s + 1, sl] for s in range(SUB)]
                R = _dot(jnp.concatenate(ms + ps, axis=0), ones)
                dv_rows, dk_rows = [], []
                for s in range(SUB):
                    a_s = R[SUB * s:SUB * s + SUB]
                    p_s = R[SUB * (SUB + s):SUB * (SUB + s) + SUB]
                    pt = p_s * tiles[s]
                    dqi = dqi + pt * k_sc[r0 + s:r0 + s + 1, :]
                    dk_rows.append(_colsum(pt * qi))
                    dv_rows.append(_colsum(a_s * doi))
                pad_lo = jnp.zeros((r0, 128), F32)
                pad_hi = jnp.zeros((CHUNK - r0 - SUB, 128), F32)
                dk_in = dk_in + jnp.concatenate([pad_lo] * (r0 > 0) + dk_rows + [pad_hi] * (r0 + SUB < CHUNK), axis=0)
                dv_acc = dv_acc + jnp.concatenate([pad_lo] * (r0 > 0) + dv_rows + [pad_hi] * (r0 + SUB < CHUNK), axis=0)
                dq_parts.append(dqi)
            dq_in = jnp.concatenate(dq_parts, axis=0)
            db = qt * dqt + q * dq_in - k * dk_in - kl * dkl
            last = _colsum(kl * dkl) + jnp.exp(bl) * _colsum(ST * dST)
            db = db + jnp.where(row64 == CHUNK - 1, last, 0.0)
            dg = _dotx(triu, db)
            dq_tot = dqt * eb + dq_in
            dk_tot = dkl * ebl + dk_in
            common = dg / f - dk_tot
            dhf_ref[:, sl] = (1.0 - lb) * sg * nsg * common
            dl0 = _colsum(nsg * common) * lb * (1.0 - lb)
            dlb_ref[0:1, sl] += dl0
            dlb_ref[1:2, sl] -= dl0
            dhq_ref[:, sl] = dq_tot * _dsilu(hq, sq)
            dhv_ref[:, sl] = dv_acc
            dS_ref[h] = dST * jnp.exp(bl) + _dot(do_h, qt, TN)

    rev = lambda c: pl.BlockSpec((CHUNK, HALF), lambda i, c=c: (N - 1 - i, c // 4))
    rev0 = pl.BlockSpec((CHUNK, HALF), lambda i: (N - 1 - i, 0))
    return _hosted_call(
        body, comm, name="hgrn2_bwd", grid=(N,),
        in_specs=[rev(C_HQ), rev(C_HF), rev(C_HV), _full(lb_logits.shape), rev0,
                  pl.BlockSpec((1, HEADS, 128, 128), lambda i: (N - 1 - i, 0, 0, 0))],
        out_specs=[rev0, rev0, rev0, _full((2, HALF))],
        out_shape=[_sds((T, HALF))] * 3 + [_sds((2, HALF))],
        scratch_shapes=[pltpu.VMEM((HEADS, 128, 128), F32), pltpu.VMEM((CHUNK, 128), F32),
                        pltpu.VMEM((CHUNK, 128), F32)],
        args=(u, u, u, lb_logits, do, states))


def _mlstm_bwd(qkc, u, dh, cst, nst, mst):
    T = u.shape[0]
    N = T // CHUNK

    def body(qk_ref, v_ref, g_ref, dh_ref, cst_ref, nst_ref, mst_ref, dqk_ref, dv_ref, dgt_ref,
             dC_ref, dn_ref):
        i = pl.program_id(0)

        @pl.when(i == 0)
        def _():
            dC_ref[...] = jnp.zeros_like(dC_ref)
            dn_ref[...] = jnp.zeros_like(dn_ref)

        consts = _ml_consts()
        lane = consts[2]
        triu = _tri(CHUNK, False)
        ones = jnp.ones((CHUNK, 128), F32)
        rowc = _iota((CHUNK, 1), 0)
        gates = g_ref[...]
        dg_all = jnp.zeros((CHUNK, 128), F32)
        di_all = jnp.zeros((CHUNK, 128), F32)
        for h in range(HEADS):
            q = qk_ref[:, ML_DQK * h:ML_DQK * (h + 1)] * ML_SCALE
            k = qk_ref[:, 256 + ML_DQK * h:256 + ML_DQK * (h + 1)]
            v = v_ref[:, 128 * h:128 * (h + 1)]
            dh_h = dh_ref[:, 128 * h:128 * (h + 1)]
            C, nrow, mprev = cst_ref[0, h], nst_ref[0, h], mst_ref[0, h]
            dC, dn = dC_ref[h], dn_ref[h]
            r = _ml_chunk(q, k, v, gates, h, C, nrow, mprev, consts)
            wn, ws, wo, s_mat = r["wn"], r["ws"], r["wo"], r["s"]
            inv = 1.0 / r["nrm"]
            dnum = dh_h * inv
            hh = r["num"] * inv
            dnrm = -_rowsum(dh_h * hh) * inv
            dden = jnp.where(jnp.abs(r["den"]) > r["floor"], dnrm * jnp.sign(r["den"]), 0.0)
            ds = _dot(dnum, v, NT) + dden
            dqk = ds * r["wi"]
            dd = ds * s_mat
            dq = _dot(dqk, k) + wn * (_dot(dnum, C, NT) + dden * nrow)
            dk_st = ws * (_dot(v, dC, NT) + dn)
            dk = _dot(dqk, q, TN) + dk_st
            dv_ref[:, 128 * h:128 * (h + 1)] = _dot(s_mat, dnum, TN) + ws * _dot(k, dC)
            dqk_ref[:, ML_DQK * h:ML_DQK * (h + 1)] = dq * ML_SCALE
            dqk_ref[:, 256 + ML_DQK * h:256 + ML_DQK * (h + 1)] = dk
            dC_ref[h] = wo * dC + _dot(q * wn, dnum, TN)
            dn_ref[h] = wo * dn + _colsum(q * (wn * dden))
            e_col = wn * (_rowsum(dnum * r["qc"]) + dden * r["qn"])
            c_col = _rowsum(k * dk_st)
            z = wo * (jnp.sum(dC * C, keepdims=True) + jnp.sum(dn * nrow, keepdims=True))
            dd_cols = _dotx(dd, ones, TN)[:, 0:1]
            dg_col = _rowsum(dd) - dd_cols + e_col - c_col
            dg_col = dg_col + jnp.where(rowc == CHUNK - 1, jnp.sum(c_col, keepdims=True) + z, 0.0)
            di_col = dd_cols + c_col
            dg_all = dg_all + jnp.where(lane == 4 + h, dg_col, 0.0)
            di_all = di_all + jnp.where(lane == h, di_col, 0.0)
        dlf = _dotx(triu, dg_all)
        dgt_ref[...] = di_all + dlf * _sig(-gates)

    rev0 = pl.BlockSpec((CHUNK, HALF), lambda i: (N - 1 - i, 0))
    st = lambda a, b: pl.BlockSpec((1, HEADS, a, b), lambda i: (N - 1 - i, 0, 0, 0))
    return pl.pallas_call(
        body, name="mlstm_bwd", grid=(N,),
        in_specs=[rev0, pl.BlockSpec((CHUNK, HALF), lambda i: (N - 1 - i, C_MV // 4)),
                  pl.BlockSpec((CHUNK, 128), lambda i: (N - 1 - i, C_GATES)), rev0,
                  st(ML_DQK, 128), st(1, ML_DQK), st(1, 1)],
        out_specs=[rev0, rev0, pl.BlockSpec((CHUNK, 128), lambda i: (N - 1 - i, 0))],
        out_shape=[_sds((T, HALF)), _sds((T, HALF)), _sds((T, 128))],
        scratch_shapes=[pltpu.VMEM((HEADS, ML_DQK, 128), F32), pltpu.VMEM((HEADS, 1, ML_DQK), F32)],
        compiler_params=_cp("arbitrary"))(qkc, u, u, dh, cst, nst, mst)


def _conv_bwd(u, pre, dqkc, cw, tm):
    T = u.shape[0]
    hb = tm // 8
    nb = T // 8

    def body(x_ref, xh_ref, pre_ref, preh_ref, d_ref, dh_ref, w_ref, dx_ref, dw_ref, db_ref):
        i = pl.program_id(0)

        @pl.when(i == 0)
        def _():
            dw_ref[...] = jnp.zeros_like(dw_ref)
            db_ref[...] = jnp.zeros_like(db_ref)

        def dpre_of(pre, d):
            s = _sig(pre)
            return d * _dsilu(pre, s)

        rowi = _iota((8, HALF), 0)
        dpre = dpre_of(pre_ref[...], d_ref[...])
        dpre_next = jnp.where(i < pl.num_programs(0) - 1, dpre_of(preh_ref[...], dh_ref[...]), 0.0)
        x = x_ref[...]
        xprev = jnp.where(i > 0, xh_ref[...], 0.0)
        dx = dpre * w_ref[3:4, :]
        db_ref[...] += _colsum(dpre)
        dws = [None] * 4
        dws[3] = _colsum(dpre * x)
        for j in (1, 2, 3):
            dx = dx + _shift_rows_up(dpre, dpre_next, j, rowi) * w_ref[3 - j:4 - j, :]
            dws[3 - j] = _colsum(dpre * _shift_rows(x, xprev, j, rowi))
        dx_ref[...] = dx
        dw_ref[...] += jnp.concatenate(dws, axis=0)

    cur = lambda blk: pl.BlockSpec((tm, HALF), lambda i, blk=blk: (i, blk))
    nxt = pl.BlockSpec((8, HALF), lambda i: (jnp.minimum((i + 1) * hb, nb - 1), 0))
    return pl.pallas_call(
        body, name="conv_bwd", grid=(T // tm,),
        in_specs=[cur(C_MQK // 4), pl.BlockSpec((8, HALF), lambda i: (jnp.maximum(i * hb - 1, 0), C_MQK // 4)),
                  cur(0), nxt, cur(0), nxt, _full(cw.shape)],
        out_specs=[cur(0), _full((4, HALF)), _full((1, HALF))],
        out_shape=[_sds((T, HALF)), _sds((4, HALF)), _sds((1, HALF))],
        compiler_params=_cp("arbitrary"))(u, u, pre, pre, dqkc, dqkc, cw)


def _in_proj_bwd(dz1, du, w, tm):
    T = dz1.shape[0]

    def body(dz_ref, du_ref, w_ref, dx_ref, db_ref):
        i = pl.program_id(0)

        @pl.when(i == 0)
        def _():
            db_ref[...] = jnp.zeros_like(db_ref)

        du_t = du_ref[...]
        db_ref[...] += _colsum(du_t)
        dx_ref[...] = ALPHA * dz_ref[...] + _dot(du_t, w_ref[...], NT)

    return pl.pallas_call(
        body, name="in_proj_bwd", grid=(T // tm,),
        in_specs=[_row(tm, D_MODEL), _row(tm, PROJ_WP), _full(w.shape)],
        out_specs=[_row(tm, D_MODEL), _full((1, PROJ_WP))],
        out_shape=[_sds((T, D_MODEL)), _sds((1, PROJ_WP))],
        compiler_params=_cp("arbitrary"))(dz1, du, w)


def _wgrad(a, b, name, tm, tn, tk, out_dtype=F32):
    T, M = a.shape
    N = b.shape[1]
    tm, tn, tk = min(tm, M), min(tn, N), min(tk, T)
    nk = T // tk

    def body(a_ref, b_ref, o_ref, acc_ref):
        kk = pl.program_id(2)

        @pl.when(kk == 0)
        def _():
            acc_ref[...] = jnp.zeros_like(acc_ref)

        acc_ref[...] += _dot(a_ref[...], b_ref[...], TN)

        @pl.when(kk == nk - 1)
        def _():
            o_ref[...] = acc_ref[...].astype(out_dtype)

    return pl.pallas_call(
        body, name=name, grid=(M // tm, N // tn, nk),
        in_specs=[pl.BlockSpec((tk, tm), lambda i, j, kk: (kk, i)), pl.BlockSpec((tk, tn), lambda i, j, kk: (kk, j))],
        out_specs=pl.BlockSpec((tm, tn), lambda i, j, kk: (i, j)), out_shape=jax.ShapeDtypeStruct((M, N), out_dtype),
        scratch_shapes=[pltpu.VMEM((tm, tn), F32)],
        compiler_params=_cp("parallel", "parallel", "arbitrary"))(a, b)


GRAD_T = jnp.bfloat16
W_IN_S, FF_S, OUT_S, PP_S = PROJ_W // N_DEV, D_FF // N_DEV, D_MODEL // N_DEV, D_MODEL // N_DEV
LATE = ("w_ffn_gate", "w_ffn_up", "w_out", "w_ffn_down", "ple_w_gate", "ple_w_proj")


def _split_cols(a, n):
    return a.reshape(a.shape[0], N_DEV, n).transpose(1, 0, 2)


def _join_cols(a):
    return a.transpose(1, 0, 2).reshape(a.shape[1], -1)


def _late_full(parts):
    gate, up, w_out, down, pg, pp = parts
    return dict(w_ffn_gate=_join_cols(gate), w_ffn_up=_join_cols(up), w_out=w_out.reshape(D_MODEL, D_MODEL),
                w_ffn_down=down.reshape(D_FF, D_MODEL), ple_w_gate=pg.reshape(D_MODEL, D_MODEL), ple_w_proj=_join_cols(pp))


def _by_owner(g):
    return [_split_cols(g["w_ffn_gate"], FF_S), _split_cols(g["w_ffn_up"], FF_S),
            g["w_out"].reshape(N_DEV, OUT_S, D_MODEL), g["w_ffn_down"].reshape(N_DEV, FF_S, D_MODEL),
            g["ple_w_gate"].reshape(N_DEV, OUT_S, D_MODEL), _split_cols(g["ple_w_proj"], PP_S)]


def _step(x, p, tgt, w_in, b_in, lb_logits, conv_w, conv_b, g_hg, g_ml, ln1_g, ln1_b, ln2_g, ln2_b, bpg, late,
          distributed):
    T = x.shape[0]
    tm = min(256, T)
    u = _in_proj(x, w_in, b_in, tm)
    pre, qkc = _conv_fwd(u, conv_w, conv_b, tm)
    (o_hg, hg_states), got = _hgrn2_fwd(u, lb_logits, _Comm("gather", late) if distributed else None)
    lw = _late_full(got) if distributed else late
    w_out, wg, wu, wd, wpg, wpp = (lw[n] for n in ("w_out", "w_ffn_gate", "w_ffn_up", "w_ffn_down", "ple_w_gate", "ple_w_proj"))
    h_ml, cst, nst, mst = _mlstm_fwd(qkc, u)
    m_in, z1, x1 = _out_proj_ln(x, u, o_hg, h_ml, g_hg, g_ml, w_out, ln1_g, ln1_b, tm)
    z2, x2 = _ffn_ln(x1, wg, wu, wd, ln2_g, ln2_b, tm)
    de, dgp, dz2, loss_vec, d_bpg, d_ln2g, d_ln2b = _ple_loss_ln2_bwd(x2, z2, p, tgt, wpg, bpg, wpp, ln2_g, ln2_b, tm)
    hh, da, dbb, dz1, d_ln1g, d_ln1b = _ffn_bwd_ln1_bwd(x1, z1, dz2, wg, wu, wd, ln1_g, ln1_b, tm)
    wk = dict(tm=512, tn=D_MODEL, tk=512, out_dtype=GRAD_T)
    big = dict(
        w_ffn_gate=_wgrad(x1, da, "wgrad_ffn_gate", 512, D_FF, 512, GRAD_T),
        w_ffn_up=_wgrad(x1, dbb, "wgrad_ffn_up", 512, D_FF, 512, GRAD_T),
        w_out=_wgrad(m_in, dz1, "wgrad_w_out", **wk),
        w_ffn_down=_wgrad(hh, dz2, "wgrad_ffn_down", D_FF, D_MODEL, 512, GRAD_T),
        ple_w_gate=_wgrad(x2, dgp, "wgrad_ple_gate", **wk),
        ple_w_proj=_wgrad(p, de, "wgrad_ple_proj", **wk))
    d_ohg, d_hml, d_hgate, d_mo, d_ghg, d_gml = _out_proj_bwd(dz1, u, o_hg, h_ml, g_hg, g_ml, w_out, tm)
    (d_hq, d_hf, d_hv, d_lb), received = _hgrn2_bwd(
        u, lb_logits, d_ohg, hg_states, _Comm("scatter", _by_owner(big)) if distributed else None)
    if distributed:
        big = dict(zip(LATE, received))
    d_qkc, d_mv, d_gates = _mlstm_bwd(qkc, u, d_hml, cst, nst, mst)
    d_mqk, d_convw, d_convb = _conv_bwd(u, pre, d_qkc, conv_w, tm)
    du = jnp.concatenate([d_hq, d_hf, d_hv, d_hgate, d_mqk, d_mv, d_mo, d_gates], axis=1)
    dx, d_bin = _in_proj_bwd(dz1, du, w_in, tm)
    big["w_in"] = _wgrad(x, du, "wgrad_w_in", 512, PROJ_WP, 512, GRAD_T)
    small = dict(b_in=d_bin, hg_lb_logits=d_lb, ml_conv_w=d_convw, ml_conv_b=d_convb, hg_norm_g=d_ghg, ml_norm_g=d_gml,
                 ln1_g=d_ln1g, ln1_b=d_ln1b, ln2_g=d_ln2g, ln2_b=d_ln2b, ple_b_gate=d_bpg)
    return loss_vec, dx, big, small


BIG = ("w_in",) + LATE
SMALL = dict(b_in=(8, 32, PROJ_WP), hg_lb_logits=(40, 8, 1024), ml_conv_w=(48, 16, 2048), ml_conv_b=(64, 8, 512),
             hg_norm_g=(72, 8, 512), ml_norm_g=(80, 8, 512), ln1_g=(88, 8, 1024), ln1_b=(96, 8, 1024),
             ln2_g=(104, 8, 1024), ln2_b=(112, 8, 1024), ple_b_gate=(120, 8, 1024))
SM_ROWS = 128


def _padc(a, n):
    return jnp.pad(a, [(0, 0)] * (a.ndim - 1) + [(0, n - a.shape[-1])])


def _pack_small(d, loss_vec=None):
    first = jnp.zeros((8, 128), F32) if loss_vec is None else loss_vec.reshape(8, 128)
    parts = [first]
    for name, (_, rows, n) in SMALL.items():
        parts.append(jnp.pad(d[name].reshape(-1), (0, rows * 128 - n)).reshape(rows, 128))
    return jnp.concatenate(parts, axis=0)


def _unpack_small(slab, shapes):
    return {name: slab[r0:r0 + rows].reshape(-1)[:n].reshape(shapes[name]) for name, (r0, rows, n) in SMALL.items()}


def _gather_two_level(blocks, name):
    n = len(blocks)

    def body(*refs):
        x_refs, out_refs = refs[:n], refs[n:2 * n]
        send_sems, recv_sems, local_sems = refs[2 * n:]
        x, y, c = lax.axis_index("x"), lax.axis_index("y"), lax.axis_index("c")
        me, sibling = (x, y, c), (x, y, 1 - c)
        chips = [(1 - x, y), (x, 1 - y), (1 - x, 1 - y)]

        def copy(i, k, block, to, own=False):
            slab = out_refs[i].at[4 * block[0] + 2 * block[1] + block[2]]
            return pltpu.make_async_remote_copy(
                src_ref=x_refs[i] if own else slab, dst_ref=slab, send_sem=send_sems.at[7 * i + k],
                recv_sem=recv_sems.at[7 * i + k], device_id=to, device_id_type=MESH)

        mine = [pltpu.make_async_copy(x_refs[i], out_refs[i].at[4 * x + 2 * y + c], local_sems.at[i]) for i in range(n)]
        for cp in mine:
            cp.start()
        first = [copy(i, 0, me, sibling, own=True) for i in range(n)]
        first += [copy(i, 1 + j, me, (*chip, c), own=True) for j, chip in enumerate(chips) for i in range(n)]
        for cp in first:
            cp.start()
        passed = []
        for j, chip in enumerate(chips):
            for i in range(n):
                copy(i, 1 + j, (*chip, c), me).wait_recv()
                passed.append(copy(i, 4 + j, (*chip, c), sibling))
                passed[-1].start()
        for i in range(n):
            copy(i, 0, sibling, me).wait_recv()
            for j, chip in enumerate(chips):
                copy(i, 4 + j, (*chip, 1 - c), me).wait_recv()
        for cp in first + passed:
            cp.wait_send()
        for cp in mine:
            cp.wait()

    return pl.pallas_call(
        body, name=name, out_shape=[jax.ShapeDtypeStruct((N_DEV,) + b.shape, b.dtype) for b in blocks],
        in_specs=[ANY] * n, out_specs=[ANY] * n,
        scratch_shapes=[pltpu.SemaphoreType.DMA((7 * n,)), pltpu.SemaphoreType.DMA((7 * n,)),
                        pltpu.SemaphoreType.DMA((n,))])(*blocks)


def _adamw(w, g, m, v):
    m = B1 * m + (1.0 - B1) * g
    v = B2 * v + (1.0 - B2) * jnp.square(g)
    m_hat = m / (1.0 - B1 ** STEP)
    v_hat = v / (1.0 - B2 ** STEP)
    return -LR * (m_hat / (jnp.sqrt(v_hat) + EPS) + WD * w), m, v


def _sum_slabs(ref):
    g = ref[0].astype(F32)
    for j in range(1, N_DEV):
        g = g + ref[j].astype(F32)
    return g


def _adamw_matrix(rb, w, m, v, name):
    R, C = w.shape
    tr = 256 if R % 256 == 0 else R

    def body(rb_ref, w_ref, m_ref, v_ref, g_ref, d_ref, m2_ref, v2_ref):
        g = _sum_slabs(rb_ref)
        g_ref[...] = g
        d_ref[...], m2_ref[...], v2_ref[...] = _adamw(w_ref[...], g, m_ref[...], v_ref[...])

    blk = pl.BlockSpec((tr, C), lambda i: (i, 0))
    return pl.pallas_call(
        body, name=name, grid=(R // tr,),
        in_specs=[pl.BlockSpec((N_DEV, tr, C), lambda i: (0, i, 0)), blk, blk, blk],
        out_specs=[blk] * 4, out_shape=[_sds((R, C))] * 4, compiler_params=_cp("parallel"))(rb, w, m, v)


def _adamw_small(sg, w, m, v):
    def body(sg_ref, w_ref, m_ref, v_ref, loss_ref, g_ref, d_ref, m2_ref, v2_ref):
        g = _sum_slabs(sg_ref)
        loss_ref[...] = (0.5 / D_MODEL) * jnp.sum(g[0:8], keepdims=True)
        g_ref[...] = g
        d_ref[...], m2_ref[...], v2_ref[...] = _adamw(w_ref[...], g, m_ref[...], v_ref[...])

    return pl.pallas_call(
        body, name="adamw_small", out_shape=[_sds((1, 1))] + [_sds((SM_ROWS, 128))] * 4)(sg, w, m, v)


WEIGHTS = ("w_in", "b_in", "hg_lb_logits", "ml_conv_w", "ml_conv_b", "hg_norm_g", "ml_norm_g", "w_out", "ln1_g", "ln1_b",
           "w_ffn_gate", "w_ffn_up", "w_ffn_down", "ln2_g", "ln2_b", "ple_w_proj", "ple_w_gate", "ple_b_gate")
CONV_S = HALF // N_DEV


def kernel(x, p, w_in, b_in, hg_lb_logits, ml_conv_w, ml_conv_b, hg_norm_g, ml_norm_g, w_out, ln1_g, ln1_b, w_ffn_gate, w_ffn_up, w_ffn_down, ln2_g, ln2_b, ple_w_proj, ple_w_gate, ple_b_gate, loss_target, m_w_in, m_b_in, m_hg_lb_logits, m_ml_conv_w, m_ml_conv_b, m_hg_norm_g, m_ml_norm_g, m_w_out, m_ln1_g, m_ln1_b, m_w_ffn_gate, m_w_ffn_up, m_w_ffn_down, m_ln2_g, m_ln2_b, m_ple_w_proj, m_ple_w_gate, m_ple_b_gate, v_w_in, v_b_in, v_hg_lb_logits, v_ml_conv_w, v_ml_conv_b, v_hg_norm_g, v_ml_norm_g, v_w_out, v_ln1_g, v_ln1_b, v_w_ffn_gate, v_w_ffn_up, v_w_ffn_down, v_ln2_g, v_ln2_b, v_ple_w_proj, v_ple_w_gate, v_ple_b_gate):
    args = locals()
    me = 4 * lax.axis_index("x") + 2 * lax.axis_index("y") + lax.axis_index("c")
    shapes = {n: args[n].shape for n in WEIGHTS}
    drop = lambda n, a: a[0] if n in BIG or n == "ml_conv_w" else a
    W = {n: drop(n, args[n]) for n in WEIGHTS}
    M = {n: drop(n, args["m_" + n]) for n in WEIGHTS}
    V = {n: drop(n, args["v_" + n]) for n in WEIGHTS}

    g_in, g_conv = _gather_two_level(
        [W["w_in"].astype(_MXU), jnp.pad(W["ml_conv_w"], ((0, 4), (0, 128 - CONV_S)))], "gather_w_in")
    w_in_full = _padc(_join_cols(g_in), PROJ_WP)
    conv_full = _join_cols(g_conv[:, :4, :CONV_S])

    loss_vec, dx, big, small = _step(
        x[0], p[0, 0], loss_target[0], w_in_full, _padc(b_in, PROJ_WP), hg_lb_logits, conv_full, ml_conv_b,
        hg_norm_g, ml_norm_g, ln1_g, ln1_b, ln2_g, ln2_b, ple_b_gate, [W[n].astype(_MXU) for n in LATE], True)

    (rb_in,), (sg,) = _comm_call([_Comm("scatter", [_split_cols(big["w_in"][:, :PROJ_W], W_IN_S)]),
                                  _Comm("gather", [_pack_small(small, loss_vec)])], "exchange_w_in_small")
    big["w_in"] = rb_in

    upd = {n: _adamw_matrix(big[n], W[n], M[n], V[n], "adamw_" + n) for n in BIG}
    place = lambda d: {**d, "b_in": _padc(d["b_in"], PROJ_WP),
                       "ml_conv_w": lax.dynamic_update_slice(jnp.zeros((4, HALF), F32), d["ml_conv_w"], (0, me * CONV_S))}
    loss, *small_upd = _adamw_small(sg, _pack_small(place(W)), _pack_small(place(M)), _pack_small(place(V)))

    outs = []
    sm_shapes = {**{n: shapes[n] for n in SMALL}, "b_in": (1, PROJ_WP), "ml_conv_w": (4, HALF)}
    for kind in range(4):
        smalls = _unpack_small(small_upd[kind], sm_shapes)
        smalls["b_in"] = smalls["b_in"][:, :PROJ_W]
        smalls["ml_conv_w"] = lax.dynamic_slice(smalls["ml_conv_w"], (0, me * CONV_S), (4, CONV_S))
        for n in WEIGHTS:
            outs.append((upd[n][kind] if n in BIG else smalls[n]).reshape(shapes[n]))
    return (loss.reshape(()), dx.reshape(x.shape), *outs)
```

```python
import functools
import math

import jax
import jax.numpy as jnp
from jax import lax
from jax.experimental import pallas as pl
from jax.experimental.pallas import tpu as pltpu

F32 = jnp.float32
_MXU = jnp.bfloat16

D_MODEL = 1024
CHUNK = 64
SUB = 16
PLE_DIM = 256
HEADS = 4
HG_DK = 128
ML_DQK = 64
HALF = 512
D_FF = 2816
PROJ_W = 3592
PROJ_WP = 3712
ALPHA = float(2 ** 0.25)
LN_EPS = 1e-5
RMS_EPS = 1e-6
ML_SCALE = ML_DQK ** -0.5
N_DEV = 8
LR, B1, B2, EPS, WD, STEP = 0.001, 0.9, 0.999, 1e-08, 0.01, 10
NEG = -1e30

C_HQ, C_HF, C_HV, C_HGATE, C_MQK, C_MV, C_MO, C_GATES = 0, 4, 8, 12, 16, 20, 24, 28

VMEM_LIMIT = 52 * 1024 * 1024

NN = (((1,), (0,)), ((), ()))
NT = (((1,), (1,)), ((), ()))
TN = (((0,), (0,)), ((), ()))


def _dot(a, b, dims=NN):
    return lax.dot_general(a.astype(_MXU), b.astype(_MXU), dims, preferred_element_type=F32)


def _dotx(a, b, dims=NN):
    return lax.dot_general(a, b, dims, precision=lax.Precision.HIGHEST, preferred_element_type=F32)


def _sig(x):
    return jax.nn.sigmoid(x)


def _cp(*sem):
    return pltpu.CompilerParams(dimension_semantics=sem, vmem_limit_bytes=VMEM_LIMIT)


def _row(tm, c, blk=0):
    return pl.BlockSpec((tm, c), lambda i, blk=blk: (i, blk))


def _full(shape):
    nd = len(shape)
    return pl.BlockSpec(tuple(shape), lambda i, nd=nd: (0,) * nd)


def _sds(shape):
    return jax.ShapeDtypeStruct(tuple(shape), F32)


def _iota(shape, axis):
    return lax.broadcasted_iota(jnp.int32, shape, axis)


def _colsum(x):
    return jnp.sum(x, axis=0, keepdims=True)


def _rowsum(x):
    return jnp.sum(x, axis=1, keepdims=True)


def _ln_fwd(z, g, b):
    mu = jnp.mean(z, axis=-1, keepdims=True)
    zc = z - mu
    var = jnp.mean(zc * zc, axis=-1, keepdims=True)
    rstd = lax.rsqrt(var + LN_EPS)
    xhat = zc * rstd
    return xhat * g + b, xhat, rstd


def _ln_bwd(dy, xhat, rstd, g):
    dxh = dy * g
    m1 = jnp.mean(dxh, axis=-1, keepdims=True)
    m2 = jnp.mean(dxh * xhat, axis=-1, keepdims=True)
    return rstd * (dxh - m1 - xhat * m2)


def _dsilu(x, s):
    return s * (1.0 + x * (1.0 - s))


MESH = pl.DeviceIdType.MESH
ANY = pl.BlockSpec(memory_space=pl.ANY)


def _flip(v, bit):
    return 1 - v if bit else v


class _Comm:
    def __init__(self, kind, srcs):
        self.kind, self.srcs, self.n = kind, list(srcs), len(srcs)

    def out_shape(self):
        lead = (N_DEV,) if self.kind == "gather" else ()
        return [jax.ShapeDtypeStruct(lead + s.shape, s.dtype) for s in self.srcs]

    def scratch(self):
        return [pltpu.SemaphoreType.DMA((7 * self.n,)), pltpu.SemaphoreType.DMA((7 * self.n,)),
                pltpu.SemaphoreType.DMA((self.n,))]

    def copies(self, srcs, dsts, send_sems, recv_sems, local_sems):
        x, y, c = lax.axis_index("x"), lax.axis_index("y"), lax.axis_index("c")
        me = 4 * x + 2 * y + c
        pick = (lambda s, j: s) if self.kind == "gather" else (lambda s, j: s.at[j])
        out = []
        for i, (s, d) in enumerate(zip(srcs, dsts)):
            out.append(pltpu.make_async_copy(pick(s, me), d.at[me], local_sems.at[i]))
            for k in range(1, N_DEV):
                px, py, pc = _flip(x, k & 4), _flip(y, k & 2), _flip(c, k & 1)
                out.append(pltpu.make_async_remote_copy(
                    src_ref=pick(s, 4 * px + 2 * py + pc), dst_ref=d.at[me], send_sem=send_sems.at[7 * i + k - 1],
                    recv_sem=recv_sems.at[7 * i + k - 1], device_id=(px, py, pc), device_id_type=MESH))
        return out


def _comm_call(comms, name):
    counts = [cm.n for cm in comms]

    def body(*refs):
        total = sum(counts)
        srcs, dsts, sems = refs[:total], refs[total:2 * total], refs[2 * total:]
        copies, o = [], 0
        for j, cm in enumerate(comms):
            copies += cm.copies(srcs[o:o + cm.n], dsts[o:o + cm.n], *sems[3 * j:3 * j + 3])
            o += cm.n
        for cp in copies:
            cp.start()
        for cp in copies:
            cp.wait()

    total = sum(counts)
    res = pl.pallas_call(
        body, name=name, in_specs=[ANY] * total, out_specs=[ANY] * total,
        out_shape=[s for cm in comms for s in cm.out_shape()],
        scratch_shapes=[s for cm in comms for s in cm.scratch()])(*[a for cm in comms for a in cm.srcs])
    out, o = [], 0
    for cm in comms:
        out.append(res[o:o + cm.n])
        o += cm.n
    return out


def _hosted_call(body, comm, *, name, grid, in_specs, out_specs, out_shape, scratch_shapes, args):
    if comm is None:
        res = pl.pallas_call(body, name=name, grid=grid, in_specs=in_specs, out_specs=out_specs, out_shape=out_shape,
                             scratch_shapes=scratch_shapes, compiler_params=_cp("arbitrary"))(*args)
        return list(res), []
    n_in, n_out, n_sc, nc = len(in_specs), len(out_specs), len(scratch_shapes), comm.n
    last = grid[0] - 1

    def hosted(*refs):
        ins, csrc = refs[:n_in], refs[n_in:n_in + nc]
        o0 = n_in + nc
        outs, cdst = refs[o0:o0 + n_out], refs[o0 + n_out:o0 + n_out + nc]
        s0 = o0 + n_out + nc
        scr, sems = refs[s0:s0 + n_sc], refs[s0 + n_sc:]
        i = pl.program_id(0)

        @pl.when(i == 0)
        def _():
            for cp in comm.copies(csrc, cdst, *sems):
                cp.start()

        body(*ins, *outs, *scr)

        @pl.when(i == last)
        def _():
            for cp in comm.copies(csrc, cdst, *sems):
                cp.wait()

    res = pl.pallas_call(
        hosted, name=name, grid=grid, in_specs=list(in_specs) + [ANY] * nc, out_specs=list(out_specs) + [ANY] * nc,
        out_shape=list(out_shape) + comm.out_shape(), scratch_shapes=list(scratch_shapes) + comm.scratch(),
        compiler_params=_cp("arbitrary"))(*args, *comm.srcs)
    return list(res[:n_out]), list(res[n_out:])


def _in_proj(x, w, b, tm):
    T = x.shape[0]

    def body(x_ref, w_ref, b_ref, o_ref):
        o_ref[...] = _dot(x_ref[...], w_ref[...]) + b_ref[...]

    return pl.pallas_call(
        body, name="in_proj", grid=(T // tm,),
        in_specs=[_row(tm, D_MODEL), _full(w.shape), _full(b.shape)],
        out_specs=_row(tm, PROJ_WP), out_shape=_sds((T, PROJ_WP)),
        compiler_params=_cp("parallel"))(x, w, b)


def _shift_rows(x, halo, j, rowi):
    r = pltpu.roll(x, j, 0)
    top = jnp.where(rowi < j, pltpu.roll(halo, j, 0), r[:8])
    return jnp.concatenate([top, r[8:]], axis=0)


def _shift_rows_up(x, halo, j, rowi):
    n = x.shape[0]
    r = pltpu.roll(x, n - j, 0)
    bot = jnp.where(rowi >= 8 - j, pltpu.roll(halo, 8 - j, 0), r[n - 8:])
    return jnp.concatenate([r[:n - 8], bot], axis=0)


def _conv_fwd(u, cw, cb, tm):
    T = u.shape[0]
    hb = tm // 8

    def body(x_ref, halo_ref, w_ref, b_ref, pre_ref, out_ref):
        i = pl.program_id(0)
        x = x_ref[...]
        halo = jnp.where(i > 0, halo_ref[...], 0.0)
        rowi = _iota((8, HALF), 0)
        acc = x * w_ref[3:4, :] + b_ref[...]
        for j in (1, 2, 3):
            acc = acc + _shift_rows(x, halo, j, rowi) * w_ref[3 - j:4 - j, :]
        pre_ref[...] = acc
        out_ref[...] = acc * _sig(acc)

    return pl.pallas_call(
        body, name="conv_fwd", grid=(T // tm,),
        in_specs=[pl.BlockSpec((tm, HALF), lambda i: (i, C_MQK // 4)),
                  pl.BlockSpec((8, HALF), lambda i: (jnp.maximum(i * hb - 1, 0), C_MQK // 4)),
                  _full(cw.shape), _full(cb.shape)],
        out_specs=[_row(tm, HALF), _row(tm, HALF)], out_shape=[_sds((T, HALF)), _sds((T, HALF))],
        compiler_params=_cp("parallel"))(u, u, cw, cb)


def _hg_gates(hq, hf, lb):
    sg = _sig(hf)
    nsg = _sig(-hf)
    f = lb + (1.0 - lb) * sg
    g = jnp.log(f)
    k = (1.0 - lb) * nsg
    sq = _sig(hq)
    return hq * sq, g, k, f, sg, nsg, sq


def _tri(n, lower):
    r, c = _iota((n, n), 0), _iota((n, n), 1)
    return jnp.where((r >= c) if lower else (c >= r), 1.0, 0.0).astype(F32)


def _hg_diag_tiles(b_sc, r0, rowi):
    bi = b_sc[r0:r0 + SUB, :]
    return [jnp.exp(jnp.where(rowi >= s, bi - b_sc[r0 + s:r0 + s + 1, :], NEG)) for s in range(SUB)]


def _hgrn2_fwd(u, lb_logits, comm=None):
    T = u.shape[0]
    N = T // CHUNK

    def body(hq_ref, hf_ref, hv_ref, lg_ref, o_ref, st_ref, S_ref, b_sc, k_sc):
        n = pl.program_id(0)

        @pl.when(n == 0)
        def _():
            S_ref[...] = jnp.zeros_like(S_ref)

        lb_all = _sig(lg_ref[0:1, :] - lg_ref[1:2, :])
        tril = _tri(CHUNK, True)
        ones = jnp.ones((128, 128), F32)
        rowi = _iota((SUB, 128), 0)
        for h in range(HEADS):
            sl = slice(128 * h, 128 * h + 128)
            q, g, k, _, _, _, _ = _hg_gates(hq_ref[:, sl], hf_ref[:, sl], lb_all[:, sl])
            b = _dotx(tril, g)
            b_sc[...] = b
            k_sc[...] = k
            ST = S_ref[h]
            st_ref[0, h] = ST
            o = _dot(q * jnp.exp(b), ST, NT)
            outs = []
            for i in range(CHUNK // SUB):
                r0 = SUB * i
                qi = q[r0:r0 + SUB]
                oi = o[r0:r0 + SUB]
                if i > 0:
                    r = b_sc[r0 - 1:r0, :]
                    qe = qi * jnp.exp(b[r0:r0 + SUB] - r)
                    ke = k[:r0] * jnp.exp(r - b[:r0])
                    oi = oi + _dot(_dot(qe, ke, NT), hv_ref[0:r0, sl])
                tiles = _hg_diag_tiles(b_sc, r0, rowi)
                ms = [qi * (k_sc[r0 + s:r0 + s + 1, :] * tiles[s]) for s in range(SUB)]
                R = _dot(jnp.concatenate(ms, axis=0), ones)
                for s in range(SUB):
                    oi = oi + R[SUB * s:SUB * s + SUB] * hv_ref[r0 + s:r0 + s + 1, sl]
                outs.append(oi)
            o_ref[:, sl] = jnp.concatenate(outs, axis=0)
            bl = b_sc[CHUNK - 1:CHUNK, :]
            S_ref[h] = ST * jnp.exp(bl) + _dot(hv_ref[:, sl], k * jnp.exp(bl - b), TN)

    blk = lambda c: pl.BlockSpec((CHUNK, HALF), lambda n, c=c: (n, c // 4))
    return _hosted_call(
        body, comm, name="hgrn2_fwd", grid=(N,),
        in_specs=[blk(C_HQ), blk(C_HF), blk(C_HV), _full(lb_logits.shape)],
        out_specs=[pl.BlockSpec((CHUNK, HALF), lambda n: (n, 0)),
                   pl.BlockSpec((1, HEADS, 128, 128), lambda n: (n, 0, 0, 0))],
        out_shape=[_sds((T, HALF)), _sds((N, HEADS, 128, 128))],
        scratch_shapes=[pltpu.VMEM((HEADS, 128, 128), F32), pltpu.VMEM((CHUNK, 128), F32),
                        pltpu.VMEM((CHUNK, 128), F32)],
        args=(u, u, u, lb_logits))


def _lane_col(x, c, lane):
    return _rowsum(jnp.where(lane == c, x, 0.0))


def _sub_row(x, r, sub):
    return _colsum(jnp.where(sub == r, x, 0.0))


def _ml_chunk(q, k, v, gates, h, C, nrow, mprev, consts):
    tril, onehot8, lane, sub8, causal = consts
    lf = jnp.minimum(gates, 0.0) - jnp.log(1.0 + jnp.exp(-jnp.abs(gates)))
    g_all = _dotx(tril, lf)
    g_rows = _dotx(onehot8, g_all, NT)
    i_rows = _dotx(onehot8, gates, NT)
    gcol = _lane_col(g_all, 4 + h, lane)
    icol = _lane_col(gates, h, lane)
    grow = _sub_row(g_rows, 4 + h, sub8)
    irow = _sub_row(i_rows, h, sub8)
    dmat = jnp.where(causal, gcol - grow + irow, NEG)
    m_inter = gcol + mprev
    m_t = jnp.maximum(m_inter, jnp.max(dmat, axis=1, keepdims=True))
    wi = jnp.exp(dmat - m_t)
    wn = jnp.exp(m_inter - m_t)
    s_mat = _dot(q, k, NT) * wi
    qc = _dot(q, C)
    qn = _rowsum(q * nrow)
    num = _dot(s_mat, v) + wn * qc
    den = _rowsum(s_mat) + wn * qn
    floor = jnp.exp(-m_t)
    nrm = jnp.maximum(jnp.abs(den), floor)
    gl = _sub_row(gcol, CHUNK - 1, _iota((CHUNK, 1), 0))
    a_row = gl - grow + irow
    m_new = jnp.maximum(gl + mprev, jnp.max(a_row, axis=1, keepdims=True))
    ws_col = jnp.exp(gl - gcol + icol - m_new)
    wo = jnp.exp(gl + mprev - m_new)
    return dict(wi=wi, wn=wn, s=s_mat, qc=qc, qn=qn, num=num, den=den, floor=floor, nrm=nrm,
                m_new=m_new, ws=ws_col, wo=wo)


def _ml_consts():
    lane = _iota((CHUNK, 128), 1)
    sub8 = _iota((8, CHUNK), 0)
    onehot8 = jnp.where(_iota((8, 128), 0) == _iota((8, 128), 1), 1.0, 0.0).astype(F32)
    causal = _iota((CHUNK, CHUNK), 0) >= _iota((CHUNK, CHUNK), 1)
    return _tri(CHUNK, True), onehot8, lane, sub8, causal


def _mlstm_fwd(qkc, u):
    T = u.shape[0]
    N = T // CHUNK

    def body(qk_ref, v_ref, g_ref, h_ref, cst_ref, nst_ref, mst_ref, C_ref, n_ref, m_ref):
        n = pl.program_id(0)

        @pl.when(n == 0)
        def _():
            C_ref[...] = jnp.zeros_like(C_ref)
            n_ref[...] = jnp.zeros_like(n_ref)
            m_ref[...] = jnp.zeros_like(m_ref)

        consts = _ml_consts()
        gates = g_ref[...]
        for h in range(HEADS):
            q = qk_ref[:, ML_DQK * h:ML_DQK * (h + 1)] * ML_SCALE
            k = qk_ref[:, 256 + ML_DQK * h:256 + ML_DQK * (h + 1)]
            v = v_ref[:, 128 * h:128 * (h + 1)]
            C, nrow, mprev = C_ref[h], n_ref[h], m_ref[h]
            cst_ref[0, h] = C
            nst_ref[0, h] = nrow
            mst_ref[0, h] = mprev
            r = _ml_chunk(q, k, v, gates, h, C, nrow, mprev, consts)
            h_ref[:, 128 * h:128 * (h + 1)] = r["num"] / r["nrm"]
            kw = k * r["ws"]
            C_ref[h] = r["wo"] * C + _dot(kw, v, TN)
            n_ref[h] = r["wo"] * nrow + _colsum(kw)
            m_ref[h] = r["m_new"]

    return pl.pallas_call(
        body, name="mlstm_fwd", grid=(N,),
        in_specs=[pl.BlockSpec((CHUNK, HALF), lambda n: (n, 0)),
                  pl.BlockSpec((CHUNK, HALF), lambda n: (n, C_MV // 4)),
                  pl.BlockSpec((CHUNK, 128), lambda n: (n, C_GATES))],
        out_specs=[pl.BlockSpec((CHUNK, HALF), lambda n: (n, 0)),
                   pl.BlockSpec((1, HEADS, ML_DQK, 128), lambda n: (n, 0, 0, 0)),
                   pl.BlockSpec((1, HEADS, 1, ML_DQK), lambda n: (n, 0, 0, 0)),
                   pl.BlockSpec((1, HEADS, 1, 1), lambda n: (n, 0, 0, 0))],
        out_shape=[_sds((T, HALF)), _sds((N, HEADS, ML_DQK, 128)), _sds((N, HEADS, 1, ML_DQK)),
                   _sds((N, HEADS, 1, 1))],
        scratch_shapes=[pltpu.VMEM((HEADS, ML_DQK, 128), F32), pltpu.VMEM((HEADS, 1, ML_DQK), F32),
                        pltpu.VMEM((HEADS, 1, 1), F32)],
        compiler_params=_cp("arbitrary"))(qkc, u, u)


GC = 4

BNT = (((2,), (2,)), ((0,), (0,)))
BNN = (((2,), (1,)), ((0,), (0,)))
BTN = (((1,), (1,)), ((0,), (0,)))


def _bdot(a, b, dims):
    return lax.dot_general(a.astype(_MXU), b.astype(_MXU), dims, preferred_element_type=F32)


def _bdotx(a, b, dims):
    return lax.dot_general(a, b, dims, precision=lax.Precision.HIGHEST, preferred_element_type=F32)


def _heads_to_batch(x, w):
    G = x.shape[0] // CHUNK
    x3 = x.reshape(G, CHUNK, HEADS * w)
    return jnp.stack([x3[:, :, w * h:w * (h + 1)] for h in range(HEADS)], axis=1).reshape(G * HEADS, CHUNK, w)


def _batch_to_heads(x3):
    B, _, w = x3.shape
    x4 = x3.reshape(B // HEADS, HEADS, CHUNK, w)
    return jnp.concatenate([x4[:, h] for h in range(HEADS)], axis=-1).reshape(B // HEADS * CHUNK, HEADS * w)


def _chunk_cumsum(x, rowmod, reverse=False):
    R = x.shape[0]
    for sh in (1, 2, 4, 8, 16, 32):
        if reverse:
            x = x + jnp.where(rowmod < CHUNK - sh, pltpu.roll(x, R - sh, 0), 0.0)
        else:
            x = x + jnp.where(rowmod >= sh, pltpu.roll(x, sh, 0), 0.0)
    return x


def _lanes_to_batch_cols(x, lane):
    G = x.shape[0] // CHUNK
    cols = [_lane_col(x, 4 + h, lane).reshape(G, CHUNK, 1) for h in range(HEADS)]
    return jnp.stack(cols, axis=1).reshape(G * HEADS, CHUNK, 1)


def _row_scalars(rows):
    lane = _iota((1, 128), 1)
    return jnp.stack([_rowsum(jnp.where(lane == 4 + h, r, 0.0)) for r in rows for h in range(HEADS)], axis=0)


def _ml_gates(gates, mprev_rows):
    R = gates.shape[0]
    G = R // CHUNK
    lane = _iota((R, 128), 1)
    rowmod = _iota((R, 128), 0) & (CHUNK - 1)
    lf = jnp.minimum(gates, 0.0) - jnp.log(1.0 + jnp.exp(-jnp.abs(gates)))
    g_all = _chunk_cumsum(lf, rowmod)
    x_all = pltpu.roll(gates, 4, 1) - g_all
    return g_all, x_all, lane, rowmod, G


def _ml_chunk_rows(g_all, x_all, mprev, g):
    gl = g_all[CHUNK * g + CHUNK - 8:CHUNK * (g + 1)]
    gl = _colsum(jnp.where(_iota((8, 128), 0) == 7, gl, 0.0))
    a = gl + x_all[CHUNK * g:CHUNK * (g + 1)]
    m_new = jnp.maximum(gl + mprev, jnp.max(a, axis=0, keepdims=True))
    return m_new, jnp.exp(gl + mprev - m_new), jnp.exp(a - m_new)


def _ml_batched(q3, k3, v3, g_all, x_all, lane, G, C3, n3, mprev3):
    gcol3 = _lanes_to_batch_cols(g_all, lane)
    onehot = jnp.where(_iota((G, 8, 128), 1) + 4 == _iota((G, 8, 128), 2), 1.0, 0.0).astype(F32)
    rows = _bdotx(onehot, x_all.reshape(G, CHUNK, 128), BNT)
    sub = _iota((G, 8, CHUNK), 1)
    row3 = jnp.stack([jnp.sum(jnp.where(sub == h, rows, 0.0), axis=1, keepdims=True) for h in range(HEADS)],
                     axis=1).reshape(G * HEADS, 1, CHUNK)
    causal = _iota((1, CHUNK, CHUNK), 1) >= _iota((1, CHUNK, CHUNK), 2)
    dmat = jnp.where(causal, gcol3 + row3, NEG)
    m_inter = gcol3 + mprev3
    m_t = jnp.maximum(m_inter, jnp.max(dmat, axis=2, keepdims=True))
    wi = jnp.exp(dmat - m_t)
    wn = jnp.exp(m_inter - m_t)
    s3 = _bdot(q3, k3, BNT) * wi
    qc = _bdot(q3, C3, BNN)
    qn = jnp.sum(q3 * n3, axis=2, keepdims=True)
    num = _bdot(s3, v3, BNN) + wn * qc
    den = jnp.sum(s3, axis=2, keepdims=True) + wn * qn
    floor = jnp.exp(-m_t)
    return dict(wi=wi, wn=wn, s=s3, qc=qc, qn=qn, num=num, den=den, floor=floor, nrm=jnp.maximum(jnp.abs(den), floor))


def _mlstm_fwd2(qkc, u):
    T = u.shape[0]
    G = min(GC, T // CHUNK)
    R = G * CHUNK
    N = T // CHUNK

    def body(qk_ref, v_ref, g_ref, h_ref, cst_ref, nst_ref, mst_ref, C_ref, n_ref, m_ref):
        @pl.when(pl.program_id(0) == 0)
        def _():
            C_ref[...] = jnp.zeros_like(C_ref)
            n_ref[...] = jnp.zeros_like(n_ref)
            m_ref[...] = jnp.zeros_like(m_ref)

        g_all, x_all, lane, _, _ = _ml_gates(g_ref[...], None)
        m_row = m_ref[...]
        mprev_rows, wo_rows, ws_parts = [], [], []
        for g in range(G):
            mprev_rows.append(m_row)
            m_row, wo, ws = _ml_chunk_rows(g_all, x_all, m_row, g)
            wo_rows.append(wo)
            ws_parts.append(ws)
        m_ref[...] = m_row
        mst_ref[...] = jnp.stack(mprev_rows, axis=0)
        ws3 = _lanes_to_batch_cols(jnp.concatenate(ws_parts, axis=0), lane)
        wo4 = _row_scalars(wo_rows).reshape(G, HEADS, 1, 1)
        q3 = _heads_to_batch(qk_ref[:, :256] * ML_SCALE, ML_DQK)
        k3 = _heads_to_batch(qk_ref[:, 256:], ML_DQK)
        v3 = _heads_to_batch(v_ref[...], 128)
        kw = k3 * ws3
        cloc = _bdot(kw, v3, BTN).reshape(G, HEADS, ML_DQK, 128)
        nloc = jnp.sum(kw, axis=1, keepdims=True).reshape(G, HEADS, 1, ML_DQK)
        C, nn = C_ref[...], n_ref[...]
        cs, ns = [], []
        for g in range(G):
            cs.append(C)
            ns.append(nn)
            C = wo4[g] * C + cloc[g]
            nn = wo4[g] * nn + nloc[g]
        C_ref[...] = C
        n_ref[...] = nn
        c4, n4 = jnp.stack(cs, axis=0), jnp.stack(ns, axis=0)
        cst_ref[...] = c4
        nst_ref[...] = n4
        r = _ml_batched(q3, k3, v3, g_all, x_all, lane, G, c4.reshape(G * HEADS, ML_DQK, 128),
                        n4.reshape(G * HEADS, 1, ML_DQK), _row_scalars(mprev_rows))
        h_ref[...] = _batch_to_heads(r["num"] / r["nrm"])

    return pl.pallas_call(
        body, name="mlstm_fwd", grid=(N // G,),
        in_specs=[pl.BlockSpec((R, HALF), lambda n: (n, 0)), pl.BlockSpec((R, HALF), lambda n: (n, C_MV // 4)),
                  pl.BlockSpec((R, 128), lambda n: (n, C_GATES))],
        out_specs=[pl.BlockSpec((R, HALF), lambda n: (n, 0)),
                   pl.BlockSpec((G, HEADS, ML_DQK, 128), lambda n: (n, 0, 0, 0)),
                   pl.BlockSpec((G, HEADS, 1, ML_DQK), lambda n: (n, 0, 0, 0)),
                   pl.BlockSpec((G, 1, 128), lambda n: (n, 0, 0))],
        out_shape=[_sds((T, HALF)), _sds((N, HEADS, ML_DQK, 128)), _sds((N, HEADS, 1, ML_DQK)), _sds((N, 1, 128))],
        scratch_shapes=[pltpu.VMEM((HEADS, ML_DQK, 128), F32), pltpu.VMEM((HEADS, 1, ML_DQK), F32),
                        pltpu.VMEM((1, 128), F32)],
        compiler_params=_cp("arbitrary"))(qkc, u, u)


def _mlstm_bwd2(qkc, u, dh, cst, nst, mst):
    T = u.shape[0]
    G = min(GC, T // CHUNK)
    R = G * CHUNK
    NG = T // R

    def body(qk_ref, v_ref, g_ref, dh_ref, cst_ref, nst_ref, mst_ref, dqk_ref, dv_ref, dgt_ref, dC_ref, dn_ref):
        @pl.when(pl.program_id(0) == 0)
        def _():
            dC_ref[...] = jnp.zeros_like(dC_ref)
            dn_ref[...] = jnp.zeros_like(dn_ref)

        B = G * HEADS
        gates = g_ref[...]
        g_all, x_all, lane, rowmod, _ = _ml_gates(gates, None)
        mprev_rows = [mst_ref[g] for g in range(G)]
        wo_rows, ws_parts = [], []
        for g in range(G):
            _, wo, ws = _ml_chunk_rows(g_all, x_all, mprev_rows[g], g)
            wo_rows.append(wo)
            ws_parts.append(ws)
        ws3 = _lanes_to_batch_cols(jnp.concatenate(ws_parts, axis=0), lane)
        wo3 = _row_scalars(wo_rows)
        wo4 = wo3.reshape(G, HEADS, 1, 1)
        q3 = _heads_to_batch(qk_ref[:, :256] * ML_SCALE, ML_DQK)
        k3 = _heads_to_batch(qk_ref[:, 256:], ML_DQK)
        v3 = _heads_to_batch(v_ref[...], 128)
        dh3 = _heads_to_batch(dh_ref[...], 128)
        C3 = cst_ref[...].reshape(B, ML_DQK, 128)
        n3 = nst_ref[...].reshape(B, 1, ML_DQK)
        r = _ml_batched(q3, k3, v3, g_all, x_all, lane, G, C3, n3, _row_scalars(mprev_rows))
        wn, s3 = r["wn"], r["s"]
        inv = 1.0 / r["nrm"]
        dnum = dh3 * inv
        dnrm = -jnp.sum(dh3 * (r["num"] * inv), axis=2, keepdims=True) * inv
        dden = jnp.where(jnp.abs(r["den"]) > r["floor"], dnrm * jnp.sign(r["den"]), 0.0)
        ds = _bdot(dnum, v3, BNT) + dden
        dqk = ds * r["wi"]
        dd = ds * s3
        qw = q3 * wn
        dcloc = _bdot(qw, dnum, BTN).reshape(G, HEADS, ML_DQK, 128)
        dnloc = jnp.sum(qw * dden, axis=1, keepdims=True).reshape(G, HEADS, 1, ML_DQK)
        dC, dn = dC_ref[...], dn_ref[...]
        dcs, dns = [None] * G, [None] * G
        for g in reversed(range(G)):
            dcs[g], dns[g] = dC, dn
            dC = wo4[g] * dC + dcloc[g]
            dn = wo4[g] * dn + dnloc[g]
        dC_ref[...] = dC
        dn_ref[...] = dn
        dC3 = jnp.stack(dcs, axis=0).reshape(B, ML_DQK, 128)
        dn3 = jnp.stack(dns, axis=0).reshape(B, 1, ML_DQK)
        dk_st = ws3 * (_bdot(v3, dC3, BNT) + dn3)
        dq = _bdot(dqk, k3, BNN) + wn * (_bdot(dnum, C3, BNT) + dden * n3)
        dk = _bdot(dqk, q3, BTN) + dk_st
        dv = _bdot(s3, dnum, BTN) + ws3 * _bdot(k3, dC3, BNN)
        dv_ref[...] = _batch_to_heads(dv)
        dqk_ref[...] = jnp.concatenate([_batch_to_heads(dq * ML_SCALE), _batch_to_heads(dk)], axis=1)
        e_col = wn * (jnp.sum(dnum * r["qc"], axis=2, keepdims=True) + dden * r["qn"])
        c_col = jnp.sum(k3 * dk_st, axis=2, keepdims=True)
        z = wo3 * (jnp.sum(dC3 * C3, axis=(1, 2), keepdims=True) + jnp.sum(dn3 * n3, axis=(1, 2), keepdims=True))
        dd_cols = _bdotx(dd, jnp.ones((B, CHUNK, 128), F32), BTN)[:, :, 0:1]
        last = _iota((1, CHUNK, 1), 1) == CHUNK - 1
        dg3 = jnp.sum(dd, axis=2, keepdims=True) - dd_cols + e_col - c_col
        dg3 = dg3 + jnp.where(last, jnp.sum(c_col, axis=1, keepdims=True) + z, 0.0)
        di3 = dd_cols + c_col

        def to_lanes(x3, first):
            x4 = x3.reshape(G, HEADS, CHUNK, 1)
            return sum(jnp.where(lane == first + h, x4[:, h].reshape(R, 1), 0.0) for h in range(HEADS))

        dlf = _chunk_cumsum(to_lanes(dg3, 4), rowmod, reverse=True)
        dgt_ref[...] = to_lanes(di3, 0) + dlf * _sig(-gates)

    rev = lambda w, c: pl.BlockSpec((R, w), lambda i, c=c: (NG - 1 - i, c))
    st = lambda *s: pl.BlockSpec((G,) + s, lambda i: (NG - 1 - i,) + (0,) * len(s))
    return pl.pallas_call(
        body, name="mlstm_bwd", grid=(NG,),
        in_specs=[rev(HALF, 0), rev(HALF, C_MV // 4), rev(128, C_GATES), rev(HALF, 0),
                  st(HEADS, ML_DQK, 128), st(HEADS, 1, ML_DQK), st(1, 128)],
        out_specs=[rev(HALF, 0), rev(HALF, 0), rev(128, 0)],
        out_shape=[_sds((T, HALF)), _sds((T, HALF)), _sds((T, 128))],
        scratch_shapes=[pltpu.VMEM((HEADS, ML_DQK, 128), F32), pltpu.VMEM((HEADS, 1, ML_DQK), F32)],
        compiler_params=_cp("arbitrary"))(qkc, u, u, dh, cst, nst, mst)


def _hg_prep(hq_ref, hf_ref, lg_ref, b_sc, k_sc):
    R = hq_ref.shape[0]
    G = R // CHUNK
    lb = _sig(lg_ref[0:1, :] - lg_ref[1:2, :])
    hq = hq_ref[...]
    q, g, k, f, sg, nsg, sq = _hg_gates(hq, hf_ref[...], lb)
    rowmod = _iota((R, HALF), 0) & (CHUNK - 1)
    b = _chunk_cumsum(g, rowmod)
    last8 = _iota((8, HALF), 0) == 7
    bl_rows = [_colsum(jnp.where(last8, b[CHUNK * c + CHUNK - 8:CHUNK * (c + 1)], 0.0)) for c in range(G)]
    bl3 = jnp.stack([r[:, 128 * h:128 * (h + 1)] for r in bl_rows for h in range(HEADS)], axis=0)
    b3, k3 = _heads_to_batch(b, 128), _heads_to_batch(k, 128)
    b_sc[...] = b3
    k_sc[...] = k3
    return dict(G=G, lb=lb, hq=hq, f=f, sg=sg, nsg=nsg, sq=sq, rowmod=rowmod, q3=_heads_to_batch(q, 128), k3=k3, b3=b3,
                bl3=bl3)


def _hg_diag_tiles3(b_sc, b3, r0, rowi):
    bi = b3[:, r0:r0 + SUB]
    return [jnp.exp(jnp.where(rowi >= s, bi - b_sc[:, r0 + s:r0 + s + 1, :], NEG)) for s in range(SUB)]


def _hgrn2_fwd2(u, lb_logits, comm=None):
    T = u.shape[0]
    G = min(GC, T // CHUNK)
    R, B, N = G * CHUNK, G * HEADS, T // CHUNK

    def body(hq_ref, hf_ref, hv_ref, lg_ref, o_ref, st_ref, S_ref, b_sc, k_sc, v_sc):
        @pl.when(pl.program_id(0) == 0)
        def _():
            S_ref[...] = jnp.zeros_like(S_ref)

        pz = _hg_prep(hq_ref, hf_ref, lg_ref, b_sc, k_sc)
        q3, k3, b3, bl3 = pz["q3"], pz["k3"], pz["b3"], pz["bl3"]
        v3 = _heads_to_batch(hv_ref[...], 128)
        v_sc[...] = v3
        stloc = _bdot(v3, k3 * jnp.exp(bl3 - b3), BTN).reshape(G, HEADS, 128, 128)
        dec = jnp.exp(bl3).reshape(G, HEADS, 1, 128)
        ST = S_ref[...]
        sts = []
        for c in range(G):
            sts.append(ST)
            ST = ST * dec[c] + stloc[c]
        S_ref[...] = ST
        st4 = jnp.stack(sts, axis=0)
        st_ref[...] = st4
        o = _bdot(q3 * jnp.exp(b3), st4.reshape(B, 128, 128), BNT)
        ones = jnp.ones((128, 128), F32)
        rowi = _iota((1, SUB, 128), 1)
        outs = []
        for i in range(CHUNK // SUB):
            r0 = SUB * i
            qi = q3[:, r0:r0 + SUB]
            oi = o[:, r0:r0 + SUB]
            if i > 0:
                r = b_sc[:, r0 - 1:r0, :]
                qe = qi * jnp.exp(b3[:, r0:r0 + SUB] - r)
                ke = k3[:, :r0] * jnp.exp(r - b3[:, :r0])
                oi = oi + _bdot(_bdot(qe, ke, BNT), v3[:, :r0], BNN)
            tiles = _hg_diag_tiles3(b_sc, b3, r0, rowi)
            ms = [qi * (k_sc[:, r0 + s:r0 + s + 1, :] * tiles[s]) for s in range(SUB)]
            Rm = _dot(jnp.concatenate(ms, axis=1).reshape(B * SUB * SUB, 128), ones).reshape(B, SUB * SUB, 128)
            for s in range(SUB):
                oi = oi + Rm[:, SUB * s:SUB * s + SUB] * v_sc[:, r0 + s:r0 + s + 1, :]
            outs.append(oi)
        o_ref[...] = _batch_to_heads(jnp.concatenate(outs, axis=1))

    blk = lambda c: pl.BlockSpec((R, HALF), lambda n, c=c: (n, c // 4))
    return _hosted_call(
        body, comm, name="hgrn2_fwd", grid=(N // G,),
        in_specs=[blk(C_HQ), blk(C_HF), blk(C_HV), _full(lb_logits.shape)],
        out_specs=[pl.BlockSpec((R, HALF), lambda n: (n, 0)),
                   pl.BlockSpec((G, HEADS, 128, 128), lambda n: (n, 0, 0, 0))],
        out_shape=[_sds((T, HALF)), _sds((N, HEADS, 128, 128))],
        scratch_shapes=[pltpu.VMEM((HEADS, 128, 128), F32)] + [pltpu.VMEM((B, CHUNK, 128), F32)] * 3,
        args=(u, u, u, lb_logits))


def _hgrn2_bwd2(u, lb_logits, do, states, comm=None):
    T = u.shape[0]
    G = min(GC, T // CHUNK)
    R, B, NG = G * CHUNK, G * HEADS, T // (G * CHUNK)

    def body(hq_ref, hf_ref, hv_ref, lg_ref, do_ref, st_ref, dhq_ref, dhf_ref, dhv_ref, dlb_ref,
             dS_ref, b_sc, k_sc, v_sc):
        @pl.when(pl.program_id(0) == 0)
        def _():
            dS_ref[...] = jnp.zeros_like(dS_ref)
            dlb_ref[...] = jnp.zeros_like(dlb_ref)

        pz = _hg_prep(hq_ref, hf_ref, lg_ref, b_sc, k_sc)
        q3, k3, b3, bl3, lb = pz["q3"], pz["k3"], pz["b3"], pz["bl3"], pz["lb"]
        v3 = _heads_to_batch(hv_ref[...], 128)
        v_sc[...] = v3
        do3 = _heads_to_batch(do_ref[...], 128)
        st3 = st_ref[...].reshape(B, 128, 128)
        eb = jnp.exp(b3)
        ebl = jnp.exp(bl3 - b3)
        qt = q3 * eb
        kl = k3 * ebl
        dstloc = _bdot(do3, qt, BTN).reshape(G, HEADS, 128, 128)
        dec = jnp.exp(bl3).reshape(G, HEADS, 1, 128)
        dST = dS_ref[...]
        dsts = [None] * G
        for c in reversed(range(G)):
            dsts[c] = dST
            dST = dST * dec[c] + dstloc[c]
        dS_ref[...] = dST
        dst3 = jnp.stack(dsts, axis=0).reshape(B, 128, 128)
        dqt = _bdot(do3, st3, BNN)
        dkl = _bdot(v3, dst3, BNN)
        dv_acc = _bdot(kl, dst3, BNT)
        ones = jnp.ones((128, 128), F32)
        rowi = _iota((1, SUB, 128), 1)
        dq_parts = []
        dk_in = jnp.zeros((B, CHUNK, 128), F32)
        for i_s in range(CHUNK // SUB):
            r0 = SUB * i_s
            qi = q3[:, r0:r0 + SUB]
            doi = do3[:, r0:r0 + SUB]
            dqi = jnp.zeros((B, SUB, 128), F32)
            if i_s > 0:
                r = b_sc[:, r0 - 1:r0, :]
                eq = jnp.exp(b3[:, r0:r0 + SUB] - r)
                ek = jnp.exp(r - b3[:, :r0])
                qe = qi * eq
                ke = k3[:, :r0] * ek
                a_off = _bdot(qe, ke, BNT)
                p_off = _bdot(doi, v3[:, :r0], BNT)
                pad = jnp.zeros((B, CHUNK - r0, 128), F32)
                dv_acc = dv_acc + jnp.concatenate([_bdot(a_off, doi, BTN), pad], axis=1)
                dqi = dqi + _bdot(p_off, ke, BNN) * eq
                dk_in = dk_in + jnp.concatenate([_bdot(p_off, qe, BTN) * ek, pad], axis=1)
            tiles = _hg_diag_tiles3(b_sc, b3, r0, rowi)
            ms = [qi * (k_sc[:, r0 + s:r0 + s + 1, :] * tiles[s]) for s in range(SUB)]
            ps = [doi * v_sc[:, r0 + s:r0 + s + 1, :] for s in range(SUB)]
            n_r = 2 * SUB * SUB
            Rm = _dot(jnp.concatenate(ms + ps, axis=1).reshape(B * n_r, 128), ones).reshape(B, n_r, 128)
            dv_rows, dk_rows = [], []
            for s in range(SUB):
                a_s = Rm[:, SUB * s:SUB * s + SUB]
                p_s = Rm[:, SUB * (SUB + s):SUB * (SUB + s) + SUB]
                pt = p_s * tiles[s]
                dqi = dqi + pt * k_sc[:, r0 + s:r0 + s + 1, :]
                dk_rows.append(jnp.sum(pt * qi, axis=1, keepdims=True))
                dv_rows.append(jnp.sum(a_s * doi, axis=1, keepdims=True))
            lo = [jnp.zeros((B, r0, 128), F32)] * (r0 > 0)
            hi = [jnp.zeros((B, CHUNK - r0 - SUB, 128), F32)] * (r0 + SUB < CHUNK)
            dk_in = dk_in + jnp.concatenate(lo + dk_rows + hi, axis=1)
            dv_acc = dv_acc + jnp.concatenate(lo + dv_rows + hi, axis=1)
            dq_parts.append(dqi)
        dq_in = jnp.concatenate(dq_parts, axis=1)
        db = qt * dqt + q3 * dq_in - k3 * dk_in - kl * dkl
        last = jnp.sum(kl * dkl, axis=1, keepdims=True) + jnp.exp(bl3) * jnp.sum(st3 * dst3, axis=1, keepdims=True)
        db = db + jnp.where(_iota((1, CHUNK, 1), 1) == CHUNK - 1, last, 0.0)
        dg = _chunk_cumsum(_batch_to_heads(db), pz["rowmod"], reverse=True)
        dq_tot = _batch_to_heads(dqt * eb + dq_in)
        dk_tot = _batch_to_heads(dkl * ebl + dk_in)
        common = dg / pz["f"] - dk_tot
        dhf_ref[...] = (1.0 - lb) * pz["sg"] * pz["nsg"] * common
        dl0 = _colsum(pz["nsg"] * common) * lb * (1.0 - lb)
        dlb_ref[0:1, :] += dl0
        dlb_ref[1:2, :] -= dl0
        dhq_ref[...] = dq_tot * _dsilu(pz["hq"], pz["sq"])
        dhv_ref[...] = _batch_to_heads(dv_acc)

    rev = lambda c: pl.BlockSpec((R, HALF), lambda i, c=c: (NG - 1 - i, c // 4))
    rev0 = pl.BlockSpec((R, HALF), lambda i: (NG - 1 - i, 0))
    return _hosted_call(
        body, comm, name="hgrn2_bwd", grid=(NG,),
        in_specs=[rev(C_HQ), rev(C_HF), rev(C_HV), _full(lb_logits.shape), rev0,
                  pl.BlockSpec((G, HEADS, 128, 128), lambda i: (NG - 1 - i, 0, 0, 0))],
        out_specs=[rev0, rev0, rev0, _full((2, HALF))],
        out_shape=[_sds((T, HALF))] * 3 + [_sds((2, HALF))],
        scratch_shapes=[pltpu.VMEM((HEADS, 128, 128), F32)] + [pltpu.VMEM((B, CHUNK, 128), F32)] * 3,
        args=(u, u, u, lb_logits, do, states))


def _head_norm(o, g):
    rs_parts, r_parts = [], []
    for h in range(HEADS):
        oh = o[:, 128 * h:128 * (h + 1)]
        rs = lax.rsqrt(jnp.mean(oh * oh, axis=-1, keepdims=True) + RMS_EPS)
        rs_parts.append(rs)
        r_parts.append(oh * rs)
    return jnp.concatenate(r_parts, axis=1), rs_parts


def _mix_in(o_hg, h_ml, hgate, mo, g_hg, g_ml):
    r_hg, _ = _head_norm(o_hg, g_hg)
    r_ml, _ = _head_norm(h_ml, g_ml)
    a = r_hg * g_hg * (hgate * _sig(hgate))
    b = r_ml * g_ml * _sig(mo)
    return jnp.concatenate([a, b], axis=1)


def _out_proj_ln(x, u, o_hg, h_ml, g_hg, g_ml, w_out, ln_g, ln_b, tm):
    T = x.shape[0]

    def body(x_ref, hgate_ref, mo_ref, ohg_ref, hml_ref, ghg_ref, gml_ref, w_ref, g_ref, b_ref,
             m_ref, z_ref, x1_ref):
        m = _mix_in(ohg_ref[...], hml_ref[...], hgate_ref[...], mo_ref[...], ghg_ref[...], gml_ref[...])
        m_ref[...] = m
        z = ALPHA * x_ref[...] + _dot(m, w_ref[...])
        z_ref[...] = z
        x1_ref[...] = _ln_fwd(z, g_ref[...], b_ref[...])[0]

    return pl.pallas_call(
        body, name="out_proj_ln1", grid=(T // tm,),
        in_specs=[_row(tm, D_MODEL), _row(tm, HALF, C_HGATE // 4), _row(tm, HALF, C_MO // 4),
                  _row(tm, HALF), _row(tm, HALF), _full(g_hg.shape), _full(g_ml.shape),
                  _full(w_out.shape), _full(ln_g.shape), _full(ln_b.shape)],
        out_specs=[_row(tm, D_MODEL)] * 3, out_shape=[_sds((T, D_MODEL))] * 3,
        compiler_params=_cp("parallel"))(x, u, u, o_hg, h_ml, g_hg, g_ml, w_out, ln_g, ln_b)


def _ffn_ln(x1, wg, wu, wd, ln_g, ln_b, tm):
    T = x1.shape[0]

    def body(x_ref, wg_ref, wu_ref, wd_ref, g_ref, b_ref, z_ref, x2_ref):
        x = x_ref[...]
        a = _dot(x, wg_ref[...])
        hh = a * _sig(a) * _dot(x, wu_ref[...])
        z = ALPHA * x + _dot(hh, wd_ref[...])
        z_ref[...] = z
        x2_ref[...] = _ln_fwd(z, g_ref[...], b_ref[...])[0]

    return pl.pallas_call(
        body, name="ffn_ln2", grid=(T // tm,),
        in_specs=[_row(tm, D_MODEL), _full(wg.shape), _full(wu.shape), _full(wd.shape),
                  _full(ln_g.shape), _full(ln_b.shape)],
        out_specs=[_row(tm, D_MODEL)] * 2, out_shape=[_sds((T, D_MODEL))] * 2,
        compiler_params=_cp("parallel"))(x1, wg, wu, wd, ln_g, ln_b)


def _ple_loss_ln2_bwd(x2, z2, p, tgt, wpg, bpg, wpp, ln_g, ln_b, tm):
    T = x2.shape[0]

    def body(x2_ref, z_ref, p_ref, t_ref, wpg_ref, bpg_ref, wpp_ref, g_ref, b_ref,
             de_ref, dgp_ref, dz_ref, loss_ref, dbpg_ref, dg_ref, db_ref):
        i = pl.program_id(0)

        @pl.when(i == 0)
        def _():
            for r in (loss_ref, dbpg_ref, dg_ref, db_ref):
                r[...] = jnp.zeros_like(r)

        x2 = x2_ref[...]
        gate = _sig(_dot(x2, wpg_ref[...]) + bpg_ref[...])
        e = _dot(p_ref[...], wpp_ref[...])
        err = x2 + gate * e - t_ref[...]
        loss_ref[...] += _colsum(err * err)
        dy = err * (1.0 / D_MODEL)
        de_ref[...] = dy * gate
        dgp = dy * e * gate * (1.0 - gate)
        dgp_ref[...] = dgp
        dbpg_ref[...] += _colsum(dgp)
        dx2 = dy + _dot(dgp, wpg_ref[...], NT)
        _, xhat, rstd = _ln_fwd(z_ref[...], g_ref[...], b_ref[...])
        dg_ref[...] += _colsum(dx2 * xhat)
        db_ref[...] += _colsum(dx2)
        dz_ref[...] = _ln_bwd(dx2, xhat, rstd, g_ref[...])

    vec = _full((1, D_MODEL))
    return pl.pallas_call(
        body, name="ple_loss_ln2_bwd", grid=(T // tm,),
        in_specs=[_row(tm, D_MODEL), _row(tm, D_MODEL), _row(tm, PLE_DIM), _row(tm, D_MODEL),
                  _full(wpg.shape), vec, _full(wpp.shape), vec, vec],
        out_specs=[_row(tm, D_MODEL)] * 3 + [vec] * 4,
        out_shape=[_sds((T, D_MODEL))] * 3 + [_sds((1, D_MODEL))] * 4,
        compiler_params=_cp("arbitrary"))(x2, z2, p, tgt, wpg, bpg, wpp, ln_g, ln_b)


def _ffn_bwd_ln1_bwd(x1, z1, dz2, wg, wu, wd, ln_g, ln_b, tm):
    T = x1.shape[0]

    def body(x_ref, z_ref, dz2_ref, wg_ref, wu_ref, wd_ref, g_ref, b_ref,
             h_ref, da_ref, dbb_ref, dz1_ref, dg_ref, db_ref):
        i = pl.program_id(0)

        @pl.when(i == 0)
        def _():
            dg_ref[...] = jnp.zeros_like(dg_ref)
            db_ref[...] = jnp.zeros_like(db_ref)

        x = x_ref[...]
        dz2 = dz2_ref[...]
        a = _dot(x, wg_ref[...])
        bb = _dot(x, wu_ref[...])
        sa = _sig(a)
        act = a * sa
        h_ref[...] = act * bb
        dh = _dot(dz2, wd_ref[...], NT)
        da = dh * bb * _dsilu(a, sa)
        dbb = dh * act
        da_ref[...] = da
        dbb_ref[...] = dbb
        dx1 = ALPHA * dz2 + _dot(da, wg_ref[...], NT) + _dot(dbb, wu_ref[...], NT)
        _, xhat, rstd = _ln_fwd(z_ref[...], g_ref[...], b_ref[...])
        dg_ref[...] += _colsum(dx1 * xhat)
        db_ref[...] += _colsum(dx1)
        dz1_ref[...] = _ln_bwd(dx1, xhat, rstd, g_ref[...])

    vec = _full((1, D_MODEL))
    return pl.pallas_call(
        body, name="ffn_bwd_ln1_bwd", grid=(T // tm,),
        in_specs=[_row(tm, D_MODEL)] * 3 + [_full(wg.shape), _full(wu.shape), _full(wd.shape), vec, vec],
        out_specs=[_row(tm, D_FF)] * 3 + [_row(tm, D_MODEL), vec, vec],
        out_shape=[_sds((T, D_FF))] * 3 + [_sds((T, D_MODEL)), _sds((1, D_MODEL)), _sds((1, D_MODEL))],
        compiler_params=_cp("arbitrary"))(x1, z1, dz2, wg, wu, wd, ln_g, ln_b)


def _out_proj_bwd(dz1, u, o_hg, h_ml, g_hg, g_ml, w_out, tm):
    T = dz1.shape[0]

    def body(dz_ref, hgate_ref, mo_ref, ohg_ref, hml_ref, ghg_ref, gml_ref, w_ref,
             dohg_ref, dhml_ref, dhgate_ref, dmo_ref, dghg_ref, dgml_ref):
        i = pl.program_id(0)

        @pl.when(i == 0)
        def _():
            dghg_ref[...] = jnp.zeros_like(dghg_ref)
            dgml_ref[...] = jnp.zeros_like(dgml_ref)

        dm = _dot(dz_ref[...], w_ref[...], NT)

        def half(dmh, o, gvec, gate_val, dgate_fac, do_ref, dgate_ref, dgvec_ref):
            r, rs = _head_norm(o, gvec)
            nrm = r * gvec
            dgate_ref[...] = dmh * nrm * dgate_fac
            dn = dmh * gate_val
            dgvec_ref[...] += _colsum(dn * r)
            dr = dn * gvec
            parts = []
            for h in range(HEADS):
                sl = slice(128 * h, 128 * (h + 1))
                parts.append(rs[h] * (dr[:, sl] - r[:, sl] * jnp.mean(dr[:, sl] * r[:, sl], axis=-1, keepdims=True)))
            do_ref[...] = jnp.concatenate(parts, axis=1)

        hg = hgate_ref[...]
        shg = _sig(hg)
        half(dm[:, :HALF], ohg_ref[...], ghg_ref[...], hg * shg, _dsilu(hg, shg), dohg_ref, dhgate_ref, dghg_ref)
        smo = _sig(mo_ref[...])
        half(dm[:, HALF:], hml_ref[...], gml_ref[...], smo, smo * (1.0 - smo), dhml_ref, dmo_ref, dgml_ref)

    vec = _full((1, HALF))
    return pl.pallas_call(
        body, name="out_proj_bwd", grid=(T // tm,),
        in_specs=[_row(tm, D_MODEL), _row(tm, HALF, C_HGATE // 4), _row(tm, HALF, C_MO // 4),
                  _row(tm, HALF), _row(tm, HALF), vec, vec, _full(w_out.shape)],
        out_specs=[_row(tm, HALF)] * 4 + [vec, vec],
        out_shape=[_sds((T, HALF))] * 4 + [_sds((1, HALF))] * 2,
        compiler_params=_cp("arbitrary"))(dz1, u, u, o_hg, h_ml, g_hg, g_ml, w_out)


def _hgrn2_bwd(u, lb_logits, do, states, comm=None):
    T = u.shape[0]
    N = T // CHUNK

    def body(hq_ref, hf_ref, hv_ref, lg_ref, do_ref, st_ref, dhq_ref, dhf_ref, dhv_ref, dlb_ref,
             dS_ref, b_sc, k_sc):
        i = pl.program_id(0)

        @pl.when(i == 0)
        def _():
            dS_ref[...] = jnp.zeros_like(dS_ref)
            dlb_ref[...] = jnp.zeros_like(dlb_ref)

        lb_all = _sig(lg_ref[0:1, :] - lg_ref[1:2, :])
        tril = _tri(CHUNK, True)
        triu = _tri(CHUNK, False)
        ones = jnp.ones((128, 128), F32)
        rowi = _iota((SUB, 128), 0)
        row64 = _iota((CHUNK, 128), 0)
        lane = _iota((SUB, 128), 1)
        for h in range(HEADS):
            sl = slice(128 * h, 128 * h + 128)
            lb = lb_all[:, sl]
            hq = hq_ref[:, sl]
            q, g, k, f, sg, nsg, sq = _hg_gates(hq, hf_ref[:, sl], lb)
            v = hv_ref[:, sl]
            do_h = do_ref[:, sl]
            b = _dotx(tril, g)
            b_sc[...] = b
            k_sc[...] = k
            ST = st_ref[0, h]
            dST = dS_ref[h]
            bl = b_sc[CHUNK - 1:CHUNK, :]
            eb = jnp.exp(b)
            ebl = jnp.exp(bl - b)
            qt = q * eb
            kl = k * ebl
            dqt = _dot(do_h, ST)
            dkl = _dot(v, dST)
            dv_acc = _dot(kl, dST, NT)
            dq_parts = []
            dk_in = jnp.zeros((CHUNK, 128), F32)
            for i_s in range(CHUNK // SUB):
                r0 = SUB * i_s
                qi = q[r0:r0 + SUB]
                doi = do_h[r0:r0 + SUB]
                dqi = jnp.zeros((SUB, 128), F32)
                if i_s > 0:
                    r = b_sc[r0 - 1:r0, :]
                    eq = jnp.exp(b[r0:r0 + SUB] - r)
                    ek = jnp.exp(r - b[:r0])
                    qe = qi * eq
                    ke = k[:r0] * ek
                    a_off = _dot(qe, ke, NT)
                    p_off = _dot(doi, v[:r0], NT)
                    pad = jnp.zeros((CHUNK - r0, 128), F32)
                    dv_acc = dv_acc + jnp.concatenate([_dot(a_off, doi, TN), pad], axis=0)
                    dqi = dqi + _dot(p_off, ke) * eq
                    dk_in = dk_in + jnp.concatenate([_dot(p_off, qe, TN) * ek, pad], axis=0)
                tiles = _hg_diag_tiles(b_sc, r0, rowi)
                ms = [qi * (k_sc[r0 + s:r0 + s + 1, :] * tiles[s]) for s in range(SUB)]
                ps = [doi * hv_ref[r0 + s:r0 + s + 1, sl] for s in range(SUB)]
                R = _dot(jnp.concatenate(ms + ps, axis=0), ones)
                dv_rows, dk_rows = [], []
                for s in range(SUB):
                    a_s = R[SUB * s:SUB * s + SUB]
                    p_s = R[SUB * (SUB + s):SUB * (SUB + s) + SUB]
                    pt = p_s * tiles[s]
                    dqi = dqi + pt * k_sc[r0 + s:r0 + s + 1, :]
                    dk_rows.append(_colsum(pt * qi))
                    dv_rows.append(_colsum(a_s * doi))
                pad_lo = jnp.zeros((r0, 128), F32)
                pad_hi = jnp.zeros((CHUNK - r0 - SUB, 128), F32)
                dk_in = dk_in + jnp.concatenate([pad_lo] * (r0 > 0) + dk_rows + [pad_hi] * (r0 + SUB < CHUNK), axis=0)
                dv_acc = dv_acc + jnp.concatenate([pad_lo] * (r0 > 0) + dv_rows + [pad_hi] * (r0 + SUB < CHUNK), axis=0)
                dq_parts.append(dqi)
            dq_in = jnp.concatenate(dq_parts, axis=0)
            db = qt * dqt + q * dq_in - k * dk_in - kl * dkl
            last = _colsum(kl * dkl) + jnp.exp(bl) * _colsum(ST * dST)
            db = db + jnp.where(row64 == CHUNK - 1, last, 0.0)
            dg = _dotx(triu, db)
            dq_tot = dqt * eb + dq_in
            dk_tot = dkl * ebl + dk_in
            common = dg / f - dk_tot
            dhf_ref[:, sl] = (1.0 - lb) * sg * nsg * common
            dl0 = _colsum(nsg * common) * lb * (1.0 - lb)
            dlb_ref[0:1, sl] += dl0
            dlb_ref[1:2, sl] -= dl0
            dhq_ref[:, sl] = dq_tot * _dsilu(hq, sq)
            dhv_ref[:, sl] = dv_acc
            dS_ref[h] = dST * jnp.exp(bl) + _dot(do_h, qt, TN)

    rev = lambda c: pl.BlockSpec((CHUNK, HALF), lambda i, c=c: (N - 1 - i, c // 4))
    rev0 = pl.BlockSpec((CHUNK, HALF), lambda i: (N - 1 - i, 0))
    return _hosted_call(
        body, comm, name="hgrn2_bwd", grid=(N,),
        in_specs=[rev(C_HQ), rev(C_HF), rev(C_HV), _full(lb_logits.shape), rev0,
                  pl.BlockSpec((1, HEADS, 128, 128), lambda i: (N - 1 - i, 0, 0, 0))],
        out_specs=[rev0, rev0, rev0, _full((2, HALF))],
        out_shape=[_sds((T, HALF))] * 3 + [_sds((2, HALF))],
        scratch_shapes=[pltpu.VMEM((HEADS, 128, 128), F32), pltpu.VMEM((CHUNK, 128), F32),
                        pltpu.VMEM((CHUNK, 128), F32)],
        args=(u, u, u, lb_logits, do, states))


def _mlstm_bwd(qkc, u, dh, cst, nst, mst):
    T = u.shape[0]
    N = T // CHUNK

    def body(qk_ref, v_ref, g_ref, dh_ref, cst_ref, nst_ref, mst_ref, dqk_ref, dv_ref, dgt_ref,
             dC_ref, dn_ref):
        i = pl.program_id(0)

        @pl.when(i == 0)
        def _():
            dC_ref[...] = jnp.zeros_like(dC_ref)
            dn_ref[...] = jnp.zeros_like(dn_ref)

        consts = _ml_consts()
        lane = consts[2]
        triu = _tri(CHUNK, False)
        ones = jnp.ones((CHUNK, 128), F32)
        rowc = _iota((CHUNK, 1), 0)
        gates = g_ref[...]
        dg_all = jnp.zeros((CHUNK, 128), F32)
        di_all = jnp.zeros((CHUNK, 128), F32)
        for h in range(HEADS):
            q = qk_ref[:, ML_DQK * h:ML_DQK * (h + 1)] * ML_SCALE
            k = qk_ref[:, 256 + ML_DQK * h:256 + ML_DQK * (h + 1)]
            v = v_ref[:, 128 * h:128 * (h + 1)]
            dh_h = dh_ref[:, 128 * h:128 * (h + 1)]
            C, nrow, mprev = cst_ref[0, h], nst_ref[0, h], mst_ref[0, h]
            dC, dn = dC_ref[h], dn_ref[h]
            r = _ml_chunk(q, k, v, gates, h, C, nrow, mprev, consts)
            wn, ws, wo, s_mat = r["wn"], r["ws"], r["wo"], r["s"]
            inv = 1.0 / r["nrm"]
            dnum = dh_h * inv
            hh = r["num"] * inv
            dnrm = -_rowsum(dh_h * hh) * inv
            dden = jnp.where(jnp.abs(r["den"]) > r["floor"], dnrm * jnp.sign(r["den"]), 0.0)
            ds = _dot(dnum, v, NT) + dden
            dqk = ds * r["wi"]
            dd = ds * s_mat
            dq = _dot(dqk, k) + wn * (_dot(dnum, C, NT) + dden * nrow)
            dk_st = ws * (_dot(v, dC, NT) + dn)
            dk = _dot(dqk, q, TN) + dk_st
            dv_ref[:, 128 * h:128 * (h + 1)] = _dot(s_mat, dnum, TN) + ws * _dot(k, dC)
            dqk_ref[:, ML_DQK * h:ML_DQK * (h + 1)] = dq * ML_SCALE
            dqk_ref[:, 256 + ML_DQK * h:256 + ML_DQK * (h + 1)] = dk
            dC_ref[h] = wo * dC + _dot(q * wn, dnum, TN)
            dn_ref[h] = wo * dn + _colsum(q * (wn * dden))
            e_col = wn * (_rowsum(dnum * r["qc"]) + dden * r["qn"])
            c_col = _rowsum(k * dk_st)
            z = wo * (jnp.sum(dC * C, keepdims=True) + jnp.sum(dn * nrow, keepdims=True))
            dd_cols = _dotx(dd, ones, TN)[:, 0:1]
            dg_col = _rowsum(dd) - dd_cols + e_col - c_col
            dg_col = dg_col + jnp.where(rowc == CHUNK - 1, jnp.sum(c_col, keepdims=True) + z, 0.0)
            di_col = dd_cols + c_col
            dg_all = dg_all + jnp.where(lane == 4 + h, dg_col, 0.0)
            di_all = di_all + jnp.where(lane == h, di_col, 0.0)
        dlf = _dotx(triu, dg_all)
        dgt_ref[...] = di_all + dlf * _sig(-gates)

    rev0 = pl.BlockSpec((CHUNK, HALF), lambda i: (N - 1 - i, 0))
    st = lambda a, b: pl.BlockSpec((1, HEADS, a, b), lambda i: (N - 1 - i, 0, 0, 0))
    return pl.pallas_call(
        body, name="mlstm_bwd", grid=(N,),
        in_specs=[rev0, pl.BlockSpec((CHUNK, HALF), lambda i: (N - 1 - i, C_MV // 4)),
                  pl.BlockSpec((CHUNK, 128), lambda i: (N - 1 - i, C_GATES)), rev0,
                  st(ML_DQK, 128), st(1, ML_DQK), st(1, 1)],
        out_specs=[rev0, rev0, pl.BlockSpec((CHUNK, 128), lambda i: (N - 1 - i, 0))],
        out_shape=[_sds((T, HALF)), _sds((T, HALF)), _sds((T, 128))],
        scratch_shapes=[pltpu.VMEM((HEADS, ML_DQK, 128), F32), pltpu.VMEM((HEADS, 1, ML_DQK), F32)],
        compiler_params=_cp("arbitrary"))(qkc, u, u, dh, cst, nst, mst)


def _conv_bwd(u, pre, dqkc, cw, tm):
    T = u.shape[0]
    hb = tm // 8
    nb = T // 8

    def body(x_ref, xh_ref, pre_ref, preh_ref, d_ref, dh_ref, w_ref, dx_ref, dw_ref, db_ref):
        i = pl.program_id(0)

        @pl.when(i == 0)
        def _():
            dw_ref[...] = jnp.zeros_like(dw_ref)
            db_ref[...] = jnp.zeros_like(db_ref)

        def dpre_of(pre, d):
            s = _sig(pre)
            return d * _dsilu(pre, s)

        rowi = _iota((8, HALF), 0)
        dpre = dpre_of(pre_ref[...], d_ref[...])
        dpre_next = jnp.where(i < pl.num_programs(0) - 1, dpre_of(preh_ref[...], dh_ref[...]), 0.0)
        x = x_ref[...]
        xprev = jnp.where(i > 0, xh_ref[...], 0.0)
        dx = dpre * w_ref[3:4, :]
        db_ref[...] += _colsum(dpre)
        dws = [None] * 4
        dws[3] = _colsum(dpre * x)
        for j in (1, 2, 3):
            dx = dx + _shift_rows_up(dpre, dpre_next, j, rowi) * w_ref[3 - j:4 - j, :]
            dws[3 - j] = _colsum(dpre * _shift_rows(x, xprev, j, rowi))
        dx_ref[...] = dx
        dw_ref[...] += jnp.concatenate(dws, axis=0)

    cur = lambda blk: pl.BlockSpec((tm, HALF), lambda i, blk=blk: (i, blk))
    nxt = pl.BlockSpec((8, HALF), lambda i: (jnp.minimum((i + 1) * hb, nb - 1), 0))
    return pl.pallas_call(
        body, name="conv_bwd", grid=(T // tm,),
        in_specs=[cur(C_MQK // 4), pl.BlockSpec((8, HALF), lambda i: (jnp.maximum(i * hb - 1, 0), C_MQK // 4)),
                  cur(0), nxt, cur(0), nxt, _full(cw.shape)],
        out_specs=[cur(0), _full((4, HALF)), _full((1, HALF))],
        out_shape=[_sds((T, HALF)), _sds((4, HALF)), _sds((1, HALF))],
        compiler_params=_cp("arbitrary"))(u, u, pre, pre, dqkc, dqkc, cw)


def _in_proj_bwd(dz1, du, w, tm):
    T = dz1.shape[0]

    def body(dz_ref, du_ref, w_ref, dx_ref, db_ref):
        i = pl.program_id(0)

        @pl.when(i == 0)
        def _():
            db_ref[...] = jnp.zeros_like(db_ref)

        du_t = du_ref[...]
        db_ref[...] += _colsum(du_t)
        dx_ref[...] = ALPHA * dz_ref[...] + _dot(du_t, w_ref[...], NT)

    return pl.pallas_call(
        body, name="in_proj_bwd", grid=(T // tm,),
        in_specs=[_row(tm, D_MODEL), _row(tm, PROJ_WP), _full(w.shape)],
        out_specs=[_row(tm, D_MODEL), _full((1, PROJ_WP))],
        out_shape=[_sds((T, D_MODEL)), _sds((1, PROJ_WP))],
        compiler_params=_cp("arbitrary"))(dz1, du, w)


def _wgrad(a, b, name, tm, tn, tk, out_dtype=F32):
    T, M = a.shape
    N = b.shape[1]
    tm, tn, tk = min(tm, M), min(tn, N), min(tk, T)
    nk = T // tk

    def body(a_ref, b_ref, o_ref, acc_ref):
        kk = pl.program_id(2)

        @pl.when(kk == 0)
        def _():
            acc_ref[...] = jnp.zeros_like(acc_ref)

        acc_ref[...] += _dot(a_ref[...], b_ref[...], TN)

        @pl.when(kk == nk - 1)
        def _():
            o_ref[...] = acc_ref[...].astype(out_dtype)

    return pl.pallas_call(
        body, name=name, grid=(M // tm, N // tn, nk),
        in_specs=[pl.BlockSpec((tk, tm), lambda i, j, kk: (kk, i)), pl.BlockSpec((tk, tn), lambda i, j, kk: (kk, j))],
        out_specs=pl.BlockSpec((tm, tn), lambda i, j, kk: (i, j)), out_shape=jax.ShapeDtypeStruct((M, N), out_dtype),
        scratch_shapes=[pltpu.VMEM((tm, tn), F32)],
        compiler_params=_cp("parallel", "parallel", "arbitrary"))(a, b)


GRAD_T = jnp.bfloat16
W_IN_S, FF_S, OUT_S, PP_S = PROJ_W // N_DEV, D_FF // N_DEV, D_MODEL // N_DEV, D_MODEL // N_DEV
LATE = ("w_ffn_gate", "w_ffn_up", "w_out", "w_ffn_down", "ple_w_gate", "ple_w_proj")


def _split_cols(a, n):
    return a.reshape(a.shape[0], N_DEV, n).transpose(1, 0, 2)


def _join_cols(a):
    return a.transpose(1, 0, 2).reshape(a.shape[1], -1)


def _late_full(parts):
    gate, up, w_out, down, pg, pp = parts
    return dict(w_ffn_gate=_join_cols(gate), w_ffn_up=_join_cols(up), w_out=w_out.reshape(D_MODEL, D_MODEL),
                w_ffn_down=down.reshape(D_FF, D_MODEL), ple_w_gate=pg.reshape(D_MODEL, D_MODEL), ple_w_proj=_join_cols(pp))


def _by_owner(g):
    return [_split_cols(g["w_ffn_gate"], FF_S), _split_cols(g["w_ffn_up"], FF_S),
            g["w_out"].reshape(N_DEV, OUT_S, D_MODEL), g["w_ffn_down"].reshape(N_DEV, FF_S, D_MODEL),
            g["ple_w_gate"].reshape(N_DEV, OUT_S, D_MODEL), _split_cols(g["ple_w_proj"], PP_S)]


def _step(x, p, tgt, w_in, b_in, lb_logits, conv_w, conv_b, g_hg, g_ml, ln1_g, ln1_b, ln2_g, ln2_b, bpg, late,
          distributed):
    T = x.shape[0]
    tm = min(256, T)
    u = _in_proj(x, w_in, b_in, tm)
    pre, qkc = _conv_fwd(u, conv_w, conv_b, tm)
    (o_hg, hg_states), got = _hgrn2_fwd2(u, lb_logits, _Comm("gather", late) if distributed else None)
    lw = _late_full(got) if distributed else late
    w_out, wg, wu, wd, wpg, wpp = (lw[n] for n in ("w_out", "w_ffn_gate", "w_ffn_up", "w_ffn_down", "ple_w_gate", "ple_w_proj"))
    h_ml, cst, nst, mst = _mlstm_fwd2(qkc, u)
    m_in, z1, x1 = _out_proj_ln(x, u, o_hg, h_ml, g_hg, g_ml, w_out, ln1_g, ln1_b, tm)
    z2, x2 = _ffn_ln(x1, wg, wu, wd, ln2_g, ln2_b, tm)
    de, dgp, dz2, loss_vec, d_bpg, d_ln2g, d_ln2b = _ple_loss_ln2_bwd(x2, z2, p, tgt, wpg, bpg, wpp, ln2_g, ln2_b, tm)
    hh, da, dbb, dz1, d_ln1g, d_ln1b = _ffn_bwd_ln1_bwd(x1, z1, dz2, wg, wu, wd, ln1_g, ln1_b, tm)
    wk = dict(tm=512, tn=D_MODEL, tk=512, out_dtype=GRAD_T)
    big = dict(
        w_ffn_gate=_wgrad(x1, da, "wgrad_ffn_gate", 512, D_FF, 512, GRAD_T),
        w_ffn_up=_wgrad(x1, dbb, "wgrad_ffn_up", 512, D_FF, 512, GRAD_T),
        w_out=_wgrad(m_in, dz1, "wgrad_w_out", **wk),
        w_ffn_down=_wgrad(hh, dz2, "wgrad_ffn_down", D_FF, D_MODEL, 512, GRAD_T),
        ple_w_gate=_wgrad(x2, dgp, "wgrad_ple_gate", **wk),
        ple_w_proj=_wgrad(p, de, "wgrad_ple_proj", **wk))
    d_ohg, d_hml, d_hgate, d_mo, d_ghg, d_gml = _out_proj_bwd(dz1, u, o_hg, h_ml, g_hg, g_ml, w_out, tm)
    (d_hq, d_hf, d_hv, d_lb), received = _hgrn2_bwd2(
        u, lb_logits, d_ohg, hg_states, _Comm("scatter", _by_owner(big)) if distributed else None)
    if distributed:
        big = dict(zip(LATE, received))
    d_qkc, d_mv, d_gates = _mlstm_bwd2(qkc, u, d_hml, cst, nst, mst)
    d_mqk, d_convw, d_convb = _conv_bwd(u, pre, d_qkc, conv_w, tm)
    du = jnp.concatenate([d_hq, d_hf, d_hv, d_hgate, d_mqk, d_mv, d_mo, d_gates], axis=1)
    dx, d_bin = _in_proj_bwd(dz1, du, w_in, tm)
    big["w_in"] = _wgrad(x, du, "wgrad_w_in", 512, PROJ_WP, 512, GRAD_T)
    small = dict(b_in=d_bin, hg_lb_logits=d_lb, ml_conv_w=d_convw, ml_conv_b=d_convb, hg_norm_g=d_ghg, ml_norm_g=d_gml,
                 ln1_g=d_ln1g, ln1_b=d_ln1b, ln2_g=d_ln2g, ln2_b=d_ln2b, ple_b_gate=d_bpg)
    return loss_vec, dx, big, small


BIG = ("w_in",) + LATE
SMALL = dict(b_in=(8, 32, PROJ_WP), hg_lb_logits=(40, 8, 1024), ml_conv_w=(48, 16, 2048), ml_conv_b=(64, 8, 512),
             hg_norm_g=(72, 8, 512), ml_norm_g=(80, 8, 512), ln1_g=(88, 8, 1024), ln1_b=(96, 8, 1024),
             ln2_g=(104, 8, 1024), ln2_b=(112, 8, 1024), ple_b_gate=(120, 8, 1024))
SM_ROWS = 128


def _padc(a, n):
    return jnp.pad(a, [(0, 0)] * (a.ndim - 1) + [(0, n - a.shape[-1])])


def _pack_small(d, loss_vec=None):
    first = jnp.zeros((8, 128), F32) if loss_vec is None else loss_vec.reshape(8, 128)
    parts = [first]
    for name, (_, rows, n) in SMALL.items():
        parts.append(jnp.pad(d[name].reshape(-1), (0, rows * 128 - n)).reshape(rows, 128))
    return jnp.concatenate(parts, axis=0)


def _unpack_small(slab, shapes):
    return {name: slab[r0:r0 + rows].reshape(-1)[:n].reshape(shapes[name]) for name, (r0, rows, n) in SMALL.items()}


def _gather_two_level(blocks, name):
    n = len(blocks)

    def body(*refs):
        x_refs, out_refs = refs[:n], refs[n:2 * n]
        send_sems, recv_sems, local_sems = refs[2 * n:]
        x, y, c = lax.axis_index("x"), lax.axis_index("y"), lax.axis_index("c")
        me, sibling = (x, y, c), (x, y, 1 - c)
        chips = [(1 - x, y), (x, 1 - y), (1 - x, 1 - y)]

        def copy(i, k, block, to, own=False):
            slab = out_refs[i].at[4 * block[0] + 2 * block[1] + block[2]]
            return pltpu.make_async_remote_copy(
                src_ref=x_refs[i] if own else slab, dst_ref=slab, send_sem=send_sems.at[7 * i + k],
                recv_sem=recv_sems.at[7 * i + k], device_id=to, device_id_type=MESH)

        mine = [pltpu.make_async_copy(x_refs[i], out_refs[i].at[4 * x + 2 * y + c], local_sems.at[i]) for i in range(n)]
        for cp in mine:
            cp.start()
        first = [copy(i, 0, me, sibling, own=True) for i in range(n)]
        first += [copy(i, 1 + j, me, (*chip, c), own=True) for j, chip in enumerate(chips) for i in range(n)]
        for cp in first:
            cp.start()
        passed = []
        for j, chip in enumerate(chips):
            for i in range(n):
                copy(i, 1 + j, (*chip, c), me).wait_recv()
                passed.append(copy(i, 4 + j, (*chip, c), sibling))
                passed[-1].start()
        for i in range(n):
            copy(i, 0, sibling, me).wait_recv()
            for j, chip in enumerate(chips):
                copy(i, 4 + j, (*chip, 1 - c), me).wait_recv()
        for cp in first + passed:
            cp.wait_send()
        for cp in mine:
            cp.wait()

    return pl.pallas_call(
        body, name=name, out_shape=[jax.ShapeDtypeStruct((N_DEV,) + b.shape, b.dtype) for b in blocks],
        in_specs=[ANY] * n, out_specs=[ANY] * n,
        scratch_shapes=[pltpu.SemaphoreType.DMA((7 * n,)), pltpu.SemaphoreType.DMA((7 * n,)),
                        pltpu.SemaphoreType.DMA((n,))])(*blocks)


def _adamw(w, g, m, v):
    m = B1 * m + (1.0 - B1) * g
    v = B2 * v + (1.0 - B2) * jnp.square(g)
    m_hat = m / (1.0 - B1 ** STEP)
    v_hat = v / (1.0 - B2 ** STEP)
    return -LR * (m_hat / (jnp.sqrt(v_hat) + EPS) + WD * w), m, v


def _sum_slabs(ref):
    g = ref[0].astype(F32)
    for j in range(1, N_DEV):
        g = g + ref[j].astype(F32)
    return g


def _adamw_matrix(rb, w, m, v, name):
    R, C = w.shape
    tr = 256 if R % 256 == 0 else R

    def body(rb_ref, w_ref, m_ref, v_ref, g_ref, d_ref, m2_ref, v2_ref):
        g = _sum_slabs(rb_ref)
        g_ref[...] = g
        d_ref[...], m2_ref[...], v2_ref[...] = _adamw(w_ref[...], g, m_ref[...], v_ref[...])

    blk = pl.BlockSpec((tr, C), lambda i: (i, 0))
    return pl.pallas_call(
        body, name=name, grid=(R // tr,),
        in_specs=[pl.BlockSpec((N_DEV, tr, C), lambda i: (0, i, 0)), blk, blk, blk],
        out_specs=[blk] * 4, out_shape=[_sds((R, C))] * 4, compiler_params=_cp("parallel"))(rb, w, m, v)


def _adamw_small(sg, w, m, v):
    def body(sg_ref, w_ref, m_ref, v_ref, loss_ref, g_ref, d_ref, m2_ref, v2_ref):
        g = _sum_slabs(sg_ref)
        loss_ref[...] = (0.5 / D_MODEL) * jnp.sum(g[0:8], keepdims=True)
        g_ref[...] = g
        d_ref[...], m2_ref[...], v2_ref[...] = _adamw(w_ref[...], g, m_ref[...], v_ref[...])

    return pl.pallas_call(
        body, name="adamw_small", out_shape=[_sds((1, 1))] + [_sds((SM_ROWS, 128))] * 4)(sg, w, m, v)


WEIGHTS = ("w_in", "b_in", "hg_lb_logits", "ml_conv_w", "ml_conv_b", "hg_norm_g", "ml_norm_g", "w_out", "ln1_g", "ln1_b",
           "w_ffn_gate", "w_ffn_up", "w_ffn_down", "ln2_g", "ln2_b", "ple_w_proj", "ple_w_gate", "ple_b_gate")
CONV_S = HALF // N_DEV


def kernel(x, p, w_in, b_in, hg_lb_logits, ml_conv_w, ml_conv_b, hg_norm_g, ml_norm_g, w_out, ln1_g, ln1_b, w_ffn_gate, w_ffn_up, w_ffn_down, ln2_g, ln2_b, ple_w_proj, ple_w_gate, ple_b_gate, loss_target, m_w_in, m_b_in, m_hg_lb_logits, m_ml_conv_w, m_ml_conv_b, m_hg_norm_g, m_ml_norm_g, m_w_out, m_ln1_g, m_ln1_b, m_w_ffn_gate, m_w_ffn_up, m_w_ffn_down, m_ln2_g, m_ln2_b, m_ple_w_proj, m_ple_w_gate, m_ple_b_gate, v_w_in, v_b_in, v_hg_lb_logits, v_ml_conv_w, v_ml_conv_b, v_hg_norm_g, v_ml_norm_g, v_w_out, v_ln1_g, v_ln1_b, v_w_ffn_gate, v_w_ffn_up, v_w_ffn_down, v_ln2_g, v_ln2_b, v_ple_w_proj, v_ple_w_gate, v_ple_b_gate):
    args = locals()
    me = 4 * lax.axis_index("x") + 2 * lax.axis_index("y") + lax.axis_index("c")
    shapes = {n: args[n].shape for n in WEIGHTS}
    drop = lambda n, a: a[0] if n in BIG or n == "ml_conv_w" else a
    W = {n: drop(n, args[n]) for n in WEIGHTS}
    M = {n: drop(n, args["m_" + n]) for n in WEIGHTS}
    V = {n: drop(n, args["v_" + n]) for n in WEIGHTS}

    g_in, g_conv = _gather_two_level(
        [W["w_in"].astype(_MXU), jnp.pad(W["ml_conv_w"], ((0, 4), (0, 128 - CONV_S)))], "gather_w_in")
    w_in_full = _padc(_join_cols(g_in), PROJ_WP)
    conv_full = _join_cols(g_conv[:, :4, :CONV_S])

    loss_vec, dx, big, small = _step(
        x[0], p[0, 0], loss_target[0], w_in_full, _padc(b_in, PROJ_WP), hg_lb_logits, conv_full, ml_conv_b,
        hg_norm_g, ml_norm_g, ln1_g, ln1_b, ln2_g, ln2_b, ple_b_gate, [W[n].astype(_MXU) for n in LATE], True)

    (rb_in,), (sg,) = _comm_call([_Comm("scatter", [_split_cols(big["w_in"][:, :PROJ_W], W_IN_S)]),
                                  _Comm("gather", [_pack_small(small, loss_vec)])], "exchange_w_in_small")
    big["w_in"] = rb_in

    upd = {n: _adamw_matrix(big[n], W[n], M[n], V[n], "adamw_" + n) for n in BIG}
    place = lambda d: {**d, "b_in": _padc(d["b_in"], PROJ_WP),
                       "ml_conv_w": lax.dynamic_update_slice(jnp.zeros((4, HALF), F32), d["ml_conv_w"], (0, me * CONV_S))}
    loss, *small_upd = _adamw_small(sg, _pack_small(place(W)), _pack_small(place(M)), _pack_small(place(V)))

    outs = []
    sm_shapes = {**{n: shapes[n] for n in SMALL}, "b_in": (1, PROJ_WP), "ml_conv_w": (4, HALF)}
    for kind in range(4):
        smalls = _unpack_small(small_upd[kind], sm_shapes)
        smalls["b_in"] = smalls["b_in"][:, :PROJ_W]
        smalls["ml_conv_w"] = lax.dynamic_slice(smalls["ml_conv_w"], (0, me * CONV_S), (4, CONV_S))
        for n in WEIGHTS:
            outs.append((upd[n][kind] if n in BIG else smalls[n]).reshape(shapes[n]))
    return (loss.reshape(()), dx.reshape(x.shape), *outs)
```

```python
import functools
import math

import jax
import jax.numpy as jnp
from jax import lax
from jax.experimental import pallas as pl
from jax.experimental.pallas import tpu as pltpu

F32 = jnp.float32
_MXU = jnp.bfloat16

D_MODEL = 1024
CHUNK = 64
SUB = 16
PLE_DIM = 256
HEADS = 4
HG_DK = 128
ML_DQK = 64
HALF = 512
D_FF = 2816
PROJ_W = 3592
PROJ_WP = 3712
ALPHA = float(2 ** 0.25)
LN_EPS = 1e-5
RMS_EPS = 1e-6
ML_SCALE = ML_DQK ** -0.5
N_DEV = 8
LR, B1, B2, EPS, WD, STEP = 0.001, 0.9, 0.999, 1e-08, 0.01, 10
NEG = -1e30

C_HQ, C_HF, C_HV, C_HGATE, C_MQK, C_MV, C_MO, C_GATES = 0, 4, 8, 12, 16, 20, 24, 28

VMEM_LIMIT = 52 * 1024 * 1024

NN = (((1,), (0,)), ((), ()))
NT = (((1,), (1,)), ((), ()))
TN = (((0,), (0,)), ((), ()))


def _dot(a, b, dims=NN):
    return lax.dot_general(a.astype(_MXU), b.astype(_MXU), dims, preferred_element_type=F32)


def _dotx(a, b, dims=NN):
    return lax.dot_general(a, b, dims, precision=lax.Precision.HIGHEST, preferred_element_type=F32)


def _sig(x):
    return jax.nn.sigmoid(x)


def _cp(*sem):
    return pltpu.CompilerParams(dimension_semantics=sem, vmem_limit_bytes=VMEM_LIMIT)


def _row(tm, c, blk=0):
    return pl.BlockSpec((tm, c), lambda i, blk=blk: (i, blk))


def _full(shape):
    nd = len(shape)
    return pl.BlockSpec(tuple(shape), lambda i, nd=nd: (0,) * nd)


def _sds(shape):
    return jax.ShapeDtypeStruct(tuple(shape), F32)


def _iota(shape, axis):
    return lax.broadcasted_iota(jnp.int32, shape, axis)


def _colsum(x):
    return jnp.sum(x, axis=0, keepdims=True)


def _rowsum(x):
    return jnp.sum(x, axis=1, keepdims=True)


def _ln_fwd(z, g, b):
    mu = jnp.mean(z, axis=-1, keepdims=True)
    zc = z - mu
    var = jnp.mean(zc * zc, axis=-1, keepdims=True)
    rstd = lax.rsqrt(var + LN_EPS)
    xhat = zc * rstd
    return xhat * g + b, xhat, rstd


def _ln_bwd(dy, xhat, rstd, g):
    dxh = dy * g
    m1 = jnp.mean(dxh, axis=-1, keepdims=True)
    m2 = jnp.mean(dxh * xhat, axis=-1, keepdims=True)
    return rstd * (dxh - m1 - xhat * m2)


def _dsilu(x, s):
    return s * (1.0 + x * (1.0 - s))


MESH = pl.DeviceIdType.MESH
ANY = pl.BlockSpec(memory_space=pl.ANY)


def _flip(v, bit):
    return 1 - v if bit else v


class _Comm:
    def __init__(self, kind, srcs):
        self.kind, self.srcs, self.n = kind, list(srcs), len(srcs)

    def out_shape(self):
        lead = (N_DEV,) if self.kind == "gather" else ()
        return [jax.ShapeDtypeStruct(lead + s.shape, s.dtype) for s in self.srcs]

    def scratch(self):
        return [pltpu.SemaphoreType.DMA((7 * self.n,)), pltpu.SemaphoreType.DMA((7 * self.n,)),
                pltpu.SemaphoreType.DMA((self.n,))]

    def copies(self, srcs, dsts, send_sems, recv_sems, local_sems):
        x, y, c = lax.axis_index("x"), lax.axis_index("y"), lax.axis_index("c")
        me = 4 * x + 2 * y + c
        pick = (lambda s, j: s) if self.kind == "gather" else (lambda s, j: s.at[j])
        out = []
        for i, (s, d) in enumerate(zip(srcs, dsts)):
            out.append(pltpu.make_async_copy(pick(s, me), d.at[me], local_sems.at[i]))
            for k in range(1, N_DEV):
                px, py, pc = _flip(x, k & 4), _flip(y, k & 2), _flip(c, k & 1)
                out.append(pltpu.make_async_remote_copy(
                    src_ref=pick(s, 4 * px + 2 * py + pc), dst_ref=d.at[me], send_sem=send_sems.at[7 * i + k - 1],
                    recv_sem=recv_sems.at[7 * i + k - 1], device_id=(px, py, pc), device_id_type=MESH))
        return out


def _comm_call(comms, name):
    counts = [cm.n for cm in comms]

    def body(*refs):
        total = sum(counts)
        srcs, dsts, sems = refs[:total], refs[total:2 * total], refs[2 * total:]
        copies, o = [], 0
        for j, cm in enumerate(comms):
            copies += cm.copies(srcs[o:o + cm.n], dsts[o:o + cm.n], *sems[3 * j:3 * j + 3])
            o += cm.n
        for cp in copies:
            cp.start()
        for cp in copies:
            cp.wait()

    total = sum(counts)
    res = pl.pallas_call(
        body, name=name, in_specs=[ANY] * total, out_specs=[ANY] * total,
        out_shape=[s for cm in comms for s in cm.out_shape()],
        scratch_shapes=[s for cm in comms for s in cm.scratch()])(*[a for cm in comms for a in cm.srcs])
    out, o = [], 0
    for cm in comms:
        out.append(res[o:o + cm.n])
        o += cm.n
    return out


def _hosted_call(body, comms, *, name, grid, in_specs, out_specs, out_shape, scratch_shapes, args):
    comms = list(comms or [])
    if not comms:
        res = pl.pallas_call(body, name=name, grid=grid, in_specs=in_specs, out_specs=out_specs, out_shape=out_shape,
                             scratch_shapes=scratch_shapes, compiler_params=_cp("arbitrary"))(*args)
        return list(res), []
    n_in, n_out, n_sc, nc = len(in_specs), len(out_specs), len(scratch_shapes), sum(cm.n for cm in comms)
    last = grid[0] - 1

    def hosted(*refs):
        ins, csrc = refs[:n_in], refs[n_in:n_in + nc]
        o0 = n_in + nc
        outs, cdst = refs[o0:o0 + n_out], refs[o0 + n_out:o0 + n_out + nc]
        s0 = o0 + n_out + nc
        scr, sems = refs[s0:s0 + n_sc], refs[s0 + n_sc:]

        def copies():
            out, o = [], 0
            for j, cm in enumerate(comms):
                out += cm.copies(csrc[o:o + cm.n], cdst[o:o + cm.n], *sems[3 * j:3 * j + 3])
                o += cm.n
            return out

        i = pl.program_id(0)

        @pl.when(i == 0)
        def _():
            for cp in copies():
                cp.start()

        body(*ins, *outs, *scr)

        @pl.when(i == last)
        def _():
            for cp in copies():
                cp.wait()

    res = pl.pallas_call(
        hosted, name=name, grid=grid, in_specs=list(in_specs) + [ANY] * nc, out_specs=list(out_specs) + [ANY] * nc,
        out_shape=list(out_shape) + [s for cm in comms for s in cm.out_shape()],
        scratch_shapes=list(scratch_shapes) + [s for cm in comms for s in cm.scratch()],
        compiler_params=_cp("arbitrary"))(*args, *[a for cm in comms for a in cm.srcs])
    got, o = [], n_out
    for cm in comms:
        got.append(list(res[o:o + cm.n]))
        o += cm.n
    return list(res[:n_out]), got


def _in_proj(x, w, b, tm, comms=None):
    T = x.shape[0]

    def body(x_ref, w_ref, b_ref, o_ref):
        o_ref[...] = _dot(x_ref[...], w_ref[...]) + b_ref[...]

    (u,), got = _hosted_call(
        body, comms, name="in_proj", grid=(T // tm,),
        in_specs=[_row(tm, D_MODEL), _full(w.shape), _full(b.shape)],
        out_specs=[_row(tm, PROJ_WP)], out_shape=[_sds((T, PROJ_WP))], scratch_shapes=[], args=(x, w, b))
    return u, got


def _shift_rows(x, halo, j, rowi):
    r = pltpu.roll(x, j, 0)
    top = jnp.where(rowi < j, pltpu.roll(halo, j, 0), r[:8])
    return jnp.concatenate([top, r[8:]], axis=0)


def _shift_rows_up(x, halo, j, rowi):
    n = x.shape[0]
    r = pltpu.roll(x, n - j, 0)
    bot = jnp.where(rowi >= 8 - j, pltpu.roll(halo, 8 - j, 0), r[n - 8:])
    return jnp.concatenate([r[:n - 8], bot], axis=0)


def _conv_fwd(u, cw, cb, tm):
    T = u.shape[0]
    hb = tm // 8

    def body(x_ref, halo_ref, w_ref, b_ref, pre_ref, out_ref):
        i = pl.program_id(0)
        x = x_ref[...]
        halo = jnp.where(i > 0, halo_ref[...], 0.0)
        rowi = _iota((8, HALF), 0)
        acc = x * w_ref[3:4, :] + b_ref[...]
        for j in (1, 2, 3):
            acc = acc + _shift_rows(x, halo, j, rowi) * w_ref[3 - j:4 - j, :]
        pre_ref[...] = acc
        out_ref[...] = acc * _sig(acc)

    return pl.pallas_call(
        body, name="conv_fwd", grid=(T // tm,),
        in_specs=[pl.BlockSpec((tm, HALF), lambda i: (i, C_MQK // 4)),
                  pl.BlockSpec((8, HALF), lambda i: (jnp.maximum(i * hb - 1, 0), C_MQK // 4)),
                  _full(cw.shape), _full(cb.shape)],
        out_specs=[_row(tm, HALF), _row(tm, HALF)], out_shape=[_sds((T, HALF)), _sds((T, HALF))],
        compiler_params=_cp("parallel"))(u, u, cw, cb)


def _hg_gates(hq, hf, lb):
    sg = _sig(hf)
    nsg = _sig(-hf)
    f = lb + (1.0 - lb) * sg
    g = jnp.log(f)
    k = (1.0 - lb) * nsg
    sq = _sig(hq)
    return hq * sq, g, k, f, sg, nsg, sq


def _tri(n, lower):
    r, c = _iota((n, n), 0), _iota((n, n), 1)
    return jnp.where((r >= c) if lower else (c >= r), 1.0, 0.0).astype(F32)


def _hg_diag_tiles(b_sc, r0, rowi):
    bi = b_sc[r0:r0 + SUB, :]
    return [jnp.exp(jnp.where(rowi >= s, bi - b_sc[r0 + s:r0 + s + 1, :], NEG)) for s in range(SUB)]


def _hgrn2_fwd(u, lb_logits, comm=None):
    T = u.shape[0]
    N = T // CHUNK

    def body(hq_ref, hf_ref, hv_ref, lg_ref, o_ref, st_ref, S_ref, b_sc, k_sc):
        n = pl.program_id(0)

        @pl.when(n == 0)
        def _():
            S_ref[...] = jnp.zeros_like(S_ref)

        lb_all = _sig(lg_ref[0:1, :] - lg_ref[1:2, :])
        tril = _tri(CHUNK, True)
        ones = jnp.ones((128, 128), F32)
        rowi = _iota((SUB, 128), 0)
        for h in range(HEADS):
            sl = slice(128 * h, 128 * h + 128)
            q, g, k, _, _, _, _ = _hg_gates(hq_ref[:, sl], hf_ref[:, sl], lb_all[:, sl])
            b = _dotx(tril, g)
            b_sc[...] = b
            k_sc[...] = k
            ST = S_ref[h]
            st_ref[0, h] = ST
            o = _dot(q * jnp.exp(b), ST, NT)
            outs = []
            for i in range(CHUNK // SUB):
                r0 = SUB * i
                qi = q[r0:r0 + SUB]
                oi = o[r0:r0 + SUB]
                if i > 0:
                    r = b_sc[r0 - 1:r0, :]
                    qe = qi * jnp.exp(b[r0:r0 + SUB] - r)
                    ke = k[:r0] * jnp.exp(r - b[:r0])
                    oi = oi + _dot(_dot(qe, ke, NT), hv_ref[0:r0, sl])
                tiles = _hg_diag_tiles(b_sc, r0, rowi)
                ms = [qi * (k_sc[r0 + s:r0 + s + 1, :] * tiles[s]) for s in range(SUB)]
                R = _dot(jnp.concatenate(ms, axis=0), ones)
                for s in range(SUB):
                    oi = oi + R[SUB * s:SUB * s + SUB] * hv_ref[r0 + s:r0 + s + 1, sl]
                outs.append(oi)
            o_ref[:, sl] = jnp.concatenate(outs, axis=0)
            bl = b_sc[CHUNK - 1:CHUNK, :]
            S_ref[h] = ST * jnp.exp(bl) + _dot(hv_ref[:, sl], k * jnp.exp(bl - b), TN)

    blk = lambda c: pl.BlockSpec((CHUNK, HALF), lambda n, c=c: (n, c // 4))
    return _hosted_call(
        body, comm, name="hgrn2_fwd", grid=(N,),
        in_specs=[blk(C_HQ), blk(C_HF), blk(C_HV), _full(lb_logits.shape)],
        out_specs=[pl.BlockSpec((CHUNK, HALF), lambda n: (n, 0)),
                   pl.BlockSpec((1, HEADS, 128, 128), lambda n: (n, 0, 0, 0))],
        out_shape=[_sds((T, HALF)), _sds((N, HEADS, 128, 128))],
        scratch_shapes=[pltpu.VMEM((HEADS, 128, 128), F32), pltpu.VMEM((CHUNK, 128), F32),
                        pltpu.VMEM((CHUNK, 128), F32)],
        args=(u, u, u, lb_logits))


def _lane_col(x, c, lane):
    return _rowsum(jnp.where(lane == c, x, 0.0))


def _sub_row(x, r, sub):
    return _colsum(jnp.where(sub == r, x, 0.0))


def _ml_chunk(q, k, v, gates, h, C, nrow, mprev, consts):
    tril, onehot8, lane, sub8, causal = consts
    lf = jnp.minimum(gates, 0.0) - jnp.log(1.0 + jnp.exp(-jnp.abs(gates)))
    g_all = _dotx(tril, lf)
    g_rows = _dotx(onehot8, g_all, NT)
    i_rows = _dotx(onehot8, gates, NT)
    gcol = _lane_col(g_all, 4 + h, lane)
    icol = _lane_col(gates, h, lane)
    grow = _sub_row(g_rows, 4 + h, sub8)
    irow = _sub_row(i_rows, h, sub8)
    dmat = jnp.where(causal, gcol - grow + irow, NEG)
    m_inter = gcol + mprev
    m_t = jnp.maximum(m_inter, jnp.max(dmat, axis=1, keepdims=True))
    wi = jnp.exp(dmat - m_t)
    wn = jnp.exp(m_inter - m_t)
    s_mat = _dot(q, k, NT) * wi
    qc = _dot(q, C)
    qn = _rowsum(q * nrow)
    num = _dot(s_mat, v) + wn * qc
    den = _rowsum(s_mat) + wn * qn
    floor = jnp.exp(-m_t)
    nrm = jnp.maximum(jnp.abs(den), floor)
    gl = _sub_row(gcol, CHUNK - 1, _iota((CHUNK, 1), 0))
    a_row = gl - grow + irow
    m_new = jnp.maximum(gl + mprev, jnp.max(a_row, axis=1, keepdims=True))
    ws_col = jnp.exp(gl - gcol + icol - m_new)
    wo = jnp.exp(gl + mprev - m_new)
    return dict(wi=wi, wn=wn, s=s_mat, qc=qc, qn=qn, num=num, den=den, floor=floor, nrm=nrm,
                m_new=m_new, ws=ws_col, wo=wo)


def _ml_consts():
    lane = _iota((CHUNK, 128), 1)
    sub8 = _iota((8, CHUNK), 0)
    onehot8 = jnp.where(_iota((8, 128), 0) == _iota((8, 128), 1), 1.0, 0.0).astype(F32)
    causal = _iota((CHUNK, CHUNK), 0) >= _iota((CHUNK, CHUNK), 1)
    return _tri(CHUNK, True), onehot8, lane, sub8, causal


def _mlstm_fwd(qkc, u):
    T = u.shape[0]
    N = T // CHUNK

    def body(qk_ref, v_ref, g_ref, h_ref, cst_ref, nst_ref, mst_ref, C_ref, n_ref, m_ref):
        n = pl.program_id(0)

        @pl.when(n == 0)
        def _():
            C_ref[...] = jnp.zeros_like(C_ref)
            n_ref[...] = jnp.zeros_like(n_ref)
            m_ref[...] = jnp.zeros_like(m_ref)

        consts = _ml_consts()
        gates = g_ref[...]
        for h in range(HEADS):
            q = qk_ref[:, ML_DQK * h:ML_DQK * (h + 1)] * ML_SCALE
            k = qk_ref[:, 256 + ML_DQK * h:256 + ML_DQK * (h + 1)]
            v = v_ref[:, 128 * h:128 * (h + 1)]
            C, nrow, mprev = C_ref[h], n_ref[h], m_ref[h]
            cst_ref[0, h] = C
            nst_ref[0, h] = nrow
            mst_ref[0, h] = mprev
            r = _ml_chunk(q, k, v, gates, h, C, nrow, mprev, consts)
            h_ref[:, 128 * h:128 * (h + 1)] = r["num"] / r["nrm"]
            kw = k * r["ws"]
            C_ref[h] = r["wo"] * C + _dot(kw, v, TN)
            n_ref[h] = r["wo"] * nrow + _colsum(kw)
            m_ref[h] = r["m_new"]

    return pl.pallas_call(
        body, name="mlstm_fwd", grid=(N,),
        in_specs=[pl.BlockSpec((CHUNK, HALF), lambda n: (n, 0)),
                  pl.BlockSpec((CHUNK, HALF), lambda n: (n, C_MV // 4)),
                  pl.BlockSpec((CHUNK, 128), lambda n: (n, C_GATES))],
        out_specs=[pl.BlockSpec((CHUNK, HALF), lambda n: (n, 0)),
                   pl.BlockSpec((1, HEADS, ML_DQK, 128), lambda n: (n, 0, 0, 0)),
                   pl.BlockSpec((1, HEADS, 1, ML_DQK), lambda n: (n, 0, 0, 0)),
                   pl.BlockSpec((1, HEADS, 1, 1), lambda n: (n, 0, 0, 0))],
        out_shape=[_sds((T, HALF)), _sds((N, HEADS, ML_DQK, 128)), _sds((N, HEADS, 1, ML_DQK)),
                   _sds((N, HEADS, 1, 1))],
        scratch_shapes=[pltpu.VMEM((HEADS, ML_DQK, 128), F32), pltpu.VMEM((HEADS, 1, ML_DQK), F32),
                        pltpu.VMEM((HEADS, 1, 1), F32)],
        compiler_params=_cp("arbitrary"))(qkc, u, u)


GC = 4

BNT = (((2,), (2,)), ((0,), (0,)))
BNN = (((2,), (1,)), ((0,), (0,)))
BTN = (((1,), (1,)), ((0,), (0,)))


def _bdot(a, b, dims):
    return lax.dot_general(a.astype(_MXU), b.astype(_MXU), dims, preferred_element_type=F32)


def _bdotx(a, b, dims):
    return lax.dot_general(a, b, dims, precision=lax.Precision.HIGHEST, preferred_element_type=F32)


def _heads_to_batch(x, w):
    G = x.shape[0] // CHUNK
    x3 = x.reshape(G, CHUNK, HEADS * w)
    return jnp.stack([x3[:, :, w * h:w * (h + 1)] for h in range(HEADS)], axis=1).reshape(G * HEADS, CHUNK, w)


def _batch_to_heads(x3):
    B, _, w = x3.shape
    x4 = x3.reshape(B // HEADS, HEADS, CHUNK, w)
    return jnp.concatenate([x4[:, h] for h in range(HEADS)], axis=-1).reshape(B // HEADS * CHUNK, HEADS * w)


def _chunk_cumsum(x, rowmod, reverse=False):
    R = x.shape[0]
    for sh in (1, 2, 4, 8, 16, 32):
        if reverse:
            x = x + jnp.where(rowmod < CHUNK - sh, pltpu.roll(x, R - sh, 0), 0.0)
        else:
            x = x + jnp.where(rowmod >= sh, pltpu.roll(x, sh, 0), 0.0)
    return x


def _lanes_to_batch_cols(x, lane):
    G = x.shape[0] // CHUNK
    cols = [_lane_col(x, 4 + h, lane).reshape(G, CHUNK, 1) for h in range(HEADS)]
    return jnp.stack(cols, axis=1).reshape(G * HEADS, CHUNK, 1)


def _row_scalars(rows):
    lane = _iota((1, 128), 1)
    return jnp.stack([_rowsum(jnp.where(lane == 4 + h, r, 0.0)) for r in rows for h in range(HEADS)], axis=0)


def _ml_gates(gates, mprev_rows):
    R = gates.shape[0]
    G = R // CHUNK
    lane = _iota((R, 128), 1)
    rowmod = _iota((R, 128), 0) & (CHUNK - 1)
    lf = jnp.minimum(gates, 0.0) - jnp.log(1.0 + jnp.exp(-jnp.abs(gates)))
    g_all = _chunk_cumsum(lf, rowmod)
    x_all = pltpu.roll(gates, 4, 1) - g_all
    return g_all, x_all, lane, rowmod, G


def _ml_chunk_rows(g_all, x_all, mprev, g):
    gl = g_all[CHUNK * g + CHUNK - 8:CHUNK * (g + 1)]
    gl = _colsum(jnp.where(_iota((8, 128), 0) == 7, gl, 0.0))
    a = gl + x_all[CHUNK * g:CHUNK * (g + 1)]
    m_new = jnp.maximum(gl + mprev, jnp.max(a, axis=0, keepdims=True))
    return m_new, jnp.exp(gl + mprev - m_new), jnp.exp(a - m_new)


def _ml_batched(q3, k3, v3, g_all, x_all, lane, G, C3, n3, mprev3):
    gcol3 = _lanes_to_batch_cols(g_all, lane)
    onehot = jnp.where(_iota((G, 8, 128), 1) + 4 == _iota((G, 8, 128), 2), 1.0, 0.0).astype(F32)
    rows = _bdotx(onehot, x_all.reshape(G, CHUNK, 128), BNT)
    sub = _iota((G, 8, CHUNK), 1)
    row3 = jnp.stack([jnp.sum(jnp.where(sub == h, rows, 0.0), axis=1, keepdims=True) for h in range(HEADS)],
                     axis=1).reshape(G * HEADS, 1, CHUNK)
    causal = _iota((1, CHUNK, CHUNK), 1) >= _iota((1, CHUNK, CHUNK), 2)
    dmat = jnp.where(causal, gcol3 + row3, NEG)
    m_inter = gcol3 + mprev3
    m_t = jnp.maximum(m_inter, jnp.max(dmat, axis=2, keepdims=True))
    wi = jnp.exp(dmat - m_t)
    wn = jnp.exp(m_inter - m_t)
    s3 = _bdot(q3, k3, BNT) * wi
    qc = _bdot(q3, C3, BNN)
    qn = jnp.sum(q3 * n3, axis=2, keepdims=True)
    num = _bdot(s3, v3, BNN) + wn * qc
    den = jnp.sum(s3, axis=2, keepdims=True) + wn * qn
    floor = jnp.exp(-m_t)
    return dict(wi=wi, wn=wn, s=s3, qc=qc, qn=qn, num=num, den=den, floor=floor, nrm=jnp.maximum(jnp.abs(den), floor))


def _mlstm_fwd2(qkc, u, comms=None):
    T = u.shape[0]
    G = min(GC, T // CHUNK)
    R = G * CHUNK
    N = T // CHUNK

    def body(qk_ref, v_ref, g_ref, h_ref, cst_ref, nst_ref, mst_ref, C_ref, n_ref, m_ref):
        @pl.when(pl.program_id(0) == 0)
        def _():
            C_ref[...] = jnp.zeros_like(C_ref)
            n_ref[...] = jnp.zeros_like(n_ref)
            m_ref[...] = jnp.zeros_like(m_ref)

        g_all, x_all, lane, _, _ = _ml_gates(g_ref[...], None)
        m_row = m_ref[...]
        mprev_rows, wo_rows, ws_parts = [], [], []
        for g in range(G):
            mprev_rows.append(m_row)
            m_row, wo, ws = _ml_chunk_rows(g_all, x_all, m_row, g)
            wo_rows.append(wo)
            ws_parts.append(ws)
        m_ref[...] = m_row
        mst_ref[...] = jnp.stack(mprev_rows, axis=0)
        ws3 = _lanes_to_batch_cols(jnp.concatenate(ws_parts, axis=0), lane)
        wo4 = _row_scalars(wo_rows).reshape(G, HEADS, 1, 1)
        q3 = _heads_to_batch(qk_ref[:, :256] * ML_SCALE, ML_DQK)
        k3 = _heads_to_batch(qk_ref[:, 256:], ML_DQK)
        v3 = _heads_to_batch(v_ref[...], 128)
        kw = k3 * ws3
        cloc = _bdot(kw, v3, BTN).reshape(G, HEADS, ML_DQK, 128)
        nloc = jnp.sum(kw, axis=1, keepdims=True).reshape(G, HEADS, 1, ML_DQK)
        C, nn = C_ref[...], n_ref[...]
        cs, ns = [], []
        for g in range(G):
            cs.append(C)
            ns.append(nn)
            C = wo4[g] * C + cloc[g]
            nn = wo4[g] * nn + nloc[g]
        C_ref[...] = C
        n_ref[...] = nn
        c4, n4 = jnp.stack(cs, axis=0), jnp.stack(ns, axis=0)
        cst_ref[...] = c4
        nst_ref[...] = n4
        r = _ml_batched(q3, k3, v3, g_all, x_all, lane, G, c4.reshape(G * HEADS, ML_DQK, 128),
                        n4.reshape(G * HEADS, 1, ML_DQK), _row_scalars(mprev_rows))
        h_ref[...] = _batch_to_heads(r["num"] / r["nrm"])

    return _hosted_call(
        body, comms, name="mlstm_fwd", grid=(N // G,),
        in_specs=[pl.BlockSpec((R, HALF), lambda n: (n, 0)), pl.BlockSpec((R, HALF), lambda n: (n, C_MV // 4)),
                  pl.BlockSpec((R, 128), lambda n: (n, C_GATES))],
        out_specs=[pl.BlockSpec((R, HALF), lambda n: (n, 0)),
                   pl.BlockSpec((G, HEADS, ML_DQK, 128), lambda n: (n, 0, 0, 0)),
                   pl.BlockSpec((G, HEADS, 1, ML_DQK), lambda n: (n, 0, 0, 0)),
                   pl.BlockSpec((G, 1, 128), lambda n: (n, 0, 0))],
        out_shape=[_sds((T, HALF)), _sds((N, HEADS, ML_DQK, 128)), _sds((N, HEADS, 1, ML_DQK)), _sds((N, 1, 128))],
        scratch_shapes=[pltpu.VMEM((HEADS, ML_DQK, 128), F32), pltpu.VMEM((HEADS, 1, ML_DQK), F32),
                        pltpu.VMEM((1, 128), F32)],
        args=(qkc, u, u))


def _mlstm_bwd2(qkc, u, dh, cst, nst, mst):
    T = u.shape[0]
    G = min(GC, T // CHUNK)
    R = G * CHUNK
    NG = T // R

    def body(qk_ref, v_ref, g_ref, dh_ref, cst_ref, nst_ref, mst_ref, dqk_ref, dv_ref, dgt_ref, dC_ref, dn_ref):
        @pl.when(pl.program_id(0) == 0)
        def _():
            dC_ref[...] = jnp.zeros_like(dC_ref)
            dn_ref[...] = jnp.zeros_like(dn_ref)

        B = G * HEADS
        gates = g_ref[...]
        g_all, x_all, lane, rowmod, _ = _ml_gates(gates, None)
        mprev_rows = [mst_ref[g] for g in range(G)]
        wo_rows, ws_parts = [], []
        for g in range(G):
            _, wo, ws = _ml_chunk_rows(g_all, x_all, mprev_rows[g], g)
            wo_rows.append(wo)
            ws_parts.append(ws)
        ws3 = _lanes_to_batch_cols(jnp.concatenate(ws_parts, axis=0), lane)
        wo3 = _row_scalars(wo_rows)
        wo4 = wo3.reshape(G, HEADS, 1, 1)
        q3 = _heads_to_batch(qk_ref[:, :256] * ML_SCALE, ML_DQK)
        k3 = _heads_to_batch(qk_ref[:, 256:], ML_DQK)
        v3 = _heads_to_batch(v_ref[...], 128)
        dh3 = _heads_to_batch(dh_ref[...], 128)
        C3 = cst_ref[...].reshape(B, ML_DQK, 128)
        n3 = nst_ref[...].reshape(B, 1, ML_DQK)
        r = _ml_batched(q3, k3, v3, g_all, x_all, lane, G, C3, n3, _row_scalars(mprev_rows))
        wn, s3 = r["wn"], r["s"]
        inv = 1.0 / r["nrm"]
        dnum = dh3 * inv
        dnrm = -jnp.sum(dh3 * (r["num"] * inv), axis=2, keepdims=True) * inv
        dden = jnp.where(jnp.abs(r["den"]) > r["floor"], dnrm * jnp.sign(r["den"]), 0.0)
        ds = _bdot(dnum, v3, BNT) + dden
        dqk = ds * r["wi"]
        dd = ds * s3
        qw = q3 * wn
        dcloc = _bdot(qw, dnum, BTN).reshape(G, HEADS, ML_DQK, 128)
        dnloc = jnp.sum(qw * dden, axis=1, keepdims=True).reshape(G, HEADS, 1, ML_DQK)
        dC, dn = dC_ref[...], dn_ref[...]
        dcs, dns = [None] * G, [None] * G
        for g in reversed(range(G)):
            dcs[g], dns[g] = dC, dn
            dC = wo4[g] * dC + dcloc[g]
            dn = wo4[g] * dn + dnloc[g]
        dC_ref[...] = dC
        dn_ref[...] = dn
        dC3 = jnp.stack(dcs, axis=0).reshape(B, ML_DQK, 128)
        dn3 = jnp.stack(dns, axis=0).reshape(B, 1, ML_DQK)
        dk_st = ws3 * (_bdot(v3, dC3, BNT) + dn3)
        dq = _bdot(dqk, k3, BNN) + wn * (_bdot(dnum, C3, BNT) + dden * n3)
        dk = _bdot(dqk, q3, BTN) + dk_st
        dv = _bdot(s3, dnum, BTN) + ws3 * _bdot(k3, dC3, BNN)
        dv_ref[...] = _batch_to_heads(dv)
        dqk_ref[...] = jnp.concatenate([_batch_to_heads(dq * ML_SCALE), _batch_to_heads(dk)], axis=1)
        e_col = wn * (jnp.sum(dnum * r["qc"], axis=2, keepdims=True) + dden * r["qn"])
        c_col = jnp.sum(k3 * dk_st, axis=2, keepdims=True)
        z = wo3 * (jnp.sum(dC3 * C3, axis=(1, 2), keepdims=True) + jnp.sum(dn3 * n3, axis=(1, 2), keepdims=True))
        dd_cols = _bdotx(dd, jnp.ones((B, CHUNK, 128), F32), BTN)[:, :, 0:1]
        last = _iota((1, CHUNK, 1), 1) == CHUNK - 1
        dg3 = jnp.sum(dd, axis=2, keepdims=True) - dd_cols + e_col - c_col
        dg3 = dg3 + jnp.where(last, jnp.sum(c_col, axis=1, keepdims=True) + z, 0.0)
        di3 = dd_cols + c_col

        def to_lanes(x3, first):
            x4 = x3.reshape(G, HEADS, CHUNK, 1)
            return sum(jnp.where(lane == first + h, x4[:, h].reshape(R, 1), 0.0) for h in range(HEADS))

        dlf = _chunk_cumsum(to_lanes(dg3, 4), rowmod, reverse=True)
        dgt_ref[...] = to_lanes(di3, 0) + dlf * _sig(-gates)

    rev = lambda w, c: pl.BlockSpec((R, w), lambda i, c=c: (NG - 1 - i, c))
    st = lambda *s: pl.BlockSpec((G,) + s, lambda i: (NG - 1 - i,) + (0,) * len(s))
    return pl.pallas_call(
        body, name="mlstm_bwd", grid=(NG,),
        in_specs=[rev(HALF, 0), rev(HALF, C_MV // 4), rev(128, C_GATES), rev(HALF, 0),
                  st(HEADS, ML_DQK, 128), st(HEADS, 1, ML_DQK), st(1, 128)],
        out_specs=[rev(HALF, 0), rev(HALF, 0), rev(128, 0)],
        out_shape=[_sds((T, HALF)), _sds((T, HALF)), _sds((T, 128))],
        scratch_shapes=[pltpu.VMEM((HEADS, ML_DQK, 128), F32), pltpu.VMEM((HEADS, 1, ML_DQK), F32)],
        compiler_params=_cp("arbitrary"))(qkc, u, u, dh, cst, nst, mst)


def _hg_prep(hq_ref, hf_ref, lg_ref, b_sc, k_sc):
    R = hq_ref.shape[0]
    G = R // CHUNK
    lb = _sig(lg_ref[0:1, :] - lg_ref[1:2, :])
    hq = hq_ref[...]
    q, g, k, f, sg, nsg, sq = _hg_gates(hq, hf_ref[...], lb)
    rowmod = _iota((R, HALF), 0) & (CHUNK - 1)
    b = _chunk_cumsum(g, rowmod)
    last8 = _iota((8, HALF), 0) == 7
    bl_rows = [_colsum(jnp.where(last8, b[CHUNK * c + CHUNK - 8:CHUNK * (c + 1)], 0.0)) for c in range(G)]
    bl3 = jnp.stack([r[:, 128 * h:128 * (h + 1)] for r in bl_rows for h in range(HEADS)], axis=0)
    b3, k3 = _heads_to_batch(b, 128), _heads_to_batch(k, 128)
    b_sc[...] = b3
    k_sc[...] = k3
    return dict(G=G, lb=lb, hq=hq, f=f, sg=sg, nsg=nsg, sq=sq, rowmod=rowmod, q3=_heads_to_batch(q, 128), k3=k3, b3=b3,
                bl3=bl3)


def _hg_diag_tiles3(b_sc, b3, r0, rowi):
    bi = b3[:, r0:r0 + SUB]
    return [jnp.exp(jnp.where(rowi >= s, bi - b_sc[:, r0 + s:r0 + s + 1, :], NEG)) for s in range(SUB)]


def _hgrn2_fwd2(u, lb_logits, comm=None):
    T = u.shape[0]
    G = min(GC, T // CHUNK)
    R, B, N = G * CHUNK, G * HEADS, T // CHUNK

    def body(hq_ref, hf_ref, hv_ref, lg_ref, o_ref, st_ref, S_ref, b_sc, k_sc, v_sc):
        @pl.when(pl.program_id(0) == 0)
        def _():
            S_ref[...] = jnp.zeros_like(S_ref)

        pz = _hg_prep(hq_ref, hf_ref, lg_ref, b_sc, k_sc)
        q3, k3, b3, bl3 = pz["q3"], pz["k3"], pz["b3"], pz["bl3"]
        v3 = _heads_to_batch(hv_ref[...], 128)
        v_sc[...] = v3
        stloc = _bdot(v3, k3 * jnp.exp(bl3 - b3), BTN).reshape(G, HEADS, 128, 128)
        dec = jnp.exp(bl3).reshape(G, HEADS, 1, 128)
        ST = S_ref[...]
        sts = []
        for c in range(G):
            sts.append(ST)
            ST = ST * dec[c] + stloc[c]
        S_ref[...] = ST
        st4 = jnp.stack(sts, axis=0)
        st_ref[...] = st4
        o = _bdot(q3 * jnp.exp(b3), st4.reshape(B, 128, 128), BNT)
        ones = jnp.ones((128, 128), F32)
        rowi = _iota((1, SUB, 128), 1)
        outs = []
        for i in range(CHUNK // SUB):
            r0 = SUB * i
            qi = q3[:, r0:r0 + SUB]
            oi = o[:, r0:r0 + SUB]
            if i > 0:
                r = b_sc[:, r0 - 1:r0, :]
                qe = qi * jnp.exp(b3[:, r0:r0 + SUB] - r)
                ke = k3[:, :r0] * jnp.exp(r - b3[:, :r0])
                oi = oi + _bdot(_bdot(qe, ke, BNT), v3[:, :r0], BNN)
            tiles = _hg_diag_tiles3(b_sc, b3, r0, rowi)
            ms = [qi * (k_sc[:, r0 + s:r0 + s + 1, :] * tiles[s]) for s in range(SUB)]
            Rm = _dot(jnp.concatenate(ms, axis=1).reshape(B * SUB * SUB, 128), ones).reshape(B, SUB * SUB, 128)
            for s in range(SUB):
                oi = oi + Rm[:, SUB * s:SUB * s + SUB] * v_sc[:, r0 + s:r0 + s + 1, :]
            outs.append(oi)
        o_ref[...] = _batch_to_heads(jnp.concatenate(outs, axis=1))

    blk = lambda c: pl.BlockSpec((R, HALF), lambda n, c=c: (n, c // 4))
    return _hosted_call(
        body, comm, name="hgrn2_fwd", grid=(N // G,),
        in_specs=[blk(C_HQ), blk(C_HF), blk(C_HV), _full(lb_logits.shape)],
        out_specs=[pl.BlockSpec((R, HALF), lambda n: (n, 0)),
                   pl.BlockSpec((G, HEADS, 128, 128), lambda n: (n, 0, 0, 0))],
        out_shape=[_sds((T, HALF)), _sds((N, HEADS, 128, 128))],
        scratch_shapes=[pltpu.VMEM((HEADS, 128, 128), F32)] + [pltpu.VMEM((B, CHUNK, 128), F32)] * 3,
        args=(u, u, u, lb_logits))


def _hgrn2_bwd2(u, lb_logits, do, states, comm=None):
    T = u.shape[0]
    G = min(GC, T // CHUNK)
    R, B, NG = G * CHUNK, G * HEADS, T // (G * CHUNK)

    def body(hq_ref, hf_ref, hv_ref, lg_ref, do_ref, st_ref, dhq_ref, dhf_ref, dhv_ref, dlb_ref,
             dS_ref, b_sc, k_sc, v_sc):
        @pl.when(pl.program_id(0) == 0)
        def _():
            dS_ref[...] = jnp.zeros_like(dS_ref)
            dlb_ref[...] = jnp.zeros_like(dlb_ref)

        pz = _hg_prep(hq_ref, hf_ref, lg_ref, b_sc, k_sc)
        q3, k3, b3, bl3, lb = pz["q3"], pz["k3"], pz["b3"], pz["bl3"], pz["lb"]
        v3 = _heads_to_batch(hv_ref[...], 128)
        v_sc[...] = v3
        do3 = _heads_to_batch(do_ref[...], 128)
        st3 = st_ref[...].reshape(B, 128, 128)
        eb = jnp.exp(b3)
        ebl = jnp.exp(bl3 - b3)
        qt = q3 * eb
        kl = k3 * ebl
        dstloc = _bdot(do3, qt, BTN).reshape(G, HEADS, 128, 128)
        dec = jnp.exp(bl3).reshape(G, HEADS, 1, 128)
        dST = dS_ref[...]
        dsts = [None] * G
        for c in reversed(range(G)):
            dsts[c] = dST
            dST = dST * dec[c] + dstloc[c]
        dS_ref[...] = dST
        dst3 = jnp.stack(dsts, axis=0).reshape(B, 128, 128)
        dqt = _bdot(do3, st3, BNN)
        dkl = _bdot(v3, dst3, BNN)
        dv_acc = _bdot(kl, dst3, BNT)
        ones = jnp.ones((128, 128), F32)
        rowi = _iota((1, SUB, 128), 1)
        dq_parts = []
        dk_in = jnp.zeros((B, CHUNK, 128), F32)
        for i_s in range(CHUNK // SUB):
            r0 = SUB * i_s
            qi = q3[:, r0:r0 + SUB]
            doi = do3[:, r0:r0 + SUB]
            dqi = jnp.zeros((B, SUB, 128), F32)
            if i_s > 0:
                r = b_sc[:, r0 - 1:r0, :]
                eq = jnp.exp(b3[:, r0:r0 + SUB] - r)
                ek = jnp.exp(r - b3[:, :r0])
                qe = qi * eq
                ke = k3[:, :r0] * ek
                a_off = _bdot(qe, ke, BNT)
                p_off = _bdot(doi, v3[:, :r0], BNT)
                pad = jnp.zeros((B, CHUNK - r0, 128), F32)
                dv_acc = dv_acc + jnp.concatenate([_bdot(a_off, doi, BTN), pad], axis=1)
                dqi = dqi + _bdot(p_off, ke, BNN) * eq
                dk_in = dk_in + jnp.concatenate([_bdot(p_off, qe, BTN) * ek, pad], axis=1)
            tiles = _hg_diag_tiles3(b_sc, b3, r0, rowi)
            ms = [qi * (k_sc[:, r0 + s:r0 + s + 1, :] * tiles[s]) for s in range(SUB)]
            ps = [doi * v_sc[:, r0 + s:r0 + s + 1, :] for s in range(SUB)]
            n_r = 2 * SUB * SUB
            Rm = _dot(jnp.concatenate(ms + ps, axis=1).reshape(B * n_r, 128), ones).reshape(B, n_r, 128)
            dv_rows, dk_rows = [], []
            for s in range(SUB):
                a_s = Rm[:, SUB * s:SUB * s + SUB]
                p_s = Rm[:, SUB * (SUB + s):SUB * (SUB + s) + SUB]
                pt = p_s * tiles[s]
                dqi = dqi + pt * k_sc[:, r0 + s:r0 + s + 1, :]
                dk_rows.append(jnp.sum(pt * qi, axis=1, keepdims=True))
                dv_rows.append(jnp.sum(a_s * doi, axis=1, keepdims=True))
            lo = [jnp.zeros((B, r0, 128), F32)] * (r0 > 0)
            hi = [jnp.zeros((B, CHUNK - r0 - SUB, 128), F32)] * (r0 + SUB < CHUNK)
            dk_in = dk_in + jnp.concatenate(lo + dk_rows + hi, axis=1)
            dv_acc = dv_acc + jnp.concatenate(lo + dv_rows + hi, axis=1)
            dq_parts.append(dqi)
        dq_in = jnp.concatenate(dq_parts, axis=1)
        db = qt * dqt + q3 * dq_in - k3 * dk_in - kl * dkl
        last = jnp.sum(kl * dkl, axis=1, keepdims=True) + jnp.exp(bl3) * jnp.sum(st3 * dst3, axis=1, keepdims=True)
        db = db + jnp.where(_iota((1, CHUNK, 1), 1) == CHUNK - 1, last, 0.0)
        dg = _chunk_cumsum(_batch_to_heads(db), pz["rowmod"], reverse=True)
        dq_tot = _batch_to_heads(dqt * eb + dq_in)
        dk_tot = _batch_to_heads(dkl * ebl + dk_in)
        common = dg / pz["f"] - dk_tot
        dhf_ref[...] = (1.0 - lb) * pz["sg"] * pz["nsg"] * common
        dl0 = _colsum(pz["nsg"] * common) * lb * (1.0 - lb)
        dlb_ref[0:1, :] += dl0
        dlb_ref[1:2, :] -= dl0
        dhq_ref[...] = dq_tot * _dsilu(pz["hq"], pz["sq"])
        dhv_ref[...] = _batch_to_heads(dv_acc)

    rev = lambda c: pl.BlockSpec((R, HALF), lambda i, c=c: (NG - 1 - i, c // 4))
    rev0 = pl.BlockSpec((R, HALF), lambda i: (NG - 1 - i, 0))
    return _hosted_call(
        body, comm, name="hgrn2_bwd", grid=(NG,),
        in_specs=[rev(C_HQ), rev(C_HF), rev(C_HV), _full(lb_logits.shape), rev0,
                  pl.BlockSpec((G, HEADS, 128, 128), lambda i: (NG - 1 - i, 0, 0, 0))],
        out_specs=[rev0, rev0, rev0, _full((2, HALF))],
        out_shape=[_sds((T, HALF))] * 3 + [_sds((2, HALF))],
        scratch_shapes=[pltpu.VMEM((HEADS, 128, 128), F32)] + [pltpu.VMEM((B, CHUNK, 128), F32)] * 3,
        args=(u, u, u, lb_logits, do, states))


def _head_norm(o, g):
    rs_parts, r_parts = [], []
    for h in range(HEADS):
        oh = o[:, 128 * h:128 * (h + 1)]
        rs = lax.rsqrt(jnp.mean(oh * oh, axis=-1, keepdims=True) + RMS_EPS)
        rs_parts.append(rs)
        r_parts.append(oh * rs)
    return jnp.concatenate(r_parts, axis=1), rs_parts


def _mix_in(o_hg, h_ml, hgate, mo, g_hg, g_ml):
    r_hg, _ = _head_norm(o_hg, g_hg)
    r_ml, _ = _head_norm(h_ml, g_ml)
    a = r_hg * g_hg * (hgate * _sig(hgate))
    b = r_ml * g_ml * _sig(mo)
    return jnp.concatenate([a, b], axis=1)


def _out_proj_ln(x, u, o_hg, h_ml, g_hg, g_ml, w_out, ln_g, ln_b, tm):
    T = x.shape[0]

    def body(x_ref, hgate_ref, mo_ref, ohg_ref, hml_ref, ghg_ref, gml_ref, w_ref, g_ref, b_ref,
             m_ref, z_ref, x1_ref):
        m = _mix_in(ohg_ref[...], hml_ref[...], hgate_ref[...], mo_ref[...], ghg_ref[...], gml_ref[...])
        m_ref[...] = m
        z = ALPHA * x_ref[...] + _dot(m, w_ref[...])
        z_ref[...] = z
        x1_ref[...] = _ln_fwd(z, g_ref[...], b_ref[...])[0]

    return pl.pallas_call(
        body, name="out_proj_ln1", grid=(T // tm,),
        in_specs=[_row(tm, D_MODEL), _row(tm, HALF, C_HGATE // 4), _row(tm, HALF, C_MO // 4),
                  _row(tm, HALF), _row(tm, HALF), _full(g_hg.shape), _full(g_ml.shape),
                  _full(w_out.shape), _full(ln_g.shape), _full(ln_b.shape)],
        out_specs=[_row(tm, D_MODEL)] * 3, out_shape=[_sds((T, D_MODEL))] * 3,
        compiler_params=_cp("parallel"))(x, u, u, o_hg, h_ml, g_hg, g_ml, w_out, ln_g, ln_b)


def _ffn_ln(x1, wg, wu, wd, ln_g, ln_b, tm):
    T = x1.shape[0]

    def body(x_ref, wg_ref, wu_ref, wd_ref, g_ref, b_ref, z_ref, x2_ref):
        x = x_ref[...]
        a = _dot(x, wg_ref[...])
        hh = a * _sig(a) * _dot(x, wu_ref[...])
        z = ALPHA * x + _dot(hh, wd_ref[...])
        z_ref[...] = z
        x2_ref[...] = _ln_fwd(z, g_ref[...], b_ref[...])[0]

    return pl.pallas_call(
        body, name="ffn_ln2", grid=(T // tm,),
        in_specs=[_row(tm, D_MODEL), _full(wg.shape), _full(wu.shape), _full(wd.shape),
                  _full(ln_g.shape), _full(ln_b.shape)],
        out_specs=[_row(tm, D_MODEL)] * 2, out_shape=[_sds((T, D_MODEL))] * 2,
        compiler_params=_cp("parallel"))(x1, wg, wu, wd, ln_g, ln_b)


def _ple_loss_ln2_bwd(x2, z2, p, tgt, wpg, bpg, wpp, ln_g, ln_b, tm):
    T = x2.shape[0]

    def body(x2_ref, z_ref, p_ref, t_ref, wpg_ref, bpg_ref, wpp_ref, g_ref, b_ref,
             de_ref, dgp_ref, dz_ref, loss_ref, dbpg_ref, dg_ref, db_ref):
        i = pl.program_id(0)

        @pl.when(i == 0)
        def _():
            for r in (loss_ref, dbpg_ref, dg_ref, db_ref):
                r[...] = jnp.zeros_like(r)

        x2 = x2_ref[...]
        gate = _sig(_dot(x2, wpg_ref[...]) + bpg_ref[...])
        e = _dot(p_ref[...], wpp_ref[...])
        err = x2 + gate * e - t_ref[...]
        loss_ref[...] += _colsum(err * err)
        dy = err * (1.0 / D_MODEL)
        de_ref[...] = dy * gate
        dgp = dy * e * gate * (1.0 - gate)
        dgp_ref[...] = dgp
        dbpg_ref[...] += _colsum(dgp)
        dx2 = dy + _dot(dgp, wpg_ref[...], NT)
        _, xhat, rstd = _ln_fwd(z_ref[...], g_ref[...], b_ref[...])
        dg_ref[...] += _colsum(dx2 * xhat)
        db_ref[...] += _colsum(dx2)
        dz_ref[...] = _ln_bwd(dx2, xhat, rstd, g_ref[...])

    vec = _full((1, D_MODEL))
    return pl.pallas_call(
        body, name="ple_loss_ln2_bwd", grid=(T // tm,),
        in_specs=[_row(tm, D_MODEL), _row(tm, D_MODEL), _row(tm, PLE_DIM), _row(tm, D_MODEL),
                  _full(wpg.shape), vec, _full(wpp.shape), vec, vec],
        out_specs=[_row(tm, D_MODEL)] * 3 + [vec] * 4,
        out_shape=[_sds((T, D_MODEL))] * 3 + [_sds((1, D_MODEL))] * 4,
        compiler_params=_cp("arbitrary"))(x2, z2, p, tgt, wpg, bpg, wpp, ln_g, ln_b)


def _ffn_bwd_ln1_bwd(x1, z1, dz2, wg, wu, wd, ln_g, ln_b, tm, comms=None):
    T = x1.shape[0]

    def body(x_ref, z_ref, dz2_ref, wg_ref, wu_ref, wd_ref, g_ref, b_ref,
             h_ref, da_ref, dbb_ref, dz1_ref, dg_ref, db_ref):
        i = pl.program_id(0)

        @pl.when(i == 0)
        def _():
            dg_ref[...] = jnp.zeros_like(dg_ref)
            db_ref[...] = jnp.zeros_like(db_ref)

        x = x_ref[...]
        dz2 = dz2_ref[...]
        a = _dot(x, wg_ref[...])
        bb = _dot(x, wu_ref[...])
        sa = _sig(a)
        act = a * sa
        h_ref[...] = act * bb
        dh = _dot(dz2, wd_ref[...], NT)
        da = dh * bb * _dsilu(a, sa)
        dbb = dh * act
        da_ref[...] = da
        dbb_ref[...] = dbb
        dx1 = ALPHA * dz2 + _dot(da, wg_ref[...], NT) + _dot(dbb, wu_ref[...], NT)
        _, xhat, rstd = _ln_fwd(z_ref[...], g_ref[...], b_ref[...])
        dg_ref[...] += _colsum(dx1 * xhat)
        db_ref[...] += _colsum(dx1)
        dz1_ref[...] = _ln_bwd(dx1, xhat, rstd, g_ref[...])

    vec = _full((1, D_MODEL))
    return _hosted_call(
        body, comms, name="ffn_bwd_ln1_bwd", grid=(T // tm,),
        in_specs=[_row(tm, D_MODEL)] * 3 + [_full(wg.shape), _full(wu.shape), _full(wd.shape), vec, vec],
        out_specs=[_row(tm, D_FF)] * 3 + [_row(tm, D_MODEL), vec, vec],
        out_shape=[_sds((T, D_FF))] * 3 + [_sds((T, D_MODEL)), _sds((1, D_MODEL)), _sds((1, D_MODEL))],
        scratch_shapes=[], args=(x1, z1, dz2, wg, wu, wd, ln_g, ln_b))


def _out_proj_bwd(dz1, u, o_hg, h_ml, g_hg, g_ml, w_out, tm):
    T = dz1.shape[0]

    def body(dz_ref, hgate_ref, mo_ref, ohg_ref, hml_ref, ghg_ref, gml_ref, w_ref,
             dohg_ref, dhml_ref, dhgate_ref, dmo_ref, dghg_ref, dgml_ref):
        i = pl.program_id(0)

        @pl.when(i == 0)
        def _():
            dghg_ref[...] = jnp.zeros_like(dghg_ref)
            dgml_ref[...] = jnp.zeros_like(dgml_ref)

        dm = _dot(dz_ref[...], w_ref[...], NT)

        def half(dmh, o, gvec, gate_val, dgate_fac, do_ref, dgate_ref, dgvec_ref):
            r, rs = _head_norm(o, gvec)
            nrm = r * gvec
            dgate_ref[...] = dmh * nrm * dgate_fac
            dn = dmh * gate_val
            dgvec_ref[...] += _colsum(dn * r)
            dr = dn * gvec
            parts = []
            for h in range(HEADS):
                sl = slice(128 * h, 128 * (h + 1))
                parts.append(rs[h] * (dr[:, sl] - r[:, sl] * jnp.mean(dr[:, sl] * r[:, sl], axis=-1, keepdims=True)))
            do_ref[...] = jnp.concatenate(parts, axis=1)

        hg = hgate_ref[...]
        shg = _sig(hg)
        half(dm[:, :HALF], ohg_ref[...], ghg_ref[...], hg * shg, _dsilu(hg, shg), dohg_ref, dhgate_ref, dghg_ref)
        smo = _sig(mo_ref[...])
        half(dm[:, HALF:], hml_ref[...], gml_ref[...], smo, smo * (1.0 - smo), dhml_ref, dmo_ref, dgml_ref)

    vec = _full((1, HALF))
    return pl.pallas_call(
        body, name="out_proj_bwd", grid=(T // tm,),
        in_specs=[_row(tm, D_MODEL), _row(tm, HALF, C_HGATE // 4), _row(tm, HALF, C_MO // 4),
                  _row(tm, HALF), _row(tm, HALF), vec, vec, _full(w_out.shape)],
        out_specs=[_row(tm, HALF)] * 4 + [vec, vec],
        out_shape=[_sds((T, HALF))] * 4 + [_sds((1, HALF))] * 2,
        compiler_params=_cp("arbitrary"))(dz1, u, u, o_hg, h_ml, g_hg, g_ml, w_out)


def _hgrn2_bwd(u, lb_logits, do, states, comm=None):
    T = u.shape[0]
    N = T // CHUNK

    def body(hq_ref, hf_ref, hv_ref, lg_ref, do_ref, st_ref, dhq_ref, dhf_ref, dhv_ref, dlb_ref,
             dS_ref, b_sc, k_sc):
        i = pl.program_id(0)

        @pl.when(i == 0)
        def _():
            dS_ref[...] = jnp.zeros_like(dS_ref)
            dlb_ref[...] = jnp.zeros_like(dlb_ref)

        lb_all = _sig(lg_ref[0:1, :] - lg_ref[1:2, :])
        tril = _tri(CHUNK, True)
        triu = _tri(CHUNK, False)
        ones = jnp.ones((128, 128), F32)
        rowi = _iota((SUB, 128), 0)
        row64 = _iota((CHUNK, 128), 0)
        lane = _iota((SUB, 128), 1)
        for h in range(HEADS):
            sl = slice(128 * h, 128 * h + 128)
            lb = lb_all[:, sl]
            hq = hq_ref[:, sl]
            q, g, k, f, sg, nsg, sq = _hg_gates(hq, hf_ref[:, sl], lb)
            v = hv_ref[:, sl]
            do_h = do_ref[:, sl]
            b = _dotx(tril, g)
            b_sc[...] = b
            k_sc[...] = k
            ST = st_ref[0, h]
            dST = dS_ref[h]
            bl = b_sc[CHUNK - 1:CHUNK, :]
            eb = jnp.exp(b)
            ebl = jnp.exp(bl - b)
            qt = q * eb
            kl = k * ebl
            dqt = _dot(do_h, ST)
            dkl = _dot(v, dST)
            dv_acc = _dot(kl, dST, NT)
            dq_parts = []
            dk_in = jnp.zeros((CHUNK, 128), F32)
            for i_s in range(CHUNK // SUB):
                r0 = SUB * i_s
                qi = q[r0:r0 + SUB]
                doi = do_h[r0:r0 + SUB]
                dqi = jnp.zeros((SUB, 128), F32)
                if i_s > 0:
                    r = b_sc[r0 - 1:r0, :]
                    eq = jnp.exp(b[r0:r0 + SUB] - r)
                    ek = jnp.exp(r - b[:r0])
                    qe = qi * eq
                    ke = k[:r0] * ek
                    a_off = _dot(qe, ke, NT)
                    p_off = _dot(doi, v[:r0], NT)
                    pad = jnp.zeros((CHUNK - r0, 128), F32)
                    dv_acc = dv_acc + jnp.concatenate([_dot(a_off, doi, TN), pad], axis=0)
                    dqi = dqi + _dot(p_off, ke) * eq
                    dk_in = dk_in + jnp.concatenate([_dot(p_off, qe, TN) * ek, pad], axis=0)
                tiles = _hg_diag_tiles(b_sc, r0, rowi)
                ms = [qi * (k_sc[r0 + s:r0 + s + 1, :] * tiles[s]) for s in range(SUB)]
                ps = [doi * hv_ref[r0 + s:r0 + s + 1, sl] for s in range(SUB)]
                R = _dot(jnp.concatenate(ms + ps, axis=0), ones)
                dv_rows, dk_rows = [], []
                for s in range(SUB):
                    a_s = R[SUB * s:SUB * s + SUB]
                    p_s = R[SUB * (SUB + s):SUB * (SUB + s) + SUB]
                    pt = p_s * tiles[s]
                    dqi = dqi + pt * k_sc[r0 + s:r0 + s + 1, :]
                    dk_rows.append(_colsum(pt * qi))
                    dv_rows.append(_colsum(a_s * doi))
                pad_lo = jnp.zeros((r0, 128), F32)
                pad_hi = jnp.zeros((CHUNK - r0 - SUB, 128), F32)
                dk_in = dk_in + jnp.concatenate([pad_lo] * (r0 > 0) + dk_rows + [pad_hi] * (r0 + SUB < CHUNK), axis=0)
                dv_acc = dv_acc + jnp.concatenate([pad_lo] * (r0 > 0) + dv_rows + [pad_hi] * (r0 + SUB < CHUNK), axis=0)
                dq_parts.append(dqi)
            dq_in = jnp.concatenate(dq_parts, axis=0)
            db = qt * dqt + q * dq_in - k * dk_in - kl * dkl
            last = _colsum(kl * dkl) + jnp.exp(bl) * _colsum(ST * dST)
            db = db + jnp.where(row64 == CHUNK - 1, last, 0.0)
            dg = _dotx(triu, db)
            dq_tot = dqt * eb + dq_in
            dk_tot = dkl * ebl + dk_in
            common = dg / f - dk_tot
            dhf_ref[:, sl] = (1.0 - lb) * sg * nsg * common
            dl0 = _colsum(nsg * common) * lb * (1.0 - lb)
            dlb_ref[0:1, sl] += dl0
            dlb_ref[1:2, sl] -= dl0
            dhq_ref[:, sl] = dq_tot * _dsilu(hq, sq)
            dhv_ref[:, sl] = dv_acc
            dS_ref[h] = dST * jnp.exp(bl) + _dot(do_h, qt, TN)

    rev = lambda c: pl.BlockSpec((CHUNK, HALF), lambda i, c=c: (N - 1 - i, c // 4))
    rev0 = pl.BlockSpec((CHUNK, HALF), lambda i: (N - 1 - i, 0))
    return _hosted_call(
        body, comm, name="hgrn2_bwd", grid=(N,),
        in_specs=[rev(C_HQ), rev(C_HF), rev(C_HV), _full(lb_logits.shape), rev0,
                  pl.BlockSpec((1, HEADS, 128, 128), lambda i: (N - 1 - i, 0, 0, 0))],
        out_specs=[rev0, rev0, rev0, _full((2, HALF))],
        out_shape=[_sds((T, HALF))] * 3 + [_sds((2, HALF))],
        scratch_shapes=[pltpu.VMEM((HEADS, 128, 128), F32), pltpu.VMEM((CHUNK, 128), F32),
                        pltpu.VMEM((CHUNK, 128), F32)],
        args=(u, u, u, lb_logits, do, states))


def _mlstm_bwd(qkc, u, dh, cst, nst, mst):
    T = u.shape[0]
    N = T // CHUNK

    def body(qk_ref, v_ref, g_ref, dh_ref, cst_ref, nst_ref, mst_ref, dqk_ref, dv_ref, dgt_ref,
             dC_ref, dn_ref):
        i = pl.program_id(0)

        @pl.when(i == 0)
        def _():
            dC_ref[...] = jnp.zeros_like(dC_ref)
            dn_ref[...] = jnp.zeros_like(dn_ref)

        consts = _ml_consts()
        lane = consts[2]
        triu = _tri(CHUNK, False)
        ones = jnp.ones((CHUNK, 128), F32)
        rowc = _iota((CHUNK, 1), 0)
        gates = g_ref[...]
        dg_all = jnp.zeros((CHUNK, 128), F32)
        di_all = jnp.zeros((CHUNK, 128), F32)
        for h in range(HEADS):
            q = qk_ref[:, ML_DQK * h:ML_DQK * (h + 1)] * ML_SCALE
            k = qk_ref[:, 256 + ML_DQK * h:256 + ML_DQK * (h + 1)]
            v = v_ref[:, 128 * h:128 * (h + 1)]
            dh_h = dh_ref[:, 128 * h:128 * (h + 1)]
            C, nrow, mprev = cst_ref[0, h], nst_ref[0, h], mst_ref[0, h]
            dC, dn = dC_ref[h], dn_ref[h]
            r = _ml_chunk(q, k, v, gates, h, C, nrow, mprev, consts)
            wn, ws, wo, s_mat = r["wn"], r["ws"], r["wo"], r["s"]
            inv = 1.0 / r["nrm"]
            dnum = dh_h * inv
            hh = r["num"] * inv
            dnrm = -_rowsum(dh_h * hh) * inv
            dden = jnp.where(jnp.abs(r["den"]) > r["floor"], dnrm * jnp.sign(r["den"]), 0.0)
            ds = _dot(dnum, v, NT) + dden
            dqk = ds * r["wi"]
            dd = ds * s_mat
            dq = _dot(dqk, k) + wn * (_dot(dnum, C, NT) + dden * nrow)
            dk_st = ws * (_dot(v, dC, NT) + dn)
            dk = _dot(dqk, q, TN) + dk_st
            dv_ref[:, 128 * h:128 * (h + 1)] = _dot(s_mat, dnum, TN) + ws * _dot(k, dC)
            dqk_ref[:, ML_DQK * h:ML_DQK * (h + 1)] = dq * ML_SCALE
            dqk_ref[:, 256 + ML_DQK * h:256 + ML_DQK * (h + 1)] = dk
            dC_ref[h] = wo * dC + _dot(q * wn, dnum, TN)
            dn_ref[h] = wo * dn + _colsum(q * (wn * dden))
            e_col = wn * (_rowsum(dnum * r["qc"]) + dden * r["qn"])
            c_col = _rowsum(k * dk_st)
            z = wo * (jnp.sum(dC * C, keepdims=True) + jnp.sum(dn * nrow, keepdims=True))
            dd_cols = _dotx(dd, ones, TN)[:, 0:1]
            dg_col = _rowsum(dd) - dd_cols + e_col - c_col
            dg_col = dg_col + jnp.where(rowc == CHUNK - 1, jnp.sum(c_col, keepdims=True) + z, 0.0)
            di_col = dd_cols + c_col
            dg_all = dg_all + jnp.where(lane == 4 + h, dg_col, 0.0)
            di_all = di_all + jnp.where(lane == h, di_col, 0.0)
        dlf = _dotx(triu, dg_all)
        dgt_ref[...] = di_all + dlf * _sig(-gates)

    rev0 = pl.BlockSpec((CHUNK, HALF), lambda i: (N - 1 - i, 0))
    st = lambda a, b: pl.BlockSpec((1, HEADS, a, b), lambda i: (N - 1 - i, 0, 0, 0))
    return pl.pallas_call(
        body, name="mlstm_bwd", grid=(N,),
        in_specs=[rev0, pl.BlockSpec((CHUNK, HALF), lambda i: (N - 1 - i, C_MV // 4)),
                  pl.BlockSpec((CHUNK, 128), lambda i: (N - 1 - i, C_GATES)), rev0,
                  st(ML_DQK, 128), st(1, ML_DQK), st(1, 1)],
        out_specs=[rev0, rev0, pl.BlockSpec((CHUNK, 128), lambda i: (N - 1 - i, 0))],
        out_shape=[_sds((T, HALF)), _sds((T, HALF)), _sds((T, 128))],
        scratch_shapes=[pltpu.VMEM((HEADS, ML_DQK, 128), F32), pltpu.VMEM((HEADS, 1, ML_DQK), F32)],
        compiler_params=_cp("arbitrary"))(qkc, u, u, dh, cst, nst, mst)


def _conv_bwd(u, pre, dqkc, cw, tm):
    T = u.shape[0]
    hb = tm // 8
    nb = T // 8

    def body(x_ref, xh_ref, pre_ref, preh_ref, d_ref, dh_ref, w_ref, dx_ref, dw_ref, db_ref):
        i = pl.program_id(0)

        @pl.when(i == 0)
        def _():
            dw_ref[...] = jnp.zeros_like(dw_ref)
            db_ref[...] = jnp.zeros_like(db_ref)

        def dpre_of(pre, d):
            s = _sig(pre)
            return d * _dsilu(pre, s)

        rowi = _iota((8, HALF), 0)
        dpre = dpre_of(pre_ref[...], d_ref[...])
        dpre_next = jnp.where(i < pl.num_programs(0) - 1, dpre_of(preh_ref[...], dh_ref[...]), 0.0)
        x = x_ref[...]
        xprev = jnp.where(i > 0, xh_ref[...], 0.0)
        dx = dpre * w_ref[3:4, :]
        db_ref[...] += _colsum(dpre)
        dws = [None] * 4
        dws[3] = _colsum(dpre * x)
        for j in (1, 2, 3):
            dx = dx + _shift_rows_up(dpre, dpre_next, j, rowi) * w_ref[3 - j:4 - j, :]
            dws[3 - j] = _colsum(dpre * _shift_rows(x, xprev, j, rowi))
        dx_ref[...] = dx
        dw_ref[...] += jnp.concatenate(dws, axis=0)

    cur = lambda blk: pl.BlockSpec((tm, HALF), lambda i, blk=blk: (i, blk))
    nxt = pl.BlockSpec((8, HALF), lambda i: (jnp.minimum((i + 1) * hb, nb - 1), 0))
    return pl.pallas_call(
        body, name="conv_bwd", grid=(T // tm,),
        in_specs=[cur(C_MQK // 4), pl.BlockSpec((8, HALF), lambda i: (jnp.maximum(i * hb - 1, 0), C_MQK // 4)),
                  cur(0), nxt, cur(0), nxt, _full(cw.shape)],
        out_specs=[cur(0), _full((4, HALF)), _full((1, HALF))],
        out_shape=[_sds((T, HALF)), _sds((4, HALF)), _sds((1, HALF))],
        compiler_params=_cp("arbitrary"))(u, u, pre, pre, dqkc, dqkc, cw)


def _in_proj_bwd(dz1, du, w, tm, comms=None):
    T = dz1.shape[0]

    def body(dz_ref, du_ref, w_ref, dx_ref):
        dx_ref[...] = ALPHA * dz_ref[...] + _dot(du_ref[...], w_ref[...], NT)

    (dx,), got = _hosted_call(
        body, comms, name="in_proj_bwd", grid=(T // tm,),
        in_specs=[_row(tm, D_MODEL), _row(tm, PROJ_WP), _full(w.shape)],
        out_specs=[_row(tm, D_MODEL)], out_shape=[_sds((T, D_MODEL))], scratch_shapes=[], args=(dz1, du, w))
    return dx, got


def _wgrad(a, b, name, tm, tn, tk, out_dtype=F32):
    T, M = a.shape
    N = b.shape[1]
    tm, tn, tk = min(tm, M), min(tn, N), min(tk, T)
    nk = T // tk

    def body(a_ref, b_ref, o_ref, acc_ref):
        kk = pl.program_id(2)

        @pl.when(kk == 0)
        def _():
            acc_ref[...] = jnp.zeros_like(acc_ref)

        acc_ref[...] += _dot(a_ref[...], b_ref[...], TN)

        @pl.when(kk == nk - 1)
        def _():
            o_ref[...] = acc_ref[...].astype(out_dtype)

    return pl.pallas_call(
        body, name=name, grid=(M // tm, N // tn, nk),
        in_specs=[pl.BlockSpec((tk, tm), lambda i, j, kk: (kk, i)), pl.BlockSpec((tk, tn), lambda i, j, kk: (kk, j))],
        out_specs=pl.BlockSpec((tm, tn), lambda i, j, kk: (i, j)), out_shape=jax.ShapeDtypeStruct((M, N), out_dtype),
        scratch_shapes=[pltpu.VMEM((tm, tn), F32)],
        compiler_params=_cp("parallel", "parallel", "arbitrary"))(a, b)


def _wgrad_colsum(a, b, name, tk, out_dtype):
    T, M = a.shape
    N = b.shape[1]
    tk = min(tk, T)
    nk = T // tk

    def body(a_ref, b_ref, o_ref, cs_ref, acc_ref):
        kk = pl.program_id(0)

        @pl.when(kk == 0)
        def _():
            acc_ref[...] = jnp.zeros_like(acc_ref)
            cs_ref[...] = jnp.zeros_like(cs_ref)

        bt = b_ref[...]
        acc_ref[...] += _dot(a_ref[...], bt, TN)
        cs_ref[...] += _colsum(bt)

        @pl.when(kk == nk - 1)
        def _():
            o_ref[...] = acc_ref[...].astype(out_dtype)

    return pl.pallas_call(
        body, name=name, grid=(nk,),
        in_specs=[pl.BlockSpec((tk, M), lambda kk: (kk, 0)), pl.BlockSpec((tk, N), lambda kk: (kk, 0))],
        out_specs=[_full((M, N)), _full((1, N))],
        out_shape=[jax.ShapeDtypeStruct((M, N), out_dtype), _sds((1, N))],
        scratch_shapes=[pltpu.VMEM((M, N), F32)], compiler_params=_cp("arbitrary"))(a, b)


GRAD_T = jnp.bfloat16
W_IN_S, FF_S, OUT_S, PP_S = PROJ_W // N_DEV, D_FF // N_DEV, D_MODEL // N_DEV, D_MODEL // N_DEV
LATE = ("w_ffn_gate", "w_ffn_up", "w_out", "w_ffn_down", "ple_w_gate", "ple_w_proj")


def _split_cols(a, n):
    return a.reshape(a.shape[0], N_DEV, n).transpose(1, 0, 2)


def _join_cols(a):
    return a.transpose(1, 0, 2).reshape(a.shape[1], -1)


def _late_full(parts):
    gate, up, w_out, down, pg, pp = parts
    return dict(w_ffn_gate=_join_cols(gate), w_ffn_up=_join_cols(up), w_out=w_out.reshape(D_MODEL, D_MODEL),
                w_ffn_down=down.reshape(D_FF, D_MODEL), ple_w_gate=pg.reshape(D_MODEL, D_MODEL), ple_w_proj=_join_cols(pp))


def _by_owner(g):
    return [_split_cols(g["w_ffn_gate"], FF_S), _split_cols(g["w_ffn_up"], FF_S),
            g["w_out"].reshape(N_DEV, OUT_S, D_MODEL), g["w_ffn_down"].reshape(N_DEV, FF_S, D_MODEL),
            g["ple_w_gate"].reshape(N_DEV, OUT_S, D_MODEL), _split_cols(g["ple_w_proj"], PP_S)]


def _step(x, p, tgt, w_in, b_in, lb_logits, conv_w, conv_b, g_hg, g_ml, ln1_g, ln1_b, ln2_g, ln2_b, bpg, late,
          distributed):
    T = x.shape[0]
    tm = min(256, T)
    gather = lambda *names: [_Comm("gather", [late[n] for n in names])] if distributed else None
    scatter = lambda *arrs: [_Comm("scatter", list(arrs))] if distributed else None
    rows = lambda a, n: a.reshape(N_DEV, n, D_MODEL)
    u, got1 = _in_proj(x, w_in, b_in, tm, gather("w_out", "ple_w_gate", "ple_w_proj"))
    pre, qkc = _conv_fwd(u, conv_w, conv_b, tm)
    (o_hg, hg_states), got2 = _hgrn2_fwd2(u, lb_logits, gather("w_ffn_gate", "w_ffn_up"))
    (h_ml, cst, nst, mst), got3 = _mlstm_fwd2(qkc, u, gather("w_ffn_down"))
    if distributed:
        w_out, wpg, wpp = got1[0][0].reshape(D_MODEL, D_MODEL), got1[0][1].reshape(D_MODEL, D_MODEL), _join_cols(got1[0][2])
        wg, wu, wd = _join_cols(got2[0][0]), _join_cols(got2[0][1]), got3[0][0].reshape(D_FF, D_MODEL)
    else:
        w_out, wg, wu, wd, wpg, wpp = (late[n] for n in ("w_out", "w_ffn_gate", "w_ffn_up", "w_ffn_down", "ple_w_gate", "ple_w_proj"))
    m_in, z1, x1 = _out_proj_ln(x, u, o_hg, h_ml, g_hg, g_ml, w_out, ln1_g, ln1_b, tm)
    z2, x2 = _ffn_ln(x1, wg, wu, wd, ln2_g, ln2_b, tm)
    de, dgp, dz2, loss_vec, d_bpg, d_ln2g, d_ln2b = _ple_loss_ln2_bwd(x2, z2, p, tgt, wpg, bpg, wpp, ln2_g, ln2_b, tm)
    wk = dict(tm=512, tn=D_MODEL, tk=512, out_dtype=GRAD_T)
    big = dict(ple_w_gate=_wgrad(x2, dgp, "wgrad_ple_gate", **wk), ple_w_proj=_wgrad(p, de, "wgrad_ple_proj", **wk))
    (hh, da, dbb, dz1, d_ln1g, d_ln1b), r1 = _ffn_bwd_ln1_bwd(
        x1, z1, dz2, wg, wu, wd, ln1_g, ln1_b, tm, scatter(rows(big["ple_w_gate"], OUT_S), _split_cols(big["ple_w_proj"], PP_S)))
    big.update(
        w_ffn_gate=_wgrad(x1, da, "wgrad_ffn_gate", 512, D_FF, 512, GRAD_T),
        w_ffn_up=_wgrad(x1, dbb, "wgrad_ffn_up", 512, D_FF, 512, GRAD_T),
        w_ffn_down=_wgrad(hh, dz2, "wgrad_ffn_down", D_FF, D_MODEL, 512, GRAD_T),
        w_out=_wgrad(m_in, dz1, "wgrad_w_out", **wk))
    d_ohg, d_hml, d_hgate, d_mo, d_ghg, d_gml = _out_proj_bwd(dz1, u, o_hg, h_ml, g_hg, g_ml, w_out, tm)
    (d_hq, d_hf, d_hv, d_lb), r2 = _hgrn2_bwd2(
        u, lb_logits, d_ohg, hg_states,
        scatter(_split_cols(big["w_ffn_gate"], FF_S), _split_cols(big["w_ffn_up"], FF_S), rows(big["w_ffn_down"], FF_S),
                rows(big["w_out"], OUT_S)))
    d_qkc, d_mv, d_gates = _mlstm_bwd2(qkc, u, d_hml, cst, nst, mst)
    d_mqk, d_convw, d_convb = _conv_bwd(u, pre, d_qkc, conv_w, tm)
    du = jnp.concatenate([d_hq, d_hf, d_hv, d_hgate, d_mqk, d_mv, d_mo, d_gates], axis=1)
    big["w_in"], d_bin = _wgrad_colsum(x, du, "wgrad_w_in", 256, GRAD_T)
    small = dict(b_in=d_bin, hg_lb_logits=d_lb, ml_conv_w=d_convw, ml_conv_b=d_convb, hg_norm_g=d_ghg, ml_norm_g=d_gml,
                 ln1_g=d_ln1g, ln1_b=d_ln1b, ln2_g=d_ln2g, ln2_b=d_ln2b, ple_b_gate=d_bpg)
    last = [_Comm("scatter", [_split_cols(big["w_in"][:, :PROJ_W], W_IN_S)]),
            _Comm("gather", [_pack_small(small, loss_vec)])] if distributed else None
    dx, r3 = _in_proj_bwd(dz1, du, w_in, tm, last)
    gathered_small = None
    if distributed:
        big = dict(ple_w_gate=r1[0][0], ple_w_proj=r1[0][1], w_ffn_gate=r2[0][0], w_ffn_up=r2[0][1],
                   w_ffn_down=r2[0][2], w_out=r2[0][3], w_in=r3[0][0])
        gathered_small = r3[1][0]
    return loss_vec, dx, big, small, gathered_small


BIG = ("w_in",) + LATE
SMALL = dict(b_in=(8, 32, PROJ_WP), hg_lb_logits=(40, 8, 1024), ml_conv_w=(48, 16, 2048), ml_conv_b=(64, 8, 512),
             hg_norm_g=(72, 8, 512), ml_norm_g=(80, 8, 512), ln1_g=(88, 8, 1024), ln1_b=(96, 8, 1024),
             ln2_g=(104, 8, 1024), ln2_b=(112, 8, 1024), ple_b_gate=(120, 8, 1024))
SM_ROWS = 128


def _padc(a, n):
    return jnp.pad(a, [(0, 0)] * (a.ndim - 1) + [(0, n - a.shape[-1])])


def _pack_small(d, loss_vec=None):
    first = jnp.zeros((8, 128), F32) if loss_vec is None else loss_vec.reshape(8, 128)
    parts = [first]
    for name, (_, rows, n) in SMALL.items():
        parts.append(jnp.pad(d[name].reshape(-1), (0, rows * 128 - n)).reshape(rows, 128))
    return jnp.concatenate(parts, axis=0)


def _unpack_small(slab, shapes):
    return {name: slab[r0:r0 + rows].reshape(-1)[:n].reshape(shapes[name]) for name, (r0, rows, n) in SMALL.items()}


def _gather_two_level(blocks, name):
    n = len(blocks)

    def body(*refs):
        x_refs, out_refs = refs[:n], refs[n:2 * n]
        send_sems, recv_sems, local_sems = refs[2 * n:]
        x, y, c = lax.axis_index("x"), lax.axis_index("y"), lax.axis_index("c")
        me, sibling = (x, y, c), (x, y, 1 - c)
        chips = [(1 - x, y), (x, 1 - y), (1 - x, 1 - y)]

        def copy(i, k, block, to, own=False):
            slab = out_refs[i].at[4 * block[0] + 2 * block[1] + block[2]]
            return pltpu.make_async_remote_copy(
                src_ref=x_refs[i] if own else slab, dst_ref=slab, send_sem=send_sems.at[7 * i + k],
                recv_sem=recv_sems.at[7 * i + k], device_id=to, device_id_type=MESH)

        mine = [pltpu.make_async_copy(x_refs[i], out_refs[i].at[4 * x + 2 * y + c], local_sems.at[i]) for i in range(n)]
        for cp in mine:
            cp.start()
        first = [copy(i, 0, me, sibling, own=True) for i in range(n)]
        first += [copy(i, 1 + j, me, (*chip, c), own=True) for j, chip in enumerate(chips) for i in range(n)]
        for cp in first:
            cp.start()
        passed = []
        for j, chip in enumerate(chips):
            for i in range(n):
                copy(i, 1 + j, (*chip, c), me).wait_recv()
                passed.append(copy(i, 4 + j, (*chip, c), sibling))
                passed[-1].start()
        for i in range(n):
            copy(i, 0, sibling, me).wait_recv()
            for j, chip in enumerate(chips):
                copy(i, 4 + j, (*chip, 1 - c), me).wait_recv()
        for cp in first + passed:
            cp.wait_send()
        for cp in mine:
            cp.wait()

    return pl.pallas_call(
        body, name=name, out_shape=[jax.ShapeDtypeStruct((N_DEV,) + b.shape, b.dtype) for b in blocks],
        in_specs=[ANY] * n, out_specs=[ANY] * n,
        scratch_shapes=[pltpu.SemaphoreType.DMA((7 * n,)), pltpu.SemaphoreType.DMA((7 * n,)),
                        pltpu.SemaphoreType.DMA((n,))])(*blocks)


def _adamw(w, g, m, v):
    m = B1 * m + (1.0 - B1) * g
    v = B2 * v + (1.0 - B2) * jnp.square(g)
    m_hat = m / (1.0 - B1 ** STEP)
    v_hat = v / (1.0 - B2 ** STEP)
    return -LR * (m_hat / (jnp.sqrt(v_hat) + EPS) + WD * w), m, v


def _sum_slabs(ref):
    g = ref[0].astype(F32)
    for j in range(1, N_DEV):
        g = g + ref[j].astype(F32)
    return g


def _adamw_matrix(rb, w, m, v, name):
    R, C = w.shape
    tr = 256 if R % 256 == 0 else R

    def body(rb_ref, w_ref, m_ref, v_ref, g_ref, d_ref, m2_ref, v2_ref):
        g = _sum_slabs(rb_ref)
        g_ref[...] = g
        d_ref[...], m2_ref[...], v2_ref[...] = _adamw(w_ref[...], g, m_ref[...], v_ref[...])

    blk = pl.BlockSpec((tr, C), lambda i: (i, 0))
    return pl.pallas_call(
        body, name=name, grid=(R // tr,),
        in_specs=[pl.BlockSpec((N_DEV, tr, C), lambda i: (0, i, 0)), blk, blk, blk],
        out_specs=[blk] * 4, out_shape=[_sds((R, C))] * 4, compiler_params=_cp("parallel"))(rb, w, m, v)


def _adamw_small(sg, w, m, v):
    def body(sg_ref, w_ref, m_ref, v_ref, loss_ref, g_ref, d_ref, m2_ref, v2_ref):
        g = _sum_slabs(sg_ref)
        loss_ref[...] = (0.5 / D_MODEL) * jnp.sum(g[0:8], keepdims=True)
        g_ref[...] = g
        d_ref[...], m2_ref[...], v2_ref[...] = _adamw(w_ref[...], g, m_ref[...], v_ref[...])

    return pl.pallas_call(
        body, name="adamw_small", out_shape=[_sds((1, 1))] + [_sds((SM_ROWS, 128))] * 4)(sg, w, m, v)


WEIGHTS = ("w_in", "b_in", "hg_lb_logits", "ml_conv_w", "ml_conv_b", "hg_norm_g", "ml_norm_g", "w_out", "ln1_g", "ln1_b",
           "w_ffn_gate", "w_ffn_up", "w_ffn_down", "ln2_g", "ln2_b", "ple_w_proj", "ple_w_gate", "ple_b_gate")
CONV_S = HALF // N_DEV


def kernel(x, p, w_in, b_in, hg_lb_logits, ml_conv_w, ml_conv_b, hg_norm_g, ml_norm_g, w_out, ln1_g, ln1_b, w_ffn_gate, w_ffn_up, w_ffn_down, ln2_g, ln2_b, ple_w_proj, ple_w_gate, ple_b_gate, loss_target, m_w_in, m_b_in, m_hg_lb_logits, m_ml_conv_w, m_ml_conv_b, m_hg_norm_g, m_ml_norm_g, m_w_out, m_ln1_g, m_ln1_b, m_w_ffn_gate, m_w_ffn_up, m_w_ffn_down, m_ln2_g, m_ln2_b, m_ple_w_proj, m_ple_w_gate, m_ple_b_gate, v_w_in, v_b_in, v_hg_lb_logits, v_ml_conv_w, v_ml_conv_b, v_hg_norm_g, v_ml_norm_g, v_w_out, v_ln1_g, v_ln1_b, v_w_ffn_gate, v_w_ffn_up, v_w_ffn_down, v_ln2_g, v_ln2_b, v_ple_w_proj, v_ple_w_gate, v_ple_b_gate):
    args = locals()
    me = 4 * lax.axis_index("x") + 2 * lax.axis_index("y") + lax.axis_index("c")
    shapes = {n: args[n].shape for n in WEIGHTS}
    drop = lambda n, a: a[0] if n in BIG or n == "ml_conv_w" else a
    W = {n: drop(n, args[n]) for n in WEIGHTS}
    M = {n: drop(n, args["m_" + n]) for n in WEIGHTS}
    V = {n: drop(n, args["v_" + n]) for n in WEIGHTS}

    g_in, g_conv = _gather_two_level(
        [W["w_in"].astype(_MXU), jnp.pad(W["ml_conv_w"], ((0, 4), (0, 128 - CONV_S)))], "gather_w_in")
    w_in_full = _padc(_join_cols(g_in), PROJ_WP)
    conv_full = _join_cols(g_conv[:, :4, :CONV_S])

    _, dx, big, _, sg = _step(
        x[0], p[0, 0], loss_target[0], w_in_full, _padc(b_in, PROJ_WP), hg_lb_logits, conv_full, ml_conv_b,
        hg_norm_g, ml_norm_g, ln1_g, ln1_b, ln2_g, ln2_b, ple_b_gate, {n: W[n].astype(_MXU) for n in LATE}, True)

    upd = {n: _adamw_matrix(big[n], W[n], M[n], V[n], "adamw_" + n) for n in BIG}
    place = lambda d: {**d, "b_in": _padc(d["b_in"], PROJ_WP),
                       "ml_conv_w": lax.dynamic_update_slice(jnp.zeros((4, HALF), F32), d["ml_conv_w"], (0, me * CONV_S))}
    loss, *small_upd = _adamw_small(sg, _pack_small(place(W)), _pack_small(place(M)), _pack_small(place(V)))

    outs = []
    sm_shapes = {**{n: shapes[n] for n in SMALL}, "b_in": (1, PROJ_WP), "ml_conv_w": (4, HALF)}
    for kind in range(4):
        smalls = _unpack_small(small_upd[kind], sm_shapes)
        smalls["b_in"] = smalls["b_in"][:, :PROJ_W]
        smalls["ml_conv_w"] = lax.dynamic_slice(smalls["ml_conv_w"], (0, me * CONV_S), (4, CONV_S))
        for n in WEIGHTS:
            outs.append((upd[n][kind] if n in BIG else smalls[n]).reshape(shapes[n]))
    return (loss.reshape(()), dx.reshape(x.shape), *outs)
```

```python
import jax
import jax.numpy as jnp
from jax import lax
from jax.experimental import pallas as pl
from jax.experimental.pallas import tpu as pltpu

F32 = jnp.float32
_MXU = jnp.bfloat16

D_MODEL = 1024
CHUNK = 64
SUB = 16
PLE_DIM = 256
HEADS = 4
ML_DQK = 64
HALF = 512
D_FF = 2816
PROJ_W = 3592
PROJ_WP = 3712
ALPHA = float(2 ** 0.25)
LN_EPS = 1e-5
RMS_EPS = 1e-6
ML_SCALE = ML_DQK ** -0.5
N_DEV = 8
LR, B1, B2, EPS, WD, STEP = 0.001, 0.9, 0.999, 1e-08, 0.01, 10
NEG = -1e30

C_HQ, C_HF, C_HV, C_HGATE, C_MQK, C_MV, C_MO, C_GATES = 0, 4, 8, 12, 16, 20, 24, 28
DU_WIDTHS = (HALF,) * 7 + (128,)

VMEM_LIMIT = 52 * 1024 * 1024
GC = 4

NN = (((1,), (0,)), ((), ()))
NT = (((1,), (1,)), ((), ()))
TN = (((0,), (0,)), ((), ()))
BNT = (((2,), (2,)), ((0,), (0,)))
BNN = (((2,), (1,)), ((0,), (0,)))
BTN = (((1,), (1,)), ((0,), (0,)))


def _dot(a, b, dims=NN):
    return lax.dot_general(a.astype(_MXU), b.astype(_MXU), dims, preferred_element_type=F32)


def _dotx(a, b, dims=NN):
    return lax.dot_general(a, b, dims, precision=lax.Precision.HIGHEST, preferred_element_type=F32)


def _sig(x):
    return jax.nn.sigmoid(x)


def _cp(*sem):
    return pltpu.CompilerParams(dimension_semantics=sem, vmem_limit_bytes=VMEM_LIMIT)


def _row(tm, c, blk=0):
    return pl.BlockSpec((tm, c), lambda i, blk=blk: (i, blk))


def _full(shape):
    nd = len(shape)
    return pl.BlockSpec(tuple(shape), lambda *_, nd=nd: (0,) * nd)


def _sds(shape, dtype=F32):
    return jax.ShapeDtypeStruct(tuple(shape), dtype)


def _iota(shape, axis):
    return lax.broadcasted_iota(jnp.int32, shape, axis)


def _colsum(x):
    return jnp.sum(x, axis=0, keepdims=True)


def _rowsum(x):
    return jnp.sum(x, axis=1, keepdims=True)


def _ln_fwd(z, g, b):
    mu = jnp.mean(z, axis=-1, keepdims=True)
    zc = z - mu
    var = jnp.mean(zc * zc, axis=-1, keepdims=True)
    rstd = lax.rsqrt(var + LN_EPS)
    xhat = zc * rstd
    return xhat * g + b, xhat, rstd


def _ln_bwd(dy, xhat, rstd, g):
    dxh = dy * g
    m1 = jnp.mean(dxh, axis=-1, keepdims=True)
    m2 = jnp.mean(dxh * xhat, axis=-1, keepdims=True)
    return rstd * (dxh - m1 - xhat * m2)


def _dsilu(x, s):
    return s * (1.0 + x * (1.0 - s))


MESH = pl.DeviceIdType.MESH
ANY = pl.BlockSpec(memory_space=pl.ANY)


def _flip(v, bit):
    return 1 - v if bit else v


class _Comm:
    def __init__(self, kind, srcs):
        self.kind, self.srcs, self.n = kind, list(srcs), len(srcs)

    def out_shape(self):
        lead = (N_DEV,) if self.kind == "gather" else ()
        return [jax.ShapeDtypeStruct(lead + s.shape, s.dtype) for s in self.srcs]

    def scratch(self):
        return [pltpu.SemaphoreType.DMA((7 * self.n,)), pltpu.SemaphoreType.DMA((7 * self.n,)),
                pltpu.SemaphoreType.DMA((self.n,))]

    def copies(self, srcs, dsts, send_sems, recv_sems, local_sems):
        x, y, c = lax.axis_index("x"), lax.axis_index("y"), lax.axis_index("c")
        me = 4 * x + 2 * y + c
        pick = (lambda s, j: s) if self.kind == "gather" else (lambda s, j: s.at[j])
        out = []
        for i, (s, d) in enumerate(zip(srcs, dsts)):
            out.append(pltpu.make_async_copy(pick(s, me), d.at[me], local_sems.at[i]))
            for k in range(1, N_DEV):
                px, py, pc = _flip(x, k & 4), _flip(y, k & 2), _flip(c, k & 1)
                out.append(pltpu.make_async_remote_copy(
                    src_ref=pick(s, 4 * px + 2 * py + pc), dst_ref=d.at[me], send_sem=send_sems.at[7 * i + k - 1],
                    recv_sem=recv_sems.at[7 * i + k - 1], device_id=(px, py, pc), device_id_type=MESH))
        return out


def _hosted_call(body, comms, *, name, grid, in_specs, out_specs, out_shape, scratch_shapes, args):
    comms = list(comms or [])
    if not comms:
        res = pl.pallas_call(body, name=name, grid=grid, in_specs=in_specs, out_specs=out_specs, out_shape=out_shape,
                             scratch_shapes=scratch_shapes, compiler_params=_cp("arbitrary"))(*args)
        return list(res), []
    n_in, n_out, n_sc, nc = len(in_specs), len(out_specs), len(scratch_shapes), sum(cm.n for cm in comms)
    last = grid[0] - 1

    def hosted(*refs):
        ins, csrc = refs[:n_in], refs[n_in:n_in + nc]
        o0 = n_in + nc
        outs, cdst = refs[o0:o0 + n_out], refs[o0 + n_out:o0 + n_out + nc]
        s0 = o0 + n_out + nc
        scr, sems = refs[s0:s0 + n_sc], refs[s0 + n_sc:]

        def copies():
            out, o = [], 0
            for j, cm in enumerate(comms):
                out += cm.copies(csrc[o:o + cm.n], cdst[o:o + cm.n], *sems[3 * j:3 * j + 3])
                o += cm.n
            return out

        i = pl.program_id(0)

        @pl.when(i == 0)
        def _():
            for cp in copies():
                cp.start()

        body(*ins, *outs, *scr)

        @pl.when(i == last)
        def _():
            for cp in copies():
                cp.wait()

    res = pl.pallas_call(
        hosted, name=name, grid=grid, in_specs=list(in_specs) + [ANY] * nc, out_specs=list(out_specs) + [ANY] * nc,
        out_shape=list(out_shape) + [s for cm in comms for s in cm.out_shape()],
        scratch_shapes=list(scratch_shapes) + [s for cm in comms for s in cm.scratch()],
        compiler_params=_cp("arbitrary"))(*args, *[a for cm in comms for a in cm.srcs])
    got, o = [], n_out
    for cm in comms:
        got.append(list(res[o:o + cm.n]))
        o += cm.n
    return list(res[:n_out]), got


def _gather_two_level(blocks, name):
    n = len(blocks)

    def body(*refs):
        x_refs, out_refs = refs[:n], refs[n:2 * n]
        send_sems, recv_sems, local_sems = refs[2 * n:]
        x, y, c = lax.axis_index("x"), lax.axis_index("y"), lax.axis_index("c")
        me, sibling = (x, y, c), (x, y, 1 - c)
        chips = [(1 - x, y), (x, 1 - y), (1 - x, 1 - y)]

        def copy(i, k, block, to, own=False):
            slab = out_refs[i].at[4 * block[0] + 2 * block[1] + block[2]]
            return pltpu.make_async_remote_copy(
                src_ref=x_refs[i] if own else slab, dst_ref=slab, send_sem=send_sems.at[7 * i + k],
                recv_sem=recv_sems.at[7 * i + k], device_id=to, device_id_type=MESH)

        mine = [pltpu.make_async_copy(x_refs[i], out_refs[i].at[4 * x + 2 * y + c], local_sems.at[i]) for i in range(n)]
        for cp in mine:
            cp.start()
        first = [copy(i, 0, me, sibling, own=True) for i in range(n)]
        first += [copy(i, 1 + j, me, (*chip, c), own=True) for j, chip in enumerate(chips) for i in range(n)]
        for cp in first:
            cp.start()
        passed = []
        for j, chip in enumerate(chips):
            for i in range(n):
                copy(i, 1 + j, (*chip, c), me).wait_recv()
                passed.append(copy(i, 4 + j, (*chip, c), sibling))
                passed[-1].start()
        for i in range(n):
            copy(i, 0, sibling, me).wait_recv()
            for j, chip in enumerate(chips):
                copy(i, 4 + j, (*chip, 1 - c), me).wait_recv()
        for cp in first + passed:
            cp.wait_send()
        for cp in mine:
            cp.wait()

    return pl.pallas_call(
        body, name=name, out_shape=[jax.ShapeDtypeStruct((N_DEV,) + b.shape, b.dtype) for b in blocks],
        in_specs=[ANY] * n, out_specs=[ANY] * n,
        scratch_shapes=[pltpu.SemaphoreType.DMA((7 * n,)), pltpu.SemaphoreType.DMA((7 * n,)),
                        pltpu.SemaphoreType.DMA((n,))])(*blocks)


def _in_proj(x, w, b, tm, comms=None):
    T = x.shape[0]

    def body(x_ref, w_ref, b_ref, o_ref):
        o_ref[...] = _dot(x_ref[...], w_ref[...]) + b_ref[...]

    (u,), got = _hosted_call(
        body, comms, name="in_proj", grid=(T // tm,),
        in_specs=[_row(tm, D_MODEL), _full(w.shape), _full(b.shape)],
        out_specs=[_row(tm, PROJ_WP)], out_shape=[_sds((T, PROJ_WP))], scratch_shapes=[], args=(x, w, b))
    return u, got


def _shift_rows(x, halo, j, rowi):
    r = pltpu.roll(x, j, 0)
    top = jnp.where(rowi < j, pltpu.roll(halo, j, 0), r[:8])
    return jnp.concatenate([top, r[8:]], axis=0)


def _shift_rows_up(x, halo, j, rowi):
    n = x.shape[0]
    r = pltpu.roll(x, n - j, 0)
    bot = jnp.where(rowi >= 8 - j, pltpu.roll(halo, 8 - j, 0), r[n - 8:])
    return jnp.concatenate([r[:n - 8], bot], axis=0)


def _conv_fwd(u, cw, cb, tm):
    T = u.shape[0]
    hb = tm // 8

    def body(x_ref, halo_ref, w_ref, b_ref, pre_ref, out_ref):
        i = pl.program_id(0)
        x = x_ref[...]
        halo = jnp.where(i > 0, halo_ref[...], 0.0)
        rowi = _iota((8, HALF), 0)
        acc = x * w_ref[3:4, :] + b_ref[...]
        for j in (1, 2, 3):
            acc = acc + _shift_rows(x, halo, j, rowi) * w_ref[3 - j:4 - j, :]
        pre_ref[...] = acc
        out_ref[...] = acc * _sig(acc)

    return pl.pallas_call(
        body, name="conv_fwd", grid=(T // tm,),
        in_specs=[pl.BlockSpec((tm, HALF), lambda i: (i, C_MQK // 4)),
                  pl.BlockSpec((8, HALF), lambda i: (jnp.maximum(i * hb - 1, 0), C_MQK // 4)),
                  _full(cw.shape), _full(cb.shape)],
        out_specs=[_row(tm, HALF), _row(tm, HALF)], out_shape=[_sds((T, HALF)), _sds((T, HALF))],
        compiler_params=_cp("parallel"))(u, u, cw, cb)


def _bdot(a, b, dims):
    return lax.dot_general(a.astype(_MXU), b.astype(_MXU), dims, preferred_element_type=F32)


def _bdotx(a, b, dims):
    return lax.dot_general(a, b, dims, precision=lax.Precision.HIGHEST, preferred_element_type=F32)


def _heads_to_batch(x, w):
    G = x.shape[0] // CHUNK
    x3 = x.reshape(G, CHUNK, HEADS * w)
    return jnp.stack([x3[:, :, w * h:w * (h + 1)] for h in range(HEADS)], axis=1).reshape(G * HEADS, CHUNK, w)


def _batch_to_heads(x3):
    B, _, w = x3.shape
    x4 = x3.reshape(B // HEADS, HEADS, CHUNK, w)
    return jnp.concatenate([x4[:, h] for h in range(HEADS)], axis=-1).reshape(B // HEADS * CHUNK, HEADS * w)


def _chunk_cumsum(x, rowmod, reverse=False):
    R = x.shape[0]
    for sh in (1, 2, 4, 8, 16, 32):
        if reverse:
            x = x + jnp.where(rowmod < CHUNK - sh, pltpu.roll(x, R - sh, 0), 0.0)
        else:
            x = x + jnp.where(rowmod >= sh, pltpu.roll(x, sh, 0), 0.0)
    return x


def _lane_col(x, c, lane):
    return _rowsum(jnp.where(lane == c, x, 0.0))


def _hg_gates(hq, hf, lb):
    sg = _sig(hf)
    nsg = _sig(-hf)
    f = lb + (1.0 - lb) * sg
    g = jnp.log(f)
    k = (1.0 - lb) * nsg
    sq = _sig(hq)
    return hq * sq, g, k, f, sg, nsg, sq


def _hg_prep(hq_ref, hf_ref, lg_ref, b_sc, k_sc):
    R = hq_ref.shape[0]
    G = R // CHUNK
    lb = _sig(lg_ref[0:1, :] - lg_ref[1:2, :])
    hq = hq_ref[...]
    q, g, k, f, sg, nsg, sq = _hg_gates(hq, hf_ref[...], lb)
    rowmod = _iota((R, HALF), 0) & (CHUNK - 1)
    b = _chunk_cumsum(g, rowmod)
    last8 = _iota((8, HALF), 0) == 7
    bl_rows = [_colsum(jnp.where(last8, b[CHUNK * c + CHUNK - 8:CHUNK * (c + 1)], 0.0)) for c in range(G)]
    bl3 = jnp.stack([r[:, 128 * h:128 * (h + 1)] for r in bl_rows for h in range(HEADS)], axis=0)
    b3, k3 = _heads_to_batch(b, 128), _heads_to_batch(k, 128)
    b_sc[...] = b3
    k_sc[...] = k3
    return dict(G=G, lb=lb, hq=hq, f=f, sg=sg, nsg=nsg, sq=sq, rowmod=rowmod, q3=_heads_to_batch(q, 128), k3=k3, b3=b3,
                bl3=bl3)


def _hg_diag_tiles(b_sc, b3, r0, rowi):
    bi = b3[:, r0:r0 + SUB]
    return [jnp.exp(jnp.where(rowi >= s, bi - b_sc[:, r0 + s:r0 + s + 1, :], NEG)) for s in range(SUB)]


def _hgrn2_fwd(u, lb_logits, comms=None):
    T = u.shape[0]
    G = min(GC, T // CHUNK)
    R, B, N = G * CHUNK, G * HEADS, T // CHUNK

    def body(hq_ref, hf_ref, hv_ref, lg_ref, o_ref, st_ref, S_ref, b_sc, k_sc, v_sc):
        @pl.when(pl.program_id(0) == 0)
        def _():
            S_ref[...] = jnp.zeros_like(S_ref)

        pz = _hg_prep(hq_ref, hf_ref, lg_ref, b_sc, k_sc)
        q3, k3, b3, bl3 = pz["q3"], pz["k3"], pz["b3"], pz["bl3"]
        v3 = _heads_to_batch(hv_ref[...], 128)
        v_sc[...] = v3
        stloc = _bdot(v3, k3 * jnp.exp(bl3 - b3), BTN).reshape(G, HEADS, 128, 128)
        dec = jnp.exp(bl3).reshape(G, HEADS, 1, 128)
        ST = S_ref[...]
        sts = []
        for c in range(G):
            sts.append(ST)
            ST = ST * dec[c] + stloc[c]
        S_ref[...] = ST
        st4 = jnp.stack(sts, axis=0)
        st_ref[...] = st4
        o = _bdot(q3 * jnp.exp(b3), st4.reshape(B, 128, 128), BNT)
        ones = jnp.ones((128, 128), F32)
        rowi = _iota((1, SUB, 128), 1)
        outs = []
        for i in range(CHUNK // SUB):
            r0 = SUB * i
            qi = q3[:, r0:r0 + SUB]
            oi = o[:, r0:r0 + SUB]
            if i > 0:
                r = b_sc[:, r0 - 1:r0, :]
                qe = qi * jnp.exp(b3[:, r0:r0 + SUB] - r)
                ke = k3[:, :r0] * jnp.exp(r - b3[:, :r0])
                oi = oi + _bdot(_bdot(qe, ke, BNT), v3[:, :r0], BNN)
            tiles = _hg_diag_tiles(b_sc, b3, r0, rowi)
            ms = [qi * (k_sc[:, r0 + s:r0 + s + 1, :] * tiles[s]) for s in range(SUB)]
            Rm = _dot(jnp.concatenate(ms, axis=1).reshape(B * SUB * SUB, 128), ones).reshape(B, SUB * SUB, 128)
            for s in range(SUB):
                oi = oi + Rm[:, SUB * s:SUB * s + SUB] * v_sc[:, r0 + s:r0 + s + 1, :]
            outs.append(oi)
        o_ref[...] = _batch_to_heads(jnp.concatenate(outs, axis=1))

    blk = lambda c: pl.BlockSpec((R, HALF), lambda n, c=c: (n, c // 4))
    return _hosted_call(
        body, comms, name="hgrn2_fwd", grid=(N // G,),
        in_specs=[blk(C_HQ), blk(C_HF), blk(C_HV), _full(lb_logits.shape)],
        out_specs=[pl.BlockSpec((R, HALF), lambda n: (n, 0)),
                   pl.BlockSpec((G, HEADS, 128, 128), lambda n: (n, 0, 0, 0))],
        out_shape=[_sds((T, HALF)), _sds((N, HEADS, 128, 128))],
        scratch_shapes=[pltpu.VMEM((HEADS, 128, 128), F32)] + [pltpu.VMEM((B, CHUNK, 128), F32)] * 3,
        args=(u, u, u, lb_logits))


def _hgrn2_bwd(u, lb_logits, do, states, comms=None):
    T = u.shape[0]
    G = min(GC, T // CHUNK)
    R, B, NG = G * CHUNK, G * HEADS, T // (G * CHUNK)

    def body(hq_ref, hf_ref, hv_ref, lg_ref, do_ref, st_ref, dhq_ref, dhf_ref, dhv_ref, dlb_ref,
             dS_ref, b_sc, k_sc, v_sc):
        @pl.when(pl.program_id(0) == 0)
        def _():
            dS_ref[...] = jnp.zeros_like(dS_ref)
            dlb_ref[...] = jnp.zeros_like(dlb_ref)

        pz = _hg_prep(hq_ref, hf_ref, lg_ref, b_sc, k_sc)
        q3, k3, b3, bl3, lb = pz["q3"], pz["k3"], pz["b3"], pz["bl3"], pz["lb"]
        v3 = _heads_to_batch(hv_ref[...], 128)
        v_sc[...] = v3
        do3 = _heads_to_batch(do_ref[...], 128)
        st3 = st_ref[...].reshape(B, 128, 128)
        eb = jnp.exp(b3)
        ebl = jnp.exp(bl3 - b3)
        qt = q3 * eb
        kl = k3 * ebl
        dstloc = _bdot(do3, qt, BTN).reshape(G, HEADS, 128, 128)
        dec = jnp.exp(bl3).reshape(G, HEADS, 1, 128)
        dST = dS_ref[...]
        dsts = [None] * G
        for c in reversed(range(G)):
            dsts[c] = dST
            dST = dST * dec[c] + dstloc[c]
        dS_ref[...] = dST
        dst3 = jnp.stack(dsts, axis=0).reshape(B, 128, 128)
        dqt = _bdot(do3, st3, BNN)
        dkl = _bdot(v3, dst3, BNN)
        dv_acc = _bdot(kl, dst3, BNT)
        ones = jnp.ones((128, 128), F32)
        rowi = _iota((1, SUB, 128), 1)
        dq_parts = []
        dk_in = jnp.zeros((B, CHUNK, 128), F32)
        for i_s in range(CHUNK // SUB):
            r0 = SUB * i_s
            qi = q3[:, r0:r0 + SUB]
            doi = do3[:, r0:r0 + SUB]
            dqi = jnp.zeros((B, SUB, 128), F32)
            if i_s > 0:
                r = b_sc[:, r0 - 1:r0, :]
                eq = jnp.exp(b3[:, r0:r0 + SUB] - r)
                ek = jnp.exp(r - b3[:, :r0])
                qe = qi * eq
                ke = k3[:, :r0] * ek
                a_off = _bdot(qe, ke, BNT)
                p_off = _bdot(doi, v3[:, :r0], BNT)
                pad = jnp.zeros((B, CHUNK - r0, 128), F32)
                dv_acc = dv_acc + jnp.concatenate([_bdot(a_off, doi, BTN), pad], axis=1)
                dqi = dqi + _bdot(p_off, ke, BNN) * eq
                dk_in = dk_in + jnp.concatenate([_bdot(p_off, qe, BTN) * ek, pad], axis=1)
            tiles = _hg_diag_tiles(b_sc, b3, r0, rowi)
            kts = [k_sc[:, r0 + s:r0 + s + 1, :] * tiles[s] for s in range(SUB)]
            ms = [qi * kts[s] for s in range(SUB)]
            ps = [doi * v_sc[:, r0 + s:r0 + s + 1, :] for s in range(SUB)]
            n_r = 2 * SUB * SUB
            Rm = _dot(jnp.concatenate(ms + ps, axis=1).reshape(B * n_r, 128), ones).reshape(B, n_r, 128)
            dv_rows, dk_rows = [], []
            for s in range(SUB):
                a_s = Rm[:, SUB * s:SUB * s + SUB]
                p_s = Rm[:, SUB * (SUB + s):SUB * (SUB + s) + SUB]
                dqi = dqi + p_s * kts[s]
                dk_rows.append(jnp.sum(p_s * (tiles[s] * qi), axis=1, keepdims=True))
                dv_rows.append(jnp.sum(a_s * doi, axis=1, keepdims=True))
            lo = [jnp.zeros((B, r0, 128), F32)] * (r0 > 0)
            hi = [jnp.zeros((B, CHUNK - r0 - SUB, 128), F32)] * (r0 + SUB < CHUNK)
            dk_in = dk_in + jnp.concatenate(lo + dk_rows + hi, axis=1)
            dv_acc = dv_acc + jnp.concatenate(lo + dv_rows + hi, axis=1)
            dq_parts.append(dqi)
        dq_in = jnp.concatenate(dq_parts, axis=1)
        db = qt * dqt + q3 * dq_in - k3 * dk_in - kl * dkl
        last = jnp.sum(kl * dkl, axis=1, keepdims=True) + jnp.exp(bl3) * jnp.sum(st3 * dst3, axis=1, keepdims=True)
        db = db + jnp.where(_iota((1, CHUNK, 1), 1) == CHUNK - 1, last, 0.0)
        dg = _chunk_cumsum(_batch_to_heads(db), pz["rowmod"], reverse=True)
        dq_tot = _batch_to_heads(dqt * eb + dq_in)
        dk_tot = _batch_to_heads(dkl * ebl + dk_in)
        common = dg / pz["f"] - dk_tot
        dhf_ref[...] = ((1.0 - lb) * pz["sg"] * pz["nsg"] * common).astype(dhf_ref.dtype)
        dl0 = _colsum(pz["nsg"] * common) * lb * (1.0 - lb)
        dlb_ref[0:1, :] += dl0
        dlb_ref[1:2, :] -= dl0
        dhq_ref[...] = (dq_tot * _dsilu(pz["hq"], pz["sq"])).astype(dhq_ref.dtype)
        dhv_ref[...] = _batch_to_heads(dv_acc).astype(dhv_ref.dtype)

    rev = lambda c: pl.BlockSpec((R, HALF), lambda i, c=c: (NG - 1 - i, c // 4))
    rev0 = pl.BlockSpec((R, HALF), lambda i: (NG - 1 - i, 0))
    return _hosted_call(
        body, comms, name="hgrn2_bwd", grid=(NG,),
        in_specs=[rev(C_HQ), rev(C_HF), rev(C_HV), _full(lb_logits.shape), rev0,
                  pl.BlockSpec((G, HEADS, 128, 128), lambda i: (NG - 1 - i, 0, 0, 0))],
        out_specs=[rev0, rev0, rev0, _full((2, HALF))],
        out_shape=[_sds((T, HALF), _MXU)] * 3 + [_sds((2, HALF))],
        scratch_shapes=[pltpu.VMEM((HEADS, 128, 128), F32)] + [pltpu.VMEM((B, CHUNK, 128), F32)] * 3,
        args=(u, u, u, lb_logits, do, states))


def _lanes_to_batch_cols(x, lane):
    G = x.shape[0] // CHUNK
    cols = [_lane_col(x, 4 + h, lane).reshape(G, CHUNK, 1) for h in range(HEADS)]
    return jnp.stack(cols, axis=1).reshape(G * HEADS, CHUNK, 1)


def _row_scalars(rows):
    lane = _iota((1, 128), 1)
    return jnp.stack([_rowsum(jnp.where(lane == 4 + h, r, 0.0)) for r in rows for h in range(HEADS)], axis=0)


def _ml_gates(gates):
    R = gates.shape[0]
    lane = _iota((R, 128), 1)
    rowmod = _iota((R, 128), 0) & (CHUNK - 1)
    lf = jnp.minimum(gates, 0.0) - jnp.log(1.0 + jnp.exp(-jnp.abs(gates)))
    g_all = _chunk_cumsum(lf, rowmod)
    x_all = pltpu.roll(gates, 4, 1) - g_all
    return g_all, x_all, lane, rowmod


def _ml_chunk_rows(g_all, x_all, mprev, g):
    gl = g_all[CHUNK * g + CHUNK - 8:CHUNK * (g + 1)]
    gl = _colsum(jnp.where(_iota((8, 128), 0) == 7, gl, 0.0))
    a = gl + x_all[CHUNK * g:CHUNK * (g + 1)]
    m_new = jnp.maximum(gl + mprev, jnp.max(a, axis=0, keepdims=True))
    return m_new, jnp.exp(gl + mprev - m_new), jnp.exp(a - m_new)


def _ml_batched(q3, k3, v3, g_all, x_all, lane, C3, n3, mprev3):
    G = g_all.shape[0] // CHUNK
    gcol3 = _lanes_to_batch_cols(g_all, lane)
    onehot = jnp.where(_iota((G, 8, 128), 1) + 4 == _iota((G, 8, 128), 2), 1.0, 0.0).astype(F32)
    rows = _bdotx(onehot, x_all.reshape(G, CHUNK, 128), BNT)
    sub = _iota((G, 8, CHUNK), 1)
    row3 = jnp.stack([jnp.sum(jnp.where(sub == h, rows, 0.0), axis=1, keepdims=True) for h in range(HEADS)],
                     axis=1).reshape(G * HEADS, 1, CHUNK)
    causal = _iota((1, CHUNK, CHUNK), 1) >= _iota((1, CHUNK, CHUNK), 2)
    dmat = jnp.where(causal, gcol3 + row3, NEG)
    m_inter = gcol3 + mprev3
    m_t = jnp.maximum(m_inter, jnp.max(dmat, axis=2, keepdims=True))
    wi = jnp.exp(dmat - m_t)
    wn = jnp.exp(m_inter - m_t)
    s3 = _bdot(q3, k3, BNT) * wi
    qc = _bdot(q3, C3, BNN)
    qn = jnp.sum(q3 * n3, axis=2, keepdims=True)
    num = _bdot(s3, v3, BNN) + wn * qc
    den = jnp.sum(s3, axis=2, keepdims=True) + wn * qn
    floor = jnp.exp(-m_t)
    return dict(wi=wi, wn=wn, s=s3, qc=qc, qn=qn, num=num, den=den, floor=floor, nrm=jnp.maximum(jnp.abs(den), floor))


def _mlstm_fwd(qkc, u, comms=None):
    T = u.shape[0]
    G = min(GC, T // CHUNK)
    R = G * CHUNK
    N = T // CHUNK

    def body(qk_ref, v_ref, g_ref, h_ref, cst_ref, nst_ref, mst_ref, C_ref, n_ref, m_ref):
        @pl.when(pl.program_id(0) == 0)
        def _():
            C_ref[...] = jnp.zeros_like(C_ref)
            n_ref[...] = jnp.zeros_like(n_ref)
            m_ref[...] = jnp.zeros_like(m_ref)

        g_all, x_all, lane, _ = _ml_gates(g_ref[...])
        m_row = m_ref[...]
        mprev_rows, wo_rows, ws_parts = [], [], []
        for g in range(G):
            mprev_rows.append(m_row)
            m_row, wo, ws = _ml_chunk_rows(g_all, x_all, m_row, g)
            wo_rows.append(wo)
            ws_parts.append(ws)
        m_ref[...] = m_row
        mst_ref[...] = jnp.stack(mprev_rows, axis=0)
        ws3 = _lanes_to_batch_cols(jnp.concatenate(ws_parts, axis=0), lane)
        wo4 = _row_scalars(wo_rows).reshape(G, HEADS, 1, 1)
        q3 = _heads_to_batch(qk_ref[:, :256] * ML_SCALE, ML_DQK)
        k3 = _heads_to_batch(qk_ref[:, 256:], ML_DQK)
        v3 = _heads_to_batch(v_ref[...], 128)
        kw = k3 * ws3
        cloc = _bdot(kw, v3, BTN).reshape(G, HEADS, ML_DQK, 128)
        nloc = jnp.sum(kw, axis=1, keepdims=True).reshape(G, HEADS, 1, ML_DQK)
        C, nn = C_ref[...], n_ref[...]
        cs, ns = [], []
        for g in range(G):
            cs.append(C)
            ns.append(nn)
            C = wo4[g] * C + cloc[g]
            nn = wo4[g] * nn + nloc[g]
        C_ref[...] = C
        n_ref[...] = nn
        c4, n4 = jnp.stack(cs, axis=0), jnp.stack(ns, axis=0)
        cst_ref[...] = c4
        nst_ref[...] = n4
        r = _ml_batched(q3, k3, v3, g_all, x_all, lane, c4.reshape(G * HEADS, ML_DQK, 128),
                        n4.reshape(G * HEADS, 1, ML_DQK), _row_scalars(mprev_rows))
        h_ref[...] = _batch_to_heads(r["num"] / r["nrm"])

    return _hosted_call(
        body, comms, name="mlstm_fwd", grid=(N // G,),
        in_specs=[pl.BlockSpec((R, HALF), lambda n: (n, 0)), pl.BlockSpec((R, HALF), lambda n: (n, C_MV // 4)),
                  pl.BlockSpec((R, 128), lambda n: (n, C_GATES))],
        out_specs=[pl.BlockSpec((R, HALF), lambda n: (n, 0)),
                   pl.BlockSpec((G, HEADS, ML_DQK, 128), lambda n: (n, 0, 0, 0)),
                   pl.BlockSpec((G, HEADS, 1, ML_DQK), lambda n: (n, 0, 0, 0)),
                   pl.BlockSpec((G, 1, 128), lambda n: (n, 0, 0))],
        out_shape=[_sds((T, HALF)), _sds((N, HEADS, ML_DQK, 128)), _sds((N, HEADS, 1, ML_DQK)), _sds((N, 1, 128))],
        scratch_shapes=[pltpu.VMEM((HEADS, ML_DQK, 128), F32), pltpu.VMEM((HEADS, 1, ML_DQK), F32),
                        pltpu.VMEM((1, 128), F32)],
        args=(qkc, u, u))


def _mlstm_bwd(qkc, u, dh, cst, nst, mst):
    T = u.shape[0]
    G = min(GC, T // CHUNK)
    R = G * CHUNK
    NG = T // R

    def body(qk_ref, v_ref, g_ref, dh_ref, cst_ref, nst_ref, mst_ref, dqk_ref, dv_ref, dgt_ref, dC_ref, dn_ref):
        @pl.when(pl.program_id(0) == 0)
        def _():
            dC_ref[...] = jnp.zeros_like(dC_ref)
            dn_ref[...] = jnp.zeros_like(dn_ref)

        B = G * HEADS
        gates = g_ref[...]
        g_all, x_all, lane, rowmod = _ml_gates(gates)
        mprev_rows = [mst_ref[g] for g in range(G)]
        wo_rows, ws_parts = [], []
        for g in range(G):
            _, wo, ws = _ml_chunk_rows(g_all, x_all, mprev_rows[g], g)
            wo_rows.append(wo)
            ws_parts.append(ws)
        ws3 = _lanes_to_batch_cols(jnp.concatenate(ws_parts, axis=0), lane)
        wo3 = _row_scalars(wo_rows)
        wo4 = wo3.reshape(G, HEADS, 1, 1)
        q3 = _heads_to_batch(qk_ref[:, :256] * ML_SCALE, ML_DQK)
        k3 = _heads_to_batch(qk_ref[:, 256:], ML_DQK)
        v3 = _heads_to_batch(v_ref[...], 128)
        dh3 = _heads_to_batch(dh_ref[...], 128)
        C3 = cst_ref[...].reshape(B, ML_DQK, 128)
        n3 = nst_ref[...].reshape(B, 1, ML_DQK)
        r = _ml_batched(q3, k3, v3, g_all, x_all, lane, C3, n3, _row_scalars(mprev_rows))
        wn, s3 = r["wn"], r["s"]
        inv = 1.0 / r["nrm"]
        dnum = dh3 * inv
        dnrm = -jnp.sum(dh3 * (r["num"] * inv), axis=2, keepdims=True) * inv
        dden = jnp.where(jnp.abs(r["den"]) > r["floor"], dnrm * jnp.sign(r["den"]), 0.0)
        ds = _bdot(dnum, v3, BNT) + dden
        dqk = ds * r["wi"]
        dd = ds * s3
        qw = q3 * wn
        dcloc = _bdot(qw, dnum, BTN).reshape(G, HEADS, ML_DQK, 128)
        dnloc = jnp.sum(qw * dden, axis=1, keepdims=True).reshape(G, HEADS, 1, ML_DQK)
        dC, dn = dC_ref[...], dn_ref[...]
        dcs, dns = [None] * G, [None] * G
        for g in reversed(range(G)):
            dcs[g], dns[g] = dC, dn
            dC = wo4[g] * dC + dcloc[g]
            dn = wo4[g] * dn + dnloc[g]
        dC_ref[...] = dC
        dn_ref[...] = dn
        dC3 = jnp.stack(dcs, axis=0).reshape(B, ML_DQK, 128)
        dn3 = jnp.stack(dns, axis=0).reshape(B, 1, ML_DQK)
        dk_st = ws3 * (_bdot(v3, dC3, BNT) + dn3)
        dq = _bdot(dqk, k3, BNN) + wn * (_bdot(dnum, C3, BNT) + dden * n3)
        dk = _bdot(dqk, q3, BTN) + dk_st
        dv = _bdot(s3, dnum, BTN) + ws3 * _bdot(k3, dC3, BNN)
        dv_ref[...] = _batch_to_heads(dv).astype(dv_ref.dtype)
        dqk_ref[...] = jnp.concatenate([_batch_to_heads(dq * ML_SCALE), _batch_to_heads(dk)], axis=1)
        e_col = wn * (jnp.sum(dnum * r["qc"], axis=2, keepdims=True) + dden * r["qn"])
        c_col = jnp.sum(k3 * dk_st, axis=2, keepdims=True)
        z = wo3 * (jnp.sum(dC3 * C3, axis=(1, 2), keepdims=True) + jnp.sum(dn3 * n3, axis=(1, 2), keepdims=True))
        dd_cols = _bdotx(dd, jnp.ones((B, CHUNK, 128), F32), BTN)[:, :, 0:1]
        last = _iota((1, CHUNK, 1), 1) == CHUNK - 1
        dg3 = jnp.sum(dd, axis=2, keepdims=True) - dd_cols + e_col - c_col
        dg3 = dg3 + jnp.where(last, jnp.sum(c_col, axis=1, keepdims=True) + z, 0.0)
        di3 = dd_cols + c_col

        def to_lanes(x3, first):
            x4 = x3.reshape(G, HEADS, CHUNK, 1)
            return sum(jnp.where(lane == first + h, x4[:, h].reshape(R, 1), 0.0) for h in range(HEADS))

        dlf = _chunk_cumsum(to_lanes(dg3, 4), rowmod, reverse=True)
        dgt_ref[...] = (to_lanes(di3, 0) + dlf * _sig(-gates)).astype(dgt_ref.dtype)

    rev = lambda w, c: pl.BlockSpec((R, w), lambda i, c=c: (NG - 1 - i, c))
    st = lambda *s: pl.BlockSpec((G,) + s, lambda i: (NG - 1 - i,) + (0,) * len(s))
    return pl.pallas_call(
        body, name="mlstm_bwd", grid=(NG,),
        in_specs=[rev(HALF, 0), rev(HALF, C_MV // 4), rev(128, C_GATES), rev(HALF, 0),
                  st(HEADS, ML_DQK, 128), st(HEADS, 1, ML_DQK), st(1, 128)],
        out_specs=[rev(HALF, 0), rev(HALF, 0), rev(128, 0)],
        out_shape=[_sds((T, HALF)), _sds((T, HALF), _MXU), _sds((T, 128), _MXU)],
        scratch_shapes=[pltpu.VMEM((HEADS, ML_DQK, 128), F32), pltpu.VMEM((HEADS, 1, ML_DQK), F32)],
        compiler_params=_cp("arbitrary"))(qkc, u, u, dh, cst, nst, mst)


def _head_norm(o):
    rs_parts, r_parts = [], []
    for h in range(HEADS):
        oh = o[:, 128 * h:128 * (h + 1)]
        rs = lax.rsqrt(jnp.mean(oh * oh, axis=-1, keepdims=True) + RMS_EPS)
        rs_parts.append(rs)
        r_parts.append(oh * rs)
    return jnp.concatenate(r_parts, axis=1), rs_parts


def _out_proj_ln(x, u, o_hg, h_ml, g_hg, g_ml, w_out, ln_g, ln_b, tm):
    T = x.shape[0]

    def body(x_ref, hgate_ref, mo_ref, ohg_ref, hml_ref, ghg_ref, gml_ref, w_ref, g_ref, b_ref,
             m_ref, z_ref, x1_ref):
        hgate = hgate_ref[...]
        a = _head_norm(ohg_ref[...])[0] * ghg_ref[...] * (hgate * _sig(hgate))
        b = _head_norm(hml_ref[...])[0] * gml_ref[...] * _sig(mo_ref[...])
        m = jnp.concatenate([a, b], axis=1)
        m_ref[...] = m.astype(m_ref.dtype)
        z = ALPHA * x_ref[...] + _dot(m, w_ref[...])
        z_ref[...] = z
        x1_ref[...] = _ln_fwd(z, g_ref[...], b_ref[...])[0]

    return pl.pallas_call(
        body, name="out_proj_ln1", grid=(T // tm,),
        in_specs=[_row(tm, D_MODEL), _row(tm, HALF, C_HGATE // 4), _row(tm, HALF, C_MO // 4),
                  _row(tm, HALF), _row(tm, HALF), _full(g_hg.shape), _full(g_ml.shape),
                  _full(w_out.shape), _full(ln_g.shape), _full(ln_b.shape)],
        out_specs=[_row(tm, D_MODEL)] * 3,
        out_shape=[_sds((T, D_MODEL), _MXU), _sds((T, D_MODEL)), _sds((T, D_MODEL))],
        compiler_params=_cp("parallel"))(x, u, u, o_hg, h_ml, g_hg, g_ml, w_out, ln_g, ln_b)


def _ffn_ln(x1, wg, wu, wd, ln_g, ln_b, tm):
    T = x1.shape[0]

    def body(x_ref, wg_ref, wu_ref, wd_ref, g_ref, b_ref, z_ref, x2_ref):
        x = x_ref[...]
        a = _dot(x, wg_ref[...])
        hh = a * _sig(a) * _dot(x, wu_ref[...])
        z = ALPHA * x + _dot(hh, wd_ref[...])
        z_ref[...] = z
        x2_ref[...] = _ln_fwd(z, g_ref[...], b_ref[...])[0]

    return pl.pallas_call(
        body, name="ffn_ln2", grid=(T // tm,),
        in_specs=[_row(tm, D_MODEL), _full(wg.shape), _full(wu.shape), _full(wd.shape),
                  _full(ln_g.shape), _full(ln_b.shape)],
        out_specs=[_row(tm, D_MODEL)] * 2, out_shape=[_sds((T, D_MODEL))] * 2,
        compiler_params=_cp("parallel"))(x1, wg, wu, wd, ln_g, ln_b)


def _ple_loss_ln2_bwd(x2, z2, p, tgt, wpg, bpg, wpp, ln_g, ln_b, tm):
    T = x2.shape[0]

    def body(x2_ref, z_ref, p_ref, t_ref, wpg_ref, bpg_ref, wpp_ref, g_ref, b_ref,
             de_ref, dgp_ref, dz_ref, loss_ref, dbpg_ref, dg_ref, db_ref):
        @pl.when(pl.program_id(0) == 0)
        def _():
            for r in (loss_ref, dbpg_ref, dg_ref, db_ref):
                r[...] = jnp.zeros_like(r)

        x2 = x2_ref[...]
        gate = _sig(_dot(x2, wpg_ref[...]) + bpg_ref[...])
        e = _dot(p_ref[...], wpp_ref[...])
        err = x2 + gate * e - t_ref[...]
        loss_ref[...] += _colsum(err * err)
        dy = err * (1.0 / D_MODEL)
        de_ref[...] = (dy * gate).astype(de_ref.dtype)
        dgp = dy * e * gate * (1.0 - gate)
        dgp_ref[...] = dgp.astype(dgp_ref.dtype)
        dbpg_ref[...] += _colsum(dgp)
        dx2 = dy + _dot(dgp, wpg_ref[...], NT)
        _, xhat, rstd = _ln_fwd(z_ref[...], g_ref[...], b_ref[...])
        dg_ref[...] += _colsum(dx2 * xhat)
        db_ref[...] += _colsum(dx2)
        dz_ref[...] = _ln_bwd(dx2, xhat, rstd, g_ref[...])

    vec = _full((1, D_MODEL))
    return pl.pallas_call(
        body, name="ple_loss_ln2_bwd", grid=(T // tm,),
        in_specs=[_row(tm, D_MODEL), _row(tm, D_MODEL), _row(tm, PLE_DIM), _row(tm, D_MODEL),
                  _full(wpg.shape), vec, _full(wpp.shape), vec, vec],
        out_specs=[_row(tm, D_MODEL)] * 3 + [vec] * 4,
        out_shape=[_sds((T, D_MODEL), _MXU)] * 2 + [_sds((T, D_MODEL))] + [_sds((1, D_MODEL))] * 4,
        compiler_params=_cp("arbitrary"))(x2, z2, p, tgt, wpg, bpg, wpp, ln_g, ln_b)


def _ffn_bwd_ln1_bwd(x1, z1, dz2, wg, wu, wd, ln_g, ln_b, tm, comms=None):
    T = x1.shape[0]

    def body(x_ref, z_ref, dz2_ref, wg_ref, wu_ref, wd_ref, g_ref, b_ref,
             h_ref, da_ref, dbb_ref, dz1_ref, dg_ref, db_ref):
        @pl.when(pl.program_id(0) == 0)
        def _():
            dg_ref[...] = jnp.zeros_like(dg_ref)
            db_ref[...] = jnp.zeros_like(db_ref)

        x = x_ref[...]
        dz2 = dz2_ref[...]
        a = _dot(x, wg_ref[...])
        bb = _dot(x, wu_ref[...])
        sa = _sig(a)
        act = a * sa
        h_ref[...] = (act * bb).astype(h_ref.dtype)
        dh = _dot(dz2, wd_ref[...], NT)
        da = (dh * bb * _dsilu(a, sa)).astype(da_ref.dtype)
        dbb = (dh * act).astype(dbb_ref.dtype)
        da_ref[...] = da
        dbb_ref[...] = dbb
        dx1 = ALPHA * dz2 + _dot(da, wg_ref[...], NT) + _dot(dbb, wu_ref[...], NT)
        _, xhat, rstd = _ln_fwd(z_ref[...], g_ref[...], b_ref[...])
        dg_ref[...] += _colsum(dx1 * xhat)
        db_ref[...] += _colsum(dx1)
        dz1_ref[...] = _ln_bwd(dx1, xhat, rstd, g_ref[...])

    vec = _full((1, D_MODEL))
    return _hosted_call(
        body, comms, name="ffn_bwd_ln1_bwd", grid=(T // tm,),
        in_specs=[_row(tm, D_MODEL)] * 3 + [_full(wg.shape), _full(wu.shape), _full(wd.shape), vec, vec],
        out_specs=[_row(tm, D_FF)] * 3 + [_row(tm, D_MODEL), vec, vec],
        out_shape=[_sds((T, D_FF), _MXU)] * 3 + [_sds((T, D_MODEL)), _sds((1, D_MODEL)), _sds((1, D_MODEL))],
        scratch_shapes=[], args=(x1, z1, dz2, wg, wu, wd, ln_g, ln_b))


def _out_proj_bwd(dz1, u, o_hg, h_ml, g_hg, g_ml, w_out, tm):
    T = dz1.shape[0]

    def body(dz_ref, hgate_ref, mo_ref, ohg_ref, hml_ref, ghg_ref, gml_ref, w_ref,
             dohg_ref, dhml_ref, dhgate_ref, dmo_ref, dghg_ref, dgml_ref):
        @pl.when(pl.program_id(0) == 0)
        def _():
            dghg_ref[...] = jnp.zeros_like(dghg_ref)
            dgml_ref[...] = jnp.zeros_like(dgml_ref)

        dm = _dot(dz_ref[...], w_ref[...], NT)

        def half(dmh, o, gvec, gate_val, dgate_fac, do_ref, dgate_ref, dgvec_ref):
            r, rs = _head_norm(o)
            dgate_ref[...] = (dmh * r * gvec * dgate_fac).astype(dgate_ref.dtype)
            dn = dmh * gate_val
            dgvec_ref[...] += _colsum(dn * r)
            dr = dn * gvec
            parts = []
            for h in range(HEADS):
                sl = slice(128 * h, 128 * (h + 1))
                parts.append(rs[h] * (dr[:, sl] - r[:, sl] * jnp.mean(dr[:, sl] * r[:, sl], axis=-1, keepdims=True)))
            do_ref[...] = jnp.concatenate(parts, axis=1)

        hg = hgate_ref[...]
        shg = _sig(hg)
        half(dm[:, :HALF], ohg_ref[...], ghg_ref[...], hg * shg, _dsilu(hg, shg), dohg_ref, dhgate_ref, dghg_ref)
        smo = _sig(mo_ref[...])
        half(dm[:, HALF:], hml_ref[...], gml_ref[...], smo, smo * (1.0 - smo), dhml_ref, dmo_ref, dgml_ref)

    vec = _full((1, HALF))
    return pl.pallas_call(
        body, name="out_proj_bwd", grid=(T // tm,),
        in_specs=[_row(tm, D_MODEL), _row(tm, HALF, C_HGATE // 4), _row(tm, HALF, C_MO // 4),
                  _row(tm, HALF), _row(tm, HALF), vec, vec, _full(w_out.shape)],
        out_specs=[_row(tm, HALF)] * 4 + [vec, vec],
        out_shape=[_sds((T, HALF))] * 2 + [_sds((T, HALF), _MXU)] * 2 + [_sds((1, HALF))] * 2,
        compiler_params=_cp("arbitrary"))(dz1, u, u, o_hg, h_ml, g_hg, g_ml, w_out)


def _conv_bwd(u, pre, dqkc, cw, tm):
    T = u.shape[0]
    hb = tm // 8
    nb = T // 8

    def body(x_ref, xh_ref, pre_ref, preh_ref, d_ref, dh_ref, w_ref, dx_ref, dw_ref, db_ref):
        i = pl.program_id(0)

        @pl.when(i == 0)
        def _():
            dw_ref[...] = jnp.zeros_like(dw_ref)
            db_ref[...] = jnp.zeros_like(db_ref)

        def dpre_of(pre, d):
            return d * _dsilu(pre, _sig(pre))

        rowi = _iota((8, HALF), 0)
        dpre = dpre_of(pre_ref[...], d_ref[...])
        dpre_next = jnp.where(i < pl.num_programs(0) - 1, dpre_of(preh_ref[...], dh_ref[...]), 0.0)
        x = x_ref[...]
        xprev = jnp.where(i > 0, xh_ref[...], 0.0)
        dx = dpre * w_ref[3:4, :]
        db_ref[...] += _colsum(dpre)
        dws = [None] * 4
        dws[3] = _colsum(dpre * x)
        for j in (1, 2, 3):
            dx = dx + _shift_rows_up(dpre, dpre_next, j, rowi) * w_ref[3 - j:4 - j, :]
            dws[3 - j] = _colsum(dpre * _shift_rows(x, xprev, j, rowi))
        dx_ref[...] = dx.astype(dx_ref.dtype)
        dw_ref[...] += jnp.concatenate(dws, axis=0)

    cur = lambda blk: pl.BlockSpec((tm, HALF), lambda i, blk=blk: (i, blk))
    nxt = pl.BlockSpec((8, HALF), lambda i: (jnp.minimum((i + 1) * hb, nb - 1), 0))
    return pl.pallas_call(
        body, name="conv_bwd", grid=(T // tm,),
        in_specs=[cur(C_MQK // 4), pl.BlockSpec((8, HALF), lambda i: (jnp.maximum(i * hb - 1, 0), C_MQK // 4)),
                  cur(0), nxt, cur(0), nxt, _full(cw.shape)],
        out_specs=[cur(0), _full((4, HALF)), _full((1, HALF))],
        out_shape=[_sds((T, HALF), _MXU), _sds((4, HALF)), _sds((1, HALF))],
        compiler_params=_cp("arbitrary"))(u, u, pre, pre, dqkc, dqkc, cw)


def _du_specs(rows):
    return [pl.BlockSpec((rows, w), lambda i: (i, 0)) for w in DU_WIDTHS]


def _in_proj_bwd(dz1, du_parts, w, tm, comms=None):
    T = dz1.shape[0]

    def body(dz_ref, *refs):
        du = jnp.concatenate([r[...] for r in refs[:8]], axis=1)
        refs[9][...] = ALPHA * dz_ref[...] + _dot(du, refs[8][...], NT)

    (dx,), got = _hosted_call(
        body, comms, name="in_proj_bwd", grid=(T // tm,),
        in_specs=[_row(tm, D_MODEL)] + _du_specs(tm) + [_full(w.shape)],
        out_specs=[_row(tm, D_MODEL)], out_shape=[_sds((T, D_MODEL))], scratch_shapes=[], args=(dz1, *du_parts, w))
    return dx, got


def _wgrad(a, b, name, tm, tn, tk):
    T, M = a.shape
    N = b.shape[1]
    tm, tn, tk = min(tm, M), min(tn, N), min(tk, T)
    nk = T // tk

    def body(a_ref, b_ref, o_ref, acc_ref):
        kk = pl.program_id(2)

        @pl.when(kk == 0)
        def _():
            acc_ref[...] = jnp.zeros_like(acc_ref)

        acc_ref[...] += _dot(a_ref[...], b_ref[...], TN)

        @pl.when(kk == nk - 1)
        def _():
            o_ref[...] = acc_ref[...].astype(o_ref.dtype)

    return pl.pallas_call(
        body, name=name, grid=(M // tm, N // tn, nk),
        in_specs=[pl.BlockSpec((tk, tm), lambda i, j, kk: (kk, i)), pl.BlockSpec((tk, tn), lambda i, j, kk: (kk, j))],
        out_specs=pl.BlockSpec((tm, tn), lambda i, j, kk: (i, j)), out_shape=_sds((M, N), _MXU),
        scratch_shapes=[pltpu.VMEM((tm, tn), F32)],
        compiler_params=_cp("parallel", "parallel", "arbitrary"))(a, b)


def _wgrad_w_in(x, du_parts, tk):
    T = x.shape[0]
    tk = min(tk, T)
    nk = T // tk

    def body(a_ref, *refs):
        o_ref, cs_ref, acc_ref = refs[8:]
        kk = pl.program_id(0)

        @pl.when(kk == 0)
        def _():
            acc_ref[...] = jnp.zeros_like(acc_ref)
            cs_ref[...] = jnp.zeros_like(cs_ref)

        du = jnp.concatenate([r[...] for r in refs[:8]], axis=1)
        acc_ref[...] += _dot(a_ref[...], du, TN)
        cs_ref[...] += _colsum(du.astype(F32))

        @pl.when(kk == nk - 1)
        def _():
            o_ref[...] = acc_ref[...].astype(o_ref.dtype)

    return pl.pallas_call(
        body, name="wgrad_w_in", grid=(nk,),
        in_specs=[pl.BlockSpec((tk, D_MODEL), lambda kk: (kk, 0))] + _du_specs(tk),
        out_specs=[_full((D_MODEL, PROJ_WP)), _full((1, PROJ_WP))],
        out_shape=[_sds((D_MODEL, PROJ_WP), _MXU), _sds((1, PROJ_WP))],
        scratch_shapes=[pltpu.VMEM((D_MODEL, PROJ_WP), F32)], compiler_params=_cp("arbitrary"))(x, *du_parts)


W_IN_S, FF_S, OUT_S, PP_S = PROJ_W // N_DEV, D_FF // N_DEV, D_MODEL // N_DEV, D_MODEL // N_DEV
LATE = ("w_ffn_gate", "w_ffn_up", "w_out", "w_ffn_down", "ple_w_gate", "ple_w_proj")
BIG = ("w_in",) + LATE


def _split_cols(a, n):
    return a.reshape(a.shape[0], N_DEV, n).transpose(1, 0, 2)


def _join_cols(a):
    return a.transpose(1, 0, 2).reshape(a.shape[1], -1)


def _step(x, p, tgt, w_in, b_in, lb_logits, conv_w, conv_b, g_hg, g_ml, ln1_g, ln1_b, ln2_g, ln2_b, bpg, late,
          distributed):
    T = x.shape[0]
    tm = min(256, T)
    gather = lambda *names: [_Comm("gather", [late[n] for n in names])] if distributed else None
    scatter = lambda *arrs: [_Comm("scatter", list(arrs))] if distributed else None
    rows = lambda a, n: a.reshape(N_DEV, n, D_MODEL)
    u, got1 = _in_proj(x, w_in, b_in, tm, gather("w_out", "ple_w_gate", "ple_w_proj"))
    pre, qkc = _conv_fwd(u, conv_w, conv_b, tm)
    (o_hg, hg_states), got2 = _hgrn2_fwd(u, lb_logits, gather("w_ffn_gate", "w_ffn_up"))
    (h_ml, cst, nst, mst), got3 = _mlstm_fwd(qkc, u, gather("w_ffn_down"))
    if distributed:
        w_out, wpg, wpp = got1[0][0].reshape(D_MODEL, D_MODEL), got1[0][1].reshape(D_MODEL, D_MODEL), _join_cols(got1[0][2])
        wg, wu, wd = _join_cols(got2[0][0]), _join_cols(got2[0][1]), got3[0][0].reshape(D_FF, D_MODEL)
    else:
        w_out, wg, wu, wd, wpg, wpp = (late[n] for n in ("w_out", "w_ffn_gate", "w_ffn_up", "w_ffn_down", "ple_w_gate", "ple_w_proj"))
    m_in, z1, x1 = _out_proj_ln(x, u, o_hg, h_ml, g_hg, g_ml, w_out, ln1_g, ln1_b, tm)
    z2, x2 = _ffn_ln(x1, wg, wu, wd, ln2_g, ln2_b, tm)
    de, dgp, dz2, loss_vec, d_bpg, d_ln2g, d_ln2b = _ple_loss_ln2_bwd(x2, z2, p, tgt, wpg, bpg, wpp, ln2_g, ln2_b, tm)
    big = dict(ple_w_gate=_wgrad(x2, dgp, "wgrad_ple_gate", 512, D_MODEL, 1024),
               ple_w_proj=_wgrad(p, de, "wgrad_ple_proj", 512, D_MODEL, 1024))
    (hh, da, dbb, dz1, d_ln1g, d_ln1b), r1 = _ffn_bwd_ln1_bwd(
        x1, z1, dz2, wg, wu, wd, ln1_g, ln1_b, tm, scatter(rows(big["ple_w_gate"], OUT_S), _split_cols(big["ple_w_proj"], PP_S)))
    big.update(
        w_ffn_gate=_wgrad(x1, da, "wgrad_ffn_gate", 512, D_FF, 1024),
        w_ffn_up=_wgrad(x1, dbb, "wgrad_ffn_up", 512, D_FF, 1024),
        w_ffn_down=_wgrad(hh, dz2, "wgrad_ffn_down", D_FF, D_MODEL, 1024),
        w_out=_wgrad(m_in, dz1, "wgrad_w_out", 512, D_MODEL, 1024))
    d_ohg, d_hml, d_hgate, d_mo, d_ghg, d_gml = _out_proj_bwd(dz1, u, o_hg, h_ml, g_hg, g_ml, w_out, tm)
    (d_hq, d_hf, d_hv, d_lb), r2 = _hgrn2_bwd(
        u, lb_logits, d_ohg, hg_states,
        scatter(_split_cols(big["w_ffn_gate"], FF_S), _split_cols(big["w_ffn_up"], FF_S), rows(big["w_ffn_down"], FF_S),
                rows(big["w_out"], OUT_S)))
    d_qkc, d_mv, d_gates = _mlstm_bwd(qkc, u, d_hml, cst, nst, mst)
    d_mqk, d_convw, d_convb = _conv_bwd(u, pre, d_qkc, conv_w, tm)
    du_parts = [d_hq, d_hf, d_hv, d_hgate, d_mqk, d_mv, d_mo, d_gates]
    big["w_in"], d_bin = _wgrad_w_in(x, du_parts, 256)
    small = dict(b_in=d_bin, hg_lb_logits=d_lb, ml_conv_w=d_convw, ml_conv_b=d_convb, hg_norm_g=d_ghg, ml_norm_g=d_gml,
                 ln1_g=d_ln1g, ln1_b=d_ln1b, ln2_g=d_ln2g, ln2_b=d_ln2b, ple_b_gate=d_bpg)
    last = [_Comm("scatter", [_split_cols(big["w_in"][:, :PROJ_W], W_IN_S)]),
            _Comm("gather", [_pack_small(small, loss_vec)])] if distributed else None
    dx, r3 = _in_proj_bwd(dz1, du_parts, w_in, tm, last)
    gathered_small = None
    if distributed:
        big = dict(ple_w_gate=r1[0][0], ple_w_proj=r1[0][1], w_ffn_gate=r2[0][0], w_ffn_up=r2[0][1],
                   w_ffn_down=r2[0][2], w_out=r2[0][3], w_in=r3[0][0])
        gathered_small = r3[1][0]
    return loss_vec, dx, big, small, gathered_small


SMALL = dict(b_in=(8, 32, PROJ_WP), hg_lb_logits=(40, 8, 1024), ml_conv_w=(48, 16, 2048), ml_conv_b=(64, 8, 512),
             hg_norm_g=(72, 8, 512), ml_norm_g=(80, 8, 512), ln1_g=(88, 8, 1024), ln1_b=(96, 8, 1024),
             ln2_g=(104, 8, 1024), ln2_b=(112, 8, 1024), ple_b_gate=(120, 8, 1024))
SM_ROWS = 128


def _padc(a, n):
    return jnp.pad(a, [(0, 0)] * (a.ndim - 1) + [(0, n - a.shape[-1])])


def _pack_small(d, loss_vec=None):
    first = jnp.zeros((8, 128), F32) if loss_vec is None else loss_vec.reshape(8, 128)
    parts = [first]
    for name, (_, rows, n) in SMALL.items():
        parts.append(jnp.pad(d[name].reshape(-1), (0, rows * 128 - n)).reshape(rows, 128))
    return jnp.concatenate(parts, axis=0)


def _unpack_small(slab, shapes):
    return {name: slab[r0:r0 + rows].reshape(-1)[:n].reshape(shapes[name]) for name, (r0, rows, n) in SMALL.items()}


def _adamw(w, g, m, v):
    m = B1 * m + (1.0 - B1) * g
    v = B2 * v + (1.0 - B2) * jnp.square(g)
    m_hat = m / (1.0 - B1 ** STEP)
    v_hat = v / (1.0 - B2 ** STEP)
    return -LR * (m_hat / (jnp.sqrt(v_hat) + EPS) + WD * w), m, v


def _sum_slabs(ref):
    g = ref[0].astype(F32)
    for j in range(1, N_DEV):
        g = g + ref[j].astype(F32)
    return g


def _adamw_matrix(rb, w, m, v, name):
    R, C = w.shape
    tr = 256 if R % 256 == 0 else R

    def body(rb_ref, w_ref, m_ref, v_ref, g_ref, d_ref, m2_ref, v2_ref):
        g = _sum_slabs(rb_ref)
        g_ref[...] = g
        d_ref[...], m2_ref[...], v2_ref[...] = _adamw(w_ref[...], g, m_ref[...], v_ref[...])

    blk = pl.BlockSpec((tr, C), lambda i: (i, 0))
    return pl.pallas_call(
        body, name=name, grid=(R // tr,),
        in_specs=[pl.BlockSpec((N_DEV, tr, C), lambda i: (0, i, 0)), blk, blk, blk],
        out_specs=[blk] * 4, out_shape=[_sds((R, C))] * 4, compiler_params=_cp("parallel"))(rb, w, m, v)


def _adamw_small(sg, w, m, v):
    def body(sg_ref, w_ref, m_ref, v_ref, loss_ref, g_ref, d_ref, m2_ref, v2_ref):
        g = _sum_slabs(sg_ref)
        loss_ref[...] = (0.5 / D_MODEL) * jnp.sum(g[0:8], keepdims=True)
        g_ref[...] = g
        d_ref[...], m2_ref[...], v2_ref[...] = _adamw(w_ref[...], g, m_ref[...], v_ref[...])

    return pl.pallas_call(
        body, name="adamw_small", out_shape=[_sds((1, 1))] + [_sds((SM_ROWS, 128))] * 4)(sg, w, m, v)


WEIGHTS = ("w_in", "b_in", "hg_lb_logits", "ml_conv_w", "ml_conv_b", "hg_norm_g", "ml_norm_g", "w_out", "ln1_g", "ln1_b",
           "w_ffn_gate", "w_ffn_up", "w_ffn_down", "ln2_g", "ln2_b", "ple_w_proj", "ple_w_gate", "ple_b_gate")
CONV_S = HALF // N_DEV


def kernel(x, p, w_in, b_in, hg_lb_logits, ml_conv_w, ml_conv_b, hg_norm_g, ml_norm_g, w_out, ln1_g, ln1_b, w_ffn_gate, w_ffn_up, w_ffn_down, ln2_g, ln2_b, ple_w_proj, ple_w_gate, ple_b_gate, loss_target, m_w_in, m_b_in, m_hg_lb_logits, m_ml_conv_w, m_ml_conv_b, m_hg_norm_g, m_ml_norm_g, m_w_out, m_ln1_g, m_ln1_b, m_w_ffn_gate, m_w_ffn_up, m_w_ffn_down, m_ln2_g, m_ln2_b, m_ple_w_proj, m_ple_w_gate, m_ple_b_gate, v_w_in, v_b_in, v_hg_lb_logits, v_ml_conv_w, v_ml_conv_b, v_hg_norm_g, v_ml_norm_g, v_w_out, v_ln1_g, v_ln1_b, v_w_ffn_gate, v_w_ffn_up, v_w_ffn_down, v_ln2_g, v_ln2_b, v_ple_w_proj, v_ple_w_gate, v_ple_b_gate):
    args = locals()
    me = 4 * lax.axis_index("x") + 2 * lax.axis_index("y") + lax.axis_index("c")
    shapes = {n: args[n].shape for n in WEIGHTS}
    drop = lambda n, a: a[0] if n in BIG or n == "ml_conv_w" else a
    W = {n: drop(n, args[n]) for n in WEIGHTS}
    M = {n: drop(n, args["m_" + n]) for n in WEIGHTS}
    V = {n: drop(n, args["v_" + n]) for n in WEIGHTS}

    g_in, g_conv = _gather_two_level(
        [W["w_in"].astype(_MXU), jnp.pad(W["ml_conv_w"], ((0, 4), (0, 128 - CONV_S)))], "gather_w_in")
    w_in_full = _padc(_join_cols(g_in), PROJ_WP)
    conv_full = _join_cols(g_conv[:, :4, :CONV_S])

    _, dx, big, _, sg = _step(
        x[0], p[0, 0], loss_target[0], w_in_full, _padc(b_in, PROJ_WP), hg_lb_logits, conv_full, ml_conv_b,
        hg_norm_g, ml_norm_g, ln1_g, ln1_b, ln2_g, ln2_b, ple_b_gate, {n: W[n].astype(_MXU) for n in LATE}, True)

    upd = {n: _adamw_matrix(big[n], W[n], M[n], V[n], "adamw_" + n) for n in BIG}
    place = lambda d: {**d, "b_in": _padc(d["b_in"], PROJ_WP),
                       "ml_conv_w": lax.dynamic_update_slice(jnp.zeros((4, HALF), F32), d["ml_conv_w"], (0, me * CONV_S))}
    loss, *small_upd = _adamw_small(sg, _pack_small(place(W)), _pack_small(place(M)), _pack_small(place(V)))

    outs = []
    sm_shapes = {**{n: shapes[n] for n in SMALL}, "b_in": (1, PROJ_WP), "ml_conv_w": (4, HALF)}
    for kind in range(4):
        smalls = _unpack_small(small_upd[kind], sm_shapes)
        smalls["b_in"] = smalls["b_in"][:, :PROJ_W]
        smalls["ml_conv_w"] = lax.dynamic_slice(smalls["ml_conv_w"], (0, me * CONV_S), (4, CONV_S))
        for n in WEIGHTS:
            outs.append((upd[n][kind] if n in BIG else smalls[n]).reshape(shapes[n]))
    return (loss.reshape(()), dx.reshape(x.shape), *outs)
```

```python
import jax
import jax.numpy as jnp
from jax import lax
from jax.experimental import pallas as pl
from jax.experimental.pallas import tpu as pltpu

F32 = jnp.float32
_MXU = jnp.bfloat16

D_MODEL = 1024
CHUNK = 64
SUB = 16
PLE_DIM = 256
HEADS = 4
ML_DQK = 64
HALF = 512
D_FF = 2816
PROJ_W = 3592
PROJ_WP = 3712
ALPHA = float(2 ** 0.25)
LN_EPS = 1e-5
RMS_EPS = 1e-6
ML_SCALE = ML_DQK ** -0.5
N_DEV = 8
LR, B1, B2, EPS, WD, STEP = 0.001, 0.9, 0.999, 1e-08, 0.01, 10
NEG = -1e30

C_HQ, C_HF, C_HV, C_HGATE, C_MQK, C_MV, C_MO, C_GATES = 0, 4, 8, 12, 16, 20, 24, 28
DU_WIDTHS = (HALF,) * 7 + (128,)

VMEM_LIMIT = 52 * 1024 * 1024
GC = 4

NN = (((1,), (0,)), ((), ()))
NT = (((1,), (1,)), ((), ()))
TN = (((0,), (0,)), ((), ()))
BNT = (((2,), (2,)), ((0,), (0,)))
BNN = (((2,), (1,)), ((0,), (0,)))
BTN = (((1,), (1,)), ((0,), (0,)))


def _dot(a, b, dims=NN):
    return lax.dot_general(a.astype(_MXU), b.astype(_MXU), dims, preferred_element_type=F32)


def _dotx(a, b, dims=NN):
    return lax.dot_general(a, b, dims, precision=lax.Precision.HIGHEST, preferred_element_type=F32)


def _sig(x):
    return jax.nn.sigmoid(x)


def _cp(*sem):
    return pltpu.CompilerParams(dimension_semantics=sem, vmem_limit_bytes=VMEM_LIMIT)


def _row(tm, c, blk=0):
    return pl.BlockSpec((tm, c), lambda i, blk=blk: (i, blk))


def _full(shape):
    nd = len(shape)
    return pl.BlockSpec(tuple(shape), lambda *_, nd=nd: (0,) * nd)


def _sds(shape, dtype=F32):
    return jax.ShapeDtypeStruct(tuple(shape), dtype)


def _iota(shape, axis):
    return lax.broadcasted_iota(jnp.int32, shape, axis)


def _colsum(x):
    return jnp.sum(x, axis=0, keepdims=True)


def _rowsum(x):
    return jnp.sum(x, axis=1, keepdims=True)


def _ln_fwd(z, g, b):
    mu = jnp.mean(z, axis=-1, keepdims=True)
    zc = z - mu
    var = jnp.mean(zc * zc, axis=-1, keepdims=True)
    rstd = lax.rsqrt(var + LN_EPS)
    xhat = zc * rstd
    return xhat * g + b, xhat, rstd


def _ln_bwd(dy, xhat, rstd, g):
    dxh = dy * g
    m1 = jnp.mean(dxh, axis=-1, keepdims=True)
    m2 = jnp.mean(dxh * xhat, axis=-1, keepdims=True)
    return rstd * (dxh - m1 - xhat * m2)


def _dsilu(x, s):
    return s * (1.0 + x * (1.0 - s))


MESH = pl.DeviceIdType.MESH
ANY = pl.BlockSpec(memory_space=pl.ANY)


def _flip(v, bit):
    return 1 - v if bit else v


class _Comm:
    def __init__(self, kind, srcs):
        self.kind, self.srcs, self.n = kind, list(srcs), len(srcs)

    def out_shape(self):
        lead = (N_DEV,) if self.kind == "gather" else ()
        return [jax.ShapeDtypeStruct(lead + s.shape, s.dtype) for s in self.srcs]

    def scratch(self):
        return [pltpu.SemaphoreType.DMA((7 * self.n,)), pltpu.SemaphoreType.DMA((7 * self.n,)),
                pltpu.SemaphoreType.DMA((self.n,))]

    def copies(self, srcs, dsts, send_sems, recv_sems, local_sems):
        x, y, c = lax.axis_index("x"), lax.axis_index("y"), lax.axis_index("c")
        me = 4 * x + 2 * y + c
        pick = (lambda s, j: s) if self.kind == "gather" else (lambda s, j: s.at[j])
        out = []
        for i, (s, d) in enumerate(zip(srcs, dsts)):
            out.append(pltpu.make_async_copy(pick(s, me), d.at[me], local_sems.at[i]))
            for k in range(1, N_DEV):
                px, py, pc = _flip(x, k & 4), _flip(y, k & 2), _flip(c, k & 1)
                out.append(pltpu.make_async_remote_copy(
                    src_ref=pick(s, 4 * px + 2 * py + pc), dst_ref=d.at[me], send_sem=send_sems.at[7 * i + k - 1],
                    recv_sem=recv_sems.at[7 * i + k - 1], device_id=(px, py, pc), device_id_type=MESH))
        return out


def _hosted_call(body, comms, *, name, grid, in_specs, out_specs, out_shape, scratch_shapes, args):
    comms = list(comms or [])
    if not comms:
        res = pl.pallas_call(body, name=name, grid=grid, in_specs=in_specs, out_specs=out_specs, out_shape=out_shape,
                             scratch_shapes=scratch_shapes, compiler_params=_cp("arbitrary"))(*args)
        return list(res), []
    n_in, n_out, n_sc, nc = len(in_specs), len(out_specs), len(scratch_shapes), sum(cm.n for cm in comms)
    last = grid[0] - 1

    def hosted(*refs):
        ins, csrc = refs[:n_in], refs[n_in:n_in + nc]
        o0 = n_in + nc
        outs, cdst = refs[o0:o0 + n_out], refs[o0 + n_out:o0 + n_out + nc]
        s0 = o0 + n_out + nc
        scr, sems = refs[s0:s0 + n_sc], refs[s0 + n_sc:]

        def copies():
            out, o = [], 0
            for j, cm in enumerate(comms):
                out += cm.copies(csrc[o:o + cm.n], cdst[o:o + cm.n], *sems[3 * j:3 * j + 3])
                o += cm.n
            return out

        i = pl.program_id(0)

        @pl.when(i == 0)
        def _():
            for cp in copies():
                cp.start()

        body(*ins, *outs, *scr)

        @pl.when(i == last)
        def _():
            for cp in copies():
                cp.wait()

    res = pl.pallas_call(
        hosted, name=name, grid=grid, in_specs=list(in_specs) + [ANY] * nc, out_specs=list(out_specs) + [ANY] * nc,
        out_shape=list(out_shape) + [s for cm in comms for s in cm.out_shape()],
        scratch_shapes=list(scratch_shapes) + [s for cm in comms for s in cm.scratch()],
        compiler_params=_cp("arbitrary"))(*args, *[a for cm in comms for a in cm.srcs])
    got, o = [], n_out
    for cm in comms:
        got.append(list(res[o:o + cm.n]))
        o += cm.n
    return list(res[:n_out]), got


def _gather_two_level(blocks, name):
    n = len(blocks)

    def body(*refs):
        x_refs, out_refs = refs[:n], refs[n:2 * n]
        send_sems, recv_sems, local_sems = refs[2 * n:]
        x, y, c = lax.axis_index("x"), lax.axis_index("y"), lax.axis_index("c")
        me, sibling = (x, y, c), (x, y, 1 - c)
        chips = [(1 - x, y), (x, 1 - y), (1 - x, 1 - y)]

        def copy(i, k, block, to, own=False):
            slab = out_refs[i].at[4 * block[0] + 2 * block[1] + block[2]]
            return pltpu.make_async_remote_copy(
                src_ref=x_refs[i] if own else slab, dst_ref=slab, send_sem=send_sems.at[7 * i + k],
                recv_sem=recv_sems.at[7 * i + k], device_id=to, device_id_type=MESH)

        mine = [pltpu.make_async_copy(x_refs[i], out_refs[i].at[4 * x + 2 * y + c], local_sems.at[i]) for i in range(n)]
        for cp in mine:
            cp.start()
        first = [copy(i, 0, me, sibling, own=True) for i in range(n)]
        first += [copy(i, 1 + j, me, (*chip, c), own=True) for j, chip in enumerate(chips) for i in range(n)]
        for cp in first:
            cp.start()
        passed = []
        for j, chip in enumerate(chips):
            for i in range(n):
                copy(i, 1 + j, (*chip, c), me).wait_recv()
                passed.append(copy(i, 4 + j, (*chip, c), sibling))
                passed[-1].start()
        for i in range(n):
            copy(i, 0, sibling, me).wait_recv()
            for j, chip in enumerate(chips):
                copy(i, 4 + j, (*chip, 1 - c), me).wait_recv()
        for cp in first + passed:
            cp.wait_send()
        for cp in mine:
            cp.wait()

    return pl.pallas_call(
        body, name=name, out_shape=[jax.ShapeDtypeStruct((N_DEV,) + b.shape, b.dtype) for b in blocks],
        in_specs=[ANY] * n, out_specs=[ANY] * n,
        scratch_shapes=[pltpu.SemaphoreType.DMA((7 * n,)), pltpu.SemaphoreType.DMA((7 * n,)),
                        pltpu.SemaphoreType.DMA((n,))])(*blocks)


def _in_proj(x, w, b, tm, comms=None):
    T = x.shape[0]

    def body(x_ref, w_ref, b_ref, o_ref):
        o_ref[...] = _dot(x_ref[...], w_ref[...]) + b_ref[...]

    (u,), got = _hosted_call(
        body, comms, name="in_proj", grid=(T // tm,),
        in_specs=[_row(tm, D_MODEL), _full(w.shape), _full(b.shape)],
        out_specs=[_row(tm, PROJ_WP)], out_shape=[_sds((T, PROJ_WP))], scratch_shapes=[], args=(x, w, b))
    return u, got


def _shift_rows(x, halo, j, rowi):
    r = pltpu.roll(x, j, 0)
    top = jnp.where(rowi < j, pltpu.roll(halo, j, 0), r[:8])
    return jnp.concatenate([top, r[8:]], axis=0)


def _shift_rows_up(x, halo, j, rowi):
    n = x.shape[0]
    r = pltpu.roll(x, n - j, 0)
    bot = jnp.where(rowi >= 8 - j, pltpu.roll(halo, 8 - j, 0), r[n - 8:])
    return jnp.concatenate([r[:n - 8], bot], axis=0)


def _conv_fwd(u, cw, cb, tm):
    T = u.shape[0]
    hb = tm // 8

    def body(x_ref, halo_ref, w_ref, b_ref, pre_ref, out_ref):
        i = pl.program_id(0)
        x = x_ref[...]
        halo = jnp.where(i > 0, halo_ref[...], 0.0)
        rowi = _iota((8, HALF), 0)
        acc = x * w_ref[3:4, :] + b_ref[...]
        for j in (1, 2, 3):
            acc = acc + _shift_rows(x, halo, j, rowi) * w_ref[3 - j:4 - j, :]
        pre_ref[...] = acc
        out_ref[...] = acc * _sig(acc)

    return pl.pallas_call(
        body, name="conv_fwd", grid=(T // tm,),
        in_specs=[pl.BlockSpec((tm, HALF), lambda i: (i, C_MQK // 4)),
                  pl.BlockSpec((8, HALF), lambda i: (jnp.maximum(i * hb - 1, 0), C_MQK // 4)),
                  _full(cw.shape), _full(cb.shape)],
        out_specs=[_row(tm, HALF), _row(tm, HALF)], out_shape=[_sds((T, HALF)), _sds((T, HALF))],
        compiler_params=_cp("parallel"))(u, u, cw, cb)


def _bdot(a, b, dims):
    return lax.dot_general(a.astype(_MXU), b.astype(_MXU), dims, preferred_element_type=F32)


def _bdotx(a, b, dims):
    return lax.dot_general(a, b, dims, precision=lax.Precision.HIGHEST, preferred_element_type=F32)


def _heads_to_batch(x, w):
    G = x.shape[0] // CHUNK
    x3 = x.reshape(G, CHUNK, HEADS * w)
    return jnp.stack([x3[:, :, w * h:w * (h + 1)] for h in range(HEADS)], axis=1).reshape(G * HEADS, CHUNK, w)


def _batch_to_heads(x3):
    B, _, w = x3.shape
    x4 = x3.reshape(B // HEADS, HEADS, CHUNK, w)
    return jnp.concatenate([x4[:, h] for h in range(HEADS)], axis=-1).reshape(B // HEADS * CHUNK, HEADS * w)


def _chunk_cumsum(x, rowmod, reverse=False):
    R = x.shape[0]
    for sh in (1, 2, 4, 8, 16, 32):
        if reverse:
            x = x + jnp.where(rowmod < CHUNK - sh, pltpu.roll(x, R - sh, 0), 0.0)
        else:
            x = x + jnp.where(rowmod >= sh, pltpu.roll(x, sh, 0), 0.0)
    return x


def _lane_col(x, c, lane):
    return _rowsum(jnp.where(lane == c, x, 0.0))


def _hg_gates(hq, hf, lb):
    sg = _sig(hf)
    nsg = _sig(-hf)
    f = lb + (1.0 - lb) * sg
    g = jnp.log(f)
    k = (1.0 - lb) * nsg
    sq = _sig(hq)
    return hq * sq, g, k, f, sg, nsg, sq


def _hg_prep(hq_ref, hf_ref, lg_ref, b_sc, k_sc):
    R = hq_ref.shape[0]
    G = R // CHUNK
    lb = _sig(lg_ref[0:1, :] - lg_ref[1:2, :])
    hq = hq_ref[...]
    q, g, k, f, sg, nsg, sq = _hg_gates(hq, hf_ref[...], lb)
    rowmod = _iota((R, HALF), 0) & (CHUNK - 1)
    b = _chunk_cumsum(g, rowmod)
    last8 = _iota((8, HALF), 0) == 7
    bl_rows = [_colsum(jnp.where(last8, b[CHUNK * c + CHUNK - 8:CHUNK * (c + 1)], 0.0)) for c in range(G)]
    bl3 = jnp.stack([r[:, 128 * h:128 * (h + 1)] for r in bl_rows for h in range(HEADS)], axis=0)
    b3, k3 = _heads_to_batch(b, 128), _heads_to_batch(k, 128)
    b_sc[...] = b3
    k_sc[...] = k3
    return dict(G=G, lb=lb, hq=hq, f=f, sg=sg, nsg=nsg, sq=sq, rowmod=rowmod, q3=_heads_to_batch(q, 128), k3=k3, b3=b3,
                bl3=bl3)


def _hg_diag_tiles(b_sc, b3, r0, rowi):
    bi = b3[:, r0:r0 + SUB]
    return [jnp.exp(jnp.where(rowi >= s, bi - b_sc[:, r0 + s:r0 + s + 1, :], NEG)) for s in range(SUB)]


def _hgrn2_fwd(u, lb_logits, comms=None):
    T = u.shape[0]
    G = min(GC, T // CHUNK)
    R, B, N = G * CHUNK, G * HEADS, T // CHUNK

    def body(hq_ref, hf_ref, hv_ref, lg_ref, o_ref, st_ref, S_ref, b_sc, k_sc, v_sc):
        @pl.when(pl.program_id(0) == 0)
        def _():
            S_ref[...] = jnp.zeros_like(S_ref)

        pz = _hg_prep(hq_ref, hf_ref, lg_ref, b_sc, k_sc)
        q3, k3, b3, bl3 = pz["q3"], pz["k3"], pz["b3"], pz["bl3"]
        v3 = _heads_to_batch(hv_ref[...], 128)
        v_sc[...] = v3
        stloc = _bdot(v3, k3 * jnp.exp(bl3 - b3), BTN).reshape(G, HEADS, 128, 128)
        dec = jnp.exp(bl3).reshape(G, HEADS, 1, 128)
        ST = S_ref[...]
        sts = []
        for c in range(G):
            sts.append(ST)
            ST = ST * dec[c] + stloc[c]
        S_ref[...] = ST
        st4 = jnp.stack(sts, axis=0)
        st_ref[...] = st4
        o = _bdot(q3 * jnp.exp(b3), st4.reshape(B, 128, 128), BNT)
        ones = jnp.ones((128, 128), F32)
        rowi = _iota((1, SUB, 128), 1)
        outs = []
        for i in range(CHUNK // SUB):
            r0 = SUB * i
            qi = q3[:, r0:r0 + SUB]
            oi = o[:, r0:r0 + SUB]
            if i > 0:
                r = b_sc[:, r0 - 1:r0, :]
                qe = qi * jnp.exp(b3[:, r0:r0 + SUB] - r)
                ke = k3[:, :r0] * jnp.exp(r - b3[:, :r0])
                oi = oi + _bdot(_bdot(qe, ke, BNT), v3[:, :r0], BNN)
            tiles = _hg_diag_tiles(b_sc, b3, r0, rowi)
            ms = [qi * (k_sc[:, r0 + s:r0 + s + 1, :] * tiles[s]) for s in range(SUB)]
            Rm = _dot(jnp.concatenate(ms, axis=1).reshape(B * SUB * SUB, 128), ones).reshape(B, SUB * SUB, 128)
            for s in range(SUB):
                oi = oi + Rm[:, SUB * s:SUB * s + SUB] * v_sc[:, r0 + s:r0 + s + 1, :]
            outs.append(oi)
        o_ref[...] = _batch_to_heads(jnp.concatenate(outs, axis=1))

    blk = lambda c: pl.BlockSpec((R, HALF), lambda n, c=c: (n, c // 4))
    return _hosted_call(
        body, comms, name="hgrn2_fwd", grid=(N // G,),
        in_specs=[blk(C_HQ), blk(C_HF), blk(C_HV), _full(lb_logits.shape)],
        out_specs=[pl.BlockSpec((R, HALF), lambda n: (n, 0)),
                   pl.BlockSpec((G, HEADS, 128, 128), lambda n: (n, 0, 0, 0))],
        out_shape=[_sds((T, HALF)), _sds((N, HEADS, 128, 128))],
        scratch_shapes=[pltpu.VMEM((HEADS, 128, 128), F32)] + [pltpu.VMEM((B, CHUNK, 128), F32)] * 3,
        args=(u, u, u, lb_logits))


def _hgrn2_bwd(u, lb_logits, do, states, comms=None):
    T = u.shape[0]
    G = min(GC, T // CHUNK)
    R, B, NG = G * CHUNK, G * HEADS, T // (G * CHUNK)

    def body(hq_ref, hf_ref, hv_ref, lg_ref, do_ref, st_ref, dhq_ref, dhf_ref, dhv_ref, dlb_ref,
             dS_ref, b_sc, k_sc, v_sc, q_sc, do_sc):
        @pl.when(pl.program_id(0) == 0)
        def _():
            dS_ref[...] = jnp.zeros_like(dS_ref)
            dlb_ref[...] = jnp.zeros_like(dlb_ref)

        pz = _hg_prep(hq_ref, hf_ref, lg_ref, b_sc, k_sc)
        q3, k3, b3, bl3, lb = pz["q3"], pz["k3"], pz["b3"], pz["bl3"], pz["lb"]
        v3 = _heads_to_batch(hv_ref[...], 128)
        v_sc[...] = v3
        do3 = _heads_to_batch(do_ref[...], 128)
        q_sc[...] = q3
        do_sc[...] = do3
        st3 = st_ref[...].reshape(B, 128, 128)
        eb = jnp.exp(b3)
        ebl = jnp.exp(bl3 - b3)
        qt = q3 * eb
        kl = k3 * ebl
        dstloc = _bdot(do3, qt, BTN).reshape(G, HEADS, 128, 128)
        dec = jnp.exp(bl3).reshape(G, HEADS, 1, 128)
        dST = dS_ref[...]
        dsts = [None] * G
        for c in reversed(range(G)):
            dsts[c] = dST
            dST = dST * dec[c] + dstloc[c]
        dS_ref[...] = dST
        dst3 = jnp.stack(dsts, axis=0).reshape(B, 128, 128)
        dqt = _bdot(do3, st3, BNN)
        dkl = _bdot(v3, dst3, BNN)
        dv_acc = _bdot(kl, dst3, BNT)
        ones = jnp.ones((128, 128), F32)
        rowi = _iota((1, SUB, 128), 1)
        dq_parts, dk_parts, dv_parts = [], [], []
        dk_in = jnp.zeros((B, CHUNK, 128), F32)
        for i_s in range(CHUNK // SUB):
            r0 = SUB * i_s
            qi = q3[:, r0:r0 + SUB]
            doi = do3[:, r0:r0 + SUB]
            dqi = jnp.zeros((B, SUB, 128), F32)
            if i_s > 0:
                r = b_sc[:, r0 - 1:r0, :]
                eq = jnp.exp(b3[:, r0:r0 + SUB] - r)
                ek = jnp.exp(r - b3[:, :r0])
                qe = qi * eq
                ke = k3[:, :r0] * ek
                a_off = _bdot(qe, ke, BNT)
                p_off = _bdot(doi, v3[:, :r0], BNT)
                pad = jnp.zeros((B, CHUNK - r0, 128), F32)
                dv_acc = dv_acc + jnp.concatenate([_bdot(a_off, doi, BTN), pad], axis=1)
                dqi = dqi + _bdot(p_off, ke, BNN) * eq
                dk_in = dk_in + jnp.concatenate([_bdot(p_off, qe, BTN) * ek, pad], axis=1)
            bi = b3[:, r0:r0 + SUB]
            ki, vi = k3[:, r0:r0 + SUB], v3[:, r0:r0 + SUB]
            tiles = _hg_diag_tiles(b_sc, b3, r0, rowi)
            kts = [k_sc[:, r0 + s:r0 + s + 1, :] * tiles[s] for s in range(SUB)]
            ps = [doi * v_sc[:, r0 + s:r0 + s + 1, :] for s in range(SUB)]
            qts = [q_sc[:, r0 + t:r0 + t + 1, :] * jnp.exp(jnp.where(rowi <= t, b_sc[:, r0 + t:r0 + t + 1, :] - bi, NEG))
                   for t in range(SUB)]
            mst = [ki * qts[t] for t in range(SUB)]
            pst = [vi * do_sc[:, r0 + t:r0 + t + 1, :] for t in range(SUB)]
            n_r = 3 * SUB * SUB
            Rm = _dot(jnp.concatenate(ps + mst + pst, axis=1).reshape(B * n_r, 128), ones).reshape(B, n_r, 128)
            dki = jnp.zeros((B, SUB, 128), F32)
            dvi = jnp.zeros((B, SUB, 128), F32)
            for j in range(SUB):
                dqi = dqi + Rm[:, SUB * j:SUB * (j + 1)] * kts[j]
                dvi = dvi + Rm[:, SUB * (SUB + j):SUB * (SUB + j + 1)] * do_sc[:, r0 + j:r0 + j + 1, :]
                dki = dki + Rm[:, SUB * (2 * SUB + j):SUB * (2 * SUB + j + 1)] * qts[j]
            dq_parts.append(dqi)
            dk_parts.append(dki)
            dv_parts.append(dvi)
        dq_in = jnp.concatenate(dq_parts, axis=1)
        dk_in = dk_in + jnp.concatenate(dk_parts, axis=1)
        dv_acc = dv_acc + jnp.concatenate(dv_parts, axis=1)
        db = qt * dqt + q3 * dq_in - k3 * dk_in - kl * dkl
        last = jnp.sum(kl * dkl, axis=1, keepdims=True) + jnp.exp(bl3) * jnp.sum(st3 * dst3, axis=1, keepdims=True)
        db = db + jnp.where(_iota((1, CHUNK, 1), 1) == CHUNK - 1, last, 0.0)
        dg = _chunk_cumsum(_batch_to_heads(db), pz["rowmod"], reverse=True)
        dq_tot = _batch_to_heads(dqt * eb + dq_in)
        dk_tot = _batch_to_heads(dkl * ebl + dk_in)
        common = dg / pz["f"] - dk_tot
        dhf_ref[...] = ((1.0 - lb) * pz["sg"] * pz["nsg"] * common).astype(dhf_ref.dtype)
        dl0 = _colsum(pz["nsg"] * common) * lb * (1.0 - lb)
        dlb_ref[0:1, :] += dl0
        dlb_ref[1:2, :] -= dl0
        dhq_ref[...] = (dq_tot * _dsilu(pz["hq"], pz["sq"])).astype(dhq_ref.dtype)
        dhv_ref[...] = _batch_to_heads(dv_acc).astype(dhv_ref.dtype)

    rev = lambda c: pl.BlockSpec((R, HALF), lambda i, c=c: (NG - 1 - i, c // 4))
    rev0 = pl.BlockSpec((R, HALF), lambda i: (NG - 1 - i, 0))
    return _hosted_call(
        body, comms, name="hgrn2_bwd", grid=(NG,),
        in_specs=[rev(C_HQ), rev(C_HF), rev(C_HV), _full(lb_logits.shape), rev0,
                  pl.BlockSpec((G, HEADS, 128, 128), lambda i: (NG - 1 - i, 0, 0, 0))],
        out_specs=[rev0, rev0, rev0, _full((2, HALF))],
        out_shape=[_sds((T, HALF), _MXU)] * 3 + [_sds((2, HALF))],
        scratch_shapes=[pltpu.VMEM((HEADS, 128, 128), F32)] + [pltpu.VMEM((B, CHUNK, 128), F32)] * 5,
        args=(u, u, u, lb_logits, do, states))


def _lanes_to_batch_cols(x, lane):
    G = x.shape[0] // CHUNK
    cols = [_lane_col(x, 4 + h, lane).reshape(G, CHUNK, 1) for h in range(HEADS)]
    return jnp.stack(cols, axis=1).reshape(G * HEADS, CHUNK, 1)


def _row_scalars(rows):
    lane = _iota((1, 128), 1)
    return jnp.stack([_rowsum(jnp.where(lane == 4 + h, r, 0.0)) for r in rows for h in range(HEADS)], axis=0)


def _ml_gates(gates):
    R = gates.shape[0]
    lane = _iota((R, 128), 1)
    rowmod = _iota((R, 128), 0) & (CHUNK - 1)
    lf = jnp.minimum(gates, 0.0) - jnp.log(1.0 + jnp.exp(-jnp.abs(gates)))
    g_all = _chunk_cumsum(lf, rowmod)
    x_all = pltpu.roll(gates, 4, 1) - g_all
    return g_all, x_all, lane, rowmod


def _ml_chunk_rows(g_all, x_all, mprev, g):
    gl = g_all[CHUNK * g + CHUNK - 8:CHUNK * (g + 1)]
    gl = _colsum(jnp.where(_iota((8, 128), 0) == 7, gl, 0.0))
    a = gl + x_all[CHUNK * g:CHUNK * (g + 1)]
    m_new = jnp.maximum(gl + mprev, jnp.max(a, axis=0, keepdims=True))
    return m_new, jnp.exp(gl + mprev - m_new), jnp.exp(a - m_new)


def _ml_batched(q3, k3, v3, g_all, x_all, lane, C3, n3, mprev3):
    G = g_all.shape[0] // CHUNK
    gcol3 = _lanes_to_batch_cols(g_all, lane)
    onehot = jnp.where(_iota((G, 8, 128), 1) + 4 == _iota((G, 8, 128), 2), 1.0, 0.0).astype(F32)
    rows = _bdotx(onehot, x_all.reshape(G, CHUNK, 128), BNT)
    sub = _iota((G, 8, CHUNK), 1)
    row3 = jnp.stack([jnp.sum(jnp.where(sub == h, rows, 0.0), axis=1, keepdims=True) for h in range(HEADS)],
                     axis=1).reshape(G * HEADS, 1, CHUNK)
    causal = _iota((1, CHUNK, CHUNK), 1) >= _iota((1, CHUNK, CHUNK), 2)
    dmat = jnp.where(causal, gcol3 + row3, NEG)
    m_inter = gcol3 + mprev3
    m_t = jnp.maximum(m_inter, jnp.max(dmat, axis=2, keepdims=True))
    wi = jnp.exp(dmat - m_t)
    wn = jnp.exp(m_inter - m_t)
    s3 = _bdot(q3, k3, BNT) * wi
    qc = _bdot(q3, C3, BNN)
    qn = jnp.sum(q3 * n3, axis=2, keepdims=True)
    num = _bdot(s3, v3, BNN) + wn * qc
    den = jnp.sum(s3, axis=2, keepdims=True) + wn * qn
    floor = jnp.exp(-m_t)
    return dict(wi=wi, wn=wn, s=s3, qc=qc, qn=qn, num=num, den=den, floor=floor, nrm=jnp.maximum(jnp.abs(den), floor))


def _mlstm_fwd(qkc, u, comms=None):
    T = u.shape[0]
    G = min(GC, T // CHUNK)
    R = G * CHUNK
    N = T // CHUNK

    def body(qk_ref, v_ref, g_ref, h_ref, cst_ref, nst_ref, mst_ref, C_ref, n_ref, m_ref):
        @pl.when(pl.program_id(0) == 0)
        def _():
            C_ref[...] = jnp.zeros_like(C_ref)
            n_ref[...] = jnp.zeros_like(n_ref)
            m_ref[...] = jnp.zeros_like(m_ref)

        g_all, x_all, lane, _ = _ml_gates(g_ref[...])
        m_row = m_ref[...]
        mprev_rows, wo_rows, ws_parts = [], [], []
        for g in range(G):
            mprev_rows.append(m_row)
            m_row, wo, ws = _ml_chunk_rows(g_all, x_all, m_row, g)
            wo_rows.append(wo)
            ws_parts.append(ws)
        m_ref[...] = m_row
        mst_ref[...] = jnp.stack(mprev_rows, axis=0)
        ws3 = _lanes_to_batch_cols(jnp.concatenate(ws_parts, axis=0), lane)
        wo4 = _row_scalars(wo_rows).reshape(G, HEADS, 1, 1)
        q3 = _heads_to_batch(qk_ref[:, :256] * ML_SCALE, ML_DQK)
        k3 = _heads_to_batch(qk_ref[:, 256:], ML_DQK)
        v3 = _heads_to_batch(v_ref[...], 128)
        kw = k3 * ws3
        cloc = _bdot(kw, v3, BTN).reshape(G, HEADS, ML_DQK, 128)
        nloc = jnp.sum(kw, axis=1, keepdims=True).reshape(G, HEADS, 1, ML_DQK)
        C, nn = C_ref[...], n_ref[...]
        cs, ns = [], []
        for g in range(G):
            cs.append(C)
            ns.append(nn)
            C = wo4[g] * C + cloc[g]
            nn = wo4[g] * nn + nloc[g]
        C_ref[...] = C
        n_ref[...] = nn
        c4, n4 = jnp.stack(cs, axis=0), jnp.stack(ns, axis=0)
        cst_ref[...] = c4
        nst_ref[...] = n4
        r = _ml_batched(q3, k3, v3, g_all, x_all, lane, c4.reshape(G * HEADS, ML_DQK, 128),
                        n4.reshape(G * HEADS, 1, ML_DQK), _row_scalars(mprev_rows))
        h_ref[...] = _batch_to_heads(r["num"] / r["nrm"])

    return _hosted_call(
        body, comms, name="mlstm_fwd", grid=(N // G,),
        in_specs=[pl.BlockSpec((R, HALF), lambda n: (n, 0)), pl.BlockSpec((R, HALF), lambda n: (n, C_MV // 4)),
                  pl.BlockSpec((R, 128), lambda n: (n, C_GATES))],
        out_specs=[pl.BlockSpec((R, HALF), lambda n: (n, 0)),
                   pl.BlockSpec((G, HEADS, ML_DQK, 128), lambda n: (n, 0, 0, 0)),
                   pl.BlockSpec((G, HEADS, 1, ML_DQK), lambda n: (n, 0, 0, 0)),
                   pl.BlockSpec((G, 1, 128), lambda n: (n, 0, 0))],
        out_shape=[_sds((T, HALF)), _sds((N, HEADS, ML_DQK, 128)), _sds((N, HEADS, 1, ML_DQK)), _sds((N, 1, 128))],
        scratch_shapes=[pltpu.VMEM((HEADS, ML_DQK, 128), F32), pltpu.VMEM((HEADS, 1, ML_DQK), F32),
                        pltpu.VMEM((1, 128), F32)],
        args=(qkc, u, u))


def _mlstm_bwd(qkc, u, dh, cst, nst, mst):
    T = u.shape[0]
    G = min(GC, T // CHUNK)
    R = G * CHUNK
    NG = T // R

    def body(qk_ref, v_ref, g_ref, dh_ref, cst_ref, nst_ref, mst_ref, dqk_ref, dv_ref, dgt_ref, dC_ref, dn_ref):
        @pl.when(pl.program_id(0) == 0)
        def _():
            dC_ref[...] = jnp.zeros_like(dC_ref)
            dn_ref[...] = jnp.zeros_like(dn_ref)

        B = G * HEADS
        gates = g_ref[...]
        g_all, x_all, lane, rowmod = _ml_gates(gates)
        mprev_rows = [mst_ref[g] for g in range(G)]
        wo_rows, ws_parts = [], []
        for g in range(G):
            _, wo, ws = _ml_chunk_rows(g_all, x_all, mprev_rows[g], g)
            wo_rows.append(wo)
            ws_parts.append(ws)
        ws3 = _lanes_to_batch_cols(jnp.concatenate(ws_parts, axis=0), lane)
        wo3 = _row_scalars(wo_rows)
        wo4 = wo3.reshape(G, HEADS, 1, 1)
        q3 = _heads_to_batch(qk_ref[:, :256] * ML_SCALE, ML_DQK)
        k3 = _heads_to_batch(qk_ref[:, 256:], ML_DQK)
        v3 = _heads_to_batch(v_ref[...], 128)
        dh3 = _heads_to_batch(dh_ref[...], 128)
        C3 = cst_ref[...].reshape(B, ML_DQK, 128)
        n3 = nst_ref[...].reshape(B, 1, ML_DQK)
        r = _ml_batched(q3, k3, v3, g_all, x_all, lane, C3, n3, _row_scalars(mprev_rows))
        wn, s3 = r["wn"], r["s"]
        inv = 1.0 / r["nrm"]
        dnum = dh3 * inv
        dnrm = -jnp.sum(dh3 * (r["num"] * inv), axis=2, keepdims=True) * inv
        dden = jnp.where(jnp.abs(r["den"]) > r["floor"], dnrm * jnp.sign(r["den"]), 0.0)
        ds = _bdot(dnum, v3, BNT) + dden
        dqk = ds * r["wi"]
        dd = ds * s3
        qw = q3 * wn
        dcloc = _bdot(qw, dnum, BTN).reshape(G, HEADS, ML_DQK, 128)
        dnloc = jnp.sum(qw * dden, axis=1, keepdims=True).reshape(G, HEADS, 1, ML_DQK)
        dC, dn = dC_ref[...], dn_ref[...]
        dcs, dns = [None] * G, [None] * G
        for g in reversed(range(G)):
            dcs[g], dns[g] = dC, dn
            dC = wo4[g] * dC + dcloc[g]
            dn = wo4[g] * dn + dnloc[g]
        dC_ref[...] = dC
        dn_ref[...] = dn
        dC3 = jnp.stack(dcs, axis=0).reshape(B, ML_DQK, 128)
        dn3 = jnp.stack(dns, axis=0).reshape(B, 1, ML_DQK)
        dk_st = ws3 * (_bdot(v3, dC3, BNT) + dn3)
        dq = _bdot(dqk, k3, BNN) + wn * (_bdot(dnum, C3, BNT) + dden * n3)
        dk = _bdot(dqk, q3, BTN) + dk_st
        dv = _bdot(s3, dnum, BTN) + ws3 * _bdot(k3, dC3, BNN)
        dv_ref[...] = _batch_to_heads(dv).astype(dv_ref.dtype)
        dqk_ref[...] = jnp.concatenate([_batch_to_heads(dq * ML_SCALE), _batch_to_heads(dk)], axis=1)
        e_col = wn * (jnp.sum(dnum * r["qc"], axis=2, keepdims=True) + dden * r["qn"])
        c_col = jnp.sum(k3 * dk_st, axis=2, keepdims=True)
        z = wo3 * (jnp.sum(dC3 * C3, axis=(1, 2), keepdims=True) + jnp.sum(dn3 * n3, axis=(1, 2), keepdims=True))
        dd_cols = _bdotx(dd, jnp.ones((B, CHUNK, 128), F32), BTN)[:, :, 0:1]
        last = _iota((1, CHUNK, 1), 1) == CHUNK - 1
        dg3 = jnp.sum(dd, axis=2, keepdims=True) - dd_cols + e_col - c_col
        dg3 = dg3 + jnp.where(last, jnp.sum(c_col, axis=1, keepdims=True) + z, 0.0)
        di3 = dd_cols + c_col

        def to_lanes(x3, first):
            x4 = x3.reshape(G, HEADS, CHUNK, 1)
            return sum(jnp.where(lane == first + h, x4[:, h].reshape(R, 1), 0.0) for h in range(HEADS))

        dlf = _chunk_cumsum(to_lanes(dg3, 4), rowmod, reverse=True)
        dgt_ref[...] = (to_lanes(di3, 0) + dlf * _sig(-gates)).astype(dgt_ref.dtype)

    rev = lambda w, c: pl.BlockSpec((R, w), lambda i, c=c: (NG - 1 - i, c))
    st = lambda *s: pl.BlockSpec((G,) + s, lambda i: (NG - 1 - i,) + (0,) * len(s))
    return pl.pallas_call(
        body, name="mlstm_bwd", grid=(NG,),
        in_specs=[rev(HALF, 0), rev(HALF, C_MV // 4), rev(128, C_GATES), rev(HALF, 0),
                  st(HEADS, ML_DQK, 128), st(HEADS, 1, ML_DQK), st(1, 128)],
        out_specs=[rev(HALF, 0), rev(HALF, 0), rev(128, 0)],
        out_shape=[_sds((T, HALF)), _sds((T, HALF), _MXU), _sds((T, 128), _MXU)],
        scratch_shapes=[pltpu.VMEM((HEADS, ML_DQK, 128), F32), pltpu.VMEM((HEADS, 1, ML_DQK), F32)],
        compiler_params=_cp("arbitrary"))(qkc, u, u, dh, cst, nst, mst)


def _head_norm(o):
    rs_parts, r_parts = [], []
    for h in range(HEADS):
        oh = o[:, 128 * h:128 * (h + 1)]
        rs = lax.rsqrt(jnp.mean(oh * oh, axis=-1, keepdims=True) + RMS_EPS)
        rs_parts.append(rs)
        r_parts.append(oh * rs)
    return jnp.concatenate(r_parts, axis=1), rs_parts


def _out_proj_ln(x, u, o_hg, h_ml, g_hg, g_ml, w_out, ln_g, ln_b, tm):
    T = x.shape[0]

    def body(x_ref, hgate_ref, mo_ref, ohg_ref, hml_ref, ghg_ref, gml_ref, w_ref, g_ref, b_ref,
             m_ref, z_ref, x1_ref):
        hgate = hgate_ref[...]
        a = _head_norm(ohg_ref[...])[0] * ghg_ref[...] * (hgate * _sig(hgate))
        b = _head_norm(hml_ref[...])[0] * gml_ref[...] * _sig(mo_ref[...])
        m = jnp.concatenate([a, b], axis=1)
        m_ref[...] = m.astype(m_ref.dtype)
        z = ALPHA * x_ref[...] + _dot(m, w_ref[...])
        z_ref[...] = z
        x1_ref[...] = _ln_fwd(z, g_ref[...], b_ref[...])[0]

    return pl.pallas_call(
        body, name="out_proj_ln1", grid=(T // tm,),
        in_specs=[_row(tm, D_MODEL), _row(tm, HALF, C_HGATE // 4), _row(tm, HALF, C_MO // 4),
                  _row(tm, HALF), _row(tm, HALF), _full(g_hg.shape), _full(g_ml.shape),
                  _full(w_out.shape), _full(ln_g.shape), _full(ln_b.shape)],
        out_specs=[_row(tm, D_MODEL)] * 3,
        out_shape=[_sds((T, D_MODEL), _MXU), _sds((T, D_MODEL)), _sds((T, D_MODEL))],
        compiler_params=_cp("parallel"))(x, u, u, o_hg, h_ml, g_hg, g_ml, w_out, ln_g, ln_b)


def _ffn_ln(x1, wg, wu, wd, ln_g, ln_b, tm):
    T = x1.shape[0]

    def body(x_ref, wg_ref, wu_ref, wd_ref, g_ref, b_ref, z_ref, x2_ref, a_ref, bb_ref, h_ref):
        x = x_ref[...]
        a = _dot(x, wg_ref[...])
        bb = _dot(x, wu_ref[...])
        hh = a * _sig(a) * bb
        a_ref[...] = a.astype(a_ref.dtype)
        bb_ref[...] = bb.astype(bb_ref.dtype)
        h_ref[...] = hh.astype(h_ref.dtype)
        z = ALPHA * x + _dot(hh, wd_ref[...])
        z_ref[...] = z
        x2_ref[...] = _ln_fwd(z, g_ref[...], b_ref[...])[0]

    return pl.pallas_call(
        body, name="ffn_ln2", grid=(T // tm,),
        in_specs=[_row(tm, D_MODEL), _full(wg.shape), _full(wu.shape), _full(wd.shape),
                  _full(ln_g.shape), _full(ln_b.shape)],
        out_specs=[_row(tm, D_MODEL)] * 2 + [_row(tm, D_FF)] * 3,
        out_shape=[_sds((T, D_MODEL))] * 2 + [_sds((T, D_FF), _MXU)] * 3,
        compiler_params=_cp("parallel"))(x1, wg, wu, wd, ln_g, ln_b)


def _ple_loss_ln2_bwd(x2, z2, p, tgt, wpg, bpg, wpp, ln_g, ln_b, tm):
    T = x2.shape[0]

    def body(x2_ref, z_ref, p_ref, t_ref, wpg_ref, bpg_ref, wpp_ref, g_ref, b_ref,
             de_ref, dgp_ref, dz_ref, loss_ref, dbpg_ref, dg_ref, db_ref):
        @pl.when(pl.program_id(0) == 0)
        def _():
            for r in (loss_ref, dbpg_ref, dg_ref, db_ref):
                r[...] = jnp.zeros_like(r)

        x2 = x2_ref[...]
        gate = _sig(_dot(x2, wpg_ref[...]) + bpg_ref[...])
        e = _dot(p_ref[...], wpp_ref[...])
        err = x2 + gate * e - t_ref[...]
        loss_ref[...] += _colsum(err * err)
        dy = err * (1.0 / D_MODEL)
        de_ref[...] = (dy * gate).astype(de_ref.dtype)
        dgp = dy * e * gate * (1.0 - gate)
        dgp_ref[...] = dgp.astype(dgp_ref.dtype)
        dbpg_ref[...] += _colsum(dgp)
        dx2 = dy + _dot(dgp, wpg_ref[...], NT)
        _, xhat, rstd = _ln_fwd(z_ref[...], g_ref[...], b_ref[...])
        dg_ref[...] += _colsum(dx2 * xhat)
        db_ref[...] += _colsum(dx2)
        dz_ref[...] = _ln_bwd(dx2, xhat, rstd, g_ref[...])

    vec = _full((1, D_MODEL))
    return pl.pallas_call(
        body, name="ple_loss_ln2_bwd", grid=(T // tm,),
        in_specs=[_row(tm, D_MODEL), _row(tm, D_MODEL), _row(tm, PLE_DIM), _row(tm, D_MODEL),
                  _full(wpg.shape), vec, _full(wpp.shape), vec, vec],
        out_specs=[_row(tm, D_MODEL)] * 3 + [vec] * 4,
        out_shape=[_sds((T, D_MODEL), _MXU)] * 2 + [_sds((T, D_MODEL))] + [_sds((1, D_MODEL))] * 4,
        compiler_params=_cp("arbitrary"))(x2, z2, p, tgt, wpg, bpg, wpp, ln_g, ln_b)


def _ffn_bwd_ln1_bwd(a_pre, b_pre, z1, dz2, wg, wu, wd, ln_g, ln_b, tm, comms=None):
    T = z1.shape[0]

    def body(a_ref, bb_ref, z_ref, dz2_ref, wg_ref, wu_ref, wd_ref, g_ref, b_ref,
             da_ref, dbb_ref, dz1_ref, dg_ref, db_ref):
        @pl.when(pl.program_id(0) == 0)
        def _():
            dg_ref[...] = jnp.zeros_like(dg_ref)
            db_ref[...] = jnp.zeros_like(db_ref)

        dz2 = dz2_ref[...]
        a = a_ref[...].astype(F32)
        bb = bb_ref[...].astype(F32)
        sa = _sig(a)
        act = a * sa
        dh = _dot(dz2, wd_ref[...], NT)
        da = (dh * bb * _dsilu(a, sa)).astype(da_ref.dtype)
        dbb = (dh * act).astype(dbb_ref.dtype)
        da_ref[...] = da
        dbb_ref[...] = dbb
        dx1 = ALPHA * dz2 + _dot(da, wg_ref[...], NT) + _dot(dbb, wu_ref[...], NT)
        _, xhat, rstd = _ln_fwd(z_ref[...], g_ref[...], b_ref[...])
        dg_ref[...] += _colsum(dx1 * xhat)
        db_ref[...] += _colsum(dx1)
        dz1_ref[...] = _ln_bwd(dx1, xhat, rstd, g_ref[...])

    vec = _full((1, D_MODEL))
    return _hosted_call(
        body, comms, name="ffn_bwd_ln1_bwd", grid=(T // tm,),
        in_specs=[_row(tm, D_FF)] * 2 + [_row(tm, D_MODEL)] * 2 + [_full(wg.shape), _full(wu.shape), _full(wd.shape), vec, vec],
        out_specs=[_row(tm, D_FF)] * 2 + [_row(tm, D_MODEL), vec, vec],
        out_shape=[_sds((T, D_FF), _MXU)] * 2 + [_sds((T, D_MODEL)), _sds((1, D_MODEL)), _sds((1, D_MODEL))],
        scratch_shapes=[], args=(a_pre, b_pre, z1, dz2, wg, wu, wd, ln_g, ln_b))


def _out_proj_bwd(dz1, u, o_hg, h_ml, g_hg, g_ml, w_out, tm):
    T = dz1.shape[0]

    def body(dz_ref, hgate_ref, mo_ref, ohg_ref, hml_ref, ghg_ref, gml_ref, w_ref,
             dohg_ref, dhml_ref, dhgate_ref, dmo_ref, dghg_ref, dgml_ref):
        @pl.when(pl.program_id(0) == 0)
        def _():
            dghg_ref[...] = jnp.zeros_like(dghg_ref)
            dgml_ref[...] = jnp.zeros_like(dgml_ref)

        dm = _dot(dz_ref[...], w_ref[...], NT)

        def half(dmh, o, gvec, gate_val, dgate_fac, do_ref, dgate_ref, dgvec_ref):
            r, rs = _head_norm(o)
            dgate_ref[...] = (dmh * r * gvec * dgate_fac).astype(dgate_ref.dtype)
            dn = dmh * gate_val
            dgvec_ref[...] += _colsum(dn * r)
            dr = dn * gvec
            parts = []
            for h in range(HEADS):
                sl = slice(128 * h, 128 * (h + 1))
                parts.append(rs[h] * (dr[:, sl] - r[:, sl] * jnp.mean(dr[:, sl] * r[:, sl], axis=-1, keepdims=True)))
            do_ref[...] = jnp.concatenate(parts, axis=1)

        hg = hgate_ref[...]
        shg = _sig(hg)
        half(dm[:, :HALF], ohg_ref[...], ghg_ref[...], hg * shg, _dsilu(hg, shg), dohg_ref, dhgate_ref, dghg_ref)
        smo = _sig(mo_ref[...])
        half(dm[:, HALF:], hml_ref[...], gml_ref[...], smo, smo * (1.0 - smo), dhml_ref, dmo_ref, dgml_ref)

    vec = _full((1, HALF))
    return pl.pallas_call(
        body, name="out_proj_bwd", grid=(T // tm,),
        in_specs=[_row(tm, D_MODEL), _row(tm, HALF, C_HGATE // 4), _row(tm, HALF, C_MO // 4),
                  _row(tm, HALF), _row(tm, HALF), vec, vec, _full(w_out.shape)],
        out_specs=[_row(tm, HALF)] * 4 + [vec, vec],
        out_shape=[_sds((T, HALF))] * 2 + [_sds((T, HALF), _MXU)] * 2 + [_sds((1, HALF))] * 2,
        compiler_params=_cp("arbitrary"))(dz1, u, u, o_hg, h_ml, g_hg, g_ml, w_out)


def _conv_bwd(u, pre, dqkc, cw, tm):
    T = u.shape[0]
    hb = tm // 8
    nb = T // 8

    def body(x_ref, xh_ref, pre_ref, preh_ref, d_ref, dh_ref, w_ref, dx_ref, dw_ref, db_ref):
        i = pl.program_id(0)

        @pl.when(i == 0)
        def _():
            dw_ref[...] = jnp.zeros_like(dw_ref)
            db_ref[...] = jnp.zeros_like(db_ref)

        def dpre_of(pre, d):
            return d * _dsilu(pre, _sig(pre))

        rowi = _iota((8, HALF), 0)
        dpre = dpre_of(pre_ref[...], d_ref[...])
        dpre_next = jnp.where(i < pl.num_programs(0) - 1, dpre_of(preh_ref[...], dh_ref[...]), 0.0)
        x = x_ref[...]
        xprev = jnp.where(i > 0, xh_ref[...], 0.0)
        dx = dpre * w_ref[3:4, :]
        db_ref[...] += _colsum(dpre)
        dws = [None] * 4
        dws[3] = _colsum(dpre * x)
        for j in (1, 2, 3):
            dx = dx + _shift_rows_up(dpre, dpre_next, j, rowi) * w_ref[3 - j:4 - j, :]
            dws[3 - j] = _colsum(dpre * _shift_rows(x, xprev, j, rowi))
        dx_ref[...] = dx.astype(dx_ref.dtype)
        dw_ref[...] += jnp.concatenate(dws, axis=0)

    cur = lambda blk: pl.BlockSpec((tm, HALF), lambda i, blk=blk: (i, blk))
    nxt = pl.BlockSpec((8, HALF), lambda i: (jnp.minimum((i + 1) * hb, nb - 1), 0))
    return pl.pallas_call(
        body, name="conv_bwd", grid=(T // tm,),
        in_specs=[cur(C_MQK // 4), pl.BlockSpec((8, HALF), lambda i: (jnp.maximum(i * hb - 1, 0), C_MQK // 4)),
                  cur(0), nxt, cur(0), nxt, _full(cw.shape)],
        out_specs=[cur(0), _full((4, HALF)), _full((1, HALF))],
        out_shape=[_sds((T, HALF), _MXU), _sds((4, HALF)), _sds((1, HALF))],
        compiler_params=_cp("arbitrary"))(u, u, pre, pre, dqkc, dqkc, cw)


def _du_specs(rows):
    return [pl.BlockSpec((rows, w), lambda i: (i, 0)) for w in DU_WIDTHS]


def _in_proj_bwd(dz1, du_parts, w, tm, comms=None):
    T = dz1.shape[0]

    def body(dz_ref, *refs):
        du = jnp.concatenate([r[...] for r in refs[:8]], axis=1)
        refs[9][...] = ALPHA * dz_ref[...] + _dot(du, refs[8][...], NT)

    (dx,), got = _hosted_call(
        body, comms, name="in_proj_bwd", grid=(T // tm,),
        in_specs=[_row(tm, D_MODEL)] + _du_specs(tm) + [_full(w.shape)],
        out_specs=[_row(tm, D_MODEL)], out_shape=[_sds((T, D_MODEL))], scratch_shapes=[], args=(dz1, *du_parts, w))
    return dx, got


def _wgrad(a, b, name, tm, tn, tk):
    T, M = a.shape
    N = b.shape[1]
    tm, tn, tk = min(tm, M), min(tn, N), min(tk, T)
    nk = T // tk

    def body(a_ref, b_ref, o_ref, acc_ref):
        kk = pl.program_id(2)

        @pl.when(kk == 0)
        def _():
            acc_ref[...] = jnp.zeros_like(acc_ref)

        acc_ref[...] += _dot(a_ref[...], b_ref[...], TN)

        @pl.when(kk == nk - 1)
        def _():
            o_ref[...] = acc_ref[...].astype(o_ref.dtype)

    return pl.pallas_call(
        body, name=name, grid=(M // tm, N // tn, nk),
        in_specs=[pl.BlockSpec((tk, tm), lambda i, j, kk: (kk, i)), pl.BlockSpec((tk, tn), lambda i, j, kk: (kk, j))],
        out_specs=pl.BlockSpec((tm, tn), lambda i, j, kk: (i, j)), out_shape=_sds((M, N), _MXU),
        scratch_shapes=[pltpu.VMEM((tm, tn), F32)],
        compiler_params=_cp("parallel", "parallel", "arbitrary"))(a, b)


def _wgrad_w_in(x, du_parts, tk):
    T = x.shape[0]
    tk = min(tk, T)
    nk = T // tk

    def body(a_ref, *refs):
        o_ref, cs_ref, acc_ref = refs[8:]
        kk = pl.program_id(0)

        @pl.when(kk == 0)
        def _():
            acc_ref[...] = jnp.zeros_like(acc_ref)
            cs_ref[...] = jnp.zeros_like(cs_ref)

        du = jnp.concatenate([r[...] for r in refs[:8]], axis=1)
        acc_ref[...] += _dot(a_ref[...], du, TN)
        cs_ref[...] += _colsum(du.astype(F32))

        @pl.when(kk == nk - 1)
        def _():
            o_ref[...] = acc_ref[...].astype(o_ref.dtype)

    return pl.pallas_call(
        body, name="wgrad_w_in", grid=(nk,),
        in_specs=[pl.BlockSpec((tk, D_MODEL), lambda kk: (kk, 0))] + _du_specs(tk),
        out_specs=[_full((D_MODEL, PROJ_WP)), _full((1, PROJ_WP))],
        out_shape=[_sds((D_MODEL, PROJ_WP), _MXU), _sds((1, PROJ_WP))],
        scratch_shapes=[pltpu.VMEM((D_MODEL, PROJ_WP), F32)], compiler_params=_cp("arbitrary"))(x, *du_parts)


W_IN_S, FF_S, OUT_S, PP_S = PROJ_W // N_DEV, D_FF // N_DEV, D_MODEL // N_DEV, D_MODEL // N_DEV
LATE = ("w_ffn_gate", "w_ffn_up", "w_out", "w_ffn_down", "ple_w_gate", "ple_w_proj")
BIG = ("w_in",) + LATE


def _split_cols(a, n):
    return a.reshape(a.shape[0], N_DEV, n).transpose(1, 0, 2)


def _join_cols(a):
    return a.transpose(1, 0, 2).reshape(a.shape[1], -1)


def _step(x, p, tgt, w_in, b_in, lb_logits, conv_w, conv_b, g_hg, g_ml, ln1_g, ln1_b, ln2_g, ln2_b, bpg, late,
          distributed):
    T = x.shape[0]
    tm = min(256, T)
    gather = lambda *names: [_Comm("gather", [late[n] for n in names])] if distributed else None
    scatter = lambda *arrs: [_Comm("scatter", list(arrs))] if distributed else None
    rows = lambda a, n: a.reshape(N_DEV, n, D_MODEL)
    u, got1 = _in_proj(x, w_in, b_in, tm, gather("w_out", "ple_w_gate", "ple_w_proj"))
    pre, qkc = _conv_fwd(u, conv_w, conv_b, tm)
    (o_hg, hg_states), got2 = _hgrn2_fwd(u, lb_logits, gather("w_ffn_gate", "w_ffn_up"))
    (h_ml, cst, nst, mst), got3 = _mlstm_fwd(qkc, u, gather("w_ffn_down"))
    if distributed:
        w_out, wpg, wpp = got1[0][0].reshape(D_MODEL, D_MODEL), got1[0][1].reshape(D_MODEL, D_MODEL), _join_cols(got1[0][2])
        wg, wu, wd = _join_cols(got2[0][0]), _join_cols(got2[0][1]), got3[0][0].reshape(D_FF, D_MODEL)
    else:
        w_out, wg, wu, wd, wpg, wpp = (late[n] for n in ("w_out", "w_ffn_gate", "w_ffn_up", "w_ffn_down", "ple_w_gate", "ple_w_proj"))
    m_in, z1, x1 = _out_proj_ln(x, u, o_hg, h_ml, g_hg, g_ml, w_out, ln1_g, ln1_b, tm)
    z2, x2, a_pre, b_pre, hh = _ffn_ln(x1, wg, wu, wd, ln2_g, ln2_b, tm)
    de, dgp, dz2, loss_vec, d_bpg, d_ln2g, d_ln2b = _ple_loss_ln2_bwd(x2, z2, p, tgt, wpg, bpg, wpp, ln2_g, ln2_b, tm)
    big = dict(ple_w_gate=_wgrad(x2, dgp, "wgrad_ple_gate", 512, D_MODEL, 1024),
               ple_w_proj=_wgrad(p, de, "wgrad_ple_proj", 512, D_MODEL, 1024))
    (da, dbb, dz1, d_ln1g, d_ln1b), r1 = _ffn_bwd_ln1_bwd(
        a_pre, b_pre, z1, dz2, wg, wu, wd, ln1_g, ln1_b, tm, scatter(rows(big["ple_w_gate"], OUT_S), _split_cols(big["ple_w_proj"], PP_S)))
    big.update(
        w_ffn_gate=_wgrad(x1, da, "wgrad_ffn_gate", 512, D_FF, 1024),
        w_ffn_up=_wgrad(x1, dbb, "wgrad_ffn_up", 512, D_FF, 1024),
        w_ffn_down=_wgrad(hh, dz2, "wgrad_ffn_down", D_FF, D_MODEL, 1024),
        w_out=_wgrad(m_in, dz1, "wgrad_w_out", 512, D_MODEL, 1024))
    d_ohg, d_hml, d_hgate, d_mo, d_ghg, d_gml = _out_proj_bwd(dz1, u, o_hg, h_ml, g_hg, g_ml, w_out, tm)
    (d_hq, d_hf, d_hv, d_lb), r2 = _hgrn2_bwd(
        u, lb_logits, d_ohg, hg_states,
        scatter(_split_cols(big["w_ffn_gate"], FF_S), _split_cols(big["w_ffn_up"], FF_S), rows(big["w_ffn_down"], FF_S),
                rows(big["w_out"], OUT_S)))
    d_qkc, d_mv, d_gates = _mlstm_bwd(qkc, u, d_hml, cst, nst, mst)
    d_mqk, d_convw, d_convb = _conv_bwd(u, pre, d_qkc, conv_w, tm)
    du_parts = [d_hq, d_hf, d_hv, d_hgate, d_mqk, d_mv, d_mo, d_gates]
    big["w_in"], d_bin = _wgrad_w_in(x, du_parts, 256)
    small = dict(b_in=d_bin, hg_lb_logits=d_lb, ml_conv_w=d_convw, ml_conv_b=d_convb, hg_norm_g=d_ghg, ml_norm_g=d_gml,
                 ln1_g=d_ln1g, ln1_b=d_ln1b, ln2_g=d_ln2g, ln2_b=d_ln2b, ple_b_gate=d_bpg)
    last = [_Comm("scatter", [_split_cols(big["w_in"][:, :PROJ_W], W_IN_S)]),
            _Comm("gather", [_pack_small(small, loss_vec)])] if distributed else None
    dx, r3 = _in_proj_bwd(dz1, du_parts, w_in, tm, last)
    gathered_small = None
    if distributed:
        big = dict(ple_w_gate=r1[0][0], ple_w_proj=r1[0][1], w_ffn_gate=r2[0][0], w_ffn_up=r2[0][1],
                   w_ffn_down=r2[0][2], w_out=r2[0][3], w_in=r3[0][0])
        gathered_small = r3[1][0]
    return loss_vec, dx, big, small, gathered_small


SMALL = dict(b_in=(8, 32, PROJ_WP), hg_lb_logits=(40, 8, 1024), ml_conv_w=(48, 16, 2048), ml_conv_b=(64, 8, 512),
             hg_norm_g=(72, 8, 512), ml_norm_g=(80, 8, 512), ln1_g=(88, 8, 1024), ln1_b=(96, 8, 1024),
             ln2_g=(104, 8, 1024), ln2_b=(112, 8, 1024), ple_b_gate=(120, 8, 1024))
SM_ROWS = 128


def _padc(a, n):
    return jnp.pad(a, [(0, 0)] * (a.ndim - 1) + [(0, n - a.shape[-1])])


def _pack_small(d, loss_vec=None):
    first = jnp.zeros((8, 128), F32) if loss_vec is None else loss_vec.reshape(8, 128)
    parts = [first]
    for name, (_, rows, n) in SMALL.items():
        parts.append(jnp.pad(d[name].reshape(-1), (0, rows * 128 - n)).reshape(rows, 128))
    return jnp.concatenate(parts, axis=0)


def _unpack_small(slab, shapes):
    return {name: slab[r0:r0 + rows].reshape(-1)[:n].reshape(shapes[name]) for name, (r0, rows, n) in SMALL.items()}


def _adamw(w, g, m, v):
    m = B1 * m + (1.0 - B1) * g
    v = B2 * v + (1.0 - B2) * jnp.square(g)
    m_hat = m / (1.0 - B1 ** STEP)
    v_hat = v / (1.0 - B2 ** STEP)
    return -LR * (m_hat / (jnp.sqrt(v_hat) + EPS) + WD * w), m, v


def _sum_slabs(ref):
    g = ref[0].astype(F32)
    for j in range(1, N_DEV):
        g = g + ref[j].astype(F32)
    return g


def _adamw_matrix(rb, w, m, v, name):
    R, C = w.shape
    tr = 256 if R % 256 == 0 else R

    def body(rb_ref, w_ref, m_ref, v_ref, g_ref, d_ref, m2_ref, v2_ref):
        g = _sum_slabs(rb_ref)
        g_ref[...] = g
        d_ref[...], m2_ref[...], v2_ref[...] = _adamw(w_ref[...], g, m_ref[...], v_ref[...])

    blk = pl.BlockSpec((tr, C), lambda i: (i, 0))
    return pl.pallas_call(
        body, name=name, grid=(R // tr,),
        in_specs=[pl.BlockSpec((N_DEV, tr, C), lambda i: (0, i, 0)), blk, blk, blk],
        out_specs=[blk] * 4, out_shape=[_sds((R, C))] * 4, compiler_params=_cp("parallel"))(rb, w, m, v)


def _adamw_small(sg, w, m, v):
    def body(sg_ref, w_ref, m_ref, v_ref, loss_ref, g_ref, d_ref, m2_ref, v2_ref):
        g = _sum_slabs(sg_ref)
        loss_ref[...] = (0.5 / D_MODEL) * jnp.sum(g[0:8], keepdims=True)
        g_ref[...] = g
        d_ref[...], m2_ref[...], v2_ref[...] = _adamw(w_ref[...], g, m_ref[...], v_ref[...])

    return pl.pallas_call(
        body, name="adamw_small", out_shape=[_sds((1, 1))] + [_sds((SM_ROWS, 128))] * 4)(sg, w, m, v)


WEIGHTS = ("w_in", "b_in", "hg_lb_logits", "ml_conv_w", "ml_conv_b", "hg_norm_g", "ml_norm_g", "w_out", "ln1_g", "ln1_b",
           "w_ffn_gate", "w_ffn_up", "w_ffn_down", "ln2_g", "ln2_b", "ple_w_proj", "ple_w_gate", "ple_b_gate")
CONV_S = HALF // N_DEV


def kernel(x, p, w_in, b_in, hg_lb_logits, ml_conv_w, ml_conv_b, hg_norm_g, ml_norm_g, w_out, ln1_g, ln1_b, w_ffn_gate, w_ffn_up, w_ffn_down, ln2_g, ln2_b, ple_w_proj, ple_w_gate, ple_b_gate, loss_target, m_w_in, m_b_in, m_hg_lb_logits, m_ml_conv_w, m_ml_conv_b, m_hg_norm_g, m_ml_norm_g, m_w_out, m_ln1_g, m_ln1_b, m_w_ffn_gate, m_w_ffn_up, m_w_ffn_down, m_ln2_g, m_ln2_b, m_ple_w_proj, m_ple_w_gate, m_ple_b_gate, v_w_in, v_b_in, v_hg_lb_logits, v_ml_conv_w, v_ml_conv_b, v_hg_norm_g, v_ml_norm_g, v_w_out, v_ln1_g, v_ln1_b, v_w_ffn_gate, v_w_ffn_up, v_w_ffn_down, v_ln2_g, v_ln2_b, v_ple_w_proj, v_ple_w_gate, v_ple_b_gate):
    args = locals()
    me = 4 * lax.axis_index("x") + 2 * lax.axis_index("y") + lax.axis_index("c")
    shapes = {n: args[n].shape for n in WEIGHTS}
    drop = lambda n, a: a[0] if n in BIG or n == "ml_conv_w" else a
    W = {n: drop(n, args[n]) for n in WEIGHTS}
    M = {n: drop(n, args["m_" + n]) for n in WEIGHTS}
    V = {n: drop(n, args["v_" + n]) for n in WEIGHTS}

    g_in, g_conv = _gather_two_level(
        [W["w_in"].astype(_MXU), jnp.pad(W["ml_conv_w"], ((0, 4), (0, 128 - CONV_S)))], "gather_w_in")
    w_in_full = _padc(_join_cols(g_in), PROJ_WP)
    conv_full = _join_cols(g_conv[:, :4, :CONV_S])

    _, dx, big, _, sg = _step(
        x[0], p[0, 0], loss_target[0], w_in_full, _padc(b_in, PROJ_WP), hg_lb_logits, conv_full, ml_conv_b,
        hg_norm_g, ml_norm_g, ln1_g, ln1_b, ln2_g, ln2_b, ple_b_gate, {n: W[n].astype(_MXU) for n in LATE}, True)

    upd = {n: _adamw_matrix(big[n], W[n], M[n], V[n], "adamw_" + n) for n in BIG}
    place = lambda d: {**d, "b_in": _padc(d["b_in"], PROJ_WP),
                       "ml_conv_w": lax.dynamic_update_slice(jnp.zeros((4, HALF), F32), d["ml_conv_w"], (0, me * CONV_S))}
    loss, *small_upd = _adamw_small(sg, _pack_small(place(W)), _pack_small(place(M)), _pack_small(place(V)))

    outs = []
    sm_shapes = {**{n: shapes[n] for n in SMALL}, "b_in": (1, PROJ_WP), "ml_conv_w": (4, HALF)}
    for kind in range(4):
        smalls = _unpack_small(small_upd[kind], sm_shapes)
        smalls["b_in"] = smalls["b_in"][:, :PROJ_W]
        smalls["ml_conv_w"] = lax.dynamic_slice(smalls["ml_conv_w"], (0, me * CONV_S), (4, CONV_S))
        for n in WEIGHTS:
            outs.append((upd[n][kind] if n in BIG else smalls[n]).reshape(shapes[n]))
    return (loss.reshape(()), dx.reshape(x.shape), *outs)
```

```python
import jax
import jax.numpy as jnp
from jax import lax
from jax.experimental import pallas as pl
from jax.experimental.pallas import tpu as pltpu

F32 = jnp.float32
_MXU = jnp.bfloat16

D_MODEL = 1024
CHUNK = 64
SUB = 16
PLE_DIM = 256
HEADS = 4
ML_DQK = 64
HALF = 512
D_FF = 2816
PROJ_W = 3592
PROJ_WP = 3712
ALPHA = float(2 ** 0.25)
LN_EPS = 1e-5
RMS_EPS = 1e-6
ML_SCALE = ML_DQK ** -0.5
N_DEV = 8
LR, B1, B2, EPS, WD, STEP = 0.001, 0.9, 0.999, 1e-08, 0.01, 10
NEG = -1e30

C_HQ, C_HF, C_HV, C_HGATE, C_MQK, C_MV, C_MO, C_GATES = 0, 4, 8, 12, 16, 20, 24, 28
DU_WIDTHS = (HALF,) * 7 + (128,)

VMEM_LIMIT = 52 * 1024 * 1024
GC = 8
ROWS = 512
ROWS_FFN = 256

NN = (((1,), (0,)), ((), ()))
NT = (((1,), (1,)), ((), ()))
TN = (((0,), (0,)), ((), ()))
BNT = (((2,), (2,)), ((0,), (0,)))
BNN = (((2,), (1,)), ((0,), (0,)))
BTN = (((1,), (1,)), ((0,), (0,)))


def _dot(a, b, dims=NN):
    return lax.dot_general(a.astype(_MXU), b.astype(_MXU), dims, preferred_element_type=F32)


def _dotx(a, b, dims=NN):
    return lax.dot_general(a, b, dims, precision=lax.Precision.HIGHEST, preferred_element_type=F32)


def _sig(x):
    return jax.nn.sigmoid(x)


def _cp(*sem):
    return pltpu.CompilerParams(dimension_semantics=sem, vmem_limit_bytes=VMEM_LIMIT)


def _row(tm, c, blk=0):
    return pl.BlockSpec((tm, c), lambda i, blk=blk: (i, blk))


def _full(shape):
    nd = len(shape)
    return pl.BlockSpec(tuple(shape), lambda *_, nd=nd: (0,) * nd)


def _sds(shape, dtype=F32):
    return jax.ShapeDtypeStruct(tuple(shape), dtype)


def _iota(shape, axis):
    return lax.broadcasted_iota(jnp.int32, shape, axis)


def _colsum(x):
    return jnp.sum(x, axis=0, keepdims=True)


def _rowsum(x):
    return jnp.sum(x, axis=1, keepdims=True)


def _ln_fwd(z, g, b):
    mu = jnp.mean(z, axis=-1, keepdims=True)
    zc = z - mu
    var = jnp.mean(zc * zc, axis=-1, keepdims=True)
    rstd = lax.rsqrt(var + LN_EPS)
    xhat = zc * rstd
    return xhat * g + b, xhat, rstd


def _ln_bwd(dy, xhat, rstd, g):
    dxh = dy * g
    m1 = jnp.mean(dxh, axis=-1, keepdims=True)
    m2 = jnp.mean(dxh * xhat, axis=-1, keepdims=True)
    return rstd * (dxh - m1 - xhat * m2)


def _dsilu(x, s):
    return s * (1.0 + x * (1.0 - s))


MESH = pl.DeviceIdType.MESH
ANY = pl.BlockSpec(memory_space=pl.ANY)


def _flip(v, bit):
    return 1 - v if bit else v


class _Comm:
    def __init__(self, kind, srcs):
        self.kind, self.srcs, self.n = kind, list(srcs), len(srcs)

    def out_shape(self):
        lead = (N_DEV,) if self.kind == "gather" else ()
        return [jax.ShapeDtypeStruct(lead + s.shape, s.dtype) for s in self.srcs]

    def scratch(self):
        return [pltpu.SemaphoreType.DMA((7 * self.n,)), pltpu.SemaphoreType.DMA((7 * self.n,)),
                pltpu.SemaphoreType.DMA((self.n,))]

    def copies(self, srcs, dsts, send_sems, recv_sems, local_sems):
        x, y, c = lax.axis_index("x"), lax.axis_index("y"), lax.axis_index("c")
        me = 4 * x + 2 * y + c
        pick = (lambda s, j: s) if self.kind == "gather" else (lambda s, j: s.at[j])
        out = []
        for i, (s, d) in enumerate(zip(srcs, dsts)):
            out.append(pltpu.make_async_copy(pick(s, me), d.at[me], local_sems.at[i]))
            for k in range(1, N_DEV):
                px, py, pc = _flip(x, k & 4), _flip(y, k & 2), _flip(c, k & 1)
                out.append(pltpu.make_async_remote_copy(
                    src_ref=pick(s, 4 * px + 2 * py + pc), dst_ref=d.at[me], send_sem=send_sems.at[7 * i + k - 1],
                    recv_sem=recv_sems.at[7 * i + k - 1], device_id=(px, py, pc), device_id_type=MESH))
        return out


def _hosted_call(body, comms, *, name, grid, in_specs, out_specs, out_shape, scratch_shapes, args):
    comms = list(comms or [])
    if not comms:
        res = pl.pallas_call(body, name=name, grid=grid, in_specs=in_specs, out_specs=out_specs, out_shape=out_shape,
                             scratch_shapes=scratch_shapes, compiler_params=_cp("arbitrary"))(*args)
        return list(res), []
    n_in, n_out, n_sc, nc = len(in_specs), len(out_specs), len(scratch_shapes), sum(cm.n for cm in comms)
    last = grid[0] - 1

    def hosted(*refs):
        ins, csrc = refs[:n_in], refs[n_in:n_in + nc]
        o0 = n_in + nc
        outs, cdst = refs[o0:o0 + n_out], refs[o0 + n_out:o0 + n_out + nc]
        s0 = o0 + n_out + nc
        scr, sems = refs[s0:s0 + n_sc], refs[s0 + n_sc:]

        def copies():
            out, o = [], 0
            for j, cm in enumerate(comms):
                out += cm.copies(csrc[o:o + cm.n], cdst[o:o + cm.n], *sems[3 * j:3 * j + 3])
                o += cm.n
            return out

        i = pl.program_id(0)

        @pl.when(i == 0)
        def _():
            for cp in copies():
                cp.start()

        body(*ins, *outs, *scr)

        @pl.when(i == last)
        def _():
            for cp in copies():
                cp.wait()

    res = pl.pallas_call(
        hosted, name=name, grid=grid, in_specs=list(in_specs) + [ANY] * nc, out_specs=list(out_specs) + [ANY] * nc,
        out_shape=list(out_shape) + [s for cm in comms for s in cm.out_shape()],
        scratch_shapes=list(scratch_shapes) + [s for cm in comms for s in cm.scratch()],
        compiler_params=_cp("arbitrary"))(*args, *[a for cm in comms for a in cm.srcs])
    got, o = [], n_out
    for cm in comms:
        got.append(list(res[o:o + cm.n]))
        o += cm.n
    return list(res[:n_out]), got


def _gather_two_level(blocks, name):
    n = len(blocks)

    def body(*refs):
        x_refs, out_refs = refs[:n], refs[n:2 * n]
        send_sems, recv_sems, local_sems = refs[2 * n:]
        x, y, c = lax.axis_index("x"), lax.axis_index("y"), lax.axis_index("c")
        me, sibling = (x, y, c), (x, y, 1 - c)
        chips = [(1 - x, y), (x, 1 - y), (1 - x, 1 - y)]

        def copy(i, k, block, to, own=False):
            slab = out_refs[i].at[4 * block[0] + 2 * block[1] + block[2]]
            return pltpu.make_async_remote_copy(
                src_ref=x_refs[i] if own else slab, dst_ref=slab, send_sem=send_sems.at[7 * i + k],
                recv_sem=recv_sems.at[7 * i + k], device_id=to, device_id_type=MESH)

        mine = [pltpu.make_async_copy(x_refs[i], out_refs[i].at[4 * x + 2 * y + c], local_sems.at[i]) for i in range(n)]
        for cp in mine:
            cp.start()
        first = [copy(i, 0, me, sibling, own=True) for i in range(n)]
        first += [copy(i, 1 + j, me, (*chip, c), own=True) for j, chip in enumerate(chips) for i in range(n)]
        for cp in first:
            cp.start()
        passed = []
        for j, chip in enumerate(chips):
            for i in range(n):
                copy(i, 1 + j, (*chip, c), me).wait_recv()
                passed.append(copy(i, 4 + j, (*chip, c), sibling))
                passed[-1].start()
        for i in range(n):
            copy(i, 0, sibling, me).wait_recv()
            for j, chip in enumerate(chips):
                copy(i, 4 + j, (*chip, 1 - c), me).wait_recv()
        for cp in first + passed:
            cp.wait_send()
        for cp in mine:
            cp.wait()

    return pl.pallas_call(
        body, name=name, out_shape=[jax.ShapeDtypeStruct((N_DEV,) + b.shape, b.dtype) for b in blocks],
        in_specs=[ANY] * n, out_specs=[ANY] * n,
        scratch_shapes=[pltpu.SemaphoreType.DMA((7 * n,)), pltpu.SemaphoreType.DMA((7 * n,)),
                        pltpu.SemaphoreType.DMA((n,))])(*blocks)


def _in_proj(x, w, b, tm, comms=None):
    T = x.shape[0]

    def body(x_ref, w_ref, b_ref, o_ref):
        o_ref[...] = _dot(x_ref[...], w_ref[...]) + b_ref[...]

    (u,), got = _hosted_call(
        body, comms, name="in_proj", grid=(T // tm,),
        in_specs=[_row(tm, D_MODEL), _full(w.shape), _full(b.shape)],
        out_specs=[_row(tm, PROJ_WP)], out_shape=[_sds((T, PROJ_WP))], scratch_shapes=[], args=(x, w, b))
    return u, got


def _shift_rows(x, halo, j, rowi):
    r = pltpu.roll(x, j, 0)
    top = jnp.where(rowi < j, pltpu.roll(halo, j, 0), r[:8])
    return jnp.concatenate([top, r[8:]], axis=0)


def _shift_rows_up(x, halo, j, rowi):
    n = x.shape[0]
    r = pltpu.roll(x, n - j, 0)
    bot = jnp.where(rowi >= 8 - j, pltpu.roll(halo, 8 - j, 0), r[n - 8:])
    return jnp.concatenate([r[:n - 8], bot], axis=0)


def _conv_fwd(u, cw, cb, tm):
    T = u.shape[0]
    hb = tm // 8

    def body(x_ref, halo_ref, w_ref, b_ref, pre_ref, out_ref):
        i = pl.program_id(0)
        x = x_ref[...]
        halo = jnp.where(i > 0, halo_ref[...], 0.0)
        rowi = _iota((8, HALF), 0)
        acc = x * w_ref[3:4, :] + b_ref[...]
        for j in (1, 2, 3):
            acc = acc + _shift_rows(x, halo, j, rowi) * w_ref[3 - j:4 - j, :]
        pre_ref[...] = acc
        out_ref[...] = acc * _sig(acc)

    return pl.pallas_call(
        body, name="conv_fwd", grid=(T // tm,),
        in_specs=[pl.BlockSpec((tm, HALF), lambda i: (i, C_MQK // 4)),
                  pl.BlockSpec((8, HALF), lambda i: (jnp.maximum(i * hb - 1, 0), C_MQK // 4)),
                  _full(cw.shape), _full(cb.shape)],
        out_specs=[_row(tm, HALF), _row(tm, HALF)], out_shape=[_sds((T, HALF)), _sds((T, HALF))],
        compiler_params=_cp("parallel"))(u, u, cw, cb)


def _bdot(a, b, dims):
    return lax.dot_general(a.astype(_MXU), b.astype(_MXU), dims, preferred_element_type=F32)


def _bdotx(a, b, dims):
    return lax.dot_general(a, b, dims, precision=lax.Precision.HIGHEST, preferred_element_type=F32)


def _heads_to_batch(x, w):
    G = x.shape[0] // CHUNK
    x3 = x.reshape(G, CHUNK, HEADS * w)
    return jnp.stack([x3[:, :, w * h:w * (h + 1)] for h in range(HEADS)], axis=1).reshape(G * HEADS, CHUNK, w)


def _batch_to_heads(x3):
    B, _, w = x3.shape
    x4 = x3.reshape(B // HEADS, HEADS, CHUNK, w)
    return jnp.concatenate([x4[:, h] for h in range(HEADS)], axis=-1).reshape(B // HEADS * CHUNK, HEADS * w)


def _chunk_cumsum(x, rowmod, reverse=False):
    R = x.shape[0]
    for sh in (1, 2, 4, 8, 16, 32):
        if reverse:
            x = x + jnp.where(rowmod < CHUNK - sh, pltpu.roll(x, R - sh, 0), 0.0)
        else:
            x = x + jnp.where(rowmod >= sh, pltpu.roll(x, sh, 0), 0.0)
    return x


def _lane_col(x, c, lane):
    return _rowsum(jnp.where(lane == c, x, 0.0))


def _hg_gates(hq, hf, lb):
    sg = _sig(hf)
    nsg = _sig(-hf)
    f = lb + (1.0 - lb) * sg
    g = jnp.log(f)
    k = (1.0 - lb) * nsg
    sq = _sig(hq)
    return hq * sq, g, k, f, sg, nsg, sq


def _hg_prep(hq_ref, hf_ref, lg_ref, b_sc, k_sc):
    R = hq_ref.shape[0]
    G = R // CHUNK
    lb = _sig(lg_ref[0:1, :] - lg_ref[1:2, :])
    hq = hq_ref[...]
    q, g, k, f, sg, nsg, sq = _hg_gates(hq, hf_ref[...], lb)
    rowmod = _iota((R, HALF), 0) & (CHUNK - 1)
    b = _chunk_cumsum(g, rowmod)
    last8 = _iota((8, HALF), 0) == 7
    bl_rows = [_colsum(jnp.where(last8, b[CHUNK * c + CHUNK - 8:CHUNK * (c + 1)], 0.0)) for c in range(G)]
    bl3 = jnp.stack([r[:, 128 * h:128 * (h + 1)] for r in bl_rows for h in range(HEADS)], axis=0)
    b3, k3 = _heads_to_batch(b, 128), _heads_to_batch(k, 128)
    b_sc[...] = b3
    k_sc[...] = k3
    return dict(G=G, lb=lb, hq=hq, f=f, sg=sg, nsg=nsg, sq=sq, rowmod=rowmod, q3=_heads_to_batch(q, 128), k3=k3, b3=b3,
                bl3=bl3)


def _hg_diag_tiles(b_sc, b3, r0, rowi):
    bi = b3[:, r0:r0 + SUB]
    return [jnp.exp(jnp.where(rowi >= s, bi - b_sc[:, r0 + s:r0 + s + 1, :], NEG)) for s in range(SUB)]


def _hgrn2_fwd(u, lb_logits, comms=None):
    T = u.shape[0]
    G = min(GC, T // CHUNK)
    R, B, N = G * CHUNK, G * HEADS, T // CHUNK

    def body(hq_ref, hf_ref, hv_ref, lg_ref, o_ref, st_ref, S_ref, b_sc, k_sc, v_sc):
        @pl.when(pl.program_id(0) == 0)
        def _():
            S_ref[...] = jnp.zeros_like(S_ref)

        pz = _hg_prep(hq_ref, hf_ref, lg_ref, b_sc, k_sc)
        q3, k3, b3, bl3 = pz["q3"], pz["k3"], pz["b3"], pz["bl3"]
        v3 = _heads_to_batch(hv_ref[...], 128)
        v_sc[...] = v3
        stloc = _bdot(v3, k3 * jnp.exp(bl3 - b3), BTN).reshape(G, HEADS, 128, 128)
        dec = jnp.exp(bl3).reshape(G, HEADS, 1, 128)
        ST = S_ref[...]
        sts = []
        for c in range(G):
            sts.append(ST)
            ST = ST * dec[c] + stloc[c]
        S_ref[...] = ST
        st4 = jnp.stack(sts, axis=0)
        st_ref[...] = st4
        o = _bdot(q3 * jnp.exp(b3), st4.reshape(B, 128, 128), BNT)
        ones = jnp.ones((128, 128), F32)
        rowi = _iota((1, SUB, 128), 1)
        outs = []
        for i in range(CHUNK // SUB):
            r0 = SUB * i
            qi = q3[:, r0:r0 + SUB]
            oi = o[:, r0:r0 + SUB]
            if i > 0:
                r = b_sc[:, r0 - 1:r0, :]
                qe = qi * jnp.exp(b3[:, r0:r0 + SUB] - r)
                ke = k3[:, :r0] * jnp.exp(r - b3[:, :r0])
                oi = oi + _bdot(_bdot(qe, ke, BNT), v3[:, :r0], BNN)
            tiles = _hg_diag_tiles(b_sc, b3, r0, rowi)
            ms = [qi * (k_sc[:, r0 + s:r0 + s + 1, :] * tiles[s]) for s in range(SUB)]
            Rm = _dot(jnp.concatenate(ms, axis=1).reshape(B * SUB * SUB, 128), ones).reshape(B, SUB * SUB, 128)
            for s in range(SUB):
                oi = oi + Rm[:, SUB * s:SUB * s + SUB] * v_sc[:, r0 + s:r0 + s + 1, :]
            outs.append(oi)
        o_ref[...] = _batch_to_heads(jnp.concatenate(outs, axis=1))

    blk = lambda c: pl.BlockSpec((R, HALF), lambda n, c=c: (n, c // 4))
    return _hosted_call(
        body, comms, name="hgrn2_fwd", grid=(N // G,),
        in_specs=[blk(C_HQ), blk(C_HF), blk(C_HV), _full(lb_logits.shape)],
        out_specs=[pl.BlockSpec((R, HALF), lambda n: (n, 0)),
                   pl.BlockSpec((G, HEADS, 128, 128), lambda n: (n, 0, 0, 0))],
        out_shape=[_sds((T, HALF)), _sds((N, HEADS, 128, 128))],
        scratch_shapes=[pltpu.VMEM((HEADS, 128, 128), F32)] + [pltpu.VMEM((B, CHUNK, 128), F32)] * 3,
        args=(u, u, u, lb_logits))


def _hgrn2_bwd(u, lb_logits, do, states, comms=None):
    T = u.shape[0]
    G = min(GC, T // CHUNK)
    R, B, NG = G * CHUNK, G * HEADS, T // (G * CHUNK)

    def body(hq_ref, hf_ref, hv_ref, lg_ref, do_ref, st_ref, dhq_ref, dhf_ref, dhv_ref, dlb_ref,
             dS_ref, b_sc, k_sc, v_sc, q_sc, do_sc):
        @pl.when(pl.program_id(0) == 0)
        def _():
            dS_ref[...] = jnp.zeros_like(dS_ref)
            dlb_ref[...] = jnp.zeros_like(dlb_ref)

        pz = _hg_prep(hq_ref, hf_ref, lg_ref, b_sc, k_sc)
        q3, k3, b3, bl3, lb = pz["q3"], pz["k3"], pz["b3"], pz["bl3"], pz["lb"]
        v3 = _heads_to_batch(hv_ref[...], 128)
        v_sc[...] = v3
        do3 = _heads_to_batch(do_ref[...], 128)
        q_sc[...] = q3
        do_sc[...] = do3
        st3 = st_ref[...].reshape(B, 128, 128)
        eb = jnp.exp(b3)
        ebl = jnp.exp(bl3 - b3)
        qt = q3 * eb
        kl = k3 * ebl
        dstloc = _bdot(do3, qt, BTN).reshape(G, HEADS, 128, 128)
        dec = jnp.exp(bl3).reshape(G, HEADS, 1, 128)
        dST = dS_ref[...]
        dsts = [None] * G
        for c in reversed(range(G)):
            dsts[c] = dST
            dST = dST * dec[c] + dstloc[c]
        dS_ref[...] = dST
        dst3 = jnp.stack(dsts, axis=0).reshape(B, 128, 128)
        dqt = _bdot(do3, st3, BNN)
        dkl = _bdot(v3, dst3, BNN)
        dv_acc = _bdot(kl, dst3, BNT)
        ones = jnp.ones((128, 128), F32)
        rowi = _iota((1, SUB, 128), 1)
        dq_parts, dk_parts, dv_parts = [], [], []
        dk_in = jnp.zeros((B, CHUNK, 128), F32)
        for i_s in range(CHUNK // SUB):
            r0 = SUB * i_s
            qi = q3[:, r0:r0 + SUB]
            doi = do3[:, r0:r0 + SUB]
            dqi = jnp.zeros((B, SUB, 128), F32)
            if i_s > 0:
                r = b_sc[:, r0 - 1:r0, :]
                eq = jnp.exp(b3[:, r0:r0 + SUB] - r)
                ek = jnp.exp(r - b3[:, :r0])
                qe = qi * eq
                ke = k3[:, :r0] * ek
                a_off = _bdot(qe, ke, BNT)
                p_off = _bdot(doi, v3[:, :r0], BNT)
                pad = jnp.zeros((B, CHUNK - r0, 128), F32)
                dv_acc = dv_acc + jnp.concatenate([_bdot(a_off, doi, BTN), pad], axis=1)
                dqi = dqi + _bdot(p_off, ke, BNN) * eq
                dk_in = dk_in + jnp.concatenate([_bdot(p_off, qe, BTN) * ek, pad], axis=1)
            bi = b3[:, r0:r0 + SUB]
            ki, vi = k3[:, r0:r0 + SUB], v3[:, r0:r0 + SUB]
            tiles = _hg_diag_tiles(b_sc, b3, r0, rowi)
            kts = [k_sc[:, r0 + s:r0 + s + 1, :] * tiles[s] for s in range(SUB)]
            ps = [doi * v_sc[:, r0 + s:r0 + s + 1, :] for s in range(SUB)]
            qts = [q_sc[:, r0 + t:r0 + t + 1, :] * jnp.exp(jnp.where(rowi <= t, b_sc[:, r0 + t:r0 + t + 1, :] - bi, NEG))
                   for t in range(SUB)]
            mst = [ki * qts[t] for t in range(SUB)]
            pst = [vi * do_sc[:, r0 + t:r0 + t + 1, :] for t in range(SUB)]
            n_r = 3 * SUB * SUB
            Rm = _dot(jnp.concatenate(ps + mst + pst, axis=1).reshape(B * n_r, 128), ones).reshape(B, n_r, 128)
            dki = jnp.zeros((B, SUB, 128), F32)
            dvi = jnp.zeros((B, SUB, 128), F32)
            for j in range(SUB):
                dqi = dqi + Rm[:, SUB * j:SUB * (j + 1)] * kts[j]
                dvi = dvi + Rm[:, SUB * (SUB + j):SUB * (SUB + j + 1)] * do_sc[:, r0 + j:r0 + j + 1, :]
                dki = dki + Rm[:, SUB * (2 * SUB + j):SUB * (2 * SUB + j + 1)] * qts[j]
            dq_parts.append(dqi)
            dk_parts.append(dki)
            dv_parts.append(dvi)
        dq_in = jnp.concatenate(dq_parts, axis=1)
        dk_in = dk_in + jnp.concatenate(dk_parts, axis=1)
        dv_acc = dv_acc + jnp.concatenate(dv_parts, axis=1)
        db = qt * dqt + q3 * dq_in - k3 * dk_in - kl * dkl
        last = jnp.sum(kl * dkl, axis=1, keepdims=True) + jnp.exp(bl3) * jnp.sum(st3 * dst3, axis=1, keepdims=True)
        db = db + jnp.where(_iota((1, CHUNK, 1), 1) == CHUNK - 1, last, 0.0)
        dg = _chunk_cumsum(_batch_to_heads(db), pz["rowmod"], reverse=True)
        dq_tot = _batch_to_heads(dqt * eb + dq_in)
        dk_tot = _batch_to_heads(dkl * ebl + dk_in)
        common = dg / pz["f"] - dk_tot
        dhf_ref[...] = ((1.0 - lb) * pz["sg"] * pz["nsg"] * common).astype(dhf_ref.dtype)
        dl0 = _colsum(pz["nsg"] * common) * lb * (1.0 - lb)
        dlb_ref[0:1, :] += dl0
        dlb_ref[1:2, :] -= dl0
        dhq_ref[...] = (dq_tot * _dsilu(pz["hq"], pz["sq"])).astype(dhq_ref.dtype)
        dhv_ref[...] = _batch_to_heads(dv_acc).astype(dhv_ref.dtype)

    rev = lambda c: pl.BlockSpec((R, HALF), lambda i, c=c: (NG - 1 - i, c // 4))
    rev0 = pl.BlockSpec((R, HALF), lambda i: (NG - 1 - i, 0))
    return _hosted_call(
        body, comms, name="hgrn2_bwd", grid=(NG,),
        in_specs=[rev(C_HQ), rev(C_HF), rev(C_HV), _full(lb_logits.shape), rev0,
                  pl.BlockSpec((G, HEADS, 128, 128), lambda i: (NG - 1 - i, 0, 0, 0))],
        out_specs=[rev0, rev0, rev0, _full((2, HALF))],
        out_shape=[_sds((T, HALF), _MXU)] * 3 + [_sds((2, HALF))],
        scratch_shapes=[pltpu.VMEM((HEADS, 128, 128), F32)] + [pltpu.VMEM((B, CHUNK, 128), F32)] * 5,
        args=(u, u, u, lb_logits, do, states))


def _lanes_to_batch_cols(x, lane):
    G = x.shape[0] // CHUNK
    cols = [_lane_col(x, 4 + h, lane).reshape(G, CHUNK, 1) for h in range(HEADS)]
    return jnp.stack(cols, axis=1).reshape(G * HEADS, CHUNK, 1)


def _row_scalars(rows):
    lane = _iota((1, 128), 1)
    return jnp.stack([_rowsum(jnp.where(lane == 4 + h, r, 0.0)) for r in rows for h in range(HEADS)], axis=0)


def _ml_gates(gates):
    R = gates.shape[0]
    lane = _iota((R, 128), 1)
    rowmod = _iota((R, 128), 0) & (CHUNK - 1)
    lf = jnp.minimum(gates, 0.0) - jnp.log(1.0 + jnp.exp(-jnp.abs(gates)))
    g_all = _chunk_cumsum(lf, rowmod)
    x_all = pltpu.roll(gates, 4, 1) - g_all
    return g_all, x_all, lane, rowmod


def _ml_chunk_rows(g_all, x_all, mprev, g):
    gl = g_all[CHUNK * g + CHUNK - 8:CHUNK * (g + 1)]
    gl = _colsum(jnp.where(_iota((8, 128), 0) == 7, gl, 0.0))
    a = gl + x_all[CHUNK * g:CHUNK * (g + 1)]
    m_new = jnp.maximum(gl + mprev, jnp.max(a, axis=0, keepdims=True))
    return m_new, jnp.exp(gl + mprev - m_new), jnp.exp(a - m_new)


def _ml_batched(q3, k3, v3, g_all, x_all, lane, C3, n3, mprev3):
    G = g_all.shape[0] // CHUNK
    gcol3 = _lanes_to_batch_cols(g_all, lane)
    onehot = jnp.where(_iota((G, 8, 128), 1) + 4 == _iota((G, 8, 128), 2), 1.0, 0.0).astype(F32)
    rows = _bdotx(onehot, x_all.reshape(G, CHUNK, 128), BNT)
    sub = _iota((G, 8, CHUNK), 1)
    row3 = jnp.stack([jnp.sum(jnp.where(sub == h, rows, 0.0), axis=1, keepdims=True) for h in range(HEADS)],
                     axis=1).reshape(G * HEADS, 1, CHUNK)
    causal = _iota((1, CHUNK, CHUNK), 1) >= _iota((1, CHUNK, CHUNK), 2)
    dmat = jnp.where(causal, gcol3 + row3, NEG)
    m_inter = gcol3 + mprev3
    m_t = jnp.maximum(m_inter, jnp.max(dmat, axis=2, keepdims=True))
    wi = jnp.exp(dmat - m_t)
    wn = jnp.exp(m_inter - m_t)
    s3 = _bdot(q3, k3, BNT) * wi
    qc = _bdot(q3, C3, BNN)
    qn = jnp.sum(q3 * n3, axis=2, keepdims=True)
    num = _bdot(s3, v3, BNN) + wn * qc
    den = jnp.sum(s3, axis=2, keepdims=True) + wn * qn
    floor = jnp.exp(-m_t)
    return dict(wi=wi, wn=wn, s=s3, qc=qc, qn=qn, num=num, den=den, floor=floor, nrm=jnp.maximum(jnp.abs(den), floor))


def _mlstm_fwd(qkc, u, comms=None):
    T = u.shape[0]
    G = min(GC, T // CHUNK)
    R = G * CHUNK
    N = T // CHUNK

    def body(qk_ref, v_ref, g_ref, h_ref, cst_ref, nst_ref, mst_ref, C_ref, n_ref, m_ref):
        @pl.when(pl.program_id(0) == 0)
        def _():
            C_ref[...] = jnp.zeros_like(C_ref)
            n_ref[...] = jnp.zeros_like(n_ref)
            m_ref[...] = jnp.zeros_like(m_ref)

        g_all, x_all, lane, _ = _ml_gates(g_ref[...])
        m_row = m_ref[...]
        mprev_rows, wo_rows, ws_parts = [], [], []
        for g in range(G):
            mprev_rows.append(m_row)
            m_row, wo, ws = _ml_chunk_rows(g_all, x_all, m_row, g)
            wo_rows.append(wo)
            ws_parts.append(ws)
        m_ref[...] = m_row
        mst_ref[...] = jnp.stack(mprev_rows, axis=0)
        ws3 = _lanes_to_batch_cols(jnp.concatenate(ws_parts, axis=0), lane)
        wo4 = _row_scalars(wo_rows).reshape(G, HEADS, 1, 1)
        q3 = _heads_to_batch(qk_ref[:, :256] * ML_SCALE, ML_DQK)
        k3 = _heads_to_batch(qk_ref[:, 256:], ML_DQK)
        v3 = _heads_to_batch(v_ref[...], 128)
        kw = k3 * ws3
        cloc = _bdot(kw, v3, BTN).reshape(G, HEADS, ML_DQK, 128)
        nloc = jnp.sum(kw, axis=1, keepdims=True).reshape(G, HEADS, 1, ML_DQK)
        C, nn = C_ref[...], n_ref[...]
        cs, ns = [], []
        for g in range(G):
            cs.append(C)
            ns.append(nn)
            C = wo4[g] * C + cloc[g]
            nn = wo4[g] * nn + nloc[g]
        C_ref[...] = C
        n_ref[...] = nn
        c4, n4 = jnp.stack(cs, axis=0), jnp.stack(ns, axis=0)
        cst_ref[...] = c4
        nst_ref[...] = n4
        r = _ml_batched(q3, k3, v3, g_all, x_all, lane, c4.reshape(G * HEADS, ML_DQK, 128),
                        n4.reshape(G * HEADS, 1, ML_DQK), _row_scalars(mprev_rows))
        h_ref[...] = _batch_to_heads(r["num"] / r["nrm"])

    return _hosted_call(
        body, comms, name="mlstm_fwd", grid=(N // G,),
        in_specs=[pl.BlockSpec((R, HALF), lambda n: (n, 0)), pl.BlockSpec((R, HALF), lambda n: (n, C_MV // 4)),
                  pl.BlockSpec((R, 128), lambda n: (n, C_GATES))],
        out_specs=[pl.BlockSpec((R, HALF), lambda n: (n, 0)),
                   pl.BlockSpec((G, HEADS, ML_DQK, 128), lambda n: (n, 0, 0, 0)),
                   pl.BlockSpec((G, HEADS, 1, ML_DQK), lambda n: (n, 0, 0, 0)),
                   pl.BlockSpec((G, 1, 128), lambda n: (n, 0, 0))],
        out_shape=[_sds((T, HALF)), _sds((N, HEADS, ML_DQK, 128)), _sds((N, HEADS, 1, ML_DQK)), _sds((N, 1, 128))],
        scratch_shapes=[pltpu.VMEM((HEADS, ML_DQK, 128), F32), pltpu.VMEM((HEADS, 1, ML_DQK), F32),
                        pltpu.VMEM((1, 128), F32)],
        args=(qkc, u, u))


def _mlstm_bwd(qkc, u, dh, cst, nst, mst):
    T = u.shape[0]
    G = min(GC, T // CHUNK)
    R = G * CHUNK
    NG = T // R

    def body(qk_ref, v_ref, g_ref, dh_ref, cst_ref, nst_ref, mst_ref, dqk_ref, dv_ref, dgt_ref, dC_ref, dn_ref):
        @pl.when(pl.program_id(0) == 0)
        def _():
            dC_ref[...] = jnp.zeros_like(dC_ref)
            dn_ref[...] = jnp.zeros_like(dn_ref)

        B = G * HEADS
        gates = g_ref[...]
        g_all, x_all, lane, rowmod = _ml_gates(gates)
        mprev_rows = [mst_ref[g] for g in range(G)]
        wo_rows, ws_parts = [], []
        for g in range(G):
            _, wo, ws = _ml_chunk_rows(g_all, x_all, mprev_rows[g], g)
            wo_rows.append(wo)
            ws_parts.append(ws)
        ws3 = _lanes_to_batch_cols(jnp.concatenate(ws_parts, axis=0), lane)
        wo3 = _row_scalars(wo_rows)
        wo4 = wo3.reshape(G, HEADS, 1, 1)
        q3 = _heads_to_batch(qk_ref[:, :256] * ML_SCALE, ML_DQK)
        k3 = _heads_to_batch(qk_ref[:, 256:], ML_DQK)
        v3 = _heads_to_batch(v_ref[...], 128)
        dh3 = _heads_to_batch(dh_ref[...], 128)
        C3 = cst_ref[...].reshape(B, ML_DQK, 128)
        n3 = nst_ref[...].reshape(B, 1, ML_DQK)
        r = _ml_batched(q3, k3, v3, g_all, x_all, lane, C3, n3, _row_scalars(mprev_rows))
        wn, s3 = r["wn"], r["s"]
        inv = 1.0 / r["nrm"]
        dnum = dh3 * inv
        dnrm = -jnp.sum(dh3 * (r["num"] * inv), axis=2, keepdims=True) * inv
        dden = jnp.where(jnp.abs(r["den"]) > r["floor"], dnrm * jnp.sign(r["den"]), 0.0)
        ds = _bdot(dnum, v3, BNT) + dden
        dqk = ds * r["wi"]
        dd = ds * s3
        qw = q3 * wn
        dcloc = _bdot(qw, dnum, BTN).reshape(G, HEADS, ML_DQK, 128)
        dnloc = jnp.sum(qw * dden, axis=1, keepdims=True).reshape(G, HEADS, 1, ML_DQK)
        dC, dn = dC_ref[...], dn_ref[...]
        dcs, dns = [None] * G, [None] * G
        for g in reversed(range(G)):
            dcs[g], dns[g] = dC, dn
            dC = wo4[g] * dC + dcloc[g]
            dn = wo4[g] * dn + dnloc[g]
        dC_ref[...] = dC
        dn_ref[...] = dn
        dC3 = jnp.stack(dcs, axis=0).reshape(B, ML_DQK, 128)
        dn3 = jnp.stack(dns, axis=0).reshape(B, 1, ML_DQK)
        dk_st = ws3 * (_bdot(v3, dC3, BNT) + dn3)
        dq = _bdot(dqk, k3, BNN) + wn * (_bdot(dnum, C3, BNT) + dden * n3)
        dk = _bdot(dqk, q3, BTN) + dk_st
        dv = _bdot(s3, dnum, BTN) + ws3 * _bdot(k3, dC3, BNN)
        dv_ref[...] = _batch_to_heads(dv).astype(dv_ref.dtype)
        dqk_ref[...] = jnp.concatenate([_batch_to_heads(dq * ML_SCALE), _batch_to_heads(dk)], axis=1)
        e_col = wn * (jnp.sum(dnum * r["qc"], axis=2, keepdims=True) + dden * r["qn"])
        c_col = jnp.sum(k3 * dk_st, axis=2, keepdims=True)
        z = wo3 * (jnp.sum(dC3 * C3, axis=(1, 2), keepdims=True) + jnp.sum(dn3 * n3, axis=(1, 2), keepdims=True))
        dd_cols = _bdotx(dd, jnp.ones((B, CHUNK, 128), F32), BTN)[:, :, 0:1]
        last = _iota((1, CHUNK, 1), 1) == CHUNK - 1
        dg3 = jnp.sum(dd, axis=2, keepdims=True) - dd_cols + e_col - c_col
        dg3 = dg3 + jnp.where(last, jnp.sum(c_col, axis=1, keepdims=True) + z, 0.0)
        di3 = dd_cols + c_col

        def to_lanes(x3, first):
            x4 = x3.reshape(G, HEADS, CHUNK, 1)
            return sum(jnp.where(lane == first + h, x4[:, h].reshape(R, 1), 0.0) for h in range(HEADS))

        dlf = _chunk_cumsum(to_lanes(dg3, 4), rowmod, reverse=True)
        dgt_ref[...] = (to_lanes(di3, 0) + dlf * _sig(-gates)).astype(dgt_ref.dtype)

    rev = lambda w, c: pl.BlockSpec((R, w), lambda i, c=c: (NG - 1 - i, c))
    st = lambda *s: pl.BlockSpec((G,) + s, lambda i: (NG - 1 - i,) + (0,) * len(s))
    return pl.pallas_call(
        body, name="mlstm_bwd", grid=(NG,),
        in_specs=[rev(HALF, 0), rev(HALF, C_MV // 4), rev(128, C_GATES), rev(HALF, 0),
                  st(HEADS, ML_DQK, 128), st(HEADS, 1, ML_DQK), st(1, 128)],
        out_specs=[rev(HALF, 0), rev(HALF, 0), rev(128, 0)],
        out_shape=[_sds((T, HALF)), _sds((T, HALF), _MXU), _sds((T, 128), _MXU)],
        scratch_shapes=[pltpu.VMEM((HEADS, ML_DQK, 128), F32), pltpu.VMEM((HEADS, 1, ML_DQK), F32)],
        compiler_params=_cp("arbitrary"))(qkc, u, u, dh, cst, nst, mst)


def _head_norm(o):
    rs_parts, r_parts = [], []
    for h in range(HEADS):
        oh = o[:, 128 * h:128 * (h + 1)]
        rs = lax.rsqrt(jnp.mean(oh * oh, axis=-1, keepdims=True) + RMS_EPS)
        rs_parts.append(rs)
        r_parts.append(oh * rs)
    return jnp.concatenate(r_parts, axis=1), rs_parts


def _out_proj_ln(x, u, o_hg, h_ml, g_hg, g_ml, w_out, ln_g, ln_b, tm):
    T = x.shape[0]

    def body(x_ref, hgate_ref, mo_ref, ohg_ref, hml_ref, ghg_ref, gml_ref, w_ref, g_ref, b_ref,
             m_ref, z_ref, x1_ref):
        hgate = hgate_ref[...]
        a = _head_norm(ohg_ref[...])[0] * ghg_ref[...] * (hgate * _sig(hgate))
        b = _head_norm(hml_ref[...])[0] * gml_ref[...] * _sig(mo_ref[...])
        m = jnp.concatenate([a, b], axis=1)
        m_ref[...] = m.astype(m_ref.dtype)
        z = ALPHA * x_ref[...] + _dot(m, w_ref[...])
        z_ref[...] = z
        x1_ref[...] = _ln_fwd(z, g_ref[...], b_ref[...])[0]

    return pl.pallas_call(
        body, name="out_proj_ln1", grid=(T // tm,),
        in_specs=[_row(tm, D_MODEL), _row(tm, HALF, C_HGATE // 4), _row(tm, HALF, C_MO // 4),
                  _row(tm, HALF), _row(tm, HALF), _full(g_hg.shape), _full(g_ml.shape),
                  _full(w_out.shape), _full(ln_g.shape), _full(ln_b.shape)],
        out_specs=[_row(tm, D_MODEL)] * 3,
        out_shape=[_sds((T, D_MODEL), _MXU), _sds((T, D_MODEL)), _sds((T, D_MODEL))],
        compiler_params=_cp("parallel"))(x, u, u, o_hg, h_ml, g_hg, g_ml, w_out, ln_g, ln_b)


def _ffn_ln(x1, wg, wu, wd, ln_g, ln_b, tm):
    T = x1.shape[0]

    def body(x_ref, wg_ref, wu_ref, wd_ref, g_ref, b_ref, z_ref, x2_ref, a_ref, bb_ref, h_ref):
        x = x_ref[...]
        a = _dot(x, wg_ref[...])
        bb = _dot(x, wu_ref[...])
        hh = a * _sig(a) * bb
        a_ref[...] = a.astype(a_ref.dtype)
        bb_ref[...] = bb.astype(bb_ref.dtype)
        h_ref[...] = hh.astype(h_ref.dtype)
        z = ALPHA * x + _dot(hh, wd_ref[...])
        z_ref[...] = z
        x2_ref[...] = _ln_fwd(z, g_ref[...], b_ref[...])[0]

    return pl.pallas_call(
        body, name="ffn_ln2", grid=(T // tm,),
        in_specs=[_row(tm, D_MODEL), _full(wg.shape), _full(wu.shape), _full(wd.shape),
                  _full(ln_g.shape), _full(ln_b.shape)],
        out_specs=[_row(tm, D_MODEL)] * 2 + [_row(tm, D_FF)] * 3,
        out_shape=[_sds((T, D_MODEL))] * 2 + [_sds((T, D_FF), _MXU)] * 3,
        compiler_params=_cp("parallel"))(x1, wg, wu, wd, ln_g, ln_b)


def _ple_loss_ln2_bwd(x2, z2, p, tgt, wpg, bpg, wpp, ln_g, ln_b, tm):
    T = x2.shape[0]

    def body(x2_ref, z_ref, p_ref, t_ref, wpg_ref, bpg_ref, wpp_ref, g_ref, b_ref,
             de_ref, dgp_ref, dz_ref, loss_ref, dbpg_ref, dg_ref, db_ref):
        @pl.when(pl.program_id(0) == 0)
        def _():
            for r in (loss_ref, dbpg_ref, dg_ref, db_ref):
                r[...] = jnp.zeros_like(r)

        x2 = x2_ref[...]
        gate = _sig(_dot(x2, wpg_ref[...]) + bpg_ref[...])
        e = _dot(p_ref[...], wpp_ref[...])
        err = x2 + gate * e - t_ref[...]
        loss_ref[...] += _colsum(err * err)
        dy = err * (1.0 / D_MODEL)
        de_ref[...] = (dy * gate).astype(de_ref.dtype)
        dgp = dy * e * gate * (1.0 - gate)
        dgp_ref[...] = dgp.astype(dgp_ref.dtype)
        dbpg_ref[...] += _colsum(dgp)
        dx2 = dy + _dot(dgp, wpg_ref[...], NT)
        _, xhat, rstd = _ln_fwd(z_ref[...], g_ref[...], b_ref[...])
        dg_ref[...] += _colsum(dx2 * xhat)
        db_ref[...] += _colsum(dx2)
        dz_ref[...] = _ln_bwd(dx2, xhat, rstd, g_ref[...])

    vec = _full((1, D_MODEL))
    return pl.pallas_call(
        body, name="ple_loss_ln2_bwd", grid=(T // tm,),
        in_specs=[_row(tm, D_MODEL), _row(tm, D_MODEL), _row(tm, PLE_DIM), _row(tm, D_MODEL),
                  _full(wpg.shape), vec, _full(wpp.shape), vec, vec],
        out_specs=[_row(tm, D_MODEL)] * 3 + [vec] * 4,
        out_shape=[_sds((T, D_MODEL), _MXU)] * 2 + [_sds((T, D_MODEL))] + [_sds((1, D_MODEL))] * 4,
        compiler_params=_cp("arbitrary"))(x2, z2, p, tgt, wpg, bpg, wpp, ln_g, ln_b)


def _ffn_bwd_ln1_bwd(a_pre, b_pre, z1, dz2, wg, wu, wd, ln_g, ln_b, tm, comms=None):
    T = z1.shape[0]

    def body(a_ref, bb_ref, z_ref, dz2_ref, wg_ref, wu_ref, wd_ref, g_ref, b_ref,
             da_ref, dbb_ref, dz1_ref, dg_ref, db_ref):
        @pl.when(pl.program_id(0) == 0)
        def _():
            dg_ref[...] = jnp.zeros_like(dg_ref)
            db_ref[...] = jnp.zeros_like(db_ref)

        dz2 = dz2_ref[...]
        a = a_ref[...].astype(F32)
        bb = bb_ref[...].astype(F32)
        sa = _sig(a)
        act = a * sa
        dh = _dot(dz2, wd_ref[...], NT)
        da = (dh * bb * _dsilu(a, sa)).astype(da_ref.dtype)
        dbb = (dh * act).astype(dbb_ref.dtype)
        da_ref[...] = da
        dbb_ref[...] = dbb
        dx1 = ALPHA * dz2 + _dot(da, wg_ref[...], NT) + _dot(dbb, wu_ref[...], NT)
        _, xhat, rstd = _ln_fwd(z_ref[...], g_ref[...], b_ref[...])
        dg_ref[...] += _colsum(dx1 * xhat)
        db_ref[...] += _colsum(dx1)
        dz1_ref[...] = _ln_bwd(dx1, xhat, rstd, g_ref[...])

    vec = _full((1, D_MODEL))
    return _hosted_call(
        body, comms, name="ffn_bwd_ln1_bwd", grid=(T // tm,),
        in_specs=[_row(tm, D_FF)] * 2 + [_row(tm, D_MODEL)] * 2 + [_full(wg.shape), _full(wu.shape), _full(wd.shape), vec, vec],
        out_specs=[_row(tm, D_FF)] * 2 + [_row(tm, D_MODEL), vec, vec],
        out_shape=[_sds((T, D_FF), _MXU)] * 2 + [_sds((T, D_MODEL)), _sds((1, D_MODEL)), _sds((1, D_MODEL))],
        scratch_shapes=[], args=(a_pre, b_pre, z1, dz2, wg, wu, wd, ln_g, ln_b))


def _out_proj_bwd(dz1, u, o_hg, h_ml, g_hg, g_ml, w_out, tm):
    T = dz1.shape[0]

    def body(dz_ref, hgate_ref, mo_ref, ohg_ref, hml_ref, ghg_ref, gml_ref, w_ref,
             dohg_ref, dhml_ref, dhgate_ref, dmo_ref, dghg_ref, dgml_ref):
        @pl.when(pl.program_id(0) == 0)
        def _():
            dghg_ref[...] = jnp.zeros_like(dghg_ref)
            dgml_ref[...] = jnp.zeros_like(dgml_ref)

        dm = _dot(dz_ref[...], w_ref[...], NT)

        def half(dmh, o, gvec, gate_val, dgate_fac, do_ref, dgate_ref, dgvec_ref):
            r, rs = _head_norm(o)
            dgate_ref[...] = (dmh * r * gvec * dgate_fac).astype(dgate_ref.dtype)
            dn = dmh * gate_val
            dgvec_ref[...] += _colsum(dn * r)
            dr = dn * gvec
            parts = []
            for h in range(HEADS):
                sl = slice(128 * h, 128 * (h + 1))
                parts.append(rs[h] * (dr[:, sl] - r[:, sl] * jnp.mean(dr[:, sl] * r[:, sl], axis=-1, keepdims=True)))
            do_ref[...] = jnp.concatenate(parts, axis=1)

        hg = hgate_ref[...]
        shg = _sig(hg)
        half(dm[:, :HALF], ohg_ref[...], ghg_ref[...], hg * shg, _dsilu(hg, shg), dohg_ref, dhgate_ref, dghg_ref)
        smo = _sig(mo_ref[...])
        half(dm[:, HALF:], hml_ref[...], gml_ref[...], smo, smo * (1.0 - smo), dhml_ref, dmo_ref, dgml_ref)

    vec = _full((1, HALF))
    return pl.pallas_call(
        body, name="out_proj_bwd", grid=(T // tm,),
        in_specs=[_row(tm, D_MODEL), _row(tm, HALF, C_HGATE // 4), _row(tm, HALF, C_MO // 4),
                  _row(tm, HALF), _row(tm, HALF), vec, vec, _full(w_out.shape)],
        out_specs=[_row(tm, HALF)] * 4 + [vec, vec],
        out_shape=[_sds((T, HALF))] * 2 + [_sds((T, HALF), _MXU)] * 2 + [_sds((1, HALF))] * 2,
        compiler_params=_cp("arbitrary"))(dz1, u, u, o_hg, h_ml, g_hg, g_ml, w_out)


def _conv_bwd(u, pre, dqkc, cw, tm):
    T = u.shape[0]
    hb = tm // 8
    nb = T // 8

    def body(x_ref, xh_ref, pre_ref, preh_ref, d_ref, dh_ref, w_ref, dx_ref, dw_ref, db_ref):
        i = pl.program_id(0)

        @pl.when(i == 0)
        def _():
            dw_ref[...] = jnp.zeros_like(dw_ref)
            db_ref[...] = jnp.zeros_like(db_ref)

        def dpre_of(pre, d):
            return d * _dsilu(pre, _sig(pre))

        rowi = _iota((8, HALF), 0)
        dpre = dpre_of(pre_ref[...], d_ref[...])
        dpre_next = jnp.where(i < pl.num_programs(0) - 1, dpre_of(preh_ref[...], dh_ref[...]), 0.0)
        x = x_ref[...]
        xprev = jnp.where(i > 0, xh_ref[...], 0.0)
        dx = dpre * w_ref[3:4, :]
        db_ref[...] += _colsum(dpre)
        dws = [None] * 4
        dws[3] = _colsum(dpre * x)
        for j in (1, 2, 3):
            dx = dx + _shift_rows_up(dpre, dpre_next, j, rowi) * w_ref[3 - j:4 - j, :]
            dws[3 - j] = _colsum(dpre * _shift_rows(x, xprev, j, rowi))
        dx_ref[...] = dx.astype(dx_ref.dtype)
        dw_ref[...] += jnp.concatenate(dws, axis=0)

    cur = lambda blk: pl.BlockSpec((tm, HALF), lambda i, blk=blk: (i, blk))
    nxt = pl.BlockSpec((8, HALF), lambda i: (jnp.minimum((i + 1) * hb, nb - 1), 0))
    return pl.pallas_call(
        body, name="conv_bwd", grid=(T // tm,),
        in_specs=[cur(C_MQK // 4), pl.BlockSpec((8, HALF), lambda i: (jnp.maximum(i * hb - 1, 0), C_MQK // 4)),
                  cur(0), nxt, cur(0), nxt, _full(cw.shape)],
        out_specs=[cur(0), _full((4, HALF)), _full((1, HALF))],
        out_shape=[_sds((T, HALF), _MXU), _sds((4, HALF)), _sds((1, HALF))],
        compiler_params=_cp("arbitrary"))(u, u, pre, pre, dqkc, dqkc, cw)


def _du_specs(rows):
    return [pl.BlockSpec((rows, w), lambda i: (i, 0)) for w in DU_WIDTHS]


def _in_proj_bwd(dz1, du_parts, w, tm, comms=None):
    T = dz1.shape[0]

    def body(dz_ref, *refs):
        du = jnp.concatenate([r[...] for r in refs[:8]], axis=1)
        refs[9][...] = ALPHA * dz_ref[...] + _dot(du, refs[8][...], NT)

    (dx,), got = _hosted_call(
        body, comms, name="in_proj_bwd", grid=(T // tm,),
        in_specs=[_row(tm, D_MODEL)] + _du_specs(tm) + [_full(w.shape)],
        out_specs=[_row(tm, D_MODEL)], out_shape=[_sds((T, D_MODEL))], scratch_shapes=[], args=(dz1, *du_parts, w))
    return dx, got


def _wgrad(a, b, name, tm, tn, tk):
    T, M = a.shape
    N = b.shape[1]
    tm, tn, tk = min(tm, M), min(tn, N), min(tk, T)
    nk = T // tk

    def body(a_ref, b_ref, o_ref, acc_ref):
        kk = pl.program_id(2)

        @pl.when(kk == 0)
        def _():
            acc_ref[...] = jnp.zeros_like(acc_ref)

        acc_ref[...] += _dot(a_ref[...], b_ref[...], TN)

        @pl.when(kk == nk - 1)
        def _():
            o_ref[...] = acc_ref[...].astype(o_ref.dtype)

    return pl.pallas_call(
        body, name=name, grid=(M // tm, N // tn, nk),
        in_specs=[pl.BlockSpec((tk, tm), lambda i, j, kk: (kk, i)), pl.BlockSpec((tk, tn), lambda i, j, kk: (kk, j))],
        out_specs=pl.BlockSpec((tm, tn), lambda i, j, kk: (i, j)), out_shape=_sds((M, N), _MXU),
        scratch_shapes=[pltpu.VMEM((tm, tn), F32)],
        compiler_params=_cp("parallel", "parallel", "arbitrary"))(a, b)


def _wgrad_w_in(x, du_parts, tk):
    T = x.shape[0]
    tk = min(tk, T)
    nk = T // tk

    def body(a_ref, *refs):
        o_ref, cs_ref, acc_ref = refs[8:]
        kk = pl.program_id(0)

        @pl.when(kk == 0)
        def _():
            acc_ref[...] = jnp.zeros_like(acc_ref)
            cs_ref[...] = jnp.zeros_like(cs_ref)

        du = jnp.concatenate([r[...] for r in refs[:8]], axis=1)
        acc_ref[...] += _dot(a_ref[...], du, TN)
        cs_ref[...] += _colsum(du.astype(F32))

        @pl.when(kk == nk - 1)
        def _():
            o_ref[...] = acc_ref[...].astype(o_ref.dtype)

    return pl.pallas_call(
        body, name="wgrad_w_in", grid=(nk,),
        in_specs=[pl.BlockSpec((tk, D_MODEL), lambda kk: (kk, 0))] + _du_specs(tk),
        out_specs=[_full((D_MODEL, PROJ_WP)), _full((1, PROJ_WP))],
        out_shape=[_sds((D_MODEL, PROJ_WP), _MXU), _sds((1, PROJ_WP))],
        scratch_shapes=[pltpu.VMEM((D_MODEL, PROJ_WP), F32)], compiler_params=_cp("arbitrary"))(x, *du_parts)


W_IN_S, FF_S, OUT_S, PP_S = PROJ_W // N_DEV, D_FF // N_DEV, D_MODEL // N_DEV, D_MODEL // N_DEV
LATE = ("w_ffn_gate", "w_ffn_up", "w_out", "w_ffn_down", "ple_w_gate", "ple_w_proj")
BIG = ("w_in",) + LATE


def _split_cols(a, n):
    return a.reshape(a.shape[0], N_DEV, n).transpose(1, 0, 2)


def _join_cols(a):
    return a.transpose(1, 0, 2).reshape(a.shape[1], -1)


def _step(x, p, tgt, w_in, b_in, lb_logits, conv_w, conv_b, g_hg, g_ml, ln1_g, ln1_b, ln2_g, ln2_b, bpg, late,
          distributed):
    T = x.shape[0]
    tm, tf = min(ROWS, T), min(ROWS_FFN, T)
    gather = lambda *names: [_Comm("gather", [late[n] for n in names])] if distributed else None
    scatter = lambda *arrs: [_Comm("scatter", list(arrs))] if distributed else None
    rows = lambda a, n: a.reshape(N_DEV, n, D_MODEL)
    u, got1 = _in_proj(x, w_in, b_in, tm, gather("w_out", "ple_w_gate", "ple_w_proj"))
    pre, qkc = _conv_fwd(u, conv_w, conv_b, tm)
    (o_hg, hg_states), got2 = _hgrn2_fwd(u, lb_logits, gather("w_ffn_gate", "w_ffn_up"))
    (h_ml, cst, nst, mst), got3 = _mlstm_fwd(qkc, u, gather("w_ffn_down"))
    if distributed:
        w_out, wpg, wpp = got1[0][0].reshape(D_MODEL, D_MODEL), got1[0][1].reshape(D_MODEL, D_MODEL), _join_cols(got1[0][2])
        wg, wu, wd = _join_cols(got2[0][0]), _join_cols(got2[0][1]), got3[0][0].reshape(D_FF, D_MODEL)
    else:
        w_out, wg, wu, wd, wpg, wpp = (late[n] for n in ("w_out", "w_ffn_gate", "w_ffn_up", "w_ffn_down", "ple_w_gate", "ple_w_proj"))
    m_in, z1, x1 = _out_proj_ln(x, u, o_hg, h_ml, g_hg, g_ml, w_out, ln1_g, ln1_b, tm)
    z2, x2, a_pre, b_pre, hh = _ffn_ln(x1, wg, wu, wd, ln2_g, ln2_b, tf)
    de, dgp, dz2, loss_vec, d_bpg, d_ln2g, d_ln2b = _ple_loss_ln2_bwd(x2, z2, p, tgt, wpg, bpg, wpp, ln2_g, ln2_b, tm)
    big = dict(ple_w_gate=_wgrad(x2, dgp, "wgrad_ple_gate", 512, D_MODEL, 1024),
               ple_w_proj=_wgrad(p, de, "wgrad_ple_proj", 512, D_MODEL, 1024))
    (da, dbb, dz1, d_ln1g, d_ln1b), r1 = _ffn_bwd_ln1_bwd(
        a_pre, b_pre, z1, dz2, wg, wu, wd, ln1_g, ln1_b, tf, scatter(rows(big["ple_w_gate"], OUT_S), _split_cols(big["ple_w_proj"], PP_S)))
    big.update(
        w_ffn_gate=_wgrad(x1, da, "wgrad_ffn_gate", 512, D_FF, 1024),
        w_ffn_up=_wgrad(x1, dbb, "wgrad_ffn_up", 512, D_FF, 1024),
        w_ffn_down=_wgrad(hh, dz2, "wgrad_ffn_down", D_FF, D_MODEL, 1024),
        w_out=_wgrad(m_in, dz1, "wgrad_w_out", 512, D_MODEL, 1024))
    d_ohg, d_hml, d_hgate, d_mo, d_ghg, d_gml = _out_proj_bwd(dz1, u, o_hg, h_ml, g_hg, g_ml, w_out, tm)
    (d_hq, d_hf, d_hv, d_lb), r2 = _hgrn2_bwd(
        u, lb_logits, d_ohg, hg_states,
        scatter(_split_cols(big["w_ffn_gate"], FF_S), _split_cols(big["w_ffn_up"], FF_S), rows(big["w_ffn_down"], FF_S),
                rows(big["w_out"], OUT_S)))
    d_qkc, d_mv, d_gates = _mlstm_bwd(qkc, u, d_hml, cst, nst, mst)
    d_mqk, d_convw, d_convb = _conv_bwd(u, pre, d_qkc, conv_w, tm)
    du_parts = [d_hq, d_hf, d_hv, d_hgate, d_mqk, d_mv, d_mo, d_gates]
    big["w_in"], d_bin = _wgrad_w_in(x, du_parts, 256)
    small = dict(b_in=d_bin, hg_lb_logits=d_lb, ml_conv_w=d_convw, ml_conv_b=d_convb, hg_norm_g=d_ghg, ml_norm_g=d_gml,
                 ln1_g=d_ln1g, ln1_b=d_ln1b, ln2_g=d_ln2g, ln2_b=d_ln2b, ple_b_gate=d_bpg)
    last = [_Comm("scatter", [_split_cols(big["w_in"][:, :PROJ_W], W_IN_S)]),
            _Comm("gather", [_pack_small(small, loss_vec)])] if distributed else None
    dx, r3 = _in_proj_bwd(dz1, du_parts, w_in, tm, last)
    gathered_small = None
    if distributed:
        big = dict(ple_w_gate=r1[0][0], ple_w_proj=r1[0][1], w_ffn_gate=r2[0][0], w_ffn_up=r2[0][1],
                   w_ffn_down=r2[0][2], w_out=r2[0][3], w_in=r3[0][0])
        gathered_small = r3[1][0]
    return loss_vec, dx, big, small, gathered_small


SMALL = dict(b_in=(8, 32, PROJ_WP), hg_lb_logits=(40, 8, 1024), ml_conv_w=(48, 16, 2048), ml_conv_b=(64, 8, 512),
             hg_norm_g=(72, 8, 512), ml_norm_g=(80, 8, 512), ln1_g=(88, 8, 1024), ln1_b=(96, 8, 1024),
             ln2_g=(104, 8, 1024), ln2_b=(112, 8, 1024), ple_b_gate=(120, 8, 1024))
SM_ROWS = 128


def _padc(a, n):
    return jnp.pad(a, [(0, 0)] * (a.ndim - 1) + [(0, n - a.shape[-1])])


def _pack_small(d, loss_vec=None):
    first = jnp.zeros((8, 128), F32) if loss_vec is None else loss_vec.reshape(8, 128)
    parts = [first]
    for name, (_, rows, n) in SMALL.items():
        parts.append(jnp.pad(d[name].reshape(-1), (0, rows * 128 - n)).reshape(rows, 128))
    return jnp.concatenate(parts, axis=0)


def _unpack_small(slab, shapes):
    return {name: slab[r0:r0 + rows].reshape(-1)[:n].reshape(shapes[name]) for name, (r0, rows, n) in SMALL.items()}


def _adamw(w, g, m, v):
    m = B1 * m + (1.0 - B1) * g
    v = B2 * v + (1.0 - B2) * jnp.square(g)
    m_hat = m / (1.0 - B1 ** STEP)
    v_hat = v / (1.0 - B2 ** STEP)
    return -LR * (m_hat / (jnp.sqrt(v_hat) + EPS) + WD * w), m, v


def _sum_slabs(ref):
    g = ref[0].astype(F32)
    for j in range(1, N_DEV):
        g = g + ref[j].astype(F32)
    return g


def _adamw_matrix(rb, w, m, v, name):
    R, C = w.shape
    tr = 256 if R % 256 == 0 else R

    def body(rb_ref, w_ref, m_ref, v_ref, g_ref, d_ref, m2_ref, v2_ref):
        g = _sum_slabs(rb_ref)
        g_ref[...] = g
        d_ref[...], m2_ref[...], v2_ref[...] = _adamw(w_ref[...], g, m_ref[...], v_ref[...])

    blk = pl.BlockSpec((tr, C), lambda i: (i, 0))
    return pl.pallas_call(
        body, name=name, grid=(R // tr,),
        in_specs=[pl.BlockSpec((N_DEV, tr, C), lambda i: (0, i, 0)), blk, blk, blk],
        out_specs=[blk] * 4, out_shape=[_sds((R, C))] * 4, compiler_params=_cp("parallel"))(rb, w, m, v)


def _adamw_small(sg, w, m, v):
    def body(sg_ref, w_ref, m_ref, v_ref, loss_ref, g_ref, d_ref, m2_ref, v2_ref):
        g = _sum_slabs(sg_ref)
        loss_ref[...] = (0.5 / D_MODEL) * jnp.sum(g[0:8], keepdims=True)
        g_ref[...] = g
        d_ref[...], m2_ref[...], v2_ref[...] = _adamw(w_ref[...], g, m_ref[...], v_ref[...])

    return pl.pallas_call(
        body, name="adamw_small", out_shape=[_sds((1, 1))] + [_sds((SM_ROWS, 128))] * 4)(sg, w, m, v)


WEIGHTS = ("w_in", "b_in", "hg_lb_logits", "ml_conv_w", "ml_conv_b", "hg_norm_g", "ml_norm_g", "w_out", "ln1_g", "ln1_b",
           "w_ffn_gate", "w_ffn_up", "w_ffn_down", "ln2_g", "ln2_b", "ple_w_proj", "ple_w_gate", "ple_b_gate")
CONV_S = HALF // N_DEV


def kernel(x, p, w_in, b_in, hg_lb_logits, ml_conv_w, ml_conv_b, hg_norm_g, ml_norm_g, w_out, ln1_g, ln1_b, w_ffn_gate, w_ffn_up, w_ffn_down, ln2_g, ln2_b, ple_w_proj, ple_w_gate, ple_b_gate, loss_target, m_w_in, m_b_in, m_hg_lb_logits, m_ml_conv_w, m_ml_conv_b, m_hg_norm_g, m_ml_norm_g, m_w_out, m_ln1_g, m_ln1_b, m_w_ffn_gate, m_w_ffn_up, m_w_ffn_down, m_ln2_g, m_ln2_b, m_ple_w_proj, m_ple_w_gate, m_ple_b_gate, v_w_in, v_b_in, v_hg_lb_logits, v_ml_conv_w, v_ml_conv_b, v_hg_norm_g, v_ml_norm_g, v_w_out, v_ln1_g, v_ln1_b, v_w_ffn_gate, v_w_ffn_up, v_w_ffn_down, v_ln2_g, v_ln2_b, v_ple_w_proj, v_ple_w_gate, v_ple_b_gate):
    args = locals()
    me = 4 * lax.axis_index("x") + 2 * lax.axis_index("y") + lax.axis_index("c")
    shapes = {n: args[n].shape for n in WEIGHTS}
    drop = lambda n, a: a[0] if n in BIG or n == "ml_conv_w" else a
    W = {n: drop(n, args[n]) for n in WEIGHTS}
    M = {n: drop(n, args["m_" + n]) for n in WEIGHTS}
    V = {n: drop(n, args["v_" + n]) for n in WEIGHTS}

    g_in, g_conv = _gather_two_level(
        [W["w_in"].astype(_MXU), jnp.pad(W["ml_conv_w"], ((0, 4), (0, 128 - CONV_S)))], "gather_w_in")
    w_in_full = _padc(_join_cols(g_in), PROJ_WP)
    conv_full = _join_cols(g_conv[:, :4, :CONV_S])

    _, dx, big, _, sg = _step(
        x[0], p[0, 0], loss_target[0], w_in_full, _padc(b_in, PROJ_WP), hg_lb_logits, conv_full, ml_conv_b,
        hg_norm_g, ml_norm_g, ln1_g, ln1_b, ln2_g, ln2_b, ple_b_gate, {n: W[n].astype(_MXU) for n in LATE}, True)

    upd = {n: _adamw_matrix(big[n], W[n], M[n], V[n], "adamw_" + n) for n in BIG}
    place = lambda d: {**d, "b_in": _padc(d["b_in"], PROJ_WP),
                       "ml_conv_w": lax.dynamic_update_slice(jnp.zeros((4, HALF), F32), d["ml_conv_w"], (0, me * CONV_S))}
    loss, *small_upd = _adamw_small(sg, _pack_small(place(W)), _pack_small(place(M)), _pack_small(place(V)))

    outs = []
    sm_shapes = {**{n: shapes[n] for n in SMALL}, "b_in": (1, PROJ_WP), "ml_conv_w": (4, HALF)}
    for kind in range(4):
        smalls = _unpack_small(small_upd[kind], sm_shapes)
        smalls["b_in"] = smalls["b_in"][:, :PROJ_W]
        smalls["ml_conv_w"] = lax.dynamic_slice(smalls["ml_conv_w"], (0, me * CONV_S), (4, CONV_S))
        for n in WEIGHTS:
            outs.append((upd[n][kind] if n in BIG else smalls[n]).reshape(shapes[n]))
    return (loss.reshape(()), dx.reshape(x.shape), *outs)
```

```python
import jax
import jax.numpy as jnp
from jax import lax
from jax.experimental import pallas as pl
from jax.experimental.pallas import tpu as pltpu

F32 = jnp.float32
_MXU = jnp.bfloat16

D_MODEL = 1024
CHUNK = 64
SUB = 16
PLE_DIM = 256
HEADS = 4
ML_DQK = 64
HALF = 512
D_FF = 2816
PROJ_W = 3592
PROJ_WP = 3712
ALPHA = float(2 ** 0.25)
LN_EPS = 1e-5
RMS_EPS = 1e-6
ML_SCALE = ML_DQK ** -0.5
N_DEV = 8
LR, B1, B2, EPS, WD, STEP = 0.001, 0.9, 0.999, 1e-08, 0.01, 10
NEG = -1e30

C_HQ, C_HF, C_HV, C_HGATE, C_MQK, C_MV, C_MO, C_GATES = 0, 4, 8, 12, 16, 20, 24, 28
DU_WIDTHS = (HALF,) * 7 + (128,)

VMEM_LIMIT = 52 * 1024 * 1024
GC = 8
ROWS = 512
ROWS_FFN = 256

NN = (((1,), (0,)), ((), ()))
NT = (((1,), (1,)), ((), ()))
TN = (((0,), (0,)), ((), ()))
BNT = (((2,), (2,)), ((0,), (0,)))
BNN = (((2,), (1,)), ((0,), (0,)))
BTN = (((1,), (1,)), ((0,), (0,)))


def _dot(a, b, dims=NN):
    return lax.dot_general(a.astype(_MXU), b.astype(_MXU), dims, preferred_element_type=F32)


def _dotx(a, b, dims=NN):
    return lax.dot_general(a, b, dims, precision=lax.Precision.HIGHEST, preferred_element_type=F32)


def _sig(x):
    return jax.nn.sigmoid(x)


def _cp(*sem):
    return pltpu.CompilerParams(dimension_semantics=sem, vmem_limit_bytes=VMEM_LIMIT)


def _row(tm, c, blk=0):
    return pl.BlockSpec((tm, c), lambda i, blk=blk: (i, blk))


def _full(shape):
    nd = len(shape)
    return pl.BlockSpec(tuple(shape), lambda *_, nd=nd: (0,) * nd)


def _sds(shape, dtype=F32):
    return jax.ShapeDtypeStruct(tuple(shape), dtype)


def _iota(shape, axis):
    return lax.broadcasted_iota(jnp.int32, shape, axis)


def _colsum(x):
    return jnp.sum(x, axis=0, keepdims=True)


def _rowsum(x):
    return jnp.sum(x, axis=1, keepdims=True)


def _ln_fwd(z, g, b):
    mu = jnp.mean(z, axis=-1, keepdims=True)
    zc = z - mu
    var = jnp.mean(zc * zc, axis=-1, keepdims=True)
    rstd = lax.rsqrt(var + LN_EPS)
    xhat = zc * rstd
    return xhat * g + b, xhat, rstd


def _ln_bwd(dy, xhat, rstd, g):
    dxh = dy * g
    m1 = jnp.mean(dxh, axis=-1, keepdims=True)
    m2 = jnp.mean(dxh * xhat, axis=-1, keepdims=True)
    return rstd * (dxh - m1 - xhat * m2)


def _dsilu(x, s):
    return s * (1.0 + x * (1.0 - s))


MESH = pl.DeviceIdType.MESH
ANY = pl.BlockSpec(memory_space=pl.ANY)


def _flip(v, bit):
    return 1 - v if bit else v


class _Comm:
    def __init__(self, kind, srcs):
        self.kind, self.srcs, self.n = kind, list(srcs), len(srcs)

    def out_shape(self):
        lead = (N_DEV,) if self.kind == "gather" else ()
        return [jax.ShapeDtypeStruct(lead + s.shape, s.dtype) for s in self.srcs]

    def scratch(self):
        return [pltpu.SemaphoreType.DMA((7 * self.n,)), pltpu.SemaphoreType.DMA((7 * self.n,)),
                pltpu.SemaphoreType.DMA((self.n,))]

    def copies(self, srcs, dsts, send_sems, recv_sems, local_sems):
        x, y, c = lax.axis_index("x"), lax.axis_index("y"), lax.axis_index("c")
        me = 4 * x + 2 * y + c
        pick = (lambda s, j: s) if self.kind == "gather" else (lambda s, j: s.at[j])
        out = []
        for i, (s, d) in enumerate(zip(srcs, dsts)):
            out.append(pltpu.make_async_copy(pick(s, me), d.at[me], local_sems.at[i]))
            for k in range(1, N_DEV):
                px, py, pc = _flip(x, k & 4), _flip(y, k & 2), _flip(c, k & 1)
                out.append(pltpu.make_async_remote_copy(
                    src_ref=pick(s, 4 * px + 2 * py + pc), dst_ref=d.at[me], send_sem=send_sems.at[7 * i + k - 1],
                    recv_sem=recv_sems.at[7 * i + k - 1], device_id=(px, py, pc), device_id_type=MESH))
        return out

    def start(self, *refs):
        for cp in self.copies(*refs):
            cp.start()

    def mid(self, *refs):
        pass

    def finish(self, *refs):
        for cp in self.copies(*refs):
            cp.wait()


class _GatherTwoLevel(_Comm):
    def __init__(self, srcs):
        super().__init__("gather", srcs)

    def _parts(self, srcs, dsts, send_sems, recv_sems, local_sems):
        x, y, c = lax.axis_index("x"), lax.axis_index("y"), lax.axis_index("c")
        me, sibling = (x, y, c), (x, y, 1 - c)
        chips = [(1 - x, y), (x, 1 - y), (1 - x, 1 - y)]

        def copy(i, k, block, to, own=False):
            slab = dsts[i].at[4 * block[0] + 2 * block[1] + block[2]]
            return pltpu.make_async_remote_copy(
                src_ref=srcs[i] if own else slab, dst_ref=slab, send_sem=send_sems.at[7 * i + k],
                recv_sem=recv_sems.at[7 * i + k], device_id=to, device_id_type=MESH)

        n = range(self.n)
        mine = [pltpu.make_async_copy(srcs[i], dsts[i].at[4 * x + 2 * y + c], local_sems.at[i]) for i in n]
        first = [copy(i, 0, me, sibling, own=True) for i in n]
        first += [copy(i, 1 + j, me, (*chip, c), own=True) for j, chip in enumerate(chips) for i in n]
        over_ici = [copy(i, 1 + j, (*chip, c), me) for j, chip in enumerate(chips) for i in n]
        passed = [copy(i, 4 + j, (*chip, c), sibling) for j, chip in enumerate(chips) for i in n]
        from_sibling = [copy(i, 0, sibling, me) for i in n]
        from_sibling += [copy(i, 4 + j, (*chip, 1 - c), me) for j, chip in enumerate(chips) for i in n]
        return mine, first, over_ici, passed, from_sibling

    def start(self, *refs):
        mine, first, _, _, _ = self._parts(*refs)
        for cp in mine + first:
            cp.start()

    def mid(self, *refs):
        _, _, over_ici, passed, _ = self._parts(*refs)
        for arrived, onward in zip(over_ici, passed):
            arrived.wait_recv()
            onward.start()

    def finish(self, *refs):
        mine, first, _, passed, from_sibling = self._parts(*refs)
        for cp in from_sibling:
            cp.wait_recv()
        for cp in first + passed:
            cp.wait_send()
        for cp in mine:
            cp.wait()


def _hosted_call(body, comms, *, name, grid, in_specs, out_specs, out_shape, scratch_shapes, args):
    comms = list(comms or [])
    if not comms:
        res = pl.pallas_call(body, name=name, grid=grid, in_specs=in_specs, out_specs=out_specs, out_shape=out_shape,
                             scratch_shapes=scratch_shapes, compiler_params=_cp("arbitrary"))(*args)
        return list(res), []
    n_in, n_out, n_sc, nc = len(in_specs), len(out_specs), len(scratch_shapes), sum(cm.n for cm in comms)
    last = grid[0] - 1

    def hosted(*refs):
        ins, csrc = refs[:n_in], refs[n_in:n_in + nc]
        o0 = n_in + nc
        outs, cdst = refs[o0:o0 + n_out], refs[o0 + n_out:o0 + n_out + nc]
        s0 = o0 + n_out + nc
        scr, sems = refs[s0:s0 + n_sc], refs[s0 + n_sc:]

        def phase(which):
            o = 0
            for j, cm in enumerate(comms):
                getattr(cm, which)(csrc[o:o + cm.n], cdst[o:o + cm.n], *sems[3 * j:3 * j + 3])
                o += cm.n

        i = pl.program_id(0)

        @pl.when(i == 0)
        def _():
            phase("start")

        body(*ins, *outs, *scr)

        @pl.when(i == (2 * last) // 3)
        def _():
            phase("mid")

        @pl.when(i == last)
        def _():
            phase("finish")

    res = pl.pallas_call(
        hosted, name=name, grid=grid, in_specs=list(in_specs) + [ANY] * nc, out_specs=list(out_specs) + [ANY] * nc,
        out_shape=list(out_shape) + [s for cm in comms for s in cm.out_shape()],
        scratch_shapes=list(scratch_shapes) + [s for cm in comms for s in cm.scratch()],
        compiler_params=_cp("arbitrary"))(*args, *[a for cm in comms for a in cm.srcs])
    got, o = [], n_out
    for cm in comms:
        got.append(list(res[o:o + cm.n]))
        o += cm.n
    return list(res[:n_out]), got


def _gather_two_level(blocks, name):
    n = len(blocks)

    def body(*refs):
        x_refs, out_refs = refs[:n], refs[n:2 * n]
        send_sems, recv_sems, local_sems = refs[2 * n:]
        x, y, c = lax.axis_index("x"), lax.axis_index("y"), lax.axis_index("c")
        me, sibling = (x, y, c), (x, y, 1 - c)
        chips = [(1 - x, y), (x, 1 - y), (1 - x, 1 - y)]

        def copy(i, k, block, to, own=False):
            slab = out_refs[i].at[4 * block[0] + 2 * block[1] + block[2]]
            return pltpu.make_async_remote_copy(
                src_ref=x_refs[i] if own else slab, dst_ref=slab, send_sem=send_sems.at[7 * i + k],
                recv_sem=recv_sems.at[7 * i + k], device_id=to, device_id_type=MESH)

        mine = [pltpu.make_async_copy(x_refs[i], out_refs[i].at[4 * x + 2 * y + c], local_sems.at[i]) for i in range(n)]
        for cp in mine:
            cp.start()
        first = [copy(i, 0, me, sibling, own=True) for i in range(n)]
        first += [copy(i, 1 + j, me, (*chip, c), own=True) for j, chip in enumerate(chips) for i in range(n)]
        for cp in first:
            cp.start()
        passed = []
        for j, chip in enumerate(chips):
            for i in range(n):
                copy(i, 1 + j, (*chip, c), me).wait_recv()
                passed.append(copy(i, 4 + j, (*chip, c), sibling))
                passed[-1].start()
        for i in range(n):
            copy(i, 0, sibling, me).wait_recv()
            for j, chip in enumerate(chips):
                copy(i, 4 + j, (*chip, 1 - c), me).wait_recv()
        for cp in first + passed:
            cp.wait_send()
        for cp in mine:
            cp.wait()

    return pl.pallas_call(
        body, name=name, out_shape=[jax.ShapeDtypeStruct((N_DEV,) + b.shape, b.dtype) for b in blocks],
        in_specs=[ANY] * n, out_specs=[ANY] * n,
        scratch_shapes=[pltpu.SemaphoreType.DMA((7 * n,)), pltpu.SemaphoreType.DMA((7 * n,)),
                        pltpu.SemaphoreType.DMA((n,))])(*blocks)


def _in_proj(x, w, b, tm, comms=None):
    T = x.shape[0]

    def body(x_ref, w_ref, b_ref, o_ref):
        o_ref[...] = _dot(x_ref[...], w_ref[...]) + b_ref[...]

    (u,), got = _hosted_call(
        body, comms, name="in_proj", grid=(T // tm,),
        in_specs=[_row(tm, D_MODEL), _full(w.shape), _full(b.shape)],
        out_specs=[_row(tm, PROJ_WP)], out_shape=[_sds((T, PROJ_WP))], scratch_shapes=[], args=(x, w, b))
    return u, got


def _shift_rows(x, halo, j, rowi):
    r = pltpu.roll(x, j, 0)
    top = jnp.where(rowi < j, pltpu.roll(halo, j, 0), r[:8])
    return jnp.concatenate([top, r[8:]], axis=0)


def _shift_rows_up(x, halo, j, rowi):
    n = x.shape[0]
    r = pltpu.roll(x, n - j, 0)
    bot = jnp.where(rowi >= 8 - j, pltpu.roll(halo, 8 - j, 0), r[n - 8:])
    return jnp.concatenate([r[:n - 8], bot], axis=0)


def _conv_fwd(u, cw, cb, tm):
    T = u.shape[0]
    hb = tm // 8

    def body(x_ref, halo_ref, w_ref, b_ref, pre_ref, out_ref):
        i = pl.program_id(0)
        x = x_ref[...]
        halo = jnp.where(i > 0, halo_ref[...], 0.0)
        rowi = _iota((8, HALF), 0)
        acc = x * w_ref[3:4, :] + b_ref[...]
        for j in (1, 2, 3):
            acc = acc + _shift_rows(x, halo, j, rowi) * w_ref[3 - j:4 - j, :]
        pre_ref[...] = acc
        out_ref[...] = acc * _sig(acc)

    return pl.pallas_call(
        body, name="conv_fwd", grid=(T // tm,),
        in_specs=[pl.BlockSpec((tm, HALF), lambda i: (i, C_MQK // 4)),
                  pl.BlockSpec((8, HALF), lambda i: (jnp.maximum(i * hb - 1, 0), C_MQK // 4)),
                  _full(cw.shape), _full(cb.shape)],
        out_specs=[_row(tm, HALF), _row(tm, HALF)], out_shape=[_sds((T, HALF)), _sds((T, HALF))],
        compiler_params=_cp("parallel"))(u, u, cw, cb)


def _bdot(a, b, dims):
    return lax.dot_general(a.astype(_MXU), b.astype(_MXU), dims, preferred_element_type=F32)


def _bdotx(a, b, dims):
    return lax.dot_general(a, b, dims, precision=lax.Precision.HIGHEST, preferred_element_type=F32)


def _heads_to_batch(x, w):
    G = x.shape[0] // CHUNK
    x3 = x.reshape(G, CHUNK, HEADS * w)
    return jnp.stack([x3[:, :, w * h:w * (h + 1)] for h in range(HEADS)], axis=1).reshape(G * HEADS, CHUNK, w)


def _batch_to_heads(x3):
    B, _, w = x3.shape
    x4 = x3.reshape(B // HEADS, HEADS, CHUNK, w)
    return jnp.concatenate([x4[:, h] for h in range(HEADS)], axis=-1).reshape(B // HEADS * CHUNK, HEADS * w)


def _chunk_cumsum(x, rowmod, reverse=False):
    R = x.shape[0]
    for sh in (1, 2, 4, 8, 16, 32):
        if reverse:
            x = x + jnp.where(rowmod < CHUNK - sh, pltpu.roll(x, R - sh, 0), 0.0)
        else:
            x = x + jnp.where(rowmod >= sh, pltpu.roll(x, sh, 0), 0.0)
    return x


def _lane_col(x, c, lane):
    return _rowsum(jnp.where(lane == c, x, 0.0))


def _hg_gates(hq, hf, lb):
    sg = _sig(hf)
    nsg = _sig(-hf)
    f = lb + (1.0 - lb) * sg
    g = jnp.log(f)
    k = (1.0 - lb) * nsg
    sq = _sig(hq)
    return hq * sq, g, k, f, sg, nsg, sq


def _hg_prep(hq_ref, hf_ref, lg_ref, b_sc, k_sc):
    R = hq_ref.shape[0]
    G = R // CHUNK
    lb = _sig(lg_ref[0:1, :] - lg_ref[1:2, :])
    hq = hq_ref[...]
    q, g, k, f, sg, nsg, sq = _hg_gates(hq, hf_ref[...], lb)
    rowmod = _iota((R, HALF), 0) & (CHUNK - 1)
    b = _chunk_cumsum(g, rowmod)
    last8 = _iota((8, HALF), 0) == 7
    bl_rows = [_colsum(jnp.where(last8, b[CHUNK * c + CHUNK - 8:CHUNK * (c + 1)], 0.0)) for c in range(G)]
    bl3 = jnp.stack([r[:, 128 * h:128 * (h + 1)] for r in bl_rows for h in range(HEADS)], axis=0)
    b3, k3 = _heads_to_batch(b, 128), _heads_to_batch(k, 128)
    b_sc[...] = b3
    k_sc[...] = k3
    return dict(G=G, lb=lb, hq=hq, f=f, sg=sg, nsg=nsg, sq=sq, rowmod=rowmod, q3=_heads_to_batch(q, 128), k3=k3, b3=b3,
                bl3=bl3)


HSUB = SUB // 2


def _lo(j):
    return HSUB * (j // HSUB)


def _hg_diag_tiles(b_sc, b3, r0, rowi):
    bi = b3[:, r0:r0 + SUB]
    return [jnp.exp(jnp.where(rowi[:, _lo(s):] >= s, bi[:, _lo(s):] - b_sc[:, r0 + s:r0 + s + 1, :], NEG))
            for s in range(SUB)]


def _hg_diag_tiles_t(b_sc, b3, r0, rowi):
    bi = b3[:, r0:r0 + SUB]
    return [jnp.exp(jnp.where(rowi[:, :_lo(t) + HSUB] <= t, b_sc[:, r0 + t:r0 + t + 1, :] - bi[:, :_lo(t) + HSUB], NEG))
            for t in range(SUB)]


def _lane_sums(pieces, ones):
    B = pieces[0].shape[0]
    hs = [p.shape[1] for p in pieces]
    R = _dot(jnp.concatenate(pieces, axis=1).reshape(B * sum(hs), 128), ones).reshape(B, sum(hs), 128)
    out, o = [], 0
    for h in hs:
        out.append(R[:, o:o + h])
        o += h
    return out


def _sum_tri(terms, low_rows):
    full = sum(t for t in terms if t.shape[1] == SUB)
    half = sum(t for t in terms if t.shape[1] == HSUB)
    lo, hi = full[:, :HSUB], full[:, HSUB:]
    return jnp.concatenate([lo + half, hi] if low_rows else [lo, hi + half], axis=1)


def _hgrn2_fwd(u, lb_logits, comms=None):
    T = u.shape[0]
    G = min(GC, T // CHUNK)
    R, B, N = G * CHUNK, G * HEADS, T // CHUNK

    def body(hq_ref, hf_ref, hv_ref, lg_ref, o_ref, st_ref, S_ref, b_sc, k_sc, v_sc):
        @pl.when(pl.program_id(0) == 0)
        def _():
            S_ref[...] = jnp.zeros_like(S_ref)

        pz = _hg_prep(hq_ref, hf_ref, lg_ref, b_sc, k_sc)
        q3, k3, b3, bl3 = pz["q3"], pz["k3"], pz["b3"], pz["bl3"]
        v3 = _heads_to_batch(hv_ref[...], 128)
        v_sc[...] = v3
        stloc = _bdot(v3, k3 * jnp.exp(bl3 - b3), BTN).reshape(G, HEADS, 128, 128)
        dec = jnp.exp(bl3).reshape(G, HEADS, 1, 128)
        ST = S_ref[...]
        sts = []
        for c in range(G):
            sts.append(ST)
            ST = ST * dec[c] + stloc[c]
        S_ref[...] = ST
        st4 = jnp.stack(sts, axis=0)
        st_ref[...] = st4
        o = _bdot(q3 * jnp.exp(b3), st4.reshape(B, 128, 128), BNT)
        ones = jnp.ones((128, 128), F32)
        rowi = _iota((1, SUB, 128), 1)
        outs = []
        for i in range(CHUNK // SUB):
            r0 = SUB * i
            qi = q3[:, r0:r0 + SUB]
            oi = o[:, r0:r0 + SUB]
            if i > 0:
                r = b_sc[:, r0 - 1:r0, :]
                qe = qi * jnp.exp(b3[:, r0:r0 + SUB] - r)
                ke = k3[:, :r0] * jnp.exp(r - b3[:, :r0])
                oi = oi + _bdot(_bdot(qe, ke, BNT), v3[:, :r0], BNN)
            tiles = _hg_diag_tiles(b_sc, b3, r0, rowi)
            a_b = _lane_sums([qi[:, _lo(s):] * (k_sc[:, r0 + s:r0 + s + 1, :] * tiles[s]) for s in range(SUB)], ones)
            outs.append(oi + _sum_tri([a_b[s] * v_sc[:, r0 + s:r0 + s + 1, :] for s in range(SUB)], False))
        o_ref[...] = _batch_to_heads(jnp.concatenate(outs, axis=1))

    blk = lambda c: pl.BlockSpec((R, HALF), lambda n, c=c: (n, c // 4))
    return _hosted_call(
        body, comms, name="hgrn2_fwd", grid=(N // G,),
        in_specs=[blk(C_HQ), blk(C_HF), blk(C_HV), _full(lb_logits.shape)],
        out_specs=[pl.BlockSpec((R, HALF), lambda n: (n, 0)),
                   pl.BlockSpec((G, HEADS, 128, 128), lambda n: (n, 0, 0, 0))],
        out_shape=[_sds((T, HALF)), _sds((N, HEADS, 128, 128))],
        scratch_shapes=[pltpu.VMEM((HEADS, 128, 128), F32)] + [pltpu.VMEM((B, CHUNK, 128), F32)] * 3,
        args=(u, u, u, lb_logits))


def _hgrn2_bwd(u, lb_logits, do, states, comms=None):
    T = u.shape[0]
    G = min(GC, T // CHUNK)
    R, B, NG = G * CHUNK, G * HEADS, T // (G * CHUNK)

    def body(hq_ref, hf_ref, hv_ref, lg_ref, do_ref, st_ref, dhq_ref, dhf_ref, dhv_ref, dlb_ref,
             dS_ref, b_sc, k_sc, v_sc, q_sc, do_sc):
        @pl.when(pl.program_id(0) == 0)
        def _():
            dS_ref[...] = jnp.zeros_like(dS_ref)
            dlb_ref[...] = jnp.zeros_like(dlb_ref)

        pz = _hg_prep(hq_ref, hf_ref, lg_ref, b_sc, k_sc)
        q3, k3, b3, bl3, lb = pz["q3"], pz["k3"], pz["b3"], pz["bl3"], pz["lb"]
        v3 = _heads_to_batch(hv_ref[...], 128)
        v_sc[...] = v3
        do3 = _heads_to_batch(do_ref[...], 128)
        q_sc[...] = q3
        do_sc[...] = do3
        st3 = st_ref[...].reshape(B, 128, 128)
        eb = jnp.exp(b3)
        ebl = jnp.exp(bl3 - b3)
        qt = q3 * eb
        kl = k3 * ebl
        dstloc = _bdot(do3, qt, BTN).reshape(G, HEADS, 128, 128)
        dec = jnp.exp(bl3).reshape(G, HEADS, 1, 128)
        dST = dS_ref[...]
        dsts = [None] * G
        for c in reversed(range(G)):
            dsts[c] = dST
            dST = dST * dec[c] + dstloc[c]
        dS_ref[...] = dST
        dst3 = jnp.stack(dsts, axis=0).reshape(B, 128, 128)
        dqt = _bdot(do3, st3, BNN)
        dkl = _bdot(v3, dst3, BNN)
        dv_acc = _bdot(kl, dst3, BNT)
        ones = jnp.ones((128, 128), F32)
        rowi = _iota((1, SUB, 128), 1)
        dq_parts, dk_parts, dv_parts = [], [], []
        dk_in = jnp.zeros((B, CHUNK, 128), F32)
        for i_s in range(CHUNK // SUB):
            r0 = SUB * i_s
            qi = q3[:, r0:r0 + SUB]
            doi = do3[:, r0:r0 + SUB]
            dqi = jnp.zeros((B, SUB, 128), F32)
            if i_s > 0:
                r = b_sc[:, r0 - 1:r0, :]
                eq = jnp.exp(b3[:, r0:r0 + SUB] - r)
                ek = jnp.exp(r - b3[:, :r0])
                qe = qi * eq
                ke = k3[:, :r0] * ek
                a_off = _bdot(qe, ke, BNT)
                p_off = _bdot(doi, v3[:, :r0], BNT)
                pad = jnp.zeros((B, CHUNK - r0, 128), F32)
                dv_acc = dv_acc + jnp.concatenate([_bdot(a_off, doi, BTN), pad], axis=1)
                dqi = dqi + _bdot(p_off, ke, BNN) * eq
                dk_in = dk_in + jnp.concatenate([_bdot(p_off, qe, BTN) * ek, pad], axis=1)
            ki, vi = k3[:, r0:r0 + SUB], v3[:, r0:r0 + SUB]
            rng = range(SUB)
            tiles = _hg_diag_tiles(b_sc, b3, r0, rowi)
            tiles_t = _hg_diag_tiles_t(b_sc, b3, r0, rowi)
            do_rows = [do_sc[:, r0 + t:r0 + t + 1, :] for t in rng]
            kts = [k_sc[:, r0 + s:r0 + s + 1, :] * tiles[s] for s in rng]
            qts = [q_sc[:, r0 + t:r0 + t + 1, :] * tiles_t[t] for t in rng]
            ps = [doi[:, _lo(s):] * v_sc[:, r0 + s:r0 + s + 1, :] for s in rng]
            mst = [ki[:, :_lo(t) + HSUB] * qts[t] for t in rng]
            pst = [vi[:, :_lo(t) + HSUB] * do_rows[t] for t in rng]
            sums = _lane_sums(ps + mst + pst, ones)
            p_b, a_t, p_t = sums[:SUB], sums[SUB:2 * SUB], sums[2 * SUB:]
            dq_parts.append(dqi + _sum_tri([p_b[s] * kts[s] for s in rng], False))
            dv_parts.append(_sum_tri([a_t[t] * do_rows[t] for t in rng], True))
            dk_parts.append(_sum_tri([p_t[t] * qts[t] for t in rng], True))
        dq_in = jnp.concatenate(dq_parts, axis=1)
        dk_in = dk_in + jnp.concatenate(dk_parts, axis=1)
        dv_acc = dv_acc + jnp.concatenate(dv_parts, axis=1)
        db = qt * dqt + q3 * dq_in - k3 * dk_in - kl * dkl
        last = jnp.sum(kl * dkl, axis=1, keepdims=True) + jnp.exp(bl3) * jnp.sum(st3 * dst3, axis=1, keepdims=True)
        db = db + jnp.where(_iota((1, CHUNK, 1), 1) == CHUNK - 1, last, 0.0)
        dg = _chunk_cumsum(_batch_to_heads(db), pz["rowmod"], reverse=True)
        dq_tot = _batch_to_heads(dqt * eb + dq_in)
        dk_tot = _batch_to_heads(dkl * ebl + dk_in)
        common = dg / pz["f"] - dk_tot
        dhf_ref[...] = ((1.0 - lb) * pz["sg"] * pz["nsg"] * common).astype(dhf_ref.dtype)
        dl0 = _colsum(pz["nsg"] * common) * lb * (1.0 - lb)
        dlb_ref[0:1, :] += dl0
        dlb_ref[1:2, :] -= dl0
        dhq_ref[...] = (dq_tot * _dsilu(pz["hq"], pz["sq"])).astype(dhq_ref.dtype)
        dhv_ref[...] = _batch_to_heads(dv_acc).astype(dhv_ref.dtype)

    rev = lambda c: pl.BlockSpec((R, HALF), lambda i, c=c: (NG - 1 - i, c // 4))
    rev0 = pl.BlockSpec((R, HALF), lambda i: (NG - 1 - i, 0))
    return _hosted_call(
        body, comms, name="hgrn2_bwd", grid=(NG,),
        in_specs=[rev(C_HQ), rev(C_HF), rev(C_HV), _full(lb_logits.shape), rev0,
                  pl.BlockSpec((G, HEADS, 128, 128), lambda i: (NG - 1 - i, 0, 0, 0))],
        out_specs=[rev0, rev0, rev0, _full((2, HALF))],
        out_shape=[_sds((T, HALF), _MXU)] * 3 + [_sds((2, HALF))],
        scratch_shapes=[pltpu.VMEM((HEADS, 128, 128), F32)] + [pltpu.VMEM((B, CHUNK, 128), F32)] * 5,
        args=(u, u, u, lb_logits, do, states))


def _lanes_to_batch_cols(x, lane):
    G = x.shape[0] // CHUNK
    cols = [_lane_col(x, 4 + h, lane).reshape(G, CHUNK, 1) for h in range(HEADS)]
    return jnp.stack(cols, axis=1).reshape(G * HEADS, CHUNK, 1)


def _row_scalars(rows):
    lane = _iota((1, 128), 1)
    return jnp.stack([_rowsum(jnp.where(lane == 4 + h, r, 0.0)) for r in rows for h in range(HEADS)], axis=0)


def _ml_gates(gates):
    R = gates.shape[0]
    lane = _iota((R, 128), 1)
    rowmod = _iota((R, 128), 0) & (CHUNK - 1)
    lf = jnp.minimum(gates, 0.0) - jnp.log(1.0 + jnp.exp(-jnp.abs(gates)))
    g_all = _chunk_cumsum(lf, rowmod)
    x_all = pltpu.roll(gates, 4, 1) - g_all
    return g_all, x_all, lane, rowmod


def _ml_chunk_rows(g_all, x_all, mprev, g):
    gl = g_all[CHUNK * g + CHUNK - 8:CHUNK * (g + 1)]
    gl = _colsum(jnp.where(_iota((8, 128), 0) == 7, gl, 0.0))
    a = gl + x_all[CHUNK * g:CHUNK * (g + 1)]
    m_new = jnp.maximum(gl + mprev, jnp.max(a, axis=0, keepdims=True))
    return m_new, jnp.exp(gl + mprev - m_new), jnp.exp(a - m_new)


def _ml_batched(q3, k3, v3, g_all, x_all, lane, C3, n3, mprev3):
    G = g_all.shape[0] // CHUNK
    gcol3 = _lanes_to_batch_cols(g_all, lane)
    onehot = jnp.where(_iota((G, 8, 128), 1) + 4 == _iota((G, 8, 128), 2), 1.0, 0.0).astype(F32)
    rows = _bdotx(onehot, x_all.reshape(G, CHUNK, 128), BNT)
    sub = _iota((G, 8, CHUNK), 1)
    row3 = jnp.stack([jnp.sum(jnp.where(sub == h, rows, 0.0), axis=1, keepdims=True) for h in range(HEADS)],
                     axis=1).reshape(G * HEADS, 1, CHUNK)
    causal = _iota((1, CHUNK, CHUNK), 1) >= _iota((1, CHUNK, CHUNK), 2)
    dmat = jnp.where(causal, gcol3 + row3, NEG)
    m_inter = gcol3 + mprev3
    m_t = jnp.maximum(m_inter, jnp.max(dmat, axis=2, keepdims=True))
    wi = jnp.exp(dmat - m_t)
    wn = jnp.exp(m_inter - m_t)
    s3 = _bdot(q3, k3, BNT) * wi
    qc = _bdot(q3, C3, BNN)
    qn = jnp.sum(q3 * n3, axis=2, keepdims=True)
    num = _bdot(s3, v3, BNN) + wn * qc
    den = jnp.sum(s3, axis=2, keepdims=True) + wn * qn
    floor = jnp.exp(-m_t)
    return dict(wi=wi, wn=wn, s=s3, qc=qc, qn=qn, num=num, den=den, floor=floor, nrm=jnp.maximum(jnp.abs(den), floor))


def _mlstm_fwd(qkc, u, comms=None):
    T = u.shape[0]
    G = min(GC, T // CHUNK)
    R = G * CHUNK
    N = T // CHUNK

    def body(qk_ref, v_ref, g_ref, h_ref, cst_ref, nst_ref, mst_ref, C_ref, n_ref, m_ref):
        @pl.when(pl.program_id(0) == 0)
        def _():
            C_ref[...] = jnp.zeros_like(C_ref)
            n_ref[...] = jnp.zeros_like(n_ref)
            m_ref[...] = jnp.zeros_like(m_ref)

        g_all, x_all, lane, _ = _ml_gates(g_ref[...])
        m_row = m_ref[...]
        mprev_rows, wo_rows, ws_parts = [], [], []
        for g in range(G):
            mprev_rows.append(m_row)
            m_row, wo, ws = _ml_chunk_rows(g_all, x_all, m_row, g)
            wo_rows.append(wo)
            ws_parts.append(ws)
        m_ref[...] = m_row
        mst_ref[...] = jnp.stack(mprev_rows, axis=0)
        ws3 = _lanes_to_batch_cols(jnp.concatenate(ws_parts, axis=0), lane)
        wo4 = _row_scalars(wo_rows).reshape(G, HEADS, 1, 1)
        q3 = _heads_to_batch(qk_ref[:, :256] * ML_SCALE, ML_DQK)
        k3 = _heads_to_batch(qk_ref[:, 256:], ML_DQK)
        v3 = _heads_to_batch(v_ref[...], 128)
        kw = k3 * ws3
        cloc = _bdot(kw, v3, BTN).reshape(G, HEADS, ML_DQK, 128)
        nloc = jnp.sum(kw, axis=1, keepdims=True).reshape(G, HEADS, 1, ML_DQK)
        C, nn = C_ref[...], n_ref[...]
        cs, ns = [], []
        for g in range(G):
            cs.append(C)
            ns.append(nn)
            C = wo4[g] * C + cloc[g]
            nn = wo4[g] * nn + nloc[g]
        C_ref[...] = C
        n_ref[...] = nn
        c4, n4 = jnp.stack(cs, axis=0), jnp.stack(ns, axis=0)
        cst_ref[...] = c4
        nst_ref[...] = n4
        r = _ml_batched(q3, k3, v3, g_all, x_all, lane, c4.reshape(G * HEADS, ML_DQK, 128),
                        n4.reshape(G * HEADS, 1, ML_DQK), _row_scalars(mprev_rows))
        h_ref[...] = _batch_to_heads(r["num"] / r["nrm"])

    return _hosted_call(
        body, comms, name="mlstm_fwd", grid=(N // G,),
        in_specs=[pl.BlockSpec((R, HALF), lambda n: (n, 0)), pl.BlockSpec((R, HALF), lambda n: (n, C_MV // 4)),
                  pl.BlockSpec((R, 128), lambda n: (n, C_GATES))],
        out_specs=[pl.BlockSpec((R, HALF), lambda n: (n, 0)),
                   pl.BlockSpec((G, HEADS, ML_DQK, 128), lambda n: (n, 0, 0, 0)),
                   pl.BlockSpec((G, HEADS, 1, ML_DQK), lambda n: (n, 0, 0, 0)),
                   pl.BlockSpec((G, 1, 128), lambda n: (n, 0, 0))],
        out_shape=[_sds((T, HALF)), _sds((N, HEADS, ML_DQK, 128)), _sds((N, HEADS, 1, ML_DQK)), _sds((N, 1, 128))],
        scratch_shapes=[pltpu.VMEM((HEADS, ML_DQK, 128), F32), pltpu.VMEM((HEADS, 1, ML_DQK), F32),
                        pltpu.VMEM((1, 128), F32)],
        args=(qkc, u, u))


def _mlstm_bwd(qkc, u, dh, cst, nst, mst):
    T = u.shape[0]
    G = min(GC, T // CHUNK)
    R = G * CHUNK
    NG = T // R

    def body(qk_ref, v_ref, g_ref, dh_ref, cst_ref, nst_ref, mst_ref, dqk_ref, dv_ref, dgt_ref, dC_ref, dn_ref):
        @pl.when(pl.program_id(0) == 0)
        def _():
            dC_ref[...] = jnp.zeros_like(dC_ref)
            dn_ref[...] = jnp.zeros_like(dn_ref)

        B = G * HEADS
        gates = g_ref[...]
        g_all, x_all, lane, rowmod = _ml_gates(gates)
        mprev_rows = [mst_ref[g] for g in range(G)]
        wo_rows, ws_parts = [], []
        for g in range(G):
            _, wo, ws = _ml_chunk_rows(g_all, x_all, mprev_rows[g], g)
            wo_rows.append(wo)
            ws_parts.append(ws)
        ws3 = _lanes_to_batch_cols(jnp.concatenate(ws_parts, axis=0), lane)
        wo3 = _row_scalars(wo_rows)
        wo4 = wo3.reshape(G, HEADS, 1, 1)
        q3 = _heads_to_batch(qk_ref[:, :256] * ML_SCALE, ML_DQK)
        k3 = _heads_to_batch(qk_ref[:, 256:], ML_DQK)
        v3 = _heads_to_batch(v_ref[...], 128)
        dh3 = _heads_to_batch(dh_ref[...], 128)
        C3 = cst_ref[...].reshape(B, ML_DQK, 128)
        n3 = nst_ref[...].reshape(B, 1, ML_DQK)
        r = _ml_batched(q3, k3, v3, g_all, x_all, lane, C3, n3, _row_scalars(mprev_rows))
        wn, s3 = r["wn"], r["s"]
        inv = 1.0 / r["nrm"]
        dnum = dh3 * inv
        dnrm = -jnp.sum(dh3 * (r["num"] * inv), axis=2, keepdims=True) * inv
        dden = jnp.where(jnp.abs(r["den"]) > r["floor"], dnrm * jnp.sign(r["den"]), 0.0)
        ds = _bdot(dnum, v3, BNT) + dden
        dqk = ds * r["wi"]
        dd = ds * s3
        qw = q3 * wn
        dcloc = _bdot(qw, dnum, BTN).reshape(G, HEADS, ML_DQK, 128)
        dnloc = jnp.sum(qw * dden, axis=1, keepdims=True).reshape(G, HEADS, 1, ML_DQK)
        dC, dn = dC_ref[...], dn_ref[...]
        dcs, dns = [None] * G, [None] * G
        for g in reversed(range(G)):
            dcs[g], dns[g] = dC, dn
            dC = wo4[g] * dC + dcloc[g]
            dn = wo4[g] * dn + dnloc[g]
        dC_ref[...] = dC
        dn_ref[...] = dn
        dC3 = jnp.stack(dcs, axis=0).reshape(B, ML_DQK, 128)
        dn3 = jnp.stack(dns, axis=0).reshape(B, 1, ML_DQK)
        dk_st = ws3 * (_bdot(v3, dC3, BNT) + dn3)
        dq = _bdot(dqk, k3, BNN) + wn * (_bdot(dnum, C3, BNT) + dden * n3)
        dk = _bdot(dqk, q3, BTN) + dk_st
        dv = _bdot(s3, dnum, BTN) + ws3 * _bdot(k3, dC3, BNN)
        dv_ref[...] = _batch_to_heads(dv).astype(dv_ref.dtype)
        dqk_ref[...] = jnp.concatenate([_batch_to_heads(dq * ML_SCALE), _batch_to_heads(dk)], axis=1)
        e_col = wn * (jnp.sum(dnum * r["qc"], axis=2, keepdims=True) + dden * r["qn"])
        c_col = jnp.sum(k3 * dk_st, axis=2, keepdims=True)
        z = wo3 * (jnp.sum(dC3 * C3, axis=(1, 2), keepdims=True) + jnp.sum(dn3 * n3, axis=(1, 2), keepdims=True))
        dd_cols = _bdotx(dd, jnp.ones((B, CHUNK, 128), F32), BTN)[:, :, 0:1]
        last = _iota((1, CHUNK, 1), 1) == CHUNK - 1
        dg3 = jnp.sum(dd, axis=2, keepdims=True) - dd_cols + e_col - c_col
        dg3 = dg3 + jnp.where(last, jnp.sum(c_col, axis=1, keepdims=True) + z, 0.0)
        di3 = dd_cols + c_col

        def to_lanes(x3, first):
            x4 = x3.reshape(G, HEADS, CHUNK, 1)
            return sum(jnp.where(lane == first + h, x4[:, h].reshape(R, 1), 0.0) for h in range(HEADS))

        dlf = _chunk_cumsum(to_lanes(dg3, 4), rowmod, reverse=True)
        dgt_ref[...] = (to_lanes(di3, 0) + dlf * _sig(-gates)).astype(dgt_ref.dtype)

    rev = lambda w, c: pl.BlockSpec((R, w), lambda i, c=c: (NG - 1 - i, c))
    st = lambda *s: pl.BlockSpec((G,) + s, lambda i: (NG - 1 - i,) + (0,) * len(s))
    return pl.pallas_call(
        body, name="mlstm_bwd", grid=(NG,),
        in_specs=[rev(HALF, 0), rev(HALF, C_MV // 4), rev(128, C_GATES), rev(HALF, 0),
                  st(HEADS, ML_DQK, 128), st(HEADS, 1, ML_DQK), st(1, 128)],
        out_specs=[rev(HALF, 0), rev(HALF, 0), rev(128, 0)],
        out_shape=[_sds((T, HALF)), _sds((T, HALF), _MXU), _sds((T, 128), _MXU)],
        scratch_shapes=[pltpu.VMEM((HEADS, ML_DQK, 128), F32), pltpu.VMEM((HEADS, 1, ML_DQK), F32)],
        compiler_params=_cp("arbitrary"))(qkc, u, u, dh, cst, nst, mst)


def _head_norm(o):
    rs_parts, r_parts = [], []
    for h in range(HEADS):
        oh = o[:, 128 * h:128 * (h + 1)]
        rs = lax.rsqrt(jnp.mean(oh * oh, axis=-1, keepdims=True) + RMS_EPS)
        rs_parts.append(rs)
        r_parts.append(oh * rs)
    return jnp.concatenate(r_parts, axis=1), rs_parts


def _out_proj_ln(x, u, o_hg, h_ml, g_hg, g_ml, w_out, ln_g, ln_b, tm):
    T = x.shape[0]

    def body(x_ref, hgate_ref, mo_ref, ohg_ref, hml_ref, ghg_ref, gml_ref, w_ref, g_ref, b_ref,
             m_ref, z_ref, x1_ref):
        hgate = hgate_ref[...]
        a = _head_norm(ohg_ref[...])[0] * ghg_ref[...] * (hgate * _sig(hgate))
        b = _head_norm(hml_ref[...])[0] * gml_ref[...] * _sig(mo_ref[...])
        m = jnp.concatenate([a, b], axis=1)
        m_ref[...] = m.astype(m_ref.dtype)
        z = ALPHA * x_ref[...] + _dot(m, w_ref[...])
        z_ref[...] = z
        x1_ref[...] = _ln_fwd(z, g_ref[...], b_ref[...])[0]

    return pl.pallas_call(
        body, name="out_proj_ln1", grid=(T // tm,),
        in_specs=[_row(tm, D_MODEL), _row(tm, HALF, C_HGATE // 4), _row(tm, HALF, C_MO // 4),
                  _row(tm, HALF), _row(tm, HALF), _full(g_hg.shape), _full(g_ml.shape),
                  _full(w_out.shape), _full(ln_g.shape), _full(ln_b.shape)],
        out_specs=[_row(tm, D_MODEL)] * 3,
        out_shape=[_sds((T, D_MODEL), _MXU), _sds((T, D_MODEL)), _sds((T, D_MODEL))],
        compiler_params=_cp("parallel"))(x, u, u, o_hg, h_ml, g_hg, g_ml, w_out, ln_g, ln_b)


def _ffn_ln(x1, wg, wu, wd, ln_g, ln_b, tm):
    T = x1.shape[0]

    def body(x_ref, wg_ref, wu_ref, wd_ref, g_ref, b_ref, z_ref, x2_ref, a_ref, bb_ref, h_ref):
        x = x_ref[...]
        a = _dot(x, wg_ref[...])
        bb = _dot(x, wu_ref[...])
        hh = a * _sig(a) * bb
        a_ref[...] = a.astype(a_ref.dtype)
        bb_ref[...] = bb.astype(bb_ref.dtype)
        h_ref[...] = hh.astype(h_ref.dtype)
        z = ALPHA * x + _dot(hh, wd_ref[...])
        z_ref[...] = z
        x2_ref[...] = _ln_fwd(z, g_ref[...], b_ref[...])[0]

    return pl.pallas_call(
        body, name="ffn_ln2", grid=(T // tm,),
        in_specs=[_row(tm, D_MODEL), _full(wg.shape), _full(wu.shape), _full(wd.shape),
                  _full(ln_g.shape), _full(ln_b.shape)],
        out_specs=[_row(tm, D_MODEL)] * 2 + [_row(tm, D_FF)] * 3,
        out_shape=[_sds((T, D_MODEL))] * 2 + [_sds((T, D_FF), _MXU)] * 3,
        compiler_params=_cp("parallel"))(x1, wg, wu, wd, ln_g, ln_b)


def _ple_loss_ln2_bwd(x2, z2, p, tgt, wpg, bpg, wpp, ln_g, ln_b, tm):
    T = x2.shape[0]

    def body(x2_ref, z_ref, p_ref, t_ref, wpg_ref, bpg_ref, wpp_ref, g_ref, b_ref,
             de_ref, dgp_ref, dz_ref, loss_ref, dbpg_ref, dg_ref, db_ref):
        @pl.when(pl.program_id(0) == 0)
        def _():
            for r in (loss_ref, dbpg_ref, dg_ref, db_ref):
                r[...] = jnp.zeros_like(r)

        x2 = x2_ref[...]
        gate = _sig(_dot(x2, wpg_ref[...]) + bpg_ref[...])
        e = _dot(p_ref[...], wpp_ref[...])
        err = x2 + gate * e - t_ref[...]
        loss_ref[...] += _colsum(err * err)
        dy = err * (1.0 / D_MODEL)
        de_ref[...] = (dy * gate).astype(de_ref.dtype)
        dgp = dy * e * gate * (1.0 - gate)
        dgp_ref[...] = dgp.astype(dgp_ref.dtype)
        dbpg_ref[...] += _colsum(dgp)
        dx2 = dy + _dot(dgp, wpg_ref[...], NT)
        _, xhat, rstd = _ln_fwd(z_ref[...], g_ref[...], b_ref[...])
        dg_ref[...] += _colsum(dx2 * xhat)
        db_ref[...] += _colsum(dx2)
        dz_ref[...] = _ln_bwd(dx2, xhat, rstd, g_ref[...])

    vec = _full((1, D_MODEL))
    return pl.pallas_call(
        body, name="ple_loss_ln2_bwd", grid=(T // tm,),
        in_specs=[_row(tm, D_MODEL), _row(tm, D_MODEL), _row(tm, PLE_DIM), _row(tm, D_MODEL),
                  _full(wpg.shape), vec, _full(wpp.shape), vec, vec],
        out_specs=[_row(tm, D_MODEL)] * 3 + [vec] * 4,
        out_shape=[_sds((T, D_MODEL), _MXU)] * 2 + [_sds((T, D_MODEL))] + [_sds((1, D_MODEL))] * 4,
        compiler_params=_cp("arbitrary"))(x2, z2, p, tgt, wpg, bpg, wpp, ln_g, ln_b)


def _ffn_bwd_ln1_bwd(a_pre, b_pre, z1, dz2, wg, wu, wd, ln_g, ln_b, tm, comms=None):
    T = z1.shape[0]

    def body(a_ref, bb_ref, z_ref, dz2_ref, wg_ref, wu_ref, wd_ref, g_ref, b_ref,
             da_ref, dbb_ref, dz1_ref, dg_ref, db_ref):
        @pl.when(pl.program_id(0) == 0)
        def _():
            dg_ref[...] = jnp.zeros_like(dg_ref)
            db_ref[...] = jnp.zeros_like(db_ref)

        dz2 = dz2_ref[...]
        a = a_ref[...].astype(F32)
        bb = bb_ref[...].astype(F32)
        sa = _sig(a)
        act = a * sa
        dh = _dot(dz2, wd_ref[...], NT)
        da = (dh * bb * _dsilu(a, sa)).astype(da_ref.dtype)
        dbb = (dh * act).astype(dbb_ref.dtype)
        da_ref[...] = da
        dbb_ref[...] = dbb
        dx1 = ALPHA * dz2 + _dot(da, wg_ref[...], NT) + _dot(dbb, wu_ref[...], NT)
        _, xhat, rstd = _ln_fwd(z_ref[...], g_ref[...], b_ref[...])
        dg_ref[...] += _colsum(dx1 * xhat)
        db_ref[...] += _colsum(dx1)
        dz1_ref[...] = _ln_bwd(dx1, xhat, rstd, g_ref[...])

    vec = _full((1, D_MODEL))
    return _hosted_call(
        body, comms, name="ffn_bwd_ln1_bwd", grid=(T // tm,),
        in_specs=[_row(tm, D_FF)] * 2 + [_row(tm, D_MODEL)] * 2 + [_full(wg.shape), _full(wu.shape), _full(wd.shape), vec, vec],
        out_specs=[_row(tm, D_FF)] * 2 + [_row(tm, D_MODEL), vec, vec],
        out_shape=[_sds((T, D_FF), _MXU)] * 2 + [_sds((T, D_MODEL)), _sds((1, D_MODEL)), _sds((1, D_MODEL))],
        scratch_shapes=[], args=(a_pre, b_pre, z1, dz2, wg, wu, wd, ln_g, ln_b))


def _out_proj_bwd(dz1, u, o_hg, h_ml, g_hg, g_ml, w_out, tm):
    T = dz1.shape[0]

    def body(dz_ref, hgate_ref, mo_ref, ohg_ref, hml_ref, ghg_ref, gml_ref, w_ref,
             dohg_ref, dhml_ref, dhgate_ref, dmo_ref, dghg_ref, dgml_ref):
        @pl.when(pl.program_id(0) == 0)
        def _():
            dghg_ref[...] = jnp.zeros_like(dghg_ref)
            dgml_ref[...] = jnp.zeros_like(dgml_ref)

        dm = _dot(dz_ref[...], w_ref[...], NT)

        def half(dmh, o, gvec, gate_val, dgate_fac, do_ref, dgate_ref, dgvec_ref):
            r, rs = _head_norm(o)
            dgate_ref[...] = (dmh * r * gvec * dgate_fac).astype(dgate_ref.dtype)
            dn = dmh * gate_val
            dgvec_ref[...] += _colsum(dn * r)
            dr = dn * gvec
            parts = []
            for h in range(HEADS):
                sl = slice(128 * h, 128 * (h + 1))
                parts.append(rs[h] * (dr[:, sl] - r[:, sl] * jnp.mean(dr[:, sl] * r[:, sl], axis=-1, keepdims=True)))
            do_ref[...] = jnp.concatenate(parts, axis=1)

        hg = hgate_ref[...]
        shg = _sig(hg)
        half(dm[:, :HALF], ohg_ref[...], ghg_ref[...], hg * shg, _dsilu(hg, shg), dohg_ref, dhgate_ref, dghg_ref)
        smo = _sig(mo_ref[...])
        half(dm[:, HALF:], hml_ref[...], gml_ref[...], smo, smo * (1.0 - smo), dhml_ref, dmo_ref, dgml_ref)

    vec = _full((1, HALF))
    return pl.pallas_call(
        body, name="out_proj_bwd", grid=(T // tm,),
        in_specs=[_row(tm, D_MODEL), _row(tm, HALF, C_HGATE // 4), _row(tm, HALF, C_MO // 4),
                  _row(tm, HALF), _row(tm, HALF), vec, vec, _full(w_out.shape)],
        out_specs=[_row(tm, HALF)] * 4 + [vec, vec],
        out_shape=[_sds((T, HALF))] * 2 + [_sds((T, HALF), _MXU)] * 2 + [_sds((1, HALF))] * 2,
        compiler_params=_cp("arbitrary"))(dz1, u, u, o_hg, h_ml, g_hg, g_ml, w_out)


def _conv_bwd(u, pre, dqkc, cw, tm):
    T = u.shape[0]
    hb = tm // 8
    nb = T // 8

    def body(x_ref, xh_ref, pre_ref, preh_ref, d_ref, dh_ref, w_ref, dx_ref, dw_ref, db_ref):
        i = pl.program_id(0)

        @pl.when(i == 0)
        def _():
            dw_ref[...] = jnp.zeros_like(dw_ref)
            db_ref[...] = jnp.zeros_like(db_ref)

        def dpre_of(pre, d):
            return d * _dsilu(pre, _sig(pre))

        rowi = _iota((8, HALF), 0)
        dpre = dpre_of(pre_ref[...], d_ref[...])
        dpre_next = jnp.where(i < pl.num_programs(0) - 1, dpre_of(preh_ref[...], dh_ref[...]), 0.0)
        x = x_ref[...]
        xprev = jnp.where(i > 0, xh_ref[...], 0.0)
        dx = dpre * w_ref[3:4, :]
        db_ref[...] += _colsum(dpre)
        dws = [None] * 4
        dws[3] = _colsum(dpre * x)
        for j in (1, 2, 3):
            dx = dx + _shift_rows_up(dpre, dpre_next, j, rowi) * w_ref[3 - j:4 - j, :]
            dws[3 - j] = _colsum(dpre * _shift_rows(x, xprev, j, rowi))
        dx_ref[...] = dx.astype(dx_ref.dtype)
        dw_ref[...] += jnp.concatenate(dws, axis=0)

    cur = lambda blk: pl.BlockSpec((tm, HALF), lambda i, blk=blk: (i, blk))
    nxt = pl.BlockSpec((8, HALF), lambda i: (jnp.minimum((i + 1) * hb, nb - 1), 0))
    return pl.pallas_call(
        body, name="conv_bwd", grid=(T // tm,),
        in_specs=[cur(C_MQK // 4), pl.BlockSpec((8, HALF), lambda i: (jnp.maximum(i * hb - 1, 0), C_MQK // 4)),
                  cur(0), nxt, cur(0), nxt, _full(cw.shape)],
        out_specs=[cur(0), _full((4, HALF)), _full((1, HALF))],
        out_shape=[_sds((T, HALF), _MXU), _sds((4, HALF)), _sds((1, HALF))],
        compiler_params=_cp("arbitrary"))(u, u, pre, pre, dqkc, dqkc, cw)


def _du_specs(rows):
    return [pl.BlockSpec((rows, w), lambda i: (i, 0)) for w in DU_WIDTHS]


def _in_proj_bwd(dz1, du_parts, w, tm, comms=None):
    T = dz1.shape[0]

    def body(dz_ref, *refs):
        du = jnp.concatenate([r[...] for r in refs[:8]], axis=1)
        refs[9][...] = ALPHA * dz_ref[...] + _dot(du, refs[8][...], NT)

    (dx,), got = _hosted_call(
        body, comms, name="in_proj_bwd", grid=(T // tm,),
        in_specs=[_row(tm, D_MODEL)] + _du_specs(tm) + [_full(w.shape)],
        out_specs=[_row(tm, D_MODEL)], out_shape=[_sds((T, D_MODEL))], scratch_shapes=[], args=(dz1, *du_parts, w))
    return dx, got


def _wgrad(a, b, name, tm, tn, tk):
    T, M = a.shape
    N = b.shape[1]
    tm, tn, tk = min(tm, M), min(tn, N), min(tk, T)
    nk = T // tk

    def body(a_ref, b_ref, o_ref, acc_ref):
        kk = pl.program_id(2)

        @pl.when(kk == 0)
        def _():
            acc_ref[...] = jnp.zeros_like(acc_ref)

        acc_ref[...] += _dot(a_ref[...], b_ref[...], TN)

        @pl.when(kk == nk - 1)
        def _():
            o_ref[...] = acc_ref[...].astype(o_ref.dtype)

    return pl.pallas_call(
        body, name=name, grid=(M // tm, N // tn, nk),
        in_specs=[pl.BlockSpec((tk, tm), lambda i, j, kk: (kk, i)), pl.BlockSpec((tk, tn), lambda i, j, kk: (kk, j))],
        out_specs=pl.BlockSpec((tm, tn), lambda i, j, kk: (i, j)), out_shape=_sds((M, N), _MXU),
        scratch_shapes=[pltpu.VMEM((tm, tn), F32)],
        compiler_params=_cp("parallel", "parallel", "arbitrary"))(a, b)


def _wgrad_w_in(x, du_parts, tk):
    T = x.shape[0]
    tk = min(tk, T)
    nk = T // tk

    def body(a_ref, *refs):
        o_ref, cs_ref, acc_ref = refs[8:]
        kk = pl.program_id(0)

        @pl.when(kk == 0)
        def _():
            acc_ref[...] = jnp.zeros_like(acc_ref)
            cs_ref[...] = jnp.zeros_like(cs_ref)

        du = jnp.concatenate([r[...] for r in refs[:8]], axis=1)
        acc_ref[...] += _dot(a_ref[...], du, TN)
        cs_ref[...] += _colsum(du.astype(F32))

        @pl.when(kk == nk - 1)
        def _():
            o_ref[...] = acc_ref[...].astype(o_ref.dtype)

    return pl.pallas_call(
        body, name="wgrad_w_in", grid=(nk,),
        in_specs=[pl.BlockSpec((tk, D_MODEL), lambda kk: (kk, 0))] + _du_specs(tk),
        out_specs=[_full((D_MODEL, PROJ_WP)), _full((1, PROJ_WP))],
        out_shape=[_sds((D_MODEL, PROJ_WP), _MXU), _sds((1, PROJ_WP))],
        scratch_shapes=[pltpu.VMEM((D_MODEL, PROJ_WP), F32)], compiler_params=_cp("arbitrary"))(x, *du_parts)


W_IN_S, FF_S, OUT_S, PP_S = PROJ_W // N_DEV, D_FF // N_DEV, D_MODEL // N_DEV, D_MODEL // N_DEV
LATE = ("w_ffn_gate", "w_ffn_up", "w_out", "w_ffn_down", "ple_w_gate", "ple_w_proj")
BIG = ("w_in",) + LATE


def _split_cols(a, n):
    return a.reshape(a.shape[0], N_DEV, n).transpose(1, 0, 2)


def _join_cols(a):
    return a.transpose(1, 0, 2).reshape(a.shape[1], -1)


def _step(x, p, tgt, w_in, b_in, lb_logits, conv_w, conv_b, g_hg, g_ml, ln1_g, ln1_b, ln2_g, ln2_b, bpg, late,
          distributed):
    T = x.shape[0]
    tm, tf = min(ROWS, T), min(ROWS_FFN, T)
    gather = lambda *names: [_GatherTwoLevel([late[n] for n in names])] if distributed else None
    scatter = lambda *arrs: [_Comm("scatter", list(arrs))] if distributed else None
    rows = lambda a, n: a.reshape(N_DEV, n, D_MODEL)
    u, got1 = _in_proj(x, w_in, b_in, tm, gather("w_out", "ple_w_gate", "ple_w_proj"))
    pre, qkc = _conv_fwd(u, conv_w, conv_b, tm)
    (o_hg, hg_states), got2 = _hgrn2_fwd(u, lb_logits, gather("w_ffn_gate", "w_ffn_up"))
    (h_ml, cst, nst, mst), got3 = _mlstm_fwd(qkc, u, gather("w_ffn_down"))
    if distributed:
        w_out, wpg, wpp = got1[0][0].reshape(D_MODEL, D_MODEL), got1[0][1].reshape(D_MODEL, D_MODEL), _join_cols(got1[0][2])
        wg, wu, wd = _join_cols(got2[0][0]), _join_cols(got2[0][1]), got3[0][0].reshape(D_FF, D_MODEL)
    else:
        w_out, wg, wu, wd, wpg, wpp = (late[n] for n in ("w_out", "w_ffn_gate", "w_ffn_up", "w_ffn_down", "ple_w_gate", "ple_w_proj"))
    m_in, z1, x1 = _out_proj_ln(x, u, o_hg, h_ml, g_hg, g_ml, w_out, ln1_g, ln1_b, tm)
    z2, x2, a_pre, b_pre, hh = _ffn_ln(x1, wg, wu, wd, ln2_g, ln2_b, tf)
    de, dgp, dz2, loss_vec, d_bpg, d_ln2g, d_ln2b = _ple_loss_ln2_bwd(x2, z2, p, tgt, wpg, bpg, wpp, ln2_g, ln2_b, tm)
    big = dict(ple_w_gate=_wgrad(x2, dgp, "wgrad_ple_gate", 512, D_MODEL, 1024),
               ple_w_proj=_wgrad(p, de, "wgrad_ple_proj", 512, D_MODEL, 1024))
    (da, dbb, dz1, d_ln1g, d_ln1b), r1 = _ffn_bwd_ln1_bwd(
        a_pre, b_pre, z1, dz2, wg, wu, wd, ln1_g, ln1_b, tf, scatter(rows(big["ple_w_gate"], OUT_S), _split_cols(big["ple_w_proj"], PP_S)))
    big.update(
        w_ffn_gate=_wgrad(x1, da, "wgrad_ffn_gate", 512, D_FF, 1024),
        w_ffn_up=_wgrad(x1, dbb, "wgrad_ffn_up", 512, D_FF, 1024),
        w_ffn_down=_wgrad(hh, dz2, "wgrad_ffn_down", D_FF, D_MODEL, 1024),
        w_out=_wgrad(m_in, dz1, "wgrad_w_out", 512, D_MODEL, 1024))
    d_ohg, d_hml, d_hgate, d_mo, d_ghg, d_gml = _out_proj_bwd(dz1, u, o_hg, h_ml, g_hg, g_ml, w_out, tm)
    (d_hq, d_hf, d_hv, d_lb), r2 = _hgrn2_bwd(
        u, lb_logits, d_ohg, hg_states,
        scatter(_split_cols(big["w_ffn_gate"], FF_S), _split_cols(big["w_ffn_up"], FF_S), rows(big["w_ffn_down"], FF_S),
                rows(big["w_out"], OUT_S)))
    d_qkc, d_mv, d_gates = _mlstm_bwd(qkc, u, d_hml, cst, nst, mst)
    d_mqk, d_convw, d_convb = _conv_bwd(u, pre, d_qkc, conv_w, tm)
    du_parts = [d_hq, d_hf, d_hv, d_hgate, d_mqk, d_mv, d_mo, d_gates]
    big["w_in"], d_bin = _wgrad_w_in(x, du_parts, 256)
    small = dict(b_in=d_bin, hg_lb_logits=d_lb, ml_conv_w=d_convw, ml_conv_b=d_convb, hg_norm_g=d_ghg, ml_norm_g=d_gml,
                 ln1_g=d_ln1g, ln1_b=d_ln1b, ln2_g=d_ln2g, ln2_b=d_ln2b, ple_b_gate=d_bpg)
    last = [_Comm("scatter", [_split_cols(big["w_in"][:, :PROJ_W], W_IN_S)]),
            _Comm("gather", [_pack_small(small, loss_vec)])] if distributed else None
    dx, r3 = _in_proj_bwd(dz1, du_parts, w_in, tm, last)
    gathered_small = None
    if distributed:
        big = dict(ple_w_gate=r1[0][0], ple_w_proj=r1[0][1], w_ffn_gate=r2[0][0], w_ffn_up=r2[0][1],
                   w_ffn_down=r2[0][2], w_out=r2[0][3], w_in=r3[0][0])
        gathered_small = r3[1][0]
    return loss_vec, dx, big, small, gathered_small


SMALL = dict(b_in=(8, 32, PROJ_WP), hg_lb_logits=(40, 8, 1024), ml_conv_w=(48, 16, 2048), ml_conv_b=(64, 8, 512),
             hg_norm_g=(72, 8, 512), ml_norm_g=(80, 8, 512), ln1_g=(88, 8, 1024), ln1_b=(96, 8, 1024),
             ln2_g=(104, 8, 1024), ln2_b=(112, 8, 1024), ple_b_gate=(120, 8, 1024))
SM_ROWS = 128


def _padc(a, n):
    return jnp.pad(a, [(0, 0)] * (a.ndim - 1) + [(0, n - a.shape[-1])])


def _pack_small(d, loss_vec=None):
    first = jnp.zeros((8, 128), F32) if loss_vec is None else loss_vec.reshape(8, 128)
    parts = [first]
    for name, (_, rows, n) in SMALL.items():
        parts.append(jnp.pad(d[name].reshape(-1), (0, rows * 128 - n)).reshape(rows, 128))
    return jnp.concatenate(parts, axis=0)


def _unpack_small(slab, shapes):
    return {name: slab[r0:r0 + rows].reshape(-1)[:n].reshape(shapes[name]) for name, (r0, rows, n) in SMALL.items()}


def _adamw(w, g, m, v):
    m = B1 * m + (1.0 - B1) * g
    v = B2 * v + (1.0 - B2) * jnp.square(g)
    m_hat = m / (1.0 - B1 ** STEP)
    v_hat = v / (1.0 - B2 ** STEP)
    return -LR * (m_hat / (jnp.sqrt(v_hat) + EPS) + WD * w), m, v


def _sum_slabs(ref):
    g = ref[0].astype(F32)
    for j in range(1, N_DEV):
        g = g + ref[j].astype(F32)
    return g


def _adamw_matrix(rb, w, m, v, name):
    R, C = w.shape
    tr = 256 if R % 256 == 0 else R

    def body(rb_ref, w_ref, m_ref, v_ref, g_ref, d_ref, m2_ref, v2_ref):
        g = _sum_slabs(rb_ref)
        g_ref[...] = g
        d_ref[...], m2_ref[...], v2_ref[...] = _adamw(w_ref[...], g, m_ref[...], v_ref[...])

    blk = pl.BlockSpec((tr, C), lambda i: (i, 0))
    return pl.pallas_call(
        body, name=name, grid=(R // tr,),
        in_specs=[pl.BlockSpec((N_DEV, tr, C), lambda i: (0, i, 0)), blk, blk, blk],
        out_specs=[blk] * 4, out_shape=[_sds((R, C))] * 4, compiler_params=_cp("parallel"))(rb, w, m, v)


def _adamw_small(sg, w, m, v):
    def body(sg_ref, w_ref, m_ref, v_ref, loss_ref, g_ref, d_ref, m2_ref, v2_ref):
        g = _sum_slabs(sg_ref)
        loss_ref[...] = (0.5 / D_MODEL) * jnp.sum(g[0:8], keepdims=True)
        g_ref[...] = g
        d_ref[...], m2_ref[...], v2_ref[...] = _adamw(w_ref[...], g, m_ref[...], v_ref[...])

    return pl.pallas_call(
        body, name="adamw_small", out_shape=[_sds((1, 1))] + [_sds((SM_ROWS, 128))] * 4)(sg, w, m, v)


WEIGHTS = ("w_in", "b_in", "hg_lb_logits", "ml_conv_w", "ml_conv_b", "hg_norm_g", "ml_norm_g", "w_out", "ln1_g", "ln1_b",
           "w_ffn_gate", "w_ffn_up", "w_ffn_down", "ln2_g", "ln2_b", "ple_w_proj", "ple_w_gate", "ple_b_gate")
CONV_S = HALF // N_DEV


def kernel(x, p, w_in, b_in, hg_lb_logits, ml_conv_w, ml_conv_b, hg_norm_g, ml_norm_g, w_out, ln1_g, ln1_b, w_ffn_gate, w_ffn_up, w_ffn_down, ln2_g, ln2_b, ple_w_proj, ple_w_gate, ple_b_gate, loss_target, m_w_in, m_b_in, m_hg_lb_logits, m_ml_conv_w, m_ml_conv_b, m_hg_norm_g, m_ml_norm_g, m_w_out, m_ln1_g, m_ln1_b, m_w_ffn_gate, m_w_ffn_up, m_w_ffn_down, m_ln2_g, m_ln2_b, m_ple_w_proj, m_ple_w_gate, m_ple_b_gate, v_w_in, v_b_in, v_hg_lb_logits, v_ml_conv_w, v_ml_conv_b, v_hg_norm_g, v_ml_norm_g, v_w_out, v_ln1_g, v_ln1_b, v_w_ffn_gate, v_w_ffn_up, v_w_ffn_down, v_ln2_g, v_ln2_b, v_ple_w_proj, v_ple_w_gate, v_ple_b_gate):
    args = locals()
    me = 4 * lax.axis_index("x") + 2 * lax.axis_index("y") + lax.axis_index("c")
    shapes = {n: args[n].shape for n in WEIGHTS}
    drop = lambda n, a: a[0] if n in BIG or n == "ml_conv_w" else a
    W = {n: drop(n, args[n]) for n in WEIGHTS}
    M = {n: drop(n, args["m_" + n]) for n in WEIGHTS}
    V = {n: drop(n, args["v_" + n]) for n in WEIGHTS}

    g_in, g_conv = _gather_two_level(
        [W["w_in"].astype(_MXU), jnp.pad(W["ml_conv_w"], ((0, 4), (0, 128 - CONV_S)))], "gather_w_in")
    w_in_full = _padc(_join_cols(g_in), PROJ_WP)
    conv_full = _join_cols(g_conv[:, :4, :CONV_S])

    _, dx, big, _, sg = _step(
        x[0], p[0, 0], loss_target[0], w_in_full, _padc(b_in, PROJ_WP), hg_lb_logits, conv_full, ml_conv_b,
        hg_norm_g, ml_norm_g, ln1_g, ln1_b, ln2_g, ln2_b, ple_b_gate, {n: W[n].astype(_MXU) for n in LATE}, True)

    upd = {n: _adamw_matrix(big[n], W[n], M[n], V[n], "adamw_" + n) for n in BIG}
    place = lambda d: {**d, "b_in": _padc(d["b_in"], PROJ_WP),
                       "ml_conv_w": lax.dynamic_update_slice(jnp.zeros((4, HALF), F32), d["ml_conv_w"], (0, me * CONV_S))}
    loss, *small_upd = _adamw_small(sg, _pack_small(place(W)), _pack_small(place(M)), _pack_small(place(V)))

    outs = []
    sm_shapes = {**{n: shapes[n] for n in SMALL}, "b_in": (1, PROJ_WP), "ml_conv_w": (4, HALF)}
    for kind in range(4):
        smalls = _unpack_small(small_upd[kind], sm_shapes)
        smalls["b_in"] = smalls["b_in"][:, :PROJ_W]
        smalls["ml_conv_w"] = lax.dynamic_slice(smalls["ml_conv_w"], (0, me * CONV_S), (4, CONV_S))
        for n in WEIGHTS:
            outs.append((upd[n][kind] if n in BIG else smalls[n]).reshape(shapes[n]))
    return (loss.reshape(()), dx.reshape(x.shape), *outs)
```

```python
import jax
import jax.numpy as jnp
from jax import lax
from jax.experimental import pallas as pl
from jax.experimental.pallas import tpu as pltpu

F32 = jnp.float32
_MXU = jnp.bfloat16

D_MODEL = 1024
CHUNK = 64
SUB = 16
PLE_DIM = 256
HEADS = 4
ML_DQK = 64
HALF = 512
D_FF = 2816
PROJ_W = 3592
PROJ_WP = 3712
ALPHA = float(2 ** 0.25)
LN_EPS = 1e-5
RMS_EPS = 1e-6
ML_SCALE = ML_DQK ** -0.5
N_DEV = 8
LR, B1, B2, EPS, WD, STEP = 0.001, 0.9, 0.999, 1e-08, 0.01, 10
NEG = -1e30

C_HQ, C_HF, C_HV, C_HGATE, C_MQK, C_MV, C_MO, C_GATES = 0, 4, 8, 12, 16, 20, 24, 28
DU_WIDTHS = (HALF,) * 7 + (128,)

VMEM_LIMIT = 52 * 1024 * 1024
GC = 8
ROWS = 512
ROWS_FFN = 256

NN = (((1,), (0,)), ((), ()))
NT = (((1,), (1,)), ((), ()))
TN = (((0,), (0,)), ((), ()))
BNT = (((2,), (2,)), ((0,), (0,)))
BNN = (((2,), (1,)), ((0,), (0,)))
BTN = (((1,), (1,)), ((0,), (0,)))


def _dot(a, b, dims=NN):
    return lax.dot_general(a.astype(_MXU), b.astype(_MXU), dims, preferred_element_type=F32)


def _dotx(a, b, dims=NN):
    return lax.dot_general(a, b, dims, precision=lax.Precision.HIGHEST, preferred_element_type=F32)


def _sig(x):
    return jax.nn.sigmoid(x)


def _cp(*sem):
    return pltpu.CompilerParams(dimension_semantics=sem, vmem_limit_bytes=VMEM_LIMIT)


def _row(tm, c, blk=0):
    return pl.BlockSpec((tm, c), lambda i, blk=blk: (i, blk))


def _full(shape):
    nd = len(shape)
    return pl.BlockSpec(tuple(shape), lambda *_, nd=nd: (0,) * nd)


def _sds(shape, dtype=F32):
    return jax.ShapeDtypeStruct(tuple(shape), dtype)


def _iota(shape, axis):
    return lax.broadcasted_iota(jnp.int32, shape, axis)


def _colsum(x):
    return jnp.sum(x, axis=0, keepdims=True)


def _rowsum(x):
    return jnp.sum(x, axis=1, keepdims=True)


def _ln_fwd(z, g, b):
    mu = jnp.mean(z, axis=-1, keepdims=True)
    zc = z - mu
    var = jnp.mean(zc * zc, axis=-1, keepdims=True)
    rstd = lax.rsqrt(var + LN_EPS)
    xhat = zc * rstd
    return xhat * g + b, xhat, rstd


def _ln_bwd(dy, xhat, rstd, g):
    dxh = dy * g
    m1 = jnp.mean(dxh, axis=-1, keepdims=True)
    m2 = jnp.mean(dxh * xhat, axis=-1, keepdims=True)
    return rstd * (dxh - m1 - xhat * m2)


def _dsilu(x, s):
    return s * (1.0 + x * (1.0 - s))


MESH = pl.DeviceIdType.MESH
ANY = pl.BlockSpec(memory_space=pl.ANY)


def _flip(v, bit):
    return 1 - v if bit else v


class _Comm:
    def __init__(self, kind, srcs):
        self.kind, self.srcs, self.n = kind, list(srcs), len(srcs)

    def out_shape(self):
        lead = (N_DEV,) if self.kind == "gather" else ()
        return [jax.ShapeDtypeStruct(lead + s.shape, s.dtype) for s in self.srcs]

    def scratch(self):
        return [pltpu.SemaphoreType.DMA((7 * self.n,)), pltpu.SemaphoreType.DMA((7 * self.n,)),
                pltpu.SemaphoreType.DMA((self.n,))]

    def copies(self, srcs, dsts, send_sems, recv_sems, local_sems):
        x, y, c = lax.axis_index("x"), lax.axis_index("y"), lax.axis_index("c")
        me = 4 * x + 2 * y + c
        pick = (lambda s, j: s) if self.kind == "gather" else (lambda s, j: s.at[j])
        out = []
        for i, (s, d) in enumerate(zip(srcs, dsts)):
            out.append(pltpu.make_async_copy(pick(s, me), d.at[me], local_sems.at[i]))
            for k in range(1, N_DEV):
                px, py, pc = _flip(x, k & 4), _flip(y, k & 2), _flip(c, k & 1)
                out.append(pltpu.make_async_remote_copy(
                    src_ref=pick(s, 4 * px + 2 * py + pc), dst_ref=d.at[me], send_sem=send_sems.at[7 * i + k - 1],
                    recv_sem=recv_sems.at[7 * i + k - 1], device_id=(px, py, pc), device_id_type=MESH))
        return out

    def start(self, *refs):
        for cp in self.copies(*refs):
            cp.start()

    def mid(self, *refs):
        pass

    def finish(self, *refs):
        for cp in self.copies(*refs):
            cp.wait()


class _GatherTwoLevel(_Comm):
    def __init__(self, srcs):
        super().__init__("gather", srcs)

    def _parts(self, srcs, dsts, send_sems, recv_sems, local_sems):
        x, y, c = lax.axis_index("x"), lax.axis_index("y"), lax.axis_index("c")
        me, sibling = (x, y, c), (x, y, 1 - c)
        chips = [(1 - x, y), (x, 1 - y), (1 - x, 1 - y)]

        def copy(i, k, block, to, own=False):
            slab = dsts[i].at[4 * block[0] + 2 * block[1] + block[2]]
            return pltpu.make_async_remote_copy(
                src_ref=srcs[i] if own else slab, dst_ref=slab, send_sem=send_sems.at[7 * i + k],
                recv_sem=recv_sems.at[7 * i + k], device_id=to, device_id_type=MESH)

        n = range(self.n)
        mine = [pltpu.make_async_copy(srcs[i], dsts[i].at[4 * x + 2 * y + c], local_sems.at[i]) for i in n]
        first = [copy(i, 0, me, sibling, own=True) for i in n]
        first += [copy(i, 1 + j, me, (*chip, c), own=True) for j, chip in enumerate(chips) for i in n]
        over_ici = [copy(i, 1 + j, (*chip, c), me) for j, chip in enumerate(chips) for i in n]
        passed = [copy(i, 4 + j, (*chip, c), sibling) for j, chip in enumerate(chips) for i in n]
        from_sibling = [copy(i, 0, sibling, me) for i in n]
        from_sibling += [copy(i, 4 + j, (*chip, 1 - c), me) for j, chip in enumerate(chips) for i in n]
        return mine, first, over_ici, passed, from_sibling

    def start(self, *refs):
        mine, first, _, _, _ = self._parts(*refs)
        for cp in mine + first:
            cp.start()

    def mid(self, *refs):
        _, _, over_ici, passed, _ = self._parts(*refs)
        for arrived, onward in zip(over_ici, passed):
            arrived.wait_recv()
            onward.start()

    def finish(self, *refs):
        mine, first, _, passed, from_sibling = self._parts(*refs)
        for cp in from_sibling:
            cp.wait_recv()
        for cp in first + passed:
            cp.wait_send()
        for cp in mine:
            cp.wait()


def _hosted_call(body, comms, *, name, grid, in_specs, out_specs, out_shape, scratch_shapes, args):
    comms = list(comms or [])
    if not comms:
        res = pl.pallas_call(body, name=name, grid=grid, in_specs=in_specs, out_specs=out_specs, out_shape=out_shape,
                             scratch_shapes=scratch_shapes, compiler_params=_cp("arbitrary"))(*args)
        return list(res), []
    n_in, n_out, n_sc, nc = len(in_specs), len(out_specs), len(scratch_shapes), sum(cm.n for cm in comms)
    last = grid[0] - 1

    def hosted(*refs):
        ins, csrc = refs[:n_in], refs[n_in:n_in + nc]
        o0 = n_in + nc
        outs, cdst = refs[o0:o0 + n_out], refs[o0 + n_out:o0 + n_out + nc]
        s0 = o0 + n_out + nc
        scr, sems = refs[s0:s0 + n_sc], refs[s0 + n_sc:]

        def phase(which):
            o = 0
            for j, cm in enumerate(comms):
                getattr(cm, which)(csrc[o:o + cm.n], cdst[o:o + cm.n], *sems[3 * j:3 * j + 3])
                o += cm.n

        i = pl.program_id(0)

        @pl.when(i == 0)
        def _():
            phase("start")

        body(*ins, *outs, *scr)

        @pl.when(i == (2 * last) // 3)
        def _():
            phase("mid")

        @pl.when(i == last)
        def _():
            phase("finish")

    res = pl.pallas_call(
        hosted, name=name, grid=grid, in_specs=list(in_specs) + [ANY] * nc, out_specs=list(out_specs) + [ANY] * nc,
        out_shape=list(out_shape) + [s for cm in comms for s in cm.out_shape()],
        scratch_shapes=list(scratch_shapes) + [s for cm in comms for s in cm.scratch()],
        compiler_params=_cp("arbitrary"))(*args, *[a for cm in comms for a in cm.srcs])
    got, o = [], n_out
    for cm in comms:
        got.append(list(res[o:o + cm.n]))
        o += cm.n
    return list(res[:n_out]), got


def _gather_two_level(blocks, name):
    n = len(blocks)

    def body(*refs):
        x_refs, out_refs = refs[:n], refs[n:2 * n]
        send_sems, recv_sems, local_sems = refs[2 * n:]
        x, y, c = lax.axis_index("x"), lax.axis_index("y"), lax.axis_index("c")
        me, sibling = (x, y, c), (x, y, 1 - c)
        chips = [(1 - x, y), (x, 1 - y), (1 - x, 1 - y)]

        def copy(i, k, block, to, own=False):
            slab = out_refs[i].at[4 * block[0] + 2 * block[1] + block[2]]
            return pltpu.make_async_remote_copy(
                src_ref=x_refs[i] if own else slab, dst_ref=slab, send_sem=send_sems.at[7 * i + k],
                recv_sem=recv_sems.at[7 * i + k], device_id=to, device_id_type=MESH)

        mine = [pltpu.make_async_copy(x_refs[i], out_refs[i].at[4 * x + 2 * y + c], local_sems.at[i]) for i in range(n)]
        for cp in mine:
            cp.start()
        first = [copy(i, 0, me, sibling, own=True) for i in range(n)]
        first += [copy(i, 1 + j, me, (*chip, c), own=True) for j, chip in enumerate(chips) for i in range(n)]
        for cp in first:
            cp.start()
        passed = []
        for j, chip in enumerate(chips):
            for i in range(n):
                copy(i, 1 + j, (*chip, c), me).wait_recv()
                passed.append(copy(i, 4 + j, (*chip, c), sibling))
                passed[-1].start()
        for i in range(n):
            copy(i, 0, sibling, me).wait_recv()
            for j, chip in enumerate(chips):
                copy(i, 4 + j, (*chip, 1 - c), me).wait_recv()
        for cp in first + passed:
            cp.wait_send()
        for cp in mine:
            cp.wait()

    return pl.pallas_call(
        body, name=name, out_shape=[jax.ShapeDtypeStruct((N_DEV,) + b.shape, b.dtype) for b in blocks],
        in_specs=[ANY] * n, out_specs=[ANY] * n,
        scratch_shapes=[pltpu.SemaphoreType.DMA((7 * n,)), pltpu.SemaphoreType.DMA((7 * n,)),
                        pltpu.SemaphoreType.DMA((n,))])(*blocks)


def _in_proj(x, w, b, tm, comms=None):
    T = x.shape[0]

    def body(x_ref, w_ref, b_ref, o_ref):
        o_ref[...] = _dot(x_ref[...], w_ref[...]) + b_ref[...]

    (u,), got = _hosted_call(
        body, comms, name="in_proj", grid=(T // tm,),
        in_specs=[_row(tm, D_MODEL), _full(w.shape), _full(b.shape)],
        out_specs=[_row(tm, PROJ_WP)], out_shape=[_sds((T, PROJ_WP))], scratch_shapes=[], args=(x, w, b))
    return u, got


def _shift_rows(x, halo, j, rowi):
    r = pltpu.roll(x, j, 0)
    top = jnp.where(rowi < j, pltpu.roll(halo, j, 0), r[:8])
    return jnp.concatenate([top, r[8:]], axis=0)


def _shift_rows_up(x, halo, j, rowi):
    n = x.shape[0]
    r = pltpu.roll(x, n - j, 0)
    bot = jnp.where(rowi >= 8 - j, pltpu.roll(halo, 8 - j, 0), r[n - 8:])
    return jnp.concatenate([r[:n - 8], bot], axis=0)


def _conv_fwd(u, cw, cb, tm):
    T = u.shape[0]
    hb = tm // 8

    def body(x_ref, halo_ref, w_ref, b_ref, pre_ref, out_ref):
        i = pl.program_id(0)
        x = x_ref[...]
        halo = jnp.where(i > 0, halo_ref[...], 0.0)
        rowi = _iota((8, HALF), 0)
        acc = x * w_ref[3:4, :] + b_ref[...]
        for j in (1, 2, 3):
            acc = acc + _shift_rows(x, halo, j, rowi) * w_ref[3 - j:4 - j, :]
        pre_ref[...] = acc
        out_ref[...] = acc * _sig(acc)

    return pl.pallas_call(
        body, name="conv_fwd", grid=(T // tm,),
        in_specs=[pl.BlockSpec((tm, HALF), lambda i: (i, C_MQK // 4)),
                  pl.BlockSpec((8, HALF), lambda i: (jnp.maximum(i * hb - 1, 0), C_MQK // 4)),
                  _full(cw.shape), _full(cb.shape)],
        out_specs=[_row(tm, HALF), _row(tm, HALF)], out_shape=[_sds((T, HALF)), _sds((T, HALF))],
        compiler_params=_cp("parallel"))(u, u, cw, cb)


def _bdot(a, b, dims):
    return lax.dot_general(a.astype(_MXU), b.astype(_MXU), dims, preferred_element_type=F32)


def _bdotx(a, b, dims):
    return lax.dot_general(a, b, dims, precision=lax.Precision.HIGHEST, preferred_element_type=F32)


def _heads_to_batch(x, w):
    G = x.shape[0] // CHUNK
    x3 = x.reshape(G, CHUNK, HEADS * w)
    return jnp.stack([x3[:, :, w * h:w * (h + 1)] for h in range(HEADS)], axis=1).reshape(G * HEADS, CHUNK, w)


def _batch_to_heads(x3):
    B, _, w = x3.shape
    x4 = x3.reshape(B // HEADS, HEADS, CHUNK, w)
    return jnp.concatenate([x4[:, h] for h in range(HEADS)], axis=-1).reshape(B // HEADS * CHUNK, HEADS * w)


def _chunk_cumsum(x, rowmod, reverse=False):
    R = x.shape[0]
    for sh in (1, 2, 4, 8, 16, 32):
        if reverse:
            x = x + jnp.where(rowmod < CHUNK - sh, pltpu.roll(x, R - sh, 0), 0.0)
        else:
            x = x + jnp.where(rowmod >= sh, pltpu.roll(x, sh, 0), 0.0)
    return x


def _lane_col(x, c, lane):
    return _rowsum(jnp.where(lane == c, x, 0.0))


def _hg_gates(hq, hf, lb):
    sg = _sig(hf)
    nsg = _sig(-hf)
    f = lb + (1.0 - lb) * sg
    g = jnp.log(f)
    k = (1.0 - lb) * nsg
    sq = _sig(hq)
    return hq * sq, g, k, f, sg, nsg, sq


def _hg_prep(hq_ref, hf_ref, lg_ref, b_sc, k_sc):
    R = hq_ref.shape[0]
    G = R // CHUNK
    lb = _sig(lg_ref[0:1, :] - lg_ref[1:2, :])
    hq = hq_ref[...]
    q, g, k, f, sg, nsg, sq = _hg_gates(hq, hf_ref[...], lb)
    rowmod = _iota((R, HALF), 0) & (CHUNK - 1)
    b = _chunk_cumsum(g, rowmod)
    last8 = _iota((8, HALF), 0) == 7
    bl_rows = [_colsum(jnp.where(last8, b[CHUNK * c + CHUNK - 8:CHUNK * (c + 1)], 0.0)) for c in range(G)]
    bl3 = jnp.stack([r[:, 128 * h:128 * (h + 1)] for r in bl_rows for h in range(HEADS)], axis=0)
    b3, k3 = _heads_to_batch(b, 128), _heads_to_batch(k, 128)
    b_sc[...] = b3
    k_sc[...] = k3
    return dict(G=G, lb=lb, hq=hq, f=f, sg=sg, nsg=nsg, sq=sq, rowmod=rowmod, q3=_heads_to_batch(q, 128), k3=k3, b3=b3,
                bl3=bl3)


HSUB = SUB // 2


def _lo(j):
    return HSUB * (j // HSUB)


def _hg_diag_tiles(b_sc, b3, r0, rowi):
    bi = b3[:, r0:r0 + SUB]
    return [jnp.exp(jnp.where(rowi[:, _lo(s):] >= s, bi[:, _lo(s):] - b_sc[:, r0 + s:r0 + s + 1, :], NEG))
            for s in range(SUB)]


def _hg_diag_tiles_t(b_sc, b3, r0, rowi):
    bi = b3[:, r0:r0 + SUB]
    return [jnp.exp(jnp.where(rowi[:, :_lo(t) + HSUB] <= t, b_sc[:, r0 + t:r0 + t + 1, :] - bi[:, :_lo(t) + HSUB], NEG))
            for t in range(SUB)]


def _lane_sums(pieces, ones):
    B = pieces[0].shape[0]
    hs = [p.shape[1] for p in pieces]
    R = _dot(jnp.concatenate(pieces, axis=1).reshape(B * sum(hs), 128), ones).reshape(B, sum(hs), 128)
    out, o = [], 0
    for h in hs:
        out.append(R[:, o:o + h])
        o += h
    return out


def _sum_tri(terms, low_rows):
    full = sum(t for t in terms if t.shape[1] == SUB)
    half = sum(t for t in terms if t.shape[1] == HSUB)
    lo, hi = full[:, :HSUB], full[:, HSUB:]
    return jnp.concatenate([lo + half, hi] if low_rows else [lo, hi + half], axis=1)


def _hgrn2_fwd(u, lb_logits, comms=None):
    T = u.shape[0]
    G = min(GC, T // CHUNK)
    R, B, N = G * CHUNK, G * HEADS, T // CHUNK

    def body(hq_ref, hf_ref, hv_ref, lg_ref, o_ref, st_ref, S_ref, b_sc, k_sc, v_sc):
        @pl.when(pl.program_id(0) == 0)
        def _():
            S_ref[...] = jnp.zeros_like(S_ref)

        pz = _hg_prep(hq_ref, hf_ref, lg_ref, b_sc, k_sc)
        q3, k3, b3, bl3 = pz["q3"], pz["k3"], pz["b3"], pz["bl3"]
        v3 = _heads_to_batch(hv_ref[...], 128)
        v_sc[...] = v3
        stloc = _bdot(v3, k3 * jnp.exp(bl3 - b3), BTN).reshape(G, HEADS, 128, 128)
        dec = jnp.exp(bl3).reshape(G, HEADS, 1, 128)
        ST = S_ref[...]
        sts = []
        for c in range(G):
            sts.append(ST)
            ST = ST * dec[c] + stloc[c]
        S_ref[...] = ST
        st4 = jnp.stack(sts, axis=0)
        st_ref[...] = st4
        o = _bdot(q3 * jnp.exp(b3), st4.reshape(B, 128, 128), BNT)
        ones = jnp.ones((128, 128), F32)
        rowi = _iota((1, SUB, 128), 1)
        outs = []
        for i in range(CHUNK // SUB):
            r0 = SUB * i
            qi = q3[:, r0:r0 + SUB]
            oi = o[:, r0:r0 + SUB]
            if i > 0:
                r = b_sc[:, r0 - 1:r0, :]
                qe = qi * jnp.exp(b3[:, r0:r0 + SUB] - r)
                ke = k3[:, :r0] * jnp.exp(r - b3[:, :r0])
                oi = oi + _bdot(_bdot(qe, ke, BNT), v3[:, :r0], BNN)
            tiles = _hg_diag_tiles(b_sc, b3, r0, rowi)
            a_b = _lane_sums([qi[:, _lo(s):] * (k_sc[:, r0 + s:r0 + s + 1, :] * tiles[s]) for s in range(SUB)], ones)
            outs.append(oi + _sum_tri([a_b[s] * v_sc[:, r0 + s:r0 + s + 1, :] for s in range(SUB)], False))
        o_ref[...] = _batch_to_heads(jnp.concatenate(outs, axis=1))

    blk = lambda c: pl.BlockSpec((R, HALF), lambda n, c=c: (n, c // 4))
    return _hosted_call(
        body, comms, name="hgrn2_fwd", grid=(N // G,),
        in_specs=[blk(C_HQ), blk(C_HF), blk(C_HV), _full(lb_logits.shape)],
        out_specs=[pl.BlockSpec((R, HALF), lambda n: (n, 0)),
                   pl.BlockSpec((G, HEADS, 128, 128), lambda n: (n, 0, 0, 0))],
        out_shape=[_sds((T, HALF)), _sds((N, HEADS, 128, 128))],
        scratch_shapes=[pltpu.VMEM((HEADS, 128, 128), F32)] + [pltpu.VMEM((B, CHUNK, 128), F32)] * 3,
        args=(u, u, u, lb_logits))


def _hgrn2_bwd(u, lb_logits, do, states, comms=None):
    T = u.shape[0]
    G = min(GC, T // CHUNK)
    R, B, NG = G * CHUNK, G * HEADS, T // (G * CHUNK)

    def body(hq_ref, hf_ref, hv_ref, lg_ref, do_ref, st_ref, dhq_ref, dhf_ref, dhv_ref, dlb_ref,
             dS_ref, b_sc, k_sc, v_sc, q_sc, do_sc):
        @pl.when(pl.program_id(0) == 0)
        def _():
            dS_ref[...] = jnp.zeros_like(dS_ref)
            dlb_ref[...] = jnp.zeros_like(dlb_ref)

        pz = _hg_prep(hq_ref, hf_ref, lg_ref, b_sc, k_sc)
        q3, k3, b3, bl3, lb = pz["q3"], pz["k3"], pz["b3"], pz["bl3"], pz["lb"]
        v3 = _heads_to_batch(hv_ref[...], 128)
        v_sc[...] = v3
        do3 = _heads_to_batch(do_ref[...], 128)
        q_sc[...] = q3
        do_sc[...] = do3
        st3 = st_ref[...].reshape(B, 128, 128)
        eb = jnp.exp(b3)
        ebl = jnp.exp(bl3 - b3)
        qt = q3 * eb
        kl = k3 * ebl
        dstloc = _bdot(do3, qt, BTN).reshape(G, HEADS, 128, 128)
        dec = jnp.exp(bl3).reshape(G, HEADS, 1, 128)
        dST = dS_ref[...]
        dsts = [None] * G
        for c in reversed(range(G)):
            dsts[c] = dST
            dST = dST * dec[c] + dstloc[c]
        dS_ref[...] = dST
        dst3 = jnp.stack(dsts, axis=0).reshape(B, 128, 128)
        dqt = _bdot(do3, st3, BNN)
        dkl = _bdot(v3, dst3, BNN)
        dv_acc = _bdot(kl, dst3, BNT)
        ones = jnp.ones((128, 128), F32)
        rowi = _iota((1, SUB, 128), 1)
        dq_parts, dk_parts, dv_parts = [], [], []
        dk_in = jnp.zeros((B, CHUNK, 128), F32)
        for i_s in range(CHUNK // SUB):
            r0 = SUB * i_s
            qi = q3[:, r0:r0 + SUB]
            doi = do3[:, r0:r0 + SUB]
            dqi = jnp.zeros((B, SUB, 128), F32)
            if i_s > 0:
                r = b_sc[:, r0 - 1:r0, :]
                eq = jnp.exp(b3[:, r0:r0 + SUB] - r)
                ek = jnp.exp(r - b3[:, :r0])
                qe = qi * eq
                ke = k3[:, :r0] * ek
                a_off = _bdot(qe, ke, BNT)
                p_off = _bdot(doi, v3[:, :r0], BNT)
                pad = jnp.zeros((B, CHUNK - r0, 128), F32)
                dv_acc = dv_acc + jnp.concatenate([_bdot(a_off, doi, BTN), pad], axis=1)
                dqi = dqi + _bdot(p_off, ke, BNN) * eq
                dk_in = dk_in + jnp.concatenate([_bdot(p_off, qe, BTN) * ek, pad], axis=1)
            ki, vi = k3[:, r0:r0 + SUB], v3[:, r0:r0 + SUB]
            rng = range(SUB)
            tiles = _hg_diag_tiles(b_sc, b3, r0, rowi)
            tiles_t = _hg_diag_tiles_t(b_sc, b3, r0, rowi)
            do_rows = [do_sc[:, r0 + t:r0 + t + 1, :] for t in rng]
            kts = [k_sc[:, r0 + s:r0 + s + 1, :] * tiles[s] for s in rng]
            qts = [q_sc[:, r0 + t:r0 + t + 1, :] * tiles_t[t] for t in rng]
            ps = [doi[:, _lo(s):] * v_sc[:, r0 + s:r0 + s + 1, :] for s in rng]
            mst = [ki[:, :_lo(t) + HSUB] * qts[t] for t in rng]
            pst = [vi[:, :_lo(t) + HSUB] * do_rows[t] for t in rng]
            sums = _lane_sums(ps + mst + pst, ones)
            p_b, a_t, p_t = sums[:SUB], sums[SUB:2 * SUB], sums[2 * SUB:]
            dq_parts.append(dqi + _sum_tri([p_b[s] * kts[s] for s in rng], False))
            dv_parts.append(_sum_tri([a_t[t] * do_rows[t] for t in rng], True))
            dk_parts.append(_sum_tri([p_t[t] * qts[t] for t in rng], True))
        dq_in = jnp.concatenate(dq_parts, axis=1)
        dk_in = dk_in + jnp.concatenate(dk_parts, axis=1)
        dv_acc = dv_acc + jnp.concatenate(dv_parts, axis=1)
        db = qt * dqt + q3 * dq_in - k3 * dk_in - kl * dkl
        last = jnp.sum(kl * dkl, axis=1, keepdims=True) + jnp.exp(bl3) * jnp.sum(st3 * dst3, axis=1, keepdims=True)
        db = db + jnp.where(_iota((1, CHUNK, 1), 1) == CHUNK - 1, last, 0.0)
        dg = _chunk_cumsum(_batch_to_heads(db), pz["rowmod"], reverse=True)
        dq_tot = _batch_to_heads(dqt * eb + dq_in)
        dk_tot = _batch_to_heads(dkl * ebl + dk_in)
        common = dg / pz["f"] - dk_tot
        dhf_ref[...] = ((1.0 - lb) * pz["sg"] * pz["nsg"] * common).astype(dhf_ref.dtype)
        dl0 = _colsum(pz["nsg"] * common) * lb * (1.0 - lb)
        dlb_ref[0:1, :] += dl0
        dlb_ref[1:2, :] -= dl0
        dhq_ref[...] = (dq_tot * _dsilu(pz["hq"], pz["sq"])).astype(dhq_ref.dtype)
        dhv_ref[...] = _batch_to_heads(dv_acc).astype(dhv_ref.dtype)

    rev = lambda c: pl.BlockSpec((R, HALF), lambda i, c=c: (NG - 1 - i, c // 4))
    rev0 = pl.BlockSpec((R, HALF), lambda i: (NG - 1 - i, 0))
    return _hosted_call(
        body, comms, name="hgrn2_bwd", grid=(NG,),
        in_specs=[rev(C_HQ), rev(C_HF), rev(C_HV), _full(lb_logits.shape), rev0,
                  pl.BlockSpec((G, HEADS, 128, 128), lambda i: (NG - 1 - i, 0, 0, 0))],
        out_specs=[rev0, rev0, rev0, _full((2, HALF))],
        out_shape=[_sds((T, HALF), _MXU)] * 3 + [_sds((2, HALF))],
        scratch_shapes=[pltpu.VMEM((HEADS, 128, 128), F32)] + [pltpu.VMEM((B, CHUNK, 128), F32)] * 5,
        args=(u, u, u, lb_logits, do, states))


def _lanes_to_batch_cols(x, lane):
    G = x.shape[0] // CHUNK
    cols = [_lane_col(x, 4 + h, lane).reshape(G, CHUNK, 1) for h in range(HEADS)]
    return jnp.stack(cols, axis=1).reshape(G * HEADS, CHUNK, 1)


def _row_scalars(rows):
    lane = _iota((1, 128), 1)
    return jnp.stack([_rowsum(jnp.where(lane == 4 + h, r, 0.0)) for r in rows for h in range(HEADS)], axis=0)


def _ml_gates(gates):
    R = gates.shape[0]
    lane = _iota((R, 128), 1)
    rowmod = _iota((R, 128), 0) & (CHUNK - 1)
    lf = jnp.minimum(gates, 0.0) - jnp.log(1.0 + jnp.exp(-jnp.abs(gates)))
    g_all = _chunk_cumsum(lf, rowmod)
    x_all = pltpu.roll(gates, 4, 1) - g_all
    return g_all, x_all, lane, rowmod


def _ml_chunk_rows(g_all, x_all, mprev, g):
    gl = g_all[CHUNK * g + CHUNK - 8:CHUNK * (g + 1)]
    gl = _colsum(jnp.where(_iota((8, 128), 0) == 7, gl, 0.0))
    a = gl + x_all[CHUNK * g:CHUNK * (g + 1)]
    m_new = jnp.maximum(gl + mprev, jnp.max(a, axis=0, keepdims=True))
    return m_new, jnp.exp(gl + mprev - m_new), jnp.exp(a - m_new)


def _ml_batched(q3, k3, v3, g_all, x_all, lane, C3, n3, mprev3):
    G = g_all.shape[0] // CHUNK
    gcol3 = _lanes_to_batch_cols(g_all, lane)
    onehot = jnp.where(_iota((G, 8, 128), 1) + 4 == _iota((G, 8, 128), 2), 1.0, 0.0).astype(F32)
    rows = _bdotx(onehot, x_all.reshape(G, CHUNK, 128), BNT)
    sub = _iota((G, 8, CHUNK), 1)
    row3 = jnp.stack([jnp.sum(jnp.where(sub == h, rows, 0.0), axis=1, keepdims=True) for h in range(HEADS)],
                     axis=1).reshape(G * HEADS, 1, CHUNK)
    causal = _iota((1, CHUNK, CHUNK), 1) >= _iota((1, CHUNK, CHUNK), 2)
    dmat = jnp.where(causal, gcol3 + row3, NEG)
    m_inter = gcol3 + mprev3
    m_t = jnp.maximum(m_inter, jnp.max(dmat, axis=2, keepdims=True))
    wi = jnp.exp(dmat - m_t)
    wn = jnp.exp(m_inter - m_t)
    s3 = _bdot(q3, k3, BNT) * wi
    qc = _bdot(q3, C3, BNN)
    qn = jnp.sum(q3 * n3, axis=2, keepdims=True)
    num = _bdot(s3, v3, BNN) + wn * qc
    den = jnp.sum(s3, axis=2, keepdims=True) + wn * qn
    floor = jnp.exp(-m_t)
    return dict(wi=wi, wn=wn, s=s3, qc=qc, qn=qn, num=num, den=den, floor=floor, nrm=jnp.maximum(jnp.abs(den), floor))


def _mlstm_fwd(qkc, u, comms=None):
    T = u.shape[0]
    G = min(GC, T // CHUNK)
    R = G * CHUNK
    N = T // CHUNK

    def body(qk_ref, v_ref, g_ref, h_ref, cst_ref, nst_ref, mst_ref, C_ref, n_ref, m_ref):
        @pl.when(pl.program_id(0) == 0)
        def _():
            C_ref[...] = jnp.zeros_like(C_ref)
            n_ref[...] = jnp.zeros_like(n_ref)
            m_ref[...] = jnp.zeros_like(m_ref)

        g_all, x_all, lane, _ = _ml_gates(g_ref[...])
        m_row = m_ref[...]
        mprev_rows, wo_rows, ws_parts = [], [], []
        for g in range(G):
            mprev_rows.append(m_row)
            m_row, wo, ws = _ml_chunk_rows(g_all, x_all, m_row, g)
            wo_rows.append(wo)
            ws_parts.append(ws)
        m_ref[...] = m_row
        mst_ref[...] = jnp.stack(mprev_rows, axis=0)
        ws3 = _lanes_to_batch_cols(jnp.concatenate(ws_parts, axis=0), lane)
        wo4 = _row_scalars(wo_rows).reshape(G, HEADS, 1, 1)
        q3 = _heads_to_batch(qk_ref[:, :256] * ML_SCALE, ML_DQK)
        k3 = _heads_to_batch(qk_ref[:, 256:], ML_DQK)
        v3 = _heads_to_batch(v_ref[...], 128)
        kw = k3 * ws3
        cloc = _bdot(kw, v3, BTN).reshape(G, HEADS, ML_DQK, 128)
        nloc = jnp.sum(kw, axis=1, keepdims=True).reshape(G, HEADS, 1, ML_DQK)
        C, nn = C_ref[...], n_ref[...]
        cs, ns = [], []
        for g in range(G):
            cs.append(C)
            ns.append(nn)
            C = wo4[g] * C + cloc[g]
            nn = wo4[g] * nn + nloc[g]
        C_ref[...] = C
        n_ref[...] = nn
        c4, n4 = jnp.stack(cs, axis=0), jnp.stack(ns, axis=0)
        cst_ref[...] = c4
        nst_ref[...] = n4
        r = _ml_batched(q3, k3, v3, g_all, x_all, lane, c4.reshape(G * HEADS, ML_DQK, 128),
                        n4.reshape(G * HEADS, 1, ML_DQK), _row_scalars(mprev_rows))
        h_ref[...] = _batch_to_heads(r["num"] / r["nrm"])

    return _hosted_call(
        body, comms, name="mlstm_fwd", grid=(N // G,),
        in_specs=[pl.BlockSpec((R, HALF), lambda n: (n, 0)), pl.BlockSpec((R, HALF), lambda n: (n, C_MV // 4)),
                  pl.BlockSpec((R, 128), lambda n: (n, C_GATES))],
        out_specs=[pl.BlockSpec((R, HALF), lambda n: (n, 0)),
                   pl.BlockSpec((G, HEADS, ML_DQK, 128), lambda n: (n, 0, 0, 0)),
                   pl.BlockSpec((G, HEADS, 1, ML_DQK), lambda n: (n, 0, 0, 0)),
                   pl.BlockSpec((G, 1, 128), lambda n: (n, 0, 0))],
        out_shape=[_sds((T, HALF)), _sds((N, HEADS, ML_DQK, 128)), _sds((N, HEADS, 1, ML_DQK)), _sds((N, 1, 128))],
        scratch_shapes=[pltpu.VMEM((HEADS, ML_DQK, 128), F32), pltpu.VMEM((HEADS, 1, ML_DQK), F32),
                        pltpu.VMEM((1, 128), F32)],
        args=(qkc, u, u))


def _mlstm_bwd(qkc, u, dh, cst, nst, mst):
    T = u.shape[0]
    G = min(GC, T // CHUNK)
    R = G * CHUNK
    NG = T // R

    def body(qk_ref, v_ref, g_ref, dh_ref, cst_ref, nst_ref, mst_ref, dqk_ref, dv_ref, dgt_ref, dC_ref, dn_ref):
        @pl.when(pl.program_id(0) == 0)
        def _():
            dC_ref[...] = jnp.zeros_like(dC_ref)
            dn_ref[...] = jnp.zeros_like(dn_ref)

        B = G * HEADS
        gates = g_ref[...]
        g_all, x_all, lane, rowmod = _ml_gates(gates)
        mprev_rows = [mst_ref[g] for g in range(G)]
        wo_rows, ws_parts = [], []
        for g in range(G):
            _, wo, ws = _ml_chunk_rows(g_all, x_all, mprev_rows[g], g)
            wo_rows.append(wo)
            ws_parts.append(ws)
        ws3 = _lanes_to_batch_cols(jnp.concatenate(ws_parts, axis=0), lane)
        wo3 = _row_scalars(wo_rows)
        wo4 = wo3.reshape(G, HEADS, 1, 1)
        q3 = _heads_to_batch(qk_ref[:, :256] * ML_SCALE, ML_DQK)
        k3 = _heads_to_batch(qk_ref[:, 256:], ML_DQK)
        v3 = _heads_to_batch(v_ref[...], 128)
        dh3 = _heads_to_batch(dh_ref[...], 128)
        C3 = cst_ref[...].reshape(B, ML_DQK, 128)
        n3 = nst_ref[...].reshape(B, 1, ML_DQK)
        r = _ml_batched(q3, k3, v3, g_all, x_all, lane, C3, n3, _row_scalars(mprev_rows))
        wn, s3 = r["wn"], r["s"]
        inv = 1.0 / r["nrm"]
        dnum = dh3 * inv
        dnrm = -jnp.sum(dh3 * (r["num"] * inv), axis=2, keepdims=True) * inv
        dden = jnp.where(jnp.abs(r["den"]) > r["floor"], dnrm * jnp.sign(r["den"]), 0.0)
        ds = _bdot(dnum, v3, BNT) + dden
        dqk = ds * r["wi"]
        dd = ds * s3
        qw = q3 * wn
        dcloc = _bdot(qw, dnum, BTN).reshape(G, HEADS, ML_DQK, 128)
        dnloc = jnp.sum(qw * dden, axis=1, keepdims=True).reshape(G, HEADS, 1, ML_DQK)
        dC, dn = dC_ref[...], dn_ref[...]
        dcs, dns = [None] * G, [None] * G
        for g in reversed(range(G)):
            dcs[g], dns[g] = dC, dn
            dC = wo4[g] * dC + dcloc[g]
            dn = wo4[g] * dn + dnloc[g]
        dC_ref[...] = dC
        dn_ref[...] = dn
        dC3 = jnp.stack(dcs, axis=0).reshape(B, ML_DQK, 128)
        dn3 = jnp.stack(dns, axis=0).reshape(B, 1, ML_DQK)
        dk_st = ws3 * (_bdot(v3, dC3, BNT) + dn3)
        dq = _bdot(dqk, k3, BNN) + wn * (_bdot(dnum, C3, BNT) + dden * n3)
        dk = _bdot(dqk, q3, BTN) + dk_st
        dv = _bdot(s3, dnum, BTN) + ws3 * _bdot(k3, dC3, BNN)
        dv_ref[...] = _batch_to_heads(dv).astype(dv_ref.dtype)
        dqk_ref[...] = jnp.concatenate([_batch_to_heads(dq * ML_SCALE), _batch_to_heads(dk)], axis=1)
        e_col = wn * (jnp.sum(dnum * r["qc"], axis=2, keepdims=True) + dden * r["qn"])
        c_col = jnp.sum(k3 * dk_st, axis=2, keepdims=True)
        z = wo3 * (jnp.sum(dC3 * C3, axis=(1, 2), keepdims=True) + jnp.sum(dn3 * n3, axis=(1, 2), keepdims=True))
        dd_cols = _bdotx(dd, jnp.ones((B, CHUNK, 128), F32), BTN)[:, :, 0:1]
        last = _iota((1, CHUNK, 1), 1) == CHUNK - 1
        dg3 = jnp.sum(dd, axis=2, keepdims=True) - dd_cols + e_col - c_col
        dg3 = dg3 + jnp.where(last, jnp.sum(c_col, axis=1, keepdims=True) + z, 0.0)
        di3 = dd_cols + c_col

        def to_lanes(x3, first):
            x4 = x3.reshape(G, HEADS, CHUNK, 1)
            return sum(jnp.where(lane == first + h, x4[:, h].reshape(R, 1), 0.0) for h in range(HEADS))

        dlf = _chunk_cumsum(to_lanes(dg3, 4), rowmod, reverse=True)
        dgt_ref[...] = (to_lanes(di3, 0) + dlf * _sig(-gates)).astype(dgt_ref.dtype)

    rev = lambda w, c: pl.BlockSpec((R, w), lambda i, c=c: (NG - 1 - i, c))
    st = lambda *s: pl.BlockSpec((G,) + s, lambda i: (NG - 1 - i,) + (0,) * len(s))
    return pl.pallas_call(
        body, name="mlstm_bwd", grid=(NG,),
        in_specs=[rev(HALF, 0), rev(HALF, C_MV // 4), rev(128, C_GATES), rev(HALF, 0),
                  st(HEADS, ML_DQK, 128), st(HEADS, 1, ML_DQK), st(1, 128)],
        out_specs=[rev(HALF, 0), rev(HALF, 0), rev(128, 0)],
        out_shape=[_sds((T, HALF)), _sds((T, HALF), _MXU), _sds((T, 128), _MXU)],
        scratch_shapes=[pltpu.VMEM((HEADS, ML_DQK, 128), F32), pltpu.VMEM((HEADS, 1, ML_DQK), F32)],
        compiler_params=_cp("arbitrary"))(qkc, u, u, dh, cst, nst, mst)


def _head_norm(o):
    rs_parts, r_parts = [], []
    for h in range(HEADS):
        oh = o[:, 128 * h:128 * (h + 1)]
        rs = lax.rsqrt(jnp.mean(oh * oh, axis=-1, keepdims=True) + RMS_EPS)
        rs_parts.append(rs)
        r_parts.append(oh * rs)
    return jnp.concatenate(r_parts, axis=1), rs_parts


def _out_proj_ln(x, u, o_hg, h_ml, g_hg, g_ml, w_out, ln_g, ln_b, tm):
    T = x.shape[0]

    def body(x_ref, hgate_ref, mo_ref, ohg_ref, hml_ref, ghg_ref, gml_ref, w_ref, g_ref, b_ref,
             m_ref, z_ref, x1_ref):
        hgate = hgate_ref[...]
        a = _head_norm(ohg_ref[...])[0] * ghg_ref[...] * (hgate * _sig(hgate))
        b = _head_norm(hml_ref[...])[0] * gml_ref[...] * _sig(mo_ref[...])
        m = jnp.concatenate([a, b], axis=1)
        m_ref[...] = m.astype(m_ref.dtype)
        z = ALPHA * x_ref[...] + _dot(m, w_ref[...])
        z_ref[...] = z
        x1_ref[...] = _ln_fwd(z, g_ref[...], b_ref[...])[0]

    return pl.pallas_call(
        body, name="out_proj_ln1", grid=(T // tm,),
        in_specs=[_row(tm, D_MODEL), _row(tm, HALF, C_HGATE // 4), _row(tm, HALF, C_MO // 4),
                  _row(tm, HALF), _row(tm, HALF), _full(g_hg.shape), _full(g_ml.shape),
                  _full(w_out.shape), _full(ln_g.shape), _full(ln_b.shape)],
        out_specs=[_row(tm, D_MODEL)] * 3,
        out_shape=[_sds((T, D_MODEL), _MXU), _sds((T, D_MODEL)), _sds((T, D_MODEL))],
        compiler_params=_cp("parallel"))(x, u, u, o_hg, h_ml, g_hg, g_ml, w_out, ln_g, ln_b)


def _ffn_ln(x1, wg, wu, wd, ln_g, ln_b, tm):
    T = x1.shape[0]

    def body(x_ref, wg_ref, wu_ref, wd_ref, g_ref, b_ref, z_ref, x2_ref, a_ref, bb_ref, h_ref):
        x = x_ref[...]
        a = _dot(x, wg_ref[...])
        bb = _dot(x, wu_ref[...])
        hh = a * _sig(a) * bb
        a_ref[...] = a.astype(a_ref.dtype)
        bb_ref[...] = bb.astype(bb_ref.dtype)
        h_ref[...] = hh.astype(h_ref.dtype)
        z = ALPHA * x + _dot(hh, wd_ref[...])
        z_ref[...] = z
        x2_ref[...] = _ln_fwd(z, g_ref[...], b_ref[...])[0]

    return pl.pallas_call(
        body, name="ffn_ln2", grid=(T // tm,),
        in_specs=[_row(tm, D_MODEL), _full(wg.shape), _full(wu.shape), _full(wd.shape),
                  _full(ln_g.shape), _full(ln_b.shape)],
        out_specs=[_row(tm, D_MODEL)] * 2 + [_row(tm, D_FF)] * 3,
        out_shape=[_sds((T, D_MODEL))] * 2 + [_sds((T, D_FF), _MXU)] * 3,
        compiler_params=_cp("parallel"))(x1, wg, wu, wd, ln_g, ln_b)


def _ple_loss_ln2_bwd(x2, z2, p, tgt, wpg, bpg, wpp, ln_g, ln_b, tm):
    T = x2.shape[0]

    def body(x2_ref, z_ref, p_ref, t_ref, wpg_ref, bpg_ref, wpp_ref, g_ref, b_ref,
             de_ref, dgp_ref, dz_ref, loss_ref, dbpg_ref, dg_ref, db_ref):
        @pl.when(pl.program_id(0) == 0)
        def _():
            for r in (loss_ref, dbpg_ref, dg_ref, db_ref):
                r[...] = jnp.zeros_like(r)

        x2 = x2_ref[...]
        gate = _sig(_dot(x2, wpg_ref[...]) + bpg_ref[...])
        e = _dot(p_ref[...], wpp_ref[...])
        err = x2 + gate * e - t_ref[...]
        loss_ref[...] += _colsum(err * err)
        dy = err * (1.0 / D_MODEL)
        de_ref[...] = (dy * gate).astype(de_ref.dtype)
        dgp = dy * e * gate * (1.0 - gate)
        dgp_ref[...] = dgp.astype(dgp_ref.dtype)
        dbpg_ref[...] += _colsum(dgp)
        dx2 = dy + _dot(dgp, wpg_ref[...], NT)
        _, xhat, rstd = _ln_fwd(z_ref[...], g_ref[...], b_ref[...])
        dg_ref[...] += _colsum(dx2 * xhat)
        db_ref[...] += _colsum(dx2)
        dz_ref[...] = _ln_bwd(dx2, xhat, rstd, g_ref[...])

    vec = _full((1, D_MODEL))
    return pl.pallas_call(
        body, name="ple_loss_ln2_bwd", grid=(T // tm,),
        in_specs=[_row(tm, D_MODEL), _row(tm, D_MODEL), _row(tm, PLE_DIM), _row(tm, D_MODEL),
                  _full(wpg.shape), vec, _full(wpp.shape), vec, vec],
        out_specs=[_row(tm, D_MODEL)] * 3 + [vec] * 4,
        out_shape=[_sds((T, D_MODEL), _MXU)] * 2 + [_sds((T, D_MODEL))] + [_sds((1, D_MODEL))] * 4,
        compiler_params=_cp("arbitrary"))(x2, z2, p, tgt, wpg, bpg, wpp, ln_g, ln_b)


def _ffn_bwd_ln1_bwd(a_pre, b_pre, z1, dz2, wg, wu, wd, ln_g, ln_b, tm, comms=None):
    T = z1.shape[0]

    def body(a_ref, bb_ref, z_ref, dz2_ref, wg_ref, wu_ref, wd_ref, g_ref, b_ref,
             da_ref, dbb_ref, dz1_ref, dg_ref, db_ref):
        @pl.when(pl.program_id(0) == 0)
        def _():
            dg_ref[...] = jnp.zeros_like(dg_ref)
            db_ref[...] = jnp.zeros_like(db_ref)

        dz2 = dz2_ref[...]
        a = a_ref[...].astype(F32)
        bb = bb_ref[...].astype(F32)
        sa = _sig(a)
        act = a * sa
        dh = _dot(dz2, wd_ref[...], NT)
        da = (dh * bb * _dsilu(a, sa)).astype(da_ref.dtype)
        dbb = (dh * act).astype(dbb_ref.dtype)
        da_ref[...] = da
        dbb_ref[...] = dbb
        dx1 = ALPHA * dz2 + _dot(da, wg_ref[...], NT) + _dot(dbb, wu_ref[...], NT)
        _, xhat, rstd = _ln_fwd(z_ref[...], g_ref[...], b_ref[...])
        dg_ref[...] += _colsum(dx1 * xhat)
        db_ref[...] += _colsum(dx1)
        dz1_ref[...] = _ln_bwd(dx1, xhat, rstd, g_ref[...])

    vec = _full((1, D_MODEL))
    return _hosted_call(
        body, comms, name="ffn_bwd_ln1_bwd", grid=(T // tm,),
        in_specs=[_row(tm, D_FF)] * 2 + [_row(tm, D_MODEL)] * 2 + [_full(wg.shape), _full(wu.shape), _full(wd.shape), vec, vec],
        out_specs=[_row(tm, D_FF)] * 2 + [_row(tm, D_MODEL), vec, vec],
        out_shape=[_sds((T, D_FF), _MXU)] * 2 + [_sds((T, D_MODEL)), _sds((1, D_MODEL)), _sds((1, D_MODEL))],
        scratch_shapes=[], args=(a_pre, b_pre, z1, dz2, wg, wu, wd, ln_g, ln_b))


def _out_proj_bwd(dz1, u, o_hg, h_ml, g_hg, g_ml, w_out, tm):
    T = dz1.shape[0]

    def body(dz_ref, hgate_ref, mo_ref, ohg_ref, hml_ref, ghg_ref, gml_ref, w_ref,
             dohg_ref, dhml_ref, dhgate_ref, dmo_ref, dghg_ref, dgml_ref):
        @pl.when(pl.program_id(0) == 0)
        def _():
            dghg_ref[...] = jnp.zeros_like(dghg_ref)
            dgml_ref[...] = jnp.zeros_like(dgml_ref)

        dm = _dot(dz_ref[...], w_ref[...], NT)

        def half(dmh, o, gvec, gate_val, dgate_fac, do_ref, dgate_ref, dgvec_ref):
            r, rs = _head_norm(o)
            dgate_ref[...] = (dmh * r * gvec * dgate_fac).astype(dgate_ref.dtype)
            dn = dmh * gate_val
            dgvec_ref[...] += _colsum(dn * r)
            dr = dn * gvec
            parts = []
            for h in range(HEADS):
                sl = slice(128 * h, 128 * (h + 1))
                parts.append(rs[h] * (dr[:, sl] - r[:, sl] * jnp.mean(dr[:, sl] * r[:, sl], axis=-1, keepdims=True)))
            do_ref[...] = jnp.concatenate(parts, axis=1)

        hg = hgate_ref[...]
        shg = _sig(hg)
        half(dm[:, :HALF], ohg_ref[...], ghg_ref[...], hg * shg, _dsilu(hg, shg), dohg_ref, dhgate_ref, dghg_ref)
        smo = _sig(mo_ref[...])
        half(dm[:, HALF:], hml_ref[...], gml_ref[...], smo, smo * (1.0 - smo), dhml_ref, dmo_ref, dgml_ref)

    vec = _full((1, HALF))
    return pl.pallas_call(
        body, name="out_proj_bwd", grid=(T // tm,),
        in_specs=[_row(tm, D_MODEL), _row(tm, HALF, C_HGATE // 4), _row(tm, HALF, C_MO // 4),
                  _row(tm, HALF), _row(tm, HALF), vec, vec, _full(w_out.shape)],
        out_specs=[_row(tm, HALF)] * 4 + [vec, vec],
        out_shape=[_sds((T, HALF))] * 2 + [_sds((T, HALF), _MXU)] * 2 + [_sds((1, HALF))] * 2,
        compiler_params=_cp("arbitrary"))(dz1, u, u, o_hg, h_ml, g_hg, g_ml, w_out)


def _conv_bwd(u, pre, dqkc, cw, tm):
    T = u.shape[0]
    hb = tm // 8
    nb = T // 8

    def body(x_ref, xh_ref, pre_ref, preh_ref, d_ref, dh_ref, w_ref, dx_ref, dw_ref, db_ref):
        i = pl.program_id(0)

        @pl.when(i == 0)
        def _():
            dw_ref[...] = jnp.zeros_like(dw_ref)
            db_ref[...] = jnp.zeros_like(db_ref)

        def dpre_of(pre, d):
            return d * _dsilu(pre, _sig(pre))

        rowi = _iota((8, HALF), 0)
        dpre = dpre_of(pre_ref[...], d_ref[...])
        dpre_next = jnp.where(i < pl.num_programs(0) - 1, dpre_of(preh_ref[...], dh_ref[...]), 0.0)
        x = x_ref[...]
        xprev = jnp.where(i > 0, xh_ref[...], 0.0)
        dx = dpre * w_ref[3:4, :]
        db_ref[...] += _colsum(dpre)
        dws = [None] * 4
        dws[3] = _colsum(dpre * x)
        for j in (1, 2, 3):
            dx = dx + _shift_rows_up(dpre, dpre_next, j, rowi) * w_ref[3 - j:4 - j, :]
            dws[3 - j] = _colsum(dpre * _shift_rows(x, xprev, j, rowi))
        dx_ref[...] = dx.astype(dx_ref.dtype)
        dw_ref[...] += jnp.concatenate(dws, axis=0)

    cur = lambda blk: pl.BlockSpec((tm, HALF), lambda i, blk=blk: (i, blk))
    nxt = pl.BlockSpec((8, HALF), lambda i: (jnp.minimum((i + 1) * hb, nb - 1), 0))
    return pl.pallas_call(
        body, name="conv_bwd", grid=(T // tm,),
        in_specs=[cur(C_MQK // 4), pl.BlockSpec((8, HALF), lambda i: (jnp.maximum(i * hb - 1, 0), C_MQK // 4)),
                  cur(0), nxt, cur(0), nxt, _full(cw.shape)],
        out_specs=[cur(0), _full((4, HALF)), _full((1, HALF))],
        out_shape=[_sds((T, HALF), _MXU), _sds((4, HALF)), _sds((1, HALF))],
        compiler_params=_cp("arbitrary"))(u, u, pre, pre, dqkc, dqkc, cw)


def _du_specs(rows):
    return [pl.BlockSpec((rows, w), lambda i: (i, 0)) for w in DU_WIDTHS]


def _in_proj_bwd(dz1, du_parts, w, tm, comms=None):
    T = dz1.shape[0]

    def body(dz_ref, *refs):
        du = jnp.concatenate([r[...] for r in refs[:8]], axis=1)
        refs[9][...] = ALPHA * dz_ref[...] + _dot(du, refs[8][...], NT)

    (dx,), got = _hosted_call(
        body, comms, name="in_proj_bwd", grid=(T // tm,),
        in_specs=[_row(tm, D_MODEL)] + _du_specs(tm) + [_full(w.shape)],
        out_specs=[_row(tm, D_MODEL)], out_shape=[_sds((T, D_MODEL))], scratch_shapes=[], args=(dz1, *du_parts, w))
    return dx, got


def _wgrad(a, b, name, tm, tn, tk):
    T, M = a.shape
    N = b.shape[1]
    tm, tn, tk = min(tm, M), min(tn, N), min(tk, T)
    nk = T // tk

    def body(a_ref, b_ref, o_ref, acc_ref):
        kk = pl.program_id(2)

        @pl.when(kk == 0)
        def _():
            acc_ref[...] = jnp.zeros_like(acc_ref)

        acc_ref[...] += _dot(a_ref[...], b_ref[...], TN)

        @pl.when(kk == nk - 1)
        def _():
            o_ref[...] = acc_ref[...].astype(o_ref.dtype)

    return pl.pallas_call(
        body, name=name, grid=(M // tm, N // tn, nk),
        in_specs=[pl.BlockSpec((tk, tm), lambda i, j, kk: (kk, i)), pl.BlockSpec((tk, tn), lambda i, j, kk: (kk, j))],
        out_specs=pl.BlockSpec((tm, tn), lambda i, j, kk: (i, j)), out_shape=_sds((M, N), _MXU),
        scratch_shapes=[pltpu.VMEM((tm, tn), F32)],
        compiler_params=_cp("parallel", "parallel", "arbitrary"))(a, b)


def _wgrad_w_in(x, du_parts, tk):
    T = x.shape[0]
    tk = min(tk, T)
    nk = T // tk

    def body(a_ref, *refs):
        o_ref, cs_ref, acc_ref = refs[8:]
        kk = pl.program_id(0)

        @pl.when(kk == 0)
        def _():
            acc_ref[...] = jnp.zeros_like(acc_ref)
            cs_ref[...] = jnp.zeros_like(cs_ref)

        du = jnp.concatenate([r[...] for r in refs[:8]], axis=1)
        acc_ref[...] += _dot(a_ref[...], du, TN)
        cs_ref[...] += _colsum(du.astype(F32))

        @pl.when(kk == nk - 1)
        def _():
            o_ref[...] = acc_ref[...].astype(o_ref.dtype)

    return pl.pallas_call(
        body, name="wgrad_w_in", grid=(nk,),
        in_specs=[pl.BlockSpec((tk, D_MODEL), lambda kk: (kk, 0))] + _du_specs(tk),
        out_specs=[_full((D_MODEL, PROJ_WP)), _full((1, PROJ_WP))],
        out_shape=[_sds((D_MODEL, PROJ_WP), _MXU), _sds((1, PROJ_WP))],
        scratch_shapes=[pltpu.VMEM((D_MODEL, PROJ_WP), F32)], compiler_params=_cp("arbitrary"))(x, *du_parts)


W_IN_S, FF_S, OUT_S, PP_S = PROJ_W // N_DEV, D_FF // N_DEV, D_MODEL // N_DEV, D_MODEL // N_DEV
LATE = ("w_ffn_gate", "w_ffn_up", "w_out", "w_ffn_down", "ple_w_gate", "ple_w_proj")
BIG = ("w_in",) + LATE


def _split_cols(a, n):
    return a.reshape(a.shape[0], N_DEV, n).transpose(1, 0, 2)


def _join_cols(a):
    return a.transpose(1, 0, 2).reshape(a.shape[1], -1)


def _step(x, p, tgt, w_in, b_in, lb_logits, conv_w, conv_b, g_hg, g_ml, ln1_g, ln1_b, ln2_g, ln2_b, bpg, late,
          distributed):
    T = x.shape[0]
    tm, tf = min(ROWS, T), min(ROWS_FFN, T)
    gather = lambda *names: [_GatherTwoLevel([late[n] for n in names])] if distributed else None
    scatter = lambda *arrs: [_Comm("scatter", list(arrs))] if distributed else None
    rows = lambda a, n: a.reshape(N_DEV, n, D_MODEL)
    u, got1 = _in_proj(x, w_in, b_in, tm, gather("w_out", "ple_w_gate", "ple_w_proj"))
    pre, qkc = _conv_fwd(u, conv_w, conv_b, tm)
    (o_hg, hg_states), got2 = _hgrn2_fwd(u, lb_logits, gather("w_ffn_gate", "w_ffn_up"))
    (h_ml, cst, nst, mst), got3 = _mlstm_fwd(qkc, u, gather("w_ffn_down"))
    if distributed:
        w_out, wpg, wpp = got1[0][0].reshape(D_MODEL, D_MODEL), got1[0][1].reshape(D_MODEL, D_MODEL), _join_cols(got1[0][2])
        wg, wu, wd = _join_cols(got2[0][0]), _join_cols(got2[0][1]), got3[0][0].reshape(D_FF, D_MODEL)
    else:
        w_out, wg, wu, wd, wpg, wpp = (late[n] for n in ("w_out", "w_ffn_gate", "w_ffn_up", "w_ffn_down", "ple_w_gate", "ple_w_proj"))
    m_in, z1, x1 = _out_proj_ln(x, u, o_hg, h_ml, g_hg, g_ml, w_out, ln1_g, ln1_b, tm)
    z2, x2, a_pre, b_pre, hh = _ffn_ln(x1, wg, wu, wd, ln2_g, ln2_b, tf)
    de, dgp, dz2, loss_vec, d_bpg, d_ln2g, d_ln2b = _ple_loss_ln2_bwd(x2, z2, p, tgt, wpg, bpg, wpp, ln2_g, ln2_b, tm)
    big = dict(ple_w_gate=_wgrad(x2, dgp, "wgrad_ple_gate", 512, D_MODEL, 1024),
               ple_w_proj=_wgrad(p, de, "wgrad_ple_proj", 512, D_MODEL, 1024))
    (da, dbb, dz1, d_ln1g, d_ln1b), r1 = _ffn_bwd_ln1_bwd(
        a_pre, b_pre, z1, dz2, wg, wu, wd, ln1_g, ln1_b, tf, scatter(rows(big["ple_w_gate"], OUT_S), _split_cols(big["ple_w_proj"], PP_S)))
    big.update(
        w_ffn_gate=_wgrad(x1, da, "wgrad_ffn_gate", 512, D_FF, 1024),
        w_ffn_up=_wgrad(x1, dbb, "wgrad_ffn_up", 512, D_FF, 1024),
        w_ffn_down=_wgrad(hh, dz2, "wgrad_ffn_down", D_FF, D_MODEL, 1024),
        w_out=_wgrad(m_in, dz1, "wgrad_w_out", 512, D_MODEL, 1024))
    d_ohg, d_hml, d_hgate, d_mo, d_ghg, d_gml = _out_proj_bwd(dz1, u, o_hg, h_ml, g_hg, g_ml, w_out, tm)
    (d_hq, d_hf, d_hv, d_lb), r2 = _hgrn2_bwd(
        u, lb_logits, d_ohg, hg_states,
        scatter(_split_cols(big["w_ffn_gate"], FF_S), _split_cols(big["w_ffn_up"], FF_S), rows(big["w_ffn_down"], FF_S),
                rows(big["w_out"], OUT_S)))
    d_qkc, d_mv, d_gates = _mlstm_bwd(qkc, u, d_hml, cst, nst, mst)
    d_mqk, d_convw, d_convb = _conv_bwd(u, pre, d_qkc, conv_w, tm)
    du_parts = [d_hq, d_hf, d_hv, d_hgate, d_mqk, d_mv, d_mo, d_gates]
    big["w_in"], d_bin = _wgrad_w_in(x, du_parts, 256)
    small = dict(b_in=d_bin[:, :PROJ_W], hg_lb_logits=d_lb, ml_conv_w=d_convw, ml_conv_b=d_convb, hg_norm_g=d_ghg,
                 ml_norm_g=d_gml, ln1_g=d_ln1g, ln1_b=d_ln1b, ln2_g=d_ln2g, ln2_b=d_ln2b, ple_b_gate=d_bpg)
    last = [_Comm("scatter", [_split_cols(big["w_in"][:, :PROJ_W], W_IN_S)]),
            _Comm("gather", [loss_vec] + [small[n] for n in SMALL])] if distributed else None
    dx, r3 = _in_proj_bwd(dz1, du_parts, w_in, tm, last)
    gathered_small = None
    if distributed:
        big = dict(ple_w_gate=r1[0][0], ple_w_proj=r1[0][1], w_ffn_gate=r2[0][0], w_ffn_up=r2[0][1],
                   w_ffn_down=r2[0][2], w_out=r2[0][3], w_in=r3[0][0])
        gathered_small = r3[1]
    return loss_vec, dx, big, small, gathered_small


SMALL = ("b_in", "hg_lb_logits", "ml_conv_w", "ml_conv_b", "hg_norm_g", "ml_norm_g", "ln1_g", "ln1_b", "ln2_g", "ln2_b",
         "ple_b_gate")


def _padc(a, n):
    return jnp.pad(a, [(0, 0)] * (a.ndim - 1) + [(0, n - a.shape[-1])])


def _adamw(w, g, m, v):
    m = B1 * m + (1.0 - B1) * g
    v = B2 * v + (1.0 - B2) * jnp.square(g)
    m_hat = m / (1.0 - B1 ** STEP)
    v_hat = v / (1.0 - B2 ** STEP)
    return -LR * (m_hat / (jnp.sqrt(v_hat) + EPS) + WD * w), m, v


def _sum_slabs(ref):
    g = ref[0].astype(F32)
    for j in range(1, N_DEV):
        g = g + ref[j].astype(F32)
    return g


def _adamw_matrix(rb, w, m, v, name):
    R, C = w.shape
    tr = 256 if R % 256 == 0 else R

    def body(rb_ref, w_ref, m_ref, v_ref, g_ref, d_ref, m2_ref, v2_ref):
        g = _sum_slabs(rb_ref)
        g_ref[...] = g
        d_ref[...], m2_ref[...], v2_ref[...] = _adamw(w_ref[...], g, m_ref[...], v_ref[...])

    blk = pl.BlockSpec((tr, C), lambda i: (i, 0))
    return pl.pallas_call(
        body, name=name, grid=(R // tr,),
        in_specs=[pl.BlockSpec((N_DEV, tr, C), lambda i: (0, i, 0)), blk, blk, blk],
        out_specs=[blk] * 4, out_shape=[_sds((R, C))] * 4, compiler_params=_cp("parallel"))(rb, w, m, v)


def _adamw_small(loss_g, gs, ws, ms, vs):
    n = len(ws)

    def body(*refs):
        loss_ref, g_refs, w_refs, m_refs, v_refs = refs[0], refs[1:1 + n], refs[1 + n:1 + 2 * n], refs[1 + 2 * n:1 + 3 * n], refs[1 + 3 * n:1 + 4 * n]
        outs = refs[1 + 4 * n:]
        outs[0][...] = (0.5 / D_MODEL) * jnp.sum(_sum_slabs(loss_ref), keepdims=True)
        for i in range(n):
            g = _sum_slabs(g_refs[i])
            outs[1 + i][...] = g
            outs[1 + n + i][...], outs[1 + 2 * n + i][...], outs[1 + 3 * n + i][...] = _adamw(
                w_refs[i][...], g, m_refs[i][...], v_refs[i][...])

    res = pl.pallas_call(
        body, name="adamw_small", out_shape=[_sds((1, 1))] + [_sds(w.shape) for w in ws] * 4)(loss_g, *gs, *ws, *ms, *vs)
    return res[0], [res[1 + k * n:1 + (k + 1) * n] for k in range(4)]


WEIGHTS = ("w_in", "b_in", "hg_lb_logits", "ml_conv_w", "ml_conv_b", "hg_norm_g", "ml_norm_g", "w_out", "ln1_g", "ln1_b",
           "w_ffn_gate", "w_ffn_up", "w_ffn_down", "ln2_g", "ln2_b", "ple_w_proj", "ple_w_gate", "ple_b_gate")
CONV_S = HALF // N_DEV


def kernel(x, p, w_in, b_in, hg_lb_logits, ml_conv_w, ml_conv_b, hg_norm_g, ml_norm_g, w_out, ln1_g, ln1_b, w_ffn_gate, w_ffn_up, w_ffn_down, ln2_g, ln2_b, ple_w_proj, ple_w_gate, ple_b_gate, loss_target, m_w_in, m_b_in, m_hg_lb_logits, m_ml_conv_w, m_ml_conv_b, m_hg_norm_g, m_ml_norm_g, m_w_out, m_ln1_g, m_ln1_b, m_w_ffn_gate, m_w_ffn_up, m_w_ffn_down, m_ln2_g, m_ln2_b, m_ple_w_proj, m_ple_w_gate, m_ple_b_gate, v_w_in, v_b_in, v_hg_lb_logits, v_ml_conv_w, v_ml_conv_b, v_hg_norm_g, v_ml_norm_g, v_w_out, v_ln1_g, v_ln1_b, v_w_ffn_gate, v_w_ffn_up, v_w_ffn_down, v_ln2_g, v_ln2_b, v_ple_w_proj, v_ple_w_gate, v_ple_b_gate):
    args = locals()
    me = 4 * lax.axis_index("x") + 2 * lax.axis_index("y") + lax.axis_index("c")
    shapes = {n: args[n].shape for n in WEIGHTS}
    drop = lambda n, a: a[0] if n in BIG or n == "ml_conv_w" else a
    W = {n: drop(n, args[n]) for n in WEIGHTS}
    M = {n: drop(n, args["m_" + n]) for n in WEIGHTS}
    V = {n: drop(n, args["v_" + n]) for n in WEIGHTS}

    g_in, g_conv = _gather_two_level(
        [W["w_in"].astype(_MXU), jnp.pad(W["ml_conv_w"], ((0, 4), (0, 128 - CONV_S)))], "gather_w_in")
    w_in_full = _padc(_join_cols(g_in), PROJ_WP)
    conv_full = _join_cols(g_conv[:, :4, :CONV_S])

    _, dx, big, _, sg = _step(
        x[0], p[0, 0], loss_target[0], w_in_full, _padc(b_in, PROJ_WP), hg_lb_logits, conv_full, ml_conv_b,
        hg_norm_g, ml_norm_g, ln1_g, ln1_b, ln2_g, ln2_b, ple_b_gate, {n: W[n].astype(_MXU) for n in LATE}, True)

    upd = {n: _adamw_matrix(big[n], W[n], M[n], V[n], "adamw_" + n) for n in BIG}
    sg = dict(zip(SMALL, sg[1:]), loss=sg[0])
    sg["ml_conv_w"] = lax.dynamic_slice(sg["ml_conv_w"], (0, 0, me * CONV_S), (N_DEV, 4, CONV_S))
    loss, small_upd = _adamw_small(sg["loss"], *[[d[n] for n in SMALL] for d in (sg, W, M, V)])

    outs = []
    for kind in range(4):
        smalls = dict(zip(SMALL, small_upd[kind]))
        for n in WEIGHTS:
            outs.append((upd[n][kind] if n in BIG else smalls[n]).reshape(shapes[n]))
    return (loss.reshape(()), dx.reshape(x.shape), *outs)
```

```python
import jax
import jax.numpy as jnp
from jax import lax
from jax.experimental import pallas as pl
from jax.experimental.pallas import tpu as pltpu

F32 = jnp.float32
_MXU = jnp.bfloat16

D_MODEL = 1024
CHUNK = 64
SUB = 16
PLE_DIM = 256
HEADS = 4
ML_DQK = 64
HALF = 512
D_FF = 2816
PROJ_W = 3592
PROJ_WP = 3712
ALPHA = float(2 ** 0.25)
LN_EPS = 1e-5
RMS_EPS = 1e-6
ML_SCALE = ML_DQK ** -0.5
N_DEV = 8
LR, B1, B2, EPS, WD, STEP = 0.001, 0.9, 0.999, 1e-08, 0.01, 10
NEG = -1e30

C_HQ, C_HF, C_HV, C_HGATE, C_MQK, C_MV, C_MO, C_GATES = 0, 4, 8, 12, 16, 20, 24, 28
DU_WIDTHS = (HALF,) * 7 + (128,)

VMEM_LIMIT = 52 * 1024 * 1024
GC = 8
ROWS = 512
ROWS_FFN = 256

NN = (((1,), (0,)), ((), ()))
NT = (((1,), (1,)), ((), ()))
TN = (((0,), (0,)), ((), ()))
BNT = (((2,), (2,)), ((0,), (0,)))
BNN = (((2,), (1,)), ((0,), (0,)))
BTN = (((1,), (1,)), ((0,), (0,)))


def _dot(a, b, dims=NN):
    return lax.dot_general(a.astype(_MXU), b.astype(_MXU), dims, preferred_element_type=F32)


def _dotx(a, b, dims=NN):
    return lax.dot_general(a, b, dims, precision=lax.Precision.HIGHEST, preferred_element_type=F32)


def _sig(x):
    return jax.nn.sigmoid(x)


def _cp(*sem):
    return pltpu.CompilerParams(dimension_semantics=sem, vmem_limit_bytes=VMEM_LIMIT)


def _row(tm, c, blk=0):
    return pl.BlockSpec((tm, c), lambda i, blk=blk: (i, blk))


def _full(shape):
    nd = len(shape)
    return pl.BlockSpec(tuple(shape), lambda *_, nd=nd: (0,) * nd)


def _sds(shape, dtype=F32):
    return jax.ShapeDtypeStruct(tuple(shape), dtype)


def _iota(shape, axis):
    return lax.broadcasted_iota(jnp.int32, shape, axis)


def _colsum(x):
    return jnp.sum(x, axis=0, keepdims=True)


def _rowsum(x):
    return jnp.sum(x, axis=1, keepdims=True)


def _ln_fwd(z, g, b):
    mu = jnp.mean(z, axis=-1, keepdims=True)
    zc = z - mu
    var = jnp.mean(zc * zc, axis=-1, keepdims=True)
    rstd = lax.rsqrt(var + LN_EPS)
    xhat = zc * rstd
    return xhat * g + b, xhat, rstd


def _ln_bwd(dy, xhat, rstd, g):
    dxh = dy * g
    m1 = jnp.mean(dxh, axis=-1, keepdims=True)
    m2 = jnp.mean(dxh * xhat, axis=-1, keepdims=True)
    return rstd * (dxh - m1 - xhat * m2)


def _dsilu(x, s):
    return s * (1.0 + x * (1.0 - s))


MESH = pl.DeviceIdType.MESH
ANY = pl.BlockSpec(memory_space=pl.ANY)


def _flip(v, bit):
    return 1 - v if bit else v


class _Comm:
    def __init__(self, kind, srcs):
        self.kind, self.srcs, self.n = kind, list(srcs), len(srcs)

    def out_shape(self):
        lead = (N_DEV,) if self.kind == "gather" else ()
        return [jax.ShapeDtypeStruct(lead + s.shape, s.dtype) for s in self.srcs]

    def scratch(self):
        return [pltpu.SemaphoreType.DMA((7 * self.n,)), pltpu.SemaphoreType.DMA((7 * self.n,)),
                pltpu.SemaphoreType.DMA((self.n,))]

    def copies(self, srcs, dsts, send_sems, recv_sems, local_sems):
        x, y, c = lax.axis_index("x"), lax.axis_index("y"), lax.axis_index("c")
        me = 4 * x + 2 * y + c
        pick = (lambda s, j: s) if self.kind == "gather" else (lambda s, j: s.at[j])
        out = []
        for i, (s, d) in enumerate(zip(srcs, dsts)):
            out.append(pltpu.make_async_copy(pick(s, me), d.at[me], local_sems.at[i]))
            for k in range(1, N_DEV):
                px, py, pc = _flip(x, k & 4), _flip(y, k & 2), _flip(c, k & 1)
                out.append(pltpu.make_async_remote_copy(
                    src_ref=pick(s, 4 * px + 2 * py + pc), dst_ref=d.at[me], send_sem=send_sems.at[7 * i + k - 1],
                    recv_sem=recv_sems.at[7 * i + k - 1], device_id=(px, py, pc), device_id_type=MESH))
        return out

    def start(self, *refs):
        for cp in self.copies(*refs):
            cp.start()

    def mid(self, *refs):
        pass

    def finish(self, *refs):
        for cp in self.copies(*refs):
            cp.wait()


class _GatherTwoLevel(_Comm):
    def __init__(self, srcs):
        super().__init__("gather", srcs)

    def _parts(self, srcs, dsts, send_sems, recv_sems, local_sems):
        x, y, c = lax.axis_index("x"), lax.axis_index("y"), lax.axis_index("c")
        me, sibling = (x, y, c), (x, y, 1 - c)
        chips = [(1 - x, y), (x, 1 - y), (1 - x, 1 - y)]

        def copy(i, k, block, to, own=False):
            slab = dsts[i].at[4 * block[0] + 2 * block[1] + block[2]]
            return pltpu.make_async_remote_copy(
                src_ref=srcs[i] if own else slab, dst_ref=slab, send_sem=send_sems.at[7 * i + k],
                recv_sem=recv_sems.at[7 * i + k], device_id=to, device_id_type=MESH)

        n = range(self.n)
        mine = [pltpu.make_async_copy(srcs[i], dsts[i].at[4 * x + 2 * y + c], local_sems.at[i]) for i in n]
        first = [copy(i, 0, me, sibling, own=True) for i in n]
        first += [copy(i, 1 + j, me, (*chip, c), own=True) for j, chip in enumerate(chips) for i in n]
        over_ici = [copy(i, 1 + j, (*chip, c), me) for j, chip in enumerate(chips) for i in n]
        passed = [copy(i, 4 + j, (*chip, c), sibling) for j, chip in enumerate(chips) for i in n]
        from_sibling = [copy(i, 0, sibling, me) for i in n]
        from_sibling += [copy(i, 4 + j, (*chip, 1 - c), me) for j, chip in enumerate(chips) for i in n]
        return mine, first, over_ici, passed, from_sibling

    def start(self, *refs):
        mine, first, _, _, _ = self._parts(*refs)
        for cp in mine + first:
            cp.start()

    def mid(self, *refs):
        _, _, over_ici, passed, _ = self._parts(*refs)
        for arrived, onward in zip(over_ici, passed):
            arrived.wait_recv()
            onward.start()

    def finish(self, *refs):
        mine, first, _, passed, from_sibling = self._parts(*refs)
        for cp in from_sibling:
            cp.wait_recv()
        for cp in first + passed:
            cp.wait_send()
        for cp in mine:
            cp.wait()


def _hosted_call(body, comms, *, name, grid, in_specs, out_specs, out_shape, scratch_shapes, args):
    comms = list(comms or [])
    if not comms:
        res = pl.pallas_call(body, name=name, grid=grid, in_specs=in_specs, out_specs=out_specs, out_shape=out_shape,
                             scratch_shapes=scratch_shapes, compiler_params=_cp("arbitrary"))(*args)
        return list(res), []
    n_in, n_out, n_sc, nc = len(in_specs), len(out_specs), len(scratch_shapes), sum(cm.n for cm in comms)
    last = grid[0] - 1

    def hosted(*refs):
        ins, csrc = refs[:n_in], refs[n_in:n_in + nc]
        o0 = n_in + nc
        outs, cdst = refs[o0:o0 + n_out], refs[o0 + n_out:o0 + n_out + nc]
        s0 = o0 + n_out + nc
        scr, sems = refs[s0:s0 + n_sc], refs[s0 + n_sc:]

        def phase(which):
            o = 0
            for j, cm in enumerate(comms):
                getattr(cm, which)(csrc[o:o + cm.n], cdst[o:o + cm.n], *sems[3 * j:3 * j + 3])
                o += cm.n

        i = pl.program_id(0)

        @pl.when(i == 0)
        def _():
            phase("start")

        body(*ins, *outs, *scr)

        @pl.when(i == (2 * last) // 3)
        def _():
            phase("mid")

        @pl.when(i == last)
        def _():
            phase("finish")

    res = pl.pallas_call(
        hosted, name=name, grid=grid, in_specs=list(in_specs) + [ANY] * nc, out_specs=list(out_specs) + [ANY] * nc,
        out_shape=list(out_shape) + [s for cm in comms for s in cm.out_shape()],
        scratch_shapes=list(scratch_shapes) + [s for cm in comms for s in cm.scratch()],
        compiler_params=_cp("arbitrary"))(*args, *[a for cm in comms for a in cm.srcs])
    got, o = [], n_out
    for cm in comms:
        got.append(list(res[o:o + cm.n]))
        o += cm.n
    return list(res[:n_out]), got


def _gather_two_level(blocks, name):
    n = len(blocks)

    def body(*refs):
        x_refs, out_refs = refs[:n], refs[n:2 * n]
        send_sems, recv_sems, local_sems = refs[2 * n:]
        x, y, c = lax.axis_index("x"), lax.axis_index("y"), lax.axis_index("c")
        me, sibling = (x, y, c), (x, y, 1 - c)
        chips = [(1 - x, y), (x, 1 - y), (1 - x, 1 - y)]

        def copy(i, k, block, to, own=False):
            slab = out_refs[i].at[4 * block[0] + 2 * block[1] + block[2]]
            return pltpu.make_async_remote_copy(
                src_ref=x_refs[i] if own else slab, dst_ref=slab, send_sem=send_sems.at[7 * i + k],
                recv_sem=recv_sems.at[7 * i + k], device_id=to, device_id_type=MESH)

        mine = [pltpu.make_async_copy(x_refs[i], out_refs[i].at[4 * x + 2 * y + c], local_sems.at[i]) for i in range(n)]
        for cp in mine:
            cp.start()
        first = [copy(i, 0, me, sibling, own=True) for i in range(n)]
        first += [copy(i, 1 + j, me, (*chip, c), own=True) for j, chip in enumerate(chips) for i in range(n)]
        for cp in first:
            cp.start()
        passed = []
        for j, chip in enumerate(chips):
            for i in range(n):
                copy(i, 1 + j, (*chip, c), me).wait_recv()
                passed.append(copy(i, 4 + j, (*chip, c), sibling))
                passed[-1].start()
        for i in range(n):
            copy(i, 0, sibling, me).wait_recv()
            for j, chip in enumerate(chips):
                copy(i, 4 + j, (*chip, 1 - c), me).wait_recv()
        for cp in first + passed:
            cp.wait_send()
        for cp in mine:
            cp.wait()

    return pl.pallas_call(
        body, name=name, out_shape=[jax.ShapeDtypeStruct((N_DEV,) + b.shape, b.dtype) for b in blocks],
        in_specs=[ANY] * n, out_specs=[ANY] * n,
        scratch_shapes=[pltpu.SemaphoreType.DMA((7 * n,)), pltpu.SemaphoreType.DMA((7 * n,)),
                        pltpu.SemaphoreType.DMA((n,))])(*blocks)


def _in_proj(x, w, b, cw, cb, tm, comms=None):
    T = x.shape[0]

    def body(x_ref, w_ref, b_ref, cw_ref, cb_ref, o_ref, pre_ref, act_ref, halo_sc):
        @pl.when(pl.program_id(0) == 0)
        def _():
            halo_sc[...] = jnp.zeros_like(halo_sc)

        o = _dot(x_ref[...], w_ref[...]) + b_ref[...]
        o_ref[...] = o
        xc = o[:, 128 * C_MQK:128 * C_MQK + HALF]
        halo = halo_sc[...]
        rowi = _iota((8, HALF), 0)
        acc = xc * cw_ref[3:4, :] + cb_ref[...]
        for j in (1, 2, 3):
            acc = acc + _shift_rows(xc, halo, j, rowi) * cw_ref[3 - j:4 - j, :]
        pre_ref[...] = acc
        act_ref[...] = acc * _sig(acc)
        halo_sc[...] = xc[tm - 8:]

    return _hosted_call(
        body, comms, name="in_proj", grid=(T // tm,),
        in_specs=[_row(tm, D_MODEL), _full(w.shape), _full(b.shape), _full(cw.shape), _full(cb.shape)],
        out_specs=[_row(tm, PROJ_WP), _row(tm, HALF), _row(tm, HALF)],
        out_shape=[_sds((T, PROJ_WP)), _sds((T, HALF)), _sds((T, HALF))],
        scratch_shapes=[pltpu.VMEM((8, HALF), F32)], args=(x, w, b, cw, cb))


def _shift_rows(x, halo, j, rowi):
    r = pltpu.roll(x, j, 0)
    top = jnp.where(rowi < j, pltpu.roll(halo, j, 0), r[:8])
    return jnp.concatenate([top, r[8:]], axis=0)


def _shift_rows_up(x, halo, j, rowi):
    n = x.shape[0]
    r = pltpu.roll(x, n - j, 0)
    bot = jnp.where(rowi >= 8 - j, pltpu.roll(halo, 8 - j, 0), r[n - 8:])
    return jnp.concatenate([r[:n - 8], bot], axis=0)


def _bdot(a, b, dims):
    return lax.dot_general(a.astype(_MXU), b.astype(_MXU), dims, preferred_element_type=F32)


def _bdotx(a, b, dims):
    return lax.dot_general(a, b, dims, precision=lax.Precision.HIGHEST, preferred_element_type=F32)


def _heads_to_batch(x, w):
    G = x.shape[0] // CHUNK
    x3 = x.reshape(G, CHUNK, HEADS * w)
    return jnp.stack([x3[:, :, w * h:w * (h + 1)] for h in range(HEADS)], axis=1).reshape(G * HEADS, CHUNK, w)


def _batch_to_heads(x3):
    B, _, w = x3.shape
    x4 = x3.reshape(B // HEADS, HEADS, CHUNK, w)
    return jnp.concatenate([x4[:, h] for h in range(HEADS)], axis=-1).reshape(B // HEADS * CHUNK, HEADS * w)


def _chunk_cumsum(x, rowmod, reverse=False):
    R = x.shape[0]
    for sh in (1, 2, 4, 8, 16, 32):
        if reverse:
            x = x + jnp.where(rowmod < CHUNK - sh, pltpu.roll(x, R - sh, 0), 0.0)
        else:
            x = x + jnp.where(rowmod >= sh, pltpu.roll(x, sh, 0), 0.0)
    return x


def _lane_col(x, c, lane):
    return _rowsum(jnp.where(lane == c, x, 0.0))


def _hg_gates(hq, hf, lb):
    sg = _sig(hf)
    nsg = _sig(-hf)
    f = lb + (1.0 - lb) * sg
    g = jnp.log(f)
    k = (1.0 - lb) * nsg
    sq = _sig(hq)
    return hq * sq, g, k, f, sg, nsg, sq


def _hg_prep(hq_ref, hf_ref, lg_ref, b_sc, k_sc):
    R = hq_ref.shape[0]
    G = R // CHUNK
    lb = _sig(lg_ref[0:1, :] - lg_ref[1:2, :])
    hq = hq_ref[...]
    q, g, k, f, sg, nsg, sq = _hg_gates(hq, hf_ref[...], lb)
    rowmod = _iota((R, HALF), 0) & (CHUNK - 1)
    b = _chunk_cumsum(g, rowmod)
    last8 = _iota((8, HALF), 0) == 7
    bl_rows = [_colsum(jnp.where(last8, b[CHUNK * c + CHUNK - 8:CHUNK * (c + 1)], 0.0)) for c in range(G)]
    bl3 = jnp.stack([r[:, 128 * h:128 * (h + 1)] for r in bl_rows for h in range(HEADS)], axis=0)
    b3, k3 = _heads_to_batch(b, 128), _heads_to_batch(k, 128)
    b_sc[...] = b3
    k_sc[...] = k3
    return dict(G=G, lb=lb, hq=hq, f=f, sg=sg, nsg=nsg, sq=sq, rowmod=rowmod, q3=_heads_to_batch(q, 128), k3=k3, b3=b3,
                bl3=bl3)


HSUB = SUB // 2


def _lo(j):
    return HSUB * (j // HSUB)


def _hg_diag_tiles(b_sc, b3, r0, rowi):
    bi = b3[:, r0:r0 + SUB]
    return [jnp.exp(jnp.where(rowi[:, _lo(s):] >= s, bi[:, _lo(s):] - b_sc[:, r0 + s:r0 + s + 1, :], NEG))
            for s in range(SUB)]


def _hg_diag_tiles_t(b_sc, b3, r0, rowi):
    bi = b3[:, r0:r0 + SUB]
    return [jnp.exp(jnp.where(rowi[:, :_lo(t) + HSUB] <= t, b_sc[:, r0 + t:r0 + t + 1, :] - bi[:, :_lo(t) + HSUB], NEG))
            for t in range(SUB)]


def _lane_sums(pieces, ones):
    B = pieces[0].shape[0]
    hs = [p.shape[1] for p in pieces]
    R = _dot(jnp.concatenate(pieces, axis=1).reshape(B * sum(hs), 128), ones).reshape(B, sum(hs), 128)
    out, o = [], 0
    for h in hs:
        out.append(R[:, o:o + h])
        o += h
    return out


def _sum_tri(terms, low_rows):
    full = sum(t for t in terms if t.shape[1] == SUB)
    half = sum(t for t in terms if t.shape[1] == HSUB)
    lo, hi = full[:, :HSUB], full[:, HSUB:]
    return jnp.concatenate([lo + half, hi] if low_rows else [lo, hi + half], axis=1)


def _hgrn2_fwd(u, lb_logits, comms=None):
    T = u.shape[0]
    G = min(GC, T // CHUNK)
    R, B, N = G * CHUNK, G * HEADS, T // CHUNK

    def body(hq_ref, hf_ref, hv_ref, lg_ref, o_ref, st_ref, S_ref, b_sc, k_sc, v_sc):
        @pl.when(pl.program_id(0) == 0)
        def _():
            S_ref[...] = jnp.zeros_like(S_ref)

        pz = _hg_prep(hq_ref, hf_ref, lg_ref, b_sc, k_sc)
        q3, k3, b3, bl3 = pz["q3"], pz["k3"], pz["b3"], pz["bl3"]
        v3 = _heads_to_batch(hv_ref[...], 128)
        v_sc[...] = v3
        stloc = _bdot(v3, k3 * jnp.exp(bl3 - b3), BTN).reshape(G, HEADS, 128, 128)
        dec = jnp.exp(bl3).reshape(G, HEADS, 1, 128)
        ST = S_ref[...]
        sts = []
        for c in range(G):
            sts.append(ST)
            ST = ST * dec[c] + stloc[c]
        S_ref[...] = ST
        st4 = jnp.stack(sts, axis=0)
        st_ref[...] = st4
        o = _bdot(q3 * jnp.exp(b3), st4.reshape(B, 128, 128), BNT)
        ones = jnp.ones((128, 128), F32)
        rowi = _iota((1, SUB, 128), 1)
        outs = []
        for i in range(CHUNK // SUB):
            r0 = SUB * i
            qi = q3[:, r0:r0 + SUB]
            oi = o[:, r0:r0 + SUB]
            if i > 0:
                r = b_sc[:, r0 - 1:r0, :]
                qe = qi * jnp.exp(b3[:, r0:r0 + SUB] - r)
                ke = k3[:, :r0] * jnp.exp(r - b3[:, :r0])
                oi = oi + _bdot(_bdot(qe, ke, BNT), v3[:, :r0], BNN)
            tiles = _hg_diag_tiles(b_sc, b3, r0, rowi)
            a_b = _lane_sums([qi[:, _lo(s):] * (k_sc[:, r0 + s:r0 + s + 1, :] * tiles[s]) for s in range(SUB)], ones)
            outs.append(oi + _sum_tri([a_b[s] * v_sc[:, r0 + s:r0 + s + 1, :] for s in range(SUB)], False))
        o_ref[...] = _batch_to_heads(jnp.concatenate(outs, axis=1))

    blk = lambda c: pl.BlockSpec((R, HALF), lambda n, c=c: (n, c // 4))
    return _hosted_call(
        body, comms, name="hgrn2_fwd", grid=(N // G,),
        in_specs=[blk(C_HQ), blk(C_HF), blk(C_HV), _full(lb_logits.shape)],
        out_specs=[pl.BlockSpec((R, HALF), lambda n: (n, 0)),
                   pl.BlockSpec((G, HEADS, 128, 128), lambda n: (n, 0, 0, 0))],
        out_shape=[_sds((T, HALF)), _sds((N, HEADS, 128, 128))],
        scratch_shapes=[pltpu.VMEM((HEADS, 128, 128), F32)] + [pltpu.VMEM((B, CHUNK, 128), F32)] * 3,
        args=(u, u, u, lb_logits))


def _hgrn2_bwd(u, lb_logits, do, states, comms=None):
    T = u.shape[0]
    G = min(GC, T // CHUNK)
    R, B, NG = G * CHUNK, G * HEADS, T // (G * CHUNK)

    def body(hq_ref, hf_ref, hv_ref, lg_ref, do_ref, st_ref, dhq_ref, dhf_ref, dhv_ref, dlb_ref,
             dS_ref, b_sc, k_sc, v_sc, q_sc, do_sc):
        @pl.when(pl.program_id(0) == 0)
        def _():
            dS_ref[...] = jnp.zeros_like(dS_ref)
            dlb_ref[...] = jnp.zeros_like(dlb_ref)

        pz = _hg_prep(hq_ref, hf_ref, lg_ref, b_sc, k_sc)
        q3, k3, b3, bl3, lb = pz["q3"], pz["k3"], pz["b3"], pz["bl3"], pz["lb"]
        v3 = _heads_to_batch(hv_ref[...], 128)
        v_sc[...] = v3
        do3 = _heads_to_batch(do_ref[...], 128)
        q_sc[...] = q3
        do_sc[...] = do3
        st3 = st_ref[...].reshape(B, 128, 128)
        eb = jnp.exp(b3)
        ebl = jnp.exp(bl3 - b3)
        qt = q3 * eb
        kl = k3 * ebl
        dstloc = _bdot(do3, qt, BTN).reshape(G, HEADS, 128, 128)
        dec = jnp.exp(bl3).reshape(G, HEADS, 1, 128)
        dST = dS_ref[...]
        dsts = [None] * G
        for c in reversed(range(G)):
            dsts[c] = dST
            dST = dST * dec[c] + dstloc[c]
        dS_ref[...] = dST
        dst3 = jnp.stack(dsts, axis=0).reshape(B, 128, 128)
        dqt = _bdot(do3, st3, BNN)
        dkl = _bdot(v3, dst3, BNN)
        dv_acc = _bdot(kl, dst3, BNT)
        ones = jnp.ones((128, 128), F32)
        rowi = _iota((1, SUB, 128), 1)
        dq_parts, dk_parts, dv_parts = [], [], []
        dk_in = jnp.zeros((B, CHUNK, 128), F32)
        for i_s in range(CHUNK // SUB):
            r0 = SUB * i_s
            qi = q3[:, r0:r0 + SUB]
            doi = do3[:, r0:r0 + SUB]
            dqi = jnp.zeros((B, SUB, 128), F32)
            if i_s > 0:
                r = b_sc[:, r0 - 1:r0, :]
                eq = jnp.exp(b3[:, r0:r0 + SUB] - r)
                ek = jnp.exp(r - b3[:, :r0])
                qe = qi * eq
                ke = k3[:, :r0] * ek
                a_off = _bdot(qe, ke, BNT)
                p_off = _bdot(doi, v3[:, :r0], BNT)
                pad = jnp.zeros((B, CHUNK - r0, 128), F32)
                dv_acc = dv_acc + jnp.concatenate([_bdot(a_off, doi, BTN), pad], axis=1)
                dqi = dqi + _bdot(p_off, ke, BNN) * eq
                dk_in = dk_in + jnp.concatenate([_bdot(p_off, qe, BTN) * ek, pad], axis=1)
            ki, vi = k3[:, r0:r0 + SUB], v3[:, r0:r0 + SUB]
            rng = range(SUB)
            tiles = _hg_diag_tiles(b_sc, b3, r0, rowi)
            tiles_t = _hg_diag_tiles_t(b_sc, b3, r0, rowi)
            do_rows = [do_sc[:, r0 + t:r0 + t + 1, :] for t in rng]
            kts = [k_sc[:, r0 + s:r0 + s + 1, :] * tiles[s] for s in rng]
            qts = [q_sc[:, r0 + t:r0 + t + 1, :] * tiles_t[t] for t in rng]
            ps = [doi[:, _lo(s):] * v_sc[:, r0 + s:r0 + s + 1, :] for s in rng]
            mst = [ki[:, :_lo(t) + HSUB] * qts[t] for t in rng]
            pst = [vi[:, :_lo(t) + HSUB] * do_rows[t] for t in rng]
            sums = _lane_sums(ps + mst + pst, ones)
            p_b, a_t, p_t = sums[:SUB], sums[SUB:2 * SUB], sums[2 * SUB:]
            dq_parts.append(dqi + _sum_tri([p_b[s] * kts[s] for s in rng], False))
            dv_parts.append(_sum_tri([a_t[t] * do_rows[t] for t in rng], True))
            dk_parts.append(_sum_tri([p_t[t] * qts[t] for t in rng], True))
        dq_in = jnp.concatenate(dq_parts, axis=1)
        dk_in = dk_in + jnp.concatenate(dk_parts, axis=1)
        dv_acc = dv_acc + jnp.concatenate(dv_parts, axis=1)
        db = qt * dqt + q3 * dq_in - k3 * dk_in - kl * dkl
        last = jnp.sum(kl * dkl, axis=1, keepdims=True) + jnp.exp(bl3) * jnp.sum(st3 * dst3, axis=1, keepdims=True)
        db = db + jnp.where(_iota((1, CHUNK, 1), 1) == CHUNK - 1, last, 0.0)
        dg = _chunk_cumsum(_batch_to_heads(db), pz["rowmod"], reverse=True)
        dq_tot = _batch_to_heads(dqt * eb + dq_in)
        dk_tot = _batch_to_heads(dkl * ebl + dk_in)
        common = dg / pz["f"] - dk_tot
        dhf_ref[...] = ((1.0 - lb) * pz["sg"] * pz["nsg"] * common).astype(dhf_ref.dtype)
        dl0 = _colsum(pz["nsg"] * common) * lb * (1.0 - lb)
        dlb_ref[0:1, :] += dl0
        dlb_ref[1:2, :] -= dl0
        dhq_ref[...] = (dq_tot * _dsilu(pz["hq"], pz["sq"])).astype(dhq_ref.dtype)
        dhv_ref[...] = _batch_to_heads(dv_acc).astype(dhv_ref.dtype)

    rev = lambda c: pl.BlockSpec((R, HALF), lambda i, c=c: (NG - 1 - i, c // 4))
    rev0 = pl.BlockSpec((R, HALF), lambda i: (NG - 1 - i, 0))
    return _hosted_call(
        body, comms, name="hgrn2_bwd", grid=(NG,),
        in_specs=[rev(C_HQ), rev(C_HF), rev(C_HV), _full(lb_logits.shape), rev0,
                  pl.BlockSpec((G, HEADS, 128, 128), lambda i: (NG - 1 - i, 0, 0, 0))],
        out_specs=[rev0, rev0, rev0, _full((2, HALF))],
        out_shape=[_sds((T, HALF), _MXU)] * 3 + [_sds((2, HALF))],
        scratch_shapes=[pltpu.VMEM((HEADS, 128, 128), F32)] + [pltpu.VMEM((B, CHUNK, 128), F32)] * 5,
        args=(u, u, u, lb_logits, do, states))


def _lanes_to_batch_cols(x, lane):
    G = x.shape[0] // CHUNK
    cols = [_lane_col(x, 4 + h, lane).reshape(G, CHUNK, 1) for h in range(HEADS)]
    return jnp.stack(cols, axis=1).reshape(G * HEADS, CHUNK, 1)


def _row_scalars(rows):
    lane = _iota((1, 128), 1)
    return jnp.stack([_rowsum(jnp.where(lane == 4 + h, r, 0.0)) for r in rows for h in range(HEADS)], axis=0)


def _ml_gates(gates):
    R = gates.shape[0]
    lane = _iota((R, 128), 1)
    rowmod = _iota((R, 128), 0) & (CHUNK - 1)
    lf = jnp.minimum(gates, 0.0) - jnp.log(1.0 + jnp.exp(-jnp.abs(gates)))
    g_all = _chunk_cumsum(lf, rowmod)
    x_all = pltpu.roll(gates, 4, 1) - g_all
    return g_all, x_all, lane, rowmod


def _ml_chunk_rows(g_all, x_all, mprev, g):
    gl = g_all[CHUNK * g + CHUNK - 8:CHUNK * (g + 1)]
    gl = _colsum(jnp.where(_iota((8, 128), 0) == 7, gl, 0.0))
    a = gl + x_all[CHUNK * g:CHUNK * (g + 1)]
    m_new = jnp.maximum(gl + mprev, jnp.max(a, axis=0, keepdims=True))
    return m_new, jnp.exp(gl + mprev - m_new), jnp.exp(a - m_new)


def _ml_batched(q3, k3, v3, g_all, x_all, lane, C3, n3, mprev3):
    G = g_all.shape[0] // CHUNK
    gcol3 = _lanes_to_batch_cols(g_all, lane)
    onehot = jnp.where(_iota((G, 8, 128), 1) + 4 == _iota((G, 8, 128), 2), 1.0, 0.0).astype(F32)
    rows = _bdotx(onehot, x_all.reshape(G, CHUNK, 128), BNT)
    sub = _iota((G, 8, CHUNK), 1)
    row3 = jnp.stack([jnp.sum(jnp.where(sub == h, rows, 0.0), axis=1, keepdims=True) for h in range(HEADS)],
                     axis=1).reshape(G * HEADS, 1, CHUNK)
    causal = _iota((1, CHUNK, CHUNK), 1) >= _iota((1, CHUNK, CHUNK), 2)
    dmat = jnp.where(causal, gcol3 + row3, NEG)
    m_inter = gcol3 + mprev3
    m_t = jnp.maximum(m_inter, jnp.max(dmat, axis=2, keepdims=True))
    wi = jnp.exp(dmat - m_t)
    wn = jnp.exp(m_inter - m_t)
    s3 = _bdot(q3, k3, BNT) * wi
    qc = _bdot(q3, C3, BNN)
    qn = jnp.sum(q3 * n3, axis=2, keepdims=True)
    num = _bdot(s3, v3, BNN) + wn * qc
    den = jnp.sum(s3, axis=2, keepdims=True) + wn * qn
    floor = jnp.exp(-m_t)
    return dict(wi=wi, wn=wn, s=s3, qc=qc, qn=qn, num=num, den=den, floor=floor, nrm=jnp.maximum(jnp.abs(den), floor))


def _mlstm_fwd(qkc, u, comms=None):
    T = u.shape[0]
    G = min(GC, T // CHUNK)
    R = G * CHUNK
    N = T // CHUNK

    def body(qk_ref, v_ref, g_ref, h_ref, cst_ref, nst_ref, mst_ref, C_ref, n_ref, m_ref):
        @pl.when(pl.program_id(0) == 0)
        def _():
            C_ref[...] = jnp.zeros_like(C_ref)
            n_ref[...] = jnp.zeros_like(n_ref)
            m_ref[...] = jnp.zeros_like(m_ref)

        g_all, x_all, lane, _ = _ml_gates(g_ref[...])
        m_row = m_ref[...]
        mprev_rows, wo_rows, ws_parts = [], [], []
        for g in range(G):
            mprev_rows.append(m_row)
            m_row, wo, ws = _ml_chunk_rows(g_all, x_all, m_row, g)
            wo_rows.append(wo)
            ws_parts.append(ws)
        m_ref[...] = m_row
        mst_ref[...] = jnp.stack(mprev_rows, axis=0)
        ws3 = _lanes_to_batch_cols(jnp.concatenate(ws_parts, axis=0), lane)
        wo4 = _row_scalars(wo_rows).reshape(G, HEADS, 1, 1)
        q3 = _heads_to_batch(qk_ref[:, :256] * ML_SCALE, ML_DQK)
        k3 = _heads_to_batch(qk_ref[:, 256:], ML_DQK)
        v3 = _heads_to_batch(v_ref[...], 128)
        kw = k3 * ws3
        cloc = _bdot(kw, v3, BTN).reshape(G, HEADS, ML_DQK, 128)
        nloc = jnp.sum(kw, axis=1, keepdims=True).reshape(G, HEADS, 1, ML_DQK)
        C, nn = C_ref[...], n_ref[...]
        cs, ns = [], []
        for g in range(G):
            cs.append(C)
            ns.append(nn)
            C = wo4[g] * C + cloc[g]
            nn = wo4[g] * nn + nloc[g]
        C_ref[...] = C
        n_ref[...] = nn
        c4, n4 = jnp.stack(cs, axis=0), jnp.stack(ns, axis=0)
        cst_ref[...] = c4
        nst_ref[...] = n4
        r = _ml_batched(q3, k3, v3, g_all, x_all, lane, c4.reshape(G * HEADS, ML_DQK, 128),
                        n4.reshape(G * HEADS, 1, ML_DQK), _row_scalars(mprev_rows))
        h_ref[...] = _batch_to_heads(r["num"] / r["nrm"])

    return _hosted_call(
        body, comms, name="mlstm_fwd", grid=(N // G,),
        in_specs=[pl.BlockSpec((R, HALF), lambda n: (n, 0)), pl.BlockSpec((R, HALF), lambda n: (n, C_MV // 4)),
                  pl.BlockSpec((R, 128), lambda n: (n, C_GATES))],
        out_specs=[pl.BlockSpec((R, HALF), lambda n: (n, 0)),
                   pl.BlockSpec((G, HEADS, ML_DQK, 128), lambda n: (n, 0, 0, 0)),
                   pl.BlockSpec((G, HEADS, 1, ML_DQK), lambda n: (n, 0, 0, 0)),
                   pl.BlockSpec((G, 1, 128), lambda n: (n, 0, 0))],
        out_shape=[_sds((T, HALF)), _sds((N, HEADS, ML_DQK, 128)), _sds((N, HEADS, 1, ML_DQK)), _sds((N, 1, 128))],
        scratch_shapes=[pltpu.VMEM((HEADS, ML_DQK, 128), F32), pltpu.VMEM((HEADS, 1, ML_DQK), F32),
                        pltpu.VMEM((1, 128), F32)],
        args=(qkc, u, u))


def _mlstm_bwd(qkc, u, dh, cst, nst, mst):
    T = u.shape[0]
    G = min(GC, T // CHUNK)
    R = G * CHUNK
    NG = T // R

    def body(qk_ref, v_ref, g_ref, dh_ref, cst_ref, nst_ref, mst_ref, dqk_ref, dv_ref, dgt_ref, dC_ref, dn_ref):
        @pl.when(pl.program_id(0) == 0)
        def _():
            dC_ref[...] = jnp.zeros_like(dC_ref)
            dn_ref[...] = jnp.zeros_like(dn_ref)

        B = G * HEADS
        gates = g_ref[...]
        g_all, x_all, lane, rowmod = _ml_gates(gates)
        mprev_rows = [mst_ref[g] for g in range(G)]
        wo_rows, ws_parts = [], []
        for g in range(G):
            _, wo, ws = _ml_chunk_rows(g_all, x_all, mprev_rows[g], g)
            wo_rows.append(wo)
            ws_parts.append(ws)
        ws3 = _lanes_to_batch_cols(jnp.concatenate(ws_parts, axis=0), lane)
        wo3 = _row_scalars(wo_rows)
        wo4 = wo3.reshape(G, HEADS, 1, 1)
        q3 = _heads_to_batch(qk_ref[:, :256] * ML_SCALE, ML_DQK)
        k3 = _heads_to_batch(qk_ref[:, 256:], ML_DQK)
        v3 = _heads_to_batch(v_ref[...], 128)
        dh3 = _heads_to_batch(dh_ref[...], 128)
        C3 = cst_ref[...].reshape(B, ML_DQK, 128)
        n3 = nst_ref[...].reshape(B, 1, ML_DQK)
        r = _ml_batched(q3, k3, v3, g_all, x_all, lane, C3, n3, _row_scalars(mprev_rows))
        wn, s3 = r["wn"], r["s"]
        inv = 1.0 / r["nrm"]
        dnum = dh3 * inv
        dnrm = -jnp.sum(dh3 * (r["num"] * inv), axis=2, keepdims=True) * inv
        dden = jnp.where(jnp.abs(r["den"]) > r["floor"], dnrm * jnp.sign(r["den"]), 0.0)
        ds = _bdot(dnum, v3, BNT) + dden
        dqk = ds * r["wi"]
        dd = ds * s3
        qw = q3 * wn
        dcloc = _bdot(qw, dnum, BTN).reshape(G, HEADS, ML_DQK, 128)
        dnloc = jnp.sum(qw * dden, axis=1, keepdims=True).reshape(G, HEADS, 1, ML_DQK)
        dC, dn = dC_ref[...], dn_ref[...]
        dcs, dns = [None] * G, [None] * G
        for g in reversed(range(G)):
            dcs[g], dns[g] = dC, dn
            dC = wo4[g] * dC + dcloc[g]
            dn = wo4[g] * dn + dnloc[g]
        dC_ref[...] = dC
        dn_ref[...] = dn
        dC3 = jnp.stack(dcs, axis=0).reshape(B, ML_DQK, 128)
        dn3 = jnp.stack(dns, axis=0).reshape(B, 1, ML_DQK)
        dk_st = ws3 * (_bdot(v3, dC3, BNT) + dn3)
        dq = _bdot(dqk, k3, BNN) + wn * (_bdot(dnum, C3, BNT) + dden * n3)
        dk = _bdot(dqk, q3, BTN) + dk_st
        dv = _bdot(s3, dnum, BTN) + ws3 * _bdot(k3, dC3, BNN)
        dv_ref[...] = _batch_to_heads(dv).astype(dv_ref.dtype)
        dqk_ref[...] = jnp.concatenate([_batch_to_heads(dq * ML_SCALE), _batch_to_heads(dk)], axis=1)
        e_col = wn * (jnp.sum(dnum * r["qc"], axis=2, keepdims=True) + dden * r["qn"])
        c_col = jnp.sum(k3 * dk_st, axis=2, keepdims=True)
        z = wo3 * (jnp.sum(dC3 * C3, axis=(1, 2), keepdims=True) + jnp.sum(dn3 * n3, axis=(1, 2), keepdims=True))
        dd_cols = _bdotx(dd, jnp.ones((B, CHUNK, 128), F32), BTN)[:, :, 0:1]
        last = _iota((1, CHUNK, 1), 1) == CHUNK - 1
        dg3 = jnp.sum(dd, axis=2, keepdims=True) - dd_cols + e_col - c_col
        dg3 = dg3 + jnp.where(last, jnp.sum(c_col, axis=1, keepdims=True) + z, 0.0)
        di3 = dd_cols + c_col

        def to_lanes(x3, first):
            x4 = x3.reshape(G, HEADS, CHUNK, 1)
            return sum(jnp.where(lane == first + h, x4[:, h].reshape(R, 1), 0.0) for h in range(HEADS))

        dlf = _chunk_cumsum(to_lanes(dg3, 4), rowmod, reverse=True)
        dgt_ref[...] = (to_lanes(di3, 0) + dlf * _sig(-gates)).astype(dgt_ref.dtype)

    rev = lambda w, c: pl.BlockSpec((R, w), lambda i, c=c: (NG - 1 - i, c))
    st = lambda *s: pl.BlockSpec((G,) + s, lambda i: (NG - 1 - i,) + (0,) * len(s))
    return pl.pallas_call(
        body, name="mlstm_bwd", grid=(NG,),
        in_specs=[rev(HALF, 0), rev(HALF, C_MV // 4), rev(128, C_GATES), rev(HALF, 0),
                  st(HEADS, ML_DQK, 128), st(HEADS, 1, ML_DQK), st(1, 128)],
        out_specs=[rev(HALF, 0), rev(HALF, 0), rev(128, 0)],
        out_shape=[_sds((T, HALF)), _sds((T, HALF), _MXU), _sds((T, 128), _MXU)],
        scratch_shapes=[pltpu.VMEM((HEADS, ML_DQK, 128), F32), pltpu.VMEM((HEADS, 1, ML_DQK), F32)],
        compiler_params=_cp("arbitrary"))(qkc, u, u, dh, cst, nst, mst)


def _head_norm(o):
    rs_parts, r_parts = [], []
    for h in range(HEADS):
        oh = o[:, 128 * h:128 * (h + 1)]
        rs = lax.rsqrt(jnp.mean(oh * oh, axis=-1, keepdims=True) + RMS_EPS)
        rs_parts.append(rs)
        r_parts.append(oh * rs)
    return jnp.concatenate(r_parts, axis=1), rs_parts


def _out_proj_ln(x, u, o_hg, h_ml, g_hg, g_ml, w_out, ln_g, ln_b, tm):
    T = x.shape[0]

    def body(x_ref, hgate_ref, mo_ref, ohg_ref, hml_ref, ghg_ref, gml_ref, w_ref, g_ref, b_ref,
             m_ref, z_ref, x1_ref):
        hgate = hgate_ref[...]
        a = _head_norm(ohg_ref[...])[0] * ghg_ref[...] * (hgate * _sig(hgate))
        b = _head_norm(hml_ref[...])[0] * gml_ref[...] * _sig(mo_ref[...])
        m = jnp.concatenate([a, b], axis=1)
        m_ref[...] = m.astype(m_ref.dtype)
        z = ALPHA * x_ref[...] + _dot(m, w_ref[...])
        z_ref[...] = z
        x1_ref[...] = _ln_fwd(z, g_ref[...], b_ref[...])[0]

    return pl.pallas_call(
        body, name="out_proj_ln1", grid=(T // tm,),
        in_specs=[_row(tm, D_MODEL), _row(tm, HALF, C_HGATE // 4), _row(tm, HALF, C_MO // 4),
                  _row(tm, HALF), _row(tm, HALF), _full(g_hg.shape), _full(g_ml.shape),
                  _full(w_out.shape), _full(ln_g.shape), _full(ln_b.shape)],
        out_specs=[_row(tm, D_MODEL)] * 3,
        out_shape=[_sds((T, D_MODEL), _MXU), _sds((T, D_MODEL)), _sds((T, D_MODEL))],
        compiler_params=_cp("parallel"))(x, u, u, o_hg, h_ml, g_hg, g_ml, w_out, ln_g, ln_b)


def _ffn_ln(x1, wg, wu, wd, ln_g, ln_b, tm):
    T = x1.shape[0]

    def body(x_ref, wg_ref, wu_ref, wd_ref, g_ref, b_ref, z_ref, x2_ref, a_ref, bb_ref, h_ref):
        x = x_ref[...]
        a = _dot(x, wg_ref[...])
        bb = _dot(x, wu_ref[...])
        hh = a * _sig(a) * bb
        a_ref[...] = a.astype(a_ref.dtype)
        bb_ref[...] = bb.astype(bb_ref.dtype)
        h_ref[...] = hh.astype(h_ref.dtype)
        z = ALPHA * x + _dot(hh, wd_ref[...])
        z_ref[...] = z
        x2_ref[...] = _ln_fwd(z, g_ref[...], b_ref[...])[0]

    return pl.pallas_call(
        body, name="ffn_ln2", grid=(T // tm,),
        in_specs=[_row(tm, D_MODEL), _full(wg.shape), _full(wu.shape), _full(wd.shape),
                  _full(ln_g.shape), _full(ln_b.shape)],
        out_specs=[_row(tm, D_MODEL)] * 2 + [_row(tm, D_FF)] * 3,
        out_shape=[_sds((T, D_MODEL))] * 2 + [_sds((T, D_FF), _MXU)] * 3,
        compiler_params=_cp("parallel"))(x1, wg, wu, wd, ln_g, ln_b)


def _ple_loss_ln2_bwd(x2, z2, p, tgt, wpg, bpg, wpp, ln_g, ln_b, tm):
    T = x2.shape[0]

    def body(x2_ref, z_ref, p_ref, t_ref, wpg_ref, bpg_ref, wpp_ref, g_ref, b_ref,
             de_ref, dgp_ref, dz_ref, loss_ref, dbpg_ref, dg_ref, db_ref):
        @pl.when(pl.program_id(0) == 0)
        def _():
            for r in (loss_ref, dbpg_ref, dg_ref, db_ref):
                r[...] = jnp.zeros_like(r)

        x2 = x2_ref[...]
        gate = _sig(_dot(x2, wpg_ref[...]) + bpg_ref[...])
        e = _dot(p_ref[...], wpp_ref[...])
        err = x2 + gate * e - t_ref[...]
        loss_ref[...] += _colsum(err * err)
        dy = err * (1.0 / D_MODEL)
        de_ref[...] = (dy * gate).astype(de_ref.dtype)
        dgp = dy * e * gate * (1.0 - gate)
        dgp_ref[...] = dgp.astype(dgp_ref.dtype)
        dbpg_ref[...] += _colsum(dgp)
        dx2 = dy + _dot(dgp, wpg_ref[...], NT)
        _, xhat, rstd = _ln_fwd(z_ref[...], g_ref[...], b_ref[...])
        dg_ref[...] += _colsum(dx2 * xhat)
        db_ref[...] += _colsum(dx2)
        dz_ref[...] = _ln_bwd(dx2, xhat, rstd, g_ref[...])

    vec = _full((1, D_MODEL))
    return pl.pallas_call(
        body, name="ple_loss_ln2_bwd", grid=(T // tm,),
        in_specs=[_row(tm, D_MODEL), _row(tm, D_MODEL), _row(tm, PLE_DIM), _row(tm, D_MODEL),
                  _full(wpg.shape), vec, _full(wpp.shape), vec, vec],
        out_specs=[_row(tm, D_MODEL)] * 3 + [vec] * 4,
        out_shape=[_sds((T, D_MODEL), _MXU)] * 2 + [_sds((T, D_MODEL))] + [_sds((1, D_MODEL))] * 4,
        compiler_params=_cp("arbitrary"))(x2, z2, p, tgt, wpg, bpg, wpp, ln_g, ln_b)


def _ffn_bwd_ln1_bwd(a_pre, b_pre, z1, dz2, wg, wu, wd, ln_g, ln_b, tm, comms=None):
    T = z1.shape[0]

    def body(a_ref, bb_ref, z_ref, dz2_ref, wg_ref, wu_ref, wd_ref, g_ref, b_ref,
             da_ref, dbb_ref, dz1_ref, dg_ref, db_ref):
        @pl.when(pl.program_id(0) == 0)
        def _():
            dg_ref[...] = jnp.zeros_like(dg_ref)
            db_ref[...] = jnp.zeros_like(db_ref)

        dz2 = dz2_ref[...]
        a = a_ref[...].astype(F32)
        bb = bb_ref[...].astype(F32)
        sa = _sig(a)
        act = a * sa
        dh = _dot(dz2, wd_ref[...], NT)
        da = (dh * bb * _dsilu(a, sa)).astype(da_ref.dtype)
        dbb = (dh * act).astype(dbb_ref.dtype)
        da_ref[...] = da
        dbb_ref[...] = dbb
        dx1 = ALPHA * dz2 + _dot(da, wg_ref[...], NT) + _dot(dbb, wu_ref[...], NT)
        _, xhat, rstd = _ln_fwd(z_ref[...], g_ref[...], b_ref[...])
        dg_ref[...] += _colsum(dx1 * xhat)
        db_ref[...] += _colsum(dx1)
        dz1_ref[...] = _ln_bwd(dx1, xhat, rstd, g_ref[...])

    vec = _full((1, D_MODEL))
    return _hosted_call(
        body, comms, name="ffn_bwd_ln1_bwd", grid=(T // tm,),
        in_specs=[_row(tm, D_FF)] * 2 + [_row(tm, D_MODEL)] * 2 + [_full(wg.shape), _full(wu.shape), _full(wd.shape), vec, vec],
        out_specs=[_row(tm, D_FF)] * 2 + [_row(tm, D_MODEL), vec, vec],
        out_shape=[_sds((T, D_FF), _MXU)] * 2 + [_sds((T, D_MODEL)), _sds((1, D_MODEL)), _sds((1, D_MODEL))],
        scratch_shapes=[], args=(a_pre, b_pre, z1, dz2, wg, wu, wd, ln_g, ln_b))


def _out_proj_bwd(dz1, u, o_hg, h_ml, g_hg, g_ml, w_out, tm):
    T = dz1.shape[0]

    def body(dz_ref, hgate_ref, mo_ref, ohg_ref, hml_ref, ghg_ref, gml_ref, w_ref,
             dohg_ref, dhml_ref, dhgate_ref, dmo_ref, dghg_ref, dgml_ref):
        @pl.when(pl.program_id(0) == 0)
        def _():
            dghg_ref[...] = jnp.zeros_like(dghg_ref)
            dgml_ref[...] = jnp.zeros_like(dgml_ref)

        dm = _dot(dz_ref[...], w_ref[...], NT)

        def half(dmh, o, gvec, gate_val, dgate_fac, do_ref, dgate_ref, dgvec_ref):
            r, rs = _head_norm(o)
            dgate_ref[...] = (dmh * r * gvec * dgate_fac).astype(dgate_ref.dtype)
            dn = dmh * gate_val
            dgvec_ref[...] += _colsum(dn * r)
            dr = dn * gvec
            parts = []
            for h in range(HEADS):
                sl = slice(128 * h, 128 * (h + 1))
                parts.append(rs[h] * (dr[:, sl] - r[:, sl] * jnp.mean(dr[:, sl] * r[:, sl], axis=-1, keepdims=True)))
            do_ref[...] = jnp.concatenate(parts, axis=1)

        hg = hgate_ref[...]
        shg = _sig(hg)
        half(dm[:, :HALF], ohg_ref[...], ghg_ref[...], hg * shg, _dsilu(hg, shg), dohg_ref, dhgate_ref, dghg_ref)
        smo = _sig(mo_ref[...])
        half(dm[:, HALF:], hml_ref[...], gml_ref[...], smo, smo * (1.0 - smo), dhml_ref, dmo_ref, dgml_ref)

    vec = _full((1, HALF))
    return pl.pallas_call(
        body, name="out_proj_bwd", grid=(T // tm,),
        in_specs=[_row(tm, D_MODEL), _row(tm, HALF, C_HGATE // 4), _row(tm, HALF, C_MO // 4),
                  _row(tm, HALF), _row(tm, HALF), vec, vec, _full(w_out.shape)],
        out_specs=[_row(tm, HALF)] * 4 + [vec, vec],
        out_shape=[_sds((T, HALF))] * 2 + [_sds((T, HALF), _MXU)] * 2 + [_sds((1, HALF))] * 2,
        compiler_params=_cp("arbitrary"))(dz1, u, u, o_hg, h_ml, g_hg, g_ml, w_out)


def _conv_bwd(u, pre, dqkc, cw, tm):
    T = u.shape[0]
    hb = tm // 8
    nb = T // 8

    def body(x_ref, xh_ref, pre_ref, preh_ref, d_ref, dh_ref, w_ref, dx_ref, dw_ref, db_ref):
        i = pl.program_id(0)

        @pl.when(i == 0)
        def _():
            dw_ref[...] = jnp.zeros_like(dw_ref)
            db_ref[...] = jnp.zeros_like(db_ref)

        def dpre_of(pre, d):
            return d * _dsilu(pre, _sig(pre))

        rowi = _iota((8, HALF), 0)
        dpre = dpre_of(pre_ref[...], d_ref[...])
        dpre_next = jnp.where(i < pl.num_programs(0) - 1, dpre_of(preh_ref[...], dh_ref[...]), 0.0)
        x = x_ref[...]
        xprev = jnp.where(i > 0, xh_ref[...], 0.0)
        dx = dpre * w_ref[3:4, :]
        db_ref[...] += _colsum(dpre)
        dws = [None] * 4
        dws[3] = _colsum(dpre * x)
        for j in (1, 2, 3):
            dx = dx + _shift_rows_up(dpre, dpre_next, j, rowi) * w_ref[3 - j:4 - j, :]
            dws[3 - j] = _colsum(dpre * _shift_rows(x, xprev, j, rowi))
        dx_ref[...] = dx.astype(dx_ref.dtype)
        dw_ref[...] += jnp.concatenate(dws, axis=0)

    cur = lambda blk: pl.BlockSpec((tm, HALF), lambda i, blk=blk: (i, blk))
    nxt = pl.BlockSpec((8, HALF), lambda i: (jnp.minimum((i + 1) * hb, nb - 1), 0))
    return pl.pallas_call(
        body, name="conv_bwd", grid=(T // tm,),
        in_specs=[cur(C_MQK // 4), pl.BlockSpec((8, HALF), lambda i: (jnp.maximum(i * hb - 1, 0), C_MQK // 4)),
                  cur(0), nxt, cur(0), nxt, _full(cw.shape)],
        out_specs=[cur(0), _full((4, HALF)), _full((1, HALF))],
        out_shape=[_sds((T, HALF), _MXU), _sds((4, HALF)), _sds((1, HALF))],
        compiler_params=_cp("arbitrary"))(u, u, pre, pre, dqkc, dqkc, cw)


def _du_specs(rows):
    return [pl.BlockSpec((rows, w), lambda i: (i, 0)) for w in DU_WIDTHS]


def _in_proj_bwd(dz1, du_parts, w, tm, comms=None):
    T = dz1.shape[0]

    def body(dz_ref, *refs):
        du = jnp.concatenate([r[...] for r in refs[:8]], axis=1)
        refs[9][...] = ALPHA * dz_ref[...] + _dot(du, refs[8][...], NT)

    (dx,), got = _hosted_call(
        body, comms, name="in_proj_bwd", grid=(T // tm,),
        in_specs=[_row(tm, D_MODEL)] + _du_specs(tm) + [_full(w.shape)],
        out_specs=[_row(tm, D_MODEL)], out_shape=[_sds((T, D_MODEL))], scratch_shapes=[], args=(dz1, *du_parts, w))
    return dx, got


def _wgrad(a, b, name, tm, tn, tk):
    T, M = a.shape
    N = b.shape[1]
    tm, tn, tk = min(tm, M), min(tn, N), min(tk, T)
    nk = T // tk

    def body(a_ref, b_ref, o_ref, acc_ref):
        kk = pl.program_id(2)

        @pl.when(kk == 0)
        def _():
            acc_ref[...] = jnp.zeros_like(acc_ref)

        acc_ref[...] += _dot(a_ref[...], b_ref[...], TN)

        @pl.when(kk == nk - 1)
        def _():
            o_ref[...] = acc_ref[...].astype(o_ref.dtype)

    return pl.pallas_call(
        body, name=name, grid=(M // tm, N // tn, nk),
        in_specs=[pl.BlockSpec((tk, tm), lambda i, j, kk: (kk, i)), pl.BlockSpec((tk, tn), lambda i, j, kk: (kk, j))],
        out_specs=pl.BlockSpec((tm, tn), lambda i, j, kk: (i, j)), out_shape=_sds((M, N), _MXU),
        scratch_shapes=[pltpu.VMEM((tm, tn), F32)],
        compiler_params=_cp("parallel", "parallel", "arbitrary"))(a, b)


W_IN_PARTS = 2


def _wgrad_w_in(x, du_parts, tk, part, comms=None):
    T = x.shape[0]
    M = D_MODEL // W_IN_PARTS
    tk = min(tk, T)
    nk = T // tk

    def body(a_ref, *refs):
        o_ref, cs_ref, acc_ref = refs[8:]
        kk = pl.program_id(0)

        @pl.when(kk == 0)
        def _():
            acc_ref[...] = jnp.zeros_like(acc_ref)
            cs_ref[...] = jnp.zeros_like(cs_ref)

        du = jnp.concatenate([r[...] for r in refs[:8]], axis=1)
        acc_ref[...] += _dot(a_ref[...], du, TN)
        cs_ref[...] += _colsum(du.astype(F32))

        @pl.when(kk == nk - 1)
        def _():
            o_ref[...] = acc_ref[...].astype(o_ref.dtype)

    return _hosted_call(
        body, comms, name="wgrad_w_in_%d" % part, grid=(nk,),
        in_specs=[pl.BlockSpec((tk, M), lambda kk: (kk, part))] + _du_specs(tk),
        out_specs=[_full((M, PROJ_WP)), _full((1, PROJ_WP))],
        out_shape=[_sds((M, PROJ_WP), _MXU), _sds((1, PROJ_WP))],
        scratch_shapes=[pltpu.VMEM((M, PROJ_WP), F32)], args=(x, *du_parts))


W_IN_S, FF_S, OUT_S, PP_S = PROJ_W // N_DEV, D_FF // N_DEV, D_MODEL // N_DEV, D_MODEL // N_DEV
LATE = ("w_ffn_gate", "w_ffn_up", "w_out", "w_ffn_down", "ple_w_gate", "ple_w_proj")
BIG = ("w_in",) + LATE


def _split_cols(a, n):
    return a.reshape(a.shape[0], N_DEV, n).transpose(1, 0, 2)


def _join_cols(a):
    return a.transpose(1, 0, 2).reshape(a.shape[1], -1)


def _step(x, p, tgt, w_in, b_in, lb_logits, conv_w, conv_b, g_hg, g_ml, ln1_g, ln1_b, ln2_g, ln2_b, bpg, late,
          distributed):
    T = x.shape[0]
    tm, tf = min(ROWS, T), min(ROWS_FFN, T)
    gather = lambda *names: [_GatherTwoLevel([late[n] for n in names])] if distributed else None
    scatter = lambda *arrs: [_Comm("scatter", list(arrs))] if distributed else None
    rows = lambda a, n: a.reshape(N_DEV, n, D_MODEL)
    (u, pre, qkc), got1 = _in_proj(x, w_in, b_in, conv_w, conv_b, tm, gather("w_out", "ple_w_gate", "ple_w_proj"))
    (o_hg, hg_states), got2 = _hgrn2_fwd(u, lb_logits, gather("w_ffn_gate", "w_ffn_up"))
    (h_ml, cst, nst, mst), got3 = _mlstm_fwd(qkc, u, gather("w_ffn_down"))
    if distributed:
        w_out, wpg, wpp = got1[0][0].reshape(D_MODEL, D_MODEL), got1[0][1].reshape(D_MODEL, D_MODEL), _join_cols(got1[0][2])
        wg, wu, wd = _join_cols(got2[0][0]), _join_cols(got2[0][1]), got3[0][0].reshape(D_FF, D_MODEL)
    else:
        w_out, wg, wu, wd, wpg, wpp = (late[n] for n in ("w_out", "w_ffn_gate", "w_ffn_up", "w_ffn_down", "ple_w_gate", "ple_w_proj"))
    m_in, z1, x1 = _out_proj_ln(x, u, o_hg, h_ml, g_hg, g_ml, w_out, ln1_g, ln1_b, tm)
    z2, x2, a_pre, b_pre, hh = _ffn_ln(x1, wg, wu, wd, ln2_g, ln2_b, tf)
    de, dgp, dz2, loss_vec, d_bpg, d_ln2g, d_ln2b = _ple_loss_ln2_bwd(x2, z2, p, tgt, wpg, bpg, wpp, ln2_g, ln2_b, tm)
    big = dict(ple_w_gate=_wgrad(x2, dgp, "wgrad_ple_gate", 512, D_MODEL, 1024),
               ple_w_proj=_wgrad(p, de, "wgrad_ple_proj", 512, D_MODEL, 1024))
    (da, dbb, dz1, d_ln1g, d_ln1b), r1 = _ffn_bwd_ln1_bwd(
        a_pre, b_pre, z1, dz2, wg, wu, wd, ln1_g, ln1_b, tf, scatter(rows(big["ple_w_gate"], OUT_S), _split_cols(big["ple_w_proj"], PP_S)))
    big.update(
        w_ffn_gate=_wgrad(x1, da, "wgrad_ffn_gate", 512, D_FF, 1024),
        w_ffn_up=_wgrad(x1, dbb, "wgrad_ffn_up", 512, D_FF, 1024),
        w_ffn_down=_wgrad(hh, dz2, "wgrad_ffn_down", D_FF, D_MODEL, 1024),
        w_out=_wgrad(m_in, dz1, "wgrad_w_out", 512, D_MODEL, 1024))
    d_ohg, d_hml, d_hgate, d_mo, d_ghg, d_gml = _out_proj_bwd(dz1, u, o_hg, h_ml, g_hg, g_ml, w_out, tm)
    (d_hq, d_hf, d_hv, d_lb), r2 = _hgrn2_bwd(
        u, lb_logits, d_ohg, hg_states,
        scatter(_split_cols(big["w_ffn_gate"], FF_S), _split_cols(big["w_ffn_up"], FF_S), rows(big["w_ffn_down"], FF_S),
                rows(big["w_out"], OUT_S)))
    d_qkc, d_mv, d_gates = _mlstm_bwd(qkc, u, d_hml, cst, nst, mst)
    d_mqk, d_convw, d_convb = _conv_bwd(u, pre, d_qkc, conv_w, tm)
    du_parts = [d_hq, d_hf, d_hv, d_hgate, d_mqk, d_mv, d_mo, d_gates]
    own = lambda g: _split_cols(g[:, :PROJ_W], W_IN_S)
    (g_in0, d_bin), _ = _wgrad_w_in(x, du_parts, 512, 0)
    (g_in1, _), r_in0 = _wgrad_w_in(x, du_parts, 512, 1, scatter(own(g_in0)))
    big["w_in"] = jnp.concatenate([g_in0, g_in1], axis=0)
    small = dict(b_in=d_bin[:, :PROJ_W], hg_lb_logits=d_lb, ml_conv_w=d_convw, ml_conv_b=d_convb, hg_norm_g=d_ghg,
                 ml_norm_g=d_gml, ln1_g=d_ln1g, ln1_b=d_ln1b, ln2_g=d_ln2g, ln2_b=d_ln2b, ple_b_gate=d_bpg)
    last = [_Comm("scatter", [own(g_in1)]), _Comm("gather", [loss_vec] + [small[n] for n in SMALL])] if distributed else None
    dx, r3 = _in_proj_bwd(dz1, du_parts, w_in, tm, last)
    gathered_small = None
    if distributed:
        big = dict(ple_w_gate=r1[0][0], ple_w_proj=r1[0][1], w_ffn_gate=r2[0][0], w_ffn_up=r2[0][1],
                   w_ffn_down=r2[0][2], w_out=r2[0][3], w_in=[r_in0[0][0], r3[0][0]])
        gathered_small = r3[1]
    return loss_vec, dx, big, small, gathered_small


SMALL = ("b_in", "hg_lb_logits", "ml_conv_w", "ml_conv_b", "hg_norm_g", "ml_norm_g", "ln1_g", "ln1_b", "ln2_g", "ln2_b",
         "ple_b_gate")


def _padc(a, n):
    return jnp.pad(a, [(0, 0)] * (a.ndim - 1) + [(0, n - a.shape[-1])])


def _adamw(w, g, m, v):
    m = B1 * m + (1.0 - B1) * g
    v = B2 * v + (1.0 - B2) * jnp.square(g)
    m_hat = m / (1.0 - B1 ** STEP)
    v_hat = v / (1.0 - B2 ** STEP)
    return -LR * (m_hat / (jnp.sqrt(v_hat) + EPS) + WD * w), m, v


def _sum_slabs(ref):
    g = ref[0].astype(F32)
    for j in range(1, N_DEV):
        g = g + ref[j].astype(F32)
    return g


def _adamw_matrix(rbs, w, m, v, name):
    rbs = list(rbs) if isinstance(rbs, (list, tuple)) else [rbs]
    nb = len(rbs)
    R, C = w.shape
    tr = 256 if (R // nb) % 256 == 0 else R // nb
    per = R // nb // tr

    def body(*refs):
        w_ref, m_ref, v_ref, g_ref, d_ref, m2_ref, v2_ref = refs[nb:]
        i = pl.program_id(0)
        g = _sum_slabs(refs[0])
        for k in range(1, nb):
            g = jnp.where(i >= k * per, _sum_slabs(refs[k]), g)
        g_ref[...] = g
        d_ref[...], m2_ref[...], v2_ref[...] = _adamw(w_ref[...], g, m_ref[...], v_ref[...])

    blk = pl.BlockSpec((tr, C), lambda i: (i, 0))
    part = lambda k: pl.BlockSpec((N_DEV, tr, C), lambda i, k=k: (0, jnp.clip(i - k * per, 0, per - 1), 0))
    return pl.pallas_call(
        body, name=name, grid=(R // tr,),
        in_specs=[part(k) for k in range(nb)] + [blk, blk, blk],
        out_specs=[blk] * 4, out_shape=[_sds((R, C))] * 4, compiler_params=_cp("parallel"))(*rbs, w, m, v)


def _adamw_small(loss_g, gs, ws, ms, vs):
    n = len(ws)

    def body(*refs):
        loss_ref, g_refs, w_refs, m_refs, v_refs = refs[0], refs[1:1 + n], refs[1 + n:1 + 2 * n], refs[1 + 2 * n:1 + 3 * n], refs[1 + 3 * n:1 + 4 * n]
        outs = refs[1 + 4 * n:]
        outs[0][...] = (0.5 / D_MODEL) * jnp.sum(_sum_slabs(loss_ref), keepdims=True)
        for i in range(n):
            g = _sum_slabs(g_refs[i])
            outs[1 + i][...] = g
            outs[1 + n + i][...], outs[1 + 2 * n + i][...], outs[1 + 3 * n + i][...] = _adamw(
                w_refs[i][...], g, m_refs[i][...], v_refs[i][...])

    res = pl.pallas_call(
        body, name="adamw_small", out_shape=[_sds((1, 1))] + [_sds(w.shape) for w in ws] * 4)(loss_g, *gs, *ws, *ms, *vs)
    return res[0], [res[1 + k * n:1 + (k + 1) * n] for k in range(4)]


WEIGHTS = ("w_in", "b_in", "hg_lb_logits", "ml_conv_w", "ml_conv_b", "hg_norm_g", "ml_norm_g", "w_out", "ln1_g", "ln1_b",
           "w_ffn_gate", "w_ffn_up", "w_ffn_down", "ln2_g", "ln2_b", "ple_w_proj", "ple_w_gate", "ple_b_gate")
CONV_S = HALF // N_DEV


def kernel(x, p, w_in, b_in, hg_lb_logits, ml_conv_w, ml_conv_b, hg_norm_g, ml_norm_g, w_out, ln1_g, ln1_b, w_ffn_gate, w_ffn_up, w_ffn_down, ln2_g, ln2_b, ple_w_proj, ple_w_gate, ple_b_gate, loss_target, m_w_in, m_b_in, m_hg_lb_logits, m_ml_conv_w, m_ml_conv_b, m_hg_norm_g, m_ml_norm_g, m_w_out, m_ln1_g, m_ln1_b, m_w_ffn_gate, m_w_ffn_up, m_w_ffn_down, m_ln2_g, m_ln2_b, m_ple_w_proj, m_ple_w_gate, m_ple_b_gate, v_w_in, v_b_in, v_hg_lb_logits, v_ml_conv_w, v_ml_conv_b, v_hg_norm_g, v_ml_norm_g, v_w_out, v_ln1_g, v_ln1_b, v_w_ffn_gate, v_w_ffn_up, v_w_ffn_down, v_ln2_g, v_ln2_b, v_ple_w_proj, v_ple_w_gate, v_ple_b_gate):
    args = locals()
    me = 4 * lax.axis_index("x") + 2 * lax.axis_index("y") + lax.axis_index("c")
    shapes = {n: args[n].shape for n in WEIGHTS}
    drop = lambda n, a: a[0] if n in BIG or n == "ml_conv_w" else a
    W = {n: drop(n, args[n]) for n in WEIGHTS}
    M = {n: drop(n, args["m_" + n]) for n in WEIGHTS}
    V = {n: drop(n, args["v_" + n]) for n in WEIGHTS}

    g_in, g_conv = _gather_two_level(
        [W["w_in"].astype(_MXU), jnp.pad(W["ml_conv_w"], ((0, 4), (0, 128 - CONV_S)))], "gather_w_in")
    w_in_full = _padc(_join_cols(g_in), PROJ_WP)
    conv_full = _join_cols(g_conv[:, :4, :CONV_S])

    _, dx, big, _, sg = _step(
        x[0], p[0, 0], loss_target[0], w_in_full, _padc(b_in, PROJ_WP), hg_lb_logits, conv_full, ml_conv_b,
        hg_norm_g, ml_norm_g, ln1_g, ln1_b, ln2_g, ln2_b, ple_b_gate, {n: W[n].astype(_MXU) for n in LATE}, True)

    upd = {n: _adamw_matrix(big[n], W[n], M[n], V[n], "adamw_" + n) for n in BIG}
    sg = dict(zip(SMALL, sg[1:]), loss=sg[0])
    sg["ml_conv_w"] = lax.dynamic_slice(sg["ml_conv_w"], (0, 0, me * CONV_S), (N_DEV, 4, CONV_S))
    loss, small_upd = _adamw_small(sg["loss"], *[[d[n] for n in SMALL] for d in (sg, W, M, V)])

    outs = []
    for kind in range(4):
        smalls = dict(zip(SMALL, small_upd[kind]))
        for n in WEIGHTS:
            outs.append((upd[n][kind] if n in BIG else smalls[n]).reshape(shapes[n]))
    return (loss.reshape(()), dx.reshape(x.shape), *outs)
```

```python
import jax
import jax.numpy as jnp
from jax import lax
from jax.experimental import pallas as pl
from jax.experimental.pallas import tpu as pltpu

F32 = jnp.float32
_MXU = jnp.bfloat16

D_MODEL = 1024
CHUNK = 64
SUB = 16
PLE_DIM = 256
HEADS = 4
ML_DQK = 64
HALF = 512
D_FF = 2816
PROJ_W = 3592
PROJ_WP = 3712
ALPHA = float(2 ** 0.25)
LN_EPS = 1e-5
RMS_EPS = 1e-6
ML_SCALE = ML_DQK ** -0.5
N_DEV = 8
LR, B1, B2, EPS, WD, STEP = 0.001, 0.9, 0.999, 1e-08, 0.01, 10
NEG = -1e30
LOG2E = 1.4426950408889634

C_HQ, C_HF, C_HV, C_HGATE, C_MQK, C_MV, C_MO, C_GATES = 0, 4, 8, 12, 16, 20, 24, 28
DU_WIDTHS = (HALF,) * 7 + (128,)

VMEM_LIMIT = 52 * 1024 * 1024
GC = 8
ROWS = 512
ROWS_FFN = 256

NN = (((1,), (0,)), ((), ()))
NT = (((1,), (1,)), ((), ()))
TN = (((0,), (0,)), ((), ()))
BNT = (((2,), (2,)), ((0,), (0,)))
BNN = (((2,), (1,)), ((0,), (0,)))
BTN = (((1,), (1,)), ((0,), (0,)))


def _dot(a, b, dims=NN):
    return lax.dot_general(a.astype(_MXU), b.astype(_MXU), dims, preferred_element_type=F32)


def _dotx(a, b, dims=NN):
    return lax.dot_general(a, b, dims, precision=lax.Precision.HIGHEST, preferred_element_type=F32)


def _sig(x):
    return jax.nn.sigmoid(x)


def _cp(*sem):
    return pltpu.CompilerParams(dimension_semantics=sem, vmem_limit_bytes=VMEM_LIMIT)


def _row(tm, c, blk=0):
    return pl.BlockSpec((tm, c), lambda i, blk=blk: (i, blk))


def _full(shape):
    nd = len(shape)
    return pl.BlockSpec(tuple(shape), lambda *_, nd=nd: (0,) * nd)


def _sds(shape, dtype=F32):
    return jax.ShapeDtypeStruct(tuple(shape), dtype)


def _iota(shape, axis):
    return lax.broadcasted_iota(jnp.int32, shape, axis)


def _colsum(x):
    return jnp.sum(x, axis=0, keepdims=True)


def _rowsum(x):
    return jnp.sum(x, axis=1, keepdims=True)


def _ln_fwd(z, g, b):
    mu = jnp.mean(z, axis=-1, keepdims=True)
    zc = z - mu
    var = jnp.mean(zc * zc, axis=-1, keepdims=True)
    rstd = lax.rsqrt(var + LN_EPS)
    xhat = zc * rstd
    return xhat * g + b, xhat, rstd


def _ln_bwd(dy, xhat, rstd, g):
    dxh = dy * g
    m1 = jnp.mean(dxh, axis=-1, keepdims=True)
    m2 = jnp.mean(dxh * xhat, axis=-1, keepdims=True)
    return rstd * (dxh - m1 - xhat * m2)


def _dsilu(x, s):
    return s * (1.0 + x * (1.0 - s))


MESH = pl.DeviceIdType.MESH
ANY = pl.BlockSpec(memory_space=pl.ANY)


def _flip(v, bit):
    return 1 - v if bit else v


class _Comm:
    def __init__(self, kind, srcs):
        self.kind, self.srcs, self.n = kind, list(srcs), len(srcs)

    def out_shape(self):
        lead = (N_DEV,) if self.kind == "gather" else ()
        return [jax.ShapeDtypeStruct(lead + s.shape, s.dtype) for s in self.srcs]

    def scratch(self):
        return [pltpu.SemaphoreType.DMA((7 * self.n,)), pltpu.SemaphoreType.DMA((7 * self.n,)),
                pltpu.SemaphoreType.DMA((self.n,))]

    def copies(self, srcs, dsts, send_sems, recv_sems, local_sems):
        x, y, c = lax.axis_index("x"), lax.axis_index("y"), lax.axis_index("c")
        me = 4 * x + 2 * y + c
        pick = (lambda s, j: s) if self.kind == "gather" else (lambda s, j: s.at[j])
        out = []
        for i, (s, d) in enumerate(zip(srcs, dsts)):
            out.append(pltpu.make_async_copy(pick(s, me), d.at[me], local_sems.at[i]))
            for k in range(1, N_DEV):
                px, py, pc = _flip(x, k & 4), _flip(y, k & 2), _flip(c, k & 1)
                out.append(pltpu.make_async_remote_copy(
                    src_ref=pick(s, 4 * px + 2 * py + pc), dst_ref=d.at[me], send_sem=send_sems.at[7 * i + k - 1],
                    recv_sem=recv_sems.at[7 * i + k - 1], device_id=(px, py, pc), device_id_type=MESH))
        return out

    def start(self, *refs):
        for cp in self.copies(*refs):
            cp.start()

    def mid(self, *refs):
        pass

    def finish(self, *refs):
        for cp in self.copies(*refs):
            cp.wait()


class _GatherTwoLevel(_Comm):
    def __init__(self, srcs):
        super().__init__("gather", srcs)

    def _parts(self, srcs, dsts, send_sems, recv_sems, local_sems):
        x, y, c = lax.axis_index("x"), lax.axis_index("y"), lax.axis_index("c")
        me, sibling = (x, y, c), (x, y, 1 - c)
        chips = [(1 - x, y), (x, 1 - y), (1 - x, 1 - y)]

        def copy(i, k, block, to, own=False):
            slab = dsts[i].at[4 * block[0] + 2 * block[1] + block[2]]
            return pltpu.make_async_remote_copy(
                src_ref=srcs[i] if own else slab, dst_ref=slab, send_sem=send_sems.at[7 * i + k],
                recv_sem=recv_sems.at[7 * i + k], device_id=to, device_id_type=MESH)

        n = range(self.n)
        mine = [pltpu.make_async_copy(srcs[i], dsts[i].at[4 * x + 2 * y + c], local_sems.at[i]) for i in n]
        first = [copy(i, 0, me, sibling, own=True) for i in n]
        first += [copy(i, 1 + j, me, (*chip, c), own=True) for j, chip in enumerate(chips) for i in n]
        over_ici = [copy(i, 1 + j, (*chip, c), me) for j, chip in enumerate(chips) for i in n]
        passed = [copy(i, 4 + j, (*chip, c), sibling) for j, chip in enumerate(chips) for i in n]
        from_sibling = [copy(i, 0, sibling, me) for i in n]
        from_sibling += [copy(i, 4 + j, (*chip, 1 - c), me) for j, chip in enumerate(chips) for i in n]
        return mine, first, over_ici, passed, from_sibling

    def start(self, *refs):
        mine, first, _, _, _ = self._parts(*refs)
        for cp in mine + first:
            cp.start()

    def mid(self, *refs):
        _, _, over_ici, passed, _ = self._parts(*refs)
        for arrived, onward in zip(over_ici, passed):
            arrived.wait_recv()
            onward.start()

    def finish(self, *refs):
        mine, first, _, passed, from_sibling = self._parts(*refs)
        for cp in from_sibling:
            cp.wait_recv()
        for cp in first + passed:
            cp.wait_send()
        for cp in mine:
            cp.wait()


def _hosted_call(body, comms, *, name, grid, in_specs, out_specs, out_shape, scratch_shapes, args):
    comms = list(comms or [])
    if not comms:
        res = pl.pallas_call(body, name=name, grid=grid, in_specs=in_specs, out_specs=out_specs, out_shape=out_shape,
                             scratch_shapes=scratch_shapes, compiler_params=_cp("arbitrary"))(*args)
        return list(res), []
    n_in, n_out, n_sc, nc = len(in_specs), len(out_specs), len(scratch_shapes), sum(cm.n for cm in comms)
    last = grid[0] - 1

    def hosted(*refs):
        ins, csrc = refs[:n_in], refs[n_in:n_in + nc]
        o0 = n_in + nc
        outs, cdst = refs[o0:o0 + n_out], refs[o0 + n_out:o0 + n_out + nc]
        s0 = o0 + n_out + nc
        scr, sems = refs[s0:s0 + n_sc], refs[s0 + n_sc:]

        def phase(which):
            o = 0
            for j, cm in enumerate(comms):
                getattr(cm, which)(csrc[o:o + cm.n], cdst[o:o + cm.n], *sems[3 * j:3 * j + 3])
                o += cm.n

        i = pl.program_id(0)

        @pl.when(i == 0)
        def _():
            phase("start")

        body(*ins, *outs, *scr)

        @pl.when(i == (2 * last) // 3)
        def _():
            phase("mid")

        @pl.when(i == last)
        def _():
            phase("finish")

    res = pl.pallas_call(
        hosted, name=name, grid=grid, in_specs=list(in_specs) + [ANY] * nc, out_specs=list(out_specs) + [ANY] * nc,
        out_shape=list(out_shape) + [s for cm in comms for s in cm.out_shape()],
        scratch_shapes=list(scratch_shapes) + [s for cm in comms for s in cm.scratch()],
        compiler_params=_cp("arbitrary"))(*args, *[a for cm in comms for a in cm.srcs])
    got, o = [], n_out
    for cm in comms:
        got.append(list(res[o:o + cm.n]))
        o += cm.n
    return list(res[:n_out]), got


def _gather_two_level(blocks, name):
    n = len(blocks)

    def body(*refs):
        x_refs, out_refs = refs[:n], refs[n:2 * n]
        send_sems, recv_sems, local_sems = refs[2 * n:]
        x, y, c = lax.axis_index("x"), lax.axis_index("y"), lax.axis_index("c")
        me, sibling = (x, y, c), (x, y, 1 - c)
        chips = [(1 - x, y), (x, 1 - y), (1 - x, 1 - y)]

        def copy(i, k, block, to, own=False):
            slab = out_refs[i].at[4 * block[0] + 2 * block[1] + block[2]]
            return pltpu.make_async_remote_copy(
                src_ref=x_refs[i] if own else slab, dst_ref=slab, send_sem=send_sems.at[7 * i + k],
                recv_sem=recv_sems.at[7 * i + k], device_id=to, device_id_type=MESH)

        mine = [pltpu.make_async_copy(x_refs[i], out_refs[i].at[4 * x + 2 * y + c], local_sems.at[i]) for i in range(n)]
        for cp in mine:
            cp.start()
        first = [copy(i, 0, me, sibling, own=True) for i in range(n)]
        first += [copy(i, 1 + j, me, (*chip, c), own=True) for j, chip in enumerate(chips) for i in range(n)]
        for cp in first:
            cp.start()
        passed = []
        for j, chip in enumerate(chips):
            for i in range(n):
                copy(i, 1 + j, (*chip, c), me).wait_recv()
                passed.append(copy(i, 4 + j, (*chip, c), sibling))
                passed[-1].start()
        for i in range(n):
            copy(i, 0, sibling, me).wait_recv()
            for j, chip in enumerate(chips):
                copy(i, 4 + j, (*chip, 1 - c), me).wait_recv()
        for cp in first + passed:
            cp.wait_send()
        for cp in mine:
            cp.wait()

    return pl.pallas_call(
        body, name=name, out_shape=[jax.ShapeDtypeStruct((N_DEV,) + b.shape, b.dtype) for b in blocks],
        in_specs=[ANY] * n, out_specs=[ANY] * n,
        scratch_shapes=[pltpu.SemaphoreType.DMA((7 * n,)), pltpu.SemaphoreType.DMA((7 * n,)),
                        pltpu.SemaphoreType.DMA((n,))])(*blocks)


def _in_proj(x, w, b, cw, cb, tm, comms=None):
    T = x.shape[0]

    def body(x_ref, w_ref, b_ref, cw_ref, cb_ref, o_ref, pre_ref, act_ref, halo_sc):
        @pl.when(pl.program_id(0) == 0)
        def _():
            halo_sc[...] = jnp.zeros_like(halo_sc)

        o = _dot(x_ref[...], w_ref[...]) + b_ref[...]
        o_ref[...] = o
        xc = o[:, 128 * C_MQK:128 * C_MQK + HALF]
        halo = halo_sc[...]
        rowi = _iota((8, HALF), 0)
        acc = xc * cw_ref[3:4, :] + cb_ref[...]
        for j in (1, 2, 3):
            acc = acc + _shift_rows(xc, halo, j, rowi) * cw_ref[3 - j:4 - j, :]
        pre_ref[...] = acc
        act_ref[...] = acc * _sig(acc)
        halo_sc[...] = xc[tm - 8:]

    return _hosted_call(
        body, comms, name="in_proj", grid=(T // tm,),
        in_specs=[_row(tm, D_MODEL), _full(w.shape), _full(b.shape), _full(cw.shape), _full(cb.shape)],
        out_specs=[_row(tm, PROJ_WP), _row(tm, HALF), _row(tm, HALF)],
        out_shape=[_sds((T, PROJ_WP)), _sds((T, HALF)), _sds((T, HALF))],
        scratch_shapes=[pltpu.VMEM((8, HALF), F32)], args=(x, w, b, cw, cb))


def _shift_rows(x, halo, j, rowi):
    r = pltpu.roll(x, j, 0)
    top = jnp.where(rowi < j, pltpu.roll(halo, j, 0), r[:8])
    return jnp.concatenate([top, r[8:]], axis=0)


def _shift_rows_up(x, halo, j, rowi):
    n = x.shape[0]
    r = pltpu.roll(x, n - j, 0)
    bot = jnp.where(rowi >= 8 - j, pltpu.roll(halo, 8 - j, 0), r[n - 8:])
    return jnp.concatenate([r[:n - 8], bot], axis=0)


def _bdot(a, b, dims):
    return lax.dot_general(a.astype(_MXU), b.astype(_MXU), dims, preferred_element_type=F32)


def _bdotx(a, b, dims):
    return lax.dot_general(a, b, dims, precision=lax.Precision.HIGHEST, preferred_element_type=F32)


def _heads_to_batch(x, w):
    G = x.shape[0] // CHUNK
    x3 = x.reshape(G, CHUNK, HEADS * w)
    return jnp.stack([x3[:, :, w * h:w * (h + 1)] for h in range(HEADS)], axis=1).reshape(G * HEADS, CHUNK, w)


def _batch_to_heads(x3):
    B, _, w = x3.shape
    x4 = x3.reshape(B // HEADS, HEADS, CHUNK, w)
    return jnp.concatenate([x4[:, h] for h in range(HEADS)], axis=-1).reshape(B // HEADS * CHUNK, HEADS * w)


def _chunk_cumsum(x, rowmod, reverse=False):
    R = x.shape[0]
    for sh in (1, 2, 4, 8, 16, 32):
        if reverse:
            x = x + jnp.where(rowmod < CHUNK - sh, pltpu.roll(x, R - sh, 0), 0.0)
        else:
            x = x + jnp.where(rowmod >= sh, pltpu.roll(x, sh, 0), 0.0)
    return x


def _lane_col(x, c, lane):
    return _rowsum(jnp.where(lane == c, x, 0.0))


def _hg_gates(hq, hf, lb):
    sg = _sig(hf)
    nsg = _sig(-hf)
    f = lb + (1.0 - lb) * sg
    g = jnp.log(f)
    k = (1.0 - lb) * nsg
    sq = _sig(hq)
    return hq * sq, g, k, f, sg, nsg, sq


def _hg_prep(hq_ref, hf_ref, lg_ref, b_sc, k_sc):
    R = hq_ref.shape[0]
    G = R // CHUNK
    lb = _sig(lg_ref[0:1, :] - lg_ref[1:2, :])
    hq = hq_ref[...]
    q, g, k, f, sg, nsg, sq = _hg_gates(hq, hf_ref[...], lb)
    rowmod = _iota((R, HALF), 0) & (CHUNK - 1)
    b = _chunk_cumsum(g, rowmod) * LOG2E
    last8 = _iota((8, HALF), 0) == 7
    bl_rows = [_colsum(jnp.where(last8, b[CHUNK * c + CHUNK - 8:CHUNK * (c + 1)], 0.0)) for c in range(G)]
    bl3 = jnp.stack([r[:, 128 * h:128 * (h + 1)] for r in bl_rows for h in range(HEADS)], axis=0)
    b3, k3 = _heads_to_batch(b, 128), _heads_to_batch(k, 128)
    b_sc[...] = b3
    k_sc[...] = k3
    return dict(G=G, lb=lb, hq=hq, f=f, sg=sg, nsg=nsg, sq=sq, rowmod=rowmod, q3=_heads_to_batch(q, 128), k3=k3, b3=b3,
                bl3=bl3)


HSUB = SUB // 2


def _lo(j):
    return HSUB * (j // HSUB)


def _hg_diag_tiles(b_sc, b3, r0, rowi):
    bi = b3[:, r0:r0 + SUB]
    return [jnp.exp2(jnp.where(rowi[:, _lo(s):] >= s, bi[:, _lo(s):] - b_sc[:, r0 + s:r0 + s + 1, :], NEG))
            for s in range(SUB)]


def _hg_diag_tiles_t(b_sc, b3, r0, rowi):
    bi = b3[:, r0:r0 + SUB]
    return [jnp.exp2(jnp.where(rowi[:, :_lo(t) + HSUB] <= t, b_sc[:, r0 + t:r0 + t + 1, :] - bi[:, :_lo(t) + HSUB], NEG))
            for t in range(SUB)]


def _lane_sums(pieces, ones):
    B = pieces[0].shape[0]
    hs = [p.shape[1] for p in pieces]
    R = _dot(jnp.concatenate(pieces, axis=1).reshape(B * sum(hs), 128), ones).reshape(B, sum(hs), 128)
    out, o = [], 0
    for h in hs:
        out.append(R[:, o:o + h])
        o += h
    return out


def _sum_tri(terms, low_rows):
    full = sum(t for t in terms if t.shape[1] == SUB)
    half = sum(t for t in terms if t.shape[1] == HSUB)
    lo, hi = full[:, :HSUB], full[:, HSUB:]
    return jnp.concatenate([lo + half, hi] if low_rows else [lo, hi + half], axis=1)


def _hgrn2_fwd(u, lb_logits, comms=None):
    T = u.shape[0]
    G = min(GC, T // CHUNK)
    R, B, N = G * CHUNK, G * HEADS, T // CHUNK

    def body(hq_ref, hf_ref, hv_ref, lg_ref, o_ref, st_ref, S_ref, b_sc, k_sc, v_sc):
        @pl.when(pl.program_id(0) == 0)
        def _():
            S_ref[...] = jnp.zeros_like(S_ref)

        pz = _hg_prep(hq_ref, hf_ref, lg_ref, b_sc, k_sc)
        q3, k3, b3, bl3 = pz["q3"], pz["k3"], pz["b3"], pz["bl3"]
        v3 = _heads_to_batch(hv_ref[...], 128)
        v_sc[...] = v3
        stloc = _bdot(v3, k3 * jnp.exp2(bl3 - b3), BTN).reshape(G, HEADS, 128, 128)
        dec = jnp.exp2(bl3).reshape(G, HEADS, 1, 128)
        ST = S_ref[...]
        sts = []
        for c in range(G):
            sts.append(ST)
            ST = ST * dec[c] + stloc[c]
        S_ref[...] = ST
        st4 = jnp.stack(sts, axis=0)
        st_ref[...] = st4
        o = _bdot(q3 * jnp.exp2(b3), st4.reshape(B, 128, 128), BNT)
        ones = jnp.ones((128, 128), F32)
        rowi = _iota((1, SUB, 128), 1)
        outs = []
        for i in range(CHUNK // SUB):
            r0 = SUB * i
            qi = q3[:, r0:r0 + SUB]
            oi = o[:, r0:r0 + SUB]
            if i > 0:
                r = b_sc[:, r0 - 1:r0, :]
                qe = qi * jnp.exp2(b3[:, r0:r0 + SUB] - r)
                ke = k3[:, :r0] * jnp.exp2(r - b3[:, :r0])
                oi = oi + _bdot(_bdot(qe, ke, BNT), v3[:, :r0], BNN)
            tiles = _hg_diag_tiles(b_sc, b3, r0, rowi)
            a_b = _lane_sums([qi[:, _lo(s):] * (k_sc[:, r0 + s:r0 + s + 1, :] * tiles[s]) for s in range(SUB)], ones)
            outs.append(oi + _sum_tri([a_b[s] * v_sc[:, r0 + s:r0 + s + 1, :] for s in range(SUB)], False))
        o_ref[...] = _batch_to_heads(jnp.concatenate(outs, axis=1))

    blk = lambda c: pl.BlockSpec((R, HALF), lambda n, c=c: (n, c // 4))
    return _hosted_call(
        body, comms, name="hgrn2_fwd", grid=(N // G,),
        in_specs=[blk(C_HQ), blk(C_HF), blk(C_HV), _full(lb_logits.shape)],
        out_specs=[pl.BlockSpec((R, HALF), lambda n: (n, 0)),
                   pl.BlockSpec((G, HEADS, 128, 128), lambda n: (n, 0, 0, 0))],
        out_shape=[_sds((T, HALF)), _sds((N, HEADS, 128, 128))],
        scratch_shapes=[pltpu.VMEM((HEADS, 128, 128), F32)] + [pltpu.VMEM((B, CHUNK, 128), F32)] * 3,
        args=(u, u, u, lb_logits))


def _hgrn2_bwd(u, lb_logits, do, states, comms=None):
    T = u.shape[0]
    G = min(GC, T // CHUNK)
    R, B, NG = G * CHUNK, G * HEADS, T // (G * CHUNK)

    def body(hq_ref, hf_ref, hv_ref, lg_ref, do_ref, st_ref, dhq_ref, dhf_ref, dhv_ref, dlb_ref,
             dS_ref, b_sc, k_sc, v_sc, q_sc, do_sc):
        @pl.when(pl.program_id(0) == 0)
        def _():
            dS_ref[...] = jnp.zeros_like(dS_ref)
            dlb_ref[...] = jnp.zeros_like(dlb_ref)

        pz = _hg_prep(hq_ref, hf_ref, lg_ref, b_sc, k_sc)
        q3, k3, b3, bl3, lb = pz["q3"], pz["k3"], pz["b3"], pz["bl3"], pz["lb"]
        v3 = _heads_to_batch(hv_ref[...], 128)
        v_sc[...] = v3
        do3 = _heads_to_batch(do_ref[...], 128)
        q_sc[...] = q3
        do_sc[...] = do3
        st3 = st_ref[...].reshape(B, 128, 128)
        eb = jnp.exp2(b3)
        ebl = jnp.exp2(bl3 - b3)
        qt = q3 * eb
        kl = k3 * ebl
        dstloc = _bdot(do3, qt, BTN).reshape(G, HEADS, 128, 128)
        dec = jnp.exp2(bl3).reshape(G, HEADS, 1, 128)
        dST = dS_ref[...]
        dsts = [None] * G
        for c in reversed(range(G)):
            dsts[c] = dST
            dST = dST * dec[c] + dstloc[c]
        dS_ref[...] = dST
        dst3 = jnp.stack(dsts, axis=0).reshape(B, 128, 128)
        dqt = _bdot(do3, st3, BNN)
        dkl = _bdot(v3, dst3, BNN)
        dv_acc = _bdot(kl, dst3, BNT)
        ones = jnp.ones((128, 128), F32)
        rowi = _iota((1, SUB, 128), 1)
        dq_parts, dk_parts, dv_parts = [], [], []
        dk_in = jnp.zeros((B, CHUNK, 128), F32)
        for i_s in range(CHUNK // SUB):
            r0 = SUB * i_s
            qi = q3[:, r0:r0 + SUB]
            doi = do3[:, r0:r0 + SUB]
            dqi = jnp.zeros((B, SUB, 128), F32)
            if i_s > 0:
                r = b_sc[:, r0 - 1:r0, :]
                eq = jnp.exp2(b3[:, r0:r0 + SUB] - r)
                ek = jnp.exp2(r - b3[:, :r0])
                qe = qi * eq
                ke = k3[:, :r0] * ek
                a_off = _bdot(qe, ke, BNT)
                p_off = _bdot(doi, v3[:, :r0], BNT)
                pad = jnp.zeros((B, CHUNK - r0, 128), F32)
                dv_acc = dv_acc + jnp.concatenate([_bdot(a_off, doi, BTN), pad], axis=1)
                dqi = dqi + _bdot(p_off, ke, BNN) * eq
                dk_in = dk_in + jnp.concatenate([_bdot(p_off, qe, BTN) * ek, pad], axis=1)
            ki, vi = k3[:, r0:r0 + SUB], v3[:, r0:r0 + SUB]
            rng = range(SUB)
            tiles = _hg_diag_tiles(b_sc, b3, r0, rowi)
            tiles_t = _hg_diag_tiles_t(b_sc, b3, r0, rowi)
            do_rows = [do_sc[:, r0 + t:r0 + t + 1, :] for t in rng]
            kts = [k_sc[:, r0 + s:r0 + s + 1, :] * tiles[s] for s in rng]
            qts = [q_sc[:, r0 + t:r0 + t + 1, :] * tiles_t[t] for t in rng]
            ps = [doi[:, _lo(s):] * v_sc[:, r0 + s:r0 + s + 1, :] for s in rng]
            mst = [ki[:, :_lo(t) + HSUB] * qts[t] for t in rng]
            pst = [vi[:, :_lo(t) + HSUB] * do_rows[t] for t in rng]
            sums = _lane_sums(ps + mst + pst, ones)
            p_b, a_t, p_t = sums[:SUB], sums[SUB:2 * SUB], sums[2 * SUB:]
            dq_parts.append(dqi + _sum_tri([p_b[s] * kts[s] for s in rng], False))
            dv_parts.append(_sum_tri([a_t[t] * do_rows[t] for t in rng], True))
            dk_parts.append(_sum_tri([p_t[t] * qts[t] for t in rng], True))
        dq_in = jnp.concatenate(dq_parts, axis=1)
        dk_in = dk_in + jnp.concatenate(dk_parts, axis=1)
        dv_acc = dv_acc + jnp.concatenate(dv_parts, axis=1)
        db = qt * dqt + q3 * dq_in - k3 * dk_in - kl * dkl
        last = jnp.sum(kl * dkl, axis=1, keepdims=True) + jnp.exp2(bl3) * jnp.sum(st3 * dst3, axis=1, keepdims=True)
        db = db + jnp.where(_iota((1, CHUNK, 1), 1) == CHUNK - 1, last, 0.0)
        dg = _chunk_cumsum(_batch_to_heads(db), pz["rowmod"], reverse=True)
        dq_tot = _batch_to_heads(dqt * eb + dq_in)
        dk_tot = _batch_to_heads(dkl * ebl + dk_in)
        common = dg / pz["f"] - dk_tot
        dhf_ref[...] = ((1.0 - lb) * pz["sg"] * pz["nsg"] * common).astype(dhf_ref.dtype)
        dl0 = _colsum(pz["nsg"] * common) * lb * (1.0 - lb)
        dlb_ref[0:1, :] += dl0
        dlb_ref[1:2, :] -= dl0
        dhq_ref[...] = (dq_tot * _dsilu(pz["hq"], pz["sq"])).astype(dhq_ref.dtype)
        dhv_ref[...] = _batch_to_heads(dv_acc).astype(dhv_ref.dtype)

    rev = lambda c: pl.BlockSpec((R, HALF), lambda i, c=c: (NG - 1 - i, c // 4))
    rev0 = pl.BlockSpec((R, HALF), lambda i: (NG - 1 - i, 0))
    return _hosted_call(
        body, comms, name="hgrn2_bwd", grid=(NG,),
        in_specs=[rev(C_HQ), rev(C_HF), rev(C_HV), _full(lb_logits.shape), rev0,
                  pl.BlockSpec((G, HEADS, 128, 128), lambda i: (NG - 1 - i, 0, 0, 0))],
        out_specs=[rev0, rev0, rev0, _full((2, HALF))],
        out_shape=[_sds((T, HALF), _MXU)] * 3 + [_sds((2, HALF))],
        scratch_shapes=[pltpu.VMEM((HEADS, 128, 128), F32)] + [pltpu.VMEM((B, CHUNK, 128), F32)] * 5,
        args=(u, u, u, lb_logits, do, states))


def _lanes_to_batch_cols(x, lane):
    G = x.shape[0] // CHUNK
    cols = [_lane_col(x, 4 + h, lane).reshape(G, CHUNK, 1) for h in range(HEADS)]
    return jnp.stack(cols, axis=1).reshape(G * HEADS, CHUNK, 1)


def _row_scalars(rows):
    lane = _iota((1, 128), 1)
    return jnp.stack([_rowsum(jnp.where(lane == 4 + h, r, 0.0)) for r in rows for h in range(HEADS)], axis=0)


def _ml_gates(gates):
    R = gates.shape[0]
    lane = _iota((R, 128), 1)
    rowmod = _iota((R, 128), 0) & (CHUNK - 1)
    lf = jnp.minimum(gates, 0.0) - jnp.log(1.0 + jnp.exp(-jnp.abs(gates)))
    g_all = _chunk_cumsum(lf, rowmod)
    x_all = pltpu.roll(gates, 4, 1) - g_all
    return g_all, x_all, lane, rowmod


def _ml_chunk_rows(g_all, x_all, mprev, g):
    gl = g_all[CHUNK * g + CHUNK - 8:CHUNK * (g + 1)]
    gl = _colsum(jnp.where(_iota((8, 128), 0) == 7, gl, 0.0))
    a = gl + x_all[CHUNK * g:CHUNK * (g + 1)]
    m_new = jnp.maximum(gl + mprev, jnp.max(a, axis=0, keepdims=True))
    return m_new, jnp.exp(gl + mprev - m_new), jnp.exp(a - m_new)


def _ml_batched(q3, k3, v3, g_all, x_all, lane, C3, n3, mprev3):
    G = g_all.shape[0] // CHUNK
    gcol3 = _lanes_to_batch_cols(g_all, lane)
    onehot = jnp.where(_iota((G, 8, 128), 1) + 4 == _iota((G, 8, 128), 2), 1.0, 0.0).astype(F32)
    rows = _bdotx(onehot, x_all.reshape(G, CHUNK, 128), BNT)
    sub = _iota((G, 8, CHUNK), 1)
    row3 = jnp.stack([jnp.sum(jnp.where(sub == h, rows, 0.0), axis=1, keepdims=True) for h in range(HEADS)],
                     axis=1).reshape(G * HEADS, 1, CHUNK)
    causal = _iota((1, CHUNK, CHUNK), 1) >= _iota((1, CHUNK, CHUNK), 2)
    dmat = jnp.where(causal, gcol3 + row3, NEG)
    m_inter = gcol3 + mprev3
    m_t = jnp.maximum(m_inter, jnp.max(dmat, axis=2, keepdims=True))
    wi = jnp.exp(dmat - m_t)
    wn = jnp.exp(m_inter - m_t)
    s3 = _bdot(q3, k3, BNT) * wi
    qc = _bdot(q3, C3, BNN)
    qn = jnp.sum(q3 * n3, axis=2, keepdims=True)
    num = _bdot(s3, v3, BNN) + wn * qc
    den = jnp.sum(s3, axis=2, keepdims=True) + wn * qn
    floor = jnp.exp(-m_t)
    return dict(wi=wi, wn=wn, s=s3, qc=qc, qn=qn, num=num, den=den, floor=floor, nrm=jnp.maximum(jnp.abs(den), floor))


def _mlstm_fwd(qkc, u, comms=None):
    T = u.shape[0]
    G = min(GC, T // CHUNK)
    R = G * CHUNK
    N = T // CHUNK

    def body(qk_ref, v_ref, g_ref, h_ref, cst_ref, nst_ref, mst_ref, C_ref, n_ref, m_ref):
        @pl.when(pl.program_id(0) == 0)
        def _():
            C_ref[...] = jnp.zeros_like(C_ref)
            n_ref[...] = jnp.zeros_like(n_ref)
            m_ref[...] = jnp.zeros_like(m_ref)

        g_all, x_all, lane, _ = _ml_gates(g_ref[...])
        m_row = m_ref[...]
        mprev_rows, wo_rows, ws_parts = [], [], []
        for g in range(G):
            mprev_rows.append(m_row)
            m_row, wo, ws = _ml_chunk_rows(g_all, x_all, m_row, g)
            wo_rows.append(wo)
            ws_parts.append(ws)
        m_ref[...] = m_row
        mst_ref[...] = jnp.stack(mprev_rows, axis=0)
        ws3 = _lanes_to_batch_cols(jnp.concatenate(ws_parts, axis=0), lane)
        wo4 = _row_scalars(wo_rows).reshape(G, HEADS, 1, 1)
        q3 = _heads_to_batch(qk_ref[:, :256] * ML_SCALE, ML_DQK)
        k3 = _heads_to_batch(qk_ref[:, 256:], ML_DQK)
        v3 = _heads_to_batch(v_ref[...], 128)
        kw = k3 * ws3
        cloc = _bdot(kw, v3, BTN).reshape(G, HEADS, ML_DQK, 128)
        nloc = jnp.sum(kw, axis=1, keepdims=True).reshape(G, HEADS, 1, ML_DQK)
        C, nn = C_ref[...], n_ref[...]
        cs, ns = [], []
        for g in range(G):
            cs.append(C)
            ns.append(nn)
            C = wo4[g] * C + cloc[g]
            nn = wo4[g] * nn + nloc[g]
        C_ref[...] = C
        n_ref[...] = nn
        c4, n4 = jnp.stack(cs, axis=0), jnp.stack(ns, axis=0)
        cst_ref[...] = c4
        nst_ref[...] = n4
        r = _ml_batched(q3, k3, v3, g_all, x_all, lane, c4.reshape(G * HEADS, ML_DQK, 128),
                        n4.reshape(G * HEADS, 1, ML_DQK), _row_scalars(mprev_rows))
        h_ref[...] = _batch_to_heads(r["num"] / r["nrm"])

    return _hosted_call(
        body, comms, name="mlstm_fwd", grid=(N // G,),
        in_specs=[pl.BlockSpec((R, HALF), lambda n: (n, 0)), pl.BlockSpec((R, HALF), lambda n: (n, C_MV // 4)),
                  pl.BlockSpec((R, 128), lambda n: (n, C_GATES))],
        out_specs=[pl.BlockSpec((R, HALF), lambda n: (n, 0)),
                   pl.BlockSpec((G, HEADS, ML_DQK, 128), lambda n: (n, 0, 0, 0)),
                   pl.BlockSpec((G, HEADS, 1, ML_DQK), lambda n: (n, 0, 0, 0)),
                   pl.BlockSpec((G, 1, 128), lambda n: (n, 0, 0))],
        out_shape=[_sds((T, HALF)), _sds((N, HEADS, ML_DQK, 128)), _sds((N, HEADS, 1, ML_DQK)), _sds((N, 1, 128))],
        scratch_shapes=[pltpu.VMEM((HEADS, ML_DQK, 128), F32), pltpu.VMEM((HEADS, 1, ML_DQK), F32),
                        pltpu.VMEM((1, 128), F32)],
        args=(qkc, u, u))


def _mlstm_bwd(qkc, u, dh, cst, nst, mst):
    T = u.shape[0]
    G = min(GC, T // CHUNK)
    R = G * CHUNK
    NG = T // R

    def body(qk_ref, v_ref, g_ref, dh_ref, cst_ref, nst_ref, mst_ref, dqk_ref, dv_ref, dgt_ref, dC_ref, dn_ref):
        @pl.when(pl.program_id(0) == 0)
        def _():
            dC_ref[...] = jnp.zeros_like(dC_ref)
            dn_ref[...] = jnp.zeros_like(dn_ref)

        B = G * HEADS
        gates = g_ref[...]
        g_all, x_all, lane, rowmod = _ml_gates(gates)
        mprev_rows = [mst_ref[g] for g in range(G)]
        wo_rows, ws_parts = [], []
        for g in range(G):
            _, wo, ws = _ml_chunk_rows(g_all, x_all, mprev_rows[g], g)
            wo_rows.append(wo)
            ws_parts.append(ws)
        ws3 = _lanes_to_batch_cols(jnp.concatenate(ws_parts, axis=0), lane)
        wo3 = _row_scalars(wo_rows)
        wo4 = wo3.reshape(G, HEADS, 1, 1)
        q3 = _heads_to_batch(qk_ref[:, :256] * ML_SCALE, ML_DQK)
        k3 = _heads_to_batch(qk_ref[:, 256:], ML_DQK)
        v3 = _heads_to_batch(v_ref[...], 128)
        dh3 = _heads_to_batch(dh_ref[...], 128)
        C3 = cst_ref[...].reshape(B, ML_DQK, 128)
        n3 = nst_ref[...].reshape(B, 1, ML_DQK)
        r = _ml_batched(q3, k3, v3, g_all, x_all, lane, C3, n3, _row_scalars(mprev_rows))
        wn, s3 = r["wn"], r["s"]
        inv = 1.0 / r["nrm"]
        dnum = dh3 * inv
        dnrm = -jnp.sum(dh3 * (r["num"] * inv), axis=2, keepdims=True) * inv
        dden = jnp.where(jnp.abs(r["den"]) > r["floor"], dnrm * jnp.sign(r["den"]), 0.0)
        ds = _bdot(dnum, v3, BNT) + dden
        dqk = ds * r["wi"]
        dd = ds * s3
        qw = q3 * wn
        dcloc = _bdot(qw, dnum, BTN).reshape(G, HEADS, ML_DQK, 128)
        dnloc = jnp.sum(qw * dden, axis=1, keepdims=True).reshape(G, HEADS, 1, ML_DQK)
        dC, dn = dC_ref[...], dn_ref[...]
        dcs, dns = [None] * G, [None] * G
        for g in reversed(range(G)):
            dcs[g], dns[g] = dC, dn
            dC = wo4[g] * dC + dcloc[g]
            dn = wo4[g] * dn + dnloc[g]
        dC_ref[...] = dC
        dn_ref[...] = dn
        dC3 = jnp.stack(dcs, axis=0).reshape(B, ML_DQK, 128)
        dn3 = jnp.stack(dns, axis=0).reshape(B, 1, ML_DQK)
        dk_st = ws3 * (_bdot(v3, dC3, BNT) + dn3)
        dq = _bdot(dqk, k3, BNN) + wn * (_bdot(dnum, C3, BNT) + dden * n3)
        dk = _bdot(dqk, q3, BTN) + dk_st
        dv = _bdot(s3, dnum, BTN) + ws3 * _bdot(k3, dC3, BNN)
        dv_ref[...] = _batch_to_heads(dv).astype(dv_ref.dtype)
        dqk_ref[...] = jnp.concatenate([_batch_to_heads(dq * ML_SCALE), _batch_to_heads(dk)], axis=1)
        e_col = wn * (jnp.sum(dnum * r["qc"], axis=2, keepdims=True) + dden * r["qn"])
        c_col = jnp.sum(k3 * dk_st, axis=2, keepdims=True)
        z = wo3 * (jnp.sum(dC3 * C3, axis=(1, 2), keepdims=True) + jnp.sum(dn3 * n3, axis=(1, 2), keepdims=True))
        dd_cols = _bdotx(dd, jnp.ones((B, CHUNK, 128), F32), BTN)[:, :, 0:1]
        last = _iota((1, CHUNK, 1), 1) == CHUNK - 1
        dg3 = jnp.sum(dd, axis=2, keepdims=True) - dd_cols + e_col - c_col
        dg3 = dg3 + jnp.where(last, jnp.sum(c_col, axis=1, keepdims=True) + z, 0.0)
        di3 = dd_cols + c_col

        def to_lanes(x3, first):
            x4 = x3.reshape(G, HEADS, CHUNK, 1)
            return sum(jnp.where(lane == first + h, x4[:, h].reshape(R, 1), 0.0) for h in range(HEADS))

        dlf = _chunk_cumsum(to_lanes(dg3, 4), rowmod, reverse=True)
        dgt_ref[...] = (to_lanes(di3, 0) + dlf * _sig(-gates)).astype(dgt_ref.dtype)

    rev = lambda w, c: pl.BlockSpec((R, w), lambda i, c=c: (NG - 1 - i, c))
    st = lambda *s: pl.BlockSpec((G,) + s, lambda i: (NG - 1 - i,) + (0,) * len(s))
    return pl.pallas_call(
        body, name="mlstm_bwd", grid=(NG,),
        in_specs=[rev(HALF, 0), rev(HALF, C_MV // 4), rev(128, C_GATES), rev(HALF, 0),
                  st(HEADS, ML_DQK, 128), st(HEADS, 1, ML_DQK), st(1, 128)],
        out_specs=[rev(HALF, 0), rev(HALF, 0), rev(128, 0)],
        out_shape=[_sds((T, HALF)), _sds((T, HALF), _MXU), _sds((T, 128), _MXU)],
        scratch_shapes=[pltpu.VMEM((HEADS, ML_DQK, 128), F32), pltpu.VMEM((HEADS, 1, ML_DQK), F32)],
        compiler_params=_cp("arbitrary"))(qkc, u, u, dh, cst, nst, mst)


def _head_norm(o):
    rs_parts, r_parts = [], []
    for h in range(HEADS):
        oh = o[:, 128 * h:128 * (h + 1)]
        rs = lax.rsqrt(jnp.mean(oh * oh, axis=-1, keepdims=True) + RMS_EPS)
        rs_parts.append(rs)
        r_parts.append(oh * rs)
    return jnp.concatenate(r_parts, axis=1), rs_parts


def _out_proj_ln(x, u, o_hg, h_ml, g_hg, g_ml, w_out, ln_g, ln_b, tm):
    T = x.shape[0]

    def body(x_ref, hgate_ref, mo_ref, ohg_ref, hml_ref, ghg_ref, gml_ref, w_ref, g_ref, b_ref,
             m_ref, z_ref, x1_ref):
        hgate = hgate_ref[...]
        a = _head_norm(ohg_ref[...])[0] * ghg_ref[...] * (hgate * _sig(hgate))
        b = _head_norm(hml_ref[...])[0] * gml_ref[...] * _sig(mo_ref[...])
        m = jnp.concatenate([a, b], axis=1)
        m_ref[...] = m.astype(m_ref.dtype)
        z = ALPHA * x_ref[...] + _dot(m, w_ref[...])
        z_ref[...] = z
        x1_ref[...] = _ln_fwd(z, g_ref[...], b_ref[...])[0]

    return pl.pallas_call(
        body, name="out_proj_ln1", grid=(T // tm,),
        in_specs=[_row(tm, D_MODEL), _row(tm, HALF, C_HGATE // 4), _row(tm, HALF, C_MO // 4),
                  _row(tm, HALF), _row(tm, HALF), _full(g_hg.shape), _full(g_ml.shape),
                  _full(w_out.shape), _full(ln_g.shape), _full(ln_b.shape)],
        out_specs=[_row(tm, D_MODEL)] * 3,
        out_shape=[_sds((T, D_MODEL), _MXU), _sds((T, D_MODEL)), _sds((T, D_MODEL))],
        compiler_params=_cp("parallel"))(x, u, u, o_hg, h_ml, g_hg, g_ml, w_out, ln_g, ln_b)


def _ffn_ln(x1, wg, wu, wd, ln_g, ln_b, tm):
    T = x1.shape[0]

    def body(x_ref, wg_ref, wu_ref, wd_ref, g_ref, b_ref, z_ref, x2_ref, a_ref, bb_ref, h_ref):
        x = x_ref[...]
        a = _dot(x, wg_ref[...])
        bb = _dot(x, wu_ref[...])
        hh = a * _sig(a) * bb
        a_ref[...] = a.astype(a_ref.dtype)
        bb_ref[...] = bb.astype(bb_ref.dtype)
        h_ref[...] = hh.astype(h_ref.dtype)
        z = ALPHA * x + _dot(hh, wd_ref[...])
        z_ref[...] = z
        x2_ref[...] = _ln_fwd(z, g_ref[...], b_ref[...])[0]

    return pl.pallas_call(
        body, name="ffn_ln2", grid=(T // tm,),
        in_specs=[_row(tm, D_MODEL), _full(wg.shape), _full(wu.shape), _full(wd.shape),
                  _full(ln_g.shape), _full(ln_b.shape)],
        out_specs=[_row(tm, D_MODEL)] * 2 + [_row(tm, D_FF)] * 3,
        out_shape=[_sds((T, D_MODEL))] * 2 + [_sds((T, D_FF), _MXU)] * 3,
        compiler_params=_cp("parallel"))(x1, wg, wu, wd, ln_g, ln_b)


def _ple_loss_ln2_bwd(x2, z2, p, tgt, wpg, bpg, wpp, ln_g, ln_b, tm):
    T = x2.shape[0]

    def body(x2_ref, z_ref, p_ref, t_ref, wpg_ref, bpg_ref, wpp_ref, g_ref, b_ref,
             de_ref, dgp_ref, dz_ref, loss_ref, dbpg_ref, dg_ref, db_ref):
        @pl.when(pl.program_id(0) == 0)
        def _():
            for r in (loss_ref, dbpg_ref, dg_ref, db_ref):
                r[...] = jnp.zeros_like(r)

        x2 = x2_ref[...]
        gate = _sig(_dot(x2, wpg_ref[...]) + bpg_ref[...])
        e = _dot(p_ref[...], wpp_ref[...])
        err = x2 + gate * e - t_ref[...]
        loss_ref[...] += _colsum(err * err)
        dy = err * (1.0 / D_MODEL)
        de_ref[...] = (dy * gate).astype(de_ref.dtype)
        dgp = dy * e * gate * (1.0 - gate)
        dgp_ref[...] = dgp.astype(dgp_ref.dtype)
        dbpg_ref[...] += _colsum(dgp)
        dx2 = dy + _dot(dgp, wpg_ref[...], NT)
        _, xhat, rstd = _ln_fwd(z_ref[...], g_ref[...], b_ref[...])
        dg_ref[...] += _colsum(dx2 * xhat)
        db_ref[...] += _colsum(dx2)
        dz_ref[...] = _ln_bwd(dx2, xhat, rstd, g_ref[...])

    vec = _full((1, D_MODEL))
    return pl.pallas_call(
        body, name="ple_loss_ln2_bwd", grid=(T // tm,),
        in_specs=[_row(tm, D_MODEL), _row(tm, D_MODEL), _row(tm, PLE_DIM), _row(tm, D_MODEL),
                  _full(wpg.shape), vec, _full(wpp.shape), vec, vec],
        out_specs=[_row(tm, D_MODEL)] * 3 + [vec] * 4,
        out_shape=[_sds((T, D_MODEL), _MXU)] * 2 + [_sds((T, D_MODEL))] + [_sds((1, D_MODEL))] * 4,
        compiler_params=_cp("arbitrary"))(x2, z2, p, tgt, wpg, bpg, wpp, ln_g, ln_b)


def _ffn_bwd_ln1_bwd(a_pre, b_pre, z1, dz2, wg, wu, wd, ln_g, ln_b, tm, comms=None):
    T = z1.shape[0]

    def body(a_ref, bb_ref, z_ref, dz2_ref, wg_ref, wu_ref, wd_ref, g_ref, b_ref,
             da_ref, dbb_ref, dz1_ref, dg_ref, db_ref):
        @pl.when(pl.program_id(0) == 0)
        def _():
            dg_ref[...] = jnp.zeros_like(dg_ref)
            db_ref[...] = jnp.zeros_like(db_ref)

        dz2 = dz2_ref[...]
        a = a_ref[...].astype(F32)
        bb = bb_ref[...].astype(F32)
        sa = _sig(a)
        act = a * sa
        dh = _dot(dz2, wd_ref[...], NT)
        da = (dh * bb * _dsilu(a, sa)).astype(da_ref.dtype)
        dbb = (dh * act).astype(dbb_ref.dtype)
        da_ref[...] = da
        dbb_ref[...] = dbb
        dx1 = ALPHA * dz2 + _dot(da, wg_ref[...], NT) + _dot(dbb, wu_ref[...], NT)
        _, xhat, rstd = _ln_fwd(z_ref[...], g_ref[...], b_ref[...])
        dg_ref[...] += _colsum(dx1 * xhat)
        db_ref[...] += _colsum(dx1)
        dz1_ref[...] = _ln_bwd(dx1, xhat, rstd, g_ref[...])

    vec = _full((1, D_MODEL))
    return _hosted_call(
        body, comms, name="ffn_bwd_ln1_bwd", grid=(T // tm,),
        in_specs=[_row(tm, D_FF)] * 2 + [_row(tm, D_MODEL)] * 2 + [_full(wg.shape), _full(wu.shape), _full(wd.shape), vec, vec],
        out_specs=[_row(tm, D_FF)] * 2 + [_row(tm, D_MODEL), vec, vec],
        out_shape=[_sds((T, D_FF), _MXU)] * 2 + [_sds((T, D_MODEL)), _sds((1, D_MODEL)), _sds((1, D_MODEL))],
        scratch_shapes=[], args=(a_pre, b_pre, z1, dz2, wg, wu, wd, ln_g, ln_b))


def _out_proj_bwd(dz1, u, o_hg, h_ml, g_hg, g_ml, w_out, tm):
    T = dz1.shape[0]

    def body(dz_ref, hgate_ref, mo_ref, ohg_ref, hml_ref, ghg_ref, gml_ref, w_ref,
             dohg_ref, dhml_ref, dhgate_ref, dmo_ref, dghg_ref, dgml_ref):
        @pl.when(pl.program_id(0) == 0)
        def _():
            dghg_ref[...] = jnp.zeros_like(dghg_ref)
            dgml_ref[...] = jnp.zeros_like(dgml_ref)

        dm = _dot(dz_ref[...], w_ref[...], NT)

        def half(dmh, o, gvec, gate_val, dgate_fac, do_ref, dgate_ref, dgvec_ref):
            r, rs = _head_norm(o)
            dgate_ref[...] = (dmh * r * gvec * dgate_fac).astype(dgate_ref.dtype)
            dn = dmh * gate_val
            dgvec_ref[...] += _colsum(dn * r)
            dr = dn * gvec
            parts = []
            for h in range(HEADS):
                sl = slice(128 * h, 128 * (h + 1))
                parts.append(rs[h] * (dr[:, sl] - r[:, sl] * jnp.mean(dr[:, sl] * r[:, sl], axis=-1, keepdims=True)))
            do_ref[...] = jnp.concatenate(parts, axis=1)

        hg = hgate_ref[...]
        shg = _sig(hg)
        half(dm[:, :HALF], ohg_ref[...], ghg_ref[...], hg * shg, _dsilu(hg, shg), dohg_ref, dhgate_ref, dghg_ref)
        smo = _sig(mo_ref[...])
        half(dm[:, HALF:], hml_ref[...], gml_ref[...], smo, smo * (1.0 - smo), dhml_ref, dmo_ref, dgml_ref)

    vec = _full((1, HALF))
    return pl.pallas_call(
        body, name="out_proj_bwd", grid=(T // tm,),
        in_specs=[_row(tm, D_MODEL), _row(tm, HALF, C_HGATE // 4), _row(tm, HALF, C_MO // 4),
                  _row(tm, HALF), _row(tm, HALF), vec, vec, _full(w_out.shape)],
        out_specs=[_row(tm, HALF)] * 4 + [vec, vec],
        out_shape=[_sds((T, HALF))] * 2 + [_sds((T, HALF), _MXU)] * 2 + [_sds((1, HALF))] * 2,
        compiler_params=_cp("arbitrary"))(dz1, u, u, o_hg, h_ml, g_hg, g_ml, w_out)


def _conv_bwd(u, pre, dqkc, cw, tm):
    T = u.shape[0]
    hb = tm // 8
    nb = T // 8

    def body(x_ref, xh_ref, pre_ref, preh_ref, d_ref, dh_ref, w_ref, dx_ref, dw_ref, db_ref):
        i = pl.program_id(0)

        @pl.when(i == 0)
        def _():
            dw_ref[...] = jnp.zeros_like(dw_ref)
            db_ref[...] = jnp.zeros_like(db_ref)

        def dpre_of(pre, d):
            return d * _dsilu(pre, _sig(pre))

        rowi = _iota((8, HALF), 0)
        dpre = dpre_of(pre_ref[...], d_ref[...])
        dpre_next = jnp.where(i < pl.num_programs(0) - 1, dpre_of(preh_ref[...], dh_ref[...]), 0.0)
        x = x_ref[...]
        xprev = jnp.where(i > 0, xh_ref[...], 0.0)
        dx = dpre * w_ref[3:4, :]
        db_ref[...] += _colsum(dpre)
        dws = [None] * 4
        dws[3] = _colsum(dpre * x)
        for j in (1, 2, 3):
            dx = dx + _shift_rows_up(dpre, dpre_next, j, rowi) * w_ref[3 - j:4 - j, :]
            dws[3 - j] = _colsum(dpre * _shift_rows(x, xprev, j, rowi))
        dx_ref[...] = dx.astype(dx_ref.dtype)
        dw_ref[...] += jnp.concatenate(dws, axis=0)

    cur = lambda blk: pl.BlockSpec((tm, HALF), lambda i, blk=blk: (i, blk))
    nxt = pl.BlockSpec((8, HALF), lambda i: (jnp.minimum((i + 1) * hb, nb - 1), 0))
    return pl.pallas_call(
        body, name="conv_bwd", grid=(T // tm,),
        in_specs=[cur(C_MQK // 4), pl.BlockSpec((8, HALF), lambda i: (jnp.maximum(i * hb - 1, 0), C_MQK // 4)),
                  cur(0), nxt, cur(0), nxt, _full(cw.shape)],
        out_specs=[cur(0), _full((4, HALF)), _full((1, HALF))],
        out_shape=[_sds((T, HALF), _MXU), _sds((4, HALF)), _sds((1, HALF))],
        compiler_params=_cp("arbitrary"))(u, u, pre, pre, dqkc, dqkc, cw)


def _du_specs(rows):
    return [pl.BlockSpec((rows, w), lambda i: (i, 0)) for w in DU_WIDTHS]


def _in_proj_bwd(dz1, du_parts, w, tm, comms=None):
    T = dz1.shape[0]

    def body(dz_ref, *refs):
        du = jnp.concatenate([r[...] for r in refs[:8]], axis=1)
        refs[9][...] = ALPHA * dz_ref[...] + _dot(du, refs[8][...], NT)

    (dx,), got = _hosted_call(
        body, comms, name="in_proj_bwd", grid=(T // tm,),
        in_specs=[_row(tm, D_MODEL)] + _du_specs(tm) + [_full(w.shape)],
        out_specs=[_row(tm, D_MODEL)], out_shape=[_sds((T, D_MODEL))], scratch_shapes=[], args=(dz1, *du_parts, w))
    return dx, got


def _wgrad(a, b, name, tm, tn, tk):
    T, M = a.shape
    N = b.shape[1]
    tm, tn, tk = min(tm, M), min(tn, N), min(tk, T)
    nk = T // tk

    def body(a_ref, b_ref, o_ref, acc_ref):
        kk = pl.program_id(2)

        @pl.when(kk == 0)
        def _():
            acc_ref[...] = jnp.zeros_like(acc_ref)

        acc_ref[...] += _dot(a_ref[...], b_ref[...], TN)

        @pl.when(kk == nk - 1)
        def _():
            o_ref[...] = acc_ref[...].astype(o_ref.dtype)

    return pl.pallas_call(
        body, name=name, grid=(M // tm, N // tn, nk),
        in_specs=[pl.BlockSpec((tk, tm), lambda i, j, kk: (kk, i)), pl.BlockSpec((tk, tn), lambda i, j, kk: (kk, j))],
        out_specs=pl.BlockSpec((tm, tn), lambda i, j, kk: (i, j)), out_shape=_sds((M, N), _MXU),
        scratch_shapes=[pltpu.VMEM((tm, tn), F32)],
        compiler_params=_cp("parallel", "parallel", "arbitrary"))(a, b)


W_IN_PARTS = 2


def _wgrad_w_in(x, du_parts, tk, part, comms=None):
    T = x.shape[0]
    M = D_MODEL // W_IN_PARTS
    tk = min(tk, T)
    nk = T // tk

    def body(a_ref, *refs):
        o_ref, cs_ref, acc_ref = refs[8:]
        kk = pl.program_id(0)

        @pl.when(kk == 0)
        def _():
            acc_ref[...] = jnp.zeros_like(acc_ref)
            cs_ref[...] = jnp.zeros_like(cs_ref)

        du = jnp.concatenate([r[...] for r in refs[:8]], axis=1)
        acc_ref[...] += _dot(a_ref[...], du, TN)
        cs_ref[...] += _colsum(du.astype(F32))

        @pl.when(kk == nk - 1)
        def _():
            o_ref[...] = acc_ref[...].astype(o_ref.dtype)

    return _hosted_call(
        body, comms, name="wgrad_w_in_%d" % part, grid=(nk,),
        in_specs=[pl.BlockSpec((tk, M), lambda kk: (kk, part))] + _du_specs(tk),
        out_specs=[_full((M, PROJ_WP)), _full((1, PROJ_WP))],
        out_shape=[_sds((M, PROJ_WP), _MXU), _sds((1, PROJ_WP))],
        scratch_shapes=[pltpu.VMEM((M, PROJ_WP), F32)], args=(x, *du_parts))


W_IN_S, FF_S, OUT_S, PP_S = PROJ_W // N_DEV, D_FF // N_DEV, D_MODEL // N_DEV, D_MODEL // N_DEV
LATE = ("w_ffn_gate", "w_ffn_up", "w_out", "w_ffn_down", "ple_w_gate", "ple_w_proj")
BIG = ("w_in",) + LATE


def _split_cols(a, n):
    return a.reshape(a.shape[0], N_DEV, n).transpose(1, 0, 2)


def _join_cols(a):
    return a.transpose(1, 0, 2).reshape(a.shape[1], -1)


def _step(x, p, tgt, w_in, b_in, lb_logits, conv_w, conv_b, g_hg, g_ml, ln1_g, ln1_b, ln2_g, ln2_b, bpg, late,
          distributed):
    T = x.shape[0]
    tm, tf = min(ROWS, T), min(ROWS_FFN, T)
    gather = lambda *names: [_GatherTwoLevel([late[n] for n in names])] if distributed else None
    scatter = lambda *arrs: [_Comm("scatter", list(arrs))] if distributed else None
    rows = lambda a, n: a.reshape(N_DEV, n, D_MODEL)
    (u, pre, qkc), got1 = _in_proj(x, w_in, b_in, conv_w, conv_b, tm, gather("w_out", "ple_w_gate", "ple_w_proj"))
    (o_hg, hg_states), got2 = _hgrn2_fwd(u, lb_logits, gather("w_ffn_gate", "w_ffn_up"))
    (h_ml, cst, nst, mst), got3 = _mlstm_fwd(qkc, u, gather("w_ffn_down"))
    if distributed:
        w_out, wpg, wpp = got1[0][0].reshape(D_MODEL, D_MODEL), got1[0][1].reshape(D_MODEL, D_MODEL), _join_cols(got1[0][2])
        wg, wu, wd = _join_cols(got2[0][0]), _join_cols(got2[0][1]), got3[0][0].reshape(D_FF, D_MODEL)
    else:
        w_out, wg, wu, wd, wpg, wpp = (late[n] for n in ("w_out", "w_ffn_gate", "w_ffn_up", "w_ffn_down", "ple_w_gate", "ple_w_proj"))
    m_in, z1, x1 = _out_proj_ln(x, u, o_hg, h_ml, g_hg, g_ml, w_out, ln1_g, ln1_b, tm)
    z2, x2, a_pre, b_pre, hh = _ffn_ln(x1, wg, wu, wd, ln2_g, ln2_b, tf)
    de, dgp, dz2, loss_vec, d_bpg, d_ln2g, d_ln2b = _ple_loss_ln2_bwd(x2, z2, p, tgt, wpg, bpg, wpp, ln2_g, ln2_b, tm)
    big = dict(ple_w_gate=_wgrad(x2, dgp, "wgrad_ple_gate", 512, D_MODEL, 1024),
               ple_w_proj=_wgrad(p, de, "wgrad_ple_proj", 512, D_MODEL, 1024))
    (da, dbb, dz1, d_ln1g, d_ln1b), r1 = _ffn_bwd_ln1_bwd(
        a_pre, b_pre, z1, dz2, wg, wu, wd, ln1_g, ln1_b, tf, scatter(rows(big["ple_w_gate"], OUT_S), _split_cols(big["ple_w_proj"], PP_S)))
    big.update(
        w_ffn_gate=_wgrad(x1, da, "wgrad_ffn_gate", 512, D_FF, 1024),
        w_ffn_up=_wgrad(x1, dbb, "wgrad_ffn_up", 512, D_FF, 1024),
        w_ffn_down=_wgrad(hh, dz2, "wgrad_ffn_down", D_FF, D_MODEL, 1024),
        w_out=_wgrad(m_in, dz1, "wgrad_w_out", 512, D_MODEL, 1024))
    d_ohg, d_hml, d_hgate, d_mo, d_ghg, d_gml = _out_proj_bwd(dz1, u, o_hg, h_ml, g_hg, g_ml, w_out, tm)
    (d_hq, d_hf, d_hv, d_lb), r2 = _hgrn2_bwd(
        u, lb_logits, d_ohg, hg_states,
        scatter(_split_cols(big["w_ffn_gate"], FF_S), _split_cols(big["w_ffn_up"], FF_S), rows(big["w_ffn_down"], FF_S),
                rows(big["w_out"], OUT_S)))
    d_qkc, d_mv, d_gates = _mlstm_bwd(qkc, u, d_hml, cst, nst, mst)
    d_mqk, d_convw, d_convb = _conv_bwd(u, pre, d_qkc, conv_w, tm)
    du_parts = [d_hq, d_hf, d_hv, d_hgate, d_mqk, d_mv, d_mo, d_gates]
    own = lambda g: _split_cols(g[:, :PROJ_W], W_IN_S)
    (g_in0, d_bin), _ = _wgrad_w_in(x, du_parts, 512, 0)
    (g_in1, _), r_in0 = _wgrad_w_in(x, du_parts, 512, 1, scatter(own(g_in0)))
    big["w_in"] = jnp.concatenate([g_in0, g_in1], axis=0)
    small = dict(b_in=d_bin[:, :PROJ_W], hg_lb_logits=d_lb, ml_conv_w=d_convw, ml_conv_b=d_convb, hg_norm_g=d_ghg,
                 ml_norm_g=d_gml, ln1_g=d_ln1g, ln1_b=d_ln1b, ln2_g=d_ln2g, ln2_b=d_ln2b, ple_b_gate=d_bpg)
    last = [_Comm("scatter", [own(g_in1)]), _Comm("gather", [loss_vec] + [small[n] for n in SMALL])] if distributed else None
    dx, r3 = _in_proj_bwd(dz1, du_parts, w_in, tm, last)
    gathered_small = None
    if distributed:
        big = dict(ple_w_gate=r1[0][0], ple_w_proj=r1[0][1], w_ffn_gate=r2[0][0], w_ffn_up=r2[0][1],
                   w_ffn_down=r2[0][2], w_out=r2[0][3], w_in=[r_in0[0][0], r3[0][0]])
        gathered_small = r3[1]
    return loss_vec, dx, big, small, gathered_small


SMALL = ("b_in", "hg_lb_logits", "ml_conv_w", "ml_conv_b", "hg_norm_g", "ml_norm_g", "ln1_g", "ln1_b", "ln2_g", "ln2_b",
         "ple_b_gate")


def _padc(a, n):
    return jnp.pad(a, [(0, 0)] * (a.ndim - 1) + [(0, n - a.shape[-1])])


def _adamw(w, g, m, v):
    m = B1 * m + (1.0 - B1) * g
    v = B2 * v + (1.0 - B2) * jnp.square(g)
    m_hat = m / (1.0 - B1 ** STEP)
    v_hat = v / (1.0 - B2 ** STEP)
    return -LR * (m_hat / (jnp.sqrt(v_hat) + EPS) + WD * w), m, v


def _sum_slabs(ref):
    g = ref[0].astype(F32)
    for j in range(1, N_DEV):
        g = g + ref[j].astype(F32)
    return g


def _adamw_matrix(rbs, w, m, v, name):
    rbs = list(rbs) if isinstance(rbs, (list, tuple)) else [rbs]
    nb = len(rbs)
    R, C = w.shape
    tr = 256 if (R // nb) % 256 == 0 else R // nb
    per = R // nb // tr

    def body(*refs):
        w_ref, m_ref, v_ref, g_ref, d_ref, m2_ref, v2_ref = refs[nb:]
        i = pl.program_id(0)
        g = _sum_slabs(refs[0])
        for k in range(1, nb):
            g = jnp.where(i >= k * per, _sum_slabs(refs[k]), g)
        g_ref[...] = g
        d_ref[...], m2_ref[...], v2_ref[...] = _adamw(w_ref[...], g, m_ref[...], v_ref[...])

    blk = pl.BlockSpec((tr, C), lambda i: (i, 0))
    part = lambda k: pl.BlockSpec((N_DEV, tr, C), lambda i, k=k: (0, jnp.clip(i - k * per, 0, per - 1), 0))
    return pl.pallas_call(
        body, name=name, grid=(R // tr,),
        in_specs=[part(k) for k in range(nb)] + [blk, blk, blk],
        out_specs=[blk] * 4, out_shape=[_sds((R, C))] * 4, compiler_params=_cp("parallel"))(*rbs, w, m, v)


def _adamw_small(loss_g, gs, ws, ms, vs):
    n = len(ws)

    def body(*refs):
        loss_ref, g_refs, w_refs, m_refs, v_refs = refs[0], refs[1:1 + n], refs[1 + n:1 + 2 * n], refs[1 + 2 * n:1 + 3 * n], refs[1 + 3 * n:1 + 4 * n]
        outs = refs[1 + 4 * n:]
        outs[0][...] = (0.5 / D_MODEL) * jnp.sum(_sum_slabs(loss_ref), keepdims=True)
        for i in range(n):
            g = _sum_slabs(g_refs[i])
            outs[1 + i][...] = g
            outs[1 + n + i][...], outs[1 + 2 * n + i][...], outs[1 + 3 * n + i][...] = _adamw(
                w_refs[i][...], g, m_refs[i][...], v_refs[i][...])

    res = pl.pallas_call(
        body, name="adamw_small", out_shape=[_sds((1, 1))] + [_sds(w.shape) for w in ws] * 4)(loss_g, *gs, *ws, *ms, *vs)
    return res[0], [res[1 + k * n:1 + (k + 1) * n] for k in range(4)]


WEIGHTS = ("w_in", "b_in", "hg_lb_logits", "ml_conv_w", "ml_conv_b", "hg_norm_g", "ml_norm_g", "w_out", "ln1_g", "ln1_b",
           "w_ffn_gate", "w_ffn_up", "w_ffn_down", "ln2_g", "ln2_b", "ple_w_proj", "ple_w_gate", "ple_b_gate")
CONV_S = HALF // N_DEV


def kernel(x, p, w_in, b_in, hg_lb_logits, ml_conv_w, ml_conv_b, hg_norm_g, ml_norm_g, w_out, ln1_g, ln1_b, w_ffn_gate, w_ffn_up, w_ffn_down, ln2_g, ln2_b, ple_w_proj, ple_w_gate, ple_b_gate, loss_target, m_w_in, m_b_in, m_hg_lb_logits, m_ml_conv_w, m_ml_conv_b, m_hg_norm_g, m_ml_norm_g, m_w_out, m_ln1_g, m_ln1_b, m_w_ffn_gate, m_w_ffn_up, m_w_ffn_down, m_ln2_g, m_ln2_b, m_ple_w_proj, m_ple_w_gate, m_ple_b_gate, v_w_in, v_b_in, v_hg_lb_logits, v_ml_conv_w, v_ml_conv_b, v_hg_norm_g, v_ml_norm_g, v_w_out, v_ln1_g, v_ln1_b, v_w_ffn_gate, v_w_ffn_up, v_w_ffn_down, v_ln2_g, v_ln2_b, v_ple_w_proj, v_ple_w_gate, v_ple_b_gate):
    args = locals()
    me = 4 * lax.axis_index("x") + 2 * lax.axis_index("y") + lax.axis_index("c")
    shapes = {n: args[n].shape for n in WEIGHTS}
    drop = lambda n, a: a[0] if n in BIG or n == "ml_conv_w" else a
    W = {n: drop(n, args[n]) for n in WEIGHTS}
    M = {n: drop(n, args["m_" + n]) for n in WEIGHTS}
    V = {n: drop(n, args["v_" + n]) for n in WEIGHTS}

    g_in, g_conv = _gather_two_level(
        [W["w_in"].astype(_MXU), jnp.pad(W["ml_conv_w"], ((0, 4), (0, 128 - CONV_S)))], "gather_w_in")
    w_in_full = _padc(_join_cols(g_in), PROJ_WP)
    conv_full = _join_cols(g_conv[:, :4, :CONV_S])

    _, dx, big, _, sg = _step(
        x[0], p[0, 0], loss_target[0], w_in_full, _padc(b_in, PROJ_WP), hg_lb_logits, conv_full, ml_conv_b,
        hg_norm_g, ml_norm_g, ln1_g, ln1_b, ln2_g, ln2_b, ple_b_gate, {n: W[n].astype(_MXU) for n in LATE}, True)

    upd = {n: _adamw_matrix(big[n], W[n], M[n], V[n], "adamw_" + n) for n in BIG}
    sg = dict(zip(SMALL, sg[1:]), loss=sg[0])
    sg["ml_conv_w"] = lax.dynamic_slice(sg["ml_conv_w"], (0, 0, me * CONV_S), (N_DEV, 4, CONV_S))
    loss, small_upd = _adamw_small(sg["loss"], *[[d[n] for n in SMALL] for d in (sg, W, M, V)])

    outs = []
    for kind in range(4):
        smalls = dict(zip(SMALL, small_upd[kind]))
        for n in WEIGHTS:
            outs.append((upd[n][kind] if n in BIG else smalls[n]).reshape(shapes[n]))
    return (loss.reshape(()), dx.reshape(x.shape), *outs)
```

```python
import jax
import jax.numpy as jnp
from jax import lax
from jax.experimental import pallas as pl
from jax.experimental.pallas import tpu as pltpu

F32 = jnp.float32
_MXU = jnp.bfloat16

D_MODEL = 1024
CHUNK = 64
SUB = 16
PLE_DIM = 256
HEADS = 4
ML_DQK = 64
HALF = 512
D_FF = 2816
PROJ_W = 3592
PROJ_WP = 3712
ALPHA = float(2 ** 0.25)
LN_EPS = 1e-5
RMS_EPS = 1e-6
ML_SCALE = ML_DQK ** -0.5
N_DEV = 8
LR, B1, B2, EPS, WD, STEP = 0.001, 0.9, 0.999, 1e-08, 0.01, 10
NEG = -1e30
LOG2E = 1.4426950408889634

C_HQ, C_HF, C_HV, C_HGATE, C_MQK, C_MV, C_MO, C_GATES = 0, 4, 8, 12, 16, 20, 24, 28
DU_WIDTHS = (HALF,) * 7 + (128,)

VMEM_LIMIT = 52 * 1024 * 1024
GC = 8
ROWS = 512
ROWS_FFN = 256

NN = (((1,), (0,)), ((), ()))
NT = (((1,), (1,)), ((), ()))
TN = (((0,), (0,)), ((), ()))
BNT = (((2,), (2,)), ((0,), (0,)))
BNN = (((2,), (1,)), ((0,), (0,)))
BTN = (((1,), (1,)), ((0,), (0,)))


def _dot(a, b, dims=NN):
    return lax.dot_general(a.astype(_MXU), b.astype(_MXU), dims, preferred_element_type=F32)


def _dotx(a, b, dims=NN):
    return lax.dot_general(a, b, dims, precision=lax.Precision.HIGHEST, preferred_element_type=F32)


def _sig(x):
    return jax.nn.sigmoid(x)


def _cp(*sem):
    return pltpu.CompilerParams(dimension_semantics=sem, vmem_limit_bytes=VMEM_LIMIT)


def _row(tm, c, blk=0):
    return pl.BlockSpec((tm, c), lambda i, blk=blk: (i, blk))


def _full(shape):
    nd = len(shape)
    return pl.BlockSpec(tuple(shape), lambda *_, nd=nd: (0,) * nd)


def _sds(shape, dtype=F32):
    return jax.ShapeDtypeStruct(tuple(shape), dtype)


def _iota(shape, axis):
    return lax.broadcasted_iota(jnp.int32, shape, axis)


def _colsum(x):
    return jnp.sum(x, axis=0, keepdims=True)


def _rowsum(x):
    return jnp.sum(x, axis=1, keepdims=True)


def _ln_fwd(z, g, b):
    mu = jnp.mean(z, axis=-1, keepdims=True)
    zc = z - mu
    var = jnp.mean(zc * zc, axis=-1, keepdims=True)
    rstd = lax.rsqrt(var + LN_EPS)
    xhat = zc * rstd
    return xhat * g + b, xhat, rstd


def _ln_bwd(dy, xhat, rstd, g):
    dxh = dy * g
    m1 = jnp.mean(dxh, axis=-1, keepdims=True)
    m2 = jnp.mean(dxh * xhat, axis=-1, keepdims=True)
    return rstd * (dxh - m1 - xhat * m2)


def _dsilu(x, s):
    return s * (1.0 + x * (1.0 - s))


MESH = pl.DeviceIdType.MESH
ANY = pl.BlockSpec(memory_space=pl.ANY)


def _flip(v, bit):
    return 1 - v if bit else v


class _Comm:
    def __init__(self, kind, srcs):
        self.kind, self.srcs, self.n = kind, list(srcs), len(srcs)

    def out_shape(self):
        lead = (N_DEV,) if self.kind == "gather" else ()
        return [jax.ShapeDtypeStruct(lead + s.shape, s.dtype) for s in self.srcs]

    def scratch(self):
        return [pltpu.SemaphoreType.DMA((7 * self.n,)), pltpu.SemaphoreType.DMA((7 * self.n,)),
                pltpu.SemaphoreType.DMA((self.n,))]

    def copies(self, srcs, dsts, send_sems, recv_sems, local_sems):
        x, y, c = lax.axis_index("x"), lax.axis_index("y"), lax.axis_index("c")
        me = 4 * x + 2 * y + c
        pick = (lambda s, j: s) if self.kind == "gather" else (lambda s, j: s.at[j])
        out = []
        for i, (s, d) in enumerate(zip(srcs, dsts)):
            out.append(pltpu.make_async_copy(pick(s, me), d.at[me], local_sems.at[i]))
            for k in range(1, N_DEV):
                px, py, pc = _flip(x, k & 4), _flip(y, k & 2), _flip(c, k & 1)
                out.append(pltpu.make_async_remote_copy(
                    src_ref=pick(s, 4 * px + 2 * py + pc), dst_ref=d.at[me], send_sem=send_sems.at[7 * i + k - 1],
                    recv_sem=recv_sems.at[7 * i + k - 1], device_id=(px, py, pc), device_id_type=MESH))
        return out

    def start(self, *refs):
        for cp in self.copies(*refs):
            cp.start()

    def mid(self, *refs):
        pass

    def finish(self, *refs):
        for cp in self.copies(*refs):
            cp.wait()


class _GatherTwoLevel(_Comm):
    def __init__(self, srcs):
        super().__init__("gather", srcs)

    def _parts(self, srcs, dsts, send_sems, recv_sems, local_sems):
        x, y, c = lax.axis_index("x"), lax.axis_index("y"), lax.axis_index("c")
        me, sibling = (x, y, c), (x, y, 1 - c)
        chips = [(1 - x, y), (x, 1 - y), (1 - x, 1 - y)]

        def copy(i, k, block, to, own=False):
            slab = dsts[i].at[4 * block[0] + 2 * block[1] + block[2]]
            return pltpu.make_async_remote_copy(
                src_ref=srcs[i] if own else slab, dst_ref=slab, send_sem=send_sems.at[7 * i + k],
                recv_sem=recv_sems.at[7 * i + k], device_id=to, device_id_type=MESH)

        n = range(self.n)
        mine = [pltpu.make_async_copy(srcs[i], dsts[i].at[4 * x + 2 * y + c], local_sems.at[i]) for i in n]
        first = [copy(i, 0, me, sibling, own=True) for i in n]
        first += [copy(i, 1 + j, me, (*chip, c), own=True) for j, chip in enumerate(chips) for i in n]
        over_ici = [copy(i, 1 + j, (*chip, c), me) for j, chip in enumerate(chips) for i in n]
        passed = [copy(i, 4 + j, (*chip, c), sibling) for j, chip in enumerate(chips) for i in n]
        from_sibling = [copy(i, 0, sibling, me) for i in n]
        from_sibling += [copy(i, 4 + j, (*chip, 1 - c), me) for j, chip in enumerate(chips) for i in n]
        return mine, first, over_ici, passed, from_sibling

    def start(self, *refs):
        mine, first, _, _, _ = self._parts(*refs)
        for cp in mine + first:
            cp.start()

    def mid(self, *refs):
        _, _, over_ici, passed, _ = self._parts(*refs)
        for arrived, onward in zip(over_ici, passed):
            arrived.wait_recv()
            onward.start()

    def finish(self, *refs):
        mine, first, _, passed, from_sibling = self._parts(*refs)
        for cp in from_sibling:
            cp.wait_recv()
        for cp in first + passed:
            cp.wait_send()
        for cp in mine:
            cp.wait()


def _hosted_call(body, comms, *, name, grid, in_specs, out_specs, out_shape, scratch_shapes, args):
    comms = list(comms or [])
    if not comms:
        res = pl.pallas_call(body, name=name, grid=grid, in_specs=in_specs, out_specs=out_specs, out_shape=out_shape,
                             scratch_shapes=scratch_shapes, compiler_params=_cp("arbitrary"))(*args)
        return list(res), []
    n_in, n_out, n_sc, nc = len(in_specs), len(out_specs), len(scratch_shapes), sum(cm.n for cm in comms)
    last = grid[0] - 1

    def hosted(*refs):
        ins, csrc = refs[:n_in], refs[n_in:n_in + nc]
        o0 = n_in + nc
        outs, cdst = refs[o0:o0 + n_out], refs[o0 + n_out:o0 + n_out + nc]
        s0 = o0 + n_out + nc
        scr, sems = refs[s0:s0 + n_sc], refs[s0 + n_sc:]

        def phase(which):
            o = 0
            for j, cm in enumerate(comms):
                getattr(cm, which)(csrc[o:o + cm.n], cdst[o:o + cm.n], *sems[3 * j:3 * j + 3])
                o += cm.n

        i = pl.program_id(0)

        @pl.when(i == 0)
        def _():
            phase("start")

        body(*ins, *outs, *scr)

        @pl.when(i == (2 * last) // 3)
        def _():
            phase("mid")

        @pl.when(i == last)
        def _():
            phase("finish")

    res = pl.pallas_call(
        hosted, name=name, grid=grid, in_specs=list(in_specs) + [ANY] * nc, out_specs=list(out_specs) + [ANY] * nc,
        out_shape=list(out_shape) + [s for cm in comms for s in cm.out_shape()],
        scratch_shapes=list(scratch_shapes) + [s for cm in comms for s in cm.scratch()],
        compiler_params=_cp("arbitrary"))(*args, *[a for cm in comms for a in cm.srcs])
    got, o = [], n_out
    for cm in comms:
        got.append(list(res[o:o + cm.n]))
        o += cm.n
    return list(res[:n_out]), got


def _gather_two_level(blocks, name):
    n = len(blocks)

    def body(*refs):
        x_refs, out_refs = refs[:n], refs[n:2 * n]
        send_sems, recv_sems, local_sems = refs[2 * n:]
        x, y, c = lax.axis_index("x"), lax.axis_index("y"), lax.axis_index("c")
        me, sibling = (x, y, c), (x, y, 1 - c)
        chips = [(1 - x, y), (x, 1 - y), (1 - x, 1 - y)]

        def copy(i, k, block, to, own=False):
            slab = out_refs[i].at[4 * block[0] + 2 * block[1] + block[2]]
            return pltpu.make_async_remote_copy(
                src_ref=x_refs[i] if own else slab, dst_ref=slab, send_sem=send_sems.at[7 * i + k],
                recv_sem=recv_sems.at[7 * i + k], device_id=to, device_id_type=MESH)

        mine = [pltpu.make_async_copy(x_refs[i], out_refs[i].at[4 * x + 2 * y + c], local_sems.at[i]) for i in range(n)]
        for cp in mine:
            cp.start()
        first = [copy(i, 0, me, sibling, own=True) for i in range(n)]
        first += [copy(i, 1 + j, me, (*chip, c), own=True) for j, chip in enumerate(chips) for i in range(n)]
        for cp in first:
            cp.start()
        passed = []
        for j, chip in enumerate(chips):
            for i in range(n):
                copy(i, 1 + j, (*chip, c), me).wait_recv()
                passed.append(copy(i, 4 + j, (*chip, c), sibling))
                passed[-1].start()
        for i in range(n):
            copy(i, 0, sibling, me).wait_recv()
            for j, chip in enumerate(chips):
                copy(i, 4 + j, (*chip, 1 - c), me).wait_recv()
        for cp in first + passed:
            cp.wait_send()
        for cp in mine:
            cp.wait()

    return pl.pallas_call(
        body, name=name, out_shape=[jax.ShapeDtypeStruct((N_DEV,) + b.shape, b.dtype) for b in blocks],
        in_specs=[ANY] * n, out_specs=[ANY] * n,
        scratch_shapes=[pltpu.SemaphoreType.DMA((7 * n,)), pltpu.SemaphoreType.DMA((7 * n,)),
                        pltpu.SemaphoreType.DMA((n,))])(*blocks)


def _in_proj(x, w, b, cw, cb, tm, comms=None):
    T = x.shape[0]

    def body(x_ref, w_ref, b_ref, cw_ref, cb_ref, o_ref, pre_ref, act_ref, halo_sc):
        @pl.when(pl.program_id(0) == 0)
        def _():
            halo_sc[...] = jnp.zeros_like(halo_sc)

        o = _dot(x_ref[...], w_ref[...]) + b_ref[...]
        o_ref[...] = o
        xc = o[:, 128 * C_MQK:128 * C_MQK + HALF]
        halo = halo_sc[...]
        rowi = _iota((8, HALF), 0)
        acc = xc * cw_ref[3:4, :] + cb_ref[...]
        for j in (1, 2, 3):
            acc = acc + _shift_rows(xc, halo, j, rowi) * cw_ref[3 - j:4 - j, :]
        pre_ref[...] = acc
        act_ref[...] = acc * _sig(acc)
        halo_sc[...] = xc[tm - 8:]

    return _hosted_call(
        body, comms, name="in_proj", grid=(T // tm,),
        in_specs=[_row(tm, D_MODEL), _full(w.shape), _full(b.shape), _full(cw.shape), _full(cb.shape)],
        out_specs=[_row(tm, PROJ_WP), _row(tm, HALF), _row(tm, HALF)],
        out_shape=[_sds((T, PROJ_WP)), _sds((T, HALF)), _sds((T, HALF))],
        scratch_shapes=[pltpu.VMEM((8, HALF), F32)], args=(x, w, b, cw, cb))


def _shift_rows(x, halo, j, rowi):
    r = pltpu.roll(x, j, 0)
    top = jnp.where(rowi < j, pltpu.roll(halo, j, 0), r[:8])
    return jnp.concatenate([top, r[8:]], axis=0)


def _shift_rows_up(x, halo, j, rowi):
    n = x.shape[0]
    r = pltpu.roll(x, n - j, 0)
    bot = jnp.where(rowi >= 8 - j, pltpu.roll(halo, 8 - j, 0), r[n - 8:])
    return jnp.concatenate([r[:n - 8], bot], axis=0)


def _bdot(a, b, dims):
    return lax.dot_general(a.astype(_MXU), b.astype(_MXU), dims, preferred_element_type=F32)


def _bdotx(a, b, dims):
    return lax.dot_general(a, b, dims, precision=lax.Precision.HIGHEST, preferred_element_type=F32)


def _heads_to_batch(x, w):
    G = x.shape[0] // CHUNK
    x3 = x.reshape(G, CHUNK, HEADS * w)
    return jnp.stack([x3[:, :, w * h:w * (h + 1)] for h in range(HEADS)], axis=1).reshape(G * HEADS, CHUNK, w)


def _batch_to_heads(x3):
    B, _, w = x3.shape
    x4 = x3.reshape(B // HEADS, HEADS, CHUNK, w)
    return jnp.concatenate([x4[:, h] for h in range(HEADS)], axis=-1).reshape(B // HEADS * CHUNK, HEADS * w)


def _chunk_cumsum(x, rowmod, reverse=False):
    R = x.shape[0]
    for sh in (1, 2, 4, 8, 16, 32):
        if reverse:
            x = x + jnp.where(rowmod < CHUNK - sh, pltpu.roll(x, R - sh, 0), 0.0)
        else:
            x = x + jnp.where(rowmod >= sh, pltpu.roll(x, sh, 0), 0.0)
    return x


def _lane_col(x, c, lane):
    return _rowsum(jnp.where(lane == c, x, 0.0))


def _hg_gates(hq, hf, lb):
    sg = _sig(hf)
    nsg = _sig(-hf)
    f = lb + (1.0 - lb) * sg
    g = jnp.log(f)
    k = (1.0 - lb) * nsg
    sq = _sig(hq)
    return hq * sq, g, k, f, sg, nsg, sq


def _hg_prep(hq_ref, hf_ref, lg_ref, b_sc, k_sc):
    R = hq_ref.shape[0]
    G = R // CHUNK
    lb = _sig(lg_ref[0:1, :] - lg_ref[1:2, :])
    hq = hq_ref[...]
    q, g, k, f, sg, nsg, sq = _hg_gates(hq, hf_ref[...], lb)
    rowmod = _iota((R, HALF), 0) & (CHUNK - 1)
    b = _chunk_cumsum(g, rowmod) * LOG2E
    last8 = _iota((8, HALF), 0) == 7
    bl_rows = [_colsum(jnp.where(last8, b[CHUNK * c + CHUNK - 8:CHUNK * (c + 1)], 0.0)) for c in range(G)]
    bl3 = jnp.stack([r[:, 128 * h:128 * (h + 1)] for r in bl_rows for h in range(HEADS)], axis=0)
    b3, k3 = _heads_to_batch(b, 128), _heads_to_batch(k, 128)
    b_sc[...] = b3
    k_sc[...] = k3
    return dict(G=G, lb=lb, hq=hq, f=f, sg=sg, nsg=nsg, sq=sq, rowmod=rowmod, q3=_heads_to_batch(q, 128), k3=k3, b3=b3,
                bl3=bl3)


HSUB = SUB // 2


def _lo(j):
    return HSUB * (j // HSUB)


def _hg_diag_tiles(b_sc, b3, r0, rowi):
    bi = b3[:, r0:r0 + SUB]
    return [jnp.exp2(jnp.where(rowi[:, _lo(s):] >= s, bi[:, _lo(s):] - b_sc[:, r0 + s:r0 + s + 1, :], NEG))
            for s in range(SUB)]


def _hg_diag_tiles_t(b_sc, b3, r0, rowi):
    bi = b3[:, r0:r0 + SUB]
    return [jnp.exp2(jnp.where(rowi[:, :_lo(t) + HSUB] <= t, b_sc[:, r0 + t:r0 + t + 1, :] - bi[:, :_lo(t) + HSUB], NEG))
            for t in range(SUB)]


def _lane_sums(pieces, ones):
    B = pieces[0].shape[0]
    hs = [p.shape[1] for p in pieces]
    R = _dot(jnp.concatenate(pieces, axis=1).reshape(B * sum(hs), 128), ones).reshape(B, sum(hs), 128)
    out, o = [], 0
    for h in hs:
        out.append(R[:, o:o + h])
        o += h
    return out


def _sum_tri(terms, low_rows):
    full = sum(t for t in terms if t.shape[1] == SUB)
    half = sum(t for t in terms if t.shape[1] == HSUB)
    lo, hi = full[:, :HSUB], full[:, HSUB:]
    return jnp.concatenate([lo + half, hi] if low_rows else [lo, hi + half], axis=1)


def _hgrn2_fwd(u, lb_logits, comms=None):
    T = u.shape[0]
    G = min(GC, T // CHUNK)
    R, B, N = G * CHUNK, G * HEADS, T // CHUNK

    def body(hq_ref, hf_ref, hv_ref, lg_ref, o_ref, st_ref, S_ref, b_sc, k_sc, v_sc):
        @pl.when(pl.program_id(0) == 0)
        def _():
            S_ref[...] = jnp.zeros_like(S_ref)

        pz = _hg_prep(hq_ref, hf_ref, lg_ref, b_sc, k_sc)
        q3, k3, b3, bl3 = pz["q3"], pz["k3"], pz["b3"], pz["bl3"]
        v3 = _heads_to_batch(hv_ref[...], 128)
        v_sc[...] = v3
        stloc = _bdot(v3, k3 * jnp.exp2(bl3 - b3), BTN).reshape(G, HEADS, 128, 128)
        dec = jnp.exp2(bl3).reshape(G, HEADS, 1, 128)
        ST = S_ref[...]
        sts = []
        for c in range(G):
            sts.append(ST)
            ST = ST * dec[c] + stloc[c]
        S_ref[...] = ST
        st4 = jnp.stack(sts, axis=0)
        st_ref[...] = st4
        o = _bdot(q3 * jnp.exp2(b3), st4.reshape(B, 128, 128), BNT)
        ones = jnp.ones((128, 128), F32)
        rowi = _iota((1, SUB, 128), 1)
        outs = []
        for i in range(CHUNK // SUB):
            r0 = SUB * i
            qi = q3[:, r0:r0 + SUB]
            oi = o[:, r0:r0 + SUB]
            if i > 0:
                r = b_sc[:, r0 - 1:r0, :]
                qe = qi * jnp.exp2(b3[:, r0:r0 + SUB] - r)
                ke = k3[:, :r0] * jnp.exp2(r - b3[:, :r0])
                oi = oi + _bdot(_bdot(qe, ke, BNT), v3[:, :r0], BNN)
            tiles = _hg_diag_tiles(b_sc, b3, r0, rowi)
            a_b = _lane_sums([qi[:, _lo(s):] * (k_sc[:, r0 + s:r0 + s + 1, :] * tiles[s]) for s in range(SUB)], ones)
            outs.append(oi + _sum_tri([a_b[s] * v_sc[:, r0 + s:r0 + s + 1, :] for s in range(SUB)], False))
        o_ref[...] = _batch_to_heads(jnp.concatenate(outs, axis=1))

    blk = lambda c: pl.BlockSpec((R, HALF), lambda n, c=c: (n, c // 4))
    return _hosted_call(
        body, comms, name="hgrn2_fwd", grid=(N // G,),
        in_specs=[blk(C_HQ), blk(C_HF), blk(C_HV), _full(lb_logits.shape)],
        out_specs=[pl.BlockSpec((R, HALF), lambda n: (n, 0)),
                   pl.BlockSpec((G, HEADS, 128, 128), lambda n: (n, 0, 0, 0))],
        out_shape=[_sds((T, HALF)), _sds((N, HEADS, 128, 128))],
        scratch_shapes=[pltpu.VMEM((HEADS, 128, 128), F32)] + [pltpu.VMEM((B, CHUNK, 128), F32)] * 3,
        args=(u, u, u, lb_logits))


def _hgrn2_bwd(u, lb_logits, do, states, comms=None):
    T = u.shape[0]
    G = min(GC, T // CHUNK)
    R, B, NG = G * CHUNK, G * HEADS, T // (G * CHUNK)

    def body(hq_ref, hf_ref, hv_ref, lg_ref, do_ref, st_ref, dhq_ref, dhf_ref, dhv_ref, dlb_ref,
             dS_ref, b_sc, k_sc, v_sc, q_sc, do_sc):
        @pl.when(pl.program_id(0) == 0)
        def _():
            dS_ref[...] = jnp.zeros_like(dS_ref)
            dlb_ref[...] = jnp.zeros_like(dlb_ref)

        pz = _hg_prep(hq_ref, hf_ref, lg_ref, b_sc, k_sc)
        q3, k3, b3, bl3, lb = pz["q3"], pz["k3"], pz["b3"], pz["bl3"], pz["lb"]
        v3 = _heads_to_batch(hv_ref[...], 128)
        v_sc[...] = v3
        do3 = _heads_to_batch(do_ref[...], 128)
        q_sc[...] = q3
        do_sc[...] = do3
        st3 = st_ref[...].reshape(B, 128, 128)
        eb = jnp.exp2(b3)
        ebl = jnp.exp2(bl3 - b3)
        qt = q3 * eb
        kl = k3 * ebl
        dstloc = _bdot(do3, qt, BTN).reshape(G, HEADS, 128, 128)
        dec = jnp.exp2(bl3).reshape(G, HEADS, 1, 128)
        dST = dS_ref[...]
        dsts = [None] * G
        for c in reversed(range(G)):
            dsts[c] = dST
            dST = dST * dec[c] + dstloc[c]
        dS_ref[...] = dST
        dst3 = jnp.stack(dsts, axis=0).reshape(B, 128, 128)
        dqt = _bdot(do3, st3, BNN)
        dkl = _bdot(v3, dst3, BNN)
        dv_acc = _bdot(kl, dst3, BNT)
        ones = jnp.ones((128, 128), F32)
        rowi = _iota((1, SUB, 128), 1)
        dq_parts, dk_parts, dv_parts = [], [], []
        dk_in = jnp.zeros((B, CHUNK, 128), F32)
        for i_s in range(CHUNK // SUB):
            r0 = SUB * i_s
            qi = q3[:, r0:r0 + SUB]
            doi = do3[:, r0:r0 + SUB]
            dqi = jnp.zeros((B, SUB, 128), F32)
            if i_s > 0:
                r = b_sc[:, r0 - 1:r0, :]
                eq = jnp.exp2(b3[:, r0:r0 + SUB] - r)
                ek = jnp.exp2(r - b3[:, :r0])
                qe = qi * eq
                ke = k3[:, :r0] * ek
                a_off = _bdot(qe, ke, BNT)
                p_off = _bdot(doi, v3[:, :r0], BNT)
                pad = jnp.zeros((B, CHUNK - r0, 128), F32)
                dv_acc = dv_acc + jnp.concatenate([_bdot(a_off, doi, BTN), pad], axis=1)
                dqi = dqi + _bdot(p_off, ke, BNN) * eq
                dk_in = dk_in + jnp.concatenate([_bdot(p_off, qe, BTN) * ek, pad], axis=1)
            ki, vi = k3[:, r0:r0 + SUB], v3[:, r0:r0 + SUB]
            rng = range(SUB)
            tiles = _hg_diag_tiles(b_sc, b3, r0, rowi)
            tiles_t = _hg_diag_tiles_t(b_sc, b3, r0, rowi)
            do_rows = [do_sc[:, r0 + t:r0 + t + 1, :] for t in rng]
            kts = [k_sc[:, r0 + s:r0 + s + 1, :] * tiles[s] for s in rng]
            qts = [q_sc[:, r0 + t:r0 + t + 1, :] * tiles_t[t] for t in rng]
            ps = [doi[:, _lo(s):] * v_sc[:, r0 + s:r0 + s + 1, :] for s in rng]
            mst = [ki[:, :_lo(t) + HSUB] * qts[t] for t in rng]
            pst = [vi[:, :_lo(t) + HSUB] * do_rows[t] for t in rng]
            sums = _lane_sums(ps + mst + pst, ones)
            p_b, a_t, p_t = sums[:SUB], sums[SUB:2 * SUB], sums[2 * SUB:]
            dq_parts.append(dqi + _sum_tri([p_b[s] * kts[s] for s in rng], False))
            dv_parts.append(_sum_tri([a_t[t] * do_rows[t] for t in rng], True))
            dk_parts.append(_sum_tri([p_t[t] * qts[t] for t in rng], True))
        dq_in = jnp.concatenate(dq_parts, axis=1)
        dk_in = dk_in + jnp.concatenate(dk_parts, axis=1)
        dv_acc = dv_acc + jnp.concatenate(dv_parts, axis=1)
        db = qt * dqt + q3 * dq_in - k3 * dk_in - kl * dkl
        last = jnp.sum(kl * dkl, axis=1, keepdims=True) + jnp.exp2(bl3) * jnp.sum(st3 * dst3, axis=1, keepdims=True)
        db = db + jnp.where(_iota((1, CHUNK, 1), 1) == CHUNK - 1, last, 0.0)
        dg = _chunk_cumsum(_batch_to_heads(db), pz["rowmod"], reverse=True)
        dq_tot = _batch_to_heads(dqt * eb + dq_in)
        dk_tot = _batch_to_heads(dkl * ebl + dk_in)
        common = dg / pz["f"] - dk_tot
        dhf_ref[...] = ((1.0 - lb) * pz["sg"] * pz["nsg"] * common).astype(dhf_ref.dtype)
        dl0 = _colsum(pz["nsg"] * common) * lb * (1.0 - lb)
        dlb_ref[0:1, :] += dl0
        dlb_ref[1:2, :] -= dl0
        dhq_ref[...] = (dq_tot * _dsilu(pz["hq"], pz["sq"])).astype(dhq_ref.dtype)
        dhv_ref[...] = _batch_to_heads(dv_acc).astype(dhv_ref.dtype)

    rev = lambda c: pl.BlockSpec((R, HALF), lambda i, c=c: (NG - 1 - i, c // 4))
    rev0 = pl.BlockSpec((R, HALF), lambda i: (NG - 1 - i, 0))
    return _hosted_call(
        body, comms, name="hgrn2_bwd", grid=(NG,),
        in_specs=[rev(C_HQ), rev(C_HF), rev(C_HV), _full(lb_logits.shape), rev0,
                  pl.BlockSpec((G, HEADS, 128, 128), lambda i: (NG - 1 - i, 0, 0, 0))],
        out_specs=[rev0, rev0, rev0, _full((2, HALF))],
        out_shape=[_sds((T, HALF), _MXU)] * 3 + [_sds((2, HALF))],
        scratch_shapes=[pltpu.VMEM((HEADS, 128, 128), F32)] + [pltpu.VMEM((B, CHUNK, 128), F32)] * 5,
        args=(u, u, u, lb_logits, do, states))


def _lanes_to_batch_cols(x, lane):
    G = x.shape[0] // CHUNK
    cols = [_lane_col(x, 4 + h, lane).reshape(G, CHUNK, 1) for h in range(HEADS)]
    return jnp.stack(cols, axis=1).reshape(G * HEADS, CHUNK, 1)


def _row_scalars(rows):
    lane = _iota((1, 128), 1)
    return jnp.stack([_rowsum(jnp.where(lane == 4 + h, r, 0.0)) for r in rows for h in range(HEADS)], axis=0)


def _ml_gates(gates):
    R = gates.shape[0]
    lane = _iota((R, 128), 1)
    rowmod = _iota((R, 128), 0) & (CHUNK - 1)
    lf = jnp.minimum(gates, 0.0) - jnp.log(1.0 + jnp.exp(-jnp.abs(gates)))
    g_all = _chunk_cumsum(lf, rowmod)
    x_all = pltpu.roll(gates, 4, 1) - g_all
    return g_all, x_all, lane, rowmod


def _ml_chunk_rows(g_all, x_all, mprev, g):
    gl = g_all[CHUNK * g + CHUNK - 8:CHUNK * (g + 1)]
    gl = _colsum(jnp.where(_iota((8, 128), 0) == 7, gl, 0.0))
    a = gl + x_all[CHUNK * g:CHUNK * (g + 1)]
    m_new = jnp.maximum(gl + mprev, jnp.max(a, axis=0, keepdims=True))
    return m_new, jnp.exp(gl + mprev - m_new), jnp.exp(a - m_new)


def _ml_batched(q3, k3, v3, g_all, x_all, lane, C3, n3, mprev3):
    G = g_all.shape[0] // CHUNK
    gcol3 = _lanes_to_batch_cols(g_all, lane)
    onehot = jnp.where(_iota((G, 8, 128), 1) + 4 == _iota((G, 8, 128), 2), 1.0, 0.0).astype(F32)
    rows = _bdotx(onehot, x_all.reshape(G, CHUNK, 128), BNT)
    sub = _iota((G, 8, CHUNK), 1)
    row3 = jnp.stack([jnp.sum(jnp.where(sub == h, rows, 0.0), axis=1, keepdims=True) for h in range(HEADS)],
                     axis=1).reshape(G * HEADS, 1, CHUNK)
    causal = _iota((1, CHUNK, CHUNK), 1) >= _iota((1, CHUNK, CHUNK), 2)
    dmat = jnp.where(causal, gcol3 + row3, NEG)
    m_inter = gcol3 + mprev3
    m_t = jnp.maximum(m_inter, jnp.max(dmat, axis=2, keepdims=True))
    wi = jnp.exp(dmat - m_t)
    wn = jnp.exp(m_inter - m_t)
    s3 = _bdot(q3, k3, BNT) * wi
    qc = _bdot(q3, C3, BNN)
    qn = jnp.sum(q3 * n3, axis=2, keepdims=True)
    num = _bdot(s3, v3, BNN) + wn * qc
    den = jnp.sum(s3, axis=2, keepdims=True) + wn * qn
    floor = jnp.exp(-m_t)
    return dict(wi=wi, wn=wn, s=s3, qc=qc, qn=qn, num=num, den=den, floor=floor, nrm=jnp.maximum(jnp.abs(den), floor))


def _mlstm_fwd(qkc, u, comms=None):
    T = u.shape[0]
    G = min(GC, T // CHUNK)
    R = G * CHUNK
    N = T // CHUNK

    def body(qk_ref, v_ref, g_ref, h_ref, cst_ref, nst_ref, mst_ref, C_ref, n_ref, m_ref):
        @pl.when(pl.program_id(0) == 0)
        def _():
            C_ref[...] = jnp.zeros_like(C_ref)
            n_ref[...] = jnp.zeros_like(n_ref)
            m_ref[...] = jnp.zeros_like(m_ref)

        g_all, x_all, lane, _ = _ml_gates(g_ref[...])
        m_row = m_ref[...]
        mprev_rows, wo_rows, ws_parts = [], [], []
        for g in range(G):
            mprev_rows.append(m_row)
            m_row, wo, ws = _ml_chunk_rows(g_all, x_all, m_row, g)
            wo_rows.append(wo)
            ws_parts.append(ws)
        m_ref[...] = m_row
        mst_ref[...] = jnp.stack(mprev_rows, axis=0)
        ws3 = _lanes_to_batch_cols(jnp.concatenate(ws_parts, axis=0), lane)
        wo4 = _row_scalars(wo_rows).reshape(G, HEADS, 1, 1)
        q3 = _heads_to_batch(qk_ref[:, :256] * ML_SCALE, ML_DQK)
        k3 = _heads_to_batch(qk_ref[:, 256:], ML_DQK)
        v3 = _heads_to_batch(v_ref[...], 128)
        kw = k3 * ws3
        cloc = _bdot(kw, v3, BTN).reshape(G, HEADS, ML_DQK, 128)
        nloc = jnp.sum(kw, axis=1, keepdims=True).reshape(G, HEADS, 1, ML_DQK)
        C, nn = C_ref[...], n_ref[...]
        cs, ns = [], []
        for g in range(G):
            cs.append(C)
            ns.append(nn)
            C = wo4[g] * C + cloc[g]
            nn = wo4[g] * nn + nloc[g]
        C_ref[...] = C
        n_ref[...] = nn
        c4, n4 = jnp.stack(cs, axis=0), jnp.stack(ns, axis=0)
        cst_ref[...] = c4
        nst_ref[...] = n4
        r = _ml_batched(q3, k3, v3, g_all, x_all, lane, c4.reshape(G * HEADS, ML_DQK, 128),
                        n4.reshape(G * HEADS, 1, ML_DQK), _row_scalars(mprev_rows))
        h_ref[...] = _batch_to_heads(r["num"] / r["nrm"])

    return _hosted_call(
        body, comms, name="mlstm_fwd", grid=(N // G,),
        in_specs=[pl.BlockSpec((R, HALF), lambda n: (n, 0)), pl.BlockSpec((R, HALF), lambda n: (n, C_MV // 4)),
                  pl.BlockSpec((R, 128), lambda n: (n, C_GATES))],
        out_specs=[pl.BlockSpec((R, HALF), lambda n: (n, 0)),
                   pl.BlockSpec((G, HEADS, ML_DQK, 128), lambda n: (n, 0, 0, 0)),
                   pl.BlockSpec((G, HEADS, 1, ML_DQK), lambda n: (n, 0, 0, 0)),
                   pl.BlockSpec((G, 1, 128), lambda n: (n, 0, 0))],
        out_shape=[_sds((T, HALF)), _sds((N, HEADS, ML_DQK, 128)), _sds((N, HEADS, 1, ML_DQK)), _sds((N, 1, 128))],
        scratch_shapes=[pltpu.VMEM((HEADS, ML_DQK, 128), F32), pltpu.VMEM((HEADS, 1, ML_DQK), F32),
                        pltpu.VMEM((1, 128), F32)],
        args=(qkc, u, u))


def _mlstm_bwd(qkc, u, dh, cst, nst, mst):
    T = u.shape[0]
    G = min(GC, T // CHUNK)
    R = G * CHUNK
    NG = T // R

    def body(qk_ref, v_ref, g_ref, dh_ref, cst_ref, nst_ref, mst_ref, dqk_ref, dv_ref, dgt_ref, dC_ref, dn_ref):
        @pl.when(pl.program_id(0) == 0)
        def _():
            dC_ref[...] = jnp.zeros_like(dC_ref)
            dn_ref[...] = jnp.zeros_like(dn_ref)

        B = G * HEADS
        gates = g_ref[...]
        g_all, x_all, lane, rowmod = _ml_gates(gates)
        mprev_rows = [mst_ref[g] for g in range(G)]
        wo_rows, ws_parts = [], []
        for g in range(G):
            _, wo, ws = _ml_chunk_rows(g_all, x_all, mprev_rows[g], g)
            wo_rows.append(wo)
            ws_parts.append(ws)
        ws3 = _lanes_to_batch_cols(jnp.concatenate(ws_parts, axis=0), lane)
        wo3 = _row_scalars(wo_rows)
        wo4 = wo3.reshape(G, HEADS, 1, 1)
        q3 = _heads_to_batch(qk_ref[:, :256] * ML_SCALE, ML_DQK)
        k3 = _heads_to_batch(qk_ref[:, 256:], ML_DQK)
        v3 = _heads_to_batch(v_ref[...], 128)
        dh3 = _heads_to_batch(dh_ref[...], 128)
        C3 = cst_ref[...].reshape(B, ML_DQK, 128)
        n3 = nst_ref[...].reshape(B, 1, ML_DQK)
        r = _ml_batched(q3, k3, v3, g_all, x_all, lane, C3, n3, _row_scalars(mprev_rows))
        wn, s3 = r["wn"], r["s"]
        inv = 1.0 / r["nrm"]
        dnum = dh3 * inv
        dnrm = -jnp.sum(dh3 * (r["num"] * inv), axis=2, keepdims=True) * inv
        dden = jnp.where(jnp.abs(r["den"]) > r["floor"], dnrm * jnp.sign(r["den"]), 0.0)
        ds = _bdot(dnum, v3, BNT) + dden
        dqk = ds * r["wi"]
        dd = ds * s3
        qw = q3 * wn
        dcloc = _bdot(qw, dnum, BTN).reshape(G, HEADS, ML_DQK, 128)
        dnloc = jnp.sum(qw * dden, axis=1, keepdims=True).reshape(G, HEADS, 1, ML_DQK)
        dC, dn = dC_ref[...], dn_ref[...]
        dcs, dns = [None] * G, [None] * G
        for g in reversed(range(G)):
            dcs[g], dns[g] = dC, dn
            dC = wo4[g] * dC + dcloc[g]
            dn = wo4[g] * dn + dnloc[g]
        dC_ref[...] = dC
        dn_ref[...] = dn
        dC3 = jnp.stack(dcs, axis=0).reshape(B, ML_DQK, 128)
        dn3 = jnp.stack(dns, axis=0).reshape(B, 1, ML_DQK)
        dk_st = ws3 * (_bdot(v3, dC3, BNT) + dn3)
        dq = _bdot(dqk, k3, BNN) + wn * (_bdot(dnum, C3, BNT) + dden * n3)
        dk = _bdot(dqk, q3, BTN) + dk_st
        dv = _bdot(s3, dnum, BTN) + ws3 * _bdot(k3, dC3, BNN)
        dv_ref[...] = _batch_to_heads(dv).astype(dv_ref.dtype)
        dqk_ref[...] = jnp.concatenate([_batch_to_heads(dq * ML_SCALE), _batch_to_heads(dk)], axis=1)
        e_col = wn * (jnp.sum(dnum * r["qc"], axis=2, keepdims=True) + dden * r["qn"])
        c_col = jnp.sum(k3 * dk_st, axis=2, keepdims=True)
        z = wo3 * (jnp.sum(dC3 * C3, axis=(1, 2), keepdims=True) + jnp.sum(dn3 * n3, axis=(1, 2), keepdims=True))
        dd_hi = dd.astype(_MXU).astype(F32)
        ones = jnp.ones((B, CHUNK, 128), F32)
        dd_cols = (_bdot(dd_hi, ones, BTN) + _bdot(dd - dd_hi, ones, BTN))[:, :, 0:1]
        last = _iota((1, CHUNK, 1), 1) == CHUNK - 1
        dg3 = jnp.sum(dd, axis=2, keepdims=True) - dd_cols + e_col - c_col
        dg3 = dg3 + jnp.where(last, jnp.sum(c_col, axis=1, keepdims=True) + z, 0.0)
        di3 = dd_cols + c_col

        def to_lanes(x3, first):
            x4 = x3.reshape(G, HEADS, CHUNK, 1)
            return sum(jnp.where(lane == first + h, x4[:, h].reshape(R, 1), 0.0) for h in range(HEADS))

        dlf = _chunk_cumsum(to_lanes(dg3, 4), rowmod, reverse=True)
        dgt_ref[...] = (to_lanes(di3, 0) + dlf * _sig(-gates)).astype(dgt_ref.dtype)

    rev = lambda w, c: pl.BlockSpec((R, w), lambda i, c=c: (NG - 1 - i, c))
    st = lambda *s: pl.BlockSpec((G,) + s, lambda i: (NG - 1 - i,) + (0,) * len(s))
    return pl.pallas_call(
        body, name="mlstm_bwd", grid=(NG,),
        in_specs=[rev(HALF, 0), rev(HALF, C_MV // 4), rev(128, C_GATES), rev(HALF, 0),
                  st(HEADS, ML_DQK, 128), st(HEADS, 1, ML_DQK), st(1, 128)],
        out_specs=[rev(HALF, 0), rev(HALF, 0), rev(128, 0)],
        out_shape=[_sds((T, HALF)), _sds((T, HALF), _MXU), _sds((T, 128), _MXU)],
        scratch_shapes=[pltpu.VMEM((HEADS, ML_DQK, 128), F32), pltpu.VMEM((HEADS, 1, ML_DQK), F32)],
        compiler_params=_cp("arbitrary"))(qkc, u, u, dh, cst, nst, mst)


def _head_norm(o):
    rs_parts, r_parts = [], []
    for h in range(HEADS):
        oh = o[:, 128 * h:128 * (h + 1)]
        rs = lax.rsqrt(jnp.mean(oh * oh, axis=-1, keepdims=True) + RMS_EPS)
        rs_parts.append(rs)
        r_parts.append(oh * rs)
    return jnp.concatenate(r_parts, axis=1), rs_parts


def _out_proj(x, u, o_hg, h_ml, g_hg, g_ml, w_out, tm):
    T = x.shape[0]

    def body(x_ref, hgate_ref, mo_ref, ohg_ref, hml_ref, ghg_ref, gml_ref, w_ref, m_ref, z_ref):
        hgate = hgate_ref[...]
        a = _head_norm(ohg_ref[...])[0] * ghg_ref[...] * (hgate * _sig(hgate))
        b = _head_norm(hml_ref[...])[0] * gml_ref[...] * _sig(mo_ref[...])
        m = jnp.concatenate([a, b], axis=1)
        m_ref[...] = m.astype(m_ref.dtype)
        z_ref[...] = ALPHA * x_ref[...] + _dot(m, w_ref[...])

    return pl.pallas_call(
        body, name="out_proj", grid=(T // tm,),
        in_specs=[_row(tm, D_MODEL), _row(tm, HALF, C_HGATE // 4), _row(tm, HALF, C_MO // 4),
                  _row(tm, HALF), _row(tm, HALF), _full(g_hg.shape), _full(g_ml.shape), _full(w_out.shape)],
        out_specs=[_row(tm, D_MODEL)] * 2,
        out_shape=[_sds((T, D_MODEL), _MXU), _sds((T, D_MODEL))],
        compiler_params=_cp("parallel"))(x, u, u, o_hg, h_ml, g_hg, g_ml, w_out)


def _ffn_ln(z1, ln1_g, ln1_b, wg, wu, wd, ln_g, ln_b, tm):
    T = z1.shape[0]

    def body(z1_ref, g1_ref, b1_ref, wg_ref, wu_ref, wd_ref, g_ref, b_ref, z_ref, x1_ref, x2_ref, a_ref, bb_ref, h_ref):
        x = _ln_fwd(z1_ref[...], g1_ref[...], b1_ref[...])[0]
        x1_ref[...] = x.astype(x1_ref.dtype)
        a = _dot(x, wg_ref[...])
        bb = _dot(x, wu_ref[...])
        hh = a * _sig(a) * bb
        a_ref[...] = a.astype(a_ref.dtype)
        bb_ref[...] = bb.astype(bb_ref.dtype)
        h_ref[...] = hh.astype(h_ref.dtype)
        z = ALPHA * x + _dot(hh, wd_ref[...])
        z_ref[...] = z
        x2_ref[...] = _ln_fwd(z, g_ref[...], b_ref[...])[0].astype(x2_ref.dtype)

    vec = _full((1, D_MODEL))
    return pl.pallas_call(
        body, name="ffn_ln2", grid=(T // tm,),
        in_specs=[_row(tm, D_MODEL), vec, vec, _full(wg.shape), _full(wu.shape), _full(wd.shape), vec, vec],
        out_specs=[_row(tm, D_MODEL)] * 3 + [_row(tm, D_FF)] * 3,
        out_shape=[_sds((T, D_MODEL))] + [_sds((T, D_MODEL), _MXU)] * 2 + [_sds((T, D_FF), _MXU)] * 3,
        compiler_params=_cp("parallel"))(z1, ln1_g, ln1_b, wg, wu, wd, ln_g, ln_b)


def _ple_loss_ln2_bwd(z2, p, tgt, wpg, bpg, wpp, ln_g, ln_b, tm):
    T = z2.shape[0]

    def body(z_ref, p_ref, t_ref, wpg_ref, bpg_ref, wpp_ref, g_ref, b_ref,
             de_ref, dgp_ref, dz_ref, loss_ref, dbpg_ref, dg_ref, db_ref):
        @pl.when(pl.program_id(0) == 0)
        def _():
            for r in (loss_ref, dbpg_ref, dg_ref, db_ref):
                r[...] = jnp.zeros_like(r)

        x2, xhat, rstd = _ln_fwd(z_ref[...], g_ref[...], b_ref[...])
        gate = _sig(_dot(x2, wpg_ref[...]) + bpg_ref[...])
        e = _dot(p_ref[...], wpp_ref[...])
        err = x2 + gate * e - t_ref[...]
        loss_ref[...] += _colsum(err * err)
        dy = err * (1.0 / D_MODEL)
        de_ref[...] = (dy * gate).astype(de_ref.dtype)
        dgp = dy * e * gate * (1.0 - gate)
        dgp_ref[...] = dgp.astype(dgp_ref.dtype)
        dbpg_ref[...] += _colsum(dgp)
        dx2 = dy + _dot(dgp, wpg_ref[...], NT)
        dg_ref[...] += _colsum(dx2 * xhat)
        db_ref[...] += _colsum(dx2)
        dz_ref[...] = _ln_bwd(dx2, xhat, rstd, g_ref[...])

    vec = _full((1, D_MODEL))
    return pl.pallas_call(
        body, name="ple_loss_ln2_bwd", grid=(T // tm,),
        in_specs=[_row(tm, D_MODEL), _row(tm, PLE_DIM), _row(tm, D_MODEL),
                  _full(wpg.shape), vec, _full(wpp.shape), vec, vec],
        out_specs=[_row(tm, D_MODEL)] * 3 + [vec] * 4,
        out_shape=[_sds((T, D_MODEL), _MXU)] * 2 + [_sds((T, D_MODEL))] + [_sds((1, D_MODEL))] * 4,
        compiler_params=_cp("arbitrary"))(z2, p, tgt, wpg, bpg, wpp, ln_g, ln_b)


def _ffn_bwd_ln1_bwd(a_pre, b_pre, z1, dz2, wg, wu, wd, ln_g, ln_b, tm, comms=None):
    T = z1.shape[0]

    def body(a_ref, bb_ref, z_ref, dz2_ref, wg_ref, wu_ref, wd_ref, g_ref, b_ref,
             da_ref, dbb_ref, dz1_ref, dg_ref, db_ref):
        @pl.when(pl.program_id(0) == 0)
        def _():
            dg_ref[...] = jnp.zeros_like(dg_ref)
            db_ref[...] = jnp.zeros_like(db_ref)

        dz2 = dz2_ref[...]
        a = a_ref[...].astype(F32)
        bb = bb_ref[...].astype(F32)
        sa = _sig(a)
        act = a * sa
        dh = _dot(dz2, wd_ref[...], NT)
        da = (dh * bb * _dsilu(a, sa)).astype(da_ref.dtype)
        dbb = (dh * act).astype(dbb_ref.dtype)
        da_ref[...] = da
        dbb_ref[...] = dbb
        dx1 = ALPHA * dz2 + _dot(da, wg_ref[...], NT) + _dot(dbb, wu_ref[...], NT)
        _, xhat, rstd = _ln_fwd(z_ref[...], g_ref[...], b_ref[...])
        dg_ref[...] += _colsum(dx1 * xhat)
        db_ref[...] += _colsum(dx1)
        dz1_ref[...] = _ln_bwd(dx1, xhat, rstd, g_ref[...])

    vec = _full((1, D_MODEL))
    return _hosted_call(
        body, comms, name="ffn_bwd_ln1_bwd", grid=(T // tm,),
        in_specs=[_row(tm, D_FF)] * 2 + [_row(tm, D_MODEL)] * 2 + [_full(wg.shape), _full(wu.shape), _full(wd.shape), vec, vec],
        out_specs=[_row(tm, D_FF)] * 2 + [_row(tm, D_MODEL), vec, vec],
        out_shape=[_sds((T, D_FF), _MXU)] * 2 + [_sds((T, D_MODEL)), _sds((1, D_MODEL)), _sds((1, D_MODEL))],
        scratch_shapes=[], args=(a_pre, b_pre, z1, dz2, wg, wu, wd, ln_g, ln_b))


def _out_proj_bwd(dz1, u, o_hg, h_ml, g_hg, g_ml, w_out, tm):
    T = dz1.shape[0]

    def body(dz_ref, hgate_ref, mo_ref, ohg_ref, hml_ref, ghg_ref, gml_ref, w_ref,
             dohg_ref, dhml_ref, dhgate_ref, dmo_ref, dghg_ref, dgml_ref):
        @pl.when(pl.program_id(0) == 0)
        def _():
            dghg_ref[...] = jnp.zeros_like(dghg_ref)
            dgml_ref[...] = jnp.zeros_like(dgml_ref)

        dm = _dot(dz_ref[...], w_ref[...], NT)

        def half(dmh, o, gvec, gate_val, dgate_fac, do_ref, dgate_ref, dgvec_ref):
            r, rs = _head_norm(o)
            dgate_ref[...] = (dmh * r * gvec * dgate_fac).astype(dgate_ref.dtype)
            dn = dmh * gate_val
            dgvec_ref[...] += _colsum(dn * r)
            dr = dn * gvec
            parts = []
            for h in range(HEADS):
                sl = slice(128 * h, 128 * (h + 1))
                parts.append(rs[h] * (dr[:, sl] - r[:, sl] * jnp.mean(dr[:, sl] * r[:, sl], axis=-1, keepdims=True)))
            do_ref[...] = jnp.concatenate(parts, axis=1)

        hg = hgate_ref[...]
        shg = _sig(hg)
        half(dm[:, :HALF], ohg_ref[...], ghg_ref[...], hg * shg, _dsilu(hg, shg), dohg_ref, dhgate_ref, dghg_ref)
        smo = _sig(mo_ref[...])
        half(dm[:, HALF:], hml_ref[...], gml_ref[...], smo, smo * (1.0 - smo), dhml_ref, dmo_ref, dgml_ref)

    vec = _full((1, HALF))
    return pl.pallas_call(
        body, name="out_proj_bwd", grid=(T // tm,),
        in_specs=[_row(tm, D_MODEL), _row(tm, HALF, C_HGATE // 4), _row(tm, HALF, C_MO // 4),
                  _row(tm, HALF), _row(tm, HALF), vec, vec, _full(w_out.shape)],
        out_specs=[_row(tm, HALF)] * 4 + [vec, vec],
        out_shape=[_sds((T, HALF))] * 2 + [_sds((T, HALF), _MXU)] * 2 + [_sds((1, HALF))] * 2,
        compiler_params=_cp("arbitrary"))(dz1, u, u, o_hg, h_ml, g_hg, g_ml, w_out)


def _conv_bwd(u, pre, dqkc, cw, tm):
    T = u.shape[0]
    hb = tm // 8
    nb = T // 8

    def body(x_ref, xh_ref, pre_ref, preh_ref, d_ref, dh_ref, w_ref, dx_ref, dw_ref, db_ref):
        i = pl.program_id(0)

        @pl.when(i == 0)
        def _():
            dw_ref[...] = jnp.zeros_like(dw_ref)
            db_ref[...] = jnp.zeros_like(db_ref)

        def dpre_of(pre, d):
            return d * _dsilu(pre, _sig(pre))

        rowi = _iota((8, HALF), 0)
        dpre = dpre_of(pre_ref[...], d_ref[...])
        dpre_next = jnp.where(i < pl.num_programs(0) - 1, dpre_of(preh_ref[...], dh_ref[...]), 0.0)
        x = x_ref[...]
        xprev = jnp.where(i > 0, xh_ref[...], 0.0)
        dx = dpre * w_ref[3:4, :]
        db_ref[...] += _colsum(dpre)
        dws = [None] * 4
        dws[3] = _colsum(dpre * x)
        for j in (1, 2, 3):
            dx = dx + _shift_rows_up(dpre, dpre_next, j, rowi) * w_ref[3 - j:4 - j, :]
            dws[3 - j] = _colsum(dpre * _shift_rows(x, xprev, j, rowi))
        dx_ref[...] = dx.astype(dx_ref.dtype)
        dw_ref[...] += jnp.concatenate(dws, axis=0)

    cur = lambda blk: pl.BlockSpec((tm, HALF), lambda i, blk=blk: (i, blk))
    nxt = pl.BlockSpec((8, HALF), lambda i: (jnp.minimum((i + 1) * hb, nb - 1), 0))
    return pl.pallas_call(
        body, name="conv_bwd", grid=(T // tm,),
        in_specs=[cur(C_MQK // 4), pl.BlockSpec((8, HALF), lambda i: (jnp.maximum(i * hb - 1, 0), C_MQK // 4)),
                  cur(0), nxt, cur(0), nxt, _full(cw.shape)],
        out_specs=[cur(0), _full((4, HALF)), _full((1, HALF))],
        out_shape=[_sds((T, HALF), _MXU), _sds((4, HALF)), _sds((1, HALF))],
        compiler_params=_cp("arbitrary"))(u, u, pre, pre, dqkc, dqkc, cw)


def _du_specs(rows):
    return [pl.BlockSpec((rows, w), lambda i: (i, 0)) for w in DU_WIDTHS]


def _in_proj_bwd(dz1, du_parts, w, tm, comms=None):
    T = dz1.shape[0]

    def body(dz_ref, *refs):
        du = jnp.concatenate([r[...] for r in refs[:8]], axis=1)
        refs[9][...] = ALPHA * dz_ref[...] + _dot(du, refs[8][...], NT)

    (dx,), got = _hosted_call(
        body, comms, name="in_proj_bwd", grid=(T // tm,),
        in_specs=[_row(tm, D_MODEL)] + _du_specs(tm) + [_full(w.shape)],
        out_specs=[_row(tm, D_MODEL)], out_shape=[_sds((T, D_MODEL))], scratch_shapes=[], args=(dz1, *du_parts, w))
    return dx, got


def _wgrad(a, b, name, tm, tn, tk):
    T, M = a.shape
    N = b.shape[1]
    tm, tn, tk = min(tm, M), min(tn, N), min(tk, T)
    nk = T // tk

    def body(a_ref, b_ref, o_ref, acc_ref):
        kk = pl.program_id(2)

        @pl.when(kk == 0)
        def _():
            acc_ref[...] = jnp.zeros_like(acc_ref)

        acc_ref[...] += _dot(a_ref[...], b_ref[...], TN)

        @pl.when(kk == nk - 1)
        def _():
            o_ref[...] = acc_ref[...].astype(o_ref.dtype)

    return pl.pallas_call(
        body, name=name, grid=(M // tm, N // tn, nk),
        in_specs=[pl.BlockSpec((tk, tm), lambda i, j, kk: (kk, i)), pl.BlockSpec((tk, tn), lambda i, j, kk: (kk, j))],
        out_specs=pl.BlockSpec((tm, tn), lambda i, j, kk: (i, j)), out_shape=_sds((M, N), _MXU),
        scratch_shapes=[pltpu.VMEM((tm, tn), F32)],
        compiler_params=_cp("parallel", "parallel", "arbitrary"))(a, b)


W_IN_PARTS = 2


def _wgrad_w_in(x, du_parts, tk, part, comms=None):
    T = x.shape[0]
    M = D_MODEL // W_IN_PARTS
    tk = min(tk, T)
    nk = T // tk

    def body(a_ref, *refs):
        o_ref, cs_ref, acc_ref = refs[8:]
        kk = pl.program_id(0)

        @pl.when(kk == 0)
        def _():
            acc_ref[...] = jnp.zeros_like(acc_ref)
            cs_ref[...] = jnp.zeros_like(cs_ref)

        du = jnp.concatenate([r[...] for r in refs[:8]], axis=1)
        acc_ref[...] += _dot(a_ref[...], du, TN)
        cs_ref[...] += _colsum(du.astype(F32))

        @pl.when(kk == nk - 1)
        def _():
            o_ref[...] = acc_ref[...].astype(o_ref.dtype)

    return _hosted_call(
        body, comms, name="wgrad_w_in_%d" % part, grid=(nk,),
        in_specs=[pl.BlockSpec((tk, M), lambda kk: (kk, part))] + _du_specs(tk),
        out_specs=[_full((M, PROJ_WP)), _full((1, PROJ_WP))],
        out_shape=[_sds((M, PROJ_WP), _MXU), _sds((1, PROJ_WP))],
        scratch_shapes=[pltpu.VMEM((M, PROJ_WP), F32)], args=(x, *du_parts))


W_IN_S, FF_S, OUT_S, PP_S = PROJ_W // N_DEV, D_FF // N_DEV, D_MODEL // N_DEV, D_MODEL // N_DEV
LATE = ("w_ffn_gate", "w_ffn_up", "w_out", "w_ffn_down", "ple_w_gate", "ple_w_proj")
BIG = ("w_in",) + LATE


def _split_cols(a, n):
    return a.reshape(a.shape[0], N_DEV, n).transpose(1, 0, 2)


def _join_cols(a):
    return a.transpose(1, 0, 2).reshape(a.shape[1], -1)


def _step(x, p, tgt, w_in, b_in, lb_logits, conv_w, conv_b, g_hg, g_ml, ln1_g, ln1_b, ln2_g, ln2_b, bpg, late,
          distributed):
    T = x.shape[0]
    tm, tf = min(ROWS, T), min(ROWS_FFN, T)
    gather = lambda *names: [_GatherTwoLevel([late[n] for n in names])] if distributed else None
    scatter = lambda *arrs: [_Comm("scatter", list(arrs))] if distributed else None
    rows = lambda a, n: a.reshape(N_DEV, n, D_MODEL)
    (u, pre, qkc), got1 = _in_proj(x, w_in, b_in, conv_w, conv_b, tm, gather("w_out", "ple_w_gate", "ple_w_proj"))
    (o_hg, hg_states), got2 = _hgrn2_fwd(u, lb_logits, gather("w_ffn_gate", "w_ffn_up"))
    (h_ml, cst, nst, mst), got3 = _mlstm_fwd(qkc, u, gather("w_ffn_down"))
    if distributed:
        w_out, wpg, wpp = got1[0][0].reshape(D_MODEL, D_MODEL), got1[0][1].reshape(D_MODEL, D_MODEL), _join_cols(got1[0][2])
        wg, wu, wd = _join_cols(got2[0][0]), _join_cols(got2[0][1]), got3[0][0].reshape(D_FF, D_MODEL)
    else:
        w_out, wg, wu, wd, wpg, wpp = (late[n] for n in ("w_out", "w_ffn_gate", "w_ffn_up", "w_ffn_down", "ple_w_gate", "ple_w_proj"))
    m_in, z1 = _out_proj(x, u, o_hg, h_ml, g_hg, g_ml, w_out, tm)
    z2, x1, x2, a_pre, b_pre, hh = _ffn_ln(z1, ln1_g, ln1_b, wg, wu, wd, ln2_g, ln2_b, tf)
    de, dgp, dz2, loss_vec, d_bpg, d_ln2g, d_ln2b = _ple_loss_ln2_bwd(z2, p, tgt, wpg, bpg, wpp, ln2_g, ln2_b, tm)
    big = dict(ple_w_gate=_wgrad(x2, dgp, "wgrad_ple_gate", 512, D_MODEL, 1024),
               ple_w_proj=_wgrad(p, de, "wgrad_ple_proj", 512, D_MODEL, 1024))
    (da, dbb, dz1, d_ln1g, d_ln1b), r1 = _ffn_bwd_ln1_bwd(
        a_pre, b_pre, z1, dz2, wg, wu, wd, ln1_g, ln1_b, tf, scatter(rows(big["ple_w_gate"], OUT_S), _split_cols(big["ple_w_proj"], PP_S)))
    big.update(
        w_ffn_gate=_wgrad(x1, da, "wgrad_ffn_gate", 512, D_FF, 1024),
        w_ffn_up=_wgrad(x1, dbb, "wgrad_ffn_up", 512, D_FF, 1024),
        w_ffn_down=_wgrad(hh, dz2, "wgrad_ffn_down", D_FF, D_MODEL, 1024),
        w_out=_wgrad(m_in, dz1, "wgrad_w_out", 512, D_MODEL, 1024))
    d_ohg, d_hml, d_hgate, d_mo, d_ghg, d_gml = _out_proj_bwd(dz1, u, o_hg, h_ml, g_hg, g_ml, w_out, tm)
    (d_hq, d_hf, d_hv, d_lb), r2 = _hgrn2_bwd(
        u, lb_logits, d_ohg, hg_states,
        scatter(_split_cols(big["w_ffn_gate"], FF_S), _split_cols(big["w_ffn_up"], FF_S), rows(big["w_ffn_down"], FF_S),
                rows(big["w_out"], OUT_S)))
    d_qkc, d_mv, d_gates = _mlstm_bwd(qkc, u, d_hml, cst, nst, mst)
    d_mqk, d_convw, d_convb = _conv_bwd(u, pre, d_qkc, conv_w, tm)
    du_parts = [d_hq, d_hf, d_hv, d_hgate, d_mqk, d_mv, d_mo, d_gates]
    own = lambda g: _split_cols(g[:, :PROJ_W], W_IN_S)
    (g_in0, d_bin), _ = _wgrad_w_in(x, du_parts, 512, 0)
    (g_in1, _), r_in0 = _wgrad_w_in(x, du_parts, 512, 1, scatter(own(g_in0)))
    big["w_in"] = jnp.concatenate([g_in0, g_in1], axis=0)
    small = dict(b_in=d_bin[:, :PROJ_W], hg_lb_logits=d_lb, ml_conv_w=d_convw, ml_conv_b=d_convb, hg_norm_g=d_ghg,
                 ml_norm_g=d_gml, ln1_g=d_ln1g, ln1_b=d_ln1b, ln2_g=d_ln2g, ln2_b=d_ln2b, ple_b_gate=d_bpg)
    last = [_Comm("scatter", [own(g_in1)]), _Comm("gather", [loss_vec] + [small[n] for n in SMALL])] if distributed else None
    dx, r3 = _in_proj_bwd(dz1, du_parts, w_in, tm, last)
    gathered_small = None
    if distributed:
        big = dict(ple_w_gate=r1[0][0], ple_w_proj=r1[0][1], w_ffn_gate=r2[0][0], w_ffn_up=r2[0][1],
                   w_ffn_down=r2[0][2], w_out=r2[0][3], w_in=[r_in0[0][0], r3[0][0]])
        gathered_small = r3[1]
    return loss_vec, dx, big, small, gathered_small


SMALL = ("b_in", "hg_lb_logits", "ml_conv_w", "ml_conv_b", "hg_norm_g", "ml_norm_g", "ln1_g", "ln1_b", "ln2_g", "ln2_b",
         "ple_b_gate")


def _padc(a, n):
    return jnp.pad(a, [(0, 0)] * (a.ndim - 1) + [(0, n - a.shape[-1])])


def _adamw(w, g, m, v):
    m = B1 * m + (1.0 - B1) * g
    v = B2 * v + (1.0 - B2) * jnp.square(g)
    m_hat = m / (1.0 - B1 ** STEP)
    v_hat = v / (1.0 - B2 ** STEP)
    return -LR * (m_hat / (jnp.sqrt(v_hat) + EPS) + WD * w), m, v


def _sum_slabs(ref):
    g = ref[0].astype(F32)
    for j in range(1, N_DEV):
        g = g + ref[j].astype(F32)
    return g


def _adamw_matrix(rbs, w, m, v, name):
    rbs = list(rbs) if isinstance(rbs, (list, tuple)) else [rbs]
    nb = len(rbs)
    R, C = w.shape
    tr = 256 if (R // nb) % 256 == 0 else R // nb
    per = R // nb // tr

    def body(*refs):
        w_ref, m_ref, v_ref, g_ref, d_ref, m2_ref, v2_ref = refs[nb:]
        i = pl.program_id(0)
        g = _sum_slabs(refs[0])
        for k in range(1, nb):
            g = jnp.where(i >= k * per, _sum_slabs(refs[k]), g)
        g_ref[...] = g
        d_ref[...], m2_ref[...], v2_ref[...] = _adamw(w_ref[...], g, m_ref[...], v_ref[...])

    blk = pl.BlockSpec((tr, C), lambda i: (i, 0))
    part = lambda k: pl.BlockSpec((N_DEV, tr, C), lambda i, k=k: (0, jnp.clip(i - k * per, 0, per - 1), 0))
    return pl.pallas_call(
        body, name=name, grid=(R // tr,),
        in_specs=[part(k) for k in range(nb)] + [blk, blk, blk],
        out_specs=[blk] * 4, out_shape=[_sds((R, C))] * 4, compiler_params=_cp("parallel"))(*rbs, w, m, v)


def _adamw_small(loss_g, gs, ws, ms, vs):
    n = len(ws)

    def body(*refs):
        loss_ref, g_refs, w_refs, m_refs, v_refs = refs[0], refs[1:1 + n], refs[1 + n:1 + 2 * n], refs[1 + 2 * n:1 + 3 * n], refs[1 + 3 * n:1 + 4 * n]
        outs = refs[1 + 4 * n:]
        outs[0][...] = (0.5 / D_MODEL) * jnp.sum(_sum_slabs(loss_ref), keepdims=True)
        for i in range(n):
            g = _sum_slabs(g_refs[i])
            outs[1 + i][...] = g
            outs[1 + n + i][...], outs[1 + 2 * n + i][...], outs[1 + 3 * n + i][...] = _adamw(
                w_refs[i][...], g, m_refs[i][...], v_refs[i][...])

    res = pl.pallas_call(
        body, name="adamw_small", out_shape=[_sds((1, 1))] + [_sds(w.shape) for w in ws] * 4)(loss_g, *gs, *ws, *ms, *vs)
    return res[0], [res[1 + k * n:1 + (k + 1) * n] for k in range(4)]


WEIGHTS = ("w_in", "b_in", "hg_lb_logits", "ml_conv_w", "ml_conv_b", "hg_norm_g", "ml_norm_g", "w_out", "ln1_g", "ln1_b",
           "w_ffn_gate", "w_ffn_up", "w_ffn_down", "ln2_g", "ln2_b", "ple_w_proj", "ple_w_gate", "ple_b_gate")
CONV_S = HALF // N_DEV


def kernel(x, p, w_in, b_in, hg_lb_logits, ml_conv_w, ml_conv_b, hg_norm_g, ml_norm_g, w_out, ln1_g, ln1_b, w_ffn_gate, w_ffn_up, w_ffn_down, ln2_g, ln2_b, ple_w_proj, ple_w_gate, ple_b_gate, loss_target, m_w_in, m_b_in, m_hg_lb_logits, m_ml_conv_w, m_ml_conv_b, m_hg_norm_g, m_ml_norm_g, m_w_out, m_ln1_g, m_ln1_b, m_w_ffn_gate, m_w_ffn_up, m_w_ffn_down, m_ln2_g, m_ln2_b, m_ple_w_proj, m_ple_w_gate, m_ple_b_gate, v_w_in, v_b_in, v_hg_lb_logits, v_ml_conv_w, v_ml_conv_b, v_hg_norm_g, v_ml_norm_g, v_w_out, v_ln1_g, v_ln1_b, v_w_ffn_gate, v_w_ffn_up, v_w_ffn_down, v_ln2_g, v_ln2_b, v_ple_w_proj, v_ple_w_gate, v_ple_b_gate):
    args = locals()
    me = 4 * lax.axis_index("x") + 2 * lax.axis_index("y") + lax.axis_index("c")
    shapes = {n: args[n].shape for n in WEIGHTS}
    drop = lambda n, a: a[0] if n in BIG or n == "ml_conv_w" else a
    W = {n: drop(n, args[n]) for n in WEIGHTS}
    M = {n: drop(n, args["m_" + n]) for n in WEIGHTS}
    V = {n: drop(n, args["v_" + n]) for n in WEIGHTS}

    g_in, g_conv = _gather_two_level(
        [W["w_in"].astype(_MXU), jnp.pad(W["ml_conv_w"], ((0, 4), (0, 128 - CONV_S)))], "gather_w_in")
    w_in_full = _padc(_join_cols(g_in), PROJ_WP)
    conv_full = _join_cols(g_conv[:, :4, :CONV_S])

    _, dx, big, _, sg = _step(
        x[0], p[0, 0], loss_target[0], w_in_full, _padc(b_in, PROJ_WP), hg_lb_logits, conv_full, ml_conv_b,
        hg_norm_g, ml_norm_g, ln1_g, ln1_b, ln2_g, ln2_b, ple_b_gate, {n: W[n].astype(_MXU) for n in LATE}, True)

    upd = {n: _adamw_matrix(big[n], W[n], M[n], V[n], "adamw_" + n) for n in BIG}
    sg = dict(zip(SMALL, sg[1:]), loss=sg[0])
    sg["ml_conv_w"] = lax.dynamic_slice(sg["ml_conv_w"], (0, 0, me * CONV_S), (N_DEV, 4, CONV_S))
    loss, small_upd = _adamw_small(sg["loss"], *[[d[n] for n in SMALL] for d in (sg, W, M, V)])

    outs = []
    for kind in range(4):
        smalls = dict(zip(SMALL, small_upd[kind]))
        for n in WEIGHTS:
            outs.append((upd[n][kind] if n in BIG else smalls[n]).reshape(shapes[n]))
    return (loss.reshape(()), dx.reshape(x.shape), *outs)
```

```python
import jax
import jax.numpy as jnp
from jax import lax
from jax.experimental import pallas as pl
from jax.experimental.pallas import tpu as pltpu

F32 = jnp.float32
_MXU = jnp.bfloat16

D_MODEL = 1024
CHUNK = 64
SUB = 16
PLE_DIM = 256
HEADS = 4
ML_DQK = 64
HALF = 512
D_FF = 2816
PROJ_W = 3592
PROJ_WP = 3712
ALPHA = float(2 ** 0.25)
LN_EPS = 1e-5
RMS_EPS = 1e-6
ML_SCALE = ML_DQK ** -0.5
N_DEV = 8
LR, B1, B2, EPS, WD, STEP = 0.001, 0.9, 0.999, 1e-08, 0.01, 10
NEG = -1e30
LOG2E = 1.4426950408889634

C_HQ, C_HF, C_HV, C_HGATE, C_MQK, C_MV, C_MO, C_GATES = 0, 4, 8, 12, 16, 20, 24, 28
DU_WIDTHS = (HALF,) * 7 + (128,)

VMEM_LIMIT = 52 * 1024 * 1024
GC = 8
ROWS = 512
ROWS_FFN = 256

NN = (((1,), (0,)), ((), ()))
NT = (((1,), (1,)), ((), ()))
TN = (((0,), (0,)), ((), ()))
BNT = (((2,), (2,)), ((0,), (0,)))
BNN = (((2,), (1,)), ((0,), (0,)))
BTN = (((1,), (1,)), ((0,), (0,)))


def _dot(a, b, dims=NN):
    return lax.dot_general(a.astype(_MXU), b.astype(_MXU), dims, preferred_element_type=F32)


def _dotx(a, b, dims=NN):
    return lax.dot_general(a, b, dims, precision=lax.Precision.HIGHEST, preferred_element_type=F32)


def _sig(x):
    return jax.nn.sigmoid(x)


def _cp(*sem):
    return pltpu.CompilerParams(dimension_semantics=sem, vmem_limit_bytes=VMEM_LIMIT)


def _row(tm, c, blk=0):
    return pl.BlockSpec((tm, c), lambda i, blk=blk: (i, blk))


def _full(shape):
    nd = len(shape)
    return pl.BlockSpec(tuple(shape), lambda *_, nd=nd: (0,) * nd)


def _sds(shape, dtype=F32):
    return jax.ShapeDtypeStruct(tuple(shape), dtype)


def _iota(shape, axis):
    return lax.broadcasted_iota(jnp.int32, shape, axis)


def _colsum(x):
    return jnp.sum(x, axis=0, keepdims=True)


def _rowsum(x):
    return jnp.sum(x, axis=1, keepdims=True)


def _ln_fwd(z, g, b):
    mu = jnp.mean(z, axis=-1, keepdims=True)
    zc = z - mu
    var = jnp.mean(zc * zc, axis=-1, keepdims=True)
    rstd = lax.rsqrt(var + LN_EPS)
    xhat = zc * rstd
    return xhat * g + b, xhat, rstd


def _ln_bwd(dy, xhat, rstd, g):
    dxh = dy * g
    m1 = jnp.mean(dxh, axis=-1, keepdims=True)
    m2 = jnp.mean(dxh * xhat, axis=-1, keepdims=True)
    return rstd * (dxh - m1 - xhat * m2)


def _dsilu(x, s):
    return s * (1.0 + x * (1.0 - s))


MESH = pl.DeviceIdType.MESH
ANY = pl.BlockSpec(memory_space=pl.ANY)


def _flip(v, bit):
    return 1 - v if bit else v


class _Comm:
    def __init__(self, kind, srcs):
        self.kind, self.srcs, self.n = kind, list(srcs), len(srcs)

    def out_shape(self):
        lead = (N_DEV,) if self.kind == "gather" else ()
        return [jax.ShapeDtypeStruct(lead + s.shape, s.dtype) for s in self.srcs]

    def scratch(self):
        return [pltpu.SemaphoreType.DMA((7 * self.n,)), pltpu.SemaphoreType.DMA((7 * self.n,)),
                pltpu.SemaphoreType.DMA((self.n,))]

    def copies(self, srcs, dsts, send_sems, recv_sems, local_sems):
        x, y, c = lax.axis_index("x"), lax.axis_index("y"), lax.axis_index("c")
        me = 4 * x + 2 * y + c
        pick = (lambda s, j: s) if self.kind == "gather" else (lambda s, j: s.at[j])
        out = []
        for i, (s, d) in enumerate(zip(srcs, dsts)):
            out.append(pltpu.make_async_copy(pick(s, me), d.at[me], local_sems.at[i]))
            for k in range(1, N_DEV):
                px, py, pc = _flip(x, k & 4), _flip(y, k & 2), _flip(c, k & 1)
                out.append(pltpu.make_async_remote_copy(
                    src_ref=pick(s, 4 * px + 2 * py + pc), dst_ref=d.at[me], send_sem=send_sems.at[7 * i + k - 1],
                    recv_sem=recv_sems.at[7 * i + k - 1], device_id=(px, py, pc), device_id_type=MESH))
        return out

    def start(self, *refs):
        for cp in self.copies(*refs):
            cp.start()

    def mid(self, *refs):
        pass

    def finish(self, *refs):
        for cp in self.copies(*refs):
            cp.wait()


class _GatherTwoLevel(_Comm):
    def __init__(self, srcs):
        super().__init__("gather", srcs)

    def _parts(self, srcs, dsts, send_sems, recv_sems, local_sems):
        x, y, c = lax.axis_index("x"), lax.axis_index("y"), lax.axis_index("c")
        me, sibling = (x, y, c), (x, y, 1 - c)
        chips = [(1 - x, y), (x, 1 - y), (1 - x, 1 - y)]

        def copy(i, k, block, to, own=False):
            slab = dsts[i].at[4 * block[0] + 2 * block[1] + block[2]]
            return pltpu.make_async_remote_copy(
                src_ref=srcs[i] if own else slab, dst_ref=slab, send_sem=send_sems.at[7 * i + k],
                recv_sem=recv_sems.at[7 * i + k], device_id=to, device_id_type=MESH)

        n = range(self.n)
        mine = [pltpu.make_async_copy(srcs[i], dsts[i].at[4 * x + 2 * y + c], local_sems.at[i]) for i in n]
        first = [copy(i, 0, me, sibling, own=True) for i in n]
        first += [copy(i, 1 + j, me, (*chip, c), own=True) for j, chip in enumerate(chips) for i in n]
        over_ici = [copy(i, 1 + j, (*chip, c), me) for j, chip in enumerate(chips) for i in n]
        passed = [copy(i, 4 + j, (*chip, c), sibling) for j, chip in enumerate(chips) for i in n]
        from_sibling = [copy(i, 0, sibling, me) for i in n]
        from_sibling += [copy(i, 4 + j, (*chip, 1 - c), me) for j, chip in enumerate(chips) for i in n]
        return mine, first, over_ici, passed, from_sibling

    def start(self, *refs):
        mine, first, _, _, _ = self._parts(*refs)
        for cp in mine + first:
            cp.start()

    def mid(self, *refs):
        _, _, over_ici, passed, _ = self._parts(*refs)
        for arrived, onward in zip(over_ici, passed):
            arrived.wait_recv()
            onward.start()

    def finish(self, *refs):
        mine, first, _, passed, from_sibling = self._parts(*refs)
        for cp in from_sibling:
            cp.wait_recv()
        for cp in first + passed:
            cp.wait_send()
        for cp in mine:
            cp.wait()


def _hosted_call(body, comms, *, name, grid, in_specs, out_specs, out_shape, scratch_shapes, args):
    comms = list(comms or [])
    if not comms:
        res = pl.pallas_call(body, name=name, grid=grid, in_specs=in_specs, out_specs=out_specs, out_shape=out_shape,
                             scratch_shapes=scratch_shapes, compiler_params=_cp("arbitrary"))(*args)
        return list(res), []
    n_in, n_out, n_sc, nc = len(in_specs), len(out_specs), len(scratch_shapes), sum(cm.n for cm in comms)
    last = grid[0] - 1

    def hosted(*refs):
        ins, csrc = refs[:n_in], refs[n_in:n_in + nc]
        o0 = n_in + nc
        outs, cdst = refs[o0:o0 + n_out], refs[o0 + n_out:o0 + n_out + nc]
        s0 = o0 + n_out + nc
        scr, sems = refs[s0:s0 + n_sc], refs[s0 + n_sc:]

        def phase(which):
            o = 0
            for j, cm in enumerate(comms):
                getattr(cm, which)(csrc[o:o + cm.n], cdst[o:o + cm.n], *sems[3 * j:3 * j + 3])
                o += cm.n

        i = pl.program_id(0)

        @pl.when(i == 0)
        def _():
            phase("start")

        body(*ins, *outs, *scr)

        @pl.when(i == (2 * last) // 3)
        def _():
            phase("mid")

        @pl.when(i == last)
        def _():
            phase("finish")

    res = pl.pallas_call(
        hosted, name=name, grid=grid, in_specs=list(in_specs) + [ANY] * nc, out_specs=list(out_specs) + [ANY] * nc,
        out_shape=list(out_shape) + [s for cm in comms for s in cm.out_shape()],
        scratch_shapes=list(scratch_shapes) + [s for cm in comms for s in cm.scratch()],
        compiler_params=_cp("arbitrary"))(*args, *[a for cm in comms for a in cm.srcs])
    got, o = [], n_out
    for cm in comms:
        got.append(list(res[o:o + cm.n]))
        o += cm.n
    return list(res[:n_out]), got


def _gather_two_level(blocks, name):
    n = len(blocks)

    def body(*refs):
        x_refs, out_refs = refs[:n], refs[n:2 * n]
        send_sems, recv_sems, local_sems = refs[2 * n:]
        x, y, c = lax.axis_index("x"), lax.axis_index("y"), lax.axis_index("c")
        me, sibling = (x, y, c), (x, y, 1 - c)
        chips = [(1 - x, y), (x, 1 - y), (1 - x, 1 - y)]

        def copy(i, k, block, to, own=False):
            slab = out_refs[i].at[4 * block[0] + 2 * block[1] + block[2]]
            return pltpu.make_async_remote_copy(
                src_ref=x_refs[i] if own else slab, dst_ref=slab, send_sem=send_sems.at[7 * i + k],
                recv_sem=recv_sems.at[7 * i + k], device_id=to, device_id_type=MESH)

        mine = [pltpu.make_async_copy(x_refs[i], out_refs[i].at[4 * x + 2 * y + c], local_sems.at[i]) for i in range(n)]
        for cp in mine:
            cp.start()
        first = [copy(i, 0, me, sibling, own=True) for i in range(n)]
        first += [copy(i, 1 + j, me, (*chip, c), own=True) for j, chip in enumerate(chips) for i in range(n)]
        for cp in first:
            cp.start()
        passed = []
        for j, chip in enumerate(chips):
            for i in range(n):
                copy(i, 1 + j, (*chip, c), me).wait_recv()
                passed.append(copy(i, 4 + j, (*chip, c), sibling))
                passed[-1].start()
        for i in range(n):
            copy(i, 0, sibling, me).wait_recv()
            for j, chip in enumerate(chips):
                copy(i, 4 + j, (*chip, 1 - c), me).wait_recv()
        for cp in first + passed:
            cp.wait_send()
        for cp in mine:
            cp.wait()

    return pl.pallas_call(
        body, name=name, out_shape=[jax.ShapeDtypeStruct((N_DEV,) + b.shape, b.dtype) for b in blocks],
        in_specs=[ANY] * n, out_specs=[ANY] * n,
        scratch_shapes=[pltpu.SemaphoreType.DMA((7 * n,)), pltpu.SemaphoreType.DMA((7 * n,)),
                        pltpu.SemaphoreType.DMA((n,))])(*blocks)


def _in_proj(x, w, b, cw, cb, tm, comms=None):
    T = x.shape[0]

    def body(x_ref, w_ref, b_ref, cw_ref, cb_ref, o_ref, pre_ref, act_ref, halo_sc):
        @pl.when(pl.program_id(0) == 0)
        def _():
            halo_sc[...] = jnp.zeros_like(halo_sc)

        o = _dot(x_ref[...], w_ref[...]) + b_ref[...]
        o_ref[...] = o
        xc = o[:, 128 * C_MQK:128 * C_MQK + HALF]
        halo = halo_sc[...]
        rowi = _iota((8, HALF), 0)
        acc = xc * cw_ref[3:4, :] + cb_ref[...]
        for j in (1, 2, 3):
            acc = acc + _shift_rows(xc, halo, j, rowi) * cw_ref[3 - j:4 - j, :]
        pre_ref[...] = acc
        act_ref[...] = acc * _sig(acc)
        halo_sc[...] = xc[tm - 8:]

    return _hosted_call(
        body, comms, name="in_proj", grid=(T // tm,),
        in_specs=[_row(tm, D_MODEL), _full(w.shape), _full(b.shape), _full(cw.shape), _full(cb.shape)],
        out_specs=[_row(tm, PROJ_WP), _row(tm, HALF), _row(tm, HALF)],
        out_shape=[_sds((T, PROJ_WP)), _sds((T, HALF)), _sds((T, HALF))],
        scratch_shapes=[pltpu.VMEM((8, HALF), F32)], args=(x, w, b, cw, cb))


def _shift_rows(x, halo, j, rowi):
    r = pltpu.roll(x, j, 0)
    top = jnp.where(rowi < j, pltpu.roll(halo, j, 0), r[:8])
    return jnp.concatenate([top, r[8:]], axis=0)


def _shift_rows_up(x, halo, j, rowi):
    n = x.shape[0]
    r = pltpu.roll(x, n - j, 0)
    bot = jnp.where(rowi >= 8 - j, pltpu.roll(halo, 8 - j, 0), r[n - 8:])
    return jnp.concatenate([r[:n - 8], bot], axis=0)


def _bdot(a, b, dims):
    return lax.dot_general(a.astype(_MXU), b.astype(_MXU), dims, preferred_element_type=F32)


def _bdotx(a, b, dims):
    return lax.dot_general(a, b, dims, precision=lax.Precision.HIGHEST, preferred_element_type=F32)


def _heads_to_batch(x, w):
    G = x.shape[0] // CHUNK
    x3 = x.reshape(G, CHUNK, HEADS * w)
    return jnp.stack([x3[:, :, w * h:w * (h + 1)] for h in range(HEADS)], axis=1).reshape(G * HEADS, CHUNK, w)


def _batch_to_heads(x3):
    B, _, w = x3.shape
    x4 = x3.reshape(B // HEADS, HEADS, CHUNK, w)
    return jnp.concatenate([x4[:, h] for h in range(HEADS)], axis=-1).reshape(B // HEADS * CHUNK, HEADS * w)


def _chunk_cumsum(x, rowmod, reverse=False):
    R = x.shape[0]
    for sh in (1, 2, 4, 8, 16, 32):
        if reverse:
            x = x + jnp.where(rowmod < CHUNK - sh, pltpu.roll(x, R - sh, 0), 0.0)
        else:
            x = x + jnp.where(rowmod >= sh, pltpu.roll(x, sh, 0), 0.0)
    return x


def _lane_col(x, c, lane):
    return _rowsum(jnp.where(lane == c, x, 0.0))


def _hg_gates(hq, hf, lb):
    sg = _sig(hf)
    nsg = _sig(-hf)
    f = lb + (1.0 - lb) * sg
    g = jnp.log(f)
    k = (1.0 - lb) * nsg
    sq = _sig(hq)
    return hq * sq, g, k, f, sg, nsg, sq


def _hg_prep(hq_ref, hf_ref, lg_ref, b_sc, k_sc):
    R = hq_ref.shape[0]
    G = R // CHUNK
    lb = _sig(lg_ref[0:1, :] - lg_ref[1:2, :])
    hq = hq_ref[...]
    q, g, k, f, sg, nsg, sq = _hg_gates(hq, hf_ref[...], lb)
    rowmod = _iota((R, HALF), 0) & (CHUNK - 1)
    b = _chunk_cumsum(g, rowmod) * LOG2E
    last8 = _iota((8, HALF), 0) == 7
    bl_rows = [_colsum(jnp.where(last8, b[CHUNK * c + CHUNK - 8:CHUNK * (c + 1)], 0.0)) for c in range(G)]
    bl3 = jnp.stack([r[:, 128 * h:128 * (h + 1)] for r in bl_rows for h in range(HEADS)], axis=0)
    b3, k3 = _heads_to_batch(b, 128), _heads_to_batch(k, 128)
    b_sc[...] = b3
    k_sc[...] = k3
    return dict(G=G, lb=lb, hq=hq, f=f, sg=sg, nsg=nsg, sq=sq, rowmod=rowmod, q3=_heads_to_batch(q, 128), k3=k3, b3=b3,
                bl3=bl3)


HSUB = SUB // 2


def _lo(j):
    return HSUB * (j // HSUB)


def _hg_diag_tiles(b_sc, b3, r0, rowi):
    bi = b3[:, r0:r0 + SUB]
    return [jnp.exp2(jnp.where(rowi[:, _lo(s):] >= s, bi[:, _lo(s):] - b_sc[:, r0 + s:r0 + s + 1, :], NEG))
            for s in range(SUB)]


def _hg_diag_tiles_t(b_sc, b3, r0, rowi):
    bi = b3[:, r0:r0 + SUB]
    return [jnp.exp2(jnp.where(rowi[:, :_lo(t) + HSUB] <= t, b_sc[:, r0 + t:r0 + t + 1, :] - bi[:, :_lo(t) + HSUB], NEG))
            for t in range(SUB)]


def _lane_sums(pieces, ones):
    B = pieces[0].shape[0]
    hs = [p.shape[1] for p in pieces]
    R = _dot(jnp.concatenate(pieces, axis=1).reshape(B * sum(hs), 128), ones).reshape(B, sum(hs), 128)
    out, o = [], 0
    for h in hs:
        out.append(R[:, o:o + h])
        o += h
    return out


def _sum_tri(terms, low_rows):
    full = sum(t for t in terms if t.shape[1] == SUB)
    half = sum(t for t in terms if t.shape[1] == HSUB)
    lo, hi = full[:, :HSUB], full[:, HSUB:]
    return jnp.concatenate([lo + half, hi] if low_rows else [lo, hi + half], axis=1)


def _hgrn2_fwd(u, lb_logits, comms=None):
    T = u.shape[0]
    G = min(GC, T // CHUNK)
    R, B, N = G * CHUNK, G * HEADS, T // CHUNK

    def body(hq_ref, hf_ref, hv_ref, lg_ref, o_ref, st_ref, S_ref, b_sc, k_sc, v_sc):
        @pl.when(pl.program_id(0) == 0)
        def _():
            S_ref[...] = jnp.zeros_like(S_ref)

        pz = _hg_prep(hq_ref, hf_ref, lg_ref, b_sc, k_sc)
        q3, k3, b3, bl3 = pz["q3"], pz["k3"], pz["b3"], pz["bl3"]
        v3 = _heads_to_batch(hv_ref[...], 128)
        v_sc[...] = v3
        stloc = _bdot(v3, k3 * jnp.exp2(bl3 - b3), BTN).reshape(G, HEADS, 128, 128)
        dec = jnp.exp2(bl3).reshape(G, HEADS, 1, 128)
        ST = S_ref[...]
        sts = []
        for c in range(G):
            sts.append(ST)
            ST = ST * dec[c] + stloc[c]
        S_ref[...] = ST
        st4 = jnp.stack(sts, axis=0)
        st_ref[...] = st4
        o = _bdot(q3 * jnp.exp2(b3), st4.reshape(B, 128, 128), BNT)
        ones = jnp.ones((128, 128), F32)
        rowi = _iota((1, SUB, 128), 1)
        outs = []
        for i in range(CHUNK // SUB):
            r0 = SUB * i
            qi = q3[:, r0:r0 + SUB]
            oi = o[:, r0:r0 + SUB]
            if i > 0:
                r = b_sc[:, r0 - 1:r0, :]
                qe = qi * jnp.exp2(b3[:, r0:r0 + SUB] - r)
                ke = k3[:, :r0] * jnp.exp2(r - b3[:, :r0])
                oi = oi + _bdot(_bdot(qe, ke, BNT), v3[:, :r0], BNN)
            tiles = _hg_diag_tiles(b_sc, b3, r0, rowi)
            a_b = _lane_sums([qi[:, _lo(s):] * (k_sc[:, r0 + s:r0 + s + 1, :] * tiles[s]) for s in range(SUB)], ones)
            outs.append(oi + _sum_tri([a_b[s] * v_sc[:, r0 + s:r0 + s + 1, :] for s in range(SUB)], False))
        o_ref[...] = _batch_to_heads(jnp.concatenate(outs, axis=1))

    blk = lambda c: pl.BlockSpec((R, HALF), lambda n, c=c: (n, c // 4))
    return _hosted_call(
        body, comms, name="hgrn2_fwd", grid=(N // G,),
        in_specs=[blk(C_HQ), blk(C_HF), blk(C_HV), _full(lb_logits.shape)],
        out_specs=[pl.BlockSpec((R, HALF), lambda n: (n, 0)),
                   pl.BlockSpec((G, HEADS, 128, 128), lambda n: (n, 0, 0, 0))],
        out_shape=[_sds((T, HALF)), _sds((N, HEADS, 128, 128))],
        scratch_shapes=[pltpu.VMEM((HEADS, 128, 128), F32)] + [pltpu.VMEM((B, CHUNK, 128), F32)] * 3,
        args=(u, u, u, lb_logits))


def _hgrn2_bwd(u, lb_logits, do, states, comms=None):
    T = u.shape[0]
    G = min(GC, T // CHUNK)
    R, B, NG = G * CHUNK, G * HEADS, T // (G * CHUNK)

    def body(hq_ref, hf_ref, hv_ref, lg_ref, do_ref, st_ref, dhq_ref, dhf_ref, dhv_ref, dlb_ref,
             dS_ref, b_sc, k_sc, v_sc, q_sc, do_sc):
        @pl.when(pl.program_id(0) == 0)
        def _():
            dS_ref[...] = jnp.zeros_like(dS_ref)
            dlb_ref[...] = jnp.zeros_like(dlb_ref)

        pz = _hg_prep(hq_ref, hf_ref, lg_ref, b_sc, k_sc)
        q3, k3, b3, bl3, lb = pz["q3"], pz["k3"], pz["b3"], pz["bl3"], pz["lb"]
        v3 = _heads_to_batch(hv_ref[...], 128)
        v_sc[...] = v3
        do3 = _heads_to_batch(do_ref[...], 128)
        q_sc[...] = q3
        do_sc[...] = do3
        st3 = st_ref[...].reshape(B, 128, 128)
        eb = jnp.exp2(b3)
        ebl = jnp.exp2(bl3 - b3)
        qt = q3 * eb
        kl = k3 * ebl
        dstloc = _bdot(do3, qt, BTN).reshape(G, HEADS, 128, 128)
        dec = jnp.exp2(bl3).reshape(G, HEADS, 1, 128)
        dST = dS_ref[...]
        dsts = [None] * G
        for c in reversed(range(G)):
            dsts[c] = dST
            dST = dST * dec[c] + dstloc[c]
        dS_ref[...] = dST
        dst3 = jnp.stack(dsts, axis=0).reshape(B, 128, 128)
        dqt = _bdot(do3, st3, BNN)
        dkl = _bdot(v3, dst3, BNN)
        dv_acc = _bdot(kl, dst3, BNT)
        ones = jnp.ones((128, 128), F32)
        rowi = _iota((1, SUB, 128), 1)
        dq_parts, dk_parts, dv_parts = [], [], []
        dk_in = jnp.zeros((B, CHUNK, 128), F32)
        for i_s in range(CHUNK // SUB):
            r0 = SUB * i_s
            qi = q3[:, r0:r0 + SUB]
            doi = do3[:, r0:r0 + SUB]
            dqi = jnp.zeros((B, SUB, 128), F32)
            if i_s > 0:
                r = b_sc[:, r0 - 1:r0, :]
                eq = jnp.exp2(b3[:, r0:r0 + SUB] - r)
                ek = jnp.exp2(r - b3[:, :r0])
                qe = qi * eq
                ke = k3[:, :r0] * ek
                a_off = _bdot(qe, ke, BNT)
                p_off = _bdot(doi, v3[:, :r0], BNT)
                pad = jnp.zeros((B, CHUNK - r0, 128), F32)
                dv_acc = dv_acc + jnp.concatenate([_bdot(a_off, doi, BTN), pad], axis=1)
                dqi = dqi + _bdot(p_off, ke, BNN) * eq
                dk_in = dk_in + jnp.concatenate([_bdot(p_off, qe, BTN) * ek, pad], axis=1)
            ki, vi = k3[:, r0:r0 + SUB], v3[:, r0:r0 + SUB]
            rng = range(SUB)
            tiles = _hg_diag_tiles(b_sc, b3, r0, rowi)
            tiles_t = _hg_diag_tiles_t(b_sc, b3, r0, rowi)
            do_rows = [do_sc[:, r0 + t:r0 + t + 1, :] for t in rng]
            kts = [k_sc[:, r0 + s:r0 + s + 1, :] * tiles[s] for s in rng]
            qts = [q_sc[:, r0 + t:r0 + t + 1, :] * tiles_t[t] for t in rng]
            ps = [doi[:, _lo(s):] * v_sc[:, r0 + s:r0 + s + 1, :] for s in rng]
            mst = [ki[:, :_lo(t) + HSUB] * qts[t] for t in rng]
            pst = [vi[:, :_lo(t) + HSUB] * do_rows[t] for t in rng]
            sums = _lane_sums(ps + mst + pst, ones)
            p_b, a_t, p_t = sums[:SUB], sums[SUB:2 * SUB], sums[2 * SUB:]
            dq_parts.append(dqi + _sum_tri([p_b[s] * kts[s] for s in rng], False))
            dv_parts.append(_sum_tri([a_t[t] * do_rows[t] for t in rng], True))
            dk_parts.append(_sum_tri([p_t[t] * qts[t] for t in rng], True))
        dq_in = jnp.concatenate(dq_parts, axis=1)
        dk_in = dk_in + jnp.concatenate(dk_parts, axis=1)
        dv_acc = dv_acc + jnp.concatenate(dv_parts, axis=1)
        db = qt * dqt + q3 * dq_in - k3 * dk_in - kl * dkl
        last = jnp.sum(kl * dkl, axis=1, keepdims=True) + jnp.exp2(bl3) * jnp.sum(st3 * dst3, axis=1, keepdims=True)
        db = db + jnp.where(_iota((1, CHUNK, 1), 1) == CHUNK - 1, last, 0.0)
        dg = _chunk_cumsum(_batch_to_heads(db), pz["rowmod"], reverse=True)
        dq_tot = _batch_to_heads(dqt * eb + dq_in)
        dk_tot = _batch_to_heads(dkl * ebl + dk_in)
        common = dg / pz["f"] - dk_tot
        dhf_ref[...] = ((1.0 - lb) * pz["sg"] * pz["nsg"] * common).astype(dhf_ref.dtype)
        dl0 = _colsum(pz["nsg"] * common) * lb * (1.0 - lb)
        dlb_ref[0:1, :] += dl0
        dlb_ref[1:2, :] -= dl0
        dhq_ref[...] = (dq_tot * _dsilu(pz["hq"], pz["sq"])).astype(dhq_ref.dtype)
        dhv_ref[...] = _batch_to_heads(dv_acc).astype(dhv_ref.dtype)

    rev = lambda c: pl.BlockSpec((R, HALF), lambda i, c=c: (NG - 1 - i, c // 4))
    rev0 = pl.BlockSpec((R, HALF), lambda i: (NG - 1 - i, 0))
    return _hosted_call(
        body, comms, name="hgrn2_bwd", grid=(NG,),
        in_specs=[rev(C_HQ), rev(C_HF), rev(C_HV), _full(lb_logits.shape), rev0,
                  pl.BlockSpec((G, HEADS, 128, 128), lambda i: (NG - 1 - i, 0, 0, 0))],
        out_specs=[rev0, rev0, rev0, _full((2, HALF))],
        out_shape=[_sds((T, HALF), _MXU)] * 3 + [_sds((2, HALF))],
        scratch_shapes=[pltpu.VMEM((HEADS, 128, 128), F32)] + [pltpu.VMEM((B, CHUNK, 128), F32)] * 5,
        args=(u, u, u, lb_logits, do, states))


def _lanes_to_batch_cols(x, lane):
    G = x.shape[0] // CHUNK
    cols = [_lane_col(x, 4 + h, lane).reshape(G, CHUNK, 1) for h in range(HEADS)]
    return jnp.stack(cols, axis=1).reshape(G * HEADS, CHUNK, 1)


def _row_scalars(rows):
    lane = _iota((1, 128), 1)
    return jnp.stack([_rowsum(jnp.where(lane == 4 + h, r, 0.0)) for r in rows for h in range(HEADS)], axis=0)


def _ml_gates(gates):
    R = gates.shape[0]
    lane = _iota((R, 128), 1)
    rowmod = _iota((R, 128), 0) & (CHUNK - 1)
    lf = jnp.minimum(gates, 0.0) - jnp.log(1.0 + jnp.exp(-jnp.abs(gates)))
    g_all = _chunk_cumsum(lf, rowmod)
    x_all = pltpu.roll(gates, 4, 1) - g_all
    return g_all, x_all, lane, rowmod


def _ml_chunk_rows(g_all, x_all, mprev, g):
    gl = g_all[CHUNK * g + CHUNK - 8:CHUNK * (g + 1)]
    gl = _colsum(jnp.where(_iota((8, 128), 0) == 7, gl, 0.0))
    a = gl + x_all[CHUNK * g:CHUNK * (g + 1)]
    m_new = jnp.maximum(gl + mprev, jnp.max(a, axis=0, keepdims=True))
    return m_new, jnp.exp(gl + mprev - m_new), jnp.exp(a - m_new)


def _ml_batched(q3, k3, v3, g_all, x_all, lane, C3, n3, mprev3):
    G = g_all.shape[0] // CHUNK
    gcol3 = _lanes_to_batch_cols(g_all, lane)
    onehot = jnp.where(_iota((G, 8, 128), 1) + 4 == _iota((G, 8, 128), 2), 1.0, 0.0).astype(F32)
    rows = _bdotx(onehot, x_all.reshape(G, CHUNK, 128), BNT)
    sub = _iota((G, 8, CHUNK), 1)
    row3 = jnp.stack([jnp.sum(jnp.where(sub == h, rows, 0.0), axis=1, keepdims=True) for h in range(HEADS)],
                     axis=1).reshape(G * HEADS, 1, CHUNK)
    causal = _iota((1, CHUNK, CHUNK), 1) >= _iota((1, CHUNK, CHUNK), 2)
    dmat = jnp.where(causal, gcol3 + row3, NEG)
    m_inter = gcol3 + mprev3
    m_t = jnp.maximum(m_inter, jnp.max(dmat, axis=2, keepdims=True))
    wi = jnp.exp(dmat - m_t)
    wn = jnp.exp(m_inter - m_t)
    s3 = _bdot(q3, k3, BNT) * wi
    qc = _bdot(q3, C3, BNN)
    qn = jnp.sum(q3 * n3, axis=2, keepdims=True)
    num = _bdot(s3, v3, BNN) + wn * qc
    den = jnp.sum(s3, axis=2, keepdims=True) + wn * qn
    floor = jnp.exp(-m_t)
    return dict(wi=wi, wn=wn, s=s3, qc=qc, qn=qn, num=num, den=den, floor=floor, nrm=jnp.maximum(jnp.abs(den), floor))


def _mlstm_fwd(qkc, u, comms=None):
    T = u.shape[0]
    G = min(GC, T // CHUNK)
    R = G * CHUNK
    N = T // CHUNK

    def body(qk_ref, v_ref, g_ref, h_ref, cst_ref, nst_ref, mst_ref, C_ref, n_ref, m_ref):
        @pl.when(pl.program_id(0) == 0)
        def _():
            C_ref[...] = jnp.zeros_like(C_ref)
            n_ref[...] = jnp.zeros_like(n_ref)
            m_ref[...] = jnp.zeros_like(m_ref)

        g_all, x_all, lane, _ = _ml_gates(g_ref[...])
        m_row = m_ref[...]
        mprev_rows, wo_rows, ws_parts = [], [], []
        for g in range(G):
            mprev_rows.append(m_row)
            m_row, wo, ws = _ml_chunk_rows(g_all, x_all, m_row, g)
            wo_rows.append(wo)
            ws_parts.append(ws)
        m_ref[...] = m_row
        mst_ref[...] = jnp.stack(mprev_rows, axis=0)
        ws3 = _lanes_to_batch_cols(jnp.concatenate(ws_parts, axis=0), lane)
        wo4 = _row_scalars(wo_rows).reshape(G, HEADS, 1, 1)
        q3 = _heads_to_batch(qk_ref[:, :256] * ML_SCALE, ML_DQK)
        k3 = _heads_to_batch(qk_ref[:, 256:], ML_DQK)
        v3 = _heads_to_batch(v_ref[...], 128)
        kw = k3 * ws3
        cloc = _bdot(kw, v3, BTN).reshape(G, HEADS, ML_DQK, 128)
        nloc = jnp.sum(kw, axis=1, keepdims=True).reshape(G, HEADS, 1, ML_DQK)
        C, nn = C_ref[...], n_ref[...]
        cs, ns = [], []
        for g in range(G):
            cs.append(C)
            ns.append(nn)
            C = wo4[g] * C + cloc[g]
            nn = wo4[g] * nn + nloc[g]
        C_ref[...] = C
        n_ref[...] = nn
        c4, n4 = jnp.stack(cs, axis=0), jnp.stack(ns, axis=0)
        cst_ref[...] = c4
        nst_ref[...] = n4
        r = _ml_batched(q3, k3, v3, g_all, x_all, lane, c4.reshape(G * HEADS, ML_DQK, 128),
                        n4.reshape(G * HEADS, 1, ML_DQK), _row_scalars(mprev_rows))
        h_ref[...] = _batch_to_heads(r["num"] / r["nrm"])

    return _hosted_call(
        body, comms, name="mlstm_fwd", grid=(N // G,),
        in_specs=[pl.BlockSpec((R, HALF), lambda n: (n, 0)), pl.BlockSpec((R, HALF), lambda n: (n, C_MV // 4)),
                  pl.BlockSpec((R, 128), lambda n: (n, C_GATES))],
        out_specs=[pl.BlockSpec((R, HALF), lambda n: (n, 0)),
                   pl.BlockSpec((G, HEADS, ML_DQK, 128), lambda n: (n, 0, 0, 0)),
                   pl.BlockSpec((G, HEADS, 1, ML_DQK), lambda n: (n, 0, 0, 0)),
                   pl.BlockSpec((G, 1, 128), lambda n: (n, 0, 0))],
        out_shape=[_sds((T, HALF)), _sds((N, HEADS, ML_DQK, 128)), _sds((N, HEADS, 1, ML_DQK)), _sds((N, 1, 128))],
        scratch_shapes=[pltpu.VMEM((HEADS, ML_DQK, 128), F32), pltpu.VMEM((HEADS, 1, ML_DQK), F32),
                        pltpu.VMEM((1, 128), F32)],
        args=(qkc, u, u))


def _mlstm_bwd(qkc, u, dh, cst, nst, mst):
    T = u.shape[0]
    G = min(GC, T // CHUNK)
    R = G * CHUNK
    NG = T // R

    def body(qk_ref, v_ref, g_ref, dh_ref, cst_ref, nst_ref, mst_ref, dqk_ref, dv_ref, dgt_ref, dC_ref, dn_ref):
        @pl.when(pl.program_id(0) == 0)
        def _():
            dC_ref[...] = jnp.zeros_like(dC_ref)
            dn_ref[...] = jnp.zeros_like(dn_ref)

        B = G * HEADS
        gates = g_ref[...]
        g_all, x_all, lane, rowmod = _ml_gates(gates)
        mprev_rows = [mst_ref[g] for g in range(G)]
        wo_rows, ws_parts = [], []
        for g in range(G):
            _, wo, ws = _ml_chunk_rows(g_all, x_all, mprev_rows[g], g)
            wo_rows.append(wo)
            ws_parts.append(ws)
        ws3 = _lanes_to_batch_cols(jnp.concatenate(ws_parts, axis=0), lane)
        wo3 = _row_scalars(wo_rows)
        wo4 = wo3.reshape(G, HEADS, 1, 1)
        q3 = _heads_to_batch(qk_ref[:, :256] * ML_SCALE, ML_DQK)
        k3 = _heads_to_batch(qk_ref[:, 256:], ML_DQK)
        v3 = _heads_to_batch(v_ref[...], 128)
        dh3 = _heads_to_batch(dh_ref[...], 128)
        C3 = cst_ref[...].reshape(B, ML_DQK, 128)
        n3 = nst_ref[...].reshape(B, 1, ML_DQK)
        r = _ml_batched(q3, k3, v3, g_all, x_all, lane, C3, n3, _row_scalars(mprev_rows))
        wn, s3 = r["wn"], r["s"]
        inv = 1.0 / r["nrm"]
        dnum = dh3 * inv
        dnrm = -jnp.sum(dh3 * (r["num"] * inv), axis=2, keepdims=True) * inv
        dden = jnp.where(jnp.abs(r["den"]) > r["floor"], dnrm * jnp.sign(r["den"]), 0.0)
        ds = _bdot(dnum, v3, BNT) + dden
        dqk = ds * r["wi"]
        dd = ds * s3
        qw = q3 * wn
        dcloc = _bdot(qw, dnum, BTN).reshape(G, HEADS, ML_DQK, 128)
        dnloc = jnp.sum(qw * dden, axis=1, keepdims=True).reshape(G, HEADS, 1, ML_DQK)
        dC, dn = dC_ref[...], dn_ref[...]
        dcs, dns = [None] * G, [None] * G
        for g in reversed(range(G)):
            dcs[g], dns[g] = dC, dn
            dC = wo4[g] * dC + dcloc[g]
            dn = wo4[g] * dn + dnloc[g]
        dC_ref[...] = dC
        dn_ref[...] = dn
        dC3 = jnp.stack(dcs, axis=0).reshape(B, ML_DQK, 128)
        dn3 = jnp.stack(dns, axis=0).reshape(B, 1, ML_DQK)
        dk_st = ws3 * (_bdot(v3, dC3, BNT) + dn3)
        dq = _bdot(dqk, k3, BNN) + wn * (_bdot(dnum, C3, BNT) + dden * n3)
        dk = _bdot(dqk, q3, BTN) + dk_st
        dv = _bdot(s3, dnum, BTN) + ws3 * _bdot(k3, dC3, BNN)
        dv_ref[...] = _batch_to_heads(dv).astype(dv_ref.dtype)
        dqk_ref[...] = jnp.concatenate([_batch_to_heads(dq * ML_SCALE), _batch_to_heads(dk)], axis=1)
        e_col = wn * (jnp.sum(dnum * r["qc"], axis=2, keepdims=True) + dden * r["qn"])
        c_col = jnp.sum(k3 * dk_st, axis=2, keepdims=True)
        z = wo3 * (jnp.sum(dC3 * C3, axis=(1, 2), keepdims=True) + jnp.sum(dn3 * n3, axis=(1, 2), keepdims=True))
        dd_hi = dd.astype(_MXU).astype(F32)
        ones = jnp.ones((B, CHUNK, 128), F32)
        dd_cols = (_bdot(dd_hi, ones, BTN) + _bdot(dd - dd_hi, ones, BTN))[:, :, 0:1]
        last = _iota((1, CHUNK, 1), 1) == CHUNK - 1
        dg3 = jnp.sum(dd, axis=2, keepdims=True) - dd_cols + e_col - c_col
        dg3 = dg3 + jnp.where(last, jnp.sum(c_col, axis=1, keepdims=True) + z, 0.0)
        di3 = dd_cols + c_col

        def to_lanes(x3, first):
            x4 = x3.reshape(G, HEADS, CHUNK, 1)
            return sum(jnp.where(lane == first + h, x4[:, h].reshape(R, 1), 0.0) for h in range(HEADS))

        dlf = _chunk_cumsum(to_lanes(dg3, 4), rowmod, reverse=True)
        dgt_ref[...] = (to_lanes(di3, 0) + dlf * _sig(-gates)).astype(dgt_ref.dtype)

    rev = lambda w, c: pl.BlockSpec((R, w), lambda i, c=c: (NG - 1 - i, c))
    st = lambda *s: pl.BlockSpec((G,) + s, lambda i: (NG - 1 - i,) + (0,) * len(s))
    return pl.pallas_call(
        body, name="mlstm_bwd", grid=(NG,),
        in_specs=[rev(HALF, 0), rev(HALF, C_MV // 4), rev(128, C_GATES), rev(HALF, 0),
                  st(HEADS, ML_DQK, 128), st(HEADS, 1, ML_DQK), st(1, 128)],
        out_specs=[rev(HALF, 0), rev(HALF, 0), rev(128, 0)],
        out_shape=[_sds((T, HALF)), _sds((T, HALF), _MXU), _sds((T, 128), _MXU)],
        scratch_shapes=[pltpu.VMEM((HEADS, ML_DQK, 128), F32), pltpu.VMEM((HEADS, 1, ML_DQK), F32)],
        compiler_params=_cp("arbitrary"))(qkc, u, u, dh, cst, nst, mst)


def _head_norm(o):
    rs_parts, r_parts = [], []
    for h in range(HEADS):
        oh = o[:, 128 * h:128 * (h + 1)]
        rs = lax.rsqrt(jnp.mean(oh * oh, axis=-1, keepdims=True) + RMS_EPS)
        rs_parts.append(rs)
        r_parts.append(oh * rs)
    return jnp.concatenate(r_parts, axis=1), rs_parts


def _out_proj(x, u, o_hg, h_ml, g_hg, g_ml, w_out, tm):
    T = x.shape[0]

    def body(x_ref, hgate_ref, mo_ref, ohg_ref, hml_ref, ghg_ref, gml_ref, w_ref, m_ref, z_ref):
        hgate = hgate_ref[...]
        a = _head_norm(ohg_ref[...])[0] * ghg_ref[...] * (hgate * _sig(hgate))
        b = _head_norm(hml_ref[...])[0] * gml_ref[...] * _sig(mo_ref[...])
        m = jnp.concatenate([a, b], axis=1)
        m_ref[...] = m.astype(m_ref.dtype)
        z_ref[...] = ALPHA * x_ref[...] + _dot(m, w_ref[...])

    return pl.pallas_call(
        body, name="out_proj", grid=(T // tm,),
        in_specs=[_row(tm, D_MODEL), _row(tm, HALF, C_HGATE // 4), _row(tm, HALF, C_MO // 4),
                  _row(tm, HALF), _row(tm, HALF), _full(g_hg.shape), _full(g_ml.shape), _full(w_out.shape)],
        out_specs=[_row(tm, D_MODEL)] * 2,
        out_shape=[_sds((T, D_MODEL), _MXU), _sds((T, D_MODEL))],
        compiler_params=_cp("parallel"))(x, u, u, o_hg, h_ml, g_hg, g_ml, w_out)


def _ffn_ln(z1, ln1_g, ln1_b, wg, wu, wd, ln_g, ln_b, tm):
    T = z1.shape[0]

    def body(z1_ref, g1_ref, b1_ref, wg_ref, wu_ref, wd_ref, g_ref, b_ref, z_ref, x1_ref, x2_ref, a_ref, bb_ref, h_ref):
        x = _ln_fwd(z1_ref[...], g1_ref[...], b1_ref[...])[0]
        x1_ref[...] = x.astype(x1_ref.dtype)
        a = _dot(x, wg_ref[...], NT)
        bb = _dot(x, wu_ref[...], NT)
        hh = a * _sig(a) * bb
        a_ref[...] = a.astype(a_ref.dtype)
        bb_ref[...] = bb.astype(bb_ref.dtype)
        h_ref[...] = hh.astype(h_ref.dtype)
        z = ALPHA * x + _dot(hh, wd_ref[...])
        z_ref[...] = z
        x2_ref[...] = _ln_fwd(z, g_ref[...], b_ref[...])[0].astype(x2_ref.dtype)

    vec = _full((1, D_MODEL))
    return pl.pallas_call(
        body, name="ffn_ln2", grid=(T // tm,),
        in_specs=[_row(tm, D_MODEL), vec, vec, _full(wg.shape), _full(wu.shape), _full(wd.shape), vec, vec],
        out_specs=[_row(tm, D_MODEL)] * 3 + [_row(tm, D_FF)] * 3,
        out_shape=[_sds((T, D_MODEL))] + [_sds((T, D_MODEL), _MXU)] * 2 + [_sds((T, D_FF), _MXU)] * 3,
        compiler_params=_cp("parallel"))(z1, ln1_g, ln1_b, wg, wu, wd, ln_g, ln_b)


def _ple_loss_ln2_bwd(z2, p, tgt, wpg, bpg, wpp, ln_g, ln_b, tm):
    T = z2.shape[0]

    def body(z_ref, p_ref, t_ref, wpg_ref, bpg_ref, wpp_ref, g_ref, b_ref,
             de_ref, dgp_ref, dz_ref, loss_ref, dbpg_ref, dg_ref, db_ref):
        @pl.when(pl.program_id(0) == 0)
        def _():
            for r in (loss_ref, dbpg_ref, dg_ref, db_ref):
                r[...] = jnp.zeros_like(r)

        x2, xhat, rstd = _ln_fwd(z_ref[...], g_ref[...], b_ref[...])
        gate = _sig(_dot(x2, wpg_ref[...]) + bpg_ref[...])
        e = _dot(p_ref[...], wpp_ref[...])
        err = x2 + gate * e - t_ref[...]
        loss_ref[...] += _colsum(err * err)
        dy = err * (1.0 / D_MODEL)
        de_ref[...] = (dy * gate).astype(de_ref.dtype)
        dgp = dy * e * gate * (1.0 - gate)
        dgp_ref[...] = dgp.astype(dgp_ref.dtype)
        dbpg_ref[...] += _colsum(dgp)
        dx2 = dy + _dot(dgp, wpg_ref[...], NT)
        dg_ref[...] += _colsum(dx2 * xhat)
        db_ref[...] += _colsum(dx2)
        dz_ref[...] = _ln_bwd(dx2, xhat, rstd, g_ref[...])

    vec = _full((1, D_MODEL))
    return pl.pallas_call(
        body, name="ple_loss_ln2_bwd", grid=(T // tm,),
        in_specs=[_row(tm, D_MODEL), _row(tm, PLE_DIM), _row(tm, D_MODEL),
                  _full(wpg.shape), vec, _full(wpp.shape), vec, vec],
        out_specs=[_row(tm, D_MODEL)] * 3 + [vec] * 4,
        out_shape=[_sds((T, D_MODEL), _MXU)] * 2 + [_sds((T, D_MODEL))] + [_sds((1, D_MODEL))] * 4,
        compiler_params=_cp("arbitrary"))(z2, p, tgt, wpg, bpg, wpp, ln_g, ln_b)


def _ffn_bwd_ln1_bwd(a_pre, b_pre, z1, dz2, wg, wu, wd, ln_g, ln_b, tm, comms=None):
    T = z1.shape[0]

    def body(a_ref, bb_ref, z_ref, dz2_ref, wg_ref, wu_ref, wd_ref, g_ref, b_ref,
             da_ref, dbb_ref, dz1_ref, dg_ref, db_ref):
        @pl.when(pl.program_id(0) == 0)
        def _():
            dg_ref[...] = jnp.zeros_like(dg_ref)
            db_ref[...] = jnp.zeros_like(db_ref)

        dz2 = dz2_ref[...]
        a = a_ref[...].astype(F32)
        bb = bb_ref[...].astype(F32)
        sa = _sig(a)
        act = a * sa
        dh = _dot(dz2, wd_ref[...], NT)
        da = (dh * bb * _dsilu(a, sa)).astype(da_ref.dtype)
        dbb = (dh * act).astype(dbb_ref.dtype)
        da_ref[...] = da
        dbb_ref[...] = dbb
        dx1 = ALPHA * dz2 + _dot(da, wg_ref[...]) + _dot(dbb, wu_ref[...])
        _, xhat, rstd = _ln_fwd(z_ref[...], g_ref[...], b_ref[...])
        dg_ref[...] += _colsum(dx1 * xhat)
        db_ref[...] += _colsum(dx1)
        dz1_ref[...] = _ln_bwd(dx1, xhat, rstd, g_ref[...])

    vec = _full((1, D_MODEL))
    return _hosted_call(
        body, comms, name="ffn_bwd_ln1_bwd", grid=(T // tm,),
        in_specs=[_row(tm, D_FF)] * 2 + [_row(tm, D_MODEL)] * 2 + [_full(wg.shape), _full(wu.shape), _full(wd.shape), vec, vec],
        out_specs=[_row(tm, D_FF)] * 2 + [_row(tm, D_MODEL), vec, vec],
        out_shape=[_sds((T, D_FF), _MXU)] * 2 + [_sds((T, D_MODEL)), _sds((1, D_MODEL)), _sds((1, D_MODEL))],
        scratch_shapes=[], args=(a_pre, b_pre, z1, dz2, wg, wu, wd, ln_g, ln_b))


def _out_proj_bwd(dz1, u, o_hg, h_ml, g_hg, g_ml, w_out, tm):
    T = dz1.shape[0]

    def body(dz_ref, hgate_ref, mo_ref, ohg_ref, hml_ref, ghg_ref, gml_ref, w_ref,
             dohg_ref, dhml_ref, dhgate_ref, dmo_ref, dghg_ref, dgml_ref):
        @pl.when(pl.program_id(0) == 0)
        def _():
            dghg_ref[...] = jnp.zeros_like(dghg_ref)
            dgml_ref[...] = jnp.zeros_like(dgml_ref)

        dm = _dot(dz_ref[...], w_ref[...], NT)

        def half(dmh, o, gvec, gate_val, dgate_fac, do_ref, dgate_ref, dgvec_ref):
            r, rs = _head_norm(o)
            dgate_ref[...] = (dmh * r * gvec * dgate_fac).astype(dgate_ref.dtype)
            dn = dmh * gate_val
            dgvec_ref[...] += _colsum(dn * r)
            dr = dn * gvec
            parts = []
            for h in range(HEADS):
                sl = slice(128 * h, 128 * (h + 1))
                parts.append(rs[h] * (dr[:, sl] - r[:, sl] * jnp.mean(dr[:, sl] * r[:, sl], axis=-1, keepdims=True)))
            do_ref[...] = jnp.concatenate(parts, axis=1)

        hg = hgate_ref[...]
        shg = _sig(hg)
        half(dm[:, :HALF], ohg_ref[...], ghg_ref[...], hg * shg, _dsilu(hg, shg), dohg_ref, dhgate_ref, dghg_ref)
        smo = _sig(mo_ref[...])
        half(dm[:, HALF:], hml_ref[...], gml_ref[...], smo, smo * (1.0 - smo), dhml_ref, dmo_ref, dgml_ref)

    vec = _full((1, HALF))
    return pl.pallas_call(
        body, name="out_proj_bwd", grid=(T // tm,),
        in_specs=[_row(tm, D_MODEL), _row(tm, HALF, C_HGATE // 4), _row(tm, HALF, C_MO // 4),
                  _row(tm, HALF), _row(tm, HALF), vec, vec, _full(w_out.shape)],
        out_specs=[_row(tm, HALF)] * 4 + [vec, vec],
        out_shape=[_sds((T, HALF))] * 2 + [_sds((T, HALF), _MXU)] * 2 + [_sds((1, HALF))] * 2,
        compiler_params=_cp("arbitrary"))(dz1, u, u, o_hg, h_ml, g_hg, g_ml, w_out)


def _conv_bwd(u, pre, dqkc, cw, tm):
    T = u.shape[0]
    hb = tm // 8
    nb = T // 8

    def body(x_ref, xh_ref, pre_ref, preh_ref, d_ref, dh_ref, w_ref, dx_ref, dw_ref, db_ref):
        i = pl.program_id(0)

        @pl.when(i == 0)
        def _():
            dw_ref[...] = jnp.zeros_like(dw_ref)
            db_ref[...] = jnp.zeros_like(db_ref)

        def dpre_of(pre, d):
            return d * _dsilu(pre, _sig(pre))

        rowi = _iota((8, HALF), 0)
        dpre = dpre_of(pre_ref[...], d_ref[...])
        dpre_next = jnp.where(i < pl.num_programs(0) - 1, dpre_of(preh_ref[...], dh_ref[...]), 0.0)
        x = x_ref[...]
        xprev = jnp.where(i > 0, xh_ref[...], 0.0)
        dx = dpre * w_ref[3:4, :]
        db_ref[...] += _colsum(dpre)
        dws = [None] * 4
        dws[3] = _colsum(dpre * x)
        for j in (1, 2, 3):
            dx = dx + _shift_rows_up(dpre, dpre_next, j, rowi) * w_ref[3 - j:4 - j, :]
            dws[3 - j] = _colsum(dpre * _shift_rows(x, xprev, j, rowi))
        dx_ref[...] = dx.astype(dx_ref.dtype)
        dw_ref[...] += jnp.concatenate(dws, axis=0)

    cur = lambda blk: pl.BlockSpec((tm, HALF), lambda i, blk=blk: (i, blk))
    nxt = pl.BlockSpec((8, HALF), lambda i: (jnp.minimum((i + 1) * hb, nb - 1), 0))
    return pl.pallas_call(
        body, name="conv_bwd", grid=(T // tm,),
        in_specs=[cur(C_MQK // 4), pl.BlockSpec((8, HALF), lambda i: (jnp.maximum(i * hb - 1, 0), C_MQK // 4)),
                  cur(0), nxt, cur(0), nxt, _full(cw.shape)],
        out_specs=[cur(0), _full((4, HALF)), _full((1, HALF))],
        out_shape=[_sds((T, HALF), _MXU), _sds((4, HALF)), _sds((1, HALF))],
        compiler_params=_cp("arbitrary"))(u, u, pre, pre, dqkc, dqkc, cw)


def _du_specs(rows):
    return [pl.BlockSpec((rows, w), lambda i: (i, 0)) for w in DU_WIDTHS]


def _in_proj_bwd(dz1, du_parts, w, tm, comms=None):
    T = dz1.shape[0]

    def body(dz_ref, *refs):
        du = jnp.concatenate([r[...] for r in refs[:8]], axis=1)
        refs[9][...] = ALPHA * dz_ref[...] + _dot(du, refs[8][...], NT)

    (dx,), got = _hosted_call(
        body, comms, name="in_proj_bwd", grid=(T // tm,),
        in_specs=[_row(tm, D_MODEL)] + _du_specs(tm) + [_full(w.shape)],
        out_specs=[_row(tm, D_MODEL)], out_shape=[_sds((T, D_MODEL))], scratch_shapes=[], args=(dz1, *du_parts, w))
    return dx, got


def _wgrad(a, b, name, tm, tn, tk):
    T, M = a.shape
    N = b.shape[1]
    tm, tn, tk = min(tm, M), min(tn, N), min(tk, T)
    nk = T // tk

    def body(a_ref, b_ref, o_ref, acc_ref):
        kk = pl.program_id(2)

        @pl.when(kk == 0)
        def _():
            acc_ref[...] = jnp.zeros_like(acc_ref)

        acc_ref[...] += _dot(a_ref[...], b_ref[...], TN)

        @pl.when(kk == nk - 1)
        def _():
            o_ref[...] = acc_ref[...].astype(o_ref.dtype)

    return pl.pallas_call(
        body, name=name, grid=(M // tm, N // tn, nk),
        in_specs=[pl.BlockSpec((tk, tm), lambda i, j, kk: (kk, i)), pl.BlockSpec((tk, tn), lambda i, j, kk: (kk, j))],
        out_specs=pl.BlockSpec((tm, tn), lambda i, j, kk: (i, j)), out_shape=_sds((M, N), _MXU),
        scratch_shapes=[pltpu.VMEM((tm, tn), F32)],
        compiler_params=_cp("parallel", "parallel", "arbitrary"))(a, b)


W_IN_PARTS = 2


def _wgrad_w_in(x, du_parts, tk, part, comms=None):
    T = x.shape[0]
    M = D_MODEL // W_IN_PARTS
    tk = min(tk, T)
    nk = T // tk

    def body(a_ref, *refs):
        o_ref, cs_ref, acc_ref = refs[8:]
        kk = pl.program_id(0)

        @pl.when(kk == 0)
        def _():
            acc_ref[...] = jnp.zeros_like(acc_ref)
            cs_ref[...] = jnp.zeros_like(cs_ref)

        du = jnp.concatenate([r[...] for r in refs[:8]], axis=1)
        acc_ref[...] += _dot(a_ref[...], du, TN)
        cs_ref[...] += _colsum(du.astype(F32))

        @pl.when(kk == nk - 1)
        def _():
            o_ref[...] = acc_ref[...].astype(o_ref.dtype)

    return _hosted_call(
        body, comms, name="wgrad_w_in_%d" % part, grid=(nk,),
        in_specs=[pl.BlockSpec((tk, M), lambda kk: (kk, part))] + _du_specs(tk),
        out_specs=[_full((M, PROJ_WP)), _full((1, PROJ_WP))],
        out_shape=[_sds((M, PROJ_WP), _MXU), _sds((1, PROJ_WP))],
        scratch_shapes=[pltpu.VMEM((M, PROJ_WP), F32)], args=(x, *du_parts))


W_IN_S, FF_S, OUT_S, PP_S = PROJ_W // N_DEV, D_FF // N_DEV, D_MODEL // N_DEV, D_MODEL // N_DEV
LATE = ("w_ffn_gate", "w_ffn_up", "w_out", "w_ffn_down", "ple_w_gate", "ple_w_proj")
BIG = ("w_in",) + LATE
TRANSPOSED = ("w_ffn_gate", "w_ffn_up")


def _split_cols(a, n):
    return a.reshape(a.shape[0], N_DEV, n).transpose(1, 0, 2)


def _join_cols(a):
    return a.transpose(1, 0, 2).reshape(a.shape[1], -1)


def _step(x, p, tgt, w_in, b_in, lb_logits, conv_w, conv_b, g_hg, g_ml, ln1_g, ln1_b, ln2_g, ln2_b, bpg, late,
          distributed):
    T = x.shape[0]
    tm, tf = min(ROWS, T), min(ROWS_FFN, T)
    gather = lambda *names: [_GatherTwoLevel([late[n] for n in names])] if distributed else None
    scatter = lambda *arrs: [_Comm("scatter", list(arrs))] if distributed else None
    rows = lambda a, n: a.reshape(N_DEV, n, D_MODEL)
    (u, pre, qkc), got1 = _in_proj(x, w_in, b_in, conv_w, conv_b, tm, gather("w_out", "ple_w_gate", "ple_w_proj"))
    (o_hg, hg_states), got2 = _hgrn2_fwd(u, lb_logits, gather("w_ffn_gate", "w_ffn_up"))
    (h_ml, cst, nst, mst), got3 = _mlstm_fwd(qkc, u, gather("w_ffn_down"))
    if distributed:
        w_out, wpg, wpp = got1[0][0].reshape(D_MODEL, D_MODEL), got1[0][1].reshape(D_MODEL, D_MODEL), _join_cols(got1[0][2])
        wg, wu, wd = (a.reshape(D_FF, D_MODEL) for a in (got2[0][0], got2[0][1], got3[0][0]))
    else:
        w_out, wg, wu, wd, wpg, wpp = (late[n] for n in ("w_out", "w_ffn_gate", "w_ffn_up", "w_ffn_down", "ple_w_gate", "ple_w_proj"))
    m_in, z1 = _out_proj(x, u, o_hg, h_ml, g_hg, g_ml, w_out, tm)
    z2, x1, x2, a_pre, b_pre, hh = _ffn_ln(z1, ln1_g, ln1_b, wg, wu, wd, ln2_g, ln2_b, tf)
    de, dgp, dz2, loss_vec, d_bpg, d_ln2g, d_ln2b = _ple_loss_ln2_bwd(z2, p, tgt, wpg, bpg, wpp, ln2_g, ln2_b, tm)
    big = dict(ple_w_gate=_wgrad(x2, dgp, "wgrad_ple_gate", 512, D_MODEL, 1024),
               ple_w_proj=_wgrad(p, de, "wgrad_ple_proj", 512, D_MODEL, 1024))
    (da, dbb, dz1, d_ln1g, d_ln1b), r1 = _ffn_bwd_ln1_bwd(
        a_pre, b_pre, z1, dz2, wg, wu, wd, ln1_g, ln1_b, tf, scatter(rows(big["ple_w_gate"], OUT_S), _split_cols(big["ple_w_proj"], PP_S)))
    big.update(
        w_ffn_gate=_wgrad(da, x1, "wgrad_ffn_gate", D_FF, D_MODEL, 1024),
        w_ffn_up=_wgrad(dbb, x1, "wgrad_ffn_up", D_FF, D_MODEL, 1024),
        w_ffn_down=_wgrad(hh, dz2, "wgrad_ffn_down", D_FF, D_MODEL, 1024),
        w_out=_wgrad(m_in, dz1, "wgrad_w_out", 512, D_MODEL, 1024))
    d_ohg, d_hml, d_hgate, d_mo, d_ghg, d_gml = _out_proj_bwd(dz1, u, o_hg, h_ml, g_hg, g_ml, w_out, tm)
    (d_hq, d_hf, d_hv, d_lb), r2 = _hgrn2_bwd(
        u, lb_logits, d_ohg, hg_states,
        scatter(rows(big["w_ffn_gate"], FF_S), rows(big["w_ffn_up"], FF_S), rows(big["w_ffn_down"], FF_S),
                rows(big["w_out"], OUT_S)))
    d_qkc, d_mv, d_gates = _mlstm_bwd(qkc, u, d_hml, cst, nst, mst)
    d_mqk, d_convw, d_convb = _conv_bwd(u, pre, d_qkc, conv_w, tm)
    du_parts = [d_hq, d_hf, d_hv, d_hgate, d_mqk, d_mv, d_mo, d_gates]
    own = lambda g: _split_cols(g[:, :PROJ_W], W_IN_S)
    (g_in0, d_bin), _ = _wgrad_w_in(x, du_parts, 512, 0)
    (g_in1, _), r_in0 = _wgrad_w_in(x, du_parts, 512, 1, scatter(own(g_in0)))
    big["w_in"] = jnp.concatenate([g_in0, g_in1], axis=0)
    small = dict(b_in=d_bin[:, :PROJ_W], hg_lb_logits=d_lb, ml_conv_w=d_convw, ml_conv_b=d_convb, hg_norm_g=d_ghg,
                 ml_norm_g=d_gml, ln1_g=d_ln1g, ln1_b=d_ln1b, ln2_g=d_ln2g, ln2_b=d_ln2b, ple_b_gate=d_bpg)
    last = [_Comm("scatter", [own(g_in1)]), _Comm("gather", [loss_vec] + [small[n] for n in SMALL])] if distributed else None
    dx, r3 = _in_proj_bwd(dz1, du_parts, w_in, tm, last)
    gathered_small = None
    if distributed:
        big = dict(ple_w_gate=r1[0][0], ple_w_proj=r1[0][1], w_ffn_gate=r2[0][0], w_ffn_up=r2[0][1],
                   w_ffn_down=r2[0][2], w_out=r2[0][3], w_in=[r_in0[0][0], r3[0][0]])
        gathered_small = r3[1]
    return loss_vec, dx, big, small, gathered_small


SMALL = ("b_in", "hg_lb_logits", "ml_conv_w", "ml_conv_b", "hg_norm_g", "ml_norm_g", "ln1_g", "ln1_b", "ln2_g", "ln2_b",
         "ple_b_gate")


def _padc(a, n):
    return jnp.pad(a, [(0, 0)] * (a.ndim - 1) + [(0, n - a.shape[-1])])


def _adamw(w, g, m, v):
    m = B1 * m + (1.0 - B1) * g
    v = B2 * v + (1.0 - B2) * jnp.square(g)
    m_hat = m / (1.0 - B1 ** STEP)
    v_hat = v / (1.0 - B2 ** STEP)
    return -LR * (m_hat / (jnp.sqrt(v_hat) + EPS) + WD * w), m, v


def _sum_slabs(ref):
    g = ref[0].astype(F32)
    for j in range(1, N_DEV):
        g = g + ref[j].astype(F32)
    return g


def _adamw_matrix(rbs, w, m, v, name):
    rbs = list(rbs) if isinstance(rbs, (list, tuple)) else [rbs]
    nb = len(rbs)
    R, C = w.shape
    tr = 256 if (R // nb) % 256 == 0 else R // nb
    per = R // nb // tr

    def body(*refs):
        w_ref, m_ref, v_ref, g_ref, d_ref, m2_ref, v2_ref = refs[nb:]
        i = pl.program_id(0)
        g = _sum_slabs(refs[0])
        for k in range(1, nb):
            g = jnp.where(i >= k * per, _sum_slabs(refs[k]), g)
        g_ref[...] = g
        d_ref[...], m2_ref[...], v2_ref[...] = _adamw(w_ref[...], g, m_ref[...], v_ref[...])

    blk = pl.BlockSpec((tr, C), lambda i: (i, 0))
    part = lambda k: pl.BlockSpec((N_DEV, tr, C), lambda i, k=k: (0, jnp.clip(i - k * per, 0, per - 1), 0))
    return pl.pallas_call(
        body, name=name, grid=(R // tr,),
        in_specs=[part(k) for k in range(nb)] + [blk, blk, blk],
        out_specs=[blk] * 4, out_shape=[_sds((R, C))] * 4, compiler_params=_cp("parallel"))(*rbs, w, m, v)


def _adamw_small(loss_g, gs, ws, ms, vs):
    n = len(ws)

    def body(*refs):
        loss_ref, g_refs, w_refs, m_refs, v_refs = refs[0], refs[1:1 + n], refs[1 + n:1 + 2 * n], refs[1 + 2 * n:1 + 3 * n], refs[1 + 3 * n:1 + 4 * n]
        outs = refs[1 + 4 * n:]
        outs[0][...] = (0.5 / D_MODEL) * jnp.sum(_sum_slabs(loss_ref), keepdims=True)
        for i in range(n):
            g = _sum_slabs(g_refs[i])
            outs[1 + i][...] = g
            outs[1 + n + i][...], outs[1 + 2 * n + i][...], outs[1 + 3 * n + i][...] = _adamw(
                w_refs[i][...], g, m_refs[i][...], v_refs[i][...])

    res = pl.pallas_call(
        body, name="adamw_small", out_shape=[_sds((1, 1))] + [_sds(w.shape) for w in ws] * 4)(loss_g, *gs, *ws, *ms, *vs)
    return res[0], [res[1 + k * n:1 + (k + 1) * n] for k in range(4)]


WEIGHTS = ("w_in", "b_in", "hg_lb_logits", "ml_conv_w", "ml_conv_b", "hg_norm_g", "ml_norm_g", "w_out", "ln1_g", "ln1_b",
           "w_ffn_gate", "w_ffn_up", "w_ffn_down", "ln2_g", "ln2_b", "ple_w_proj", "ple_w_gate", "ple_b_gate")
CONV_S = HALF // N_DEV


def kernel(x, p, w_in, b_in, hg_lb_logits, ml_conv_w, ml_conv_b, hg_norm_g, ml_norm_g, w_out, ln1_g, ln1_b, w_ffn_gate, w_ffn_up, w_ffn_down, ln2_g, ln2_b, ple_w_proj, ple_w_gate, ple_b_gate, loss_target, m_w_in, m_b_in, m_hg_lb_logits, m_ml_conv_w, m_ml_conv_b, m_hg_norm_g, m_ml_norm_g, m_w_out, m_ln1_g, m_ln1_b, m_w_ffn_gate, m_w_ffn_up, m_w_ffn_down, m_ln2_g, m_ln2_b, m_ple_w_proj, m_ple_w_gate, m_ple_b_gate, v_w_in, v_b_in, v_hg_lb_logits, v_ml_conv_w, v_ml_conv_b, v_hg_norm_g, v_ml_norm_g, v_w_out, v_ln1_g, v_ln1_b, v_w_ffn_gate, v_w_ffn_up, v_w_ffn_down, v_ln2_g, v_ln2_b, v_ple_w_proj, v_ple_w_gate, v_ple_b_gate):
    args = locals()
    me = 4 * lax.axis_index("x") + 2 * lax.axis_index("y") + lax.axis_index("c")
    shapes = {n: args[n].shape for n in WEIGHTS}
    def drop(n, a):
        a = a[0] if n in BIG or n == "ml_conv_w" else a
        return a.T if n in TRANSPOSED else a

    W = {n: drop(n, args[n]) for n in WEIGHTS}
    M = {n: drop(n, args["m_" + n]) for n in WEIGHTS}
    V = {n: drop(n, args["v_" + n]) for n in WEIGHTS}

    g_in, g_conv = _gather_two_level(
        [W["w_in"].astype(_MXU), jnp.pad(W["ml_conv_w"], ((0, 4), (0, 128 - CONV_S)))], "gather_w_in")
    w_in_full = _padc(_join_cols(g_in), PROJ_WP)
    conv_full = _join_cols(g_conv[:, :4, :CONV_S])

    _, dx, big, _, sg = _step(
        x[0], p[0, 0], loss_target[0], w_in_full, _padc(b_in, PROJ_WP), hg_lb_logits, conv_full, ml_conv_b,
        hg_norm_g, ml_norm_g, ln1_g, ln1_b, ln2_g, ln2_b, ple_b_gate, {n: W[n].astype(_MXU) for n in LATE}, True)

    upd = {n: _adamw_matrix(big[n], W[n], M[n], V[n], "adamw_" + n) for n in BIG}
    sg = dict(zip(SMALL, sg[1:]), loss=sg[0])
    sg["ml_conv_w"] = lax.dynamic_slice(sg["ml_conv_w"], (0, 0, me * CONV_S), (N_DEV, 4, CONV_S))
    loss, small_upd = _adamw_small(sg["loss"], *[[d[n] for n in SMALL] for d in (sg, W, M, V)])

    outs = []
    for kind in range(4):
        smalls = dict(zip(SMALL, small_upd[kind]))
        for n in WEIGHTS:
            o = upd[n][kind] if n in BIG else smalls[n]
            outs.append((o.T if n in TRANSPOSED else o).reshape(shapes[n]))
    return (loss.reshape(()), dx.reshape(x.shape), *outs)
```

```python
import jax
import jax.numpy as jnp
from jax import lax
from jax.experimental import pallas as pl
from jax.experimental.pallas import tpu as pltpu

F32 = jnp.float32
_MXU = jnp.bfloat16

D_MODEL = 1024
CHUNK = 64
SUB = 16
PLE_DIM = 256
HEADS = 4
ML_DQK = 64
HALF = 512
D_FF = 2816
PROJ_W = 3592
PROJ_WP = 3712
ALPHA = float(2 ** 0.25)
LN_EPS = 1e-5
RMS_EPS = 1e-6
ML_SCALE = ML_DQK ** -0.5
N_DEV = 8
LR, B1, B2, EPS, WD, STEP = 0.001, 0.9, 0.999, 1e-08, 0.01, 10
NEG = -1e30
LOG2E = 1.4426950408889634

C_HQ, C_HF, C_HV, C_HGATE, C_MQK, C_MV, C_MO, C_GATES = 0, 4, 8, 12, 16, 20, 24, 28
DU_WIDTHS = (HALF,) * 7 + (128,)

VMEM_LIMIT = 52 * 1024 * 1024
GC = 8
ROWS = 512
ROWS_FFN = 256

NN = (((1,), (0,)), ((), ()))
NT = (((1,), (1,)), ((), ()))
TN = (((0,), (0,)), ((), ()))
BNT = (((2,), (2,)), ((0,), (0,)))
BNN = (((2,), (1,)), ((0,), (0,)))
BTN = (((1,), (1,)), ((0,), (0,)))


def _dot(a, b, dims=NN):
    return lax.dot_general(a.astype(_MXU), b.astype(_MXU), dims, preferred_element_type=F32)


def _dotx(a, b, dims=NN):
    return lax.dot_general(a, b, dims, precision=lax.Precision.HIGHEST, preferred_element_type=F32)


def _sig(x):
    return jax.nn.sigmoid(x)


def _cp(*sem):
    return pltpu.CompilerParams(dimension_semantics=sem, vmem_limit_bytes=VMEM_LIMIT)


def _row(tm, c, blk=0):
    return pl.BlockSpec((tm, c), lambda i, blk=blk: (i, blk))


def _full(shape):
    nd = len(shape)
    return pl.BlockSpec(tuple(shape), lambda *_, nd=nd: (0,) * nd)


def _sds(shape, dtype=F32):
    return jax.ShapeDtypeStruct(tuple(shape), dtype)


def _iota(shape, axis):
    return lax.broadcasted_iota(jnp.int32, shape, axis)


def _colsum(x):
    return jnp.sum(x, axis=0, keepdims=True)


def _rowsum(x):
    return jnp.sum(x, axis=1, keepdims=True)


def _ln_fwd(z, g, b):
    mu = jnp.mean(z, axis=-1, keepdims=True)
    zc = z - mu
    var = jnp.mean(zc * zc, axis=-1, keepdims=True)
    rstd = lax.rsqrt(var + LN_EPS)
    xhat = zc * rstd
    return xhat * g + b, xhat, rstd


def _ln_bwd(dy, xhat, rstd, g):
    dxh = dy * g
    m1 = jnp.mean(dxh, axis=-1, keepdims=True)
    m2 = jnp.mean(dxh * xhat, axis=-1, keepdims=True)
    return rstd * (dxh - m1 - xhat * m2)


def _dsilu(x, s):
    return s * (1.0 + x * (1.0 - s))


MESH = pl.DeviceIdType.MESH
ANY = pl.BlockSpec(memory_space=pl.ANY)


def _flip(v, bit):
    return 1 - v if bit else v


class _Comm:
    def __init__(self, kind, srcs):
        self.kind, self.srcs, self.n = kind, list(srcs), len(srcs)

    def out_shape(self):
        lead = (N_DEV,) if self.kind == "gather" else ()
        return [jax.ShapeDtypeStruct(lead + s.shape, s.dtype) for s in self.srcs]

    def scratch(self):
        return [pltpu.SemaphoreType.DMA((7 * self.n,)), pltpu.SemaphoreType.DMA((7 * self.n,)),
                pltpu.SemaphoreType.DMA((self.n,))]

    def copies(self, srcs, dsts, send_sems, recv_sems, local_sems):
        x, y, c = lax.axis_index("x"), lax.axis_index("y"), lax.axis_index("c")
        me = 4 * x + 2 * y + c
        pick = (lambda s, j: s) if self.kind == "gather" else (lambda s, j: s.at[j])
        out = []
        for i, (s, d) in enumerate(zip(srcs, dsts)):
            out.append(pltpu.make_async_copy(pick(s, me), d.at[me], local_sems.at[i]))
            for k in range(1, N_DEV):
                px, py, pc = _flip(x, k & 4), _flip(y, k & 2), _flip(c, k & 1)
                out.append(pltpu.make_async_remote_copy(
                    src_ref=pick(s, 4 * px + 2 * py + pc), dst_ref=d.at[me], send_sem=send_sems.at[7 * i + k - 1],
                    recv_sem=recv_sems.at[7 * i + k - 1], device_id=(px, py, pc), device_id_type=MESH))
        return out

    def start(self, *refs):
        for cp in self.copies(*refs):
            cp.start()

    def mid(self, *refs):
        pass

    def finish(self, *refs):
        for cp in self.copies(*refs):
            cp.wait()


class _GatherTwoLevel(_Comm):
    def __init__(self, srcs):
        super().__init__("gather", srcs)

    def _parts(self, srcs, dsts, send_sems, recv_sems, local_sems):
        x, y, c = lax.axis_index("x"), lax.axis_index("y"), lax.axis_index("c")
        me, sibling = (x, y, c), (x, y, 1 - c)
        chips = [(1 - x, y), (x, 1 - y), (1 - x, 1 - y)]

        def copy(i, k, block, to, own=False):
            slab = dsts[i].at[4 * block[0] + 2 * block[1] + block[2]]
            return pltpu.make_async_remote_copy(
                src_ref=srcs[i] if own else slab, dst_ref=slab, send_sem=send_sems.at[7 * i + k],
                recv_sem=recv_sems.at[7 * i + k], device_id=to, device_id_type=MESH)

        n = range(self.n)
        mine = [pltpu.make_async_copy(srcs[i], dsts[i].at[4 * x + 2 * y + c], local_sems.at[i]) for i in n]
        first = [copy(i, 0, me, sibling, own=True) for i in n]
        first += [copy(i, 1 + j, me, (*chip, c), own=True) for j, chip in enumerate(chips) for i in n]
        over_ici = [copy(i, 1 + j, (*chip, c), me) for j, chip in enumerate(chips) for i in n]
        passed = [copy(i, 4 + j, (*chip, c), sibling) for j, chip in enumerate(chips) for i in n]
        from_sibling = [copy(i, 0, sibling, me) for i in n]
        from_sibling += [copy(i, 4 + j, (*chip, 1 - c), me) for j, chip in enumerate(chips) for i in n]
        return mine, first, over_ici, passed, from_sibling

    def start(self, *refs):
        mine, first, _, _, _ = self._parts(*refs)
        for cp in mine + first:
            cp.start()

    def mid(self, *refs):
        _, _, over_ici, passed, _ = self._parts(*refs)
        for arrived, onward in zip(over_ici, passed):
            arrived.wait_recv()
            onward.start()

    def finish(self, *refs):
        mine, first, _, passed, from_sibling = self._parts(*refs)
        for cp in from_sibling:
            cp.wait_recv()
        for cp in first + passed:
            cp.wait_send()
        for cp in mine:
            cp.wait()


def _hosted_call(body, comms, *, name, grid, in_specs, out_specs, out_shape, scratch_shapes, args):
    comms = list(comms or [])
    if not comms:
        res = pl.pallas_call(body, name=name, grid=grid, in_specs=in_specs, out_specs=out_specs, out_shape=out_shape,
                             scratch_shapes=scratch_shapes, compiler_params=_cp("arbitrary"))(*args)
        return list(res), []
    n_in, n_out, n_sc, nc = len(in_specs), len(out_specs), len(scratch_shapes), sum(cm.n for cm in comms)
    last = grid[0] - 1

    def hosted(*refs):
        ins, csrc = refs[:n_in], refs[n_in:n_in + nc]
        o0 = n_in + nc
        outs, cdst = refs[o0:o0 + n_out], refs[o0 + n_out:o0 + n_out + nc]
        s0 = o0 + n_out + nc
        scr, sems = refs[s0:s0 + n_sc], refs[s0 + n_sc:]

        def phase(which):
            o = 0
            for j, cm in enumerate(comms):
                getattr(cm, which)(csrc[o:o + cm.n], cdst[o:o + cm.n], *sems[3 * j:3 * j + 3])
                o += cm.n

        i = pl.program_id(0)

        @pl.when(i == 0)
        def _():
            phase("start")

        body(*ins, *outs, *scr)

        @pl.when(i == (2 * last) // 3)
        def _():
            phase("mid")

        @pl.when(i == last)
        def _():
            phase("finish")

    res = pl.pallas_call(
        hosted, name=name, grid=grid, in_specs=list(in_specs) + [ANY] * nc, out_specs=list(out_specs) + [ANY] * nc,
        out_shape=list(out_shape) + [s for cm in comms for s in cm.out_shape()],
        scratch_shapes=list(scratch_shapes) + [s for cm in comms for s in cm.scratch()],
        compiler_params=_cp("arbitrary"))(*args, *[a for cm in comms for a in cm.srcs])
    got, o = [], n_out
    for cm in comms:
        got.append(list(res[o:o + cm.n]))
        o += cm.n
    return list(res[:n_out]), got


def _gather_two_level(blocks, name):
    n = len(blocks)

    def body(*refs):
        x_refs, out_refs = refs[:n], refs[n:2 * n]
        send_sems, recv_sems, local_sems = refs[2 * n:]
        x, y, c = lax.axis_index("x"), lax.axis_index("y"), lax.axis_index("c")
        me, sibling = (x, y, c), (x, y, 1 - c)
        chips = [(1 - x, y), (x, 1 - y), (1 - x, 1 - y)]

        def copy(i, k, block, to, own=False):
            slab = out_refs[i].at[4 * block[0] + 2 * block[1] + block[2]]
            return pltpu.make_async_remote_copy(
                src_ref=x_refs[i] if own else slab, dst_ref=slab, send_sem=send_sems.at[7 * i + k],
                recv_sem=recv_sems.at[7 * i + k], device_id=to, device_id_type=MESH)

        mine = [pltpu.make_async_copy(x_refs[i], out_refs[i].at[4 * x + 2 * y + c], local_sems.at[i]) for i in range(n)]
        for cp in mine:
            cp.start()
        first = [copy(i, 0, me, sibling, own=True) for i in range(n)]
        first += [copy(i, 1 + j, me, (*chip, c), own=True) for j, chip in enumerate(chips) for i in range(n)]
        for cp in first:
            cp.start()
        passed = []
        for j, chip in enumerate(chips):
            for i in range(n):
                copy(i, 1 + j, (*chip, c), me).wait_recv()
                passed.append(copy(i, 4 + j, (*chip, c), sibling))
                passed[-1].start()
        for i in range(n):
            copy(i, 0, sibling, me).wait_recv()
            for j, chip in enumerate(chips):
                copy(i, 4 + j, (*chip, 1 - c), me).wait_recv()
        for cp in first + passed:
            cp.wait_send()
        for cp in mine:
            cp.wait()

    return pl.pallas_call(
        body, name=name, out_shape=[jax.ShapeDtypeStruct((N_DEV,) + b.shape, b.dtype) for b in blocks],
        in_specs=[ANY] * n, out_specs=[ANY] * n,
        scratch_shapes=[pltpu.SemaphoreType.DMA((7 * n,)), pltpu.SemaphoreType.DMA((7 * n,)),
                        pltpu.SemaphoreType.DMA((n,))])(*blocks)


def _in_proj(x, w, b, cw, cb, tm, comms=None):
    T = x.shape[0]

    def body(x_ref, w_ref, b_ref, cw_ref, cb_ref, o_ref, pre_ref, act_ref, halo_sc):
        @pl.when(pl.program_id(0) == 0)
        def _():
            halo_sc[...] = jnp.zeros_like(halo_sc)

        o = _dot(x_ref[...], w_ref[...]) + b_ref[...]
        o_ref[...] = o
        xc = o[:, 128 * C_MQK:128 * C_MQK + HALF]
        halo = halo_sc[...]
        rowi = _iota((8, HALF), 0)
        acc = xc * cw_ref[3:4, :] + cb_ref[...]
        for j in (1, 2, 3):
            acc = acc + _shift_rows(xc, halo, j, rowi) * cw_ref[3 - j:4 - j, :]
        pre_ref[...] = acc
        act_ref[...] = acc * _sig(acc)
        halo_sc[...] = xc[tm - 8:]

    return _hosted_call(
        body, comms, name="in_proj", grid=(T // tm,),
        in_specs=[_row(tm, D_MODEL), _full(w.shape), _full(b.shape), _full(cw.shape), _full(cb.shape)],
        out_specs=[_row(tm, PROJ_WP), _row(tm, HALF), _row(tm, HALF)],
        out_shape=[_sds((T, PROJ_WP)), _sds((T, HALF)), _sds((T, HALF))],
        scratch_shapes=[pltpu.VMEM((8, HALF), F32)], args=(x, w, b, cw, cb))


def _shift_rows(x, halo, j, rowi):
    r = pltpu.roll(x, j, 0)
    top = jnp.where(rowi < j, pltpu.roll(halo, j, 0), r[:8])
    return jnp.concatenate([top, r[8:]], axis=0)


def _shift_rows_up(x, halo, j, rowi):
    n = x.shape[0]
    r = pltpu.roll(x, n - j, 0)
    bot = jnp.where(rowi >= 8 - j, pltpu.roll(halo, 8 - j, 0), r[n - 8:])
    return jnp.concatenate([r[:n - 8], bot], axis=0)


def _bdot(a, b, dims):
    return lax.dot_general(a.astype(_MXU), b.astype(_MXU), dims, preferred_element_type=F32)


def _bdotx(a, b, dims):
    return lax.dot_general(a, b, dims, precision=lax.Precision.HIGHEST, preferred_element_type=F32)


def _heads_to_batch(x, w):
    G = x.shape[0] // CHUNK
    x3 = x.reshape(G, CHUNK, HEADS * w)
    return jnp.stack([x3[:, :, w * h:w * (h + 1)] for h in range(HEADS)], axis=1).reshape(G * HEADS, CHUNK, w)


def _batch_to_heads(x3):
    B, _, w = x3.shape
    x4 = x3.reshape(B // HEADS, HEADS, CHUNK, w)
    return jnp.concatenate([x4[:, h] for h in range(HEADS)], axis=-1).reshape(B // HEADS * CHUNK, HEADS * w)


def _chunk_cumsum(x, rowmod, reverse=False):
    R = x.shape[0]
    for sh in (1, 2, 4, 8, 16, 32):
        if reverse:
            x = x + jnp.where(rowmod < CHUNK - sh, pltpu.roll(x, R - sh, 0), 0.0)
        else:
            x = x + jnp.where(rowmod >= sh, pltpu.roll(x, sh, 0), 0.0)
    return x


def _lane_col(x, c, lane):
    return _rowsum(jnp.where(lane == c, x, 0.0))


def _hg_gates(hq, hf, lb):
    sg = _sig(hf)
    nsg = _sig(-hf)
    f = lb + (1.0 - lb) * sg
    g = jnp.log(f)
    k = (1.0 - lb) * nsg
    sq = _sig(hq)
    return hq * sq, g, k, f, sg, nsg, sq


def _hg_prep(hq_ref, hf_ref, lg_ref, b_sc, k_sc):
    R = hq_ref.shape[0]
    G = R // CHUNK
    lb = _sig(lg_ref[0:1, :] - lg_ref[1:2, :])
    hq = hq_ref[...]
    q, g, k, f, sg, nsg, sq = _hg_gates(hq, hf_ref[...], lb)
    rowmod = _iota((R, HALF), 0) & (CHUNK - 1)
    b = _chunk_cumsum(g, rowmod) * LOG2E
    last8 = _iota((8, HALF), 0) == 7
    bl_rows = [_colsum(jnp.where(last8, b[CHUNK * c + CHUNK - 8:CHUNK * (c + 1)], 0.0)) for c in range(G)]
    bl3 = jnp.stack([r[:, 128 * h:128 * (h + 1)] for r in bl_rows for h in range(HEADS)], axis=0)
    b3, k3 = _heads_to_batch(b, 128), _heads_to_batch(k, 128)
    b_sc[...] = b3
    k_sc[...] = k3
    return dict(G=G, lb=lb, hq=hq, f=f, sg=sg, nsg=nsg, sq=sq, rowmod=rowmod, q3=_heads_to_batch(q, 128), k3=k3, b3=b3,
                bl3=bl3)


HSUB = 8


def _lo(j):
    return HSUB * (j // HSUB)


def _hg_diag_tiles(b_sc, b3, r0, rowi):
    bi = b3[:, r0:r0 + SUB]
    return [jnp.exp2(jnp.where(rowi[:, _lo(s):] >= s, bi[:, _lo(s):] - b_sc[:, r0 + s:r0 + s + 1, :], NEG))
            for s in range(SUB)]


def _hg_diag_tiles_t(b_sc, b3, r0, rowi):
    bi = b3[:, r0:r0 + SUB]
    return [jnp.exp2(jnp.where(rowi[:, :_lo(t) + HSUB] <= t, b_sc[:, r0 + t:r0 + t + 1, :] - bi[:, :_lo(t) + HSUB], NEG))
            for t in range(SUB)]


def _lane_sums(pieces, ones):
    B = pieces[0].shape[0]
    hs = [p.shape[1] for p in pieces]
    R = _dot(jnp.concatenate(pieces, axis=1).reshape(B * sum(hs), 128), ones).reshape(B, sum(hs), 128)
    out, o = [], 0
    for h in hs:
        out.append(R[:, o:o + h])
        o += h
    return out


def _sum_tri(terms, low_rows):
    if SUB == HSUB:
        return sum(terms)
    full = sum(t for t in terms if t.shape[1] == SUB)
    half = sum(t for t in terms if t.shape[1] == HSUB)
    lo, hi = full[:, :HSUB], full[:, HSUB:]
    return jnp.concatenate([lo + half, hi] if low_rows else [lo, hi + half], axis=1)


def _hgrn2_fwd(u, lb_logits, comms=None):
    T = u.shape[0]
    G = min(GC, T // CHUNK)
    R, B, N = G * CHUNK, G * HEADS, T // CHUNK

    def body(hq_ref, hf_ref, hv_ref, lg_ref, o_ref, st_ref, S_ref, b_sc, k_sc, v_sc):
        @pl.when(pl.program_id(0) == 0)
        def _():
            S_ref[...] = jnp.zeros_like(S_ref)

        pz = _hg_prep(hq_ref, hf_ref, lg_ref, b_sc, k_sc)
        q3, k3, b3, bl3 = pz["q3"], pz["k3"], pz["b3"], pz["bl3"]
        v3 = _heads_to_batch(hv_ref[...], 128)
        v_sc[...] = v3
        stloc = _bdot(v3, k3 * jnp.exp2(bl3 - b3), BTN).reshape(G, HEADS, 128, 128)
        dec = jnp.exp2(bl3).reshape(G, HEADS, 1, 128)
        ST = S_ref[...]
        sts = []
        for c in range(G):
            sts.append(ST)
            ST = ST * dec[c] + stloc[c]
        S_ref[...] = ST
        st4 = jnp.stack(sts, axis=0)
        st_ref[...] = st4
        o = _bdot(q3 * jnp.exp2(b3), st4.reshape(B, 128, 128), BNT)
        ones = jnp.ones((128, 128), F32)
        rowi = _iota((1, SUB, 128), 1)
        outs = []
        for i in range(CHUNK // SUB):
            r0 = SUB * i
            qi = q3[:, r0:r0 + SUB]
            oi = o[:, r0:r0 + SUB]
            if i > 0:
                r = b_sc[:, r0 - 1:r0, :]
                qe = qi * jnp.exp2(b3[:, r0:r0 + SUB] - r)
                ke = k3[:, :r0] * jnp.exp2(r - b3[:, :r0])
                oi = oi + _bdot(_bdot(qe, ke, BNT), v3[:, :r0], BNN)
            tiles = _hg_diag_tiles(b_sc, b3, r0, rowi)
            a_b = _lane_sums([qi[:, _lo(s):] * (k_sc[:, r0 + s:r0 + s + 1, :] * tiles[s]) for s in range(SUB)], ones)
            outs.append(oi + _sum_tri([a_b[s] * v_sc[:, r0 + s:r0 + s + 1, :] for s in range(SUB)], False))
        o_ref[...] = _batch_to_heads(jnp.concatenate(outs, axis=1))

    blk = lambda c: pl.BlockSpec((R, HALF), lambda n, c=c: (n, c // 4))
    return _hosted_call(
        body, comms, name="hgrn2_fwd", grid=(N // G,),
        in_specs=[blk(C_HQ), blk(C_HF), blk(C_HV), _full(lb_logits.shape)],
        out_specs=[pl.BlockSpec((R, HALF), lambda n: (n, 0)),
                   pl.BlockSpec((G, HEADS, 128, 128), lambda n: (n, 0, 0, 0))],
        out_shape=[_sds((T, HALF)), _sds((N, HEADS, 128, 128))],
        scratch_shapes=[pltpu.VMEM((HEADS, 128, 128), F32)] + [pltpu.VMEM((B, CHUNK, 128), F32)] * 3,
        args=(u, u, u, lb_logits))


def _hgrn2_bwd(u, lb_logits, do, states, comms=None):
    T = u.shape[0]
    G = min(GC, T // CHUNK)
    R, B, NG = G * CHUNK, G * HEADS, T // (G * CHUNK)

    def body(hq_ref, hf_ref, hv_ref, lg_ref, do_ref, st_ref, dhq_ref, dhf_ref, dhv_ref, dlb_ref,
             dS_ref, b_sc, k_sc, v_sc, q_sc, do_sc):
        @pl.when(pl.program_id(0) == 0)
        def _():
            dS_ref[...] = jnp.zeros_like(dS_ref)
            dlb_ref[...] = jnp.zeros_like(dlb_ref)

        pz = _hg_prep(hq_ref, hf_ref, lg_ref, b_sc, k_sc)
        q3, k3, b3, bl3, lb = pz["q3"], pz["k3"], pz["b3"], pz["bl3"], pz["lb"]
        v3 = _heads_to_batch(hv_ref[...], 128)
        v_sc[...] = v3
        do3 = _heads_to_batch(do_ref[...], 128)
        q_sc[...] = q3
        do_sc[...] = do3
        st3 = st_ref[...].reshape(B, 128, 128)
        eb = jnp.exp2(b3)
        ebl = jnp.exp2(bl3 - b3)
        qt = q3 * eb
        kl = k3 * ebl
        dstloc = _bdot(do3, qt, BTN).reshape(G, HEADS, 128, 128)
        dec = jnp.exp2(bl3).reshape(G, HEADS, 1, 128)
        dST = dS_ref[...]
        dsts = [None] * G
        for c in reversed(range(G)):
            dsts[c] = dST
            dST = dST * dec[c] + dstloc[c]
        dS_ref[...] = dST
        dst3 = jnp.stack(dsts, axis=0).reshape(B, 128, 128)
        dqt = _bdot(do3, st3, BNN)
        dkl = _bdot(v3, dst3, BNN)
        dv_acc = _bdot(kl, dst3, BNT)
        ones = jnp.ones((128, 128), F32)
        rowi = _iota((1, SUB, 128), 1)
        dq_parts, dk_parts, dv_parts = [], [], []
        dk_in = jnp.zeros((B, CHUNK, 128), F32)
        for i_s in range(CHUNK // SUB):
            r0 = SUB * i_s
            qi = q3[:, r0:r0 + SUB]
            doi = do3[:, r0:r0 + SUB]
            dqi = jnp.zeros((B, SUB, 128), F32)
            if i_s > 0:
                r = b_sc[:, r0 - 1:r0, :]
                eq = jnp.exp2(b3[:, r0:r0 + SUB] - r)
                ek = jnp.exp2(r - b3[:, :r0])
                qe = qi * eq
                ke = k3[:, :r0] * ek
                a_off = _bdot(qe, ke, BNT)
                p_off = _bdot(doi, v3[:, :r0], BNT)
                pad = jnp.zeros((B, CHUNK - r0, 128), F32)
                dv_acc = dv_acc + jnp.concatenate([_bdot(a_off, doi, BTN), pad], axis=1)
                dqi = dqi + _bdot(p_off, ke, BNN) * eq
                dk_in = dk_in + jnp.concatenate([_bdot(p_off, qe, BTN) * ek, pad], axis=1)
            ki, vi = k3[:, r0:r0 + SUB], v3[:, r0:r0 + SUB]
            rng = range(SUB)
            tiles = _hg_diag_tiles(b_sc, b3, r0, rowi)
            tiles_t = _hg_diag_tiles_t(b_sc, b3, r0, rowi)
            do_rows = [do_sc[:, r0 + t:r0 + t + 1, :] for t in rng]
            kts = [k_sc[:, r0 + s:r0 + s + 1, :] * tiles[s] for s in rng]
            qts = [q_sc[:, r0 + t:r0 + t + 1, :] * tiles_t[t] for t in rng]
            ps = [doi[:, _lo(s):] * v_sc[:, r0 + s:r0 + s + 1, :] for s in rng]
            mst = [ki[:, :_lo(t) + HSUB] * qts[t] for t in rng]
            pst = [vi[:, :_lo(t) + HSUB] * do_rows[t] for t in rng]
            sums = _lane_sums(ps + mst + pst, ones)
            p_b, a_t, p_t = sums[:SUB], sums[SUB:2 * SUB], sums[2 * SUB:]
            dq_parts.append(dqi + _sum_tri([p_b[s] * kts[s] for s in rng], False))
            dv_parts.append(_sum_tri([a_t[t] * do_rows[t] for t in rng], True))
            dk_parts.append(_sum_tri([p_t[t] * qts[t] for t in rng], True))
        dq_in = jnp.concatenate(dq_parts, axis=1)
        dk_in = dk_in + jnp.concatenate(dk_parts, axis=1)
        dv_acc = dv_acc + jnp.concatenate(dv_parts, axis=1)
        db = qt * dqt + q3 * dq_in - k3 * dk_in - kl * dkl
        last = jnp.sum(kl * dkl, axis=1, keepdims=True) + jnp.exp2(bl3) * jnp.sum(st3 * dst3, axis=1, keepdims=True)
        db = db + jnp.where(_iota((1, CHUNK, 1), 1) == CHUNK - 1, last, 0.0)
        dg = _chunk_cumsum(_batch_to_heads(db), pz["rowmod"], reverse=True)
        dq_tot = _batch_to_heads(dqt * eb + dq_in)
        dk_tot = _batch_to_heads(dkl * ebl + dk_in)
        common = dg / pz["f"] - dk_tot
        dhf_ref[...] = ((1.0 - lb) * pz["sg"] * pz["nsg"] * common).astype(dhf_ref.dtype)
        dl0 = _colsum(pz["nsg"] * common) * lb * (1.0 - lb)
        dlb_ref[0:1, :] += dl0
        dlb_ref[1:2, :] -= dl0
        dhq_ref[...] = (dq_tot * _dsilu(pz["hq"], pz["sq"])).astype(dhq_ref.dtype)
        dhv_ref[...] = _batch_to_heads(dv_acc).astype(dhv_ref.dtype)

    rev = lambda c: pl.BlockSpec((R, HALF), lambda i, c=c: (NG - 1 - i, c // 4))
    rev0 = pl.BlockSpec((R, HALF), lambda i: (NG - 1 - i, 0))
    return _hosted_call(
        body, comms, name="hgrn2_bwd", grid=(NG,),
        in_specs=[rev(C_HQ), rev(C_HF), rev(C_HV), _full(lb_logits.shape), rev0,
                  pl.BlockSpec((G, HEADS, 128, 128), lambda i: (NG - 1 - i, 0, 0, 0))],
        out_specs=[rev0, rev0, rev0, _full((2, HALF))],
        out_shape=[_sds((T, HALF), _MXU)] * 3 + [_sds((2, HALF))],
        scratch_shapes=[pltpu.VMEM((HEADS, 128, 128), F32)] + [pltpu.VMEM((B, CHUNK, 128), F32)] * 5,
        args=(u, u, u, lb_logits, do, states))


def _lanes_to_batch_cols(x, lane):
    G = x.shape[0] // CHUNK
    cols = [_lane_col(x, 4 + h, lane).reshape(G, CHUNK, 1) for h in range(HEADS)]
    return jnp.stack(cols, axis=1).reshape(G * HEADS, CHUNK, 1)


def _row_scalars(rows):
    lane = _iota((1, 128), 1)
    return jnp.stack([_rowsum(jnp.where(lane == 4 + h, r, 0.0)) for r in rows for h in range(HEADS)], axis=0)


def _ml_gates(gates):
    R = gates.shape[0]
    lane = _iota((R, 128), 1)
    rowmod = _iota((R, 128), 0) & (CHUNK - 1)
    lf = jnp.minimum(gates, 0.0) - jnp.log(1.0 + jnp.exp(-jnp.abs(gates)))
    g_all = _chunk_cumsum(lf, rowmod)
    x_all = pltpu.roll(gates, 4, 1) - g_all
    return g_all, x_all, lane, rowmod


def _ml_chunk_rows(g_all, x_all, mprev, g):
    gl = g_all[CHUNK * g + CHUNK - 8:CHUNK * (g + 1)]
    gl = _colsum(jnp.where(_iota((8, 128), 0) == 7, gl, 0.0))
    a = gl + x_all[CHUNK * g:CHUNK * (g + 1)]
    m_new = jnp.maximum(gl + mprev, jnp.max(a, axis=0, keepdims=True))
    return m_new, jnp.exp(gl + mprev - m_new), jnp.exp(a - m_new)


def _ml_batched(q3, k3, v3, g_all, x_all, lane, C3, n3, mprev3):
    G = g_all.shape[0] // CHUNK
    gcol3 = _lanes_to_batch_cols(g_all, lane)
    onehot = jnp.where(_iota((G, 8, 128), 1) + 4 == _iota((G, 8, 128), 2), 1.0, 0.0).astype(F32)
    rows = _bdotx(onehot, x_all.reshape(G, CHUNK, 128), BNT)
    sub = _iota((G, 8, CHUNK), 1)
    row3 = jnp.stack([jnp.sum(jnp.where(sub == h, rows, 0.0), axis=1, keepdims=True) for h in range(HEADS)],
                     axis=1).reshape(G * HEADS, 1, CHUNK)
    causal = _iota((1, CHUNK, CHUNK), 1) >= _iota((1, CHUNK, CHUNK), 2)
    dmat = jnp.where(causal, gcol3 + row3, NEG)
    m_inter = gcol3 + mprev3
    m_t = jnp.maximum(m_inter, jnp.max(dmat, axis=2, keepdims=True))
    wi = jnp.exp(dmat - m_t)
    wn = jnp.exp(m_inter - m_t)
    s3 = _bdot(q3, k3, BNT) * wi
    qc = _bdot(q3, C3, BNN)
    qn = jnp.sum(q3 * n3, axis=2, keepdims=True)
    num = _bdot(s3, v3, BNN) + wn * qc
    den = jnp.sum(s3, axis=2, keepdims=True) + wn * qn
    floor = jnp.exp(-m_t)
    return dict(wi=wi, wn=wn, s=s3, qc=qc, qn=qn, num=num, den=den, floor=floor, nrm=jnp.maximum(jnp.abs(den), floor))


def _mlstm_fwd(qkc, u, comms=None):
    T = u.shape[0]
    G = min(GC, T // CHUNK)
    R = G * CHUNK
    N = T // CHUNK

    def body(qk_ref, v_ref, g_ref, h_ref, cst_ref, nst_ref, mst_ref, C_ref, n_ref, m_ref):
        @pl.when(pl.program_id(0) == 0)
        def _():
            C_ref[...] = jnp.zeros_like(C_ref)
            n_ref[...] = jnp.zeros_like(n_ref)
            m_ref[...] = jnp.zeros_like(m_ref)

        g_all, x_all, lane, _ = _ml_gates(g_ref[...])
        m_row = m_ref[...]
        mprev_rows, wo_rows, ws_parts = [], [], []
        for g in range(G):
            mprev_rows.append(m_row)
            m_row, wo, ws = _ml_chunk_rows(g_all, x_all, m_row, g)
            wo_rows.append(wo)
            ws_parts.append(ws)
        m_ref[...] = m_row
        mst_ref[...] = jnp.stack(mprev_rows, axis=0)
        ws3 = _lanes_to_batch_cols(jnp.concatenate(ws_parts, axis=0), lane)
        wo4 = _row_scalars(wo_rows).reshape(G, HEADS, 1, 1)
        q3 = _heads_to_batch(qk_ref[:, :256] * ML_SCALE, ML_DQK)
        k3 = _heads_to_batch(qk_ref[:, 256:], ML_DQK)
        v3 = _heads_to_batch(v_ref[...], 128)
        kw = k3 * ws3
        cloc = _bdot(kw, v3, BTN).reshape(G, HEADS, ML_DQK, 128)
        nloc = jnp.sum(kw, axis=1, keepdims=True).reshape(G, HEADS, 1, ML_DQK)
        C, nn = C_ref[...], n_ref[...]
        cs, ns = [], []
        for g in range(G):
            cs.append(C)
            ns.append(nn)
            C = wo4[g] * C + cloc[g]
            nn = wo4[g] * nn + nloc[g]
        C_ref[...] = C
        n_ref[...] = nn
        c4, n4 = jnp.stack(cs, axis=0), jnp.stack(ns, axis=0)
        cst_ref[...] = c4
        nst_ref[...] = n4
        r = _ml_batched(q3, k3, v3, g_all, x_all, lane, c4.reshape(G * HEADS, ML_DQK, 128),
                        n4.reshape(G * HEADS, 1, ML_DQK), _row_scalars(mprev_rows))
        h_ref[...] = _batch_to_heads(r["num"] / r["nrm"])

    return _hosted_call(
        body, comms, name="mlstm_fwd", grid=(N // G,),
        in_specs=[pl.BlockSpec((R, HALF), lambda n: (n, 0)), pl.BlockSpec((R, HALF), lambda n: (n, C_MV // 4)),
                  pl.BlockSpec((R, 128), lambda n: (n, C_GATES))],
        out_specs=[pl.BlockSpec((R, HALF), lambda n: (n, 0)),
                   pl.BlockSpec((G, HEADS, ML_DQK, 128), lambda n: (n, 0, 0, 0)),
                   pl.BlockSpec((G, HEADS, 1, ML_DQK), lambda n: (n, 0, 0, 0)),
                   pl.BlockSpec((G, 1, 128), lambda n: (n, 0, 0))],
        out_shape=[_sds((T, HALF)), _sds((N, HEADS, ML_DQK, 128)), _sds((N, HEADS, 1, ML_DQK)), _sds((N, 1, 128))],
        scratch_shapes=[pltpu.VMEM((HEADS, ML_DQK, 128), F32), pltpu.VMEM((HEADS, 1, ML_DQK), F32),
                        pltpu.VMEM((1, 128), F32)],
        args=(qkc, u, u))


def _mlstm_bwd(qkc, u, dh, cst, nst, mst, pre, cw):
    T = u.shape[0]
    G = min(GC, T // CHUNK)
    R = G * CHUNK
    NG = T // R

    def body(qk_ref, v_ref, g_ref, dh_ref, cst_ref, nst_ref, mst_ref, pre_ref, x_ref, xh_ref, cw_ref,
             dmqk_ref, dv_ref, dgt_ref, dcw_ref, dcb_ref, dC_ref, dn_ref, next_sc):
        @pl.when(pl.program_id(0) == 0)
        def _():
            for r in (dC_ref, dn_ref, next_sc, dcw_ref, dcb_ref):
                r[...] = jnp.zeros_like(r)

        B = G * HEADS
        gates = g_ref[...]
        g_all, x_all, lane, rowmod = _ml_gates(gates)
        mprev_rows = [mst_ref[g] for g in range(G)]
        wo_rows, ws_parts = [], []
        for g in range(G):
            _, wo, ws = _ml_chunk_rows(g_all, x_all, mprev_rows[g], g)
            wo_rows.append(wo)
            ws_parts.append(ws)
        ws3 = _lanes_to_batch_cols(jnp.concatenate(ws_parts, axis=0), lane)
        wo3 = _row_scalars(wo_rows)
        wo4 = wo3.reshape(G, HEADS, 1, 1)
        q3 = _heads_to_batch(qk_ref[:, :256] * ML_SCALE, ML_DQK)
        k3 = _heads_to_batch(qk_ref[:, 256:], ML_DQK)
        v3 = _heads_to_batch(v_ref[...], 128)
        dh3 = _heads_to_batch(dh_ref[...], 128)
        C3 = cst_ref[...].reshape(B, ML_DQK, 128)
        n3 = nst_ref[...].reshape(B, 1, ML_DQK)
        r = _ml_batched(q3, k3, v3, g_all, x_all, lane, C3, n3, _row_scalars(mprev_rows))
        wn, s3 = r["wn"], r["s"]
        inv = 1.0 / r["nrm"]
        dnum = dh3 * inv
        dnrm = -jnp.sum(dh3 * (r["num"] * inv), axis=2, keepdims=True) * inv
        dden = jnp.where(jnp.abs(r["den"]) > r["floor"], dnrm * jnp.sign(r["den"]), 0.0)
        ds = _bdot(dnum, v3, BNT) + dden
        dqk = ds * r["wi"]
        dd = ds * s3
        qw = q3 * wn
        dcloc = _bdot(qw, dnum, BTN).reshape(G, HEADS, ML_DQK, 128)
        dnloc = jnp.sum(qw * dden, axis=1, keepdims=True).reshape(G, HEADS, 1, ML_DQK)
        dC, dn = dC_ref[...], dn_ref[...]
        dcs, dns = [None] * G, [None] * G
        for g in reversed(range(G)):
            dcs[g], dns[g] = dC, dn
            dC = wo4[g] * dC + dcloc[g]
            dn = wo4[g] * dn + dnloc[g]
        dC_ref[...] = dC
        dn_ref[...] = dn
        dC3 = jnp.stack(dcs, axis=0).reshape(B, ML_DQK, 128)
        dn3 = jnp.stack(dns, axis=0).reshape(B, 1, ML_DQK)
        dk_st = ws3 * (_bdot(v3, dC3, BNT) + dn3)
        dq = _bdot(dqk, k3, BNN) + wn * (_bdot(dnum, C3, BNT) + dden * n3)
        dk = _bdot(dqk, q3, BTN) + dk_st
        dv = _bdot(s3, dnum, BTN) + ws3 * _bdot(k3, dC3, BNN)
        dv_ref[...] = _batch_to_heads(dv).astype(dv_ref.dtype)
        dqk = jnp.concatenate([_batch_to_heads(dq * ML_SCALE), _batch_to_heads(dk)], axis=1)
        pre = pre_ref[...]
        dpre = dqk * _dsilu(pre, _sig(pre))
        x = x_ref[...]
        xprev = jnp.where(pl.program_id(0) < NG - 1, xh_ref[...], 0.0)
        nxt = next_sc[...]
        row8 = _iota((8, HALF), 0)
        dx = dpre * cw_ref[3:4, :]
        dws = [None, None, None, _colsum(dpre * x)]
        for j in (1, 2, 3):
            dx = dx + _shift_rows_up(dpre, nxt, j, row8) * cw_ref[3 - j:4 - j, :]
            dws[3 - j] = _colsum(dpre * _shift_rows(x, xprev, j, row8))
        dmqk_ref[...] = dx.astype(dmqk_ref.dtype)
        dcw_ref[...] += jnp.concatenate(dws, axis=0)
        dcb_ref[...] += _colsum(dpre)
        next_sc[...] = dpre[:8]
        e_col = wn * (jnp.sum(dnum * r["qc"], axis=2, keepdims=True) + dden * r["qn"])
        c_col = jnp.sum(k3 * dk_st, axis=2, keepdims=True)
        z = wo3 * (jnp.sum(dC3 * C3, axis=(1, 2), keepdims=True) + jnp.sum(dn3 * n3, axis=(1, 2), keepdims=True))
        dd_hi = dd.astype(_MXU).astype(F32)
        ones = jnp.ones((B, CHUNK, 128), F32)
        dd_cols = (_bdot(dd_hi, ones, BTN) + _bdot(dd - dd_hi, ones, BTN))[:, :, 0:1]
        last = _iota((1, CHUNK, 1), 1) == CHUNK - 1
        dg3 = jnp.sum(dd, axis=2, keepdims=True) - dd_cols + e_col - c_col
        dg3 = dg3 + jnp.where(last, jnp.sum(c_col, axis=1, keepdims=True) + z, 0.0)
        di3 = dd_cols + c_col

        def to_lanes(x3, first):
            x4 = x3.reshape(G, HEADS, CHUNK, 1)
            return sum(jnp.where(lane == first + h, x4[:, h].reshape(R, 1), 0.0) for h in range(HEADS))

        dlf = _chunk_cumsum(to_lanes(dg3, 4), rowmod, reverse=True)
        dgt_ref[...] = (to_lanes(di3, 0) + dlf * _sig(-gates)).astype(dgt_ref.dtype)

    rev = lambda w, c: pl.BlockSpec((R, w), lambda i, c=c: (NG - 1 - i, c))
    st = lambda *s: pl.BlockSpec((G,) + s, lambda i: (NG - 1 - i,) + (0,) * len(s))
    halo = pl.BlockSpec((8, HALF), lambda i: (jnp.maximum((NG - 1 - i) * (R // 8) - 1, 0), C_MQK // 4))
    return pl.pallas_call(
        body, name="mlstm_bwd", grid=(NG,),
        in_specs=[rev(HALF, 0), rev(HALF, C_MV // 4), rev(128, C_GATES), rev(HALF, 0),
                  st(HEADS, ML_DQK, 128), st(HEADS, 1, ML_DQK), st(1, 128),
                  rev(HALF, 0), rev(HALF, C_MQK // 4), halo, _full(cw.shape)],
        out_specs=[rev(HALF, 0), rev(HALF, 0), rev(128, 0), _full((4, HALF)), _full((1, HALF))],
        out_shape=[_sds((T, HALF), _MXU), _sds((T, HALF), _MXU), _sds((T, 128), _MXU), _sds((4, HALF)), _sds((1, HALF))],
        scratch_shapes=[pltpu.VMEM((HEADS, ML_DQK, 128), F32), pltpu.VMEM((HEADS, 1, ML_DQK), F32),
                        pltpu.VMEM((8, HALF), F32)],
        compiler_params=_cp("arbitrary"))(qkc, u, u, dh, cst, nst, mst, pre, u, u, cw)


def _head_norm(o):
    rs_parts, r_parts = [], []
    for h in range(HEADS):
        oh = o[:, 128 * h:128 * (h + 1)]
        rs = lax.rsqrt(jnp.mean(oh * oh, axis=-1, keepdims=True) + RMS_EPS)
        rs_parts.append(rs)
        r_parts.append(oh * rs)
    return jnp.concatenate(r_parts, axis=1), rs_parts


def _out_proj(x, u, o_hg, h_ml, g_hg, g_ml, w_out, tm):
    T = x.shape[0]

    def body(x_ref, hgate_ref, mo_ref, ohg_ref, hml_ref, ghg_ref, gml_ref, w_ref, m_ref, z_ref):
        hgate = hgate_ref[...]
        a = _head_norm(ohg_ref[...])[0] * ghg_ref[...] * (hgate * _sig(hgate))
        b = _head_norm(hml_ref[...])[0] * gml_ref[...] * _sig(mo_ref[...])
        m = jnp.concatenate([a, b], axis=1)
        m_ref[...] = m.astype(m_ref.dtype)
        z_ref[...] = ALPHA * x_ref[...] + _dot(m, w_ref[...])

    return pl.pallas_call(
        body, name="out_proj", grid=(T // tm,),
        in_specs=[_row(tm, D_MODEL), _row(tm, HALF, C_HGATE // 4), _row(tm, HALF, C_MO // 4),
                  _row(tm, HALF), _row(tm, HALF), _full(g_hg.shape), _full(g_ml.shape), _full(w_out.shape)],
        out_specs=[_row(tm, D_MODEL)] * 2,
        out_shape=[_sds((T, D_MODEL), _MXU), _sds((T, D_MODEL))],
        compiler_params=_cp("parallel"))(x, u, u, o_hg, h_ml, g_hg, g_ml, w_out)


def _ffn_ln(z1, ln1_g, ln1_b, wg, wu, wd, ln_g, ln_b, tm):
    T = z1.shape[0]

    def body(z1_ref, g1_ref, b1_ref, wg_ref, wu_ref, wd_ref, g_ref, b_ref, z_ref, x1_ref, x2_ref, a_ref, bb_ref, h_ref):
        x = _ln_fwd(z1_ref[...], g1_ref[...], b1_ref[...])[0]
        x1_ref[...] = x.astype(x1_ref.dtype)
        a = _dot(x, wg_ref[...], NT)
        bb = _dot(x, wu_ref[...], NT)
        hh = a * _sig(a) * bb
        a_ref[...] = a.astype(a_ref.dtype)
        bb_ref[...] = bb.astype(bb_ref.dtype)
        h_ref[...] = hh.astype(h_ref.dtype)
        z = ALPHA * x + _dot(hh, wd_ref[...])
        z_ref[...] = z
        x2_ref[...] = _ln_fwd(z, g_ref[...], b_ref[...])[0].astype(x2_ref.dtype)

    vec = _full((1, D_MODEL))
    return pl.pallas_call(
        body, name="ffn_ln2", grid=(T // tm,),
        in_specs=[_row(tm, D_MODEL), vec, vec, _full(wg.shape), _full(wu.shape), _full(wd.shape), vec, vec],
        out_specs=[_row(tm, D_MODEL)] * 3 + [_row(tm, D_FF)] * 3,
        out_shape=[_sds((T, D_MODEL))] + [_sds((T, D_MODEL), _MXU)] * 2 + [_sds((T, D_FF), _MXU)] * 3,
        compiler_params=_cp("parallel"))(z1, ln1_g, ln1_b, wg, wu, wd, ln_g, ln_b)


def _ple_loss_ln2_bwd(z2, p, tgt, wpg, bpg, wpp, ln_g, ln_b, tm):
    T = z2.shape[0]

    def body(z_ref, p_ref, t_ref, wpg_ref, bpg_ref, wpp_ref, g_ref, b_ref,
             de_ref, dgp_ref, dz_ref, loss_ref, dbpg_ref, dg_ref, db_ref):
        @pl.when(pl.program_id(0) == 0)
        def _():
            for r in (loss_ref, dbpg_ref, dg_ref, db_ref):
                r[...] = jnp.zeros_like(r)

        x2, xhat, rstd = _ln_fwd(z_ref[...], g_ref[...], b_ref[...])
        gate = _sig(_dot(x2, wpg_ref[...]) + bpg_ref[...])
        e = _dot(p_ref[...], wpp_ref[...])
        err = x2 + gate * e - t_ref[...]
        loss_ref[...] += _colsum(err * err)
        dy = err * (1.0 / D_MODEL)
        de_ref[...] = (dy * gate).astype(de_ref.dtype)
        dgp = dy * e * gate * (1.0 - gate)
        dgp_ref[...] = dgp.astype(dgp_ref.dtype)
        dbpg_ref[...] += _colsum(dgp)
        dx2 = dy + _dot(dgp, wpg_ref[...], NT)
        dg_ref[...] += _colsum(dx2 * xhat)
        db_ref[...] += _colsum(dx2)
        dz_ref[...] = _ln_bwd(dx2, xhat, rstd, g_ref[...])

    vec = _full((1, D_MODEL))
    return pl.pallas_call(
        body, name="ple_loss_ln2_bwd", grid=(T // tm,),
        in_specs=[_row(tm, D_MODEL), _row(tm, PLE_DIM), _row(tm, D_MODEL),
                  _full(wpg.shape), vec, _full(wpp.shape), vec, vec],
        out_specs=[_row(tm, D_MODEL)] * 3 + [vec] * 4,
        out_shape=[_sds((T, D_MODEL), _MXU)] * 2 + [_sds((T, D_MODEL))] + [_sds((1, D_MODEL))] * 4,
        compiler_params=_cp("arbitrary"))(z2, p, tgt, wpg, bpg, wpp, ln_g, ln_b)


def _ffn_bwd_ln1_bwd(a_pre, b_pre, z1, dz2, wg, wu, wd, ln_g, ln_b, tm, comms=None):
    T = z1.shape[0]

    def body(a_ref, bb_ref, z_ref, dz2_ref, wg_ref, wu_ref, wd_ref, g_ref, b_ref,
             da_ref, dbb_ref, dz1_ref, dg_ref, db_ref):
        @pl.when(pl.program_id(0) == 0)
        def _():
            dg_ref[...] = jnp.zeros_like(dg_ref)
            db_ref[...] = jnp.zeros_like(db_ref)

        dz2 = dz2_ref[...]
        a = a_ref[...].astype(F32)
        bb = bb_ref[...].astype(F32)
        sa = _sig(a)
        act = a * sa
        dh = _dot(dz2, wd_ref[...], NT)
        da = (dh * bb * _dsilu(a, sa)).astype(da_ref.dtype)
        dbb = (dh * act).astype(dbb_ref.dtype)
        da_ref[...] = da
        dbb_ref[...] = dbb
        dx1 = ALPHA * dz2 + _dot(da, wg_ref[...]) + _dot(dbb, wu_ref[...])
        _, xhat, rstd = _ln_fwd(z_ref[...], g_ref[...], b_ref[...])
        dg_ref[...] += _colsum(dx1 * xhat)
        db_ref[...] += _colsum(dx1)
        dz1_ref[...] = _ln_bwd(dx1, xhat, rstd, g_ref[...])

    vec = _full((1, D_MODEL))
    return _hosted_call(
        body, comms, name="ffn_bwd_ln1_bwd", grid=(T // tm,),
        in_specs=[_row(tm, D_FF)] * 2 + [_row(tm, D_MODEL)] * 2 + [_full(wg.shape), _full(wu.shape), _full(wd.shape), vec, vec],
        out_specs=[_row(tm, D_FF)] * 2 + [_row(tm, D_MODEL), vec, vec],
        out_shape=[_sds((T, D_FF), _MXU)] * 2 + [_sds((T, D_MODEL)), _sds((1, D_MODEL)), _sds((1, D_MODEL))],
        scratch_shapes=[], args=(a_pre, b_pre, z1, dz2, wg, wu, wd, ln_g, ln_b))


def _out_proj_bwd(dz1, u, o_hg, h_ml, g_hg, g_ml, w_out, tm):
    T = dz1.shape[0]

    def body(dz_ref, hgate_ref, mo_ref, ohg_ref, hml_ref, ghg_ref, gml_ref, w_ref,
             dohg_ref, dhml_ref, dhgate_ref, dmo_ref, dghg_ref, dgml_ref):
        @pl.when(pl.program_id(0) == 0)
        def _():
            dghg_ref[...] = jnp.zeros_like(dghg_ref)
            dgml_ref[...] = jnp.zeros_like(dgml_ref)

        dm = _dot(dz_ref[...], w_ref[...], NT)

        def half(dmh, o, gvec, gate_val, dgate_fac, do_ref, dgate_ref, dgvec_ref):
            r, rs = _head_norm(o)
            dgate_ref[...] = (dmh * r * gvec * dgate_fac).astype(dgate_ref.dtype)
            dn = dmh * gate_val
            dgvec_ref[...] += _colsum(dn * r)
            dr = dn * gvec
            parts = []
            for h in range(HEADS):
                sl = slice(128 * h, 128 * (h + 1))
                parts.append(rs[h] * (dr[:, sl] - r[:, sl] * jnp.mean(dr[:, sl] * r[:, sl], axis=-1, keepdims=True)))
            do_ref[...] = jnp.concatenate(parts, axis=1)

        hg = hgate_ref[...]
        shg = _sig(hg)
        half(dm[:, :HALF], ohg_ref[...], ghg_ref[...], hg * shg, _dsilu(hg, shg), dohg_ref, dhgate_ref, dghg_ref)
        smo = _sig(mo_ref[...])
        half(dm[:, HALF:], hml_ref[...], gml_ref[...], smo, smo * (1.0 - smo), dhml_ref, dmo_ref, dgml_ref)

    vec = _full((1, HALF))
    return pl.pallas_call(
        body, name="out_proj_bwd", grid=(T // tm,),
        in_specs=[_row(tm, D_MODEL), _row(tm, HALF, C_HGATE // 4), _row(tm, HALF, C_MO // 4),
                  _row(tm, HALF), _row(tm, HALF), vec, vec, _full(w_out.shape)],
        out_specs=[_row(tm, HALF)] * 4 + [vec, vec],
        out_shape=[_sds((T, HALF))] * 2 + [_sds((T, HALF), _MXU)] * 2 + [_sds((1, HALF))] * 2,
        compiler_params=_cp("arbitrary"))(dz1, u, u, o_hg, h_ml, g_hg, g_ml, w_out)


def _du_specs(rows):
    return [pl.BlockSpec((rows, w), lambda i: (i, 0)) for w in DU_WIDTHS]


def _in_proj_bwd(dz1, du_parts, w, tm, comms=None):
    T = dz1.shape[0]

    def body(dz_ref, *refs):
        du = jnp.concatenate([r[...] for r in refs[:8]], axis=1)
        refs[9][...] = ALPHA * dz_ref[...] + _dot(du, refs[8][...], NT)

    (dx,), got = _hosted_call(
        body, comms, name="in_proj_bwd", grid=(T // tm,),
        in_specs=[_row(tm, D_MODEL)] + _du_specs(tm) + [_full(w.shape)],
        out_specs=[_row(tm, D_MODEL)], out_shape=[_sds((T, D_MODEL))], scratch_shapes=[], args=(dz1, *du_parts, w))
    return dx, got


def _wgrad(a, b, name, tm, tn, tk):
    T, M = a.shape
    N = b.shape[1]
    tm, tn, tk = min(tm, M), min(tn, N), min(tk, T)
    nk = T // tk

    def body(a_ref, b_ref, o_ref, acc_ref):
        kk = pl.program_id(2)

        @pl.when(kk == 0)
        def _():
            acc_ref[...] = jnp.zeros_like(acc_ref)

        acc_ref[...] += _dot(a_ref[...], b_ref[...], TN)

        @pl.when(kk == nk - 1)
        def _():
            o_ref[...] = acc_ref[...].astype(o_ref.dtype)

    return pl.pallas_call(
        body, name=name, grid=(M // tm, N // tn, nk),
        in_specs=[pl.BlockSpec((tk, tm), lambda i, j, kk: (kk, i)), pl.BlockSpec((tk, tn), lambda i, j, kk: (kk, j))],
        out_specs=pl.BlockSpec((tm, tn), lambda i, j, kk: (i, j)), out_shape=_sds((M, N), _MXU),
        scratch_shapes=[pltpu.VMEM((tm, tn), F32)],
        compiler_params=_cp("parallel", "parallel", "arbitrary"))(a, b)


W_IN_PARTS = 2


def _wgrad_w_in(x, du_parts, tk, part, comms=None):
    T = x.shape[0]
    M = D_MODEL // W_IN_PARTS
    tk = min(tk, T)
    nk = T // tk

    def body(a_ref, *refs):
        o_ref, cs_ref, acc_ref = refs[8:]
        kk = pl.program_id(0)

        @pl.when(kk == 0)
        def _():
            acc_ref[...] = jnp.zeros_like(acc_ref)
            cs_ref[...] = jnp.zeros_like(cs_ref)

        du = jnp.concatenate([r[...] for r in refs[:8]], axis=1)
        acc_ref[...] += _dot(a_ref[...], du, TN)
        cs_ref[...] += _colsum(du.astype(F32))

        @pl.when(kk == nk - 1)
        def _():
            o_ref[...] = acc_ref[...].astype(o_ref.dtype)

    return _hosted_call(
        body, comms, name="wgrad_w_in_%d" % part, grid=(nk,),
        in_specs=[pl.BlockSpec((tk, M), lambda kk: (kk, part))] + _du_specs(tk),
        out_specs=[_full((M, PROJ_WP)), _full((1, PROJ_WP))],
        out_shape=[_sds((M, PROJ_WP), _MXU), _sds((1, PROJ_WP))],
        scratch_shapes=[pltpu.VMEM((M, PROJ_WP), F32)], args=(x, *du_parts))


W_IN_S, FF_S, OUT_S, PP_S = PROJ_W // N_DEV, D_FF // N_DEV, D_MODEL // N_DEV, D_MODEL // N_DEV
LATE = ("w_ffn_gate", "w_ffn_up", "w_out", "w_ffn_down", "ple_w_gate", "ple_w_proj")
BIG = ("w_in",) + LATE
TRANSPOSED = ("w_ffn_gate", "w_ffn_up")


def _split_cols(a, n):
    return a.reshape(a.shape[0], N_DEV, n).transpose(1, 0, 2)


def _join_cols(a):
    return a.transpose(1, 0, 2).reshape(a.shape[1], -1)


def _step(x, p, tgt, w_in, b_in, lb_logits, conv_w, conv_b, g_hg, g_ml, ln1_g, ln1_b, ln2_g, ln2_b, bpg, late,
          distributed):
    T = x.shape[0]
    tm, tf = min(ROWS, T), min(ROWS_FFN, T)
    gather = lambda *names: [_GatherTwoLevel([late[n] for n in names])] if distributed else None
    scatter = lambda *arrs: [_Comm("scatter", list(arrs))] if distributed else None
    rows = lambda a, n: a.reshape(N_DEV, n, D_MODEL)
    (u, pre, qkc), got1 = _in_proj(x, w_in, b_in, conv_w, conv_b, tm, gather("w_out", "ple_w_gate", "ple_w_proj"))
    (o_hg, hg_states), got2 = _hgrn2_fwd(u, lb_logits, gather("w_ffn_gate", "w_ffn_up"))
    (h_ml, cst, nst, mst), got3 = _mlstm_fwd(qkc, u, gather("w_ffn_down"))
    if distributed:
        w_out, wpg, wpp = got1[0][0].reshape(D_MODEL, D_MODEL), got1[0][1].reshape(D_MODEL, D_MODEL), _join_cols(got1[0][2])
        wg, wu, wd = (a.reshape(D_FF, D_MODEL) for a in (got2[0][0], got2[0][1], got3[0][0]))
    else:
        w_out, wg, wu, wd, wpg, wpp = (late[n] for n in ("w_out", "w_ffn_gate", "w_ffn_up", "w_ffn_down", "ple_w_gate", "ple_w_proj"))
    m_in, z1 = _out_proj(x, u, o_hg, h_ml, g_hg, g_ml, w_out, tm)
    z2, x1, x2, a_pre, b_pre, hh = _ffn_ln(z1, ln1_g, ln1_b, wg, wu, wd, ln2_g, ln2_b, tf)
    de, dgp, dz2, loss_vec, d_bpg, d_ln2g, d_ln2b = _ple_loss_ln2_bwd(z2, p, tgt, wpg, bpg, wpp, ln2_g, ln2_b, tm)
    big = dict(ple_w_gate=_wgrad(x2, dgp, "wgrad_ple_gate", 512, D_MODEL, 1024),
               ple_w_proj=_wgrad(p, de, "wgrad_ple_proj", 512, D_MODEL, 1024))
    (da, dbb, dz1, d_ln1g, d_ln1b), r1 = _ffn_bwd_ln1_bwd(
        a_pre, b_pre, z1, dz2, wg, wu, wd, ln1_g, ln1_b, tf, scatter(rows(big["ple_w_gate"], OUT_S), _split_cols(big["ple_w_proj"], PP_S)))
    big.update(
        w_ffn_gate=_wgrad(da, x1, "wgrad_ffn_gate", D_FF, D_MODEL, 1024),
        w_ffn_up=_wgrad(dbb, x1, "wgrad_ffn_up", D_FF, D_MODEL, 1024),
        w_ffn_down=_wgrad(hh, dz2, "wgrad_ffn_down", D_FF, D_MODEL, 1024),
        w_out=_wgrad(m_in, dz1, "wgrad_w_out", 512, D_MODEL, 1024))
    d_ohg, d_hml, d_hgate, d_mo, d_ghg, d_gml = _out_proj_bwd(dz1, u, o_hg, h_ml, g_hg, g_ml, w_out, tm)
    (d_hq, d_hf, d_hv, d_lb), r2 = _hgrn2_bwd(
        u, lb_logits, d_ohg, hg_states,
        scatter(rows(big["w_ffn_gate"], FF_S), rows(big["w_ffn_up"], FF_S), rows(big["w_ffn_down"], FF_S),
                rows(big["w_out"], OUT_S)))
    d_mqk, d_mv, d_gates, d_convw, d_convb = _mlstm_bwd(qkc, u, d_hml, cst, nst, mst, pre, conv_w)
    du_parts = [d_hq, d_hf, d_hv, d_hgate, d_mqk, d_mv, d_mo, d_gates]
    own = lambda g: _split_cols(g[:, :PROJ_W], W_IN_S)
    (g_in0, d_bin), _ = _wgrad_w_in(x, du_parts, 512, 0)
    (g_in1, _), r_in0 = _wgrad_w_in(x, du_parts, 512, 1, scatter(own(g_in0)))
    big["w_in"] = jnp.concatenate([g_in0, g_in1], axis=0)
    small = dict(b_in=d_bin[:, :PROJ_W], hg_lb_logits=d_lb, ml_conv_w=d_convw, ml_conv_b=d_convb, hg_norm_g=d_ghg,
                 ml_norm_g=d_gml, ln1_g=d_ln1g, ln1_b=d_ln1b, ln2_g=d_ln2g, ln2_b=d_ln2b, ple_b_gate=d_bpg)
    last = [_Comm("scatter", [own(g_in1)]), _Comm("gather", [loss_vec] + [small[n] for n in SMALL])] if distributed else None
    dx, r3 = _in_proj_bwd(dz1, du_parts, w_in, tm, last)
    gathered_small = None
    if distributed:
        big = dict(ple_w_gate=r1[0][0], ple_w_proj=r1[0][1], w_ffn_gate=r2[0][0], w_ffn_up=r2[0][1],
                   w_ffn_down=r2[0][2], w_out=r2[0][3], w_in=[r_in0[0][0], r3[0][0]])
        gathered_small = r3[1]
    return loss_vec, dx, big, small, gathered_small


SMALL = ("b_in", "hg_lb_logits", "ml_conv_w", "ml_conv_b", "hg_norm_g", "ml_norm_g", "ln1_g", "ln1_b", "ln2_g", "ln2_b",
         "ple_b_gate")


def _padc(a, n):
    return jnp.pad(a, [(0, 0)] * (a.ndim - 1) + [(0, n - a.shape[-1])])


def _adamw(w, g, m, v):
    m = B1 * m + (1.0 - B1) * g
    v = B2 * v + (1.0 - B2) * jnp.square(g)
    m_hat = m / (1.0 - B1 ** STEP)
    v_hat = v / (1.0 - B2 ** STEP)
    return -LR * (m_hat / (jnp.sqrt(v_hat) + EPS) + WD * w), m, v


def _sum_slabs(ref):
    g = ref[0].astype(F32)
    for j in range(1, N_DEV):
        g = g + ref[j].astype(F32)
    return g


def _adamw_matrix(rbs, w, m, v, name):
    rbs = list(rbs) if isinstance(rbs, (list, tuple)) else [rbs]
    nb = len(rbs)
    R, C = w.shape
    tr = 256 if (R // nb) % 256 == 0 else R // nb
    per = R // nb // tr

    def body(*refs):
        w_ref, m_ref, v_ref, g_ref, d_ref, m2_ref, v2_ref = refs[nb:]
        i = pl.program_id(0)
        g = _sum_slabs(refs[0])
        for k in range(1, nb):
            g = jnp.where(i >= k * per, _sum_slabs(refs[k]), g)
        g_ref[...] = g
        d_ref[...], m2_ref[...], v2_ref[...] = _adamw(w_ref[...], g, m_ref[...], v_ref[...])

    blk = pl.BlockSpec((tr, C), lambda i: (i, 0))
    part = lambda k: pl.BlockSpec((N_DEV, tr, C), lambda i, k=k: (0, jnp.clip(i - k * per, 0, per - 1), 0))
    return pl.pallas_call(
        body, name=name, grid=(R // tr,),
        in_specs=[part(k) for k in range(nb)] + [blk, blk, blk],
        out_specs=[blk] * 4, out_shape=[_sds((R, C))] * 4, compiler_params=_cp("parallel"))(*rbs, w, m, v)


def _adamw_small(loss_g, gs, ws, ms, vs):
    n = len(ws)

    def body(*refs):
        loss_ref, g_refs, w_refs, m_refs, v_refs = refs[0], refs[1:1 + n], refs[1 + n:1 + 2 * n], refs[1 + 2 * n:1 + 3 * n], refs[1 + 3 * n:1 + 4 * n]
        outs = refs[1 + 4 * n:]
        outs[0][...] = (0.5 / D_MODEL) * jnp.sum(_sum_slabs(loss_ref), keepdims=True)
        for i in range(n):
            g = _sum_slabs(g_refs[i])
            outs[1 + i][...] = g
            outs[1 + n + i][...], outs[1 + 2 * n + i][...], outs[1 + 3 * n + i][...] = _adamw(
                w_refs[i][...], g, m_refs[i][...], v_refs[i][...])

    res = pl.pallas_call(
        body, name="adamw_small", out_shape=[_sds((1, 1))] + [_sds(w.shape) for w in ws] * 4)(loss_g, *gs, *ws, *ms, *vs)
    return res[0], [res[1 + k * n:1 + (k + 1) * n] for k in range(4)]


WEIGHTS = ("w_in", "b_in", "hg_lb_logits", "ml_conv_w", "ml_conv_b", "hg_norm_g", "ml_norm_g", "w_out", "ln1_g", "ln1_b",
           "w_ffn_gate", "w_ffn_up", "w_ffn_down", "ln2_g", "ln2_b", "ple_w_proj", "ple_w_gate", "ple_b_gate")
CONV_S = HALF // N_DEV


def kernel(x, p, w_in, b_in, hg_lb_logits, ml_conv_w, ml_conv_b, hg_norm_g, ml_norm_g, w_out, ln1_g, ln1_b, w_ffn_gate, w_ffn_up, w_ffn_down, ln2_g, ln2_b, ple_w_proj, ple_w_gate, ple_b_gate, loss_target, m_w_in, m_b_in, m_hg_lb_logits, m_ml_conv_w, m_ml_conv_b, m_hg_norm_g, m_ml_norm_g, m_w_out, m_ln1_g, m_ln1_b, m_w_ffn_gate, m_w_ffn_up, m_w_ffn_down, m_ln2_g, m_ln2_b, m_ple_w_proj, m_ple_w_gate, m_ple_b_gate, v_w_in, v_b_in, v_hg_lb_logits, v_ml_conv_w, v_ml_conv_b, v_hg_norm_g, v_ml_norm_g, v_w_out, v_ln1_g, v_ln1_b, v_w_ffn_gate, v_w_ffn_up, v_w_ffn_down, v_ln2_g, v_ln2_b, v_ple_w_proj, v_ple_w_gate, v_ple_b_gate):
    args = locals()
    me = 4 * lax.axis_index("x") + 2 * lax.axis_index("y") + lax.axis_index("c")
    shapes = {n: args[n].shape for n in WEIGHTS}
    def drop(n, a):
        a = a[0] if n in BIG or n == "ml_conv_w" else a
        return a.T if n in TRANSPOSED else a

    W = {n: drop(n, args[n]) for n in WEIGHTS}
    M = {n: drop(n, args["m_" + n]) for n in WEIGHTS}
    V = {n: drop(n, args["v_" + n]) for n in WEIGHTS}

    g_in, g_conv = _gather_two_level(
        [W["w_in"].astype(_MXU), jnp.pad(W["ml_conv_w"], ((0, 4), (0, 128 - CONV_S)))], "gather_w_in")
    w_in_full = _padc(_join_cols(g_in), PROJ_WP)
    conv_full = _join_cols(g_conv[:, :4, :CONV_S])

    _, dx, big, _, sg = _step(
        x[0], p[0, 0], loss_target[0], w_in_full, _padc(b_in, PROJ_WP), hg_lb_logits, conv_full, ml_conv_b,
        hg_norm_g, ml_norm_g, ln1_g, ln1_b, ln2_g, ln2_b, ple_b_gate, {n: W[n].astype(_MXU) for n in LATE}, True)

    upd = {n: _adamw_matrix(big[n], W[n], M[n], V[n], "adamw_" + n) for n in BIG}
    sg = dict(zip(SMALL, sg[1:]), loss=sg[0])
    sg["ml_conv_w"] = lax.dynamic_slice(sg["ml_conv_w"], (0, 0, me * CONV_S), (N_DEV, 4, CONV_S))
    loss, small_upd = _adamw_small(sg["loss"], *[[d[n] for n in SMALL] for d in (sg, W, M, V)])

    outs = []
    for kind in range(4):
        smalls = dict(zip(SMALL, small_upd[kind]))
        for n in WEIGHTS:
            o = upd[n][kind] if n in BIG else smalls[n]
            outs.append((o.T if n in TRANSPOSED else o).reshape(shapes[n]))
    return (loss.reshape(()), dx.reshape(x.shape), *outs)
```

```python
import jax
import jax.numpy as jnp
from jax import lax
from jax.experimental import pallas as pl
from jax.experimental.pallas import tpu as pltpu

F32 = jnp.float32
_MXU = jnp.bfloat16

D_MODEL = 1024
CHUNK = 64
SUB = 16
PLE_DIM = 256
HEADS = 4
ML_DQK = 64
HALF = 512
D_FF = 2816
PROJ_W = 3592
PROJ_WP = 3712
ALPHA = float(2 ** 0.25)
LN_EPS = 1e-5
RMS_EPS = 1e-6
ML_SCALE = ML_DQK ** -0.5
N_DEV = 8
LR, B1, B2, EPS, WD, STEP = 0.001, 0.9, 0.999, 1e-08, 0.01, 10
NEG = -1e30
LOG2E = 1.4426950408889634

C_HQ, C_HF, C_HV, C_HGATE, C_MQK, C_MV, C_MO, C_GATES = 0, 4, 8, 12, 16, 20, 24, 28
DU_WIDTHS = (HALF,) * 7 + (128,)

VMEM_LIMIT = 52 * 1024 * 1024
GC = 8
ROWS = 512
ROWS_FFN = 256

NN = (((1,), (0,)), ((), ()))
NT = (((1,), (1,)), ((), ()))
TN = (((0,), (0,)), ((), ()))
BNT = (((2,), (2,)), ((0,), (0,)))
BNN = (((2,), (1,)), ((0,), (0,)))
BTN = (((1,), (1,)), ((0,), (0,)))


def _dot(a, b, dims=NN):
    return lax.dot_general(a.astype(_MXU), b.astype(_MXU), dims, preferred_element_type=F32)


def _sig(x):
    return jax.nn.sigmoid(x)


def _cp(*sem):
    return pltpu.CompilerParams(dimension_semantics=sem, vmem_limit_bytes=VMEM_LIMIT)


def _row(tm, c, blk=0):
    return pl.BlockSpec((tm, c), lambda i, blk=blk: (i, blk))


def _full(shape):
    nd = len(shape)
    return pl.BlockSpec(tuple(shape), lambda *_, nd=nd: (0,) * nd)


def _sds(shape, dtype=F32):
    return jax.ShapeDtypeStruct(tuple(shape), dtype)


def _iota(shape, axis):
    return lax.broadcasted_iota(jnp.int32, shape, axis)


def _colsum(x):
    return jnp.sum(x, axis=0, keepdims=True)


def _rowsum(x):
    return jnp.sum(x, axis=1, keepdims=True)


def _ln_fwd(z, g, b):
    mu = jnp.mean(z, axis=-1, keepdims=True)
    zc = z - mu
    var = jnp.mean(zc * zc, axis=-1, keepdims=True)
    rstd = lax.rsqrt(var + LN_EPS)
    xhat = zc * rstd
    return xhat * g + b, xhat, rstd


def _ln_bwd(dy, xhat, rstd, g):
    dxh = dy * g
    m1 = jnp.mean(dxh, axis=-1, keepdims=True)
    m2 = jnp.mean(dxh * xhat, axis=-1, keepdims=True)
    return rstd * (dxh - m1 - xhat * m2)


def _dsilu(x, s):
    return s * (1.0 + x * (1.0 - s))


MESH = pl.DeviceIdType.MESH
ANY = pl.BlockSpec(memory_space=pl.ANY)


def _flip(v, bit):
    return 1 - v if bit else v


class _Comm:
    def __init__(self, kind, srcs):
        self.kind, self.srcs, self.n = kind, list(srcs), len(srcs)

    def out_shape(self):
        lead = (N_DEV,) if self.kind == "gather" else ()
        return [jax.ShapeDtypeStruct(lead + s.shape, s.dtype) for s in self.srcs]

    def scratch(self):
        return [pltpu.SemaphoreType.DMA((7 * self.n,)), pltpu.SemaphoreType.DMA((7 * self.n,)),
                pltpu.SemaphoreType.DMA((self.n,))]

    def copies(self, srcs, dsts, send_sems, recv_sems, local_sems):
        x, y, c = lax.axis_index("x"), lax.axis_index("y"), lax.axis_index("c")
        me = 4 * x + 2 * y + c
        pick = (lambda s, j: s) if self.kind == "gather" else (lambda s, j: s.at[j])
        out = []
        for i, (s, d) in enumerate(zip(srcs, dsts)):
            out.append(pltpu.make_async_copy(pick(s, me), d.at[me], local_sems.at[i]))
            for k in range(1, N_DEV):
                px, py, pc = _flip(x, k & 4), _flip(y, k & 2), _flip(c, k & 1)
                out.append(pltpu.make_async_remote_copy(
                    src_ref=pick(s, 4 * px + 2 * py + pc), dst_ref=d.at[me], send_sem=send_sems.at[7 * i + k - 1],
                    recv_sem=recv_sems.at[7 * i + k - 1], device_id=(px, py, pc), device_id_type=MESH))
        return out

    def start(self, *refs):
        for cp in self.copies(*refs):
            cp.start()

    def mid(self, *refs):
        pass

    def finish(self, *refs):
        for cp in self.copies(*refs):
            cp.wait()


class _GatherTwoLevel(_Comm):
    def __init__(self, srcs):
        super().__init__("gather", srcs)

    def _parts(self, srcs, dsts, send_sems, recv_sems, local_sems):
        x, y, c = lax.axis_index("x"), lax.axis_index("y"), lax.axis_index("c")
        me, sibling = (x, y, c), (x, y, 1 - c)
        chips = [(1 - x, y), (x, 1 - y), (1 - x, 1 - y)]

        def copy(i, k, block, to, own=False):
            slab = dsts[i].at[4 * block[0] + 2 * block[1] + block[2]]
            return pltpu.make_async_remote_copy(
                src_ref=srcs[i] if own else slab, dst_ref=slab, send_sem=send_sems.at[7 * i + k],
                recv_sem=recv_sems.at[7 * i + k], device_id=to, device_id_type=MESH)

        n = range(self.n)
        mine = [pltpu.make_async_copy(srcs[i], dsts[i].at[4 * x + 2 * y + c], local_sems.at[i]) for i in n]
        first = [copy(i, 0, me, sibling, own=True) for i in n]
        first += [copy(i, 1 + j, me, (*chip, c), own=True) for j, chip in enumerate(chips) for i in n]
        over_ici = [copy(i, 1 + j, (*chip, c), me) for j, chip in enumerate(chips) for i in n]
        passed = [copy(i, 4 + j, (*chip, c), sibling) for j, chip in enumerate(chips) for i in n]
        from_sibling = [copy(i, 0, sibling, me) for i in n]
        from_sibling += [copy(i, 4 + j, (*chip, 1 - c), me) for j, chip in enumerate(chips) for i in n]
        return mine, first, over_ici, passed, from_sibling

    def start(self, *refs):
        mine, first, _, _, _ = self._parts(*refs)
        for cp in mine + first:
            cp.start()

    def mid(self, *refs):
        _, _, over_ici, passed, _ = self._parts(*refs)
        for arrived, onward in zip(over_ici, passed):
            arrived.wait_recv()
            onward.start()

    def finish(self, *refs):
        mine, first, _, passed, from_sibling = self._parts(*refs)
        for cp in from_sibling:
            cp.wait_recv()
        for cp in first + passed:
            cp.wait_send()
        for cp in mine:
            cp.wait()


def _hosted_call(body, comms, *, name, grid, in_specs, out_specs, out_shape, scratch_shapes, args):
    comms = list(comms or [])
    if not comms:
        res = pl.pallas_call(body, name=name, grid=grid, in_specs=in_specs, out_specs=out_specs, out_shape=out_shape,
                             scratch_shapes=scratch_shapes, compiler_params=_cp("arbitrary"))(*args)
        return list(res), []
    n_in, n_out, n_sc, nc = len(in_specs), len(out_specs), len(scratch_shapes), sum(cm.n for cm in comms)
    last = grid[0] - 1

    def hosted(*refs):
        ins, csrc = refs[:n_in], refs[n_in:n_in + nc]
        o0 = n_in + nc
        outs, cdst = refs[o0:o0 + n_out], refs[o0 + n_out:o0 + n_out + nc]
        s0 = o0 + n_out + nc
        scr, sems = refs[s0:s0 + n_sc], refs[s0 + n_sc:]

        def phase(which):
            o = 0
            for j, cm in enumerate(comms):
                getattr(cm, which)(csrc[o:o + cm.n], cdst[o:o + cm.n], *sems[3 * j:3 * j + 3])
                o += cm.n

        i = pl.program_id(0)

        @pl.when(i == 0)
        def _():
            phase("start")

        body(*ins, *outs, *scr)

        @pl.when(i == (2 * last) // 3)
        def _():
            phase("mid")

        @pl.when(i == last)
        def _():
            phase("finish")

    res = pl.pallas_call(
        hosted, name=name, grid=grid, in_specs=list(in_specs) + [ANY] * nc, out_specs=list(out_specs) + [ANY] * nc,
        out_shape=list(out_shape) + [s for cm in comms for s in cm.out_shape()],
        scratch_shapes=list(scratch_shapes) + [s for cm in comms for s in cm.scratch()],
        compiler_params=_cp("arbitrary"))(*args, *[a for cm in comms for a in cm.srcs])
    got, o = [], n_out
    for cm in comms:
        got.append(list(res[o:o + cm.n]))
        o += cm.n
    return list(res[:n_out]), got


def _gather_two_level(blocks, name):
    n = len(blocks)

    def body(*refs):
        x_refs, out_refs = refs[:n], refs[n:2 * n]
        send_sems, recv_sems, local_sems = refs[2 * n:]
        x, y, c = lax.axis_index("x"), lax.axis_index("y"), lax.axis_index("c")
        me, sibling = (x, y, c), (x, y, 1 - c)
        chips = [(1 - x, y), (x, 1 - y), (1 - x, 1 - y)]

        def copy(i, k, block, to, own=False):
            slab = out_refs[i].at[4 * block[0] + 2 * block[1] + block[2]]
            return pltpu.make_async_remote_copy(
                src_ref=x_refs[i] if own else slab, dst_ref=slab, send_sem=send_sems.at[7 * i + k],
                recv_sem=recv_sems.at[7 * i + k], device_id=to, device_id_type=MESH)

        mine = [pltpu.make_async_copy(x_refs[i], out_refs[i].at[4 * x + 2 * y + c], local_sems.at[i]) for i in range(n)]
        for cp in mine:
            cp.start()
        first = [copy(i, 0, me, sibling, own=True) for i in range(n)]
        first += [copy(i, 1 + j, me, (*chip, c), own=True) for j, chip in enumerate(chips) for i in range(n)]
        for cp in first:
            cp.start()
        passed = []
        for j, chip in enumerate(chips):
            for i in range(n):
                copy(i, 1 + j, (*chip, c), me).wait_recv()
                passed.append(copy(i, 4 + j, (*chip, c), sibling))
                passed[-1].start()
        for i in range(n):
            copy(i, 0, sibling, me).wait_recv()
            for j, chip in enumerate(chips):
                copy(i, 4 + j, (*chip, 1 - c), me).wait_recv()
        for cp in first + passed:
            cp.wait_send()
        for cp in mine:
            cp.wait()

    return pl.pallas_call(
        body, name=name, out_shape=[jax.ShapeDtypeStruct((N_DEV,) + b.shape, b.dtype) for b in blocks],
        in_specs=[ANY] * n, out_specs=[ANY] * n,
        scratch_shapes=[pltpu.SemaphoreType.DMA((7 * n,)), pltpu.SemaphoreType.DMA((7 * n,)),
                        pltpu.SemaphoreType.DMA((n,))])(*blocks)


def _in_proj(x, w, b, cw, cb, tm, comms=None):
    T = x.shape[0]

    def body(x_ref, w_ref, b_ref, cw_ref, cb_ref, o_ref, pre_ref, act_ref, halo_sc):
        @pl.when(pl.program_id(0) == 0)
        def _():
            halo_sc[...] = jnp.zeros_like(halo_sc)

        o = _dot(x_ref[...], w_ref[...]) + b_ref[...]
        o_ref[...] = o
        xc = o[:, 128 * C_MQK:128 * C_MQK + HALF]
        halo = halo_sc[...]
        rowi = _iota((8, HALF), 0)
        acc = xc * cw_ref[3:4, :] + cb_ref[...]
        for j in (1, 2, 3):
            acc = acc + _shift_rows(xc, halo, j, rowi) * cw_ref[3 - j:4 - j, :]
        pre_ref[...] = acc
        act_ref[...] = acc * _sig(acc)
        halo_sc[...] = xc[tm - 8:]

    return _hosted_call(
        body, comms, name="in_proj", grid=(T // tm,),
        in_specs=[_row(tm, D_MODEL), _full(w.shape), _full(b.shape), _full(cw.shape), _full(cb.shape)],
        out_specs=[_row(tm, PROJ_WP), _row(tm, HALF), _row(tm, HALF)],
        out_shape=[_sds((T, PROJ_WP)), _sds((T, HALF)), _sds((T, HALF))],
        scratch_shapes=[pltpu.VMEM((8, HALF), F32)], args=(x, w, b, cw, cb))


def _shift_rows(x, halo, j, rowi):
    r = pltpu.roll(x, j, 0)
    top = jnp.where(rowi < j, pltpu.roll(halo, j, 0), r[:8])
    return jnp.concatenate([top, r[8:]], axis=0)


def _shift_rows_up(x, halo, j, rowi):
    n = x.shape[0]
    r = pltpu.roll(x, n - j, 0)
    bot = jnp.where(rowi >= 8 - j, pltpu.roll(halo, 8 - j, 0), r[n - 8:])
    return jnp.concatenate([r[:n - 8], bot], axis=0)


def _bdot(a, b, dims):
    return lax.dot_general(a.astype(_MXU), b.astype(_MXU), dims, preferred_element_type=F32)


def _bdotx(a, b, dims):
    return lax.dot_general(a, b, dims, precision=lax.Precision.HIGHEST, preferred_element_type=F32)


def _heads_to_batch(x, w):
    G = x.shape[0] // CHUNK
    x3 = x.reshape(G, CHUNK, HEADS * w)
    return jnp.stack([x3[:, :, w * h:w * (h + 1)] for h in range(HEADS)], axis=1).reshape(G * HEADS, CHUNK, w)


def _batch_to_heads(x3):
    B, _, w = x3.shape
    x4 = x3.reshape(B // HEADS, HEADS, CHUNK, w)
    return jnp.concatenate([x4[:, h] for h in range(HEADS)], axis=-1).reshape(B // HEADS * CHUNK, HEADS * w)


def _chunk_cumsum(x, rowmod, reverse=False):
    R = x.shape[0]
    for sh in (1, 2, 4, 8, 16, 32):
        if reverse:
            x = x + jnp.where(rowmod < CHUNK - sh, pltpu.roll(x, R - sh, 0), 0.0)
        else:
            x = x + jnp.where(rowmod >= sh, pltpu.roll(x, sh, 0), 0.0)
    return x


def _lane_col(x, c, lane):
    return _rowsum(jnp.where(lane == c, x, 0.0))


def _hg_gates(hq, hf, lb):
    sg = _sig(hf)
    nsg = _sig(-hf)
    f = lb + (1.0 - lb) * sg
    g = jnp.log(f)
    k = (1.0 - lb) * nsg
    sq = _sig(hq)
    return hq * sq, g, k, f, sg, nsg, sq


def _hg_prep(hq_ref, hf_ref, lg_ref, b_sc, k_sc):
    R = hq_ref.shape[0]
    G = R // CHUNK
    lb = _sig(lg_ref[0:1, :] - lg_ref[1:2, :])
    hq = hq_ref[...]
    q, g, k, f, sg, nsg, sq = _hg_gates(hq, hf_ref[...], lb)
    rowmod = _iota((R, HALF), 0) & (CHUNK - 1)
    b = _chunk_cumsum(g, rowmod) * LOG2E
    last8 = _iota((8, HALF), 0) == 7
    bl_rows = [_colsum(jnp.where(last8, b[CHUNK * c + CHUNK - 8:CHUNK * (c + 1)], 0.0)) for c in range(G)]
    bl3 = jnp.stack([r[:, 128 * h:128 * (h + 1)] for r in bl_rows for h in range(HEADS)], axis=0)
    b3, k3 = _heads_to_batch(b, 128), _heads_to_batch(k, 128)
    b_sc[...] = b3
    k_sc[...] = k3
    return dict(G=G, lb=lb, hq=hq, f=f, sg=sg, nsg=nsg, sq=sq, rowmod=rowmod, q3=_heads_to_batch(q, 128), k3=k3, b3=b3,
                bl3=bl3)


HSUB = 8


def _lo(j):
    return HSUB * (j // HSUB)


def _hg_diag_tiles(b_sc, b3, r0, rowi):
    bi = b3[:, r0:r0 + SUB]
    return [jnp.exp2(jnp.where(rowi[:, _lo(s):] >= s, bi[:, _lo(s):] - b_sc[:, r0 + s:r0 + s + 1, :], NEG))
            for s in range(SUB)]


def _hg_diag_tiles_t(b_sc, b3, r0, rowi):
    bi = b3[:, r0:r0 + SUB]
    return [jnp.exp2(jnp.where(rowi[:, :_lo(t) + HSUB] <= t, b_sc[:, r0 + t:r0 + t + 1, :] - bi[:, :_lo(t) + HSUB], NEG))
            for t in range(SUB)]


def _lane_sums(pieces, ones):
    B = pieces[0].shape[0]
    hs = [p.shape[1] for p in pieces]
    R = _dot(jnp.concatenate(pieces, axis=1).reshape(B * sum(hs), 128), ones).reshape(B, sum(hs), 128)
    out, o = [], 0
    for h in hs:
        out.append(R[:, o:o + h])
        o += h
    return out


def _sum_tri(terms, low_rows):
    if SUB == HSUB:
        return sum(terms)
    full = sum(t for t in terms if t.shape[1] == SUB)
    half = sum(t for t in terms if t.shape[1] == HSUB)
    lo, hi = full[:, :HSUB], full[:, HSUB:]
    return jnp.concatenate([lo + half, hi] if low_rows else [lo, hi + half], axis=1)


def _hgrn2_fwd(u, lb_logits, comms=None):
    T = u.shape[0]
    G = min(GC, T // CHUNK)
    R, B, N = G * CHUNK, G * HEADS, T // CHUNK

    def body(hq_ref, hf_ref, hv_ref, lg_ref, o_ref, st_ref, S_ref, b_sc, k_sc, v_sc):
        @pl.when(pl.program_id(0) == 0)
        def _():
            S_ref[...] = jnp.zeros_like(S_ref)

        pz = _hg_prep(hq_ref, hf_ref, lg_ref, b_sc, k_sc)
        q3, k3, b3, bl3 = pz["q3"], pz["k3"], pz["b3"], pz["bl3"]
        v3 = _heads_to_batch(hv_ref[...], 128)
        v_sc[...] = v3
        stloc = _bdot(v3, k3 * jnp.exp2(bl3 - b3), BTN).reshape(G, HEADS, 128, 128)
        dec = jnp.exp2(bl3).reshape(G, HEADS, 1, 128)
        ST = S_ref[...]
        sts = []
        for c in range(G):
            sts.append(ST)
            ST = ST * dec[c] + stloc[c]
        S_ref[...] = ST
        st4 = jnp.stack(sts, axis=0)
        st_ref[...] = st4
        o = _bdot(q3 * jnp.exp2(b3), st4.reshape(B, 128, 128), BNT)
        ones = jnp.ones((128, 128), F32)
        rowi = _iota((1, SUB, 128), 1)
        outs = []
        for i in range(CHUNK // SUB):
            r0 = SUB * i
            qi = q3[:, r0:r0 + SUB]
            oi = o[:, r0:r0 + SUB]
            if i > 0:
                r = b_sc[:, r0 - 1:r0, :]
                qe = qi * jnp.exp2(b3[:, r0:r0 + SUB] - r)
                ke = k3[:, :r0] * jnp.exp2(r - b3[:, :r0])
                oi = oi + _bdot(_bdot(qe, ke, BNT), v3[:, :r0], BNN)
            tiles = _hg_diag_tiles(b_sc, b3, r0, rowi)
            a_b = _lane_sums([qi[:, _lo(s):] * (k_sc[:, r0 + s:r0 + s + 1, :] * tiles[s]) for s in range(SUB)], ones)
            outs.append(oi + _sum_tri([a_b[s] * v_sc[:, r0 + s:r0 + s + 1, :] for s in range(SUB)], False))
        o_ref[...] = _batch_to_heads(jnp.concatenate(outs, axis=1))

    blk = lambda c: pl.BlockSpec((R, HALF), lambda n, c=c: (n, c // 4))
    return _hosted_call(
        body, comms, name="hgrn2_fwd", grid=(N // G,),
        in_specs=[blk(C_HQ), blk(C_HF), blk(C_HV), _full(lb_logits.shape)],
        out_specs=[pl.BlockSpec((R, HALF), lambda n: (n, 0)),
                   pl.BlockSpec((G, HEADS, 128, 128), lambda n: (n, 0, 0, 0))],
        out_shape=[_sds((T, HALF)), _sds((N, HEADS, 128, 128))],
        scratch_shapes=[pltpu.VMEM((HEADS, 128, 128), F32)] + [pltpu.VMEM((B, CHUNK, 128), F32)] * 3,
        args=(u, u, u, lb_logits))


def _hgrn2_bwd(u, lb_logits, do, states, comms=None):
    T = u.shape[0]
    G = min(GC, T // CHUNK)
    R, B, NG = G * CHUNK, G * HEADS, T // (G * CHUNK)

    def body(hq_ref, hf_ref, hv_ref, lg_ref, do_ref, st_ref, dhq_ref, dhf_ref, dhv_ref, dlb_ref,
             dS_ref, b_sc, k_sc, v_sc, q_sc, do_sc):
        @pl.when(pl.program_id(0) == 0)
        def _():
            dS_ref[...] = jnp.zeros_like(dS_ref)
            dlb_ref[...] = jnp.zeros_like(dlb_ref)

        pz = _hg_prep(hq_ref, hf_ref, lg_ref, b_sc, k_sc)
        q3, k3, b3, bl3, lb = pz["q3"], pz["k3"], pz["b3"], pz["bl3"], pz["lb"]
        v3 = _heads_to_batch(hv_ref[...], 128)
        v_sc[...] = v3
        do3 = _heads_to_batch(do_ref[...], 128)
        q_sc[...] = q3
        do_sc[...] = do3
        st3 = st_ref[...].reshape(B, 128, 128)
        eb = jnp.exp2(b3)
        ebl = jnp.exp2(bl3 - b3)
        qt = q3 * eb
        kl = k3 * ebl
        dstloc = _bdot(do3, qt, BTN).reshape(G, HEADS, 128, 128)
        dec = jnp.exp2(bl3).reshape(G, HEADS, 1, 128)
        dST = dS_ref[...]
        dsts = [None] * G
        for c in reversed(range(G)):
            dsts[c] = dST
            dST = dST * dec[c] + dstloc[c]
        dS_ref[...] = dST
        dst3 = jnp.stack(dsts, axis=0).reshape(B, 128, 128)
        dqt = _bdot(do3, st3, BNN)
        dkl = _bdot(v3, dst3, BNN)
        dv_acc = _bdot(kl, dst3, BNT)
        ones = jnp.ones((128, 128), F32)
        rowi = _iota((1, SUB, 128), 1)
        dq_parts, dk_parts, dv_parts = [], [], []
        dk_in = jnp.zeros((B, CHUNK, 128), F32)
        for i_s in range(CHUNK // SUB):
            r0 = SUB * i_s
            qi = q3[:, r0:r0 + SUB]
            doi = do3[:, r0:r0 + SUB]
            dqi = jnp.zeros((B, SUB, 128), F32)
            if i_s > 0:
                r = b_sc[:, r0 - 1:r0, :]
                eq = jnp.exp2(b3[:, r0:r0 + SUB] - r)
                ek = jnp.exp2(r - b3[:, :r0])
                qe = qi * eq
                ke = k3[:, :r0] * ek
                a_off = _bdot(qe, ke, BNT)
                p_off = _bdot(doi, v3[:, :r0], BNT)
                pad = jnp.zeros((B, CHUNK - r0, 128), F32)
                dv_acc = dv_acc + jnp.concatenate([_bdot(a_off, doi, BTN), pad], axis=1)
                dqi = dqi + _bdot(p_off, ke, BNN) * eq
                dk_in = dk_in + jnp.concatenate([_bdot(p_off, qe, BTN) * ek, pad], axis=1)
            ki, vi = k3[:, r0:r0 + SUB], v3[:, r0:r0 + SUB]
            rng = range(SUB)
            tiles = _hg_diag_tiles(b_sc, b3, r0, rowi)
            tiles_t = _hg_diag_tiles_t(b_sc, b3, r0, rowi)
            do_rows = [do_sc[:, r0 + t:r0 + t + 1, :] for t in rng]
            kts = [k_sc[:, r0 + s:r0 + s + 1, :] * tiles[s] for s in rng]
            qts = [q_sc[:, r0 + t:r0 + t + 1, :] * tiles_t[t] for t in rng]
            ps = [doi[:, _lo(s):] * v_sc[:, r0 + s:r0 + s + 1, :] for s in rng]
            mst = [ki[:, :_lo(t) + HSUB] * qts[t] for t in rng]
            pst = [vi[:, :_lo(t) + HSUB] * do_rows[t] for t in rng]
            sums = _lane_sums(ps + mst + pst, ones)
            p_b, a_t, p_t = sums[:SUB], sums[SUB:2 * SUB], sums[2 * SUB:]
            dq_parts.append(dqi + _sum_tri([p_b[s] * kts[s] for s in rng], False))
            dv_parts.append(_sum_tri([a_t[t] * do_rows[t] for t in rng], True))
            dk_parts.append(_sum_tri([p_t[t] * qts[t] for t in rng], True))
        dq_in = jnp.concatenate(dq_parts, axis=1)
        dk_in = dk_in + jnp.concatenate(dk_parts, axis=1)
        dv_acc = dv_acc + jnp.concatenate(dv_parts, axis=1)
        db = qt * dqt + q3 * dq_in - k3 * dk_in - kl * dkl
        last = jnp.sum(kl * dkl, axis=1, keepdims=True) + jnp.exp2(bl3) * jnp.sum(st3 * dst3, axis=1, keepdims=True)
        db = db + jnp.where(_iota((1, CHUNK, 1), 1) == CHUNK - 1, last, 0.0)
        dg = _chunk_cumsum(_batch_to_heads(db), pz["rowmod"], reverse=True)
        dq_tot = _batch_to_heads(dqt * eb + dq_in)
        dk_tot = _batch_to_heads(dkl * ebl + dk_in)
        common = dg / pz["f"] - dk_tot
        dhf_ref[...] = ((1.0 - lb) * pz["sg"] * pz["nsg"] * common).astype(dhf_ref.dtype)
        dl0 = _colsum(pz["nsg"] * common) * lb * (1.0 - lb)
        dlb_ref[0:1, :] += dl0
        dlb_ref[1:2, :] -= dl0
        dhq_ref[...] = (dq_tot * _dsilu(pz["hq"], pz["sq"])).astype(dhq_ref.dtype)
        dhv_ref[...] = _batch_to_heads(dv_acc).astype(dhv_ref.dtype)

    rev = lambda c: pl.BlockSpec((R, HALF), lambda i, c=c: (NG - 1 - i, c // 4))
    rev0 = pl.BlockSpec((R, HALF), lambda i: (NG - 1 - i, 0))
    return _hosted_call(
        body, comms, name="hgrn2_bwd", grid=(NG,),
        in_specs=[rev(C_HQ), rev(C_HF), rev(C_HV), _full(lb_logits.shape), rev0,
                  pl.BlockSpec((G, HEADS, 128, 128), lambda i: (NG - 1 - i, 0, 0, 0))],
        out_specs=[rev0, rev0, rev0, _full((2, HALF))],
        out_shape=[_sds((T, HALF), _MXU)] * 3 + [_sds((2, HALF))],
        scratch_shapes=[pltpu.VMEM((HEADS, 128, 128), F32)] + [pltpu.VMEM((B, CHUNK, 128), F32)] * 5,
        args=(u, u, u, lb_logits, do, states))


def _lanes_to_batch_cols(x, lane):
    G = x.shape[0] // CHUNK
    cols = [_lane_col(x, 4 + h, lane).reshape(G, CHUNK, 1) for h in range(HEADS)]
    return jnp.stack(cols, axis=1).reshape(G * HEADS, CHUNK, 1)


def _row_scalars(rows):
    lane = _iota((1, 128), 1)
    return jnp.stack([_rowsum(jnp.where(lane == 4 + h, r, 0.0)) for r in rows for h in range(HEADS)], axis=0)


def _ml_gates(gates):
    R = gates.shape[0]
    lane = _iota((R, 128), 1)
    rowmod = _iota((R, 128), 0) & (CHUNK - 1)
    lf = jnp.minimum(gates, 0.0) - jnp.log(1.0 + jnp.exp(-jnp.abs(gates)))
    g_all = _chunk_cumsum(lf, rowmod)
    x_all = pltpu.roll(gates, 4, 1) - g_all
    return g_all, x_all, lane, rowmod


def _ml_chunk_rows(g_all, x_all, mprev, g):
    gl = g_all[CHUNK * g + CHUNK - 8:CHUNK * (g + 1)]
    gl = _colsum(jnp.where(_iota((8, 128), 0) == 7, gl, 0.0))
    a = gl + x_all[CHUNK * g:CHUNK * (g + 1)]
    m_new = jnp.maximum(gl + mprev, jnp.max(a, axis=0, keepdims=True))
    return m_new, jnp.exp(gl + mprev - m_new), jnp.exp(a - m_new)


def _ml_batched(q3, k3, v3, g_all, x_all, lane, C3, n3, mprev3):
    G = g_all.shape[0] // CHUNK
    gcol3 = _lanes_to_batch_cols(g_all, lane)
    onehot = jnp.where(_iota((G, 8, 128), 1) + 4 == _iota((G, 8, 128), 2), 1.0, 0.0).astype(F32)
    rows = _bdotx(onehot, x_all.reshape(G, CHUNK, 128), BNT)
    sub = _iota((G, 8, CHUNK), 1)
    row3 = jnp.stack([jnp.sum(jnp.where(sub == h, rows, 0.0), axis=1, keepdims=True) for h in range(HEADS)],
                     axis=1).reshape(G * HEADS, 1, CHUNK)
    causal = _iota((1, CHUNK, CHUNK), 1) >= _iota((1, CHUNK, CHUNK), 2)
    dmat = jnp.where(causal, gcol3 + row3, NEG)
    m_inter = gcol3 + mprev3
    m_t = jnp.maximum(m_inter, jnp.max(dmat, axis=2, keepdims=True))
    wi = jnp.exp(dmat - m_t)
    wn = jnp.exp(m_inter - m_t)
    s3 = _bdot(q3, k3, BNT) * wi
    qc = _bdot(q3, C3, BNN)
    qn = jnp.sum(q3 * n3, axis=2, keepdims=True)
    num = _bdot(s3, v3, BNN) + wn * qc
    den = jnp.sum(s3, axis=2, keepdims=True) + wn * qn
    floor = jnp.exp(-m_t)
    return dict(wi=wi, wn=wn, s=s3, qc=qc, qn=qn, num=num, den=den, floor=floor, nrm=jnp.maximum(jnp.abs(den), floor))


def _mlstm_fwd(qkc, u, comms=None):
    T = u.shape[0]
    G = min(GC, T // CHUNK)
    R = G * CHUNK
    N = T // CHUNK

    def body(qk_ref, v_ref, g_ref, h_ref, cst_ref, nst_ref, mst_ref, C_ref, n_ref, m_ref):
        @pl.when(pl.program_id(0) == 0)
        def _():
            C_ref[...] = jnp.zeros_like(C_ref)
            n_ref[...] = jnp.zeros_like(n_ref)
            m_ref[...] = jnp.zeros_like(m_ref)

        g_all, x_all, lane, _ = _ml_gates(g_ref[...])
        m_row = m_ref[...]
        mprev_rows, wo_rows, ws_parts = [], [], []
        for g in range(G):
            mprev_rows.append(m_row)
            m_row, wo, ws = _ml_chunk_rows(g_all, x_all, m_row, g)
            wo_rows.append(wo)
            ws_parts.append(ws)
        m_ref[...] = m_row
        mst_ref[...] = jnp.stack(mprev_rows, axis=0)
        ws3 = _lanes_to_batch_cols(jnp.concatenate(ws_parts, axis=0), lane)
        wo4 = _row_scalars(wo_rows).reshape(G, HEADS, 1, 1)
        q3 = _heads_to_batch(qk_ref[:, :256] * ML_SCALE, ML_DQK)
        k3 = _heads_to_batch(qk_ref[:, 256:], ML_DQK)
        v3 = _heads_to_batch(v_ref[...], 128)
        kw = k3 * ws3
        cloc = _bdot(kw, v3, BTN).reshape(G, HEADS, ML_DQK, 128)
        nloc = jnp.sum(kw, axis=1, keepdims=True).reshape(G, HEADS, 1, ML_DQK)
        C, nn = C_ref[...], n_ref[...]
        cs, ns = [], []
        for g in range(G):
            cs.append(C)
            ns.append(nn)
            C = wo4[g] * C + cloc[g]
            nn = wo4[g] * nn + nloc[g]
        C_ref[...] = C
        n_ref[...] = nn
        c4, n4 = jnp.stack(cs, axis=0), jnp.stack(ns, axis=0)
        cst_ref[...] = c4
        nst_ref[...] = n4
        r = _ml_batched(q3, k3, v3, g_all, x_all, lane, c4.reshape(G * HEADS, ML_DQK, 128),
                        n4.reshape(G * HEADS, 1, ML_DQK), _row_scalars(mprev_rows))
        h_ref[...] = _batch_to_heads(r["num"] / r["nrm"])

    return _hosted_call(
        body, comms, name="mlstm_fwd", grid=(N // G,),
        in_specs=[pl.BlockSpec((R, HALF), lambda n: (n, 0)), pl.BlockSpec((R, HALF), lambda n: (n, C_MV // 4)),
                  pl.BlockSpec((R, 128), lambda n: (n, C_GATES))],
        out_specs=[pl.BlockSpec((R, HALF), lambda n: (n, 0)),
                   pl.BlockSpec((G, HEADS, ML_DQK, 128), lambda n: (n, 0, 0, 0)),
                   pl.BlockSpec((G, HEADS, 1, ML_DQK), lambda n: (n, 0, 0, 0)),
                   pl.BlockSpec((G, 1, 128), lambda n: (n, 0, 0))],
        out_shape=[_sds((T, HALF)), _sds((N, HEADS, ML_DQK, 128)), _sds((N, HEADS, 1, ML_DQK)), _sds((N, 1, 128))],
        scratch_shapes=[pltpu.VMEM((HEADS, ML_DQK, 128), F32), pltpu.VMEM((HEADS, 1, ML_DQK), F32),
                        pltpu.VMEM((1, 128), F32)],
        args=(qkc, u, u))


def _mlstm_bwd(qkc, u, dh, cst, nst, mst, pre, cw):
    T = u.shape[0]
    G = min(GC, T // CHUNK)
    R = G * CHUNK
    NG = T // R

    def body(qk_ref, v_ref, g_ref, dh_ref, cst_ref, nst_ref, mst_ref, pre_ref, x_ref, xh_ref, cw_ref,
             dmqk_ref, dv_ref, dgt_ref, dcw_ref, dcb_ref, dC_ref, dn_ref, next_sc):
        @pl.when(pl.program_id(0) == 0)
        def _():
            for r in (dC_ref, dn_ref, next_sc, dcw_ref, dcb_ref):
                r[...] = jnp.zeros_like(r)

        B = G * HEADS
        gates = g_ref[...]
        g_all, x_all, lane, rowmod = _ml_gates(gates)
        mprev_rows = [mst_ref[g] for g in range(G)]
        wo_rows, ws_parts = [], []
        for g in range(G):
            _, wo, ws = _ml_chunk_rows(g_all, x_all, mprev_rows[g], g)
            wo_rows.append(wo)
            ws_parts.append(ws)
        ws3 = _lanes_to_batch_cols(jnp.concatenate(ws_parts, axis=0), lane)
        wo3 = _row_scalars(wo_rows)
        wo4 = wo3.reshape(G, HEADS, 1, 1)
        q3 = _heads_to_batch(qk_ref[:, :256] * ML_SCALE, ML_DQK)
        k3 = _heads_to_batch(qk_ref[:, 256:], ML_DQK)
        v3 = _heads_to_batch(v_ref[...], 128)
        dh3 = _heads_to_batch(dh_ref[...], 128)
        C3 = cst_ref[...].reshape(B, ML_DQK, 128)
        n3 = nst_ref[...].reshape(B, 1, ML_DQK)
        r = _ml_batched(q3, k3, v3, g_all, x_all, lane, C3, n3, _row_scalars(mprev_rows))
        wn, s3 = r["wn"], r["s"]
        inv = 1.0 / r["nrm"]
        dnum = dh3 * inv
        dnrm = -jnp.sum(dh3 * (r["num"] * inv), axis=2, keepdims=True) * inv
        dden = jnp.where(jnp.abs(r["den"]) > r["floor"], dnrm * jnp.sign(r["den"]), 0.0)
        ds = _bdot(dnum, v3, BNT) + dden
        dqk = ds * r["wi"]
        dd = ds * s3
        qw = q3 * wn
        dcloc = _bdot(qw, dnum, BTN).reshape(G, HEADS, ML_DQK, 128)
        dnloc = jnp.sum(qw * dden, axis=1, keepdims=True).reshape(G, HEADS, 1, ML_DQK)
        dC, dn = dC_ref[...], dn_ref[...]
        dcs, dns = [None] * G, [None] * G
        for g in reversed(range(G)):
            dcs[g], dns[g] = dC, dn
            dC = wo4[g] * dC + dcloc[g]
            dn = wo4[g] * dn + dnloc[g]
        dC_ref[...] = dC
        dn_ref[...] = dn
        dC3 = jnp.stack(dcs, axis=0).reshape(B, ML_DQK, 128)
        dn3 = jnp.stack(dns, axis=0).reshape(B, 1, ML_DQK)
        dk_st = ws3 * (_bdot(v3, dC3, BNT) + dn3)
        dq = _bdot(dqk, k3, BNN) + wn * (_bdot(dnum, C3, BNT) + dden * n3)
        dk = _bdot(dqk, q3, BTN) + dk_st
        dv = _bdot(s3, dnum, BTN) + ws3 * _bdot(k3, dC3, BNN)
        dv_ref[...] = _batch_to_heads(dv).astype(dv_ref.dtype)
        dqk = jnp.concatenate([_batch_to_heads(dq * ML_SCALE), _batch_to_heads(dk)], axis=1)
        pre = pre_ref[...]
        dpre = dqk * _dsilu(pre, _sig(pre))
        x = x_ref[...]
        xprev = jnp.where(pl.program_id(0) < NG - 1, xh_ref[...], 0.0)
        nxt = next_sc[...]
        row8 = _iota((8, HALF), 0)
        dx = dpre * cw_ref[3:4, :]
        dws = [None, None, None, _colsum(dpre * x)]
        for j in (1, 2, 3):
            dx = dx + _shift_rows_up(dpre, nxt, j, row8) * cw_ref[3 - j:4 - j, :]
            dws[3 - j] = _colsum(dpre * _shift_rows(x, xprev, j, row8))
        dmqk_ref[...] = dx.astype(dmqk_ref.dtype)
        dcw_ref[...] += jnp.concatenate(dws, axis=0)
        dcb_ref[...] += _colsum(dpre)
        next_sc[...] = dpre[:8]
        e_col = wn * (jnp.sum(dnum * r["qc"], axis=2, keepdims=True) + dden * r["qn"])
        c_col = jnp.sum(k3 * dk_st, axis=2, keepdims=True)
        z = wo3 * (jnp.sum(dC3 * C3, axis=(1, 2), keepdims=True) + jnp.sum(dn3 * n3, axis=(1, 2), keepdims=True))
        dd_hi = dd.astype(_MXU).astype(F32)
        ones = jnp.ones((B, CHUNK, 128), F32)
        dd_cols = (_bdot(dd_hi, ones, BTN) + _bdot(dd - dd_hi, ones, BTN))[:, :, 0:1]
        last = _iota((1, CHUNK, 1), 1) == CHUNK - 1
        dg3 = jnp.sum(dd, axis=2, keepdims=True) - dd_cols + e_col - c_col
        dg3 = dg3 + jnp.where(last, jnp.sum(c_col, axis=1, keepdims=True) + z, 0.0)
        di3 = dd_cols + c_col

        def to_lanes(x3, first):
            x4 = x3.reshape(G, HEADS, CHUNK, 1)
            return sum(jnp.where(lane == first + h, x4[:, h].reshape(R, 1), 0.0) for h in range(HEADS))

        dlf = _chunk_cumsum(to_lanes(dg3, 4), rowmod, reverse=True)
        dgt_ref[...] = (to_lanes(di3, 0) + dlf * _sig(-gates)).astype(dgt_ref.dtype)

    rev = lambda w, c: pl.BlockSpec((R, w), lambda i, c=c: (NG - 1 - i, c))
    st = lambda *s: pl.BlockSpec((G,) + s, lambda i: (NG - 1 - i,) + (0,) * len(s))
    halo = pl.BlockSpec((8, HALF), lambda i: (jnp.maximum((NG - 1 - i) * (R // 8) - 1, 0), C_MQK // 4))
    return pl.pallas_call(
        body, name="mlstm_bwd", grid=(NG,),
        in_specs=[rev(HALF, 0), rev(HALF, C_MV // 4), rev(128, C_GATES), rev(HALF, 0),
                  st(HEADS, ML_DQK, 128), st(HEADS, 1, ML_DQK), st(1, 128),
                  rev(HALF, 0), rev(HALF, C_MQK // 4), halo, _full(cw.shape)],
        out_specs=[rev(HALF, 0), rev(HALF, 0), rev(128, 0), _full((4, HALF)), _full((1, HALF))],
        out_shape=[_sds((T, HALF), _MXU), _sds((T, HALF), _MXU), _sds((T, 128), _MXU), _sds((4, HALF)), _sds((1, HALF))],
        scratch_shapes=[pltpu.VMEM((HEADS, ML_DQK, 128), F32), pltpu.VMEM((HEADS, 1, ML_DQK), F32),
                        pltpu.VMEM((8, HALF), F32)],
        compiler_params=_cp("arbitrary"))(qkc, u, u, dh, cst, nst, mst, pre, u, u, cw)


def _head_norm(o):
    rs_parts, r_parts = [], []
    for h in range(HEADS):
        oh = o[:, 128 * h:128 * (h + 1)]
        rs = lax.rsqrt(jnp.mean(oh * oh, axis=-1, keepdims=True) + RMS_EPS)
        rs_parts.append(rs)
        r_parts.append(oh * rs)
    return jnp.concatenate(r_parts, axis=1), rs_parts


def _out_proj(x, u, o_hg, h_ml, g_hg, g_ml, w_out, tm):
    T = x.shape[0]

    def body(x_ref, hgate_ref, mo_ref, ohg_ref, hml_ref, ghg_ref, gml_ref, w_ref, m_ref, z_ref):
        hgate = hgate_ref[...]
        a = _head_norm(ohg_ref[...])[0] * ghg_ref[...] * (hgate * _sig(hgate))
        b = _head_norm(hml_ref[...])[0] * gml_ref[...] * _sig(mo_ref[...])
        m = jnp.concatenate([a, b], axis=1)
        m_ref[...] = m.astype(m_ref.dtype)
        z_ref[...] = ALPHA * x_ref[...] + _dot(m, w_ref[...])

    return pl.pallas_call(
        body, name="out_proj", grid=(T // tm,),
        in_specs=[_row(tm, D_MODEL), _row(tm, HALF, C_HGATE // 4), _row(tm, HALF, C_MO // 4),
                  _row(tm, HALF), _row(tm, HALF), _full(g_hg.shape), _full(g_ml.shape), _full(w_out.shape)],
        out_specs=[_row(tm, D_MODEL)] * 2,
        out_shape=[_sds((T, D_MODEL), _MXU), _sds((T, D_MODEL))],
        compiler_params=_cp("parallel"))(x, u, u, o_hg, h_ml, g_hg, g_ml, w_out)


def _ffn_ln(z1, ln1_g, ln1_b, wg, wu, wd, ln_g, ln_b, tm):
    T = z1.shape[0]

    def body(z1_ref, g1_ref, b1_ref, wg_ref, wu_ref, wd_ref, g_ref, b_ref, z_ref, x1_ref, x2_ref, a_ref, bb_ref, h_ref):
        x = _ln_fwd(z1_ref[...], g1_ref[...], b1_ref[...])[0]
        x1_ref[...] = x.astype(x1_ref.dtype)
        a = _dot(x, wg_ref[...], NT)
        bb = _dot(x, wu_ref[...], NT)
        hh = a * _sig(a) * bb
        a_ref[...] = a.astype(a_ref.dtype)
        bb_ref[...] = bb.astype(bb_ref.dtype)
        h_ref[...] = hh.astype(h_ref.dtype)
        z = ALPHA * x + _dot(hh, wd_ref[...])
        z_ref[...] = z
        x2_ref[...] = _ln_fwd(z, g_ref[...], b_ref[...])[0].astype(x2_ref.dtype)

    vec = _full((1, D_MODEL))
    return pl.pallas_call(
        body, name="ffn_ln2", grid=(T // tm,),
        in_specs=[_row(tm, D_MODEL), vec, vec, _full(wg.shape), _full(wu.shape), _full(wd.shape), vec, vec],
        out_specs=[_row(tm, D_MODEL)] * 3 + [_row(tm, D_FF)] * 3,
        out_shape=[_sds((T, D_MODEL))] + [_sds((T, D_MODEL), _MXU)] * 2 + [_sds((T, D_FF), _MXU)] * 3,
        compiler_params=_cp("parallel"))(z1, ln1_g, ln1_b, wg, wu, wd, ln_g, ln_b)


def _ple_loss_ln2_bwd(z2, p, tgt, wpg, bpg, wpp, ln_g, ln_b, tm):
    T = z2.shape[0]

    def body(z_ref, p_ref, t_ref, wpg_ref, bpg_ref, wpp_ref, g_ref, b_ref,
             de_ref, dgp_ref, dz_ref, loss_ref, dbpg_ref, dg_ref, db_ref):
        @pl.when(pl.program_id(0) == 0)
        def _():
            for r in (loss_ref, dbpg_ref, dg_ref, db_ref):
                r[...] = jnp.zeros_like(r)

        x2, xhat, rstd = _ln_fwd(z_ref[...], g_ref[...], b_ref[...])
        gate = _sig(_dot(x2, wpg_ref[...]) + bpg_ref[...])
        e = _dot(p_ref[...], wpp_ref[...])
        err = x2 + gate * e - t_ref[...]
        loss_ref[...] += _colsum(err * err)
        dy = err * (1.0 / D_MODEL)
        de_ref[...] = (dy * gate).astype(de_ref.dtype)
        dgp = dy * e * gate * (1.0 - gate)
        dgp_ref[...] = dgp.astype(dgp_ref.dtype)
        dbpg_ref[...] += _colsum(dgp)
        dx2 = dy + _dot(dgp, wpg_ref[...], NT)
        dg_ref[...] += _colsum(dx2 * xhat)
        db_ref[...] += _colsum(dx2)
        dz_ref[...] = _ln_bwd(dx2, xhat, rstd, g_ref[...])

    vec = _full((1, D_MODEL))
    return pl.pallas_call(
        body, name="ple_loss_ln2_bwd", grid=(T // tm,),
        in_specs=[_row(tm, D_MODEL), _row(tm, PLE_DIM), _row(tm, D_MODEL),
                  _full(wpg.shape), vec, _full(wpp.shape), vec, vec],
        out_specs=[_row(tm, D_MODEL)] * 3 + [vec] * 4,
        out_shape=[_sds((T, D_MODEL), _MXU)] * 2 + [_sds((T, D_MODEL))] + [_sds((1, D_MODEL))] * 4,
        compiler_params=_cp("arbitrary"))(z2, p, tgt, wpg, bpg, wpp, ln_g, ln_b)


def _ffn_bwd_ln1_bwd(a_pre, b_pre, z1, dz2, wg, wu, wd, ln_g, ln_b, tm, comms=None):
    T = z1.shape[0]

    def body(a_ref, bb_ref, z_ref, dz2_ref, wg_ref, wu_ref, wd_ref, g_ref, b_ref,
             da_ref, dbb_ref, dz1_ref, dg_ref, db_ref):
        @pl.when(pl.program_id(0) == 0)
        def _():
            dg_ref[...] = jnp.zeros_like(dg_ref)
            db_ref[...] = jnp.zeros_like(db_ref)

        dz2 = dz2_ref[...]
        a = a_ref[...].astype(F32)
        bb = bb_ref[...].astype(F32)
        sa = _sig(a)
        act = a * sa
        dh = _dot(dz2, wd_ref[...], NT)
        da = (dh * bb * _dsilu(a, sa)).astype(da_ref.dtype)
        dbb = (dh * act).astype(dbb_ref.dtype)
        da_ref[...] = da
        dbb_ref[...] = dbb
        dx1 = ALPHA * dz2 + _dot(da, wg_ref[...]) + _dot(dbb, wu_ref[...])
        _, xhat, rstd = _ln_fwd(z_ref[...], g_ref[...], b_ref[...])
        dg_ref[...] += _colsum(dx1 * xhat)
        db_ref[...] += _colsum(dx1)
        dz1_ref[...] = _ln_bwd(dx1, xhat, rstd, g_ref[...])

    vec = _full((1, D_MODEL))
    return _hosted_call(
        body, comms, name="ffn_bwd_ln1_bwd", grid=(T // tm,),
        in_specs=[_row(tm, D_FF)] * 2 + [_row(tm, D_MODEL)] * 2 + [_full(wg.shape), _full(wu.shape), _full(wd.shape), vec, vec],
        out_specs=[_row(tm, D_FF)] * 2 + [_row(tm, D_MODEL), vec, vec],
        out_shape=[_sds((T, D_FF), _MXU)] * 2 + [_sds((T, D_MODEL)), _sds((1, D_MODEL)), _sds((1, D_MODEL))],
        scratch_shapes=[], args=(a_pre, b_pre, z1, dz2, wg, wu, wd, ln_g, ln_b))


def _out_proj_bwd(dz1, u, o_hg, h_ml, g_hg, g_ml, w_out, tm):
    T = dz1.shape[0]

    def body(dz_ref, hgate_ref, mo_ref, ohg_ref, hml_ref, ghg_ref, gml_ref, w_ref,
             dohg_ref, dhml_ref, dhgate_ref, dmo_ref, dghg_ref, dgml_ref):
        @pl.when(pl.program_id(0) == 0)
        def _():
            dghg_ref[...] = jnp.zeros_like(dghg_ref)
            dgml_ref[...] = jnp.zeros_like(dgml_ref)

        dm = _dot(dz_ref[...], w_ref[...], NT)

        def half(dmh, o, gvec, gate_val, dgate_fac, do_ref, dgate_ref, dgvec_ref):
            r, rs = _head_norm(o)
            dgate_ref[...] = (dmh * r * gvec * dgate_fac).astype(dgate_ref.dtype)
            dn = dmh * gate_val
            dgvec_ref[...] += _colsum(dn * r)
            dr = dn * gvec
            parts = []
            for h in range(HEADS):
                sl = slice(128 * h, 128 * (h + 1))
                parts.append(rs[h] * (dr[:, sl] - r[:, sl] * jnp.mean(dr[:, sl] * r[:, sl], axis=-1, keepdims=True)))
            do_ref[...] = jnp.concatenate(parts, axis=1)

        hg = hgate_ref[...]
        shg = _sig(hg)
        half(dm[:, :HALF], ohg_ref[...], ghg_ref[...], hg * shg, _dsilu(hg, shg), dohg_ref, dhgate_ref, dghg_ref)
        smo = _sig(mo_ref[...])
        half(dm[:, HALF:], hml_ref[...], gml_ref[...], smo, smo * (1.0 - smo), dhml_ref, dmo_ref, dgml_ref)

    vec = _full((1, HALF))
    return pl.pallas_call(
        body, name="out_proj_bwd", grid=(T // tm,),
        in_specs=[_row(tm, D_MODEL), _row(tm, HALF, C_HGATE // 4), _row(tm, HALF, C_MO // 4),
                  _row(tm, HALF), _row(tm, HALF), vec, vec, _full(w_out.shape)],
        out_specs=[_row(tm, HALF)] * 4 + [vec, vec],
        out_shape=[_sds((T, HALF))] * 2 + [_sds((T, HALF), _MXU)] * 2 + [_sds((1, HALF))] * 2,
        compiler_params=_cp("arbitrary"))(dz1, u, u, o_hg, h_ml, g_hg, g_ml, w_out)


def _du_specs(rows):
    return [pl.BlockSpec((rows, w), lambda i: (i, 0)) for w in DU_WIDTHS]


def _in_proj_bwd(dz1, du_parts, w, tm, comms=None):
    T = dz1.shape[0]

    def body(dz_ref, *refs):
        du = jnp.concatenate([r[...] for r in refs[:8]], axis=1)
        refs[9][...] = ALPHA * dz_ref[...] + _dot(du, refs[8][...], NT)

    (dx,), got = _hosted_call(
        body, comms, name="in_proj_bwd", grid=(T // tm,),
        in_specs=[_row(tm, D_MODEL)] + _du_specs(tm) + [_full(w.shape)],
        out_specs=[_row(tm, D_MODEL)], out_shape=[_sds((T, D_MODEL))], scratch_shapes=[], args=(dz1, *du_parts, w))
    return dx, got


def _wgrad(a, b, name, tm, tn, tk):
    T, M = a.shape
    N = b.shape[1]
    tm, tn, tk = min(tm, M), min(tn, N), min(tk, T)
    nk = T // tk

    def body(a_ref, b_ref, o_ref, acc_ref):
        kk = pl.program_id(2)

        @pl.when(kk == 0)
        def _():
            acc_ref[...] = jnp.zeros_like(acc_ref)

        acc_ref[...] += _dot(a_ref[...], b_ref[...], TN)

        @pl.when(kk == nk - 1)
        def _():
            o_ref[...] = acc_ref[...].astype(o_ref.dtype)

    return pl.pallas_call(
        body, name=name, grid=(M // tm, N // tn, nk),
        in_specs=[pl.BlockSpec((tk, tm), lambda i, j, kk: (kk, i)), pl.BlockSpec((tk, tn), lambda i, j, kk: (kk, j))],
        out_specs=pl.BlockSpec((tm, tn), lambda i, j, kk: (i, j)), out_shape=_sds((M, N), _MXU),
        scratch_shapes=[pltpu.VMEM((tm, tn), F32)],
        compiler_params=_cp("parallel", "parallel", "arbitrary"))(a, b)


W_IN_TAIL = 256


def _wgrad_w_in(x, du_parts, tk, M, blk, name, comms=None):
    T = x.shape[0]
    tk = min(tk, T)
    nk = T // tk

    def body(a_ref, *refs):
        o_ref, cs_ref, acc_ref = refs[8:]
        kk = pl.program_id(0)

        @pl.when(kk == 0)
        def _():
            acc_ref[...] = jnp.zeros_like(acc_ref)
            cs_ref[...] = jnp.zeros_like(cs_ref)

        du = jnp.concatenate([r[...] for r in refs[:8]], axis=1)
        acc_ref[...] += _dot(a_ref[...], du, TN)
        cs_ref[...] += _colsum(du.astype(F32))

        @pl.when(kk == nk - 1)
        def _():
            o_ref[...] = acc_ref[...].astype(o_ref.dtype)

    return _hosted_call(
        body, comms, name=name, grid=(nk,),
        in_specs=[pl.BlockSpec((tk, M), lambda kk: (kk, blk))] + _du_specs(tk),
        out_specs=[_full((M, PROJ_WP)), _full((1, PROJ_WP))],
        out_shape=[_sds((M, PROJ_WP), _MXU), _sds((1, PROJ_WP))],
        scratch_shapes=[pltpu.VMEM((M, PROJ_WP), F32)], args=(x, *du_parts))


W_IN_S, FF_S, OUT_S, PP_S = PROJ_W // N_DEV, D_FF // N_DEV, D_MODEL // N_DEV, D_MODEL // N_DEV
LATE = ("w_ffn_gate", "w_ffn_up", "w_out", "w_ffn_down", "ple_w_gate", "ple_w_proj")
BIG = ("w_in",) + LATE
TRANSPOSED = ("w_ffn_gate", "w_ffn_up")


def _split_cols(a, n):
    return a.reshape(a.shape[0], N_DEV, n).transpose(1, 0, 2)


def _join_cols(a):
    return a.transpose(1, 0, 2).reshape(a.shape[1], -1)


def _step(x, p, tgt, w_in, b_in, lb_logits, conv_w, conv_b, g_hg, g_ml, ln1_g, ln1_b, ln2_g, ln2_b, bpg, late,
          distributed):
    T = x.shape[0]
    tm, tf = min(ROWS, T), min(ROWS_FFN, T)
    gather = lambda *names: [_GatherTwoLevel([late[n] for n in names])] if distributed else None
    scatter = lambda *arrs: [_Comm("scatter", list(arrs))] if distributed else None
    rows = lambda a, n: a.reshape(N_DEV, n, D_MODEL)
    (u, pre, qkc), got1 = _in_proj(x, w_in, b_in, conv_w, conv_b, tm, gather("w_out", "ple_w_gate", "ple_w_proj"))
    (o_hg, hg_states), got2 = _hgrn2_fwd(u, lb_logits, gather("w_ffn_gate", "w_ffn_up"))
    (h_ml, cst, nst, mst), got3 = _mlstm_fwd(qkc, u, gather("w_ffn_down"))
    if distributed:
        w_out, wpg, wpp = got1[0][0].reshape(D_MODEL, D_MODEL), got1[0][1].reshape(D_MODEL, D_MODEL), _join_cols(got1[0][2])
        wg, wu, wd = (a.reshape(D_FF, D_MODEL) for a in (got2[0][0], got2[0][1], got3[0][0]))
    else:
        w_out, wg, wu, wd, wpg, wpp = (late[n] for n in ("w_out", "w_ffn_gate", "w_ffn_up", "w_ffn_down", "ple_w_gate", "ple_w_proj"))
    m_in, z1 = _out_proj(x, u, o_hg, h_ml, g_hg, g_ml, w_out, tm)
    z2, x1, x2, a_pre, b_pre, hh = _ffn_ln(z1, ln1_g, ln1_b, wg, wu, wd, ln2_g, ln2_b, tf)
    de, dgp, dz2, loss_vec, d_bpg, d_ln2g, d_ln2b = _ple_loss_ln2_bwd(z2, p, tgt, wpg, bpg, wpp, ln2_g, ln2_b, tm)
    big = dict(ple_w_gate=_wgrad(x2, dgp, "wgrad_ple_gate", 512, D_MODEL, 1024),
               ple_w_proj=_wgrad(p, de, "wgrad_ple_proj", 512, D_MODEL, 1024))
    (da, dbb, dz1, d_ln1g, d_ln1b), r1 = _ffn_bwd_ln1_bwd(
        a_pre, b_pre, z1, dz2, wg, wu, wd, ln1_g, ln1_b, tf, scatter(rows(big["ple_w_gate"], OUT_S), _split_cols(big["ple_w_proj"], PP_S)))
    big.update(
        w_ffn_gate=_wgrad(da, x1, "wgrad_ffn_gate", D_FF, D_MODEL, 1024),
        w_ffn_up=_wgrad(dbb, x1, "wgrad_ffn_up", D_FF, D_MODEL, 1024),
        w_ffn_down=_wgrad(hh, dz2, "wgrad_ffn_down", D_FF, D_MODEL, 1024),
        w_out=_wgrad(m_in, dz1, "wgrad_w_out", 512, D_MODEL, 1024))
    d_ohg, d_hml, d_hgate, d_mo, d_ghg, d_gml = _out_proj_bwd(dz1, u, o_hg, h_ml, g_hg, g_ml, w_out, tm)
    (d_hq, d_hf, d_hv, d_lb), r2 = _hgrn2_bwd(
        u, lb_logits, d_ohg, hg_states,
        scatter(rows(big["w_ffn_gate"], FF_S), rows(big["w_ffn_up"], FF_S), rows(big["w_ffn_down"], FF_S),
                rows(big["w_out"], OUT_S)))
    d_mqk, d_mv, d_gates, d_convw, d_convb = _mlstm_bwd(qkc, u, d_hml, cst, nst, mst, pre, conv_w)
    du_parts = [d_hq, d_hf, d_hv, d_hgate, d_mqk, d_mv, d_mo, d_gates]
    own = lambda g: _split_cols(g[:, :PROJ_W], W_IN_S)
    main = D_MODEL - W_IN_TAIL
    (g_tail, d_bin), _ = _wgrad_w_in(x, du_parts, 512, W_IN_TAIL, main // W_IN_TAIL, "wgrad_w_in_tail")
    (g_main, _), r_tail = _wgrad_w_in(x, du_parts, 512, main, 0, "wgrad_w_in_main", scatter(own(g_tail)))
    big["w_in"] = jnp.concatenate([g_main, g_tail], axis=0)
    small = dict(b_in=d_bin[:, :PROJ_W], hg_lb_logits=d_lb, ml_conv_w=d_convw, ml_conv_b=d_convb, hg_norm_g=d_ghg,
                 ml_norm_g=d_gml, ln1_g=d_ln1g, ln1_b=d_ln1b, ln2_g=d_ln2g, ln2_b=d_ln2b, ple_b_gate=d_bpg)
    last = [_Comm("scatter", [own(g_main)]), _Comm("gather", [loss_vec] + [small[n] for n in SMALL])] if distributed else None
    dx, r3 = _in_proj_bwd(dz1, du_parts, w_in, tm, last)
    gathered_small = None
    if distributed:
        big = dict(ple_w_gate=r1[0][0], ple_w_proj=r1[0][1], w_ffn_gate=r2[0][0], w_ffn_up=r2[0][1],
                   w_ffn_down=r2[0][2], w_out=r2[0][3], w_in=[r3[0][0], r_tail[0][0]])
        gathered_small = r3[1]
    return loss_vec, dx, big, small, gathered_small


SMALL = ("b_in", "hg_lb_logits", "ml_conv_w", "ml_conv_b", "hg_norm_g", "ml_norm_g", "ln1_g", "ln1_b", "ln2_g", "ln2_b",
         "ple_b_gate")


def _padc(a, n):
    return jnp.pad(a, [(0, 0)] * (a.ndim - 1) + [(0, n - a.shape[-1])])


def _adamw(w, g, m, v):
    m = B1 * m + (1.0 - B1) * g
    v = B2 * v + (1.0 - B2) * jnp.square(g)
    m_hat = m / (1.0 - B1 ** STEP)
    v_hat = v / (1.0 - B2 ** STEP)
    return -LR * (m_hat / (jnp.sqrt(v_hat) + EPS) + WD * w), m, v


def _sum_slabs(ref):
    g = ref[0].astype(F32)
    for j in range(1, N_DEV):
        g = g + ref[j].astype(F32)
    return g


def _adamw_matrix(rbs, w, m, v, name):
    rbs = list(rbs) if isinstance(rbs, (list, tuple)) else [rbs]
    nb = len(rbs)
    R, C = w.shape
    tr = 256 if all(rb.shape[1] % 256 == 0 for rb in rbs) else R
    tiles = [rb.shape[1] // tr for rb in rbs]
    first = [sum(tiles[:k]) for k in range(nb)]

    def body(*refs):
        w_ref, m_ref, v_ref, g_ref, d_ref, m2_ref, v2_ref = refs[nb:]
        i = pl.program_id(0)
        g = _sum_slabs(refs[0])
        for k in range(1, nb):
            g = jnp.where(i >= first[k], _sum_slabs(refs[k]), g)
        g_ref[...] = g
        d_ref[...], m2_ref[...], v2_ref[...] = _adamw(w_ref[...], g, m_ref[...], v_ref[...])

    blk = pl.BlockSpec((tr, C), lambda i: (i, 0))
    part = lambda k: pl.BlockSpec((N_DEV, tr, C), lambda i, k=k: (0, jnp.clip(i - first[k], 0, tiles[k] - 1), 0))
    return pl.pallas_call(
        body, name=name, grid=(R // tr,),
        in_specs=[part(k) for k in range(nb)] + [blk, blk, blk],
        out_specs=[blk] * 4, out_shape=[_sds((R, C))] * 4, compiler_params=_cp("parallel"))(*rbs, w, m, v)


def _adamw_small(loss_g, gs, ws, ms, vs):
    n = len(ws)

    def body(*refs):
        loss_ref, g_refs, w_refs, m_refs, v_refs = refs[0], refs[1:1 + n], refs[1 + n:1 + 2 * n], refs[1 + 2 * n:1 + 3 * n], refs[1 + 3 * n:1 + 4 * n]
        outs = refs[1 + 4 * n:]
        outs[0][...] = (0.5 / D_MODEL) * jnp.sum(_sum_slabs(loss_ref), keepdims=True)
        for i in range(n):
            g = _sum_slabs(g_refs[i])
            outs[1 + i][...] = g
            outs[1 + n + i][...], outs[1 + 2 * n + i][...], outs[1 + 3 * n + i][...] = _adamw(
                w_refs[i][...], g, m_refs[i][...], v_refs[i][...])

    res = pl.pallas_call(
        body, name="adamw_small", out_shape=[_sds((1, 1))] + [_sds(w.shape) for w in ws] * 4)(loss_g, *gs, *ws, *ms, *vs)
    return res[0], [res[1 + k * n:1 + (k + 1) * n] for k in range(4)]


WEIGHTS = ("w_in", "b_in", "hg_lb_logits", "ml_conv_w", "ml_conv_b", "hg_norm_g", "ml_norm_g", "w_out", "ln1_g", "ln1_b",
           "w_ffn_gate", "w_ffn_up", "w_ffn_down", "ln2_g", "ln2_b", "ple_w_proj", "ple_w_gate", "ple_b_gate")
CONV_S = HALF // N_DEV


def kernel(x, p, w_in, b_in, hg_lb_logits, ml_conv_w, ml_conv_b, hg_norm_g, ml_norm_g, w_out, ln1_g, ln1_b, w_ffn_gate, w_ffn_up, w_ffn_down, ln2_g, ln2_b, ple_w_proj, ple_w_gate, ple_b_gate, loss_target, m_w_in, m_b_in, m_hg_lb_logits, m_ml_conv_w, m_ml_conv_b, m_hg_norm_g, m_ml_norm_g, m_w_out, m_ln1_g, m_ln1_b, m_w_ffn_gate, m_w_ffn_up, m_w_ffn_down, m_ln2_g, m_ln2_b, m_ple_w_proj, m_ple_w_gate, m_ple_b_gate, v_w_in, v_b_in, v_hg_lb_logits, v_ml_conv_w, v_ml_conv_b, v_hg_norm_g, v_ml_norm_g, v_w_out, v_ln1_g, v_ln1_b, v_w_ffn_gate, v_w_ffn_up, v_w_ffn_down, v_ln2_g, v_ln2_b, v_ple_w_proj, v_ple_w_gate, v_ple_b_gate):
    args = locals()
    me = 4 * lax.axis_index("x") + 2 * lax.axis_index("y") + lax.axis_index("c")
    shapes = {n: args[n].shape for n in WEIGHTS}
    def drop(n, a):
        a = a[0] if n in BIG or n == "ml_conv_w" else a
        return a.T if n in TRANSPOSED else a

    W = {n: drop(n, args[n]) for n in WEIGHTS}
    M = {n: drop(n, args["m_" + n]) for n in WEIGHTS}
    V = {n: drop(n, args["v_" + n]) for n in WEIGHTS}

    g_in, g_conv = _gather_two_level(
        [W["w_in"].astype(_MXU), jnp.pad(W["ml_conv_w"], ((0, 4), (0, 128 - CONV_S)))], "gather_w_in")
    w_in_full = _padc(_join_cols(g_in), PROJ_WP)
    conv_full = _join_cols(g_conv[:, :4, :CONV_S])

    _, dx, big, _, sg = _step(
        x[0], p[0, 0], loss_target[0], w_in_full, _padc(b_in, PROJ_WP), hg_lb_logits, conv_full, ml_conv_b,
        hg_norm_g, ml_norm_g, ln1_g, ln1_b, ln2_g, ln2_b, ple_b_gate, {n: W[n].astype(_MXU) for n in LATE}, True)

    upd = {n: _adamw_matrix(big[n], W[n], M[n], V[n], "adamw_" + n) for n in BIG}
    sg = dict(zip(SMALL, sg[1:]), loss=sg[0])
    sg["ml_conv_w"] = lax.dynamic_slice(sg["ml_conv_w"], (0, 0, me * CONV_S), (N_DEV, 4, CONV_S))
    loss, small_upd = _adamw_small(sg["loss"], *[[d[n] for n in SMALL] for d in (sg, W, M, V)])

    outs = []
    for kind in range(4):
        smalls = dict(zip(SMALL, small_upd[kind]))
        for n in WEIGHTS:
            o = upd[n][kind] if n in BIG else smalls[n]
            outs.append((o.T if n in TRANSPOSED else o).reshape(shapes[n]))
    return (loss.reshape(()), dx.reshape(x.shape), *outs)
```

```python
import jax
import jax.numpy as jnp
from jax import lax
from jax.experimental import pallas as pl
from jax.experimental.pallas import tpu as pltpu

F32 = jnp.float32
_MXU = jnp.bfloat16

D_MODEL = 1024
CHUNK = 64
SUB = 16
PLE_DIM = 256
HEADS = 4
ML_DQK = 64
HALF = 512
D_FF = 2816
PROJ_W = 3592
PROJ_WP = 3712
ALPHA = float(2 ** 0.25)
LN_EPS = 1e-5
RMS_EPS = 1e-6
ML_SCALE = ML_DQK ** -0.5
N_DEV = 8
LR, B1, B2, EPS, WD, STEP = 0.001, 0.9, 0.999, 1e-08, 0.01, 10
NEG = -1e30
LOG2E = 1.4426950408889634

C_HQ, C_HF, C_HV, C_HGATE, C_MQK, C_MV, C_MO, C_GATES = 0, 4, 8, 12, 16, 20, 24, 28
DU_WIDTHS = (HALF,) * 7 + (128,)

VMEM_LIMIT = 52 * 1024 * 1024
GC = 8
ROWS = 512
ROWS_FFN = 256

NN = (((1,), (0,)), ((), ()))
NT = (((1,), (1,)), ((), ()))
TN = (((0,), (0,)), ((), ()))
BNT = (((2,), (2,)), ((0,), (0,)))
BNN = (((2,), (1,)), ((0,), (0,)))
BTN = (((1,), (1,)), ((0,), (0,)))


def _dot(a, b, dims=NN):
    return lax.dot_general(a.astype(_MXU), b.astype(_MXU), dims, preferred_element_type=F32)


def _dotx(a, b, dims=NN):
    return lax.dot_general(a, b, dims, precision=lax.Precision.HIGHEST, preferred_element_type=F32)


def _sig(x):
    return jax.nn.sigmoid(x)


def _cp(*sem):
    return pltpu.CompilerParams(dimension_semantics=sem, vmem_limit_bytes=VMEM_LIMIT)


def _row(tm, c, blk=0):
    return pl.BlockSpec((tm, c), lambda i, blk=blk: (i, blk))


def _full(shape):
    nd = len(shape)
    return pl.BlockSpec(tuple(shape), lambda *_, nd=nd: (0,) * nd)


def _sds(shape, dtype=F32):
    return jax.ShapeDtypeStruct(tuple(shape), dtype)


def _iota(shape, axis):
    return lax.broadcasted_iota(jnp.int32, shape, axis)


def _colsum(x):
    return jnp.sum(x, axis=0, keepdims=True)


def _rowsum(x):
    return jnp.sum(x, axis=1, keepdims=True)


def _ln_fwd(z, g, b):
    mu = jnp.mean(z, axis=-1, keepdims=True)
    zc = z - mu
    var = jnp.mean(zc * zc, axis=-1, keepdims=True)
    rstd = lax.rsqrt(var + LN_EPS)
    xhat = zc * rstd
    return xhat * g + b, xhat, rstd


def _ln_bwd(dy, xhat, rstd, g):
    dxh = dy * g
    m1 = jnp.mean(dxh, axis=-1, keepdims=True)
    m2 = jnp.mean(dxh * xhat, axis=-1, keepdims=True)
    return rstd * (dxh - m1 - xhat * m2)


def _dsilu(x, s):
    return s * (1.0 + x * (1.0 - s))


MESH = pl.DeviceIdType.MESH
ANY = pl.BlockSpec(memory_space=pl.ANY)


def _flip(v, bit):
    return 1 - v if bit else v


class _Comm:
    def __init__(self, kind, srcs):
        self.kind, self.srcs, self.n = kind, list(srcs), len(srcs)

    def out_shape(self):
        lead = (N_DEV,) if self.kind == "gather" else ()
        return [jax.ShapeDtypeStruct(lead + s.shape, s.dtype) for s in self.srcs]

    def scratch(self):
        return [pltpu.SemaphoreType.DMA((7 * self.n,)), pltpu.SemaphoreType.DMA((7 * self.n,)),
                pltpu.SemaphoreType.DMA((self.n,))]

    def copies(self, srcs, dsts, send_sems, recv_sems, local_sems):
        x, y, c = lax.axis_index("x"), lax.axis_index("y"), lax.axis_index("c")
        me = 4 * x + 2 * y + c
        pick = (lambda s, j: s) if self.kind == "gather" else (lambda s, j: s.at[j])
        out = []
        for i, (s, d) in enumerate(zip(srcs, dsts)):
            out.append(pltpu.make_async_copy(pick(s, me), d.at[me], local_sems.at[i]))
            for k in range(1, N_DEV):
                px, py, pc = _flip(x, k & 4), _flip(y, k & 2), _flip(c, k & 1)
                out.append(pltpu.make_async_remote_copy(
                    src_ref=pick(s, 4 * px + 2 * py + pc), dst_ref=d.at[me], send_sem=send_sems.at[7 * i + k - 1],
                    recv_sem=recv_sems.at[7 * i + k - 1], device_id=(px, py, pc), device_id_type=MESH))
        return out

    def start(self, *refs):
        for cp in self.copies(*refs):
            cp.start()

    def mid(self, *refs):
        pass

    def finish(self, *refs):
        for cp in self.copies(*refs):
            cp.wait()


class _GatherTwoLevel(_Comm):
    def __init__(self, srcs):
        super().__init__("gather", srcs)

    def _parts(self, srcs, dsts, send_sems, recv_sems, local_sems):
        x, y, c = lax.axis_index("x"), lax.axis_index("y"), lax.axis_index("c")
        me, sibling = (x, y, c), (x, y, 1 - c)
        chips = [(1 - x, y), (x, 1 - y), (1 - x, 1 - y)]

        def copy(i, k, block, to, own=False):
            slab = dsts[i].at[4 * block[0] + 2 * block[1] + block[2]]
            return pltpu.make_async_remote_copy(
                src_ref=srcs[i] if own else slab, dst_ref=slab, send_sem=send_sems.at[7 * i + k],
                recv_sem=recv_sems.at[7 * i + k], device_id=to, device_id_type=MESH)

        n = range(self.n)
        mine = [pltpu.make_async_copy(srcs[i], dsts[i].at[4 * x + 2 * y + c], local_sems.at[i]) for i in n]
        first = [copy(i, 0, me, sibling, own=True) for i in n]
        first += [copy(i, 1 + j, me, (*chip, c), own=True) for j, chip in enumerate(chips) for i in n]
        over_ici = [copy(i, 1 + j, (*chip, c), me) for j, chip in enumerate(chips) for i in n]
        passed = [copy(i, 4 + j, (*chip, c), sibling) for j, chip in enumerate(chips) for i in n]
        from_sibling = [copy(i, 0, sibling, me) for i in n]
        from_sibling += [copy(i, 4 + j, (*chip, 1 - c), me) for j, chip in enumerate(chips) for i in n]
        return mine, first, over_ici, passed, from_sibling

    def start(self, *refs):
        mine, first, _, _, _ = self._parts(*refs)
        for cp in mine + first:
            cp.start()

    def mid(self, *refs):
        _, _, over_ici, passed, _ = self._parts(*refs)
        for arrived, onward in zip(over_ici, passed):
            arrived.wait_recv()
            onward.start()

    def finish(self, *refs):
        mine, first, _, passed, from_sibling = self._parts(*refs)
        for cp in from_sibling:
            cp.wait_recv()
        for cp in first + passed:
            cp.wait_send()
        for cp in mine:
            cp.wait()


def _hosted_call(body, comms, *, name, grid, in_specs, out_specs, out_shape, scratch_shapes, args):
    comms = list(comms or [])
    if not comms:
        res = pl.pallas_call(body, name=name, grid=grid, in_specs=in_specs, out_specs=out_specs, out_shape=out_shape,
                             scratch_shapes=scratch_shapes, compiler_params=_cp("arbitrary"))(*args)
        return list(res), []
    n_in, n_out, n_sc, nc = len(in_specs), len(out_specs), len(scratch_shapes), sum(cm.n for cm in comms)
    last = grid[0] - 1

    def hosted(*refs):
        ins, csrc = refs[:n_in], refs[n_in:n_in + nc]
        o0 = n_in + nc
        outs, cdst = refs[o0:o0 + n_out], refs[o0 + n_out:o0 + n_out + nc]
        s0 = o0 + n_out + nc
        scr, sems = refs[s0:s0 + n_sc], refs[s0 + n_sc:]

        def phase(which):
            o = 0
            for j, cm in enumerate(comms):
                getattr(cm, which)(csrc[o:o + cm.n], cdst[o:o + cm.n], *sems[3 * j:3 * j + 3])
                o += cm.n

        i = pl.program_id(0)

        @pl.when(i == 0)
        def _():
            phase("start")

        body(*ins, *outs, *scr)

        @pl.when(i == (2 * last) // 3)
        def _():
            phase("mid")

        @pl.when(i == last)
        def _():
            phase("finish")

    res = pl.pallas_call(
        hosted, name=name, grid=grid, in_specs=list(in_specs) + [ANY] * nc, out_specs=list(out_specs) + [ANY] * nc,
        out_shape=list(out_shape) + [s for cm in comms for s in cm.out_shape()],
        scratch_shapes=list(scratch_shapes) + [s for cm in comms for s in cm.scratch()],
        compiler_params=_cp("arbitrary"))(*args, *[a for cm in comms for a in cm.srcs])
    got, o = [], n_out
    for cm in comms:
        got.append(list(res[o:o + cm.n]))
        o += cm.n
    return list(res[:n_out]), got


def _gather_two_level(blocks, name):
    n = len(blocks)

    def body(*refs):
        x_refs, out_refs = refs[:n], refs[n:2 * n]
        send_sems, recv_sems, local_sems = refs[2 * n:]
        x, y, c = lax.axis_index("x"), lax.axis_index("y"), lax.axis_index("c")
        me, sibling = (x, y, c), (x, y, 1 - c)
        chips = [(1 - x, y), (x, 1 - y), (1 - x, 1 - y)]

        def copy(i, k, block, to, own=False):
            slab = out_refs[i].at[4 * block[0] + 2 * block[1] + block[2]]
            return pltpu.make_async_remote_copy(
                src_ref=x_refs[i] if own else slab, dst_ref=slab, send_sem=send_sems.at[7 * i + k],
                recv_sem=recv_sems.at[7 * i + k], device_id=to, device_id_type=MESH)

        mine = [pltpu.make_async_copy(x_refs[i], out_refs[i].at[4 * x + 2 * y + c], local_sems.at[i]) for i in range(n)]
        for cp in mine:
            cp.start()
        first = [copy(i, 0, me, sibling, own=True) for i in range(n)]
        first += [copy(i, 1 + j, me, (*chip, c), own=True) for j, chip in enumerate(chips) for i in range(n)]
        for cp in first:
            cp.start()
        passed = []
        for j, chip in enumerate(chips):
            for i in range(n):
                copy(i, 1 + j, (*chip, c), me).wait_recv()
                passed.append(copy(i, 4 + j, (*chip, c), sibling))
                passed[-1].start()
        for i in range(n):
            copy(i, 0, sibling, me).wait_recv()
            for j, chip in enumerate(chips):
                copy(i, 4 + j, (*chip, 1 - c), me).wait_recv()
        for cp in first + passed:
            cp.wait_send()
        for cp in mine:
            cp.wait()

    return pl.pallas_call(
        body, name=name, out_shape=[jax.ShapeDtypeStruct((N_DEV,) + b.shape, b.dtype) for b in blocks],
        in_specs=[ANY] * n, out_specs=[ANY] * n,
        scratch_shapes=[pltpu.SemaphoreType.DMA((7 * n,)), pltpu.SemaphoreType.DMA((7 * n,)),
                        pltpu.SemaphoreType.DMA((n,))])(*blocks)


def _in_proj(x, w, b, cw, cb, tm, comms=None):
    T = x.shape[0]

    def body(x_ref, w_ref, b_ref, cw_ref, cb_ref, o_ref, pre_ref, act_ref, halo_sc):
        @pl.when(pl.program_id(0) == 0)
        def _():
            halo_sc[...] = jnp.zeros_like(halo_sc)

        o = _dot(x_ref[...], w_ref[...]) + b_ref[...]
        o_ref[...] = o
        xc = o[:, 128 * C_MQK:128 * C_MQK + HALF]
        halo = halo_sc[...]
        rowi = _iota((8, HALF), 0)
        acc = xc * cw_ref[3:4, :] + cb_ref[...]
        for j in (1, 2, 3):
            acc = acc + _shift_rows(xc, halo, j, rowi) * cw_ref[3 - j:4 - j, :]
        pre_ref[...] = acc
        act_ref[...] = acc * _sig(acc)
        halo_sc[...] = xc[tm - 8:]

    return _hosted_call(
        body, comms, name="in_proj", grid=(T // tm,),
        in_specs=[_row(tm, D_MODEL), _full(w.shape), _full(b.shape), _full(cw.shape), _full(cb.shape)],
        out_specs=[_row(tm, PROJ_WP), _row(tm, HALF), _row(tm, HALF)],
        out_shape=[_sds((T, PROJ_WP)), _sds((T, HALF)), _sds((T, HALF))],
        scratch_shapes=[pltpu.VMEM((8, HALF), F32)], args=(x, w, b, cw, cb))


def _shift_rows(x, halo, j, rowi):
    r = pltpu.roll(x, j, 0)
    top = jnp.where(rowi < j, pltpu.roll(halo, j, 0), r[:8])
    return jnp.concatenate([top, r[8:]], axis=0)


def _shift_rows_up(x, halo, j, rowi):
    n = x.shape[0]
    r = pltpu.roll(x, n - j, 0)
    bot = jnp.where(rowi >= 8 - j, pltpu.roll(halo, 8 - j, 0), r[n - 8:])
    return jnp.concatenate([r[:n - 8], bot], axis=0)


def _bdot(a, b, dims):
    return lax.dot_general(a.astype(_MXU), b.astype(_MXU), dims, preferred_element_type=F32)


def _bdotx(a, b, dims):
    return lax.dot_general(a, b, dims, precision=lax.Precision.HIGHEST, preferred_element_type=F32)


def _heads_to_batch(x, w):
    G = x.shape[0] // CHUNK
    x3 = x.reshape(G, CHUNK, HEADS * w)
    return jnp.stack([x3[:, :, w * h:w * (h + 1)] for h in range(HEADS)], axis=1).reshape(G * HEADS, CHUNK, w)


def _batch_to_heads(x3):
    B, _, w = x3.shape
    x4 = x3.reshape(B // HEADS, HEADS, CHUNK, w)
    return jnp.concatenate([x4[:, h] for h in range(HEADS)], axis=-1).reshape(B // HEADS * CHUNK, HEADS * w)


def _chunk_cumsum(x, rowmod, reverse=False):
    R = x.shape[0]
    for sh in (1, 2, 4, 8, 16, 32):
        if reverse:
            x = x + jnp.where(rowmod < CHUNK - sh, pltpu.roll(x, R - sh, 0), 0.0)
        else:
            x = x + jnp.where(rowmod >= sh, pltpu.roll(x, sh, 0), 0.0)
    return x


def _lane_col(x, c, lane):
    return _rowsum(jnp.where(lane == c, x, 0.0))


def _hg_gates(hq, hf, lb):
    sg = _sig(hf)
    nsg = _sig(-hf)
    f = lb + (1.0 - lb) * sg
    g = jnp.log(f)
    k = (1.0 - lb) * nsg
    sq = _sig(hq)
    return hq * sq, g, k, f, sg, nsg, sq


def _hg_prep(hq_ref, hf_ref, lg_ref, b_sc, k_sc):
    R = hq_ref.shape[0]
    G = R // CHUNK
    lb = _sig(lg_ref[0:1, :] - lg_ref[1:2, :])
    hq = hq_ref[...]
    q, g, k, f, sg, nsg, sq = _hg_gates(hq, hf_ref[...], lb)
    rowmod = _iota((R, HALF), 0) & (CHUNK - 1)
    b = _chunk_cumsum(g, rowmod) * LOG2E
    last8 = _iota((8, HALF), 0) == 7
    bl_rows = [_colsum(jnp.where(last8, b[CHUNK * c + CHUNK - 8:CHUNK * (c + 1)], 0.0)) for c in range(G)]
    bl3 = jnp.stack([r[:, 128 * h:128 * (h + 1)] for r in bl_rows for h in range(HEADS)], axis=0)
    b3, k3 = _heads_to_batch(b, 128), _heads_to_batch(k, 128)
    b_sc[...] = b3
    k_sc[...] = k3
    return dict(G=G, lb=lb, hq=hq, f=f, sg=sg, nsg=nsg, sq=sq, rowmod=rowmod, q3=_heads_to_batch(q, 128), k3=k3, b3=b3,
                bl3=bl3)


HSUB = 8


def _lo(j):
    return HSUB * (j // HSUB)


def _hg_diag_tiles(b_sc, b3, r0, rowi):
    bi = b3[:, r0:r0 + SUB]
    return [jnp.exp2(jnp.where(rowi[:, _lo(s):] >= s, bi[:, _lo(s):] - b_sc[:, r0 + s:r0 + s + 1, :], NEG))
            for s in range(SUB)]


def _hg_diag_tiles_t(b_sc, b3, r0, rowi):
    bi = b3[:, r0:r0 + SUB]
    return [jnp.exp2(jnp.where(rowi[:, :_lo(t) + HSUB] <= t, b_sc[:, r0 + t:r0 + t + 1, :] - bi[:, :_lo(t) + HSUB], NEG))
            for t in range(SUB)]


def _lane_sums(pieces, ones):
    B = pieces[0].shape[0]
    hs = [p.shape[1] for p in pieces]
    R = _dot(jnp.concatenate(pieces, axis=1).reshape(B * sum(hs), 128), ones).reshape(B, sum(hs), 128)
    out, o = [], 0
    for h in hs:
        out.append(R[:, o:o + h])
        o += h
    return out


def _sum_tri(terms, low_rows):
    if SUB == HSUB:
        return sum(terms)
    full = sum(t for t in terms if t.shape[1] == SUB)
    half = sum(t for t in terms if t.shape[1] == HSUB)
    lo, hi = full[:, :HSUB], full[:, HSUB:]
    return jnp.concatenate([lo + half, hi] if low_rows else [lo, hi + half], axis=1)


def _hgrn2_fwd(u, lb_logits, comms=None):
    T = u.shape[0]
    G = min(GC, T // CHUNK)
    R, B, N = G * CHUNK, G * HEADS, T // CHUNK

    def body(hq_ref, hf_ref, hv_ref, lg_ref, o_ref, st_ref, S_ref, b_sc, k_sc, v_sc):
        @pl.when(pl.program_id(0) == 0)
        def _():
            S_ref[...] = jnp.zeros_like(S_ref)

        pz = _hg_prep(hq_ref, hf_ref, lg_ref, b_sc, k_sc)
        q3, k3, b3, bl3 = pz["q3"], pz["k3"], pz["b3"], pz["bl3"]
        v3 = _heads_to_batch(hv_ref[...], 128)
        v_sc[...] = v3
        stloc = _bdot(v3, k3 * jnp.exp2(bl3 - b3), BTN).reshape(G, HEADS, 128, 128)
        dec = jnp.exp2(bl3).reshape(G, HEADS, 1, 128)
        ST = S_ref[...]
        sts = []
        for c in range(G):
            sts.append(ST)
            ST = ST * dec[c] + stloc[c]
        S_ref[...] = ST
        st4 = jnp.stack(sts, axis=0)
        st_ref[...] = st4
        o = _bdot(q3 * jnp.exp2(b3), st4.reshape(B, 128, 128), BNT)
        ones = jnp.ones((128, 128), F32)
        rowi = _iota((1, SUB, 128), 1)
        outs = []
        for i in range(CHUNK // SUB):
            r0 = SUB * i
            qi = q3[:, r0:r0 + SUB]
            oi = o[:, r0:r0 + SUB]
            if i > 0:
                r = b_sc[:, r0 - 1:r0, :]
                qe = qi * jnp.exp2(b3[:, r0:r0 + SUB] - r)
                ke = k3[:, :r0] * jnp.exp2(r - b3[:, :r0])
                oi = oi + _bdot(_bdot(qe, ke, BNT), v3[:, :r0], BNN)
            tiles = _hg_diag_tiles(b_sc, b3, r0, rowi)
            a_b = _lane_sums([qi[:, _lo(s):] * (k_sc[:, r0 + s:r0 + s + 1, :] * tiles[s]) for s in range(SUB)], ones)
            outs.append(oi + _sum_tri([a_b[s] * v_sc[:, r0 + s:r0 + s + 1, :] for s in range(SUB)], False))
        o_ref[...] = _batch_to_heads(jnp.concatenate(outs, axis=1))

    blk = lambda c: pl.BlockSpec((R, HALF), lambda n, c=c: (n, c // 4))
    return _hosted_call(
        body, comms, name="hgrn2_fwd", grid=(N // G,),
        in_specs=[blk(C_HQ), blk(C_HF), blk(C_HV), _full(lb_logits.shape)],
        out_specs=[pl.BlockSpec((R, HALF), lambda n: (n, 0)),
                   pl.BlockSpec((G, HEADS, 128, 128), lambda n: (n, 0, 0, 0))],
        out_shape=[_sds((T, HALF)), _sds((N, HEADS, 128, 128))],
        scratch_shapes=[pltpu.VMEM((HEADS, 128, 128), F32)] + [pltpu.VMEM((B, CHUNK, 128), F32)] * 3,
        args=(u, u, u, lb_logits))


def _hgrn2_bwd(u, lb_logits, do, states, comms=None):
    T = u.shape[0]
    G = min(GC, T // CHUNK)
    R, B, NG = G * CHUNK, G * HEADS, T // (G * CHUNK)

    def body(hq_ref, hf_ref, hv_ref, lg_ref, do_ref, st_ref, dhq_ref, dhf_ref, dhv_ref, dlb_ref,
             dS_ref, b_sc, k_sc, v_sc, q_sc, do_sc):
        @pl.when(pl.program_id(0) == 0)
        def _():
            dS_ref[...] = jnp.zeros_like(dS_ref)
            dlb_ref[...] = jnp.zeros_like(dlb_ref)

        pz = _hg_prep(hq_ref, hf_ref, lg_ref, b_sc, k_sc)
        q3, k3, b3, bl3, lb = pz["q3"], pz["k3"], pz["b3"], pz["bl3"], pz["lb"]
        v3 = _heads_to_batch(hv_ref[...], 128)
        v_sc[...] = v3
        do3 = _heads_to_batch(do_ref[...], 128)
        q_sc[...] = q3
        do_sc[...] = do3
        st3 = st_ref[...].reshape(B, 128, 128)
        eb = jnp.exp2(b3)
        ebl = jnp.exp2(bl3 - b3)
        qt = q3 * eb
        kl = k3 * ebl
        dstloc = _bdot(do3, qt, BTN).reshape(G, HEADS, 128, 128)
        dec = jnp.exp2(bl3).reshape(G, HEADS, 1, 128)
        dST = dS_ref[...]
        dsts = [None] * G
        for c in reversed(range(G)):
            dsts[c] = dST
            dST = dST * dec[c] + dstloc[c]
        dS_ref[...] = dST
        dst3 = jnp.stack(dsts, axis=0).reshape(B, 128, 128)
        dqt = _bdot(do3, st3, BNN)
        dkl = _bdot(v3, dst3, BNN)
        dv_acc = _bdot(kl, dst3, BNT)
        ones = jnp.ones((128, 128), F32)
        rowi = _iota((1, SUB, 128), 1)
        dq_parts, dk_parts, dv_parts = [], [], []
        dk_in = jnp.zeros((B, CHUNK, 128), F32)
        for i_s in range(CHUNK // SUB):
            r0 = SUB * i_s
            qi = q3[:, r0:r0 + SUB]
            doi = do3[:, r0:r0 + SUB]
            dqi = jnp.zeros((B, SUB, 128), F32)
            if i_s > 0:
                r = b_sc[:, r0 - 1:r0, :]
                eq = jnp.exp2(b3[:, r0:r0 + SUB] - r)
                ek = jnp.exp2(r - b3[:, :r0])
                qe = qi * eq
                ke = k3[:, :r0] * ek
                a_off = _bdot(qe, ke, BNT)
                p_off = _bdot(doi, v3[:, :r0], BNT)
                pad = jnp.zeros((B, CHUNK - r0, 128), F32)
                dv_acc = dv_acc + jnp.concatenate([_bdot(a_off, doi, BTN), pad], axis=1)
                dqi = dqi + _bdot(p_off, ke, BNN) * eq
                dk_in = dk_in + jnp.concatenate([_bdot(p_off, qe, BTN) * ek, pad], axis=1)
            ki, vi = k3[:, r0:r0 + SUB], v3[:, r0:r0 + SUB]
            rng = range(SUB)
            tiles = _hg_diag_tiles(b_sc, b3, r0, rowi)
            tiles_t = _hg_diag_tiles_t(b_sc, b3, r0, rowi)
            do_rows = [do_sc[:, r0 + t:r0 + t + 1, :] for t in rng]
            kts = [k_sc[:, r0 + s:r0 + s + 1, :] * tiles[s] for s in rng]
            qts = [q_sc[:, r0 + t:r0 + t + 1, :] * tiles_t[t] for t in rng]
            ps = [doi[:, _lo(s):] * v_sc[:, r0 + s:r0 + s + 1, :] for s in rng]
            mst = [ki[:, :_lo(t) + HSUB] * qts[t] for t in rng]
            pst = [vi[:, :_lo(t) + HSUB] * do_rows[t] for t in rng]
            sums = _lane_sums(ps + mst + pst, ones)
            p_b, a_t, p_t = sums[:SUB], sums[SUB:2 * SUB], sums[2 * SUB:]
            dq_parts.append(dqi + _sum_tri([p_b[s] * kts[s] for s in rng], False))
            dv_parts.append(_sum_tri([a_t[t] * do_rows[t] for t in rng], True))
            dk_parts.append(_sum_tri([p_t[t] * qts[t] for t in rng], True))
        dq_in = jnp.concatenate(dq_parts, axis=1)
        dk_in = dk_in + jnp.concatenate(dk_parts, axis=1)
        dv_acc = dv_acc + jnp.concatenate(dv_parts, axis=1)
        db = qt * dqt + q3 * dq_in - k3 * dk_in - kl * dkl
        last = jnp.sum(kl * dkl, axis=1, keepdims=True) + jnp.exp2(bl3) * jnp.sum(st3 * dst3, axis=1, keepdims=True)
        db = db + jnp.where(_iota((1, CHUNK, 1), 1) == CHUNK - 1, last, 0.0)
        dg = _chunk_cumsum(_batch_to_heads(db), pz["rowmod"], reverse=True)
        dq_tot = _batch_to_heads(dqt * eb + dq_in)
        dk_tot = _batch_to_heads(dkl * ebl + dk_in)
        common = dg / pz["f"] - dk_tot
        dhf_ref[...] = ((1.0 - lb) * pz["sg"] * pz["nsg"] * common).astype(dhf_ref.dtype)
        dl0 = _colsum(pz["nsg"] * common) * lb * (1.0 - lb)
        dlb_ref[0:1, :] += dl0
        dlb_ref[1:2, :] -= dl0
        dhq_ref[...] = (dq_tot * _dsilu(pz["hq"], pz["sq"])).astype(dhq_ref.dtype)
        dhv_ref[...] = _batch_to_heads(dv_acc).astype(dhv_ref.dtype)

    rev = lambda c: pl.BlockSpec((R, HALF), lambda i, c=c: (NG - 1 - i, c // 4))
    rev0 = pl.BlockSpec((R, HALF), lambda i: (NG - 1 - i, 0))
    return _hosted_call(
        body, comms, name="hgrn2_bwd", grid=(NG,),
        in_specs=[rev(C_HQ), rev(C_HF), rev(C_HV), _full(lb_logits.shape), rev0,
                  pl.BlockSpec((G, HEADS, 128, 128), lambda i: (NG - 1 - i, 0, 0, 0))],
        out_specs=[rev0, rev0, rev0, _full((2, HALF))],
        out_shape=[_sds((T, HALF), _MXU)] * 3 + [_sds((2, HALF))],
        scratch_shapes=[pltpu.VMEM((HEADS, 128, 128), F32)] + [pltpu.VMEM((B, CHUNK, 128), F32)] * 5,
        args=(u, u, u, lb_logits, do, states))


def _lanes_to_batch_cols(x, lane):
    G = x.shape[0] // CHUNK
    cols = [_lane_col(x, 4 + h, lane).reshape(G, CHUNK, 1) for h in range(HEADS)]
    return jnp.stack(cols, axis=1).reshape(G * HEADS, CHUNK, 1)


def _row_scalars(rows):
    lane = _iota((1, 128), 1)
    return jnp.stack([_rowsum(jnp.where(lane == 4 + h, r, 0.0)) for r in rows for h in range(HEADS)], axis=0)


def _ml_gates(gates):
    R = gates.shape[0]
    lane = _iota((R, 128), 1)
    rowmod = _iota((R, 128), 0) & (CHUNK - 1)
    lf = jnp.minimum(gates, 0.0) - jnp.log(1.0 + jnp.exp(-jnp.abs(gates)))
    g_all = _chunk_cumsum(lf, rowmod)
    x_all = pltpu.roll(gates, 4, 1) - g_all
    return g_all, x_all, lane, rowmod


def _ml_chunk_rows(g_all, x_all, mprev, g):
    gl = g_all[CHUNK * g + CHUNK - 8:CHUNK * (g + 1)]
    gl = _colsum(jnp.where(_iota((8, 128), 0) == 7, gl, 0.0))
    a = gl + x_all[CHUNK * g:CHUNK * (g + 1)]
    m_new = jnp.maximum(gl + mprev, jnp.max(a, axis=0, keepdims=True))
    return m_new, jnp.exp(gl + mprev - m_new), jnp.exp(a - m_new)


def _ml_batched(q3, k3, v3, g_all, x_all, lane, C3, n3, mprev3):
    G = g_all.shape[0] // CHUNK
    gcol3 = _lanes_to_batch_cols(g_all, lane)
    onehot = jnp.where(_iota((G, 8, 128), 1) + 4 == _iota((G, 8, 128), 2), 1.0, 0.0).astype(F32)
    rows = _bdotx(onehot, x_all.reshape(G, CHUNK, 128), BNT)
    sub = _iota((G, 8, CHUNK), 1)
    row3 = jnp.stack([jnp.sum(jnp.where(sub == h, rows, 0.0), axis=1, keepdims=True) for h in range(HEADS)],
                     axis=1).reshape(G * HEADS, 1, CHUNK)
    causal = _iota((1, CHUNK, CHUNK), 1) >= _iota((1, CHUNK, CHUNK), 2)
    dmat = jnp.where(causal, gcol3 + row3, NEG)
    m_inter = gcol3 + mprev3
    m_t = jnp.maximum(m_inter, jnp.max(dmat, axis=2, keepdims=True))
    wi = jnp.exp(dmat - m_t)
    wn = jnp.exp(m_inter - m_t)
    s3 = _bdot(q3, k3, BNT) * wi
    qc = _bdot(q3, C3, BNN)
    qn = jnp.sum(q3 * n3, axis=2, keepdims=True)
    num = _bdot(s3, v3, BNN) + wn * qc
    den = jnp.sum(s3, axis=2, keepdims=True) + wn * qn
    floor = jnp.exp(-m_t)
    return dict(wi=wi, wn=wn, s=s3, qc=qc, qn=qn, num=num, den=den, floor=floor, nrm=jnp.maximum(jnp.abs(den), floor))


def _mlstm_fwd(qkc, u, comms=None):
    T = u.shape[0]
    G = min(GC, T // CHUNK)
    R = G * CHUNK
    N = T // CHUNK

    def body(qk_ref, v_ref, g_ref, h_ref, cst_ref, nst_ref, mst_ref, C_ref, n_ref, m_ref):
        @pl.when(pl.program_id(0) == 0)
        def _():
            C_ref[...] = jnp.zeros_like(C_ref)
            n_ref[...] = jnp.zeros_like(n_ref)
            m_ref[...] = jnp.zeros_like(m_ref)

        g_all, x_all, lane, _ = _ml_gates(g_ref[...])
        m_row = m_ref[...]
        mprev_rows, wo_rows, ws_parts = [], [], []
        for g in range(G):
            mprev_rows.append(m_row)
            m_row, wo, ws = _ml_chunk_rows(g_all, x_all, m_row, g)
            wo_rows.append(wo)
            ws_parts.append(ws)
        m_ref[...] = m_row
        mst_ref[...] = jnp.stack(mprev_rows, axis=0)
        ws3 = _lanes_to_batch_cols(jnp.concatenate(ws_parts, axis=0), lane)
        wo4 = _row_scalars(wo_rows).reshape(G, HEADS, 1, 1)
        q3 = _heads_to_batch(qk_ref[:, :256] * ML_SCALE, ML_DQK)
        k3 = _heads_to_batch(qk_ref[:, 256:], ML_DQK)
        v3 = _heads_to_batch(v_ref[...], 128)
        kw = k3 * ws3
        cloc = _bdot(kw, v3, BTN).reshape(G, HEADS, ML_DQK, 128)
        nloc = jnp.sum(kw, axis=1, keepdims=True).reshape(G, HEADS, 1, ML_DQK)
        C, nn = C_ref[...], n_ref[...]
        cs, ns = [], []
        for g in range(G):
            cs.append(C)
            ns.append(nn)
            C = wo4[g] * C + cloc[g]
            nn = wo4[g] * nn + nloc[g]
        C_ref[...] = C
        n_ref[...] = nn
        c4, n4 = jnp.stack(cs, axis=0), jnp.stack(ns, axis=0)
        cst_ref[...] = c4
        nst_ref[...] = n4
        r = _ml_batched(q3, k3, v3, g_all, x_all, lane, c4.reshape(G * HEADS, ML_DQK, 128),
                        n4.reshape(G * HEADS, 1, ML_DQK), _row_scalars(mprev_rows))
        h_ref[...] = _batch_to_heads(r["num"] / r["nrm"])

    return _hosted_call(
        body, comms, name="mlstm_fwd", grid=(N // G,),
        in_specs=[pl.BlockSpec((R, HALF), lambda n: (n, 0)), pl.BlockSpec((R, HALF), lambda n: (n, C_MV // 4)),
                  pl.BlockSpec((R, 128), lambda n: (n, C_GATES))],
        out_specs=[pl.BlockSpec((R, HALF), lambda n: (n, 0)),
                   pl.BlockSpec((G, HEADS, ML_DQK, 128), lambda n: (n, 0, 0, 0)),
                   pl.BlockSpec((G, HEADS, 1, ML_DQK), lambda n: (n, 0, 0, 0)),
                   pl.BlockSpec((G, 1, 128), lambda n: (n, 0, 0))],
        out_shape=[_sds((T, HALF)), _sds((N, HEADS, ML_DQK, 128)), _sds((N, HEADS, 1, ML_DQK)), _sds((N, 1, 128))],
        scratch_shapes=[pltpu.VMEM((HEADS, ML_DQK, 128), F32), pltpu.VMEM((HEADS, 1, ML_DQK), F32),
                        pltpu.VMEM((1, 128), F32)],
        args=(qkc, u, u))


def _mlstm_bwd(qkc, u, dh, cst, nst, mst, pre, cw):
    T = u.shape[0]
    G = min(GC, T // CHUNK)
    R = G * CHUNK
    NG = T // R

    def body(qk_ref, v_ref, g_ref, dh_ref, cst_ref, nst_ref, mst_ref, pre_ref, x_ref, xh_ref, cw_ref,
             dmqk_ref, dv_ref, dgt_ref, dcw_ref, dcb_ref, dC_ref, dn_ref, next_sc):
        @pl.when(pl.program_id(0) == 0)
        def _():
            for r in (dC_ref, dn_ref, next_sc, dcw_ref, dcb_ref):
                r[...] = jnp.zeros_like(r)

        B = G * HEADS
        gates = g_ref[...]
        g_all, x_all, lane, rowmod = _ml_gates(gates)
        mprev_rows = [mst_ref[g] for g in range(G)]
        wo_rows, ws_parts = [], []
        for g in range(G):
            _, wo, ws = _ml_chunk_rows(g_all, x_all, mprev_rows[g], g)
            wo_rows.append(wo)
            ws_parts.append(ws)
        ws3 = _lanes_to_batch_cols(jnp.concatenate(ws_parts, axis=0), lane)
        wo3 = _row_scalars(wo_rows)
        wo4 = wo3.reshape(G, HEADS, 1, 1)
        q3 = _heads_to_batch(qk_ref[:, :256] * ML_SCALE, ML_DQK)
        k3 = _heads_to_batch(qk_ref[:, 256:], ML_DQK)
        v3 = _heads_to_batch(v_ref[...], 128)
        dh3 = _heads_to_batch(dh_ref[...], 128)
        C3 = cst_ref[...].reshape(B, ML_DQK, 128)
        n3 = nst_ref[...].reshape(B, 1, ML_DQK)
        r = _ml_batched(q3, k3, v3, g_all, x_all, lane, C3, n3, _row_scalars(mprev_rows))
        wn, s3 = r["wn"], r["s"]
        inv = 1.0 / r["nrm"]
        dnum = dh3 * inv
        dnrm = -jnp.sum(dh3 * (r["num"] * inv), axis=2, keepdims=True) * inv
        dden = jnp.where(jnp.abs(r["den"]) > r["floor"], dnrm * jnp.sign(r["den"]), 0.0)
        ds = _bdot(dnum, v3, BNT) + dden
        dqk = ds * r["wi"]
        dd = ds * s3
        qw = q3 * wn
        dcloc = _bdot(qw, dnum, BTN).reshape(G, HEADS, ML_DQK, 128)
        dnloc = jnp.sum(qw * dden, axis=1, keepdims=True).reshape(G, HEADS, 1, ML_DQK)
        dC, dn = dC_ref[...], dn_ref[...]
        dcs, dns = [None] * G, [None] * G
        for g in reversed(range(G)):
            dcs[g], dns[g] = dC, dn
            dC = wo4[g] * dC + dcloc[g]
            dn = wo4[g] * dn + dnloc[g]
        dC_ref[...] = dC
        dn_ref[...] = dn
        dC3 = jnp.stack(dcs, axis=0).reshape(B, ML_DQK, 128)
        dn3 = jnp.stack(dns, axis=0).reshape(B, 1, ML_DQK)
        dk_st = ws3 * (_bdot(v3, dC3, BNT) + dn3)
        dq = _bdot(dqk, k3, BNN) + wn * (_bdot(dnum, C3, BNT) + dden * n3)
        dk = _bdot(dqk, q3, BTN) + dk_st
        dv = _bdot(s3, dnum, BTN) + ws3 * _bdot(k3, dC3, BNN)
        dv_ref[...] = _batch_to_heads(dv).astype(dv_ref.dtype)
        dqk = jnp.concatenate([_batch_to_heads(dq * ML_SCALE), _batch_to_heads(dk)], axis=1)
        pre = pre_ref[...]
        dpre = dqk * _dsilu(pre, _sig(pre))
        x = x_ref[...]
        xprev = jnp.where(pl.program_id(0) < NG - 1, xh_ref[...], 0.0)
        nxt = next_sc[...]
        row8 = _iota((8, HALF), 0)
        dx = dpre * cw_ref[3:4, :]
        dws = [None, None, None, _colsum(dpre * x)]
        for j in (1, 2, 3):
            dx = dx + _shift_rows_up(dpre, nxt, j, row8) * cw_ref[3 - j:4 - j, :]
            dws[3 - j] = _colsum(dpre * _shift_rows(x, xprev, j, row8))
        dmqk_ref[...] = dx.astype(dmqk_ref.dtype)
        dcw_ref[...] += jnp.concatenate(dws, axis=0)
        dcb_ref[...] += _colsum(dpre)
        next_sc[...] = dpre[:8]
        e_col = wn * (jnp.sum(dnum * r["qc"], axis=2, keepdims=True) + dden * r["qn"])
        c_col = jnp.sum(k3 * dk_st, axis=2, keepdims=True)
        z = wo3 * (jnp.sum(dC3 * C3, axis=(1, 2), keepdims=True) + jnp.sum(dn3 * n3, axis=(1, 2), keepdims=True))
        dd_hi = dd.astype(_MXU).astype(F32)
        ones = jnp.ones((B, CHUNK, 128), F32)
        dd_cols = (_bdot(dd_hi, ones, BTN) + _bdot(dd - dd_hi, ones, BTN))[:, :, 0:1]
        last = _iota((1, CHUNK, 1), 1) == CHUNK - 1
        dg3 = jnp.sum(dd, axis=2, keepdims=True) - dd_cols + e_col - c_col
        dg3 = dg3 + jnp.where(last, jnp.sum(c_col, axis=1, keepdims=True) + z, 0.0)
        di3 = dd_cols + c_col

        def to_lanes(x3, first):
            x4 = x3.reshape(G, HEADS, CHUNK, 1)
            return sum(jnp.where(lane == first + h, x4[:, h].reshape(R, 1), 0.0) for h in range(HEADS))

        dlf = _chunk_cumsum(to_lanes(dg3, 4), rowmod, reverse=True)
        dgt_ref[...] = (to_lanes(di3, 0) + dlf * _sig(-gates)).astype(dgt_ref.dtype)

    rev = lambda w, c: pl.BlockSpec((R, w), lambda i, c=c: (NG - 1 - i, c))
    st = lambda *s: pl.BlockSpec((G,) + s, lambda i: (NG - 1 - i,) + (0,) * len(s))
    halo = pl.BlockSpec((8, HALF), lambda i: (jnp.maximum((NG - 1 - i) * (R // 8) - 1, 0), C_MQK // 4))
    return pl.pallas_call(
        body, name="mlstm_bwd", grid=(NG,),
        in_specs=[rev(HALF, 0), rev(HALF, C_MV // 4), rev(128, C_GATES), rev(HALF, 0),
                  st(HEADS, ML_DQK, 128), st(HEADS, 1, ML_DQK), st(1, 128),
                  rev(HALF, 0), rev(HALF, C_MQK // 4), halo, _full(cw.shape)],
        out_specs=[rev(HALF, 0), rev(HALF, 0), rev(128, 0), _full((4, HALF)), _full((1, HALF))],
        out_shape=[_sds((T, HALF), _MXU), _sds((T, HALF), _MXU), _sds((T, 128), _MXU), _sds((4, HALF)), _sds((1, HALF))],
        scratch_shapes=[pltpu.VMEM((HEADS, ML_DQK, 128), F32), pltpu.VMEM((HEADS, 1, ML_DQK), F32),
                        pltpu.VMEM((8, HALF), F32)],
        compiler_params=_cp("arbitrary"))(qkc, u, u, dh, cst, nst, mst, pre, u, u, cw)


def _head_norm(o):
    rs_parts, r_parts = [], []
    for h in range(HEADS):
        oh = o[:, 128 * h:128 * (h + 1)]
        rs = lax.rsqrt(jnp.mean(oh * oh, axis=-1, keepdims=True) + RMS_EPS)
        rs_parts.append(rs)
        r_parts.append(oh * rs)
    return jnp.concatenate(r_parts, axis=1), rs_parts


def _out_proj(x, u, o_hg, h_ml, g_hg, g_ml, w_out, tm):
    T = x.shape[0]

    def body(x_ref, hgate_ref, mo_ref, ohg_ref, hml_ref, ghg_ref, gml_ref, w_ref, m_ref, z_ref):
        hgate = hgate_ref[...]
        a = _head_norm(ohg_ref[...])[0] * ghg_ref[...] * (hgate * _sig(hgate))
        b = _head_norm(hml_ref[...])[0] * gml_ref[...] * _sig(mo_ref[...])
        m = jnp.concatenate([a, b], axis=1)
        m_ref[...] = m.astype(m_ref.dtype)
        z_ref[...] = ALPHA * x_ref[...] + _dot(m, w_ref[...])

    return pl.pallas_call(
        body, name="out_proj", grid=(T // tm,),
        in_specs=[_row(tm, D_MODEL), _row(tm, HALF, C_HGATE // 4), _row(tm, HALF, C_MO // 4),
                  _row(tm, HALF), _row(tm, HALF), _full(g_hg.shape), _full(g_ml.shape), _full(w_out.shape)],
        out_specs=[_row(tm, D_MODEL)] * 2,
        out_shape=[_sds((T, D_MODEL), _MXU), _sds((T, D_MODEL))],
        compiler_params=_cp("parallel"))(x, u, u, o_hg, h_ml, g_hg, g_ml, w_out)


def _ffn_ln(z1, ln1_g, ln1_b, wg, wu, wd, ln_g, ln_b, tm):
    T = z1.shape[0]

    def body(z1_ref, g1_ref, b1_ref, wg_ref, wu_ref, wd_ref, g_ref, b_ref, z_ref, x1_ref, x2_ref, a_ref, bb_ref, h_ref):
        x = _ln_fwd(z1_ref[...], g1_ref[...], b1_ref[...])[0]
        x1_ref[...] = x.astype(x1_ref.dtype)
        a = _dot(x, wg_ref[...], NT)
        bb = _dot(x, wu_ref[...], NT)
        hh = a * _sig(a) * bb
        a_ref[...] = a.astype(a_ref.dtype)
        bb_ref[...] = bb.astype(bb_ref.dtype)
        h_ref[...] = hh.astype(h_ref.dtype)
        z = ALPHA * x + _dot(hh, wd_ref[...])
        z_ref[...] = z
        x2_ref[...] = _ln_fwd(z, g_ref[...], b_ref[...])[0].astype(x2_ref.dtype)

    vec = _full((1, D_MODEL))
    return pl.pallas_call(
        body, name="ffn_ln2", grid=(T // tm,),
        in_specs=[_row(tm, D_MODEL), vec, vec, _full(wg.shape), _full(wu.shape), _full(wd.shape), vec, vec],
        out_specs=[_row(tm, D_MODEL)] * 3 + [_row(tm, D_FF)] * 3,
        out_shape=[_sds((T, D_MODEL))] + [_sds((T, D_MODEL), _MXU)] * 2 + [_sds((T, D_FF), _MXU)] * 3,
        compiler_params=_cp("parallel"))(z1, ln1_g, ln1_b, wg, wu, wd, ln_g, ln_b)


def _ple_loss_ln2_bwd(z2, p, tgt, wpg, bpg, wpp, ln_g, ln_b, tm):
    T = z2.shape[0]

    def body(z_ref, p_ref, t_ref, wpg_ref, bpg_ref, wpp_ref, g_ref, b_ref,
             de_ref, dgp_ref, dz_ref, loss_ref, dbpg_ref, dg_ref, db_ref):
        @pl.when(pl.program_id(0) == 0)
        def _():
            for r in (loss_ref, dbpg_ref, dg_ref, db_ref):
                r[...] = jnp.zeros_like(r)

        x2, xhat, rstd = _ln_fwd(z_ref[...], g_ref[...], b_ref[...])
        gate = _sig(_dot(x2, wpg_ref[...]) + bpg_ref[...])
        e = _dot(p_ref[...], wpp_ref[...])
        err = x2 + gate * e - t_ref[...]
        loss_ref[...] += _colsum(err * err)
        dy = err * (1.0 / D_MODEL)
        de_ref[...] = (dy * gate).astype(de_ref.dtype)
        dgp = dy * e * gate * (1.0 - gate)
        dgp_ref[...] = dgp.astype(dgp_ref.dtype)
        dbpg_ref[...] += _colsum(dgp)
        dx2 = dy + _dot(dgp, wpg_ref[...], NT)
        dg_ref[...] += _colsum(dx2 * xhat)
        db_ref[...] += _colsum(dx2)
        dz_ref[...] = _ln_bwd(dx2, xhat, rstd, g_ref[...])

    vec = _full((1, D_MODEL))
    return pl.pallas_call(
        body, name="ple_loss_ln2_bwd", grid=(T // tm,),
        in_specs=[_row(tm, D_MODEL), _row(tm, PLE_DIM), _row(tm, D_MODEL),
                  _full(wpg.shape), vec, _full(wpp.shape), vec, vec],
        out_specs=[_row(tm, D_MODEL)] * 3 + [vec] * 4,
        out_shape=[_sds((T, D_MODEL), _MXU)] * 2 + [_sds((T, D_MODEL))] + [_sds((1, D_MODEL))] * 4,
        compiler_params=_cp("arbitrary"))(z2, p, tgt, wpg, bpg, wpp, ln_g, ln_b)


def _ffn_bwd_ln1_bwd(a_pre, b_pre, z1, dz2, wg, wu, wd, ln_g, ln_b, tm, comms=None):
    T = z1.shape[0]

    def body(a_ref, bb_ref, z_ref, dz2_ref, wg_ref, wu_ref, wd_ref, g_ref, b_ref,
             da_ref, dbb_ref, dz1_ref, dg_ref, db_ref):
        @pl.when(pl.program_id(0) == 0)
        def _():
            dg_ref[...] = jnp.zeros_like(dg_ref)
            db_ref[...] = jnp.zeros_like(db_ref)

        dz2 = dz2_ref[...]
        a = a_ref[...].astype(F32)
        bb = bb_ref[...].astype(F32)
        sa = _sig(a)
        act = a * sa
        dh = _dot(dz2, wd_ref[...], NT)
        da = (dh * bb * _dsilu(a, sa)).astype(da_ref.dtype)
        dbb = (dh * act).astype(dbb_ref.dtype)
        da_ref[...] = da
        dbb_ref[...] = dbb
        dx1 = ALPHA * dz2 + _dot(da, wg_ref[...]) + _dot(dbb, wu_ref[...])
        _, xhat, rstd = _ln_fwd(z_ref[...], g_ref[...], b_ref[...])
        dg_ref[...] += _colsum(dx1 * xhat)
        db_ref[...] += _colsum(dx1)
        dz1_ref[...] = _ln_bwd(dx1, xhat, rstd, g_ref[...])

    vec = _full((1, D_MODEL))
    return _hosted_call(
        body, comms, name="ffn_bwd_ln1_bwd", grid=(T // tm,),
        in_specs=[_row(tm, D_FF)] * 2 + [_row(tm, D_MODEL)] * 2 + [_full(wg.shape), _full(wu.shape), _full(wd.shape), vec, vec],
        out_specs=[_row(tm, D_FF)] * 2 + [_row(tm, D_MODEL), vec, vec],
        out_shape=[_sds((T, D_FF), _MXU)] * 2 + [_sds((T, D_MODEL)), _sds((1, D_MODEL)), _sds((1, D_MODEL))],
        scratch_shapes=[], args=(a_pre, b_pre, z1, dz2, wg, wu, wd, ln_g, ln_b))


def _out_proj_bwd(dz1, u, o_hg, h_ml, g_hg, g_ml, w_out, tm):
    T = dz1.shape[0]

    def body(dz_ref, hgate_ref, mo_ref, ohg_ref, hml_ref, ghg_ref, gml_ref, w_ref,
             dohg_ref, dhml_ref, dhgate_ref, dmo_ref, dghg_ref, dgml_ref):
        @pl.when(pl.program_id(0) == 0)
        def _():
            dghg_ref[...] = jnp.zeros_like(dghg_ref)
            dgml_ref[...] = jnp.zeros_like(dgml_ref)

        dm = _dot(dz_ref[...], w_ref[...], NT)

        def half(dmh, o, gvec, gate_val, dgate_fac, do_ref, dgate_ref, dgvec_ref):
            r, rs = _head_norm(o)
            dgate_ref[...] = (dmh * r * gvec * dgate_fac).astype(dgate_ref.dtype)
            dn = dmh * gate_val
            dgvec_ref[...] += _colsum(dn * r)
            dr = dn * gvec
            parts = []
            for h in range(HEADS):
                sl = slice(128 * h, 128 * (h + 1))
                parts.append(rs[h] * (dr[:, sl] - r[:, sl] * jnp.mean(dr[:, sl] * r[:, sl], axis=-1, keepdims=True)))
            do_ref[...] = jnp.concatenate(parts, axis=1)

        hg = hgate_ref[...]
        shg = _sig(hg)
        half(dm[:, :HALF], ohg_ref[...], ghg_ref[...], hg * shg, _dsilu(hg, shg), dohg_ref, dhgate_ref, dghg_ref)
        smo = _sig(mo_ref[...])
        half(dm[:, HALF:], hml_ref[...], gml_ref[...], smo, smo * (1.0 - smo), dhml_ref, dmo_ref, dgml_ref)

    vec = _full((1, HALF))
    return pl.pallas_call(
        body, name="out_proj_bwd", grid=(T // tm,),
        in_specs=[_row(tm, D_MODEL), _row(tm, HALF, C_HGATE // 4), _row(tm, HALF, C_MO // 4),
                  _row(tm, HALF), _row(tm, HALF), vec, vec, _full(w_out.shape)],
        out_specs=[_row(tm, HALF)] * 4 + [vec, vec],
        out_shape=[_sds((T, HALF))] * 2 + [_sds((T, HALF), _MXU)] * 2 + [_sds((1, HALF))] * 2,
        compiler_params=_cp("arbitrary"))(dz1, u, u, o_hg, h_ml, g_hg, g_ml, w_out)


def _du_specs(rows):
    return [pl.BlockSpec((rows, w), lambda i: (i, 0)) for w in DU_WIDTHS]


def _in_proj_bwd(dz1, du_parts, w, tm, comms=None):
    T = dz1.shape[0]

    def body(dz_ref, *refs):
        du = jnp.concatenate([r[...] for r in refs[:8]], axis=1)
        refs[9][...] = ALPHA * dz_ref[...] + _dot(du, refs[8][...], NT)

    (dx,), got = _hosted_call(
        body, comms, name="in_proj_bwd", grid=(T // tm,),
        in_specs=[_row(tm, D_MODEL)] + _du_specs(tm) + [_full(w.shape)],
        out_specs=[_row(tm, D_MODEL)], out_shape=[_sds((T, D_MODEL))], scratch_shapes=[], args=(dz1, *du_parts, w))
    return dx, got


def _wgrad(a, b, name, tm, tn, tk):
    T, M = a.shape
    N = b.shape[1]
    tm, tn, tk = min(tm, M), min(tn, N), min(tk, T)
    nk = T // tk

    def body(a_ref, b_ref, o_ref, acc_ref):
        kk = pl.program_id(2)

        @pl.when(kk == 0)
        def _():
            acc_ref[...] = jnp.zeros_like(acc_ref)

        acc_ref[...] += _dot(a_ref[...], b_ref[...], TN)

        @pl.when(kk == nk - 1)
        def _():
            o_ref[...] = acc_ref[...].astype(o_ref.dtype)

    return pl.pallas_call(
        body, name=name, grid=(M // tm, N // tn, nk),
        in_specs=[pl.BlockSpec((tk, tm), lambda i, j, kk: (kk, i)), pl.BlockSpec((tk, tn), lambda i, j, kk: (kk, j))],
        out_specs=pl.BlockSpec((tm, tn), lambda i, j, kk: (i, j)), out_shape=_sds((M, N), _MXU),
        scratch_shapes=[pltpu.VMEM((tm, tn), F32)],
        compiler_params=_cp("parallel", "parallel", "arbitrary"))(a, b)


W_IN_PARTS = 2


def _wgrad_w_in(x, du_parts, tk, part, comms=None):
    T = x.shape[0]
    M = D_MODEL // W_IN_PARTS
    tk = min(tk, T)
    nk = T // tk

    def body(a_ref, *refs):
        o_ref, cs_ref, acc_ref = refs[8:]
        kk = pl.program_id(0)

        @pl.when(kk == 0)
        def _():
            acc_ref[...] = jnp.zeros_like(acc_ref)
            cs_ref[...] = jnp.zeros_like(cs_ref)

        du = jnp.concatenate([r[...] for r in refs[:8]], axis=1)
        acc_ref[...] += _dot(a_ref[...], du, TN)
        cs_ref[...] += _colsum(du.astype(F32))

        @pl.when(kk == nk - 1)
        def _():
            o_ref[...] = acc_ref[...].astype(o_ref.dtype)

    return _hosted_call(
        body, comms, name="wgrad_w_in_%d" % part, grid=(nk,),
        in_specs=[pl.BlockSpec((tk, M), lambda kk: (kk, part))] + _du_specs(tk),
        out_specs=[_full((M, PROJ_WP)), _full((1, PROJ_WP))],
        out_shape=[_sds((M, PROJ_WP), _MXU), _sds((1, PROJ_WP))],
        scratch_shapes=[pltpu.VMEM((M, PROJ_WP), F32)], args=(x, *du_parts))


W_IN_S, FF_S, OUT_S, PP_S = PROJ_W // N_DEV, D_FF // N_DEV, D_MODEL // N_DEV, D_MODEL // N_DEV
LATE = ("w_ffn_gate", "w_ffn_up", "w_out", "w_ffn_down", "ple_w_gate", "ple_w_proj")
BIG = ("w_in",) + LATE
TRANSPOSED = ("w_ffn_gate", "w_ffn_up")


def _split_cols(a, n):
    return a.reshape(a.shape[0], N_DEV, n).transpose(1, 0, 2)


def _join_cols(a):
    return a.transpose(1, 0, 2).reshape(a.shape[1], -1)


def _step(x, p, tgt, w_in, b_in, lb_logits, conv_w, conv_b, g_hg, g_ml, ln1_g, ln1_b, ln2_g, ln2_b, bpg, late,
          distributed):
    T = x.shape[0]
    tm, tf = min(ROWS, T), min(ROWS_FFN, T)
    gather = lambda *names: [_GatherTwoLevel([late[n] for n in names])] if distributed else None
    scatter = lambda *arrs: [_Comm("scatter", list(arrs))] if distributed else None
    rows = lambda a, n: a.reshape(N_DEV, n, D_MODEL)
    (u, pre, qkc), got1 = _in_proj(x, w_in, b_in, conv_w, conv_b, tm, gather("w_out", "ple_w_gate", "ple_w_proj"))
    (o_hg, hg_states), got2 = _hgrn2_fwd(u, lb_logits, gather("w_ffn_gate", "w_ffn_up"))
    (h_ml, cst, nst, mst), got3 = _mlstm_fwd(qkc, u, gather("w_ffn_down"))
    if distributed:
        w_out, wpg, wpp = got1[0][0].reshape(D_MODEL, D_MODEL), got1[0][1].reshape(D_MODEL, D_MODEL), _join_cols(got1[0][2])
        wg, wu, wd = (a.reshape(D_FF, D_MODEL) for a in (got2[0][0], got2[0][1], got3[0][0]))
    else:
        w_out, wg, wu, wd, wpg, wpp = (late[n] for n in ("w_out", "w_ffn_gate", "w_ffn_up", "w_ffn_down", "ple_w_gate", "ple_w_proj"))
    m_in, z1 = _out_proj(x, u, o_hg, h_ml, g_hg, g_ml, w_out, tm)
    z2, x1, x2, a_pre, b_pre, hh = _ffn_ln(z1, ln1_g, ln1_b, wg, wu, wd, ln2_g, ln2_b, tf)
    de, dgp, dz2, loss_vec, d_bpg, d_ln2g, d_ln2b = _ple_loss_ln2_bwd(z2, p, tgt, wpg, bpg, wpp, ln2_g, ln2_b, tm)
    big = dict(ple_w_gate=_wgrad(x2, dgp, "wgrad_ple_gate", D_MODEL, D_MODEL, 1024),
               ple_w_proj=_wgrad(p, de, "wgrad_ple_proj", 512, D_MODEL, 1024))
    (da, dbb, dz1, d_ln1g, d_ln1b), r1 = _ffn_bwd_ln1_bwd(
        a_pre, b_pre, z1, dz2, wg, wu, wd, ln1_g, ln1_b, tf, scatter(rows(big["ple_w_gate"], OUT_S), _split_cols(big["ple_w_proj"], PP_S)))
    big.update(
        w_ffn_gate=_wgrad(da, x1, "wgrad_ffn_gate", D_FF, D_MODEL, 1024),
        w_ffn_up=_wgrad(dbb, x1, "wgrad_ffn_up", D_FF, D_MODEL, 1024),
        w_ffn_down=_wgrad(hh, dz2, "wgrad_ffn_down", D_FF, D_MODEL, 1024),
        w_out=_wgrad(m_in, dz1, "wgrad_w_out", D_MODEL, D_MODEL, 1024))
    d_ohg, d_hml, d_hgate, d_mo, d_ghg, d_gml = _out_proj_bwd(dz1, u, o_hg, h_ml, g_hg, g_ml, w_out, tm)
    (d_hq, d_hf, d_hv, d_lb), r2 = _hgrn2_bwd(
        u, lb_logits, d_ohg, hg_states,
        scatter(rows(big["w_ffn_gate"], FF_S), rows(big["w_ffn_up"], FF_S), rows(big["w_ffn_down"], FF_S),
                rows(big["w_out"], OUT_S)))
    d_mqk, d_mv, d_gates, d_convw, d_convb = _mlstm_bwd(qkc, u, d_hml, cst, nst, mst, pre, conv_w)
    du_parts = [d_hq, d_hf, d_hv, d_hgate, d_mqk, d_mv, d_mo, d_gates]
    own = lambda g: _split_cols(g[:, :PROJ_W], W_IN_S)
    (g_in0, d_bin), _ = _wgrad_w_in(x, du_parts, 512, 0)
    (g_in1, _), r_in0 = _wgrad_w_in(x, du_parts, 512, 1, scatter(own(g_in0)))
    big["w_in"] = jnp.concatenate([g_in0, g_in1], axis=0)
    small = dict(b_in=d_bin[:, :PROJ_W], hg_lb_logits=d_lb, ml_conv_w=d_convw, ml_conv_b=d_convb, hg_norm_g=d_ghg,
                 ml_norm_g=d_gml, ln1_g=d_ln1g, ln1_b=d_ln1b, ln2_g=d_ln2g, ln2_b=d_ln2b, ple_b_gate=d_bpg)
    last = [_Comm("scatter", [own(g_in1)]), _Comm("gather", [loss_vec] + [small[n] for n in SMALL])] if distributed else None
    dx, r3 = _in_proj_bwd(dz1, du_parts, w_in, tm, last)
    gathered_small = None
    if distributed:
        big = dict(ple_w_gate=r1[0][0], ple_w_proj=r1[0][1], w_ffn_gate=r2[0][0], w_ffn_up=r2[0][1],
                   w_ffn_down=r2[0][2], w_out=r2[0][3], w_in=[r_in0[0][0], r3[0][0]])
        gathered_small = r3[1]
    return loss_vec, dx, big, small, gathered_small


SMALL = ("b_in", "hg_lb_logits", "ml_conv_w", "ml_conv_b", "hg_norm_g", "ml_norm_g", "ln1_g", "ln1_b", "ln2_g", "ln2_b",
         "ple_b_gate")


def _padc(a, n):
    return jnp.pad(a, [(0, 0)] * (a.ndim - 1) + [(0, n - a.shape[-1])])


def _adamw(w, g, m, v):
    m = B1 * m + (1.0 - B1) * g
    v = B2 * v + (1.0 - B2) * jnp.square(g)
    m_hat = m / (1.0 - B1 ** STEP)
    v_hat = v / (1.0 - B2 ** STEP)
    return -LR * (m_hat / (jnp.sqrt(v_hat) + EPS) + WD * w), m, v


def _sum_slabs(ref):
    g = ref[0].astype(F32)
    for j in range(1, N_DEV):
        g = g + ref[j].astype(F32)
    return g


def _adamw_matrix(rbs, w, m, v, name):
    rbs = list(rbs) if isinstance(rbs, (list, tuple)) else [rbs]
    nb = len(rbs)
    R, C = w.shape
    tr = 256 if (R // nb) % 256 == 0 else R // nb
    per = R // nb // tr

    def body(*refs):
        w_ref, m_ref, v_ref, g_ref, d_ref, m2_ref, v2_ref = refs[nb:]
        i = pl.program_id(0)
        g = _sum_slabs(refs[0])
        for k in range(1, nb):
            g = jnp.where(i >= k * per, _sum_slabs(refs[k]), g)
        g_ref[...] = g
        d_ref[...], m2_ref[...], v2_ref[...] = _adamw(w_ref[...], g, m_ref[...], v_ref[...])

    blk = pl.BlockSpec((tr, C), lambda i: (i, 0))
    part = lambda k: pl.BlockSpec((N_DEV, tr, C), lambda i, k=k: (0, jnp.clip(i - k * per, 0, per - 1), 0))
    return pl.pallas_call(
        body, name=name, grid=(R // tr,),
        in_specs=[part(k) for k in range(nb)] + [blk, blk, blk],
        out_specs=[blk] * 4, out_shape=[_sds((R, C))] * 4, compiler_params=_cp("parallel"))(*rbs, w, m, v)


def _adamw_small(loss_g, gs, ws, ms, vs):
    n = len(ws)

    def body(*refs):
        loss_ref, g_refs, w_refs, m_refs, v_refs = refs[0], refs[1:1 + n], refs[1 + n:1 + 2 * n], refs[1 + 2 * n:1 + 3 * n], refs[1 + 3 * n:1 + 4 * n]
        outs = refs[1 + 4 * n:]
        outs[0][...] = (0.5 / D_MODEL) * jnp.sum(_sum_slabs(loss_ref), keepdims=True)
        for i in range(n):
            g = _sum_slabs(g_refs[i])
            outs[1 + i][...] = g
            outs[1 + n + i][...], outs[1 + 2 * n + i][...], outs[1 + 3 * n + i][...] = _adamw(
                w_refs[i][...], g, m_refs[i][...], v_refs[i][...])

    res = pl.pallas_call(
        body, name="adamw_small", out_shape=[_sds((1, 1))] + [_sds(w.shape) for w in ws] * 4)(loss_g, *gs, *ws, *ms, *vs)
    return res[0], [res[1 + k * n:1 + (k + 1) * n] for k in range(4)]


WEIGHTS = ("w_in", "b_in", "hg_lb_logits", "ml_conv_w", "ml_conv_b", "hg_norm_g", "ml_norm_g", "w_out", "ln1_g", "ln1_b",
           "w_ffn_gate", "w_ffn_up", "w_ffn_down", "ln2_g", "ln2_b", "ple_w_proj", "ple_w_gate", "ple_b_gate")
CONV_S = HALF // N_DEV


def kernel(x, p, w_in, b_in, hg_lb_logits, ml_conv_w, ml_conv_b, hg_norm_g, ml_norm_g, w_out, ln1_g, ln1_b, w_ffn_gate, w_ffn_up, w_ffn_down, ln2_g, ln2_b, ple_w_proj, ple_w_gate, ple_b_gate, loss_target, m_w_in, m_b_in, m_hg_lb_logits, m_ml_conv_w, m_ml_conv_b, m_hg_norm_g, m_ml_norm_g, m_w_out, m_ln1_g, m_ln1_b, m_w_ffn_gate, m_w_ffn_up, m_w_ffn_down, m_ln2_g, m_ln2_b, m_ple_w_proj, m_ple_w_gate, m_ple_b_gate, v_w_in, v_b_in, v_hg_lb_logits, v_ml_conv_w, v_ml_conv_b, v_hg_norm_g, v_ml_norm_g, v_w_out, v_ln1_g, v_ln1_b, v_w_ffn_gate, v_w_ffn_up, v_w_ffn_down, v_ln2_g, v_ln2_b, v_ple_w_proj, v_ple_w_gate, v_ple_b_gate):
    args = locals()
    me = 4 * lax.axis_index("x") + 2 * lax.axis_index("y") + lax.axis_index("c")
    shapes = {n: args[n].shape for n in WEIGHTS}
    def drop(n, a):
        a = a[0] if n in BIG or n == "ml_conv_w" else a
        return a.T if n in TRANSPOSED else a

    W = {n: drop(n, args[n]) for n in WEIGHTS}
    M = {n: drop(n, args["m_" + n]) for n in WEIGHTS}
    V = {n: drop(n, args["v_" + n]) for n in WEIGHTS}

    g_in, g_conv = _gather_two_level(
        [W["w_in"].astype(_MXU), jnp.pad(W["ml_conv_w"], ((0, 4), (0, 128 - CONV_S)))], "gather_w_in")
    w_in_full = _padc(_join_cols(g_in), PROJ_WP)
    conv_full = _join_cols(g_conv[:, :4, :CONV_S])

    _, dx, big, _, sg = _step(
        x[0], p[0, 0], loss_target[0], w_in_full, _padc(b_in, PROJ_WP), hg_lb_logits, conv_full, ml_conv_b,
        hg_norm_g, ml_norm_g, ln1_g, ln1_b, ln2_g, ln2_b, ple_b_gate, {n: W[n].astype(_MXU) for n in LATE}, True)

    upd = {n: _adamw_matrix(big[n], W[n], M[n], V[n], "adamw_" + n) for n in BIG}
    sg = dict(zip(SMALL, sg[1:]), loss=sg[0])
    sg["ml_conv_w"] = lax.dynamic_slice(sg["ml_conv_w"], (0, 0, me * CONV_S), (N_DEV, 4, CONV_S))
    loss, small_upd = _adamw_small(sg["loss"], *[[d[n] for n in SMALL] for d in (sg, W, M, V)])

    outs = []
    for kind in range(4):
        smalls = dict(zip(SMALL, small_upd[kind]))
        for n in WEIGHTS:
            o = upd[n][kind] if n in BIG else smalls[n]
            outs.append((o.T if n in TRANSPOSED else o).reshape(shapes[n]))
    return (loss.reshape(()), dx.reshape(x.shape), *outs)
```

```python
import jax
import jax.numpy as jnp
from jax import lax
from jax.experimental import pallas as pl
from jax.experimental.pallas import tpu as pltpu

F32 = jnp.float32
_MXU = jnp.bfloat16

D_MODEL = 1024
CHUNK = 64
SUB = 16
PLE_DIM = 256
HEADS = 4
ML_DQK = 64
HALF = 512
D_FF = 2816
PROJ_W = 3592
PROJ_WP = 3712
ALPHA = float(2 ** 0.25)
LN_EPS = 1e-5
RMS_EPS = 1e-6
ML_SCALE = ML_DQK ** -0.5
N_DEV = 8
LR, B1, B2, EPS, WD, STEP = 0.001, 0.9, 0.999, 1e-08, 0.01, 10
NEG = -1e30
LOG2E = 1.4426950408889634

C_HQ, C_HF, C_HV, C_HGATE, C_MQK, C_MV, C_MO, C_GATES = 0, 4, 8, 12, 16, 20, 24, 28
DU_WIDTHS = (HALF,) * 7 + (128,)

VMEM_LIMIT = 52 * 1024 * 1024
GC = 8
ROWS = 512
ROWS_FFN = 256

NN = (((1,), (0,)), ((), ()))
NT = (((1,), (1,)), ((), ()))
TN = (((0,), (0,)), ((), ()))
BNT = (((2,), (2,)), ((0,), (0,)))
BNN = (((2,), (1,)), ((0,), (0,)))
BTN = (((1,), (1,)), ((0,), (0,)))


def _dot(a, b, dims=NN):
    return lax.dot_general(a.astype(_MXU), b.astype(_MXU), dims, preferred_element_type=F32)


def _sig(x):
    return jax.nn.sigmoid(x)


def _cp(*sem):
    return pltpu.CompilerParams(dimension_semantics=sem, vmem_limit_bytes=VMEM_LIMIT)


def _row(tm, c, blk=0):
    return pl.BlockSpec((tm, c), lambda i, blk=blk: (i, blk))


def _full(shape):
    nd = len(shape)
    return pl.BlockSpec(tuple(shape), lambda *_, nd=nd: (0,) * nd)


def _sds(shape, dtype=F32):
    return jax.ShapeDtypeStruct(tuple(shape), dtype)


def _iota(shape, axis):
    return lax.broadcasted_iota(jnp.int32, shape, axis)


def _colsum(x):
    return jnp.sum(x, axis=0, keepdims=True)


def _rowsum(x):
    return jnp.sum(x, axis=1, keepdims=True)


def _ln_fwd(z, g, b):
    mu = jnp.mean(z, axis=-1, keepdims=True)
    zc = z - mu
    var = jnp.mean(zc * zc, axis=-1, keepdims=True)
    rstd = lax.rsqrt(var + LN_EPS)
    xhat = zc * rstd
    return xhat * g + b, xhat, rstd


def _ln_bwd(dy, xhat, rstd, g):
    dxh = dy * g
    m1 = jnp.mean(dxh, axis=-1, keepdims=True)
    m2 = jnp.mean(dxh * xhat, axis=-1, keepdims=True)
    return rstd * (dxh - m1 - xhat * m2)


def _dsilu(x, s):
    return s * (1.0 + x * (1.0 - s))


MESH = pl.DeviceIdType.MESH
ANY = pl.BlockSpec(memory_space=pl.ANY)


def _flip(v, bit):
    return 1 - v if bit else v


class _Comm:
    def __init__(self, kind, srcs):
        self.kind, self.srcs, self.n = kind, list(srcs), len(srcs)

    def out_shape(self):
        lead = (N_DEV,) if self.kind == "gather" else ()
        return [jax.ShapeDtypeStruct(lead + s.shape, s.dtype) for s in self.srcs]

    def scratch(self):
        return [pltpu.SemaphoreType.DMA((7 * self.n,)), pltpu.SemaphoreType.DMA((7 * self.n,)),
                pltpu.SemaphoreType.DMA((self.n,))]

    def copies(self, srcs, dsts, send_sems, recv_sems, local_sems):
        x, y, c = lax.axis_index("x"), lax.axis_index("y"), lax.axis_index("c")
        me = 4 * x + 2 * y + c
        pick = (lambda s, j: s) if self.kind == "gather" else (lambda s, j: s.at[j])
        out = []
        for i, (s, d) in enumerate(zip(srcs, dsts)):
            out.append(pltpu.make_async_copy(pick(s, me), d.at[me], local_sems.at[i]))
            for k in range(1, N_DEV):
                px, py, pc = _flip(x, k & 4), _flip(y, k & 2), _flip(c, k & 1)
                out.append(pltpu.make_async_remote_copy(
                    src_ref=pick(s, 4 * px + 2 * py + pc), dst_ref=d.at[me], send_sem=send_sems.at[7 * i + k - 1],
                    recv_sem=recv_sems.at[7 * i + k - 1], device_id=(px, py, pc), device_id_type=MESH))
        return out

    def start(self, *refs):
        for cp in self.copies(*refs):
            cp.start()

    def mid(self, *refs):
        pass

    def finish(self, *refs):
        for cp in self.copies(*refs):
            cp.wait()


class _GatherTwoLevel(_Comm):
    def __init__(self, srcs):
        super().__init__("gather", srcs)

    def _parts(self, srcs, dsts, send_sems, recv_sems, local_sems):
        x, y, c = lax.axis_index("x"), lax.axis_index("y"), lax.axis_index("c")
        me, sibling = (x, y, c), (x, y, 1 - c)
        chips = [(1 - x, y), (x, 1 - y), (1 - x, 1 - y)]

        def copy(i, k, block, to, own=False):
            slab = dsts[i].at[4 * block[0] + 2 * block[1] + block[2]]
            return pltpu.make_async_remote_copy(
                src_ref=srcs[i] if own else slab, dst_ref=slab, send_sem=send_sems.at[7 * i + k],
                recv_sem=recv_sems.at[7 * i + k], device_id=to, device_id_type=MESH)

        n = range(self.n)
        mine = [pltpu.make_async_copy(srcs[i], dsts[i].at[4 * x + 2 * y + c], local_sems.at[i]) for i in n]
        first = [copy(i, 0, me, sibling, own=True) for i in n]
        first += [copy(i, 1 + j, me, (*chip, c), own=True) for j, chip in enumerate(chips) for i in n]
        over_ici = [copy(i, 1 + j, (*chip, c), me) for j, chip in enumerate(chips) for i in n]
        passed = [copy(i, 4 + j, (*chip, c), sibling) for j, chip in enumerate(chips) for i in n]
        from_sibling = [copy(i, 0, sibling, me) for i in n]
        from_sibling += [copy(i, 4 + j, (*chip, 1 - c), me) for j, chip in enumerate(chips) for i in n]
        return mine, first, over_ici, passed, from_sibling

    def start(self, *refs):
        mine, first, _, _, _ = self._parts(*refs)
        for cp in mine + first:
            cp.start()

    def mid(self, *refs):
        _, _, over_ici, passed, _ = self._parts(*refs)
        for arrived, onward in zip(over_ici, passed):
            arrived.wait_recv()
            onward.start()

    def finish(self, *refs):
        mine, first, _, passed, from_sibling = self._parts(*refs)
        for cp in from_sibling:
            cp.wait_recv()
        for cp in first + passed:
            cp.wait_send()
        for cp in mine:
            cp.wait()


def _hosted_call(body, comms, *, name, grid, in_specs, out_specs, out_shape, scratch_shapes, args):
    comms = list(comms or [])
    if not comms:
        res = pl.pallas_call(body, name=name, grid=grid, in_specs=in_specs, out_specs=out_specs, out_shape=out_shape,
                             scratch_shapes=scratch_shapes, compiler_params=_cp("arbitrary"))(*args)
        return list(res), []
    n_in, n_out, n_sc, nc = len(in_specs), len(out_specs), len(scratch_shapes), sum(cm.n for cm in comms)
    last = grid[0] - 1

    def hosted(*refs):
        ins, csrc = refs[:n_in], refs[n_in:n_in + nc]
        o0 = n_in + nc
        outs, cdst = refs[o0:o0 + n_out], refs[o0 + n_out:o0 + n_out + nc]
        s0 = o0 + n_out + nc
        scr, sems = refs[s0:s0 + n_sc], refs[s0 + n_sc:]

        def phase(which):
            o = 0
            for j, cm in enumerate(comms):
                getattr(cm, which)(csrc[o:o + cm.n], cdst[o:o + cm.n], *sems[3 * j:3 * j + 3])
                o += cm.n

        i = pl.program_id(0)

        @pl.when(i == 0)
        def _():
            phase("start")

        body(*ins, *outs, *scr)

        @pl.when(i == (2 * last) // 3)
        def _():
            phase("mid")

        @pl.when(i == last)
        def _():
            phase("finish")

    res = pl.pallas_call(
        hosted, name=name, grid=grid, in_specs=list(in_specs) + [ANY] * nc, out_specs=list(out_specs) + [ANY] * nc,
        out_shape=list(out_shape) + [s for cm in comms for s in cm.out_shape()],
        scratch_shapes=list(scratch_shapes) + [s for cm in comms for s in cm.scratch()],
        compiler_params=_cp("arbitrary"))(*args, *[a for cm in comms for a in cm.srcs])
    got, o = [], n_out
    for cm in comms:
        got.append(list(res[o:o + cm.n]))
        o += cm.n
    return list(res[:n_out]), got


def _gather_two_level(blocks, name):
    n = len(blocks)

    def body(*refs):
        x_refs, out_refs = refs[:n], refs[n:2 * n]
        send_sems, recv_sems, local_sems = refs[2 * n:]
        x, y, c = lax.axis_index("x"), lax.axis_index("y"), lax.axis_index("c")
        me, sibling = (x, y, c), (x, y, 1 - c)
        chips = [(1 - x, y), (x, 1 - y), (1 - x, 1 - y)]

        def copy(i, k, block, to, own=False):
            slab = out_refs[i].at[4 * block[0] + 2 * block[1] + block[2]]
            return pltpu.make_async_remote_copy(
                src_ref=x_refs[i] if own else slab, dst_ref=slab, send_sem=send_sems.at[7 * i + k],
                recv_sem=recv_sems.at[7 * i + k], device_id=to, device_id_type=MESH)

        mine = [pltpu.make_async_copy(x_refs[i], out_refs[i].at[4 * x + 2 * y + c], local_sems.at[i]) for i in range(n)]
        for cp in mine:
            cp.start()
        first = [copy(i, 0, me, sibling, own=True) for i in range(n)]
        first += [copy(i, 1 + j, me, (*chip, c), own=True) for j, chip in enumerate(chips) for i in range(n)]
        for cp in first:
            cp.start()
        passed = []
        for j, chip in enumerate(chips):
            for i in range(n):
                copy(i, 1 + j, (*chip, c), me).wait_recv()
                passed.append(copy(i, 4 + j, (*chip, c), sibling))
                passed[-1].start()
        for i in range(n):
            copy(i, 0, sibling, me).wait_recv()
            for j, chip in enumerate(chips):
                copy(i, 4 + j, (*chip, 1 - c), me).wait_recv()
        for cp in first + passed:
            cp.wait_send()
        for cp in mine:
            cp.wait()

    return pl.pallas_call(
        body, name=name, out_shape=[jax.ShapeDtypeStruct((N_DEV,) + b.shape, b.dtype) for b in blocks],
        in_specs=[ANY] * n, out_specs=[ANY] * n,
        scratch_shapes=[pltpu.SemaphoreType.DMA((7 * n,)), pltpu.SemaphoreType.DMA((7 * n,)),
                        pltpu.SemaphoreType.DMA((n,))])(*blocks)


def _in_proj(x, w, b, cw, cb, tm, comms=None):
    T = x.shape[0]

    def body(x_ref, w_ref, b_ref, cw_ref, cb_ref, o_ref, pre_ref, act_ref, halo_sc):
        @pl.when(pl.program_id(0) == 0)
        def _():
            halo_sc[...] = jnp.zeros_like(halo_sc)

        o = _dot(x_ref[...], w_ref[...]) + b_ref[...]
        o_ref[...] = o
        xc = o[:, 128 * C_MQK:128 * C_MQK + HALF]
        halo = halo_sc[...]
        rowi = _iota((8, HALF), 0)
        acc = xc * cw_ref[3:4, :] + cb_ref[...]
        for j in (1, 2, 3):
            acc = acc + _shift_rows(xc, halo, j, rowi) * cw_ref[3 - j:4 - j, :]
        pre_ref[...] = acc
        act_ref[...] = acc * _sig(acc)
        halo_sc[...] = xc[tm - 8:]

    return _hosted_call(
        body, comms, name="in_proj", grid=(T // tm,),
        in_specs=[_row(tm, D_MODEL), _full(w.shape), _full(b.shape), _full(cw.shape), _full(cb.shape)],
        out_specs=[_row(tm, PROJ_WP), _row(tm, HALF), _row(tm, HALF)],
        out_shape=[_sds((T, PROJ_WP)), _sds((T, HALF)), _sds((T, HALF))],
        scratch_shapes=[pltpu.VMEM((8, HALF), F32)], args=(x, w, b, cw, cb))


def _shift_rows(x, halo, j, rowi):
    r = pltpu.roll(x, j, 0)
    top = jnp.where(rowi < j, pltpu.roll(halo, j, 0), r[:8])
    return jnp.concatenate([top, r[8:]], axis=0)


def _shift_rows_up(x, halo, j, rowi):
    n = x.shape[0]
    r = pltpu.roll(x, n - j, 0)
    bot = jnp.where(rowi >= 8 - j, pltpu.roll(halo, 8 - j, 0), r[n - 8:])
    return jnp.concatenate([r[:n - 8], bot], axis=0)


def _bdot(a, b, dims):
    return lax.dot_general(a.astype(_MXU), b.astype(_MXU), dims, preferred_element_type=F32)


def _bdotx(a, b, dims):
    return lax.dot_general(a, b, dims, precision=lax.Precision.HIGHEST, preferred_element_type=F32)


def _heads_to_batch(x, w):
    G = x.shape[0] // CHUNK
    x3 = x.reshape(G, CHUNK, HEADS * w)
    return jnp.stack([x3[:, :, w * h:w * (h + 1)] for h in range(HEADS)], axis=1).reshape(G * HEADS, CHUNK, w)


def _batch_to_heads(x3):
    B, _, w = x3.shape
    x4 = x3.reshape(B // HEADS, HEADS, CHUNK, w)
    return jnp.concatenate([x4[:, h] for h in range(HEADS)], axis=-1).reshape(B // HEADS * CHUNK, HEADS * w)


def _chunk_cumsum(x, rowmod, reverse=False):
    R = x.shape[0]
    for sh in (1, 2, 4, 8, 16, 32):
        if reverse:
            x = x + jnp.where(rowmod < CHUNK - sh, pltpu.roll(x, R - sh, 0), 0.0)
        else:
            x = x + jnp.where(rowmod >= sh, pltpu.roll(x, sh, 0), 0.0)
    return x


def _lane_col(x, c, lane):
    return _rowsum(jnp.where(lane == c, x, 0.0))


def _hg_gates(hq, hf, lb):
    sg = _sig(hf)
    nsg = _sig(-hf)
    f = lb + (1.0 - lb) * sg
    g = jnp.log(f)
    k = (1.0 - lb) * nsg
    sq = _sig(hq)
    return hq * sq, g, k, f, sg, nsg, sq


def _hg_prep(hq_ref, hf_ref, lg_ref, b_sc, k_sc):
    R = hq_ref.shape[0]
    G = R // CHUNK
    lb = _sig(lg_ref[0:1, :] - lg_ref[1:2, :])
    hq = hq_ref[...]
    q, g, k, f, sg, nsg, sq = _hg_gates(hq, hf_ref[...], lb)
    rowmod = _iota((R, HALF), 0) & (CHUNK - 1)
    b = _chunk_cumsum(g, rowmod) * LOG2E
    last8 = _iota((8, HALF), 0) == 7
    bl_rows = [_colsum(jnp.where(last8, b[CHUNK * c + CHUNK - 8:CHUNK * (c + 1)], 0.0)) for c in range(G)]
    bl3 = jnp.stack([r[:, 128 * h:128 * (h + 1)] for r in bl_rows for h in range(HEADS)], axis=0)
    b3, k3 = _heads_to_batch(b, 128), _heads_to_batch(k, 128)
    b_sc[...] = b3
    k_sc[...] = k3
    return dict(G=G, lb=lb, hq=hq, f=f, sg=sg, nsg=nsg, sq=sq, rowmod=rowmod, q3=_heads_to_batch(q, 128), k3=k3, b3=b3,
                bl3=bl3)


HSUB = 8


def _lo(j):
    return HSUB * (j // HSUB)


def _hg_diag_tiles(b_sc, b3, r0, rowi):
    bi = b3[:, r0:r0 + SUB]
    return [jnp.exp2(jnp.where(rowi[:, _lo(s):] >= s, bi[:, _lo(s):] - b_sc[:, r0 + s:r0 + s + 1, :], NEG))
            for s in range(SUB)]


def _hg_diag_tiles_t(b_sc, b3, r0, rowi):
    bi = b3[:, r0:r0 + SUB]
    return [jnp.exp2(jnp.where(rowi[:, :_lo(t) + HSUB] <= t, b_sc[:, r0 + t:r0 + t + 1, :] - bi[:, :_lo(t) + HSUB], NEG))
            for t in range(SUB)]


def _lane_sums(pieces, ones):
    B = pieces[0].shape[0]
    hs = [p.shape[1] for p in pieces]
    R = _dot(jnp.concatenate(pieces, axis=1).reshape(B * sum(hs), 128), ones).reshape(B, sum(hs), 128)
    out, o = [], 0
    for h in hs:
        out.append(R[:, o:o + h])
        o += h
    return out


def _sum_tri(terms, low_rows):
    if SUB == HSUB:
        return sum(terms)
    full = sum(t for t in terms if t.shape[1] == SUB)
    half = sum(t for t in terms if t.shape[1] == HSUB)
    lo, hi = full[:, :HSUB], full[:, HSUB:]
    return jnp.concatenate([lo + half, hi] if low_rows else [lo, hi + half], axis=1)


def _hgrn2_fwd(u, lb_logits, comms=None):
    T = u.shape[0]
    G = min(GC, T // CHUNK)
    R, B, N = G * CHUNK, G * HEADS, T // CHUNK

    def body(hq_ref, hf_ref, hv_ref, lg_ref, o_ref, st_ref, S_ref, b_sc, k_sc, v_sc):
        @pl.when(pl.program_id(0) == 0)
        def _():
            S_ref[...] = jnp.zeros_like(S_ref)

        pz = _hg_prep(hq_ref, hf_ref, lg_ref, b_sc, k_sc)
        q3, k3, b3, bl3 = pz["q3"], pz["k3"], pz["b3"], pz["bl3"]
        v3 = _heads_to_batch(hv_ref[...], 128)
        v_sc[...] = v3
        stloc = _bdot(v3, k3 * jnp.exp2(bl3 - b3), BTN).reshape(G, HEADS, 128, 128)
        dec = jnp.exp2(bl3).reshape(G, HEADS, 1, 128)
        ST = S_ref[...]
        sts = []
        for c in range(G):
            sts.append(ST)
            ST = ST * dec[c] + stloc[c]
        S_ref[...] = ST
        st4 = jnp.stack(sts, axis=0)
        st_ref[...] = st4
        o = _bdot(q3 * jnp.exp2(b3), st4.reshape(B, 128, 128), BNT)
        ones = jnp.ones((128, 128), F32)
        rowi = _iota((1, SUB, 128), 1)
        outs = []
        for i in range(CHUNK // SUB):
            r0 = SUB * i
            qi = q3[:, r0:r0 + SUB]
            oi = o[:, r0:r0 + SUB]
            if i > 0:
                r = b_sc[:, r0 - 1:r0, :]
                qe = qi * jnp.exp2(b3[:, r0:r0 + SUB] - r)
                ke = k3[:, :r0] * jnp.exp2(r - b3[:, :r0])
                oi = oi + _bdot(_bdot(qe, ke, BNT), v3[:, :r0], BNN)
            tiles = _hg_diag_tiles(b_sc, b3, r0, rowi)
            a_b = _lane_sums([qi[:, _lo(s):] * (k_sc[:, r0 + s:r0 + s + 1, :] * tiles[s]) for s in range(SUB)], ones)
            outs.append(oi + _sum_tri([a_b[s] * v_sc[:, r0 + s:r0 + s + 1, :] for s in range(SUB)], False))
        o_ref[...] = _batch_to_heads(jnp.concatenate(outs, axis=1))

    blk = lambda c: pl.BlockSpec((R, HALF), lambda n, c=c: (n, c // 4))
    return _hosted_call(
        body, comms, name="hgrn2_fwd", grid=(N // G,),
        in_specs=[blk(C_HQ), blk(C_HF), blk(C_HV), _full(lb_logits.shape)],
        out_specs=[pl.BlockSpec((R, HALF), lambda n: (n, 0)),
                   pl.BlockSpec((G, HEADS, 128, 128), lambda n: (n, 0, 0, 0))],
        out_shape=[_sds((T, HALF)), _sds((N, HEADS, 128, 128))],
        scratch_shapes=[pltpu.VMEM((HEADS, 128, 128), F32)] + [pltpu.VMEM((B, CHUNK, 128), F32)] * 3,
        args=(u, u, u, lb_logits))


def _hgrn2_bwd(u, lb_logits, do, states, comms=None):
    T = u.shape[0]
    G = min(GC, T // CHUNK)
    R, B, NG = G * CHUNK, G * HEADS, T // (G * CHUNK)

    def body(hq_ref, hf_ref, hv_ref, lg_ref, do_ref, st_ref, dhq_ref, dhf_ref, dhv_ref, dlb_ref,
             dS_ref, b_sc, k_sc, v_sc, q_sc, do_sc):
        @pl.when(pl.program_id(0) == 0)
        def _():
            dS_ref[...] = jnp.zeros_like(dS_ref)
            dlb_ref[...] = jnp.zeros_like(dlb_ref)

        pz = _hg_prep(hq_ref, hf_ref, lg_ref, b_sc, k_sc)
        q3, k3, b3, bl3, lb = pz["q3"], pz["k3"], pz["b3"], pz["bl3"], pz["lb"]
        v3 = _heads_to_batch(hv_ref[...], 128)
        v_sc[...] = v3
        do3 = _heads_to_batch(do_ref[...], 128)
        q_sc[...] = q3
        do_sc[...] = do3
        st3 = st_ref[...].reshape(B, 128, 128)
        eb = jnp.exp2(b3)
        ebl = jnp.exp2(bl3 - b3)
        qt = q3 * eb
        kl = k3 * ebl
        dstloc = _bdot(do3, qt, BTN).reshape(G, HEADS, 128, 128)
        dec = jnp.exp2(bl3).reshape(G, HEADS, 1, 128)
        dST = dS_ref[...]
        dsts = [None] * G
        for c in reversed(range(G)):
            dsts[c] = dST
            dST = dST * dec[c] + dstloc[c]
        dS_ref[...] = dST
        dst3 = jnp.stack(dsts, axis=0).reshape(B, 128, 128)
        dqt = _bdot(do3, st3, BNN)
        dkl = _bdot(v3, dst3, BNN)
        dv_acc = _bdot(kl, dst3, BNT)
        ones = jnp.ones((128, 128), F32)
        rowi = _iota((1, SUB, 128), 1)
        dq_parts, dk_parts, dv_parts = [], [], []
        dk_in = jnp.zeros((B, CHUNK, 128), F32)
        for i_s in range(CHUNK // SUB):
            r0 = SUB * i_s
            qi = q3[:, r0:r0 + SUB]
            doi = do3[:, r0:r0 + SUB]
            dqi = jnp.zeros((B, SUB, 128), F32)
            if i_s > 0:
                r = b_sc[:, r0 - 1:r0, :]
                eq = jnp.exp2(b3[:, r0:r0 + SUB] - r)
                ek = jnp.exp2(r - b3[:, :r0])
                qe = qi * eq
                ke = k3[:, :r0] * ek
                a_off = _bdot(qe, ke, BNT)
                p_off = _bdot(doi, v3[:, :r0], BNT)
                pad = jnp.zeros((B, CHUNK - r0, 128), F32)
                dv_acc = dv_acc + jnp.concatenate([_bdot(a_off, doi, BTN), pad], axis=1)
                dqi = dqi + _bdot(p_off, ke, BNN) * eq
                dk_in = dk_in + jnp.concatenate([_bdot(p_off, qe, BTN) * ek, pad], axis=1)
            ki, vi = k3[:, r0:r0 + SUB], v3[:, r0:r0 + SUB]
            rng = range(SUB)
            tiles = _hg_diag_tiles(b_sc, b3, r0, rowi)
            tiles_t = _hg_diag_tiles_t(b_sc, b3, r0, rowi)
            do_rows = [do_sc[:, r0 + t:r0 + t + 1, :] for t in rng]
            kts = [k_sc[:, r0 + s:r0 + s + 1, :] * tiles[s] for s in rng]
            qts = [q_sc[:, r0 + t:r0 + t + 1, :] * tiles_t[t] for t in rng]
            ps = [doi[:, _lo(s):] * v_sc[:, r0 + s:r0 + s + 1, :] for s in rng]
            mst = [ki[:, :_lo(t) + HSUB] * qts[t] for t in rng]
            pst = [vi[:, :_lo(t) + HSUB] * do_rows[t] for t in rng]
            sums = _lane_sums(ps + mst + pst, ones)
            p_b, a_t, p_t = sums[:SUB], sums[SUB:2 * SUB], sums[2 * SUB:]
            dq_parts.append(dqi + _sum_tri([p_b[s] * kts[s] for s in rng], False))
            dv_parts.append(_sum_tri([a_t[t] * do_rows[t] for t in rng], True))
            dk_parts.append(_sum_tri([p_t[t] * qts[t] for t in rng], True))
        dq_in = jnp.concatenate(dq_parts, axis=1)
        dk_in = dk_in + jnp.concatenate(dk_parts, axis=1)
        dv_acc = dv_acc + jnp.concatenate(dv_parts, axis=1)
        db = qt * dqt + q3 * dq_in - k3 * dk_in - kl * dkl
        last = jnp.sum(kl * dkl, axis=1, keepdims=True) + jnp.exp2(bl3) * jnp.sum(st3 * dst3, axis=1, keepdims=True)
        db = db + jnp.where(_iota((1, CHUNK, 1), 1) == CHUNK - 1, last, 0.0)
        dg = _chunk_cumsum(_batch_to_heads(db), pz["rowmod"], reverse=True)
        dq_tot = _batch_to_heads(dqt * eb + dq_in)
        dk_tot = _batch_to_heads(dkl * ebl + dk_in)
        common = dg / pz["f"] - dk_tot
        dhf_ref[...] = ((1.0 - lb) * pz["sg"] * pz["nsg"] * common).astype(dhf_ref.dtype)
        dl0 = _colsum(pz["nsg"] * common) * lb * (1.0 - lb)
        dlb_ref[0:1, :] += dl0
        dlb_ref[1:2, :] -= dl0
        dhq_ref[...] = (dq_tot * _dsilu(pz["hq"], pz["sq"])).astype(dhq_ref.dtype)
        dhv_ref[...] = _batch_to_heads(dv_acc).astype(dhv_ref.dtype)

    rev = lambda c: pl.BlockSpec((R, HALF), lambda i, c=c: (NG - 1 - i, c // 4))
    rev0 = pl.BlockSpec((R, HALF), lambda i: (NG - 1 - i, 0))
    return _hosted_call(
        body, comms, name="hgrn2_bwd", grid=(NG,),
        in_specs=[rev(C_HQ), rev(C_HF), rev(C_HV), _full(lb_logits.shape), rev0,
                  pl.BlockSpec((G, HEADS, 128, 128), lambda i: (NG - 1 - i, 0, 0, 0))],
        out_specs=[rev0, rev0, rev0, _full((2, HALF))],
        out_shape=[_sds((T, HALF), _MXU)] * 3 + [_sds((2, HALF))],
        scratch_shapes=[pltpu.VMEM((HEADS, 128, 128), F32)] + [pltpu.VMEM((B, CHUNK, 128), F32)] * 5,
        args=(u, u, u, lb_logits, do, states))


def _lanes_to_batch_cols(x, lane):
    G = x.shape[0] // CHUNK
    cols = [_lane_col(x, 4 + h, lane).reshape(G, CHUNK, 1) for h in range(HEADS)]
    return jnp.stack(cols, axis=1).reshape(G * HEADS, CHUNK, 1)


def _row_scalars(rows):
    lane = _iota((1, 128), 1)
    return jnp.stack([_rowsum(jnp.where(lane == 4 + h, r, 0.0)) for r in rows for h in range(HEADS)], axis=0)


def _ml_gates(gates):
    R = gates.shape[0]
    lane = _iota((R, 128), 1)
    rowmod = _iota((R, 128), 0) & (CHUNK - 1)
    lf = jnp.minimum(gates, 0.0) - jnp.log(1.0 + jnp.exp(-jnp.abs(gates)))
    g_all = _chunk_cumsum(lf, rowmod)
    x_all = pltpu.roll(gates, 4, 1) - g_all
    return g_all, x_all, lane, rowmod


def _ml_chunk_rows(g_all, x_all, mprev, g):
    gl = g_all[CHUNK * g + CHUNK - 8:CHUNK * (g + 1)]
    gl = _colsum(jnp.where(_iota((8, 128), 0) == 7, gl, 0.0))
    a = gl + x_all[CHUNK * g:CHUNK * (g + 1)]
    m_new = jnp.maximum(gl + mprev, jnp.max(a, axis=0, keepdims=True))
    return m_new, jnp.exp(gl + mprev - m_new), jnp.exp(a - m_new)


def _ml_batched(q3, k3, v3, g_all, x_all, lane, C3, n3, mprev3):
    G = g_all.shape[0] // CHUNK
    gcol3 = _lanes_to_batch_cols(g_all, lane)
    onehot = jnp.where(_iota((G, 8, 128), 1) + 4 == _iota((G, 8, 128), 2), 1.0, 0.0).astype(F32)
    rows = _bdotx(onehot, x_all.reshape(G, CHUNK, 128), BNT)
    sub = _iota((G, 8, CHUNK), 1)
    row3 = jnp.stack([jnp.sum(jnp.where(sub == h, rows, 0.0), axis=1, keepdims=True) for h in range(HEADS)],
                     axis=1).reshape(G * HEADS, 1, CHUNK)
    causal = _iota((1, CHUNK, CHUNK), 1) >= _iota((1, CHUNK, CHUNK), 2)
    dmat = jnp.where(causal, gcol3 + row3, NEG)
    m_inter = gcol3 + mprev3
    m_t = jnp.maximum(m_inter, jnp.max(dmat, axis=2, keepdims=True))
    wi = jnp.exp(dmat - m_t)
    wn = jnp.exp(m_inter - m_t)
    s3 = _bdot(q3, k3, BNT) * wi
    qc = _bdot(q3, C3, BNN)
    qn = jnp.sum(q3 * n3, axis=2, keepdims=True)
    num = _bdot(s3, v3, BNN) + wn * qc
    den = jnp.sum(s3, axis=2, keepdims=True) + wn * qn
    floor = jnp.exp(-m_t)
    return dict(wi=wi, wn=wn, s=s3, qc=qc, qn=qn, num=num, den=den, floor=floor, nrm=jnp.maximum(jnp.abs(den), floor))


def _mlstm_fwd(qkc, u, comms=None):
    T = u.shape[0]
    G = min(GC, T // CHUNK)
    R = G * CHUNK
    N = T // CHUNK

    def body(qk_ref, v_ref, g_ref, h_ref, cst_ref, nst_ref, mst_ref, C_ref, n_ref, m_ref):
        @pl.when(pl.program_id(0) == 0)
        def _():
            C_ref[...] = jnp.zeros_like(C_ref)
            n_ref[...] = jnp.zeros_like(n_ref)
            m_ref[...] = jnp.zeros_like(m_ref)

        g_all, x_all, lane, _ = _ml_gates(g_ref[...])
        m_row = m_ref[...]
        mprev_rows, wo_rows, ws_parts = [], [], []
        for g in range(G):
            mprev_rows.append(m_row)
            m_row, wo, ws = _ml_chunk_rows(g_all, x_all, m_row, g)
            wo_rows.append(wo)
            ws_parts.append(ws)
        m_ref[...] = m_row
        mst_ref[...] = jnp.stack(mprev_rows, axis=0)
        ws3 = _lanes_to_batch_cols(jnp.concatenate(ws_parts, axis=0), lane)
        wo4 = _row_scalars(wo_rows).reshape(G, HEADS, 1, 1)
        q3 = _heads_to_batch(qk_ref[:, :256] * ML_SCALE, ML_DQK)
        k3 = _heads_to_batch(qk_ref[:, 256:], ML_DQK)
        v3 = _heads_to_batch(v_ref[...], 128)
        kw = k3 * ws3
        cloc = _bdot(kw, v3, BTN).reshape(G, HEADS, ML_DQK, 128)
        nloc = jnp.sum(kw, axis=1, keepdims=True).reshape(G, HEADS, 1, ML_DQK)
        C, nn = C_ref[...], n_ref[...]
        cs, ns = [], []
        for g in range(G):
            cs.append(C)
            ns.append(nn)
            C = wo4[g] * C + cloc[g]
            nn = wo4[g] * nn + nloc[g]
        C_ref[...] = C
        n_ref[...] = nn
        c4, n4 = jnp.stack(cs, axis=0), jnp.stack(ns, axis=0)
        cst_ref[...] = c4
        nst_ref[...] = n4
        r = _ml_batched(q3, k3, v3, g_all, x_all, lane, c4.reshape(G * HEADS, ML_DQK, 128),
                        n4.reshape(G * HEADS, 1, ML_DQK), _row_scalars(mprev_rows))
        h_ref[...] = _batch_to_heads(r["num"] / r["nrm"])

    return _hosted_call(
        body, comms, name="mlstm_fwd", grid=(N // G,),
        in_specs=[pl.BlockSpec((R, HALF), lambda n: (n, 0)), pl.BlockSpec((R, HALF), lambda n: (n, C_MV // 4)),
                  pl.BlockSpec((R, 128), lambda n: (n, C_GATES))],
        out_specs=[pl.BlockSpec((R, HALF), lambda n: (n, 0)),
                   pl.BlockSpec((G, HEADS, ML_DQK, 128), lambda n: (n, 0, 0, 0)),
                   pl.BlockSpec((G, HEADS, 1, ML_DQK), lambda n: (n, 0, 0, 0)),
                   pl.BlockSpec((G, 1, 128), lambda n: (n, 0, 0))],
        out_shape=[_sds((T, HALF)), _sds((N, HEADS, ML_DQK, 128)), _sds((N, HEADS, 1, ML_DQK)), _sds((N, 1, 128))],
        scratch_shapes=[pltpu.VMEM((HEADS, ML_DQK, 128), F32), pltpu.VMEM((HEADS, 1, ML_DQK), F32),
                        pltpu.VMEM((1, 128), F32)],
        args=(qkc, u, u))


def _mlstm_bwd(qkc, u, dh, cst, nst, mst, pre, cw):
    T = u.shape[0]
    G = min(GC, T // CHUNK)
    R = G * CHUNK
    NG = T // R

    def body(qk_ref, v_ref, g_ref, dh_ref, cst_ref, nst_ref, mst_ref, pre_ref, x_ref, xh_ref, cw_ref,
             dmqk_ref, dv_ref, dgt_ref, dcw_ref, dcb_ref, dC_ref, dn_ref, next_sc):
        @pl.when(pl.program_id(0) == 0)
        def _():
            for r in (dC_ref, dn_ref, next_sc, dcw_ref, dcb_ref):
                r[...] = jnp.zeros_like(r)

        B = G * HEADS
        gates = g_ref[...]
        g_all, x_all, lane, rowmod = _ml_gates(gates)
        mprev_rows = [mst_ref[g] for g in range(G)]
        wo_rows, ws_parts = [], []
        for g in range(G):
            _, wo, ws = _ml_chunk_rows(g_all, x_all, mprev_rows[g], g)
            wo_rows.append(wo)
            ws_parts.append(ws)
        ws3 = _lanes_to_batch_cols(jnp.concatenate(ws_parts, axis=0), lane)
        wo3 = _row_scalars(wo_rows)
        wo4 = wo3.reshape(G, HEADS, 1, 1)
        q3 = _heads_to_batch(qk_ref[:, :256] * ML_SCALE, ML_DQK)
        k3 = _heads_to_batch(qk_ref[:, 256:], ML_DQK)
        v3 = _heads_to_batch(v_ref[...], 128)
        dh3 = _heads_to_batch(dh_ref[...], 128)
        C3 = cst_ref[...].reshape(B, ML_DQK, 128)
        n3 = nst_ref[...].reshape(B, 1, ML_DQK)
        r = _ml_batched(q3, k3, v3, g_all, x_all, lane, C3, n3, _row_scalars(mprev_rows))
        wn, s3 = r["wn"], r["s"]
        inv = 1.0 / r["nrm"]
        dnum = dh3 * inv
        dnrm = -jnp.sum(dh3 * (r["num"] * inv), axis=2, keepdims=True) * inv
        dden = jnp.where(jnp.abs(r["den"]) > r["floor"], dnrm * jnp.sign(r["den"]), 0.0)
        ds = _bdot(dnum, v3, BNT) + dden
        dqk = ds * r["wi"]
        dd = ds * s3
        qw = q3 * wn
        dcloc = _bdot(qw, dnum, BTN).reshape(G, HEADS, ML_DQK, 128)
        dnloc = jnp.sum(qw * dden, axis=1, keepdims=True).reshape(G, HEADS, 1, ML_DQK)
        dC, dn = dC_ref[...], dn_ref[...]
        dcs, dns = [None] * G, [None] * G
        for g in reversed(range(G)):
            dcs[g], dns[g] = dC, dn
            dC = wo4[g] * dC + dcloc[g]
            dn = wo4[g] * dn + dnloc[g]
        dC_ref[...] = dC
        dn_ref[...] = dn
        dC3 = jnp.stack(dcs, axis=0).reshape(B, ML_DQK, 128)
        dn3 = jnp.stack(dns, axis=0).reshape(B, 1, ML_DQK)
        dk_st = ws3 * (_bdot(v3, dC3, BNT) + dn3)
        dq = _bdot(dqk, k3, BNN) + wn * (_bdot(dnum, C3, BNT) + dden * n3)
        dk = _bdot(dqk, q3, BTN) + dk_st
        dv = _bdot(s3, dnum, BTN) + ws3 * _bdot(k3, dC3, BNN)
        dv_ref[...] = _batch_to_heads(dv).astype(dv_ref.dtype)
        dqk = jnp.concatenate([_batch_to_heads(dq * ML_SCALE), _batch_to_heads(dk)], axis=1)
        pre = pre_ref[...]
        dpre = dqk * _dsilu(pre, _sig(pre))
        x = x_ref[...]
        xprev = jnp.where(pl.program_id(0) < NG - 1, xh_ref[...], 0.0)
        nxt = next_sc[...]
        row8 = _iota((8, HALF), 0)
        dx = dpre * cw_ref[3:4, :]
        dws = [None, None, None, _colsum(dpre * x)]
        for j in (1, 2, 3):
            dx = dx + _shift_rows_up(dpre, nxt, j, row8) * cw_ref[3 - j:4 - j, :]
            dws[3 - j] = _colsum(dpre * _shift_rows(x, xprev, j, row8))
        dmqk_ref[...] = dx.astype(dmqk_ref.dtype)
        dcw_ref[...] += jnp.concatenate(dws, axis=0)
        dcb_ref[...] += _colsum(dpre)
        next_sc[...] = dpre[:8]
        e_col = wn * (jnp.sum(dnum * r["qc"], axis=2, keepdims=True) + dden * r["qn"])
        c_col = jnp.sum(k3 * dk_st, axis=2, keepdims=True)
        z = wo3 * (jnp.sum(dC3 * C3, axis=(1, 2), keepdims=True) + jnp.sum(dn3 * n3, axis=(1, 2), keepdims=True))
        dd_hi = dd.astype(_MXU).astype(F32)
        ones = jnp.ones((B, CHUNK, 128), F32)
        dd_cols = (_bdot(dd_hi, ones, BTN) + _bdot(dd - dd_hi, ones, BTN))[:, :, 0:1]
        last = _iota((1, CHUNK, 1), 1) == CHUNK - 1
        dg3 = jnp.sum(dd, axis=2, keepdims=True) - dd_cols + e_col - c_col
        dg3 = dg3 + jnp.where(last, jnp.sum(c_col, axis=1, keepdims=True) + z, 0.0)
        di3 = dd_cols + c_col

        def to_lanes(x3, first):
            x4 = x3.reshape(G, HEADS, CHUNK, 1)
            return sum(jnp.where(lane == first + h, x4[:, h].reshape(R, 1), 0.0) for h in range(HEADS))

        dlf = _chunk_cumsum(to_lanes(dg3, 4), rowmod, reverse=True)
        dgt_ref[...] = (to_lanes(di3, 0) + dlf * _sig(-gates)).astype(dgt_ref.dtype)

    rev = lambda w, c: pl.BlockSpec((R, w), lambda i, c=c: (NG - 1 - i, c))
    st = lambda *s: pl.BlockSpec((G,) + s, lambda i: (NG - 1 - i,) + (0,) * len(s))
    halo = pl.BlockSpec((8, HALF), lambda i: (jnp.maximum((NG - 1 - i) * (R // 8) - 1, 0), C_MQK // 4))
    return pl.pallas_call(
        body, name="mlstm_bwd", grid=(NG,),
        in_specs=[rev(HALF, 0), rev(HALF, C_MV // 4), rev(128, C_GATES), rev(HALF, 0),
                  st(HEADS, ML_DQK, 128), st(HEADS, 1, ML_DQK), st(1, 128),
                  rev(HALF, 0), rev(HALF, C_MQK // 4), halo, _full(cw.shape)],
        out_specs=[rev(HALF, 0), rev(HALF, 0), rev(128, 0), _full((4, HALF)), _full((1, HALF))],
        out_shape=[_sds((T, HALF), _MXU), _sds((T, HALF), _MXU), _sds((T, 128), _MXU), _sds((4, HALF)), _sds((1, HALF))],
        scratch_shapes=[pltpu.VMEM((HEADS, ML_DQK, 128), F32), pltpu.VMEM((HEADS, 1, ML_DQK), F32),
                        pltpu.VMEM((8, HALF), F32)],
        compiler_params=_cp("arbitrary"))(qkc, u, u, dh, cst, nst, mst, pre, u, u, cw)


def _head_norm(o):
    rs_parts, r_parts = [], []
    for h in range(HEADS):
        oh = o[:, 128 * h:128 * (h + 1)]
        rs = lax.rsqrt(jnp.mean(oh * oh, axis=-1, keepdims=True) + RMS_EPS)
        rs_parts.append(rs)
        r_parts.append(oh * rs)
    return jnp.concatenate(r_parts, axis=1), rs_parts


def _out_proj(x, u, o_hg, h_ml, g_hg, g_ml, w_out, tm):
    T = x.shape[0]

    def body(x_ref, hgate_ref, mo_ref, ohg_ref, hml_ref, ghg_ref, gml_ref, w_ref, m_ref, z_ref):
        hgate = hgate_ref[...]
        a = _head_norm(ohg_ref[...])[0] * ghg_ref[...] * (hgate * _sig(hgate))
        b = _head_norm(hml_ref[...])[0] * gml_ref[...] * _sig(mo_ref[...])
        m = jnp.concatenate([a, b], axis=1)
        m_ref[...] = m.astype(m_ref.dtype)
        z_ref[...] = ALPHA * x_ref[...] + _dot(m, w_ref[...])

    return pl.pallas_call(
        body, name="out_proj", grid=(T // tm,),
        in_specs=[_row(tm, D_MODEL), _row(tm, HALF, C_HGATE // 4), _row(tm, HALF, C_MO // 4),
                  _row(tm, HALF), _row(tm, HALF), _full(g_hg.shape), _full(g_ml.shape), _full(w_out.shape)],
        out_specs=[_row(tm, D_MODEL)] * 2,
        out_shape=[_sds((T, D_MODEL), _MXU), _sds((T, D_MODEL))],
        compiler_params=_cp("parallel"))(x, u, u, o_hg, h_ml, g_hg, g_ml, w_out)


def _ffn_ln(z1, ln1_g, ln1_b, wg, wu, wd, ln_g, ln_b, tm):
    T = z1.shape[0]

    def body(z1_ref, g1_ref, b1_ref, wg_ref, wu_ref, wd_ref, g_ref, b_ref, z_ref, x1_ref, x2_ref, a_ref, bb_ref, h_ref):
        x = _ln_fwd(z1_ref[...], g1_ref[...], b1_ref[...])[0]
        x1_ref[...] = x.astype(x1_ref.dtype)
        a = _dot(x, wg_ref[...], NT)
        bb = _dot(x, wu_ref[...], NT)
        hh = a * _sig(a) * bb
        a_ref[...] = a.astype(a_ref.dtype)
        bb_ref[...] = bb.astype(bb_ref.dtype)
        h_ref[...] = hh.astype(h_ref.dtype)
        z = ALPHA * x + _dot(hh, wd_ref[...])
        z_ref[...] = z
        x2_ref[...] = _ln_fwd(z, g_ref[...], b_ref[...])[0].astype(x2_ref.dtype)

    vec = _full((1, D_MODEL))
    return pl.pallas_call(
        body, name="ffn_ln2", grid=(T // tm,),
        in_specs=[_row(tm, D_MODEL), vec, vec, _full(wg.shape), _full(wu.shape), _full(wd.shape), vec, vec],
        out_specs=[_row(tm, D_MODEL)] * 3 + [_row(tm, D_FF)] * 3,
        out_shape=[_sds((T, D_MODEL))] + [_sds((T, D_MODEL), _MXU)] * 2 + [_sds((T, D_FF), _MXU)] * 3,
        compiler_params=_cp("parallel"))(z1, ln1_g, ln1_b, wg, wu, wd, ln_g, ln_b)


def _ple_loss_ln2_bwd(z2, p, tgt, wpg, bpg, wpp, ln_g, ln_b, tm):
    T = z2.shape[0]

    def body(z_ref, p_ref, t_ref, wpg_ref, bpg_ref, wpp_ref, g_ref, b_ref,
             de_ref, dgp_ref, dz_ref, loss_ref, dbpg_ref, dg_ref, db_ref):
        @pl.when(pl.program_id(0) == 0)
        def _():
            for r in (loss_ref, dbpg_ref, dg_ref, db_ref):
                r[...] = jnp.zeros_like(r)

        x2, xhat, rstd = _ln_fwd(z_ref[...], g_ref[...], b_ref[...])
        gate = _sig(_dot(x2, wpg_ref[...]) + bpg_ref[...])
        e = _dot(p_ref[...], wpp_ref[...])
        err = x2 + gate * e - t_ref[...]
        loss_ref[...] += _colsum(err * err)
        dy = err * (1.0 / D_MODEL)
        de_ref[...] = (dy * gate).astype(de_ref.dtype)
        dgp = dy * e * gate * (1.0 - gate)
        dgp_ref[...] = dgp.astype(dgp_ref.dtype)
        dbpg_ref[...] += _colsum(dgp)
        dx2 = dy + _dot(dgp, wpg_ref[...], NT)
        dg_ref[...] += _colsum(dx2 * xhat)
        db_ref[...] += _colsum(dx2)
        dz_ref[...] = _ln_bwd(dx2, xhat, rstd, g_ref[...])

    vec = _full((1, D_MODEL))
    return pl.pallas_call(
        body, name="ple_loss_ln2_bwd", grid=(T // tm,),
        in_specs=[_row(tm, D_MODEL), _row(tm, PLE_DIM), _row(tm, D_MODEL),
                  _full(wpg.shape), vec, _full(wpp.shape), vec, vec],
        out_specs=[_row(tm, D_MODEL)] * 3 + [vec] * 4,
        out_shape=[_sds((T, D_MODEL), _MXU)] * 2 + [_sds((T, D_MODEL))] + [_sds((1, D_MODEL))] * 4,
        compiler_params=_cp("arbitrary"))(z2, p, tgt, wpg, bpg, wpp, ln_g, ln_b)


def _ffn_bwd_ln1_bwd(a_pre, b_pre, z1, dz2, wg, wu, wd, ln_g, ln_b, tm, comms=None):
    T = z1.shape[0]

    def body(a_ref, bb_ref, z_ref, dz2_ref, wg_ref, wu_ref, wd_ref, g_ref, b_ref,
             da_ref, dbb_ref, dz1_ref, dg_ref, db_ref):
        @pl.when(pl.program_id(0) == 0)
        def _():
            dg_ref[...] = jnp.zeros_like(dg_ref)
            db_ref[...] = jnp.zeros_like(db_ref)

        dz2 = dz2_ref[...]
        a = a_ref[...].astype(F32)
        bb = bb_ref[...].astype(F32)
        sa = _sig(a)
        act = a * sa
        dh = _dot(dz2, wd_ref[...], NT)
        da = (dh * bb * _dsilu(a, sa)).astype(da_ref.dtype)
        dbb = (dh * act).astype(dbb_ref.dtype)
        da_ref[...] = da
        dbb_ref[...] = dbb
        dx1 = ALPHA * dz2 + _dot(da, wg_ref[...]) + _dot(dbb, wu_ref[...])
        _, xhat, rstd = _ln_fwd(z_ref[...], g_ref[...], b_ref[...])
        dg_ref[...] += _colsum(dx1 * xhat)
        db_ref[...] += _colsum(dx1)
        dz1_ref[...] = _ln_bwd(dx1, xhat, rstd, g_ref[...])

    vec = _full((1, D_MODEL))
    return _hosted_call(
        body, comms, name="ffn_bwd_ln1_bwd", grid=(T // tm,),
        in_specs=[_row(tm, D_FF)] * 2 + [_row(tm, D_MODEL)] * 2 + [_full(wg.shape), _full(wu.shape), _full(wd.shape), vec, vec],
        out_specs=[_row(tm, D_FF)] * 2 + [_row(tm, D_MODEL), vec, vec],
        out_shape=[_sds((T, D_FF), _MXU)] * 2 + [_sds((T, D_MODEL)), _sds((1, D_MODEL)), _sds((1, D_MODEL))],
        scratch_shapes=[], args=(a_pre, b_pre, z1, dz2, wg, wu, wd, ln_g, ln_b))


def _out_proj_bwd(dz1, u, o_hg, h_ml, g_hg, g_ml, w_out, tm):
    T = dz1.shape[0]

    def body(dz_ref, hgate_ref, mo_ref, ohg_ref, hml_ref, ghg_ref, gml_ref, w_ref,
             dohg_ref, dhml_ref, dhgate_ref, dmo_ref, dghg_ref, dgml_ref):
        @pl.when(pl.program_id(0) == 0)
        def _():
            dghg_ref[...] = jnp.zeros_like(dghg_ref)
            dgml_ref[...] = jnp.zeros_like(dgml_ref)

        dm = _dot(dz_ref[...], w_ref[...], NT)

        def half(dmh, o, gvec, gate_val, dgate_fac, do_ref, dgate_ref, dgvec_ref):
            r, rs = _head_norm(o)
            dgate_ref[...] = (dmh * r * gvec * dgate_fac).astype(dgate_ref.dtype)
            dn = dmh * gate_val
            dgvec_ref[...] += _colsum(dn * r)
            dr = dn * gvec
            parts = []
            for h in range(HEADS):
                sl = slice(128 * h, 128 * (h + 1))
                parts.append(rs[h] * (dr[:, sl] - r[:, sl] * jnp.mean(dr[:, sl] * r[:, sl], axis=-1, keepdims=True)))
            do_ref[...] = jnp.concatenate(parts, axis=1)

        hg = hgate_ref[...]
        shg = _sig(hg)
        half(dm[:, :HALF], ohg_ref[...], ghg_ref[...], hg * shg, _dsilu(hg, shg), dohg_ref, dhgate_ref, dghg_ref)
        smo = _sig(mo_ref[...])
        half(dm[:, HALF:], hml_ref[...], gml_ref[...], smo, smo * (1.0 - smo), dhml_ref, dmo_ref, dgml_ref)

    vec = _full((1, HALF))
    return pl.pallas_call(
        body, name="out_proj_bwd", grid=(T // tm,),
        in_specs=[_row(tm, D_MODEL), _row(tm, HALF, C_HGATE // 4), _row(tm, HALF, C_MO // 4),
                  _row(tm, HALF), _row(tm, HALF), vec, vec, _full(w_out.shape)],
        out_specs=[_row(tm, HALF)] * 4 + [vec, vec],
        out_shape=[_sds((T, HALF))] * 2 + [_sds((T, HALF), _MXU)] * 2 + [_sds((1, HALF))] * 2,
        compiler_params=_cp("arbitrary"))(dz1, u, u, o_hg, h_ml, g_hg, g_ml, w_out)


def _du_specs(rows):
    return [pl.BlockSpec((rows, w), lambda i: (i, 0)) for w in DU_WIDTHS]


def _in_proj_bwd(dz1, du_parts, w, tm, comms=None):
    T = dz1.shape[0]

    def body(dz_ref, *refs):
        du = jnp.concatenate([r[...] for r in refs[:8]], axis=1)
        refs[9][...] = ALPHA * dz_ref[...] + _dot(du, refs[8][...], NT)

    (dx,), got = _hosted_call(
        body, comms, name="in_proj_bwd", grid=(T // tm,),
        in_specs=[_row(tm, D_MODEL)] + _du_specs(tm) + [_full(w.shape)],
        out_specs=[_row(tm, D_MODEL)], out_shape=[_sds((T, D_MODEL))], scratch_shapes=[], args=(dz1, *du_parts, w))
    return dx, got


def _wgrad(a, b, name, tm, tn, tk):
    T, M = a.shape
    N = b.shape[1]
    tm, tn, tk = min(tm, M), min(tn, N), min(tk, T)
    nk = T // tk

    def body(a_ref, b_ref, o_ref, acc_ref):
        kk = pl.program_id(2)

        @pl.when(kk == 0)
        def _():
            acc_ref[...] = jnp.zeros_like(acc_ref)

        acc_ref[...] += _dot(a_ref[...], b_ref[...], TN)

        @pl.when(kk == nk - 1)
        def _():
            o_ref[...] = acc_ref[...].astype(o_ref.dtype)

    return pl.pallas_call(
        body, name=name, grid=(M // tm, N // tn, nk),
        in_specs=[pl.BlockSpec((tk, tm), lambda i, j, kk: (kk, i)), pl.BlockSpec((tk, tn), lambda i, j, kk: (kk, j))],
        out_specs=pl.BlockSpec((tm, tn), lambda i, j, kk: (i, j)), out_shape=_sds((M, N), _MXU),
        scratch_shapes=[pltpu.VMEM((tm, tn), F32)],
        compiler_params=_cp("parallel", "parallel", "arbitrary"))(a, b)


W_IN_PARTS = 2


def _wgrad_w_in(x, du_parts, tk, part, comms=None):
    T = x.shape[0]
    M = D_MODEL // W_IN_PARTS
    tk = min(tk, T)
    nk = T // tk

    def body(a_ref, *refs):
        o_ref, cs_ref, acc_ref = refs[8:]
        kk = pl.program_id(0)

        @pl.when(kk == 0)
        def _():
            acc_ref[...] = jnp.zeros_like(acc_ref)
            cs_ref[...] = jnp.zeros_like(cs_ref)

        du = jnp.concatenate([r[...] for r in refs[:8]], axis=1)
        acc_ref[...] += _dot(a_ref[...], du, TN)
        cs_ref[...] += _colsum(du.astype(F32))

        @pl.when(kk == nk - 1)
        def _():
            o_ref[...] = acc_ref[...].astype(o_ref.dtype)

    return _hosted_call(
        body, comms, name="wgrad_w_in_%d" % part, grid=(nk,),
        in_specs=[pl.BlockSpec((tk, M), lambda kk: (kk, part))] + _du_specs(tk),
        out_specs=[_full((M, PROJ_WP)), _full((1, PROJ_WP))],
        out_shape=[_sds((M, PROJ_WP), _MXU), _sds((1, PROJ_WP))],
        scratch_shapes=[pltpu.VMEM((M, PROJ_WP), F32)], args=(x, *du_parts))


W_IN_S, FF_S, OUT_S, PP_S = PROJ_W // N_DEV, D_FF // N_DEV, D_MODEL // N_DEV, D_MODEL // N_DEV
LATE = ("w_ffn_gate", "w_ffn_up", "w_out", "w_ffn_down", "ple_w_gate", "ple_w_proj")
BIG = ("w_in",) + LATE
TRANSPOSED = ("w_ffn_gate", "w_ffn_up")


def _split_cols(a, n):
    return a.reshape(a.shape[0], N_DEV, n).transpose(1, 0, 2)


def _join_cols(a):
    return a.transpose(1, 0, 2).reshape(a.shape[1], -1)


def _step(x, p, tgt, w_in, b_in, lb_logits, conv_w, conv_b, g_hg, g_ml, ln1_g, ln1_b, ln2_g, ln2_b, bpg, late,
          distributed):
    T = x.shape[0]
    tm, tf = min(ROWS, T), min(ROWS_FFN, T)
    gather = lambda *names: [_GatherTwoLevel([late[n] for n in names])] if distributed else None
    scatter = lambda *arrs: [_Comm("scatter", list(arrs))] if distributed else None
    rows = lambda a, n: a.reshape(N_DEV, n, D_MODEL)
    (u, pre, qkc), got1 = _in_proj(x, w_in, b_in, conv_w, conv_b, tm, gather("w_out", "ple_w_gate", "ple_w_proj"))
    (o_hg, hg_states), got2 = _hgrn2_fwd(u, lb_logits, gather("w_ffn_gate", "w_ffn_up"))
    (h_ml, cst, nst, mst), got3 = _mlstm_fwd(qkc, u, gather("w_ffn_down"))
    if distributed:
        w_out, wpg, wpp = got1[0][0].reshape(D_MODEL, D_MODEL), got1[0][1].reshape(D_MODEL, D_MODEL), _join_cols(got1[0][2])
        wg, wu, wd = (a.reshape(D_FF, D_MODEL) for a in (got2[0][0], got2[0][1], got3[0][0]))
    else:
        w_out, wg, wu, wd, wpg, wpp = (late[n] for n in ("w_out", "w_ffn_gate", "w_ffn_up", "w_ffn_down", "ple_w_gate", "ple_w_proj"))
    m_in, z1 = _out_proj(x, u, o_hg, h_ml, g_hg, g_ml, w_out, tm)
    z2, x1, x2, a_pre, b_pre, hh = _ffn_ln(z1, ln1_g, ln1_b, wg, wu, wd, ln2_g, ln2_b, tf)
    de, dgp, dz2, loss_vec, d_bpg, d_ln2g, d_ln2b = _ple_loss_ln2_bwd(z2, p, tgt, wpg, bpg, wpp, ln2_g, ln2_b, tm)
    big = dict(ple_w_gate=_wgrad(x2, dgp, "wgrad_ple_gate", D_MODEL, D_MODEL, 2048),
               ple_w_proj=_wgrad(p, de, "wgrad_ple_proj", 512, D_MODEL, 2048))
    (da, dbb, dz1, d_ln1g, d_ln1b), r1 = _ffn_bwd_ln1_bwd(
        a_pre, b_pre, z1, dz2, wg, wu, wd, ln1_g, ln1_b, tf, scatter(rows(big["ple_w_gate"], OUT_S), _split_cols(big["ple_w_proj"], PP_S)))
    big.update(
        w_ffn_gate=_wgrad(da, x1, "wgrad_ffn_gate", D_FF, D_MODEL, 1024),
        w_ffn_up=_wgrad(dbb, x1, "wgrad_ffn_up", D_FF, D_MODEL, 1024),
        w_ffn_down=_wgrad(hh, dz2, "wgrad_ffn_down", D_FF, D_MODEL, 1024),
        w_out=_wgrad(m_in, dz1, "wgrad_w_out", D_MODEL, D_MODEL, 2048))
    d_ohg, d_hml, d_hgate, d_mo, d_ghg, d_gml = _out_proj_bwd(dz1, u, o_hg, h_ml, g_hg, g_ml, w_out, tm)
    (d_hq, d_hf, d_hv, d_lb), r2 = _hgrn2_bwd(
        u, lb_logits, d_ohg, hg_states,
        scatter(rows(big["w_ffn_gate"], FF_S), rows(big["w_ffn_up"], FF_S), rows(big["w_ffn_down"], FF_S),
                rows(big["w_out"], OUT_S)))
    d_mqk, d_mv, d_gates, d_convw, d_convb = _mlstm_bwd(qkc, u, d_hml, cst, nst, mst, pre, conv_w)
    du_parts = [d_hq, d_hf, d_hv, d_hgate, d_mqk, d_mv, d_mo, d_gates]
    own = lambda g: _split_cols(g[:, :PROJ_W], W_IN_S)
    (g_in0, d_bin), _ = _wgrad_w_in(x, du_parts, 512, 0)
    (g_in1, _), r_in0 = _wgrad_w_in(x, du_parts, 512, 1, scatter(own(g_in0)))
    big["w_in"] = jnp.concatenate([g_in0, g_in1], axis=0)
    small = dict(b_in=d_bin[:, :PROJ_W], hg_lb_logits=d_lb, ml_conv_w=d_convw, ml_conv_b=d_convb, hg_norm_g=d_ghg,
                 ml_norm_g=d_gml, ln1_g=d_ln1g, ln1_b=d_ln1b, ln2_g=d_ln2g, ln2_b=d_ln2b, ple_b_gate=d_bpg)
    last = [_Comm("scatter", [own(g_in1)]), _Comm("gather", [loss_vec] + [small[n] for n in SMALL])] if distributed else None
    dx, r3 = _in_proj_bwd(dz1, du_parts, w_in, tm, last)
    gathered_small = None
    if distributed:
        big = dict(ple_w_gate=r1[0][0], ple_w_proj=r1[0][1], w_ffn_gate=r2[0][0], w_ffn_up=r2[0][1],
                   w_ffn_down=r2[0][2], w_out=r2[0][3], w_in=[r_in0[0][0], r3[0][0]])
        gathered_small = r3[1]
    return loss_vec, dx, big, small, gathered_small


SMALL = ("b_in", "hg_lb_logits", "ml_conv_w", "ml_conv_b", "hg_norm_g", "ml_norm_g", "ln1_g", "ln1_b", "ln2_g", "ln2_b",
         "ple_b_gate")


def _padc(a, n):
    return jnp.pad(a, [(0, 0)] * (a.ndim - 1) + [(0, n - a.shape[-1])])


def _adamw(w, g, m, v):
    m = B1 * m + (1.0 - B1) * g
    v = B2 * v + (1.0 - B2) * jnp.square(g)
    m_hat = m / (1.0 - B1 ** STEP)
    v_hat = v / (1.0 - B2 ** STEP)
    return -LR * (m_hat / (jnp.sqrt(v_hat) + EPS) + WD * w), m, v


def _sum_slabs(ref):
    g = ref[0].astype(F32)
    for j in range(1, N_DEV):
        g = g + ref[j].astype(F32)
    return g


def _adamw_matrix(rbs, w, m, v, name):
    rbs = list(rbs) if isinstance(rbs, (list, tuple)) else [rbs]
    nb = len(rbs)
    R, C = w.shape
    tr = 256 if (R // nb) % 256 == 0 else R // nb
    per = R // nb // tr

    def body(*refs):
        w_ref, m_ref, v_ref, g_ref, d_ref, m2_ref, v2_ref = refs[nb:]
        i = pl.program_id(0)
        g = _sum_slabs(refs[0])
        for k in range(1, nb):
            g = jnp.where(i >= k * per, _sum_slabs(refs[k]), g)
        g_ref[...] = g
        d_ref[...], m2_ref[...], v2_ref[...] = _adamw(w_ref[...], g, m_ref[...], v_ref[...])

    blk = pl.BlockSpec((tr, C), lambda i: (i, 0))
    part = lambda k: pl.BlockSpec((N_DEV, tr, C), lambda i, k=k: (0, jnp.clip(i - k * per, 0, per - 1), 0))
    return pl.pallas_call(
        body, name=name, grid=(R // tr,),
        in_specs=[part(k) for k in range(nb)] + [blk, blk, blk],
        out_specs=[blk] * 4, out_shape=[_sds((R, C))] * 4, compiler_params=_cp("parallel"))(*rbs, w, m, v)


def _adamw_small(loss_g, gs, ws, ms, vs):
    n = len(ws)

    def body(*refs):
        loss_ref, g_refs, w_refs, m_refs, v_refs = refs[0], refs[1:1 + n], refs[1 + n:1 + 2 * n], refs[1 + 2 * n:1 + 3 * n], refs[1 + 3 * n:1 + 4 * n]
        outs = refs[1 + 4 * n:]
        outs[0][...] = (0.5 / D_MODEL) * jnp.sum(_sum_slabs(loss_ref), keepdims=True)
        for i in range(n):
            g = _sum_slabs(g_refs[i])
            outs[1 + i][...] = g
            outs[1 + n + i][...], outs[1 + 2 * n + i][...], outs[1 + 3 * n + i][...] = _adamw(
                w_refs[i][...], g, m_refs[i][...], v_refs[i][...])

    res = pl.pallas_call(
        body, name="adamw_small", out_shape=[_sds((1, 1))] + [_sds(w.shape) for w in ws] * 4)(loss_g, *gs, *ws, *ms, *vs)
    return res[0], [res[1 + k * n:1 + (k + 1) * n] for k in range(4)]


WEIGHTS = ("w_in", "b_in", "hg_lb_logits", "ml_conv_w", "ml_conv_b", "hg_norm_g", "ml_norm_g", "w_out", "ln1_g", "ln1_b",
           "w_ffn_gate", "w_ffn_up", "w_ffn_down", "ln2_g", "ln2_b", "ple_w_proj", "ple_w_gate", "ple_b_gate")
CONV_S = HALF // N_DEV


def kernel(x, p, w_in, b_in, hg_lb_logits, ml_conv_w, ml_conv_b, hg_norm_g, ml_norm_g, w_out, ln1_g, ln1_b, w_ffn_gate, w_ffn_up, w_ffn_down, ln2_g, ln2_b, ple_w_proj, ple_w_gate, ple_b_gate, loss_target, m_w_in, m_b_in, m_hg_lb_logits, m_ml_conv_w, m_ml_conv_b, m_hg_norm_g, m_ml_norm_g, m_w_out, m_ln1_g, m_ln1_b, m_w_ffn_gate, m_w_ffn_up, m_w_ffn_down, m_ln2_g, m_ln2_b, m_ple_w_proj, m_ple_w_gate, m_ple_b_gate, v_w_in, v_b_in, v_hg_lb_logits, v_ml_conv_w, v_ml_conv_b, v_hg_norm_g, v_ml_norm_g, v_w_out, v_ln1_g, v_ln1_b, v_w_ffn_gate, v_w_ffn_up, v_w_ffn_down, v_ln2_g, v_ln2_b, v_ple_w_proj, v_ple_w_gate, v_ple_b_gate):
    args = locals()
    me = 4 * lax.axis_index("x") + 2 * lax.axis_index("y") + lax.axis_index("c")
    shapes = {n: args[n].shape for n in WEIGHTS}
    def drop(n, a):
        a = a[0] if n in BIG or n == "ml_conv_w" else a
        return a.T if n in TRANSPOSED else a

    W = {n: drop(n, args[n]) for n in WEIGHTS}
    M = {n: drop(n, args["m_" + n]) for n in WEIGHTS}
    V = {n: drop(n, args["v_" + n]) for n in WEIGHTS}

    g_in, g_conv = _gather_two_level(
        [W["w_in"].astype(_MXU), jnp.pad(W["ml_conv_w"], ((0, 4), (0, 128 - CONV_S)))], "gather_w_in")
    w_in_full = _padc(_join_cols(g_in), PROJ_WP)
    conv_full = _join_cols(g_conv[:, :4, :CONV_S])

    _, dx, big, _, sg = _step(
        x[0], p[0, 0], loss_target[0], w_in_full, _padc(b_in, PROJ_WP), hg_lb_logits, conv_full, ml_conv_b,
        hg_norm_g, ml_norm_g, ln1_g, ln1_b, ln2_g, ln2_b, ple_b_gate, {n: W[n].astype(_MXU) for n in LATE}, True)

    upd = {n: _adamw_matrix(big[n], W[n], M[n], V[n], "adamw_" + n) for n in BIG}
    sg = dict(zip(SMALL, sg[1:]), loss=sg[0])
    sg["ml_conv_w"] = lax.dynamic_slice(sg["ml_conv_w"], (0, 0, me * CONV_S), (N_DEV, 4, CONV_S))
    loss, small_upd = _adamw_small(sg["loss"], *[[d[n] for n in SMALL] for d in (sg, W, M, V)])

    outs = []
    for kind in range(4):
        smalls = dict(zip(SMALL, small_upd[kind]))
        for n in WEIGHTS:
            o = upd[n][kind] if n in BIG else smalls[n]
            outs.append((o.T if n in TRANSPOSED else o).reshape(shapes[n]))
    return (loss.reshape(()), dx.reshape(x.shape), *outs)
```

```python
import jax
import jax.numpy as jnp
from jax import lax
from jax.experimental import pallas as pl
from jax.experimental.pallas import tpu as pltpu

F32 = jnp.float32
_MXU = jnp.bfloat16

D_MODEL = 1024
CHUNK = 64
SUB = 16
PLE_DIM = 256
HEADS = 4
ML_DQK = 64
HALF = 512
D_FF = 2816
PROJ_W = 3592
PROJ_WP = 3712
ALPHA = float(2 ** 0.25)
LN_EPS = 1e-5
RMS_EPS = 1e-6
ML_SCALE = ML_DQK ** -0.5
N_DEV = 8
LR, B1, B2, EPS, WD, STEP = 0.001, 0.9, 0.999, 1e-08, 0.01, 10
NEG = -1e30
LOG2E = 1.4426950408889634

C_HQ, C_HF, C_HV, C_HGATE, C_MQK, C_MV, C_MO, C_GATES = 0, 4, 8, 12, 16, 20, 24, 28
DU_WIDTHS = (HALF,) * 7 + (128,)

VMEM_LIMIT = 52 * 1024 * 1024
GC = 8
ROWS = 512
ROWS_FFN = 256

NN = (((1,), (0,)), ((), ()))
NT = (((1,), (1,)), ((), ()))
TN = (((0,), (0,)), ((), ()))
BNT = (((2,), (2,)), ((0,), (0,)))
BNN = (((2,), (1,)), ((0,), (0,)))
BTN = (((1,), (1,)), ((0,), (0,)))


def _dot(a, b, dims=NN):
    return lax.dot_general(a.astype(_MXU), b.astype(_MXU), dims, preferred_element_type=F32)


def _sig(x):
    return jax.nn.sigmoid(x)


def _cp(*sem):
    return pltpu.CompilerParams(dimension_semantics=sem, vmem_limit_bytes=VMEM_LIMIT)


def _row(tm, c, blk=0):
    return pl.BlockSpec((tm, c), lambda i, blk=blk: (i, blk))


def _full(shape):
    nd = len(shape)
    return pl.BlockSpec(tuple(shape), lambda *_, nd=nd: (0,) * nd)


def _sds(shape, dtype=F32):
    return jax.ShapeDtypeStruct(tuple(shape), dtype)


def _iota(shape, axis):
    return lax.broadcasted_iota(jnp.int32, shape, axis)


def _colsum(x):
    return jnp.sum(x, axis=0, keepdims=True)


def _rowsum(x):
    return jnp.sum(x, axis=1, keepdims=True)


def _ln_fwd(z, g, b):
    mu = jnp.mean(z, axis=-1, keepdims=True)
    zc = z - mu
    var = jnp.mean(zc * zc, axis=-1, keepdims=True)
    rstd = lax.rsqrt(var + LN_EPS)
    xhat = zc * rstd
    return xhat * g + b, xhat, rstd


def _ln_bwd(dy, xhat, rstd, g):
    dxh = dy * g
    m1 = jnp.mean(dxh, axis=-1, keepdims=True)
    m2 = jnp.mean(dxh * xhat, axis=-1, keepdims=True)
    return rstd * (dxh - m1 - xhat * m2)


def _dsilu(x, s):
    return s * (1.0 + x * (1.0 - s))


MESH = pl.DeviceIdType.MESH
ANY = pl.BlockSpec(memory_space=pl.ANY)


def _flip(v, bit):
    return 1 - v if bit else v


class _Comm:
    def __init__(self, kind, srcs):
        self.kind, self.srcs, self.n = kind, list(srcs), len(srcs)

    def out_shape(self):
        lead = (N_DEV,) if self.kind == "gather" else ()
        return [jax.ShapeDtypeStruct(lead + s.shape, s.dtype) for s in self.srcs]

    def scratch(self):
        return [pltpu.SemaphoreType.DMA((7 * self.n,)), pltpu.SemaphoreType.DMA((7 * self.n,)),
                pltpu.SemaphoreType.DMA((self.n,))]

    def copies(self, srcs, dsts, send_sems, recv_sems, local_sems):
        x, y, c = lax.axis_index("x"), lax.axis_index("y"), lax.axis_index("c")
        me = 4 * x + 2 * y + c
        pick = (lambda s, j: s) if self.kind == "gather" else (lambda s, j: s.at[j])
        out = []
        for i, (s, d) in enumerate(zip(srcs, dsts)):
            out.append(pltpu.make_async_copy(pick(s, me), d.at[me], local_sems.at[i]))
            for k in range(1, N_DEV):
                px, py, pc = _flip(x, k & 4), _flip(y, k & 2), _flip(c, k & 1)
                out.append(pltpu.make_async_remote_copy(
                    src_ref=pick(s, 4 * px + 2 * py + pc), dst_ref=d.at[me], send_sem=send_sems.at[7 * i + k - 1],
                    recv_sem=recv_sems.at[7 * i + k - 1], device_id=(px, py, pc), device_id_type=MESH))
        return out

    def start(self, *refs):
        for cp in self.copies(*refs):
            cp.start()

    def mid(self, *refs):
        pass

    def finish(self, *refs):
        for cp in self.copies(*refs):
            cp.wait()


class _GatherTwoLevel(_Comm):
    def __init__(self, srcs):
        super().__init__("gather", srcs)

    def _parts(self, srcs, dsts, send_sems, recv_sems, local_sems):
        x, y, c = lax.axis_index("x"), lax.axis_index("y"), lax.axis_index("c")
        me, sibling = (x, y, c), (x, y, 1 - c)
        chips = [(1 - x, y), (x, 1 - y), (1 - x, 1 - y)]

        def copy(i, k, block, to, own=False):
            slab = dsts[i].at[4 * block[0] + 2 * block[1] + block[2]]
            return pltpu.make_async_remote_copy(
                src_ref=srcs[i] if own else slab, dst_ref=slab, send_sem=send_sems.at[7 * i + k],
                recv_sem=recv_sems.at[7 * i + k], device_id=to, device_id_type=MESH)

        n = range(self.n)
        mine = [pltpu.make_async_copy(srcs[i], dsts[i].at[4 * x + 2 * y + c], local_sems.at[i]) for i in n]
        first = [copy(i, 0, me, sibling, own=True) for i in n]
        first += [copy(i, 1 + j, me, (*chip, c), own=True) for j, chip in enumerate(chips) for i in n]
        over_ici = [copy(i, 1 + j, (*chip, c), me) for j, chip in enumerate(chips) for i in n]
        passed = [copy(i, 4 + j, (*chip, c), sibling) for j, chip in enumerate(chips) for i in n]
        from_sibling = [copy(i, 0, sibling, me) for i in n]
        from_sibling += [copy(i, 4 + j, (*chip, 1 - c), me) for j, chip in enumerate(chips) for i in n]
        return mine, first, over_ici, passed, from_sibling

    def start(self, *refs):
        mine, first, _, _, _ = self._parts(*refs)
        for cp in mine + first:
            cp.start()

    def mid(self, *refs):
        _, _, over_ici, passed, _ = self._parts(*refs)
        for arrived, onward in zip(over_ici, passed):
            arrived.wait_recv()
            onward.start()

    def finish(self, *refs):
        mine, first, _, passed, from_sibling = self._parts(*refs)
        for cp in from_sibling:
            cp.wait_recv()
        for cp in first + passed:
            cp.wait_send()
        for cp in mine:
            cp.wait()


def _hosted_call(body, comms, *, name, grid, in_specs, out_specs, out_shape, scratch_shapes, args):
    comms = list(comms or [])
    if not comms:
        res = pl.pallas_call(body, name=name, grid=grid, in_specs=in_specs, out_specs=out_specs, out_shape=out_shape,
                             scratch_shapes=scratch_shapes, compiler_params=_cp("arbitrary"))(*args)
        return list(res), []
    n_in, n_out, n_sc, nc = len(in_specs), len(out_specs), len(scratch_shapes), sum(cm.n for cm in comms)
    last = grid[0] - 1

    def hosted(*refs):
        ins, csrc = refs[:n_in], refs[n_in:n_in + nc]
        o0 = n_in + nc
        outs, cdst = refs[o0:o0 + n_out], refs[o0 + n_out:o0 + n_out + nc]
        s0 = o0 + n_out + nc
        scr, sems = refs[s0:s0 + n_sc], refs[s0 + n_sc:]

        def phase(which):
            o = 0
            for j, cm in enumerate(comms):
                getattr(cm, which)(csrc[o:o + cm.n], cdst[o:o + cm.n], *sems[3 * j:3 * j + 3])
                o += cm.n

        i = pl.program_id(0)

        @pl.when(i == 0)
        def _():
            phase("start")

        body(*ins, *outs, *scr)

        @pl.when(i == (2 * last) // 3)
        def _():
            phase("mid")

        @pl.when(i == last)
        def _():
            phase("finish")

    res = pl.pallas_call(
        hosted, name=name, grid=grid, in_specs=list(in_specs) + [ANY] * nc, out_specs=list(out_specs) + [ANY] * nc,
        out_shape=list(out_shape) + [s for cm in comms for s in cm.out_shape()],
        scratch_shapes=list(scratch_shapes) + [s for cm in comms for s in cm.scratch()],
        compiler_params=_cp("arbitrary"))(*args, *[a for cm in comms for a in cm.srcs])
    got, o = [], n_out
    for cm in comms:
        got.append(list(res[o:o + cm.n]))
        o += cm.n
    return list(res[:n_out]), got


def _gather_two_level(blocks, name):
    n = len(blocks)

    def body(*refs):
        x_refs, out_refs = refs[:n], refs[n:2 * n]
        send_sems, recv_sems, local_sems = refs[2 * n:]
        x, y, c = lax.axis_index("x"), lax.axis_index("y"), lax.axis_index("c")
        me, sibling = (x, y, c), (x, y, 1 - c)
        chips = [(1 - x, y), (x, 1 - y), (1 - x, 1 - y)]

        def copy(i, k, block, to, own=False):
            slab = out_refs[i].at[4 * block[0] + 2 * block[1] + block[2]]
            return pltpu.make_async_remote_copy(
                src_ref=x_refs[i] if own else slab, dst_ref=slab, send_sem=send_sems.at[7 * i + k],
                recv_sem=recv_sems.at[7 * i + k], device_id=to, device_id_type=MESH)

        mine = [pltpu.make_async_copy(x_refs[i], out_refs[i].at[4 * x + 2 * y + c], local_sems.at[i]) for i in range(n)]
        for cp in mine:
            cp.start()
        first = [copy(i, 0, me, sibling, own=True) for i in range(n)]
        first += [copy(i, 1 + j, me, (*chip, c), own=True) for j, chip in enumerate(chips) for i in range(n)]
        for cp in first:
            cp.start()
        passed = []
        for j, chip in enumerate(chips):
            for i in range(n):
                copy(i, 1 + j, (*chip, c), me).wait_recv()
                passed.append(copy(i, 4 + j, (*chip, c), sibling))
                passed[-1].start()
        for i in range(n):
            copy(i, 0, sibling, me).wait_recv()
            for j, chip in enumerate(chips):
                copy(i, 4 + j, (*chip, 1 - c), me).wait_recv()
        for cp in first + passed:
            cp.wait_send()
        for cp in mine:
            cp.wait()

    return pl.pallas_call(
        body, name=name, out_shape=[jax.ShapeDtypeStruct((N_DEV,) + b.shape, b.dtype) for b in blocks],
        in_specs=[ANY] * n, out_specs=[ANY] * n,
        scratch_shapes=[pltpu.SemaphoreType.DMA((7 * n,)), pltpu.SemaphoreType.DMA((7 * n,)),
                        pltpu.SemaphoreType.DMA((n,))])(*blocks)


def _in_proj(x, w, b, cw, cb, tm, comms=None):
    T = x.shape[0]

    def body(x_ref, w_ref, b_ref, cw_ref, cb_ref, o_ref, pre_ref, act_ref, halo_sc):
        @pl.when(pl.program_id(0) == 0)
        def _():
            halo_sc[...] = jnp.zeros_like(halo_sc)

        o = _dot(x_ref[...], w_ref[...]) + b_ref[...]
        o_ref[...] = o
        xc = o[:, 128 * C_MQK:128 * C_MQK + HALF]
        halo = halo_sc[...]
        rowi = _iota((8, HALF), 0)
        acc = xc * cw_ref[3:4, :] + cb_ref[...]
        for j in (1, 2, 3):
            acc = acc + _shift_rows(xc, halo, j, rowi) * cw_ref[3 - j:4 - j, :]
        pre_ref[...] = acc
        act_ref[...] = acc * _sig(acc)
        halo_sc[...] = xc[tm - 8:]

    return _hosted_call(
        body, comms, name="in_proj", grid=(T // tm,),
        in_specs=[_row(tm, D_MODEL), _full(w.shape), _full(b.shape), _full(cw.shape), _full(cb.shape)],
        out_specs=[_row(tm, PROJ_WP), _row(tm, HALF), _row(tm, HALF)],
        out_shape=[_sds((T, PROJ_WP)), _sds((T, HALF)), _sds((T, HALF))],
        scratch_shapes=[pltpu.VMEM((8, HALF), F32)], args=(x, w, b, cw, cb))


def _shift_rows(x, halo, j, rowi):
    r = pltpu.roll(x, j, 0)
    top = jnp.where(rowi < j, pltpu.roll(halo, j, 0), r[:8])
    return jnp.concatenate([top, r[8:]], axis=0)


def _shift_rows_up(x, halo, j, rowi):
    n = x.shape[0]
    r = pltpu.roll(x, n - j, 0)
    bot = jnp.where(rowi >= 8 - j, pltpu.roll(halo, 8 - j, 0), r[n - 8:])
    return jnp.concatenate([r[:n - 8], bot], axis=0)


def _bdot(a, b, dims):
    return lax.dot_general(a.astype(_MXU), b.astype(_MXU), dims, preferred_element_type=F32)


def _bdotx(a, b, dims):
    return lax.dot_general(a, b, dims, precision=lax.Precision.HIGHEST, preferred_element_type=F32)


def _heads_to_batch(x, w):
    G = x.shape[0] // CHUNK
    x3 = x.reshape(G, CHUNK, HEADS * w)
    return jnp.stack([x3[:, :, w * h:w * (h + 1)] for h in range(HEADS)], axis=1).reshape(G * HEADS, CHUNK, w)


def _batch_to_heads(x3):
    B, _, w = x3.shape
    x4 = x3.reshape(B // HEADS, HEADS, CHUNK, w)
    return jnp.concatenate([x4[:, h] for h in range(HEADS)], axis=-1).reshape(B // HEADS * CHUNK, HEADS * w)


def _chunk_cumsum(x, rowmod, reverse=False):
    R = x.shape[0]
    for sh in (1, 2, 4, 8, 16, 32):
        if reverse:
            x = x + jnp.where(rowmod < CHUNK - sh, pltpu.roll(x, R - sh, 0), 0.0)
        else:
            x = x + jnp.where(rowmod >= sh, pltpu.roll(x, sh, 0), 0.0)
    return x


def _lane_col(x, c, lane):
    return _rowsum(jnp.where(lane == c, x, 0.0))


def _hg_gates(hq, hf, lb):
    sg = _sig(hf)
    nsg = _sig(-hf)
    f = lb + (1.0 - lb) * sg
    g = jnp.log(f)
    k = (1.0 - lb) * nsg
    sq = _sig(hq)
    return hq * sq, g, k, f, sg, nsg, sq


def _hg_prep(hq_ref, hf_ref, lg_ref, b_sc, k_sc):
    R = hq_ref.shape[0]
    G = R // CHUNK
    lb = _sig(lg_ref[0:1, :] - lg_ref[1:2, :])
    hq = hq_ref[...]
    q, g, k, f, sg, nsg, sq = _hg_gates(hq, hf_ref[...], lb)
    rowmod = _iota((R, HALF), 0) & (CHUNK - 1)
    b = _chunk_cumsum(g, rowmod) * LOG2E
    last8 = _iota((8, HALF), 0) == 7
    bl_rows = [_colsum(jnp.where(last8, b[CHUNK * c + CHUNK - 8:CHUNK * (c + 1)], 0.0)) for c in range(G)]
    bl3 = jnp.stack([r[:, 128 * h:128 * (h + 1)] for r in bl_rows for h in range(HEADS)], axis=0)
    b3, k3 = _heads_to_batch(b, 128), _heads_to_batch(k, 128)
    b_sc[...] = b3
    k_sc[...] = k3
    return dict(G=G, lb=lb, hq=hq, f=f, sg=sg, nsg=nsg, sq=sq, rowmod=rowmod, q3=_heads_to_batch(q, 128), k3=k3, b3=b3,
                bl3=bl3)


HSUB = 8


def _lo(j):
    return HSUB * (j // HSUB)


def _hg_diag_tiles(b_sc, b3, r0, rowi):
    bi = b3[:, r0:r0 + SUB]
    return [jnp.exp2(jnp.where(rowi[:, _lo(s):] >= s, bi[:, _lo(s):] - b_sc[:, r0 + s:r0 + s + 1, :], NEG))
            for s in range(SUB)]


def _hg_diag_tiles_t(b_sc, b3, r0, rowi):
    bi = b3[:, r0:r0 + SUB]
    return [jnp.exp2(jnp.where(rowi[:, :_lo(t) + HSUB] <= t, b_sc[:, r0 + t:r0 + t + 1, :] - bi[:, :_lo(t) + HSUB], NEG))
            for t in range(SUB)]


def _lane_sums(pieces, ones):
    B = pieces[0].shape[0]
    hs = [p.shape[1] for p in pieces]
    R = _dot(jnp.concatenate(pieces, axis=1).reshape(B * sum(hs), 128), ones).reshape(B, sum(hs), 128)
    out, o = [], 0
    for h in hs:
        out.append(R[:, o:o + h])
        o += h
    return out


def _sum_tri(terms, low_rows):
    if SUB == HSUB:
        return sum(terms)
    full = sum(t for t in terms if t.shape[1] == SUB)
    half = sum(t for t in terms if t.shape[1] == HSUB)
    lo, hi = full[:, :HSUB], full[:, HSUB:]
    return jnp.concatenate([lo + half, hi] if low_rows else [lo, hi + half], axis=1)


def _hgrn2_fwd(u, lb_logits, comms=None):
    T = u.shape[0]
    G = min(GC, T // CHUNK)
    R, B, N = G * CHUNK, G * HEADS, T // CHUNK

    def body(hq_ref, hf_ref, hv_ref, lg_ref, o_ref, st_ref, S_ref, b_sc, k_sc, v_sc):
        @pl.when(pl.program_id(0) == 0)
        def _():
            S_ref[...] = jnp.zeros_like(S_ref)

        pz = _hg_prep(hq_ref, hf_ref, lg_ref, b_sc, k_sc)
        q3, k3, b3, bl3 = pz["q3"], pz["k3"], pz["b3"], pz["bl3"]
        v3 = _heads_to_batch(hv_ref[...], 128)
        v_sc[...] = v3
        stloc = _bdot(v3, k3 * jnp.exp2(bl3 - b3), BTN).reshape(G, HEADS, 128, 128)
        dec = jnp.exp2(bl3).reshape(G, HEADS, 1, 128)
        ST = S_ref[...]
        sts = []
        for c in range(G):
            sts.append(ST)
            ST = ST * dec[c] + stloc[c]
        S_ref[...] = ST
        st4 = jnp.stack(sts, axis=0)
        st_ref[...] = st4
        o = _bdot(q3 * jnp.exp2(b3), st4.reshape(B, 128, 128), BNT)
        ones = jnp.ones((128, 128), F32)
        rowi = _iota((1, SUB, 128), 1)
        outs = []
        for i in range(CHUNK // SUB):
            r0 = SUB * i
            qi = q3[:, r0:r0 + SUB]
            oi = o[:, r0:r0 + SUB]
            if i > 0:
                r = b_sc[:, r0 - 1:r0, :]
                qe = qi * jnp.exp2(b3[:, r0:r0 + SUB] - r)
                ke = k3[:, :r0] * jnp.exp2(r - b3[:, :r0])
                oi = oi + _bdot(_bdot(qe, ke, BNT), v3[:, :r0], BNN)
            tiles = _hg_diag_tiles(b_sc, b3, r0, rowi)
            a_b = _lane_sums([qi[:, _lo(s):] * (k_sc[:, r0 + s:r0 + s + 1, :] * tiles[s]) for s in range(SUB)], ones)
            outs.append(oi + _sum_tri([a_b[s] * v_sc[:, r0 + s:r0 + s + 1, :] for s in range(SUB)], False))
        o_ref[...] = _batch_to_heads(jnp.concatenate(outs, axis=1))

    blk = lambda c: pl.BlockSpec((R, HALF), lambda n, c=c: (n, c // 4))
    return _hosted_call(
        body, comms, name="hgrn2_fwd", grid=(N // G,),
        in_specs=[blk(C_HQ), blk(C_HF), blk(C_HV), _full(lb_logits.shape)],
        out_specs=[pl.BlockSpec((R, HALF), lambda n: (n, 0)),
                   pl.BlockSpec((G, HEADS, 128, 128), lambda n: (n, 0, 0, 0))],
        out_shape=[_sds((T, HALF)), _sds((N, HEADS, 128, 128))],
        scratch_shapes=[pltpu.VMEM((HEADS, 128, 128), F32)] + [pltpu.VMEM((B, CHUNK, 128), F32)] * 3,
        args=(u, u, u, lb_logits))


def _hgrn2_bwd(u, lb_logits, do, states, comms=None):
    T = u.shape[0]
    G = min(GC, T // CHUNK)
    R, B, NG = G * CHUNK, G * HEADS, T // (G * CHUNK)

    def body(hq_ref, hf_ref, hv_ref, lg_ref, do_ref, st_ref, dhq_ref, dhf_ref, dhv_ref, dlb_ref,
             dS_ref, b_sc, k_sc, v_sc, q_sc, do_sc):
        @pl.when(pl.program_id(0) == 0)
        def _():
            dS_ref[...] = jnp.zeros_like(dS_ref)
            dlb_ref[...] = jnp.zeros_like(dlb_ref)

        pz = _hg_prep(hq_ref, hf_ref, lg_ref, b_sc, k_sc)
        q3, k3, b3, bl3, lb = pz["q3"], pz["k3"], pz["b3"], pz["bl3"], pz["lb"]
        v3 = _heads_to_batch(hv_ref[...], 128)
        v_sc[...] = v3
        do3 = _heads_to_batch(do_ref[...], 128)
        q_sc[...] = q3
        do_sc[...] = do3
        st3 = st_ref[...].reshape(B, 128, 128)
        eb = jnp.exp2(b3)
        ebl = jnp.exp2(bl3 - b3)
        qt = q3 * eb
        kl = k3 * ebl
        dstloc = _bdot(do3, qt, BTN).reshape(G, HEADS, 128, 128)
        dec = jnp.exp2(bl3).reshape(G, HEADS, 1, 128)
        dST = dS_ref[...]
        dsts = [None] * G
        for c in reversed(range(G)):
            dsts[c] = dST
            dST = dST * dec[c] + dstloc[c]
        dS_ref[...] = dST
        dst3 = jnp.stack(dsts, axis=0).reshape(B, 128, 128)
        dqt = _bdot(do3, st3, BNN)
        dkl = _bdot(v3, dst3, BNN)
        dv_acc = _bdot(kl, dst3, BNT)
        ones = jnp.ones((128, 128), F32)
        rowi = _iota((1, SUB, 128), 1)
        dq_parts, dk_parts, dv_parts = [], [], []
        dk_in = jnp.zeros((B, CHUNK, 128), F32)
        for i_s in range(CHUNK // SUB):
            r0 = SUB * i_s
            qi = q3[:, r0:r0 + SUB]
            doi = do3[:, r0:r0 + SUB]
            dqi = jnp.zeros((B, SUB, 128), F32)
            if i_s > 0:
                r = b_sc[:, r0 - 1:r0, :]
                eq = jnp.exp2(b3[:, r0:r0 + SUB] - r)
                ek = jnp.exp2(r - b3[:, :r0])
                qe = qi * eq
                ke = k3[:, :r0] * ek
                a_off = _bdot(qe, ke, BNT)
                p_off = _bdot(doi, v3[:, :r0], BNT)
                pad = jnp.zeros((B, CHUNK - r0, 128), F32)
                dv_acc = dv_acc + jnp.concatenate([_bdot(a_off, doi, BTN), pad], axis=1)
                dqi = dqi + _bdot(p_off, ke, BNN) * eq
                dk_in = dk_in + jnp.concatenate([_bdot(p_off, qe, BTN) * ek, pad], axis=1)
            ki, vi = k3[:, r0:r0 + SUB], v3[:, r0:r0 + SUB]
            rng = range(SUB)
            tiles = _hg_diag_tiles(b_sc, b3, r0, rowi)
            tiles_t = _hg_diag_tiles_t(b_sc, b3, r0, rowi)
            do_rows = [do_sc[:, r0 + t:r0 + t + 1, :] for t in rng]
            kts = [k_sc[:, r0 + s:r0 + s + 1, :] * tiles[s] for s in rng]
            qts = [q_sc[:, r0 + t:r0 + t + 1, :] * tiles_t[t] for t in rng]
            ps = [doi[:, _lo(s):] * v_sc[:, r0 + s:r0 + s + 1, :] for s in rng]
            mst = [ki[:, :_lo(t) + HSUB] * qts[t] for t in rng]
            pst = [vi[:, :_lo(t) + HSUB] * do_rows[t] for t in rng]
            sums = _lane_sums(ps + mst + pst, ones)
            p_b, a_t, p_t = sums[:SUB], sums[SUB:2 * SUB], sums[2 * SUB:]
            dq_parts.append(dqi + _sum_tri([p_b[s] * kts[s] for s in rng], False))
            dv_parts.append(_sum_tri([a_t[t] * do_rows[t] for t in rng], True))
            dk_parts.append(_sum_tri([p_t[t] * qts[t] for t in rng], True))
        dq_in = jnp.concatenate(dq_parts, axis=1)
        dk_in = dk_in + jnp.concatenate(dk_parts, axis=1)
        dv_acc = dv_acc + jnp.concatenate(dv_parts, axis=1)
        db = qt * dqt + q3 * dq_in - k3 * dk_in - kl * dkl
        last = jnp.sum(kl * dkl, axis=1, keepdims=True) + jnp.exp2(bl3) * jnp.sum(st3 * dst3, axis=1, keepdims=True)
        db = db + jnp.where(_iota((1, CHUNK, 1), 1) == CHUNK - 1, last, 0.0)
        dg = _chunk_cumsum(_batch_to_heads(db), pz["rowmod"], reverse=True)
        dq_tot = _batch_to_heads(dqt * eb + dq_in)
        dk_tot = _batch_to_heads(dkl * ebl + dk_in)
        common = dg / pz["f"] - dk_tot
        dhf_ref[...] = ((1.0 - lb) * pz["sg"] * pz["nsg"] * common).astype(dhf_ref.dtype)
        dl0 = _colsum(pz["nsg"] * common) * lb * (1.0 - lb)
        dlb_ref[0:1, :] += dl0
        dlb_ref[1:2, :] -= dl0
        dhq_ref[...] = (dq_tot * _dsilu(pz["hq"], pz["sq"])).astype(dhq_ref.dtype)
        dhv_ref[...] = _batch_to_heads(dv_acc).astype(dhv_ref.dtype)

    rev = lambda c: pl.BlockSpec((R, HALF), lambda i, c=c: (NG - 1 - i, c // 4))
    rev0 = pl.BlockSpec((R, HALF), lambda i: (NG - 1 - i, 0))
    return _hosted_call(
        body, comms, name="hgrn2_bwd", grid=(NG,),
        in_specs=[rev(C_HQ), rev(C_HF), rev(C_HV), _full(lb_logits.shape), rev0,
                  pl.BlockSpec((G, HEADS, 128, 128), lambda i: (NG - 1 - i, 0, 0, 0))],
        out_specs=[rev0, rev0, rev0, _full((2, HALF))],
        out_shape=[_sds((T, HALF), _MXU)] * 3 + [_sds((2, HALF))],
        scratch_shapes=[pltpu.VMEM((HEADS, 128, 128), F32)] + [pltpu.VMEM((B, CHUNK, 128), F32)] * 5,
        args=(u, u, u, lb_logits, do, states))


def _lanes_to_batch_cols(x, lane):
    G = x.shape[0] // CHUNK
    cols = [_lane_col(x, 4 + h, lane).reshape(G, CHUNK, 1) for h in range(HEADS)]
    return jnp.stack(cols, axis=1).reshape(G * HEADS, CHUNK, 1)


def _row_scalars(rows):
    lane = _iota((1, 128), 1)
    return jnp.stack([_rowsum(jnp.where(lane == 4 + h, r, 0.0)) for r in rows for h in range(HEADS)], axis=0)


def _ml_gates(gates):
    R = gates.shape[0]
    lane = _iota((R, 128), 1)
    rowmod = _iota((R, 128), 0) & (CHUNK - 1)
    lf = jnp.minimum(gates, 0.0) - jnp.log(1.0 + jnp.exp(-jnp.abs(gates)))
    g_all = _chunk_cumsum(lf, rowmod)
    x_all = pltpu.roll(gates, 4, 1) - g_all
    return g_all, x_all, lane, rowmod


def _ml_chunk_rows(g_all, x_all, mprev, g):
    gl = g_all[CHUNK * g + CHUNK - 8:CHUNK * (g + 1)]
    gl = _colsum(jnp.where(_iota((8, 128), 0) == 7, gl, 0.0))
    a = gl + x_all[CHUNK * g:CHUNK * (g + 1)]
    m_new = jnp.maximum(gl + mprev, jnp.max(a, axis=0, keepdims=True))
    return m_new, jnp.exp(gl + mprev - m_new), jnp.exp(a - m_new)


def _ml_batched(q3, k3, v3, g_all, x_all, lane, C3, n3, mprev3):
    G = g_all.shape[0] // CHUNK
    gcol3 = _lanes_to_batch_cols(g_all, lane)
    onehot = jnp.where(_iota((G, 8, 128), 1) + 4 == _iota((G, 8, 128), 2), 1.0, 0.0).astype(F32)
    rows = _bdotx(onehot, x_all.reshape(G, CHUNK, 128), BNT)
    sub = _iota((G, 8, CHUNK), 1)
    row3 = jnp.stack([jnp.sum(jnp.where(sub == h, rows, 0.0), axis=1, keepdims=True) for h in range(HEADS)],
                     axis=1).reshape(G * HEADS, 1, CHUNK)
    causal = _iota((1, CHUNK, CHUNK), 1) >= _iota((1, CHUNK, CHUNK), 2)
    dmat = jnp.where(causal, gcol3 + row3, NEG)
    m_inter = gcol3 + mprev3
    m_t = jnp.maximum(m_inter, jnp.max(dmat, axis=2, keepdims=True))
    wi = jnp.exp(dmat - m_t)
    wn = jnp.exp(m_inter - m_t)
    s3 = _bdot(q3, k3, BNT) * wi
    qc = _bdot(q3, C3, BNN)
    qn = jnp.sum(q3 * n3, axis=2, keepdims=True)
    num = _bdot(s3, v3, BNN) + wn * qc
    den = jnp.sum(s3, axis=2, keepdims=True) + wn * qn
    floor = jnp.exp(-m_t)
    return dict(wi=wi, wn=wn, s=s3, qc=qc, qn=qn, num=num, den=den, floor=floor, nrm=jnp.maximum(jnp.abs(den), floor))


def _mlstm_fwd(qkc, u, comms=None):
    T = u.shape[0]
    G = min(GC, T // CHUNK)
    R = G * CHUNK
    N = T // CHUNK

    def body(qk_ref, v_ref, g_ref, h_ref, cst_ref, nst_ref, mst_ref, C_ref, n_ref, m_ref):
        @pl.when(pl.program_id(0) == 0)
        def _():
            C_ref[...] = jnp.zeros_like(C_ref)
            n_ref[...] = jnp.zeros_like(n_ref)
            m_ref[...] = jnp.zeros_like(m_ref)

        g_all, x_all, lane, _ = _ml_gates(g_ref[...])
        m_row = m_ref[...]
        mprev_rows, wo_rows, ws_parts = [], [], []
        for g in range(G):
            mprev_rows.append(m_row)
            m_row, wo, ws = _ml_chunk_rows(g_all, x_all, m_row, g)
            wo_rows.append(wo)
            ws_parts.append(ws)
        m_ref[...] = m_row
        mst_ref[...] = jnp.stack(mprev_rows, axis=0)
        ws3 = _lanes_to_batch_cols(jnp.concatenate(ws_parts, axis=0), lane)
        wo4 = _row_scalars(wo_rows).reshape(G, HEADS, 1, 1)
        q3 = _heads_to_batch(qk_ref[:, :256] * ML_SCALE, ML_DQK)
        k3 = _heads_to_batch(qk_ref[:, 256:], ML_DQK)
        v3 = _heads_to_batch(v_ref[...], 128)
        kw = k3 * ws3
        cloc = _bdot(kw, v3, BTN).reshape(G, HEADS, ML_DQK, 128)
        nloc = jnp.sum(kw, axis=1, keepdims=True).reshape(G, HEADS, 1, ML_DQK)
        C, nn = C_ref[...], n_ref[...]
        cs, ns = [], []
        for g in range(G):
            cs.append(C)
            ns.append(nn)
            C = wo4[g] * C + cloc[g]
            nn = wo4[g] * nn + nloc[g]
        C_ref[...] = C
        n_ref[...] = nn
        c4, n4 = jnp.stack(cs, axis=0), jnp.stack(ns, axis=0)
        cst_ref[...] = c4
        nst_ref[...] = n4
        r = _ml_batched(q3, k3, v3, g_all, x_all, lane, c4.reshape(G * HEADS, ML_DQK, 128),
                        n4.reshape(G * HEADS, 1, ML_DQK), _row_scalars(mprev_rows))
        h_ref[...] = _batch_to_heads(r["num"] / r["nrm"])

    return _hosted_call(
        body, comms, name="mlstm_fwd", grid=(N // G,),
        in_specs=[pl.BlockSpec((R, HALF), lambda n: (n, 0)), pl.BlockSpec((R, HALF), lambda n: (n, C_MV // 4)),
                  pl.BlockSpec((R, 128), lambda n: (n, C_GATES))],
        out_specs=[pl.BlockSpec((R, HALF), lambda n: (n, 0)),
                   pl.BlockSpec((G, HEADS, ML_DQK, 128), lambda n: (n, 0, 0, 0)),
                   pl.BlockSpec((G, HEADS, 1, ML_DQK), lambda n: (n, 0, 0, 0)),
                   pl.BlockSpec((G, 1, 128), lambda n: (n, 0, 0))],
        out_shape=[_sds((T, HALF)), _sds((N, HEADS, ML_DQK, 128)), _sds((N, HEADS, 1, ML_DQK)), _sds((N, 1, 128))],
        scratch_shapes=[pltpu.VMEM((HEADS, ML_DQK, 128), F32), pltpu.VMEM((HEADS, 1, ML_DQK), F32),
                        pltpu.VMEM((1, 128), F32)],
        args=(qkc, u, u))


def _mlstm_bwd(qkc, u, dh, cst, nst, mst, pre, cw):
    T = u.shape[0]
    G = min(GC, T // CHUNK)
    R = G * CHUNK
    NG = T // R

    def body(qk_ref, v_ref, g_ref, dh_ref, cst_ref, nst_ref, mst_ref, pre_ref, x_ref, xh_ref, cw_ref,
             dmqk_ref, dv_ref, dgt_ref, dcw_ref, dcb_ref, dC_ref, dn_ref, next_sc):
        @pl.when(pl.program_id(0) == 0)
        def _():
            for r in (dC_ref, dn_ref, next_sc, dcw_ref, dcb_ref):
                r[...] = jnp.zeros_like(r)

        B = G * HEADS
        gates = g_ref[...]
        g_all, x_all, lane, rowmod = _ml_gates(gates)
        mprev_rows = [mst_ref[g] for g in range(G)]
        wo_rows, ws_parts = [], []
        for g in range(G):
            _, wo, ws = _ml_chunk_rows(g_all, x_all, mprev_rows[g], g)
            wo_rows.append(wo)
            ws_parts.append(ws)
        ws3 = _lanes_to_batch_cols(jnp.concatenate(ws_parts, axis=0), lane)
        wo3 = _row_scalars(wo_rows)
        wo4 = wo3.reshape(G, HEADS, 1, 1)
        q3 = _heads_to_batch(qk_ref[:, :256] * ML_SCALE, ML_DQK)
        k3 = _heads_to_batch(qk_ref[:, 256:], ML_DQK)
        v3 = _heads_to_batch(v_ref[...], 128)
        dh3 = _heads_to_batch(dh_ref[...], 128)
        C3 = cst_ref[...].reshape(B, ML_DQK, 128)
        n3 = nst_ref[...].reshape(B, 1, ML_DQK)
        r = _ml_batched(q3, k3, v3, g_all, x_all, lane, C3, n3, _row_scalars(mprev_rows))
        wn, s3 = r["wn"], r["s"]
        inv = 1.0 / r["nrm"]
        dnum = dh3 * inv
        dnrm = -jnp.sum(dh3 * (r["num"] * inv), axis=2, keepdims=True) * inv
        dden = jnp.where(jnp.abs(r["den"]) > r["floor"], dnrm * jnp.sign(r["den"]), 0.0)
        ds = _bdot(dnum, v3, BNT) + dden
        dqk = ds * r["wi"]
        dd = ds * s3
        qw = q3 * wn
        dcloc = _bdot(qw, dnum, BTN).reshape(G, HEADS, ML_DQK, 128)
        dnloc = jnp.sum(qw * dden, axis=1, keepdims=True).reshape(G, HEADS, 1, ML_DQK)
        dC, dn = dC_ref[...], dn_ref[...]
        dcs, dns = [None] * G, [None] * G
        for g in reversed(range(G)):
            dcs[g], dns[g] = dC, dn
            dC = wo4[g] * dC + dcloc[g]
            dn = wo4[g] * dn + dnloc[g]
        dC_ref[...] = dC
        dn_ref[...] = dn
        dC3 = jnp.stack(dcs, axis=0).reshape(B, ML_DQK, 128)
        dn3 = jnp.stack(dns, axis=0).reshape(B, 1, ML_DQK)
        dk_st = ws3 * (_bdot(v3, dC3, BNT) + dn3)
        dq = _bdot(dqk, k3, BNN) + wn * (_bdot(dnum, C3, BNT) + dden * n3)
        dk = _bdot(dqk, q3, BTN) + dk_st
        dv = _bdot(s3, dnum, BTN) + ws3 * _bdot(k3, dC3, BNN)
        dv_ref[...] = _batch_to_heads(dv).astype(dv_ref.dtype)
        dqk = jnp.concatenate([_batch_to_heads(dq * ML_SCALE), _batch_to_heads(dk)], axis=1)
        pre = pre_ref[...]
        dpre = dqk * _dsilu(pre, _sig(pre))
        x = x_ref[...]
        xprev = jnp.where(pl.program_id(0) < NG - 1, xh_ref[...], 0.0)
        nxt = next_sc[...]
        row8 = _iota((8, HALF), 0)
        dx = dpre * cw_ref[3:4, :]
        dws = [None, None, None, _colsum(dpre * x)]
        for j in (1, 2, 3):
            dx = dx + _shift_rows_up(dpre, nxt, j, row8) * cw_ref[3 - j:4 - j, :]
            dws[3 - j] = _colsum(dpre * _shift_rows(x, xprev, j, row8))
        dmqk_ref[...] = dx.astype(dmqk_ref.dtype)
        dcw_ref[...] += jnp.concatenate(dws, axis=0)
        dcb_ref[...] += _colsum(dpre)
        next_sc[...] = dpre[:8]
        e_col = wn * (jnp.sum(dnum * r["qc"], axis=2, keepdims=True) + dden * r["qn"])
        c_col = jnp.sum(k3 * dk_st, axis=2, keepdims=True)
        z = wo3 * (jnp.sum(dC3 * C3, axis=(1, 2), keepdims=True) + jnp.sum(dn3 * n3, axis=(1, 2), keepdims=True))
        dd_hi = dd.astype(_MXU).astype(F32)
        ones = jnp.ones((B, CHUNK, 128), F32)
        dd_cols = (_bdot(dd_hi, ones, BTN) + _bdot(dd - dd_hi, ones, BTN))[:, :, 0:1]
        last = _iota((1, CHUNK, 1), 1) == CHUNK - 1
        dg3 = jnp.sum(dd, axis=2, keepdims=True) - dd_cols + e_col - c_col
        dg3 = dg3 + jnp.where(last, jnp.sum(c_col, axis=1, keepdims=True) + z, 0.0)
        di3 = dd_cols + c_col

        def to_lanes(x3, first):
            x4 = x3.reshape(G, HEADS, CHUNK, 1)
            return sum(jnp.where(lane == first + h, x4[:, h].reshape(R, 1), 0.0) for h in range(HEADS))

        dlf = _chunk_cumsum(to_lanes(dg3, 4), rowmod, reverse=True)
        dgt_ref[...] = (to_lanes(di3, 0) + dlf * _sig(-gates)).astype(dgt_ref.dtype)

    rev = lambda w, c: pl.BlockSpec((R, w), lambda i, c=c: (NG - 1 - i, c))
    st = lambda *s: pl.BlockSpec((G,) + s, lambda i: (NG - 1 - i,) + (0,) * len(s))
    halo = pl.BlockSpec((8, HALF), lambda i: (jnp.maximum((NG - 1 - i) * (R // 8) - 1, 0), C_MQK // 4))
    return pl.pallas_call(
        body, name="mlstm_bwd", grid=(NG,),
        in_specs=[rev(HALF, 0), rev(HALF, C_MV // 4), rev(128, C_GATES), rev(HALF, 0),
                  st(HEADS, ML_DQK, 128), st(HEADS, 1, ML_DQK), st(1, 128),
                  rev(HALF, 0), rev(HALF, C_MQK // 4), halo, _full(cw.shape)],
        out_specs=[rev(HALF, 0), rev(HALF, 0), rev(128, 0), _full((4, HALF)), _full((1, HALF))],
        out_shape=[_sds((T, HALF), _MXU), _sds((T, HALF), _MXU), _sds((T, 128), _MXU), _sds((4, HALF)), _sds((1, HALF))],
        scratch_shapes=[pltpu.VMEM((HEADS, ML_DQK, 128), F32), pltpu.VMEM((HEADS, 1, ML_DQK), F32),
                        pltpu.VMEM((8, HALF), F32)],
        compiler_params=_cp("arbitrary"))(qkc, u, u, dh, cst, nst, mst, pre, u, u, cw)


def _head_norm(o):
    rs_parts, r_parts = [], []
    for h in range(HEADS):
        oh = o[:, 128 * h:128 * (h + 1)]
        rs = lax.rsqrt(jnp.mean(oh * oh, axis=-1, keepdims=True) + RMS_EPS)
        rs_parts.append(rs)
        r_parts.append(oh * rs)
    return jnp.concatenate(r_parts, axis=1), rs_parts


def _out_proj(x, u, o_hg, h_ml, g_hg, g_ml, w_out, tm):
    T = x.shape[0]

    def body(x_ref, hgate_ref, mo_ref, ohg_ref, hml_ref, ghg_ref, gml_ref, w_ref, m_ref, z_ref):
        hgate = hgate_ref[...]
        a = _head_norm(ohg_ref[...])[0] * ghg_ref[...] * (hgate * _sig(hgate))
        b = _head_norm(hml_ref[...])[0] * gml_ref[...] * _sig(mo_ref[...])
        m = jnp.concatenate([a, b], axis=1)
        m_ref[...] = m.astype(m_ref.dtype)
        z_ref[...] = ALPHA * x_ref[...] + _dot(m, w_ref[...])

    return pl.pallas_call(
        body, name="out_proj", grid=(T // tm,),
        in_specs=[_row(tm, D_MODEL), _row(tm, HALF, C_HGATE // 4), _row(tm, HALF, C_MO // 4),
                  _row(tm, HALF), _row(tm, HALF), _full(g_hg.shape), _full(g_ml.shape), _full(w_out.shape)],
        out_specs=[_row(tm, D_MODEL)] * 2,
        out_shape=[_sds((T, D_MODEL), _MXU), _sds((T, D_MODEL))],
        compiler_params=_cp("parallel"))(x, u, u, o_hg, h_ml, g_hg, g_ml, w_out)


def _ffn_ln(z1, ln1_g, ln1_b, wg, wu, wd, ln_g, ln_b, tm):
    T = z1.shape[0]

    def body(z1_ref, g1_ref, b1_ref, wg_ref, wu_ref, wd_ref, g_ref, b_ref, z_ref, x1_ref, x2_ref, a_ref, bb_ref, h_ref):
        x = _ln_fwd(z1_ref[...], g1_ref[...], b1_ref[...])[0]
        x1_ref[...] = x.astype(x1_ref.dtype)
        a = _dot(x, wg_ref[...], NT)
        bb = _dot(x, wu_ref[...], NT)
        hh = a * _sig(a) * bb
        a_ref[...] = a.astype(a_ref.dtype)
        bb_ref[...] = bb.astype(bb_ref.dtype)
        h_ref[...] = hh.astype(h_ref.dtype)
        z = ALPHA * x + _dot(hh, wd_ref[...])
        z_ref[...] = z
        x2_ref[...] = _ln_fwd(z, g_ref[...], b_ref[...])[0].astype(x2_ref.dtype)

    vec = _full((1, D_MODEL))
    return pl.pallas_call(
        body, name="ffn_ln2", grid=(T // tm,),
        in_specs=[_row(tm, D_MODEL), vec, vec, _full(wg.shape), _full(wu.shape), _full(wd.shape), vec, vec],
        out_specs=[_row(tm, D_MODEL)] * 3 + [_row(tm, D_FF)] * 3,
        out_shape=[_sds((T, D_MODEL))] + [_sds((T, D_MODEL), _MXU)] * 2 + [_sds((T, D_FF), _MXU)] * 3,
        compiler_params=_cp("parallel"))(z1, ln1_g, ln1_b, wg, wu, wd, ln_g, ln_b)


def _ple_loss_ln2_bwd(z2, p, tgt, wpg, bpg, wpp, ln_g, ln_b, tm):
    T = z2.shape[0]

    def body(z_ref, p_ref, t_ref, wpg_ref, bpg_ref, wpp_ref, g_ref, b_ref,
             de_ref, dgp_ref, dz_ref, loss_ref, dbpg_ref, dg_ref, db_ref):
        @pl.when(pl.program_id(0) == 0)
        def _():
            for r in (loss_ref, dbpg_ref, dg_ref, db_ref):
                r[...] = jnp.zeros_like(r)

        x2, xhat, rstd = _ln_fwd(z_ref[...], g_ref[...], b_ref[...])
        gate = _sig(_dot(x2, wpg_ref[...]) + bpg_ref[...])
        e = _dot(p_ref[...], wpp_ref[...])
        err = x2 + gate * e - t_ref[...]
        loss_ref[...] += _colsum(err * err)
        dy = err * (1.0 / D_MODEL)
        de_ref[...] = (dy * gate).astype(de_ref.dtype)
        dgp = dy * e * gate * (1.0 - gate)
        dgp_ref[...] = dgp.astype(dgp_ref.dtype)
        dbpg_ref[...] += _colsum(dgp)
        dx2 = dy + _dot(dgp, wpg_ref[...], NT)
        dg_ref[...] += _colsum(dx2 * xhat)
        db_ref[...] += _colsum(dx2)
        dz_ref[...] = _ln_bwd(dx2, xhat, rstd, g_ref[...])

    vec = _full((1, D_MODEL))
    return pl.pallas_call(
        body, name="ple_loss_ln2_bwd", grid=(T // tm,),
        in_specs=[_row(tm, D_MODEL), _row(tm, PLE_DIM), _row(tm, D_MODEL),
                  _full(wpg.shape), vec, _full(wpp.shape), vec, vec],
        out_specs=[_row(tm, D_MODEL)] * 3 + [vec] * 4,
        out_shape=[_sds((T, D_MODEL), _MXU)] * 2 + [_sds((T, D_MODEL))] + [_sds((1, D_MODEL))] * 4,
        compiler_params=_cp("arbitrary"))(z2, p, tgt, wpg, bpg, wpp, ln_g, ln_b)


def _ffn_bwd_ln1_bwd(a_pre, b_pre, z1, dz2, wg, wu, wd, ln_g, ln_b, tm, comms=None):
    T = z1.shape[0]

    def body(a_ref, bb_ref, z_ref, dz2_ref, wg_ref, wu_ref, wd_ref, g_ref, b_ref,
             da_ref, dbb_ref, dz1_ref, dg_ref, db_ref):
        @pl.when(pl.program_id(0) == 0)
        def _():
            dg_ref[...] = jnp.zeros_like(dg_ref)
            db_ref[...] = jnp.zeros_like(db_ref)

        dz2 = dz2_ref[...]
        a = a_ref[...].astype(F32)
        bb = bb_ref[...].astype(F32)
        sa = _sig(a)
        act = a * sa
        dh = _dot(dz2, wd_ref[...], NT)
        da = (dh * bb * _dsilu(a, sa)).astype(da_ref.dtype)
        dbb = (dh * act).astype(dbb_ref.dtype)
        da_ref[...] = da
        dbb_ref[...] = dbb
        dx1 = ALPHA * dz2 + _dot(da, wg_ref[...]) + _dot(dbb, wu_ref[...])
        _, xhat, rstd = _ln_fwd(z_ref[...], g_ref[...], b_ref[...])
        dg_ref[...] += _colsum(dx1 * xhat)
        db_ref[...] += _colsum(dx1)
        dz1_ref[...] = _ln_bwd(dx1, xhat, rstd, g_ref[...])

    vec = _full((1, D_MODEL))
    return _hosted_call(
        body, comms, name="ffn_bwd_ln1_bwd", grid=(T // tm,),
        in_specs=[_row(tm, D_FF)] * 2 + [_row(tm, D_MODEL)] * 2 + [_full(wg.shape), _full(wu.shape), _full(wd.shape), vec, vec],
        out_specs=[_row(tm, D_FF)] * 2 + [_row(tm, D_MODEL), vec, vec],
        out_shape=[_sds((T, D_FF), _MXU)] * 2 + [_sds((T, D_MODEL)), _sds((1, D_MODEL)), _sds((1, D_MODEL))],
        scratch_shapes=[], args=(a_pre, b_pre, z1, dz2, wg, wu, wd, ln_g, ln_b))


def _out_proj_bwd(dz1, u, o_hg, h_ml, g_hg, g_ml, w_out, tm):
    T = dz1.shape[0]

    def body(dz_ref, hgate_ref, mo_ref, ohg_ref, hml_ref, ghg_ref, gml_ref, w_ref,
             dohg_ref, dhml_ref, dhgate_ref, dmo_ref, dghg_ref, dgml_ref):
        @pl.when(pl.program_id(0) == 0)
        def _():
            dghg_ref[...] = jnp.zeros_like(dghg_ref)
            dgml_ref[...] = jnp.zeros_like(dgml_ref)

        dm = _dot(dz_ref[...], w_ref[...], NT)

        def half(dmh, o, gvec, gate_val, dgate_fac, do_ref, dgate_ref, dgvec_ref):
            r, rs = _head_norm(o)
            dgate_ref[...] = (dmh * r * gvec * dgate_fac).astype(dgate_ref.dtype)
            dn = dmh * gate_val
            dgvec_ref[...] += _colsum(dn * r)
            dr = dn * gvec
            parts = []
            for h in range(HEADS):
                sl = slice(128 * h, 128 * (h + 1))
                parts.append(rs[h] * (dr[:, sl] - r[:, sl] * jnp.mean(dr[:, sl] * r[:, sl], axis=-1, keepdims=True)))
            do_ref[...] = jnp.concatenate(parts, axis=1)

        hg = hgate_ref[...]
        shg = _sig(hg)
        half(dm[:, :HALF], ohg_ref[...], ghg_ref[...], hg * shg, _dsilu(hg, shg), dohg_ref, dhgate_ref, dghg_ref)
        smo = _sig(mo_ref[...])
        half(dm[:, HALF:], hml_ref[...], gml_ref[...], smo, smo * (1.0 - smo), dhml_ref, dmo_ref, dgml_ref)

    vec = _full((1, HALF))
    return pl.pallas_call(
        body, name="out_proj_bwd", grid=(T // tm,),
        in_specs=[_row(tm, D_MODEL), _row(tm, HALF, C_HGATE // 4), _row(tm, HALF, C_MO // 4),
                  _row(tm, HALF), _row(tm, HALF), vec, vec, _full(w_out.shape)],
        out_specs=[_row(tm, HALF)] * 4 + [vec, vec],
        out_shape=[_sds((T, HALF))] * 2 + [_sds((T, HALF), _MXU)] * 2 + [_sds((1, HALF))] * 2,
        compiler_params=_cp("arbitrary"))(dz1, u, u, o_hg, h_ml, g_hg, g_ml, w_out)


def _du_specs(rows):
    return [pl.BlockSpec((rows, w), lambda i: (i, 0)) for w in DU_WIDTHS]


def _in_proj_bwd(dz1, du_parts, w, tm, comms=None):
    T = dz1.shape[0]

    def body(dz_ref, *refs):
        du = jnp.concatenate([r[...] for r in refs[:8]], axis=1)
        refs[9][...] = ALPHA * dz_ref[...] + _dot(du, refs[8][...], NT)

    (dx,), got = _hosted_call(
        body, comms, name="in_proj_bwd", grid=(T // tm,),
        in_specs=[_row(tm, D_MODEL)] + _du_specs(tm) + [_full(w.shape)],
        out_specs=[_row(tm, D_MODEL)], out_shape=[_sds((T, D_MODEL))], scratch_shapes=[], args=(dz1, *du_parts, w))
    return dx, got


def _wgrad(a, b, name, tm, tn, tk):
    T, M = a.shape
    N = b.shape[1]
    tm, tn, tk = min(tm, M), min(tn, N), min(tk, T)
    nk = T // tk

    def body(a_ref, b_ref, o_ref, acc_ref):
        kk = pl.program_id(2)

        @pl.when(kk == 0)
        def _():
            acc_ref[...] = jnp.zeros_like(acc_ref)

        acc_ref[...] += _dot(a_ref[...], b_ref[...], TN)

        @pl.when(kk == nk - 1)
        def _():
            o_ref[...] = acc_ref[...].astype(o_ref.dtype)

    return pl.pallas_call(
        body, name=name, grid=(M // tm, N // tn, nk),
        in_specs=[pl.BlockSpec((tk, tm), lambda i, j, kk: (kk, i)), pl.BlockSpec((tk, tn), lambda i, j, kk: (kk, j))],
        out_specs=pl.BlockSpec((tm, tn), lambda i, j, kk: (i, j)), out_shape=_sds((M, N), _MXU),
        scratch_shapes=[pltpu.VMEM((tm, tn), F32)],
        compiler_params=_cp("parallel", "parallel", "arbitrary"))(a, b)


W_IN_PARTS = 2


def _wgrad_w_in(x, du_parts, tk, part, comms=None):
    T = x.shape[0]
    M = D_MODEL // W_IN_PARTS
    tk = min(tk, T)
    nk = T // tk

    def body(a_ref, *refs):
        o_ref, cs_ref, acc_ref = refs[8:]
        kk = pl.program_id(0)

        @pl.when(kk == 0)
        def _():
            acc_ref[...] = jnp.zeros_like(acc_ref)
            cs_ref[...] = jnp.zeros_like(cs_ref)

        du = jnp.concatenate([r[...] for r in refs[:8]], axis=1)
        acc_ref[...] += _dot(a_ref[...], du, TN)
        cs_ref[...] += _colsum(du.astype(F32))

        @pl.when(kk == nk - 1)
        def _():
            o_ref[...] = acc_ref[...].astype(o_ref.dtype)

    return _hosted_call(
        body, comms, name="wgrad_w_in_%d" % part, grid=(nk,),
        in_specs=[pl.BlockSpec((tk, M), lambda kk: (kk, part))] + _du_specs(tk),
        out_specs=[_full((M, PROJ_WP)), _full((1, PROJ_WP))],
        out_shape=[_sds((M, PROJ_WP), _MXU), _sds((1, PROJ_WP))],
        scratch_shapes=[pltpu.VMEM((M, PROJ_WP), F32)], args=(x, *du_parts))


W_IN_S, FF_S, OUT_S, PP_S = PROJ_W // N_DEV, D_FF // N_DEV, D_MODEL // N_DEV, D_MODEL // N_DEV
LATE = ("w_ffn_gate", "w_ffn_up", "w_out", "w_ffn_down", "ple_w_gate", "ple_w_proj")
BIG = ("w_in",) + LATE
TRANSPOSED = ("w_ffn_gate", "w_ffn_up")


def _split_cols(a, n):
    return a.reshape(a.shape[0], N_DEV, n).transpose(1, 0, 2)


def _join_cols(a):
    return a.transpose(1, 0, 2).reshape(a.shape[1], -1)


def _step(x, p, tgt, w_in, b_in, lb_logits, conv_w, conv_b, g_hg, g_ml, ln1_g, ln1_b, ln2_g, ln2_b, bpg, late,
          distributed):
    T = x.shape[0]
    tm, tf = min(ROWS, T), min(ROWS_FFN, T)
    gather = lambda *names: [_GatherTwoLevel([late[n] for n in names])] if distributed else None
    scatter = lambda *arrs: [_Comm("scatter", list(arrs))] if distributed else None
    rows = lambda a, n: a.reshape(N_DEV, n, D_MODEL)
    (u, pre, qkc), got1 = _in_proj(x, w_in, b_in, conv_w, conv_b, tm, gather("w_out", "ple_w_gate", "ple_w_proj"))
    (o_hg, hg_states), got2 = _hgrn2_fwd(u, lb_logits, gather("w_ffn_gate", "w_ffn_up"))
    (h_ml, cst, nst, mst), got3 = _mlstm_fwd(qkc, u, gather("w_ffn_down"))
    if distributed:
        w_out, wpg, wpp = got1[0][0].reshape(D_MODEL, D_MODEL), got1[0][1].reshape(D_MODEL, D_MODEL), _join_cols(got1[0][2])
        wg, wu, wd = (a.reshape(D_FF, D_MODEL) for a in (got2[0][0], got2[0][1], got3[0][0]))
    else:
        w_out, wg, wu, wd, wpg, wpp = (late[n] for n in ("w_out", "w_ffn_gate", "w_ffn_up", "w_ffn_down", "ple_w_gate", "ple_w_proj"))
    m_in, z1 = _out_proj(x, u, o_hg, h_ml, g_hg, g_ml, w_out, tm)
    z2, x1, x2, a_pre, b_pre, hh = _ffn_ln(z1, ln1_g, ln1_b, wg, wu, wd, ln2_g, ln2_b, tf)
    de, dgp, dz2, loss_vec, d_bpg, d_ln2g, d_ln2b = _ple_loss_ln2_bwd(z2, p, tgt, wpg, bpg, wpp, ln2_g, ln2_b, tm)
    big = dict(ple_w_gate=_wgrad(x2, dgp, "wgrad_ple_gate", D_MODEL, D_MODEL, 2048),
               ple_w_proj=_wgrad(p, de, "wgrad_ple_proj", 512, D_MODEL, 2048))
    (da, dbb, dz1, d_ln1g, d_ln1b), r1 = _ffn_bwd_ln1_bwd(
        a_pre, b_pre, z1, dz2, wg, wu, wd, ln1_g, ln1_b, tf, scatter(rows(big["ple_w_gate"], OUT_S), _split_cols(big["ple_w_proj"], PP_S)))
    big.update(
        w_ffn_gate=_wgrad(da, x1, "wgrad_ffn_gate", D_FF, D_MODEL, 1024),
        w_ffn_up=_wgrad(dbb, x1, "wgrad_ffn_up", D_FF, D_MODEL, 1024),
        w_ffn_down=_wgrad(hh, dz2, "wgrad_ffn_down", D_FF, D_MODEL, 1024),
        w_out=_wgrad(m_in, dz1, "wgrad_w_out", D_MODEL, D_MODEL, 2048))
    d_ohg, d_hml, d_hgate, d_mo, d_ghg, d_gml = _out_proj_bwd(dz1, u, o_hg, h_ml, g_hg, g_ml, w_out, tm)
    (d_hq, d_hf, d_hv, d_lb), r2 = _hgrn2_bwd(
        u, lb_logits, d_ohg, hg_states,
        scatter(rows(big["w_ffn_gate"], FF_S), rows(big["w_ffn_up"], FF_S), rows(big["w_ffn_down"], FF_S),
                rows(big["w_out"], OUT_S)))
    d_mqk, d_mv, d_gates, d_convw, d_convb = _mlstm_bwd(qkc, u, d_hml, cst, nst, mst, pre, conv_w)
    du_parts = [d_hq, d_hf, d_hv, d_hgate, d_mqk, d_mv, d_mo, d_gates]
    own = lambda g: _split_cols(g[:, :PROJ_W], W_IN_S)
    (g_in0, d_bin), _ = _wgrad_w_in(x, du_parts, 512, 0)
    (g_in1, _), r_in0 = _wgrad_w_in(x, du_parts, 512, 1, scatter(own(g_in0)))
    big["w_in"] = jnp.concatenate([g_in0, g_in1], axis=0)
    small = dict(b_in=d_bin[:, :PROJ_W], hg_lb_logits=d_lb, ml_conv_w=d_convw, ml_conv_b=d_convb, hg_norm_g=d_ghg,
                 ml_norm_g=d_gml, ln1_g=d_ln1g, ln1_b=d_ln1b, ln2_g=d_ln2g, ln2_b=d_ln2b, ple_b_gate=d_bpg)
    last = [_Comm("scatter", [own(g_in1)]), _Comm("gather", [loss_vec] + [small[n] for n in SMALL])] if distributed else None
    dx, r3 = _in_proj_bwd(dz1, du_parts, w_in, tm, last)
    gathered_small = None
    if distributed:
        big = dict(ple_w_gate=r1[0][0], ple_w_proj=r1[0][1], w_ffn_gate=r2[0][0], w_ffn_up=r2[0][1],
                   w_ffn_down=r2[0][2], w_out=r2[0][3], w_in=[r_in0[0][0], r3[0][0]])
        gathered_small = r3[1]
    return loss_vec, dx, big, small, gathered_small


SMALL = ("b_in", "hg_lb_logits", "ml_conv_w", "ml_conv_b", "hg_norm_g", "ml_norm_g", "ln1_g", "ln1_b", "ln2_g", "ln2_b",
         "ple_b_gate")


def _padc(a, n):
    return jnp.pad(a, [(0, 0)] * (a.ndim - 1) + [(0, n - a.shape[-1])])


def _adamw(w, g, m, v):
    m = B1 * m + (1.0 - B1) * g
    v = B2 * v + (1.0 - B2) * jnp.square(g)
    m_hat = m / (1.0 - B1 ** STEP)
    v_hat = v / (1.0 - B2 ** STEP)
    return -LR * (m_hat / (jnp.sqrt(v_hat) + EPS) + WD * w), m, v


def _sum_slabs(ref):
    g = ref[0].astype(F32)
    for j in range(1, N_DEV):
        g = g + ref[j].astype(F32)
    return g


def _adamw_matrix(rbs, w, m, v, name):
    rbs = list(rbs) if isinstance(rbs, (list, tuple)) else [rbs]
    nb = len(rbs)
    R, C = w.shape
    rows = R // nb
    tr = 256 if rows % 256 == 0 else rows // 2 if rows % 32 == 0 else rows
    per = rows // tr

    def body(*refs):
        w_ref, m_ref, v_ref, g_ref, d_ref, m2_ref, v2_ref = refs[nb:]
        i = pl.program_id(0)
        g = _sum_slabs(refs[0])
        for k in range(1, nb):
            g = jnp.where(i >= k * per, _sum_slabs(refs[k]), g)
        g_ref[...] = g
        d_ref[...], m2_ref[...], v2_ref[...] = _adamw(w_ref[...], g, m_ref[...], v_ref[...])

    blk = pl.BlockSpec((tr, C), lambda i: (i, 0))
    part = lambda k: pl.BlockSpec((N_DEV, tr, C), lambda i, k=k: (0, jnp.clip(i - k * per, 0, per - 1), 0))
    return pl.pallas_call(
        body, name=name, grid=(R // tr,),
        in_specs=[part(k) for k in range(nb)] + [blk, blk, blk],
        out_specs=[blk] * 4, out_shape=[_sds((R, C))] * 4, compiler_params=_cp("parallel"))(*rbs, w, m, v)


def _adamw_small(loss_g, gs, ws, ms, vs):
    n = len(ws)

    def body(*refs):
        loss_ref, g_refs, w_refs, m_refs, v_refs = refs[0], refs[1:1 + n], refs[1 + n:1 + 2 * n], refs[1 + 2 * n:1 + 3 * n], refs[1 + 3 * n:1 + 4 * n]
        outs = refs[1 + 4 * n:]
        outs[0][...] = (0.5 / D_MODEL) * jnp.sum(_sum_slabs(loss_ref), keepdims=True)
        for i in range(n):
            g = _sum_slabs(g_refs[i])
            outs[1 + i][...] = g
            outs[1 + n + i][...], outs[1 + 2 * n + i][...], outs[1 + 3 * n + i][...] = _adamw(
                w_refs[i][...], g, m_refs[i][...], v_refs[i][...])

    res = pl.pallas_call(
        body, name="adamw_small", out_shape=[_sds((1, 1))] + [_sds(w.shape) for w in ws] * 4)(loss_g, *gs, *ws, *ms, *vs)
    return res[0], [res[1 + k * n:1 + (k + 1) * n] for k in range(4)]


WEIGHTS = ("w_in", "b_in", "hg_lb_logits", "ml_conv_w", "ml_conv_b", "hg_norm_g", "ml_norm_g", "w_out", "ln1_g", "ln1_b",
           "w_ffn_gate", "w_ffn_up", "w_ffn_down", "ln2_g", "ln2_b", "ple_w_proj", "ple_w_gate", "ple_b_gate")
CONV_S = HALF // N_DEV


def kernel(x, p, w_in, b_in, hg_lb_logits, ml_conv_w, ml_conv_b, hg_norm_g, ml_norm_g, w_out, ln1_g, ln1_b, w_ffn_gate, w_ffn_up, w_ffn_down, ln2_g, ln2_b, ple_w_proj, ple_w_gate, ple_b_gate, loss_target, m_w_in, m_b_in, m_hg_lb_logits, m_ml_conv_w, m_ml_conv_b, m_hg_norm_g, m_ml_norm_g, m_w_out, m_ln1_g, m_ln1_b, m_w_ffn_gate, m_w_ffn_up, m_w_ffn_down, m_ln2_g, m_ln2_b, m_ple_w_proj, m_ple_w_gate, m_ple_b_gate, v_w_in, v_b_in, v_hg_lb_logits, v_ml_conv_w, v_ml_conv_b, v_hg_norm_g, v_ml_norm_g, v_w_out, v_ln1_g, v_ln1_b, v_w_ffn_gate, v_w_ffn_up, v_w_ffn_down, v_ln2_g, v_ln2_b, v_ple_w_proj, v_ple_w_gate, v_ple_b_gate):
    args = locals()
    me = 4 * lax.axis_index("x") + 2 * lax.axis_index("y") + lax.axis_index("c")
    shapes = {n: args[n].shape for n in WEIGHTS}
    def drop(n, a):
        a = a[0] if n in BIG or n == "ml_conv_w" else a
        return a.T if n in TRANSPOSED else a

    W = {n: drop(n, args[n]) for n in WEIGHTS}
    M = {n: drop(n, args["m_" + n]) for n in WEIGHTS}
    V = {n: drop(n, args["v_" + n]) for n in WEIGHTS}

    g_in, g_conv = _gather_two_level(
        [W["w_in"].astype(_MXU), jnp.pad(W["ml_conv_w"], ((0, 4), (0, 128 - CONV_S)))], "gather_w_in")
    w_in_full = _padc(_join_cols(g_in), PROJ_WP)
    conv_full = _join_cols(g_conv[:, :4, :CONV_S])

    _, dx, big, _, sg = _step(
        x[0], p[0, 0], loss_target[0], w_in_full, _padc(b_in, PROJ_WP), hg_lb_logits, conv_full, ml_conv_b,
        hg_norm_g, ml_norm_g, ln1_g, ln1_b, ln2_g, ln2_b, ple_b_gate, {n: W[n].astype(_MXU) for n in LATE}, True)

    upd = {n: _adamw_matrix(big[n], W[n], M[n], V[n], "adamw_" + n) for n in BIG}
    sg = dict(zip(SMALL, sg[1:]), loss=sg[0])
    sg["ml_conv_w"] = lax.dynamic_slice(sg["ml_conv_w"], (0, 0, me * CONV_S), (N_DEV, 4, CONV_S))
    loss, small_upd = _adamw_small(sg["loss"], *[[d[n] for n in SMALL] for d in (sg, W, M, V)])

    outs = []
    for kind in range(4):
        smalls = dict(zip(SMALL, small_upd[kind]))
        for n in WEIGHTS:
            o = upd[n][kind] if n in BIG else smalls[n]
            outs.append((o.T if n in TRANSPOSED else o).reshape(shapes[n]))
    return (loss.reshape(()), dx.reshape(x.shape), *outs)
```

```python
import jax
import jax.numpy as jnp
from jax import lax
from jax.experimental import pallas as pl
from jax.experimental.pallas import tpu as pltpu

F32 = jnp.float32
_MXU = jnp.bfloat16

D_MODEL = 1024
CHUNK = 64
SUB = 16
PLE_DIM = 256
HEADS = 4
ML_DQK = 64
HALF = 512
D_FF = 2816
PROJ_W = 3592
PROJ_WP = 3712
ALPHA = float(2 ** 0.25)
LN_EPS = 1e-5
RMS_EPS = 1e-6
ML_SCALE = ML_DQK ** -0.5
N_DEV = 8
LR, B1, B2, EPS, WD, STEP = 0.001, 0.9, 0.999, 1e-08, 0.01, 10
NEG = -1e30
LOG2E = 1.4426950408889634

C_HQ, C_HF, C_HV, C_HGATE, C_MQK, C_MV, C_MO, C_GATES = 0, 4, 8, 12, 16, 20, 24, 28
DU_WIDTHS = (HALF,) * 7 + (128,)

VMEM_LIMIT = 52 * 1024 * 1024
GC = 8
ROWS = 512
ROWS_FFN = 256

NN = (((1,), (0,)), ((), ()))
NT = (((1,), (1,)), ((), ()))
TN = (((0,), (0,)), ((), ()))
BNT = (((2,), (2,)), ((0,), (0,)))
BNN = (((2,), (1,)), ((0,), (0,)))
BTN = (((1,), (1,)), ((0,), (0,)))


def _dot(a, b, dims=NN):
    return lax.dot_general(a.astype(_MXU), b.astype(_MXU), dims, preferred_element_type=F32)


def _sig(x):
    return jax.nn.sigmoid(x)


def _cp(*sem):
    return pltpu.CompilerParams(dimension_semantics=sem, vmem_limit_bytes=VMEM_LIMIT)


def _row(tm, c, blk=0):
    return pl.BlockSpec((tm, c), lambda i, blk=blk: (i, blk))


def _full(shape):
    nd = len(shape)
    return pl.BlockSpec(tuple(shape), lambda *_, nd=nd: (0,) * nd)


def _sds(shape, dtype=F32):
    return jax.ShapeDtypeStruct(tuple(shape), dtype)


def _iota(shape, axis):
    return lax.broadcasted_iota(jnp.int32, shape, axis)


def _colsum(x):
    return jnp.sum(x, axis=0, keepdims=True)


def _rowsum(x):
    return jnp.sum(x, axis=1, keepdims=True)


def _ln_fwd(z, g, b):
    mu = jnp.mean(z, axis=-1, keepdims=True)
    zc = z - mu
    var = jnp.mean(zc * zc, axis=-1, keepdims=True)
    rstd = lax.rsqrt(var + LN_EPS)
    xhat = zc * rstd
    return xhat * g + b, xhat, rstd


def _ln_bwd(dy, xhat, rstd, g):
    dxh = dy * g
    m1 = jnp.mean(dxh, axis=-1, keepdims=True)
    m2 = jnp.mean(dxh * xhat, axis=-1, keepdims=True)
    return rstd * (dxh - m1 - xhat * m2)


def _dsilu(x, s):
    return s * (1.0 + x * (1.0 - s))


MESH = pl.DeviceIdType.MESH
ANY = pl.BlockSpec(memory_space=pl.ANY)


def _flip(v, bit):
    return 1 - v if bit else v


class _Comm:
    def __init__(self, kind, srcs):
        self.kind, self.srcs, self.n = kind, list(srcs), len(srcs)

    def out_shape(self):
        lead = (N_DEV,) if self.kind == "gather" else ()
        return [jax.ShapeDtypeStruct(lead + s.shape, s.dtype) for s in self.srcs]

    def scratch(self):
        return [pltpu.SemaphoreType.DMA((7 * self.n,)), pltpu.SemaphoreType.DMA((7 * self.n,)),
                pltpu.SemaphoreType.DMA((self.n,))]

    def copies(self, srcs, dsts, send_sems, recv_sems, local_sems):
        x, y, c = lax.axis_index("x"), lax.axis_index("y"), lax.axis_index("c")
        me = 4 * x + 2 * y + c
        pick = (lambda s, j: s) if self.kind == "gather" else (lambda s, j: s.at[j])
        out = []
        for i, (s, d) in enumerate(zip(srcs, dsts)):
            out.append(pltpu.make_async_copy(pick(s, me), d.at[me], local_sems.at[i]))
            for k in range(1, N_DEV):
                px, py, pc = _flip(x, k & 4), _flip(y, k & 2), _flip(c, k & 1)
                out.append(pltpu.make_async_remote_copy(
                    src_ref=pick(s, 4 * px + 2 * py + pc), dst_ref=d.at[me], send_sem=send_sems.at[7 * i + k - 1],
                    recv_sem=recv_sems.at[7 * i + k - 1], device_id=(px, py, pc), device_id_type=MESH))
        return out

    def start(self, *refs):
        for cp in self.copies(*refs):
            cp.start()

    def mid(self, *refs):
        pass

    def finish(self, *refs):
        for cp in self.copies(*refs):
            cp.wait()


class _GatherTwoLevel(_Comm):
    def __init__(self, srcs):
        super().__init__("gather", srcs)

    def _parts(self, srcs, dsts, send_sems, recv_sems, local_sems):
        x, y, c = lax.axis_index("x"), lax.axis_index("y"), lax.axis_index("c")
        me, sibling = (x, y, c), (x, y, 1 - c)
        chips = [(1 - x, y), (x, 1 - y), (1 - x, 1 - y)]

        def copy(i, k, block, to, own=False):
            slab = dsts[i].at[4 * block[0] + 2 * block[1] + block[2]]
            return pltpu.make_async_remote_copy(
                src_ref=srcs[i] if own else slab, dst_ref=slab, send_sem=send_sems.at[7 * i + k],
                recv_sem=recv_sems.at[7 * i + k], device_id=to, device_id_type=MESH)

        n = range(self.n)
        mine = [pltpu.make_async_copy(srcs[i], dsts[i].at[4 * x + 2 * y + c], local_sems.at[i]) for i in n]
        first = [copy(i, 0, me, sibling, own=True) for i in n]
        first += [copy(i, 1 + j, me, (*chip, c), own=True) for j, chip in enumerate(chips) for i in n]
        over_ici = [copy(i, 1 + j, (*chip, c), me) for j, chip in enumerate(chips) for i in n]
        passed = [copy(i, 4 + j, (*chip, c), sibling) for j, chip in enumerate(chips) for i in n]
        from_sibling = [copy(i, 0, sibling, me) for i in n]
        from_sibling += [copy(i, 4 + j, (*chip, 1 - c), me) for j, chip in enumerate(chips) for i in n]
        return mine, first, over_ici, passed, from_sibling

    def start(self, *refs):
        mine, first, _, _, _ = self._parts(*refs)
        for cp in mine + first:
            cp.start()

    def mid(self, *refs):
        _, _, over_ici, passed, _ = self._parts(*refs)
        for arrived, onward in zip(over_ici, passed):
            arrived.wait_recv()
            onward.start()

    def finish(self, *refs):
        mine, first, _, passed, from_sibling = self._parts(*refs)
        for cp in from_sibling:
            cp.wait_recv()
        for cp in first + passed:
            cp.wait_send()
        for cp in mine:
            cp.wait()


def _hosted_call(body, comms, *, name, grid, in_specs, out_specs, out_shape, scratch_shapes, args):
    comms = list(comms or [])
    if not comms:
        res = pl.pallas_call(body, name=name, grid=grid, in_specs=in_specs, out_specs=out_specs, out_shape=out_shape,
                             scratch_shapes=scratch_shapes, compiler_params=_cp("arbitrary"))(*args)
        return list(res), []
    n_in, n_out, n_sc, nc = len(in_specs), len(out_specs), len(scratch_shapes), sum(cm.n for cm in comms)
    last = grid[0] - 1

    def hosted(*refs):
        ins, csrc = refs[:n_in], refs[n_in:n_in + nc]
        o0 = n_in + nc
        outs, cdst = refs[o0:o0 + n_out], refs[o0 + n_out:o0 + n_out + nc]
        s0 = o0 + n_out + nc
        scr, sems = refs[s0:s0 + n_sc], refs[s0 + n_sc:]

        def phase(which):
            o = 0
            for j, cm in enumerate(comms):
                getattr(cm, which)(csrc[o:o + cm.n], cdst[o:o + cm.n], *sems[3 * j:3 * j + 3])
                o += cm.n

        i = pl.program_id(0)

        @pl.when(i == 0)
        def _():
            phase("start")

        body(*ins, *outs, *scr)

        @pl.when(i == (2 * last) // 3)
        def _():
            phase("mid")

        @pl.when(i == last)
        def _():
            phase("finish")

    res = pl.pallas_call(
        hosted, name=name, grid=grid, in_specs=list(in_specs) + [ANY] * nc, out_specs=list(out_specs) + [ANY] * nc,
        out_shape=list(out_shape) + [s for cm in comms for s in cm.out_shape()],
        scratch_shapes=list(scratch_shapes) + [s for cm in comms for s in cm.scratch()],
        compiler_params=_cp("arbitrary"))(*args, *[a for cm in comms for a in cm.srcs])
    got, o = [], n_out
    for cm in comms:
        got.append(list(res[o:o + cm.n]))
        o += cm.n
    return list(res[:n_out]), got


def _gather_two_level(blocks, name):
    n = len(blocks)

    def body(*refs):
        x_refs, out_refs = refs[:n], refs[n:2 * n]
        send_sems, recv_sems, local_sems = refs[2 * n:]
        x, y, c = lax.axis_index("x"), lax.axis_index("y"), lax.axis_index("c")
        me, sibling = (x, y, c), (x, y, 1 - c)
        chips = [(1 - x, y), (x, 1 - y), (1 - x, 1 - y)]

        def copy(i, k, block, to, own=False):
            slab = out_refs[i].at[4 * block[0] + 2 * block[1] + block[2]]
            return pltpu.make_async_remote_copy(
                src_ref=x_refs[i] if own else slab, dst_ref=slab, send_sem=send_sems.at[7 * i + k],
                recv_sem=recv_sems.at[7 * i + k], device_id=to, device_id_type=MESH)

        mine = [pltpu.make_async_copy(x_refs[i], out_refs[i].at[4 * x + 2 * y + c], local_sems.at[i]) for i in range(n)]
        for cp in mine:
            cp.start()
        first = [copy(i, 0, me, sibling, own=True) for i in range(n)]
        first += [copy(i, 1 + j, me, (*chip, c), own=True) for j, chip in enumerate(chips) for i in range(n)]
        for cp in first:
            cp.start()
        passed = []
        for j, chip in enumerate(chips):
            for i in range(n):
                copy(i, 1 + j, (*chip, c), me).wait_recv()
                passed.append(copy(i, 4 + j, (*chip, c), sibling))
                passed[-1].start()
        for i in range(n):
            copy(i, 0, sibling, me).wait_recv()
            for j, chip in enumerate(chips):
                copy(i, 4 + j, (*chip, 1 - c), me).wait_recv()
        for cp in first + passed:
            cp.wait_send()
        for cp in mine:
            cp.wait()

    return pl.pallas_call(
        body, name=name, out_shape=[jax.ShapeDtypeStruct((N_DEV,) + b.shape, b.dtype) for b in blocks],
        in_specs=[ANY] * n, out_specs=[ANY] * n,
        scratch_shapes=[pltpu.SemaphoreType.DMA((7 * n,)), pltpu.SemaphoreType.DMA((7 * n,)),
                        pltpu.SemaphoreType.DMA((n,))])(*blocks)


def _in_proj(x, w, b, cw, cb, tm, comms=None):
    T = x.shape[0]

    def body(x_ref, w_ref, b_ref, cw_ref, cb_ref, o_ref, pre_ref, act_ref, halo_sc):
        @pl.when(pl.program_id(0) == 0)
        def _():
            halo_sc[...] = jnp.zeros_like(halo_sc)

        o = _dot(x_ref[...], w_ref[...]) + b_ref[...]
        o_ref[...] = o
        xc = o[:, 128 * C_MQK:128 * C_MQK + HALF]
        halo = halo_sc[...]
        rowi = _iota((8, HALF), 0)
        acc = xc * cw_ref[3:4, :] + cb_ref[...]
        for j in (1, 2, 3):
            acc = acc + _shift_rows(xc, halo, j, rowi) * cw_ref[3 - j:4 - j, :]
        pre_ref[...] = acc
        act_ref[...] = acc * _sig(acc)
        halo_sc[...] = xc[tm - 8:]

    return _hosted_call(
        body, comms, name="in_proj", grid=(T // tm,),
        in_specs=[_row(tm, D_MODEL), _full(w.shape), _full(b.shape), _full(cw.shape), _full(cb.shape)],
        out_specs=[_row(tm, PROJ_WP), _row(tm, HALF), _row(tm, HALF)],
        out_shape=[_sds((T, PROJ_WP)), _sds((T, HALF)), _sds((T, HALF))],
        scratch_shapes=[pltpu.VMEM((8, HALF), F32)], args=(x, w, b, cw, cb))


def _shift_rows(x, halo, j, rowi):
    r = pltpu.roll(x, j, 0)
    top = jnp.where(rowi < j, pltpu.roll(halo, j, 0), r[:8])
    return jnp.concatenate([top, r[8:]], axis=0)


def _shift_rows_up(x, halo, j, rowi):
    n = x.shape[0]
    r = pltpu.roll(x, n - j, 0)
    bot = jnp.where(rowi >= 8 - j, pltpu.roll(halo, 8 - j, 0), r[n - 8:])
    return jnp.concatenate([r[:n - 8], bot], axis=0)


def _bdot(a, b, dims):
    return lax.dot_general(a.astype(_MXU), b.astype(_MXU), dims, preferred_element_type=F32)


def _bdotx(a, b, dims):
    return lax.dot_general(a, b, dims, precision=lax.Precision.HIGHEST, preferred_element_type=F32)


def _heads_to_batch(x, w):
    G = x.shape[0] // CHUNK
    x3 = x.reshape(G, CHUNK, HEADS * w)
    return jnp.stack([x3[:, :, w * h:w * (h + 1)] for h in range(HEADS)], axis=1).reshape(G * HEADS, CHUNK, w)


def _batch_to_heads(x3):
    B, _, w = x3.shape
    x4 = x3.reshape(B // HEADS, HEADS, CHUNK, w)
    return jnp.concatenate([x4[:, h] for h in range(HEADS)], axis=-1).reshape(B // HEADS * CHUNK, HEADS * w)


def _chunk_cumsum(x, rowmod, reverse=False):
    R = x.shape[0]
    for sh in (1, 2, 4, 8, 16, 32):
        if reverse:
            x = x + jnp.where(rowmod < CHUNK - sh, pltpu.roll(x, R - sh, 0), 0.0)
        else:
            x = x + jnp.where(rowmod >= sh, pltpu.roll(x, sh, 0), 0.0)
    return x


def _lane_col(x, c, lane):
    return _rowsum(jnp.where(lane == c, x, 0.0))


def _hg_gates(hq, hf, lb):
    sg = _sig(hf)
    nsg = _sig(-hf)
    f = lb + (1.0 - lb) * sg
    g = jnp.log(f)
    k = (1.0 - lb) * nsg
    sq = _sig(hq)
    return hq * sq, g, k, f, sg, nsg, sq


def _hg_prep(hq_ref, hf_ref, lg_ref, b_sc, k_sc):
    R = hq_ref.shape[0]
    G = R // CHUNK
    lb = _sig(lg_ref[0:1, :] - lg_ref[1:2, :])
    hq = hq_ref[...]
    q, g, k, f, sg, nsg, sq = _hg_gates(hq, hf_ref[...], lb)
    rowmod = _iota((R, HALF), 0) & (CHUNK - 1)
    b = _chunk_cumsum(g, rowmod) * LOG2E
    last8 = _iota((8, HALF), 0) == 7
    bl_rows = [_colsum(jnp.where(last8, b[CHUNK * c + CHUNK - 8:CHUNK * (c + 1)], 0.0)) for c in range(G)]
    bl3 = jnp.stack([r[:, 128 * h:128 * (h + 1)] for r in bl_rows for h in range(HEADS)], axis=0)
    b3, k3 = _heads_to_batch(b, 128), _heads_to_batch(k, 128)
    b_sc[...] = b3
    k_sc[...] = k3
    return dict(G=G, lb=lb, hq=hq, f=f, sg=sg, nsg=nsg, sq=sq, rowmod=rowmod, q3=_heads_to_batch(q, 128), k3=k3, b3=b3,
                bl3=bl3)


HSUB = 8


def _lo(j):
    return HSUB * (j // HSUB)


def _hg_diag_tiles(b_sc, b3, r0, rowi):
    bi = b3[:, r0:r0 + SUB]
    return [jnp.exp2(jnp.where(rowi[:, _lo(s):] >= s, bi[:, _lo(s):] - b_sc[:, r0 + s:r0 + s + 1, :], NEG))
            for s in range(SUB)]


def _hg_diag_tiles_t(b_sc, b3, r0, rowi):
    bi = b3[:, r0:r0 + SUB]
    return [jnp.exp2(jnp.where(rowi[:, :_lo(t) + HSUB] <= t, b_sc[:, r0 + t:r0 + t + 1, :] - bi[:, :_lo(t) + HSUB], NEG))
            for t in range(SUB)]


def _lane_sums(pieces, ones):
    B = pieces[0].shape[0]
    hs = [p.shape[1] for p in pieces]
    R = _dot(jnp.concatenate(pieces, axis=1).reshape(B * sum(hs), 128), ones).reshape(B, sum(hs), 128)
    out, o = [], 0
    for h in hs:
        out.append(R[:, o:o + h])
        o += h
    return out


def _sum_tri(terms, low_rows):
    if SUB == HSUB:
        return sum(terms)
    full = sum(t for t in terms if t.shape[1] == SUB)
    half = sum(t for t in terms if t.shape[1] == HSUB)
    lo, hi = full[:, :HSUB], full[:, HSUB:]
    return jnp.concatenate([lo + half, hi] if low_rows else [lo, hi + half], axis=1)


def _hgrn2_fwd(u, lb_logits, comms=None):
    T = u.shape[0]
    G = min(GC, T // CHUNK)
    R, B, N = G * CHUNK, G * HEADS, T // CHUNK

    def body(hq_ref, hf_ref, hv_ref, lg_ref, o_ref, st_ref, S_ref, b_sc, k_sc, v_sc):
        @pl.when(pl.program_id(0) == 0)
        def _():
            S_ref[...] = jnp.zeros_like(S_ref)

        pz = _hg_prep(hq_ref, hf_ref, lg_ref, b_sc, k_sc)
        q3, k3, b3, bl3 = pz["q3"], pz["k3"], pz["b3"], pz["bl3"]
        v3 = _heads_to_batch(hv_ref[...], 128)
        v_sc[...] = v3
        stloc = _bdot(v3, k3 * jnp.exp2(bl3 - b3), BTN).reshape(G, HEADS, 128, 128)
        dec = jnp.exp2(bl3).reshape(G, HEADS, 1, 128)
        ST = S_ref[...]
        sts = []
        for c in range(G):
            sts.append(ST)
            ST = ST * dec[c] + stloc[c]
        S_ref[...] = ST
        st4 = jnp.stack(sts, axis=0)
        st_ref[...] = st4
        o = _bdot(q3 * jnp.exp2(b3), st4.reshape(B, 128, 128), BNT)
        ones = jnp.ones((128, 128), F32)
        rowi = _iota((1, SUB, 128), 1)
        outs = []
        for i in range(CHUNK // SUB):
            r0 = SUB * i
            qi = q3[:, r0:r0 + SUB]
            oi = o[:, r0:r0 + SUB]
            if i > 0:
                r = b_sc[:, r0 - 1:r0, :]
                qe = qi * jnp.exp2(b3[:, r0:r0 + SUB] - r)
                ke = k3[:, :r0] * jnp.exp2(r - b3[:, :r0])
                oi = oi + _bdot(_bdot(qe, ke, BNT), v3[:, :r0], BNN)
            tiles = _hg_diag_tiles(b_sc, b3, r0, rowi)
            a_b = _lane_sums([qi[:, _lo(s):] * (k_sc[:, r0 + s:r0 + s + 1, :] * tiles[s]) for s in range(SUB)], ones)
            outs.append(oi + _sum_tri([a_b[s] * v_sc[:, r0 + s:r0 + s + 1, :] for s in range(SUB)], False))
        o_ref[...] = _batch_to_heads(jnp.concatenate(outs, axis=1))

    blk = lambda c: pl.BlockSpec((R, HALF), lambda n, c=c: (n, c // 4))
    return _hosted_call(
        body, comms, name="hgrn2_fwd", grid=(N // G,),
        in_specs=[blk(C_HQ), blk(C_HF), blk(C_HV), _full(lb_logits.shape)],
        out_specs=[pl.BlockSpec((R, HALF), lambda n: (n, 0)),
                   pl.BlockSpec((G, HEADS, 128, 128), lambda n: (n, 0, 0, 0))],
        out_shape=[_sds((T, HALF)), _sds((N, HEADS, 128, 128))],
        scratch_shapes=[pltpu.VMEM((HEADS, 128, 128), F32)] + [pltpu.VMEM((B, CHUNK, 128), F32)] * 3,
        args=(u, u, u, lb_logits))


def _hgrn2_bwd(u, lb_logits, do, states, comms=None):
    T = u.shape[0]
    G = min(GC, T // CHUNK)
    R, B, NG = G * CHUNK, G * HEADS, T // (G * CHUNK)

    def body(hq_ref, hf_ref, hv_ref, lg_ref, do_ref, st_ref, dhq_ref, dhf_ref, dhv_ref, dlb_ref,
             dS_ref, b_sc, k_sc, v_sc, q_sc, do_sc):
        @pl.when(pl.program_id(0) == 0)
        def _():
            dS_ref[...] = jnp.zeros_like(dS_ref)
            dlb_ref[...] = jnp.zeros_like(dlb_ref)

        pz = _hg_prep(hq_ref, hf_ref, lg_ref, b_sc, k_sc)
        q3, k3, b3, bl3, lb = pz["q3"], pz["k3"], pz["b3"], pz["bl3"], pz["lb"]
        v3 = _heads_to_batch(hv_ref[...], 128)
        v_sc[...] = v3
        do3 = _heads_to_batch(do_ref[...], 128)
        q_sc[...] = q3
        do_sc[...] = do3
        st3 = st_ref[...].reshape(B, 128, 128)
        eb = jnp.exp2(b3)
        ebl = jnp.exp2(bl3 - b3)
        qt = q3 * eb
        kl = k3 * ebl
        dstloc = _bdot(do3, qt, BTN).reshape(G, HEADS, 128, 128)
        dec = jnp.exp2(bl3).reshape(G, HEADS, 1, 128)
        dST = dS_ref[...]
        dsts = [None] * G
        for c in reversed(range(G)):
            dsts[c] = dST
            dST = dST * dec[c] + dstloc[c]
        dS_ref[...] = dST
        dst3 = jnp.stack(dsts, axis=0).reshape(B, 128, 128)
        dqt = _bdot(do3, st3, BNN)
        dkl = _bdot(v3, dst3, BNN)
        dv_acc = _bdot(kl, dst3, BNT)
        ones = jnp.ones((128, 128), F32)
        rowi = _iota((1, SUB, 128), 1)
        dq_parts, dk_parts, dv_parts = [], [], []
        dk_in = jnp.zeros((B, CHUNK, 128), F32)
        for i_s in range(CHUNK // SUB):
            r0 = SUB * i_s
            qi = q3[:, r0:r0 + SUB]
            doi = do3[:, r0:r0 + SUB]
            dqi = jnp.zeros((B, SUB, 128), F32)
            if i_s > 0:
                r = b_sc[:, r0 - 1:r0, :]
                eq = jnp.exp2(b3[:, r0:r0 + SUB] - r)
                ek = jnp.exp2(r - b3[:, :r0])
                qe = qi * eq
                ke = k3[:, :r0] * ek
                a_off = _bdot(qe, ke, BNT)
                p_off = _bdot(doi, v3[:, :r0], BNT)
                pad = jnp.zeros((B, CHUNK - r0, 128), F32)
                dv_acc = dv_acc + jnp.concatenate([_bdot(a_off, doi, BTN), pad], axis=1)
                dqi = dqi + _bdot(p_off, ke, BNN) * eq
                dk_in = dk_in + jnp.concatenate([_bdot(p_off, qe, BTN) * ek, pad], axis=1)
            ki, vi = k3[:, r0:r0 + SUB], v3[:, r0:r0 + SUB]
            rng = range(SUB)
            tiles = _hg_diag_tiles(b_sc, b3, r0, rowi)
            tiles_t = _hg_diag_tiles_t(b_sc, b3, r0, rowi)
            do_rows = [do_sc[:, r0 + t:r0 + t + 1, :] for t in rng]
            kts = [k_sc[:, r0 + s:r0 + s + 1, :] * tiles[s] for s in rng]
            qts = [q_sc[:, r0 + t:r0 + t + 1, :] * tiles_t[t] for t in rng]
            ps = [doi[:, _lo(s):] * v_sc[:, r0 + s:r0 + s + 1, :] for s in rng]
            mst = [ki[:, :_lo(t) + HSUB] * qts[t] for t in rng]
            pst = [vi[:, :_lo(t) + HSUB] * do_rows[t] for t in rng]
            sums = _lane_sums(ps + mst + pst, ones)
            p_b, a_t, p_t = sums[:SUB], sums[SUB:2 * SUB], sums[2 * SUB:]
            dq_parts.append(dqi + _sum_tri([p_b[s] * kts[s] for s in rng], False))
            dv_parts.append(_sum_tri([a_t[t] * do_rows[t] for t in rng], True))
            dk_parts.append(_sum_tri([p_t[t] * qts[t] for t in rng], True))
        dq_in = jnp.concatenate(dq_parts, axis=1)
        dk_in = dk_in + jnp.concatenate(dk_parts, axis=1)
        dv_acc = dv_acc + jnp.concatenate(dv_parts, axis=1)
        db = qt * dqt + q3 * dq_in - k3 * dk_in - kl * dkl
        last = jnp.sum(kl * dkl, axis=1, keepdims=True) + jnp.exp2(bl3) * jnp.sum(st3 * dst3, axis=1, keepdims=True)
        db = db + jnp.where(_iota((1, CHUNK, 1), 1) == CHUNK - 1, last, 0.0)
        dg = _chunk_cumsum(_batch_to_heads(db), pz["rowmod"], reverse=True)
        dq_tot = _batch_to_heads(dqt * eb + dq_in)
        dk_tot = _batch_to_heads(dkl * ebl + dk_in)
        common = dg / pz["f"] - dk_tot
        dhf_ref[...] = ((1.0 - lb) * pz["sg"] * pz["nsg"] * common).astype(dhf_ref.dtype)
        dl0 = _colsum(pz["nsg"] * common) * lb * (1.0 - lb)
        dlb_ref[0:1, :] += dl0
        dlb_ref[1:2, :] -= dl0
        dhq_ref[...] = (dq_tot * _dsilu(pz["hq"], pz["sq"])).astype(dhq_ref.dtype)
        dhv_ref[...] = _batch_to_heads(dv_acc).astype(dhv_ref.dtype)

    rev = lambda c: pl.BlockSpec((R, HALF), lambda i, c=c: (NG - 1 - i, c // 4))
    rev0 = pl.BlockSpec((R, HALF), lambda i: (NG - 1 - i, 0))
    return _hosted_call(
        body, comms, name="hgrn2_bwd", grid=(NG,),
        in_specs=[rev(C_HQ), rev(C_HF), rev(C_HV), _full(lb_logits.shape), rev0,
                  pl.BlockSpec((G, HEADS, 128, 128), lambda i: (NG - 1 - i, 0, 0, 0))],
        out_specs=[rev0, rev0, rev0, _full((2, HALF))],
        out_shape=[_sds((T, HALF), _MXU)] * 3 + [_sds((2, HALF))],
        scratch_shapes=[pltpu.VMEM((HEADS, 128, 128), F32)] + [pltpu.VMEM((B, CHUNK, 128), F32)] * 5,
        args=(u, u, u, lb_logits, do, states))


def _lanes_to_batch_cols(x, lane):
    G = x.shape[0] // CHUNK
    cols = [_lane_col(x, 4 + h, lane).reshape(G, CHUNK, 1) for h in range(HEADS)]
    return jnp.stack(cols, axis=1).reshape(G * HEADS, CHUNK, 1)


def _row_scalars(rows):
    lane = _iota((1, 128), 1)
    return jnp.stack([_rowsum(jnp.where(lane == 4 + h, r, 0.0)) for r in rows for h in range(HEADS)], axis=0)


def _ml_gates(gates):
    R = gates.shape[0]
    lane = _iota((R, 128), 1)
    rowmod = _iota((R, 128), 0) & (CHUNK - 1)
    lf = jnp.minimum(gates, 0.0) - jnp.log(1.0 + jnp.exp(-jnp.abs(gates)))
    g_all = _chunk_cumsum(lf, rowmod)
    x_all = pltpu.roll(gates, 4, 1) - g_all
    return g_all, x_all, lane, rowmod


def _ml_chunk_rows(g_all, x_all, mprev, g):
    gl = g_all[CHUNK * g + CHUNK - 8:CHUNK * (g + 1)]
    gl = _colsum(jnp.where(_iota((8, 128), 0) == 7, gl, 0.0))
    a = gl + x_all[CHUNK * g:CHUNK * (g + 1)]
    m_new = jnp.maximum(gl + mprev, jnp.max(a, axis=0, keepdims=True))
    return m_new, jnp.exp(gl + mprev - m_new), jnp.exp(a - m_new)


def _ml_batched(q3, k3, v3, g_all, x_all, lane, C3, n3, mprev3):
    G = g_all.shape[0] // CHUNK
    gcol3 = _lanes_to_batch_cols(g_all, lane)
    onehot = jnp.where(_iota((G, 8, 128), 1) + 4 == _iota((G, 8, 128), 2), 1.0, 0.0).astype(F32)
    rows = _bdotx(onehot, x_all.reshape(G, CHUNK, 128), BNT)
    sub = _iota((G, 8, CHUNK), 1)
    row3 = jnp.stack([jnp.sum(jnp.where(sub == h, rows, 0.0), axis=1, keepdims=True) for h in range(HEADS)],
                     axis=1).reshape(G * HEADS, 1, CHUNK)
    causal = _iota((1, CHUNK, CHUNK), 1) >= _iota((1, CHUNK, CHUNK), 2)
    dmat = jnp.where(causal, gcol3 + row3, NEG)
    m_inter = gcol3 + mprev3
    m_t = jnp.maximum(m_inter, jnp.max(dmat, axis=2, keepdims=True))
    wi = jnp.exp(dmat - m_t)
    wn = jnp.exp(m_inter - m_t)
    s3 = _bdot(q3, k3, BNT) * wi
    qc = _bdot(q3, C3, BNN)
    qn = jnp.sum(q3 * n3, axis=2, keepdims=True)
    num = _bdot(s3, v3, BNN) + wn * qc
    den = jnp.sum(s3, axis=2, keepdims=True) + wn * qn
    floor = jnp.exp(-m_t)
    return dict(wi=wi, wn=wn, s=s3, qc=qc, qn=qn, num=num, den=den, floor=floor, nrm=jnp.maximum(jnp.abs(den), floor))


def _mlstm_fwd(qkc, u, comms=None):
    T = u.shape[0]
    G = min(GC, T // CHUNK)
    R = G * CHUNK
    N = T // CHUNK

    def body(qk_ref, v_ref, g_ref, h_ref, cst_ref, nst_ref, mst_ref, C_ref, n_ref, m_ref):
        @pl.when(pl.program_id(0) == 0)
        def _():
            C_ref[...] = jnp.zeros_like(C_ref)
            n_ref[...] = jnp.zeros_like(n_ref)
            m_ref[...] = jnp.zeros_like(m_ref)

        g_all, x_all, lane, _ = _ml_gates(g_ref[...])
        m_row = m_ref[...]
        mprev_rows, wo_rows, ws_parts = [], [], []
        for g in range(G):
            mprev_rows.append(m_row)
            m_row, wo, ws = _ml_chunk_rows(g_all, x_all, m_row, g)
            wo_rows.append(wo)
            ws_parts.append(ws)
        m_ref[...] = m_row
        mst_ref[...] = jnp.stack(mprev_rows, axis=0)
        ws3 = _lanes_to_batch_cols(jnp.concatenate(ws_parts, axis=0), lane)
        wo4 = _row_scalars(wo_rows).reshape(G, HEADS, 1, 1)
        q3 = _heads_to_batch(qk_ref[:, :256] * ML_SCALE, ML_DQK)
        k3 = _heads_to_batch(qk_ref[:, 256:], ML_DQK)
        v3 = _heads_to_batch(v_ref[...], 128)
        kw = k3 * ws3
        cloc = _bdot(kw, v3, BTN).reshape(G, HEADS, ML_DQK, 128)
        nloc = jnp.sum(kw, axis=1, keepdims=True).reshape(G, HEADS, 1, ML_DQK)
        C, nn = C_ref[...], n_ref[...]
        cs, ns = [], []
        for g in range(G):
            cs.append(C)
            ns.append(nn)
            C = wo4[g] * C + cloc[g]
            nn = wo4[g] * nn + nloc[g]
        C_ref[...] = C
        n_ref[...] = nn
        c4, n4 = jnp.stack(cs, axis=0), jnp.stack(ns, axis=0)
        cst_ref[...] = c4
        nst_ref[...] = n4
        r = _ml_batched(q3, k3, v3, g_all, x_all, lane, c4.reshape(G * HEADS, ML_DQK, 128),
                        n4.reshape(G * HEADS, 1, ML_DQK), _row_scalars(mprev_rows))
        h_ref[...] = _batch_to_heads(r["num"] / r["nrm"])

    return _hosted_call(
        body, comms, name="mlstm_fwd", grid=(N // G,),
        in_specs=[pl.BlockSpec((R, HALF), lambda n: (n, 0)), pl.BlockSpec((R, HALF), lambda n: (n, C_MV // 4)),
                  pl.BlockSpec((R, 128), lambda n: (n, C_GATES))],
        out_specs=[pl.BlockSpec((R, HALF), lambda n: (n, 0)),
                   pl.BlockSpec((G, HEADS, ML_DQK, 128), lambda n: (n, 0, 0, 0)),
                   pl.BlockSpec((G, HEADS, 1, ML_DQK), lambda n: (n, 0, 0, 0)),
                   pl.BlockSpec((G, 1, 128), lambda n: (n, 0, 0))],
        out_shape=[_sds((T, HALF)), _sds((N, HEADS, ML_DQK, 128)), _sds((N, HEADS, 1, ML_DQK)), _sds((N, 1, 128))],
        scratch_shapes=[pltpu.VMEM((HEADS, ML_DQK, 128), F32), pltpu.VMEM((HEADS, 1, ML_DQK), F32),
                        pltpu.VMEM((1, 128), F32)],
        args=(qkc, u, u))


def _mlstm_bwd(qkc, u, dh, cst, nst, mst, pre, cw, comms=None):
    T = u.shape[0]
    G = min(GC, T // CHUNK)
    R = G * CHUNK
    NG = T // R

    def body(qk_ref, v_ref, g_ref, dh_ref, cst_ref, nst_ref, mst_ref, pre_ref, x_ref, xh_ref, cw_ref,
             dmqk_ref, dv_ref, dgt_ref, dcw_ref, dcb_ref, dC_ref, dn_ref, next_sc):
        @pl.when(pl.program_id(0) == 0)
        def _():
            for r in (dC_ref, dn_ref, next_sc, dcw_ref, dcb_ref):
                r[...] = jnp.zeros_like(r)

        B = G * HEADS
        gates = g_ref[...]
        g_all, x_all, lane, rowmod = _ml_gates(gates)
        mprev_rows = [mst_ref[g] for g in range(G)]
        wo_rows, ws_parts = [], []
        for g in range(G):
            _, wo, ws = _ml_chunk_rows(g_all, x_all, mprev_rows[g], g)
            wo_rows.append(wo)
            ws_parts.append(ws)
        ws3 = _lanes_to_batch_cols(jnp.concatenate(ws_parts, axis=0), lane)
        wo3 = _row_scalars(wo_rows)
        wo4 = wo3.reshape(G, HEADS, 1, 1)
        q3 = _heads_to_batch(qk_ref[:, :256] * ML_SCALE, ML_DQK)
        k3 = _heads_to_batch(qk_ref[:, 256:], ML_DQK)
        v3 = _heads_to_batch(v_ref[...], 128)
        dh3 = _heads_to_batch(dh_ref[...], 128)
        C3 = cst_ref[...].reshape(B, ML_DQK, 128)
        n3 = nst_ref[...].reshape(B, 1, ML_DQK)
        r = _ml_batched(q3, k3, v3, g_all, x_all, lane, C3, n3, _row_scalars(mprev_rows))
        wn, s3 = r["wn"], r["s"]
        inv = 1.0 / r["nrm"]
        dnum = dh3 * inv
        dnrm = -jnp.sum(dh3 * (r["num"] * inv), axis=2, keepdims=True) * inv
        dden = jnp.where(jnp.abs(r["den"]) > r["floor"], dnrm * jnp.sign(r["den"]), 0.0)
        ds = _bdot(dnum, v3, BNT) + dden
        dqk = ds * r["wi"]
        dd = ds * s3
        qw = q3 * wn
        dcloc = _bdot(qw, dnum, BTN).reshape(G, HEADS, ML_DQK, 128)
        dnloc = jnp.sum(qw * dden, axis=1, keepdims=True).reshape(G, HEADS, 1, ML_DQK)
        dC, dn = dC_ref[...], dn_ref[...]
        dcs, dns = [None] * G, [None] * G
        for g in reversed(range(G)):
            dcs[g], dns[g] = dC, dn
            dC = wo4[g] * dC + dcloc[g]
            dn = wo4[g] * dn + dnloc[g]
        dC_ref[...] = dC
        dn_ref[...] = dn
        dC3 = jnp.stack(dcs, axis=0).reshape(B, ML_DQK, 128)
        dn3 = jnp.stack(dns, axis=0).reshape(B, 1, ML_DQK)
        dk_st = ws3 * (_bdot(v3, dC3, BNT) + dn3)
        dq = _bdot(dqk, k3, BNN) + wn * (_bdot(dnum, C3, BNT) + dden * n3)
        dk = _bdot(dqk, q3, BTN) + dk_st
        dv = _bdot(s3, dnum, BTN) + ws3 * _bdot(k3, dC3, BNN)
        dv_ref[...] = _batch_to_heads(dv).astype(dv_ref.dtype)
        dqk = jnp.concatenate([_batch_to_heads(dq * ML_SCALE), _batch_to_heads(dk)], axis=1)
        pre = pre_ref[...]
        dpre = dqk * _dsilu(pre, _sig(pre))
        x = x_ref[...]
        xprev = jnp.where(pl.program_id(0) < NG - 1, xh_ref[...], 0.0)
        nxt = next_sc[...]
        row8 = _iota((8, HALF), 0)
        dx = dpre * cw_ref[3:4, :]
        dws = [None, None, None, _colsum(dpre * x)]
        for j in (1, 2, 3):
            dx = dx + _shift_rows_up(dpre, nxt, j, row8) * cw_ref[3 - j:4 - j, :]
            dws[3 - j] = _colsum(dpre * _shift_rows(x, xprev, j, row8))
        dmqk_ref[...] = dx.astype(dmqk_ref.dtype)
        dcw_ref[...] += jnp.concatenate(dws, axis=0)
        dcb_ref[...] += _colsum(dpre)
        next_sc[...] = dpre[:8]
        e_col = wn * (jnp.sum(dnum * r["qc"], axis=2, keepdims=True) + dden * r["qn"])
        c_col = jnp.sum(k3 * dk_st, axis=2, keepdims=True)
        z = wo3 * (jnp.sum(dC3 * C3, axis=(1, 2), keepdims=True) + jnp.sum(dn3 * n3, axis=(1, 2), keepdims=True))
        dd_hi = dd.astype(_MXU).astype(F32)
        ones = jnp.ones((B, CHUNK, 128), F32)
        dd_cols = (_bdot(dd_hi, ones, BTN) + _bdot(dd - dd_hi, ones, BTN))[:, :, 0:1]
        last = _iota((1, CHUNK, 1), 1) == CHUNK - 1
        dg3 = jnp.sum(dd, axis=2, keepdims=True) - dd_cols + e_col - c_col
        dg3 = dg3 + jnp.where(last, jnp.sum(c_col, axis=1, keepdims=True) + z, 0.0)
        di3 = dd_cols + c_col

        def to_lanes(x3, first):
            x4 = x3.reshape(G, HEADS, CHUNK, 1)
            return sum(jnp.where(lane == first + h, x4[:, h].reshape(R, 1), 0.0) for h in range(HEADS))

        dlf = _chunk_cumsum(to_lanes(dg3, 4), rowmod, reverse=True)
        dgt_ref[...] = (to_lanes(di3, 0) + dlf * _sig(-gates)).astype(dgt_ref.dtype)

    rev = lambda w, c: pl.BlockSpec((R, w), lambda i, c=c: (NG - 1 - i, c))
    st = lambda *s: pl.BlockSpec((G,) + s, lambda i: (NG - 1 - i,) + (0,) * len(s))
    halo = pl.BlockSpec((8, HALF), lambda i: (jnp.maximum((NG - 1 - i) * (R // 8) - 1, 0), C_MQK // 4))
    return _hosted_call(
        body, comms, name="mlstm_bwd", grid=(NG,),
        in_specs=[rev(HALF, 0), rev(HALF, C_MV // 4), rev(128, C_GATES), rev(HALF, 0),
                  st(HEADS, ML_DQK, 128), st(HEADS, 1, ML_DQK), st(1, 128),
                  rev(HALF, 0), rev(HALF, C_MQK // 4), halo, _full(cw.shape)],
        out_specs=[rev(HALF, 0), rev(HALF, 0), rev(128, 0), _full((4, HALF)), _full((1, HALF))],
        out_shape=[_sds((T, HALF), _MXU), _sds((T, HALF), _MXU), _sds((T, 128), _MXU), _sds((4, HALF)), _sds((1, HALF))],
        scratch_shapes=[pltpu.VMEM((HEADS, ML_DQK, 128), F32), pltpu.VMEM((HEADS, 1, ML_DQK), F32),
                        pltpu.VMEM((8, HALF), F32)],
        args=(qkc, u, u, dh, cst, nst, mst, pre, u, u, cw))


def _head_norm(o):
    rs_parts, r_parts = [], []
    for h in range(HEADS):
        oh = o[:, 128 * h:128 * (h + 1)]
        rs = lax.rsqrt(jnp.mean(oh * oh, axis=-1, keepdims=True) + RMS_EPS)
        rs_parts.append(rs)
        r_parts.append(oh * rs)
    return jnp.concatenate(r_parts, axis=1), rs_parts


def _out_proj(x, u, o_hg, h_ml, g_hg, g_ml, w_out, tm):
    T = x.shape[0]

    def body(x_ref, hgate_ref, mo_ref, ohg_ref, hml_ref, ghg_ref, gml_ref, w_ref, m_ref, z_ref):
        hgate = hgate_ref[...]
        a = _head_norm(ohg_ref[...])[0] * ghg_ref[...] * (hgate * _sig(hgate))
        b = _head_norm(hml_ref[...])[0] * gml_ref[...] * _sig(mo_ref[...])
        m = jnp.concatenate([a, b], axis=1)
        m_ref[...] = m.astype(m_ref.dtype)
        z_ref[...] = ALPHA * x_ref[...] + _dot(m, w_ref[...])

    return pl.pallas_call(
        body, name="out_proj", grid=(T // tm,),
        in_specs=[_row(tm, D_MODEL), _row(tm, HALF, C_HGATE // 4), _row(tm, HALF, C_MO // 4),
                  _row(tm, HALF), _row(tm, HALF), _full(g_hg.shape), _full(g_ml.shape), _full(w_out.shape)],
        out_specs=[_row(tm, D_MODEL)] * 2,
        out_shape=[_sds((T, D_MODEL), _MXU), _sds((T, D_MODEL))],
        compiler_params=_cp("parallel"))(x, u, u, o_hg, h_ml, g_hg, g_ml, w_out)


def _ffn_ln(z1, ln1_g, ln1_b, wg, wu, wd, ln_g, ln_b, tm):
    T = z1.shape[0]

    def body(z1_ref, g1_ref, b1_ref, wg_ref, wu_ref, wd_ref, g_ref, b_ref, z_ref, x1_ref, x2_ref, a_ref, bb_ref, h_ref):
        x = _ln_fwd(z1_ref[...], g1_ref[...], b1_ref[...])[0]
        x1_ref[...] = x.astype(x1_ref.dtype)
        a = _dot(x, wg_ref[...], NT)
        bb = _dot(x, wu_ref[...], NT)
        hh = a * _sig(a) * bb
        a_ref[...] = a.astype(a_ref.dtype)
        bb_ref[...] = bb.astype(bb_ref.dtype)
        h_ref[...] = hh.astype(h_ref.dtype)
        z = ALPHA * x + _dot(hh, wd_ref[...])
        z_ref[...] = z
        x2_ref[...] = _ln_fwd(z, g_ref[...], b_ref[...])[0].astype(x2_ref.dtype)

    vec = _full((1, D_MODEL))
    return pl.pallas_call(
        body, name="ffn_ln2", grid=(T // tm,),
        in_specs=[_row(tm, D_MODEL), vec, vec, _full(wg.shape), _full(wu.shape), _full(wd.shape), vec, vec],
        out_specs=[_row(tm, D_MODEL)] * 3 + [_row(tm, D_FF)] * 3,
        out_shape=[_sds((T, D_MODEL))] + [_sds((T, D_MODEL), _MXU)] * 2 + [_sds((T, D_FF), _MXU)] * 3,
        compiler_params=_cp("parallel"))(z1, ln1_g, ln1_b, wg, wu, wd, ln_g, ln_b)


def _ple_loss_ln2_bwd(z2, p, tgt, wpg, bpg, wpp, ln_g, ln_b, tm):
    T = z2.shape[0]

    def body(z_ref, p_ref, t_ref, wpg_ref, bpg_ref, wpp_ref, g_ref, b_ref,
             de_ref, dgp_ref, dz_ref, loss_ref, dbpg_ref, dg_ref, db_ref):
        @pl.when(pl.program_id(0) == 0)
        def _():
            for r in (loss_ref, dbpg_ref, dg_ref, db_ref):
                r[...] = jnp.zeros_like(r)

        x2, xhat, rstd = _ln_fwd(z_ref[...], g_ref[...], b_ref[...])
        gate = _sig(_dot(x2, wpg_ref[...]) + bpg_ref[...])
        e = _dot(p_ref[...], wpp_ref[...])
        err = x2 + gate * e - t_ref[...]
        loss_ref[...] += _colsum(err * err)
        dy = err * (1.0 / D_MODEL)
        de_ref[...] = (dy * gate).astype(de_ref.dtype)
        dgp = dy * e * gate * (1.0 - gate)
        dgp_ref[...] = dgp.astype(dgp_ref.dtype)
        dbpg_ref[...] += _colsum(dgp)
        dx2 = dy + _dot(dgp, wpg_ref[...], NT)
        dg_ref[...] += _colsum(dx2 * xhat)
        db_ref[...] += _colsum(dx2)
        dz_ref[...] = _ln_bwd(dx2, xhat, rstd, g_ref[...])

    vec = _full((1, D_MODEL))
    return pl.pallas_call(
        body, name="ple_loss_ln2_bwd", grid=(T // tm,),
        in_specs=[_row(tm, D_MODEL), _row(tm, PLE_DIM), _row(tm, D_MODEL),
                  _full(wpg.shape), vec, _full(wpp.shape), vec, vec],
        out_specs=[_row(tm, D_MODEL)] * 3 + [vec] * 4,
        out_shape=[_sds((T, D_MODEL), _MXU)] * 2 + [_sds((T, D_MODEL))] + [_sds((1, D_MODEL))] * 4,
        compiler_params=_cp("arbitrary"))(z2, p, tgt, wpg, bpg, wpp, ln_g, ln_b)


def _ffn_bwd_ln1_bwd(a_pre, b_pre, z1, dz2, wg, wu, wd, ln_g, ln_b, tm, comms=None):
    T = z1.shape[0]

    def body(a_ref, bb_ref, z_ref, dz2_ref, wg_ref, wu_ref, wd_ref, g_ref, b_ref,
             da_ref, dbb_ref, dz1_ref, dg_ref, db_ref):
        @pl.when(pl.program_id(0) == 0)
        def _():
            dg_ref[...] = jnp.zeros_like(dg_ref)
            db_ref[...] = jnp.zeros_like(db_ref)

        dz2 = dz2_ref[...]
        a = a_ref[...].astype(F32)
        bb = bb_ref[...].astype(F32)
        sa = _sig(a)
        act = a * sa
        dh = _dot(dz2, wd_ref[...], NT)
        da = (dh * bb * _dsilu(a, sa)).astype(da_ref.dtype)
        dbb = (dh * act).astype(dbb_ref.dtype)
        da_ref[...] = da
        dbb_ref[...] = dbb
        dx1 = ALPHA * dz2 + _dot(da, wg_ref[...]) + _dot(dbb, wu_ref[...])
        _, xhat, rstd = _ln_fwd(z_ref[...], g_ref[...], b_ref[...])
        dg_ref[...] += _colsum(dx1 * xhat)
        db_ref[...] += _colsum(dx1)
        dz1_ref[...] = _ln_bwd(dx1, xhat, rstd, g_ref[...])

    vec = _full((1, D_MODEL))
    return _hosted_call(
        body, comms, name="ffn_bwd_ln1_bwd", grid=(T // tm,),
        in_specs=[_row(tm, D_FF)] * 2 + [_row(tm, D_MODEL)] * 2 + [_full(wg.shape), _full(wu.shape), _full(wd.shape), vec, vec],
        out_specs=[_row(tm, D_FF)] * 2 + [_row(tm, D_MODEL), vec, vec],
        out_shape=[_sds((T, D_FF), _MXU)] * 2 + [_sds((T, D_MODEL)), _sds((1, D_MODEL)), _sds((1, D_MODEL))],
        scratch_shapes=[], args=(a_pre, b_pre, z1, dz2, wg, wu, wd, ln_g, ln_b))


def _out_proj_bwd(dz1, u, o_hg, h_ml, g_hg, g_ml, w_out, tm):
    T = dz1.shape[0]

    def body(dz_ref, hgate_ref, mo_ref, ohg_ref, hml_ref, ghg_ref, gml_ref, w_ref,
             dohg_ref, dhml_ref, dhgate_ref, dmo_ref, dghg_ref, dgml_ref):
        @pl.when(pl.program_id(0) == 0)
        def _():
            dghg_ref[...] = jnp.zeros_like(dghg_ref)
            dgml_ref[...] = jnp.zeros_like(dgml_ref)

        dm = _dot(dz_ref[...], w_ref[...], NT)

        def half(dmh, o, gvec, gate_val, dgate_fac, do_ref, dgate_ref, dgvec_ref):
            r, rs = _head_norm(o)
            dgate_ref[...] = (dmh * r * gvec * dgate_fac).astype(dgate_ref.dtype)
            dn = dmh * gate_val
            dgvec_ref[...] += _colsum(dn * r)
            dr = dn * gvec
            parts = []
            for h in range(HEADS):
                sl = slice(128 * h, 128 * (h + 1))
                parts.append(rs[h] * (dr[:, sl] - r[:, sl] * jnp.mean(dr[:, sl] * r[:, sl], axis=-1, keepdims=True)))
            do_ref[...] = jnp.concatenate(parts, axis=1)

        hg = hgate_ref[...]
        shg = _sig(hg)
        half(dm[:, :HALF], ohg_ref[...], ghg_ref[...], hg * shg, _dsilu(hg, shg), dohg_ref, dhgate_ref, dghg_ref)
        smo = _sig(mo_ref[...])
        half(dm[:, HALF:], hml_ref[...], gml_ref[...], smo, smo * (1.0 - smo), dhml_ref, dmo_ref, dgml_ref)

    vec = _full((1, HALF))
    return pl.pallas_call(
        body, name="out_proj_bwd", grid=(T // tm,),
        in_specs=[_row(tm, D_MODEL), _row(tm, HALF, C_HGATE // 4), _row(tm, HALF, C_MO // 4),
                  _row(tm, HALF), _row(tm, HALF), vec, vec, _full(w_out.shape)],
        out_specs=[_row(tm, HALF)] * 4 + [vec, vec],
        out_shape=[_sds((T, HALF))] * 2 + [_sds((T, HALF), _MXU)] * 2 + [_sds((1, HALF))] * 2,
        compiler_params=_cp("arbitrary"))(dz1, u, u, o_hg, h_ml, g_hg, g_ml, w_out)


def _du_specs(rows):
    return [pl.BlockSpec((rows, w), lambda i: (i, 0)) for w in DU_WIDTHS]


def _in_proj_bwd(dz1, du_parts, w, tm, comms=None):
    T = dz1.shape[0]

    def body(dz_ref, *refs):
        du = jnp.concatenate([r[...] for r in refs[:8]], axis=1)
        refs[9][...] = ALPHA * dz_ref[...] + _dot(du, refs[8][...], NT)

    (dx,), got = _hosted_call(
        body, comms, name="in_proj_bwd", grid=(T // tm,),
        in_specs=[_row(tm, D_MODEL)] + _du_specs(tm) + [_full(w.shape)],
        out_specs=[_row(tm, D_MODEL)], out_shape=[_sds((T, D_MODEL))], scratch_shapes=[], args=(dz1, *du_parts, w))
    return dx, got


def _wgrad(a, b, name, tm, tn, tk):
    T, M = a.shape
    N = b.shape[1]
    tm, tn, tk = min(tm, M), min(tn, N), min(tk, T)
    nk = T // tk

    def body(a_ref, b_ref, o_ref, acc_ref):
        kk = pl.program_id(2)

        @pl.when(kk == 0)
        def _():
            acc_ref[...] = jnp.zeros_like(acc_ref)

        acc_ref[...] += _dot(a_ref[...], b_ref[...], TN)

        @pl.when(kk == nk - 1)
        def _():
            o_ref[...] = acc_ref[...].astype(o_ref.dtype)

    return pl.pallas_call(
        body, name=name, grid=(M // tm, N // tn, nk),
        in_specs=[pl.BlockSpec((tk, tm), lambda i, j, kk: (kk, i)), pl.BlockSpec((tk, tn), lambda i, j, kk: (kk, j))],
        out_specs=pl.BlockSpec((tm, tn), lambda i, j, kk: (i, j)), out_shape=_sds((M, N), _MXU),
        scratch_shapes=[pltpu.VMEM((tm, tn), F32)],
        compiler_params=_cp("parallel", "parallel", "arbitrary"))(a, b)


W_IN_PARTS = 2


def _wgrad_w_in(x, du_parts, tk, part, comms=None):
    T = x.shape[0]
    M = D_MODEL // W_IN_PARTS
    tk = min(tk, T)
    nk = T // tk

    def body(a_ref, *refs):
        o_ref, cs_ref, acc_ref = refs[8:]
        kk = pl.program_id(0)

        @pl.when(kk == 0)
        def _():
            acc_ref[...] = jnp.zeros_like(acc_ref)
            cs_ref[...] = jnp.zeros_like(cs_ref)

        du = jnp.concatenate([r[...] for r in refs[:8]], axis=1)
        acc_ref[...] += _dot(a_ref[...], du, TN)
        cs_ref[...] += _colsum(du.astype(F32))

        @pl.when(kk == nk - 1)
        def _():
            o_ref[...] = acc_ref[...].astype(o_ref.dtype)

    return _hosted_call(
        body, comms, name="wgrad_w_in_%d" % part, grid=(nk,),
        in_specs=[pl.BlockSpec((tk, M), lambda kk: (kk, part))] + _du_specs(tk),
        out_specs=[_full((M, PROJ_WP)), _full((1, PROJ_WP))],
        out_shape=[_sds((M, PROJ_WP), _MXU), _sds((1, PROJ_WP))],
        scratch_shapes=[pltpu.VMEM((M, PROJ_WP), F32)], args=(x, *du_parts))


W_IN_S, FF_S, OUT_S, PP_S = PROJ_W // N_DEV, D_FF // N_DEV, D_MODEL // N_DEV, D_MODEL // N_DEV
LATE = ("w_ffn_gate", "w_ffn_up", "w_out", "w_ffn_down", "ple_w_gate", "ple_w_proj")
BIG = ("w_in",) + LATE
TRANSPOSED = ("w_ffn_gate", "w_ffn_up")


def _split_cols(a, n):
    return a.reshape(a.shape[0], N_DEV, n).transpose(1, 0, 2)


def _join_cols(a):
    return a.transpose(1, 0, 2).reshape(a.shape[1], -1)


def _step(x, p, tgt, w_in, b_in, lb_logits, conv_w, conv_b, g_hg, g_ml, ln1_g, ln1_b, ln2_g, ln2_b, bpg, late,
          distributed):
    T = x.shape[0]
    tm, tf = min(ROWS, T), min(ROWS_FFN, T)
    gather = lambda *names: [_GatherTwoLevel([late[n] for n in names])] if distributed else None
    scatter = lambda *arrs: [_Comm("scatter", list(arrs))] if distributed else None
    rows = lambda a, n: a.reshape(N_DEV, n, D_MODEL)
    (u, pre, qkc), got1 = _in_proj(x, w_in, b_in, conv_w, conv_b, tm, gather("w_out", "ple_w_gate", "ple_w_proj"))
    (o_hg, hg_states), got2 = _hgrn2_fwd(u, lb_logits, gather("w_ffn_gate", "w_ffn_up"))
    (h_ml, cst, nst, mst), got3 = _mlstm_fwd(qkc, u, gather("w_ffn_down"))
    if distributed:
        w_out, wpg, wpp = got1[0][0].reshape(D_MODEL, D_MODEL), got1[0][1].reshape(D_MODEL, D_MODEL), _join_cols(got1[0][2])
        wg, wu, wd = (a.reshape(D_FF, D_MODEL) for a in (got2[0][0], got2[0][1], got3[0][0]))
    else:
        w_out, wg, wu, wd, wpg, wpp = (late[n] for n in ("w_out", "w_ffn_gate", "w_ffn_up", "w_ffn_down", "ple_w_gate", "ple_w_proj"))
    m_in, z1 = _out_proj(x, u, o_hg, h_ml, g_hg, g_ml, w_out, tm)
    z2, x1, x2, a_pre, b_pre, hh = _ffn_ln(z1, ln1_g, ln1_b, wg, wu, wd, ln2_g, ln2_b, tf)
    de, dgp, dz2, loss_vec, d_bpg, d_ln2g, d_ln2b = _ple_loss_ln2_bwd(z2, p, tgt, wpg, bpg, wpp, ln2_g, ln2_b, tm)
    big = dict(ple_w_gate=_wgrad(x2, dgp, "wgrad_ple_gate", D_MODEL, D_MODEL, 2048),
               ple_w_proj=_wgrad(p, de, "wgrad_ple_proj", 512, D_MODEL, 2048))
    (da, dbb, dz1, d_ln1g, d_ln1b), r1 = _ffn_bwd_ln1_bwd(
        a_pre, b_pre, z1, dz2, wg, wu, wd, ln1_g, ln1_b, tf, scatter(rows(big["ple_w_gate"], OUT_S), _split_cols(big["ple_w_proj"], PP_S)))
    big.update(
        w_ffn_gate=_wgrad(da, x1, "wgrad_ffn_gate", D_FF, D_MODEL, 1024),
        w_ffn_up=_wgrad(dbb, x1, "wgrad_ffn_up", D_FF, D_MODEL, 1024),
        w_ffn_down=_wgrad(hh, dz2, "wgrad_ffn_down", D_FF, D_MODEL, 1024),
        w_out=_wgrad(m_in, dz1, "wgrad_w_out", D_MODEL, D_MODEL, 2048))
    d_ohg, d_hml, d_hgate, d_mo, d_ghg, d_gml = _out_proj_bwd(dz1, u, o_hg, h_ml, g_hg, g_ml, w_out, tm)
    (d_hq, d_hf, d_hv, d_lb), r2 = _hgrn2_bwd(
        u, lb_logits, d_ohg, hg_states,
        scatter(rows(big["w_ffn_gate"], FF_S), rows(big["w_ffn_up"], FF_S)))
    (d_mqk, d_mv, d_gates, d_convw, d_convb), r2b = _mlstm_bwd(
        qkc, u, d_hml, cst, nst, mst, pre, conv_w, scatter(rows(big["w_ffn_down"], FF_S), rows(big["w_out"], OUT_S)))
    du_parts = [d_hq, d_hf, d_hv, d_hgate, d_mqk, d_mv, d_mo, d_gates]
    own = lambda g: _split_cols(g[:, :PROJ_W], W_IN_S)
    (g_in0, d_bin), _ = _wgrad_w_in(x, du_parts, 512, 0)
    (g_in1, _), r_in0 = _wgrad_w_in(x, du_parts, 512, 1, scatter(own(g_in0)))
    big["w_in"] = jnp.concatenate([g_in0, g_in1], axis=0)
    small = dict(b_in=d_bin[:, :PROJ_W], hg_lb_logits=d_lb, ml_conv_w=d_convw, ml_conv_b=d_convb, hg_norm_g=d_ghg,
                 ml_norm_g=d_gml, ln1_g=d_ln1g, ln1_b=d_ln1b, ln2_g=d_ln2g, ln2_b=d_ln2b, ple_b_gate=d_bpg)
    last = [_Comm("scatter", [own(g_in1)]), _Comm("gather", [loss_vec] + [small[n] for n in SMALL])] if distributed else None
    dx, r3 = _in_proj_bwd(dz1, du_parts, w_in, tm, last)
    gathered_small = None
    if distributed:
        big = dict(ple_w_gate=r1[0][0], ple_w_proj=r1[0][1], w_ffn_gate=r2[0][0], w_ffn_up=r2[0][1],
                   w_ffn_down=r2b[0][0], w_out=r2b[0][1], w_in=[r_in0[0][0], r3[0][0]])
        gathered_small = r3[1]
    return loss_vec, dx, big, small, gathered_small


SMALL = ("b_in", "hg_lb_logits", "ml_conv_w", "ml_conv_b", "hg_norm_g", "ml_norm_g", "ln1_g", "ln1_b", "ln2_g", "ln2_b",
         "ple_b_gate")


def _padc(a, n):
    return jnp.pad(a, [(0, 0)] * (a.ndim - 1) + [(0, n - a.shape[-1])])


def _adamw(w, g, m, v):
    m = B1 * m + (1.0 - B1) * g
    v = B2 * v + (1.0 - B2) * jnp.square(g)
    m_hat = m / (1.0 - B1 ** STEP)
    v_hat = v / (1.0 - B2 ** STEP)
    return -LR * (m_hat / (jnp.sqrt(v_hat) + EPS) + WD * w), m, v


def _sum_slabs(ref):
    g = ref[0].astype(F32)
    for j in range(1, N_DEV):
        g = g + ref[j].astype(F32)
    return g


def _adamw_matrix(rbs, w, m, v, name):
    rbs = list(rbs) if isinstance(rbs, (list, tuple)) else [rbs]
    nb = len(rbs)
    R, C = w.shape
    rows = R // nb
    tr = 256 if rows % 256 == 0 else rows // 2 if rows % 32 == 0 else rows
    per = rows // tr

    def body(*refs):
        w_ref, m_ref, v_ref, g_ref, d_ref, m2_ref, v2_ref = refs[nb:]
        i = pl.program_id(0)
        g = _sum_slabs(refs[0])
        for k in range(1, nb):
            g = jnp.where(i >= k * per, _sum_slabs(refs[k]), g)
        g_ref[...] = g
        d_ref[...], m2_ref[...], v2_ref[...] = _adamw(w_ref[...], g, m_ref[...], v_ref[...])

    blk = pl.BlockSpec((tr, C), lambda i: (i, 0))
    part = lambda k: pl.BlockSpec((N_DEV, tr, C), lambda i, k=k: (0, jnp.clip(i - k * per, 0, per - 1), 0))
    return pl.pallas_call(
        body, name=name, grid=(R // tr,),
        in_specs=[part(k) for k in range(nb)] + [blk, blk, blk],
        out_specs=[blk] * 4, out_shape=[_sds((R, C))] * 4, compiler_params=_cp("parallel"))(*rbs, w, m, v)


def _adamw_small(loss_g, gs, ws, ms, vs):
    n = len(ws)

    def body(*refs):
        loss_ref, g_refs, w_refs, m_refs, v_refs = refs[0], refs[1:1 + n], refs[1 + n:1 + 2 * n], refs[1 + 2 * n:1 + 3 * n], refs[1 + 3 * n:1 + 4 * n]
        outs = refs[1 + 4 * n:]
        outs[0][...] = (0.5 / D_MODEL) * jnp.sum(_sum_slabs(loss_ref), keepdims=True)
        for i in range(n):
            g = _sum_slabs(g_refs[i])
            outs[1 + i][...] = g
            outs[1 + n + i][...], outs[1 + 2 * n + i][...], outs[1 + 3 * n + i][...] = _adamw(
                w_refs[i][...], g, m_refs[i][...], v_refs[i][...])

    res = pl.pallas_call(
        body, name="adamw_small", out_shape=[_sds((1, 1))] + [_sds(w.shape) for w in ws] * 4)(loss_g, *gs, *ws, *ms, *vs)
    return res[0], [res[1 + k * n:1 + (k + 1) * n] for k in range(4)]


WEIGHTS = ("w_in", "b_in", "hg_lb_logits", "ml_conv_w", "ml_conv_b", "hg_norm_g", "ml_norm_g", "w_out", "ln1_g", "ln1_b",
           "w_ffn_gate", "w_ffn_up", "w_ffn_down", "ln2_g", "ln2_b", "ple_w_proj", "ple_w_gate", "ple_b_gate")
CONV_S = HALF // N_DEV


def kernel(x, p, w_in, b_in, hg_lb_logits, ml_conv_w, ml_conv_b, hg_norm_g, ml_norm_g, w_out, ln1_g, ln1_b, w_ffn_gate, w_ffn_up, w_ffn_down, ln2_g, ln2_b, ple_w_proj, ple_w_gate, ple_b_gate, loss_target, m_w_in, m_b_in, m_hg_lb_logits, m_ml_conv_w, m_ml_conv_b, m_hg_norm_g, m_ml_norm_g, m_w_out, m_ln1_g, m_ln1_b, m_w_ffn_gate, m_w_ffn_up, m_w_ffn_down, m_ln2_g, m_ln2_b, m_ple_w_proj, m_ple_w_gate, m_ple_b_gate, v_w_in, v_b_in, v_hg_lb_logits, v_ml_conv_w, v_ml_conv_b, v_hg_norm_g, v_ml_norm_g, v_w_out, v_ln1_g, v_ln1_b, v_w_ffn_gate, v_w_ffn_up, v_w_ffn_down, v_ln2_g, v_ln2_b, v_ple_w_proj, v_ple_w_gate, v_ple_b_gate):
    args = locals()
    me = 4 * lax.axis_index("x") + 2 * lax.axis_index("y") + lax.axis_index("c")
    shapes = {n: args[n].shape for n in WEIGHTS}
    def drop(n, a):
        a = a[0] if n in BIG or n == "ml_conv_w" else a
        return a.T if n in TRANSPOSED else a

    W = {n: drop(n, args[n]) for n in WEIGHTS}
    M = {n: drop(n, args["m_" + n]) for n in WEIGHTS}
    V = {n: drop(n, args["v_" + n]) for n in WEIGHTS}

    g_in, g_conv = _gather_two_level(
        [W["w_in"].astype(_MXU), jnp.pad(W["ml_conv_w"], ((0, 4), (0, 128 - CONV_S)))], "gather_w_in")
    w_in_full = _padc(_join_cols(g_in), PROJ_WP)
    conv_full = _join_cols(g_conv[:, :4, :CONV_S])

    _, dx, big, _, sg = _step(
        x[0], p[0, 0], loss_target[0], w_in_full, _padc(b_in, PROJ_WP), hg_lb_logits, conv_full, ml_conv_b,
        hg_norm_g, ml_norm_g, ln1_g, ln1_b, ln2_g, ln2_b, ple_b_gate, {n: W[n].astype(_MXU) for n in LATE}, True)

    upd = {n: _adamw_matrix(big[n], W[n], M[n], V[n], "adamw_" + n) for n in BIG}
    sg = dict(zip(SMALL, sg[1:]), loss=sg[0])
    sg["ml_conv_w"] = lax.dynamic_slice(sg["ml_conv_w"], (0, 0, me * CONV_S), (N_DEV, 4, CONV_S))
    loss, small_upd = _adamw_small(sg["loss"], *[[d[n] for n in SMALL] for d in (sg, W, M, V)])

    outs = []
    for kind in range(4):
        smalls = dict(zip(SMALL, small_upd[kind]))
        for n in WEIGHTS:
            o = upd[n][kind] if n in BIG else smalls[n]
            outs.append((o.T if n in TRANSPOSED else o).reshape(shapes[n]))
    return (loss.reshape(()), dx.reshape(x.shape), *outs)
```

```python
import jax
import jax.numpy as jnp
from jax import lax
from jax.experimental import pallas as pl
from jax.experimental.pallas import tpu as pltpu

F32 = jnp.float32
_MXU = jnp.bfloat16

D_MODEL = 1024
CHUNK = 64
SUB = 16
PLE_DIM = 256
HEADS = 4
ML_DQK = 64
HALF = 512
D_FF = 2816
PROJ_W = 3592
PROJ_WP = 3712
ALPHA = float(2 ** 0.25)
LN_EPS = 1e-5
RMS_EPS = 1e-6
ML_SCALE = ML_DQK ** -0.5
N_DEV = 8
LR, B1, B2, EPS, WD, STEP = 0.001, 0.9, 0.999, 1e-08, 0.01, 10
NEG = -1e30
LOG2E = 1.4426950408889634

C_HQ, C_HF, C_HV, C_HGATE, C_MQK, C_MV, C_MO, C_GATES = 0, 4, 8, 12, 16, 20, 24, 28
DU_WIDTHS = (HALF,) * 7 + (128,)

VMEM_LIMIT = 52 * 1024 * 1024
GC = 8
ROWS = 512
ROWS_FFN = 256

NN = (((1,), (0,)), ((), ()))
NT = (((1,), (1,)), ((), ()))
TN = (((0,), (0,)), ((), ()))
BNT = (((2,), (2,)), ((0,), (0,)))
BNN = (((2,), (1,)), ((0,), (0,)))
BTN = (((1,), (1,)), ((0,), (0,)))


def _dot(a, b, dims=NN):
    return lax.dot_general(a.astype(_MXU), b.astype(_MXU), dims, preferred_element_type=F32)


def _sig(x):
    return jax.nn.sigmoid(x)


def _cp(*sem):
    return pltpu.CompilerParams(dimension_semantics=sem, vmem_limit_bytes=VMEM_LIMIT)


def _row(tm, c, blk=0):
    return pl.BlockSpec((tm, c), lambda i, blk=blk: (i, blk))


def _full(shape):
    nd = len(shape)
    return pl.BlockSpec(tuple(shape), lambda *_, nd=nd: (0,) * nd)


def _sds(shape, dtype=F32):
    return jax.ShapeDtypeStruct(tuple(shape), dtype)


def _iota(shape, axis):
    return lax.broadcasted_iota(jnp.int32, shape, axis)


def _colsum(x):
    return jnp.sum(x, axis=0, keepdims=True)


def _rowsum(x):
    return jnp.sum(x, axis=1, keepdims=True)


def _ln_fwd(z, g, b):
    mu = jnp.mean(z, axis=-1, keepdims=True)
    zc = z - mu
    var = jnp.mean(zc * zc, axis=-1, keepdims=True)
    rstd = lax.rsqrt(var + LN_EPS)
    xhat = zc * rstd
    return xhat * g + b, xhat, rstd


def _ln_bwd(dy, xhat, rstd, g):
    dxh = dy * g
    m1 = jnp.mean(dxh, axis=-1, keepdims=True)
    m2 = jnp.mean(dxh * xhat, axis=-1, keepdims=True)
    return rstd * (dxh - m1 - xhat * m2)


def _dsilu(x, s):
    return s * (1.0 + x * (1.0 - s))


MESH = pl.DeviceIdType.MESH
ANY = pl.BlockSpec(memory_space=pl.ANY)


def _flip(v, bit):
    return 1 - v if bit else v


class _Comm:
    def __init__(self, kind, srcs):
        self.kind, self.srcs, self.n = kind, list(srcs), len(srcs)

    def out_shape(self):
        lead = (N_DEV,) if self.kind == "gather" else ()
        return [jax.ShapeDtypeStruct(lead + s.shape, s.dtype) for s in self.srcs]

    def scratch(self):
        return [pltpu.SemaphoreType.DMA((7 * self.n,)), pltpu.SemaphoreType.DMA((7 * self.n,)),
                pltpu.SemaphoreType.DMA((self.n,))]

    def copies(self, srcs, dsts, send_sems, recv_sems, local_sems):
        x, y, c = lax.axis_index("x"), lax.axis_index("y"), lax.axis_index("c")
        me = 4 * x + 2 * y + c
        pick = (lambda s, j: s) if self.kind == "gather" else (lambda s, j: s.at[j])
        out = []
        for i, (s, d) in enumerate(zip(srcs, dsts)):
            out.append(pltpu.make_async_copy(pick(s, me), d.at[me], local_sems.at[i]))
            for k in range(1, N_DEV):
                px, py, pc = _flip(x, k & 4), _flip(y, k & 2), _flip(c, k & 1)
                out.append(pltpu.make_async_remote_copy(
                    src_ref=pick(s, 4 * px + 2 * py + pc), dst_ref=d.at[me], send_sem=send_sems.at[7 * i + k - 1],
                    recv_sem=recv_sems.at[7 * i + k - 1], device_id=(px, py, pc), device_id_type=MESH))
        return out

    def start(self, *refs):
        for cp in self.copies(*refs):
            cp.start()

    def mid(self, *refs):
        pass

    def finish(self, *refs):
        for cp in self.copies(*refs):
            cp.wait()


class _GatherTwoLevel(_Comm):
    def __init__(self, srcs):
        super().__init__("gather", srcs)

    def _parts(self, srcs, dsts, send_sems, recv_sems, local_sems):
        x, y, c = lax.axis_index("x"), lax.axis_index("y"), lax.axis_index("c")
        me, sibling = (x, y, c), (x, y, 1 - c)
        chips = [(1 - x, y), (x, 1 - y), (1 - x, 1 - y)]

        def copy(i, k, block, to, own=False):
            slab = dsts[i].at[4 * block[0] + 2 * block[1] + block[2]]
            return pltpu.make_async_remote_copy(
                src_ref=srcs[i] if own else slab, dst_ref=slab, send_sem=send_sems.at[7 * i + k],
                recv_sem=recv_sems.at[7 * i + k], device_id=to, device_id_type=MESH)

        n = range(self.n)
        mine = [pltpu.make_async_copy(srcs[i], dsts[i].at[4 * x + 2 * y + c], local_sems.at[i]) for i in n]
        first = [copy(i, 0, me, sibling, own=True) for i in n]
        first += [copy(i, 1 + j, me, (*chip, c), own=True) for j, chip in enumerate(chips) for i in n]
        over_ici = [copy(i, 1 + j, (*chip, c), me) for j, chip in enumerate(chips) for i in n]
        passed = [copy(i, 4 + j, (*chip, c), sibling) for j, chip in enumerate(chips) for i in n]
        from_sibling = [copy(i, 0, sibling, me) for i in n]
        from_sibling += [copy(i, 4 + j, (*chip, 1 - c), me) for j, chip in enumerate(chips) for i in n]
        return mine, first, over_ici, passed, from_sibling

    def start(self, *refs):
        mine, first, _, _, _ = self._parts(*refs)
        for cp in mine + first:
            cp.start()

    def mid(self, *refs):
        _, _, over_ici, passed, _ = self._parts(*refs)
        for arrived, onward in zip(over_ici, passed):
            arrived.wait_recv()
            onward.start()

    def finish(self, *refs):
        mine, first, _, passed, from_sibling = self._parts(*refs)
        for cp in from_sibling:
            cp.wait_recv()
        for cp in first + passed:
            cp.wait_send()
        for cp in mine:
            cp.wait()


def _hosted_call(body, comms, *, name, grid, in_specs, out_specs, out_shape, scratch_shapes, args):
    comms = list(comms or [])
    if not comms:
        res = pl.pallas_call(body, name=name, grid=grid, in_specs=in_specs, out_specs=out_specs, out_shape=out_shape,
                             scratch_shapes=scratch_shapes, compiler_params=_cp("arbitrary"))(*args)
        return list(res), []
    n_in, n_out, n_sc, nc = len(in_specs), len(out_specs), len(scratch_shapes), sum(cm.n for cm in comms)
    last = grid[0] - 1

    def hosted(*refs):
        ins, csrc = refs[:n_in], refs[n_in:n_in + nc]
        o0 = n_in + nc
        outs, cdst = refs[o0:o0 + n_out], refs[o0 + n_out:o0 + n_out + nc]
        s0 = o0 + n_out + nc
        scr, sems = refs[s0:s0 + n_sc], refs[s0 + n_sc:]

        def phase(which):
            o = 0
            for j, cm in enumerate(comms):
                getattr(cm, which)(csrc[o:o + cm.n], cdst[o:o + cm.n], *sems[3 * j:3 * j + 3])
                o += cm.n

        i = pl.program_id(0)

        @pl.when(i == 0)
        def _():
            phase("start")

        body(*ins, *outs, *scr)

        @pl.when(i == (2 * last) // 3)
        def _():
            phase("mid")

        @pl.when(i == last)
        def _():
            phase("finish")

    res = pl.pallas_call(
        hosted, name=name, grid=grid, in_specs=list(in_specs) + [ANY] * nc, out_specs=list(out_specs) + [ANY] * nc,
        out_shape=list(out_shape) + [s for cm in comms for s in cm.out_shape()],
        scratch_shapes=list(scratch_shapes) + [s for cm in comms for s in cm.scratch()],
        compiler_params=_cp("arbitrary"))(*args, *[a for cm in comms for a in cm.srcs])
    got, o = [], n_out
    for cm in comms:
        got.append(list(res[o:o + cm.n]))
        o += cm.n
    return list(res[:n_out]), got


def _gather_two_level(blocks, name):
    n = len(blocks)

    def body(*refs):
        x_refs, out_refs = refs[:n], refs[n:2 * n]
        send_sems, recv_sems, local_sems = refs[2 * n:]
        x, y, c = lax.axis_index("x"), lax.axis_index("y"), lax.axis_index("c")
        me, sibling = (x, y, c), (x, y, 1 - c)
        chips = [(1 - x, y), (x, 1 - y), (1 - x, 1 - y)]

        def copy(i, k, block, to, own=False):
            slab = out_refs[i].at[4 * block[0] + 2 * block[1] + block[2]]
            return pltpu.make_async_remote_copy(
                src_ref=x_refs[i] if own else slab, dst_ref=slab, send_sem=send_sems.at[7 * i + k],
                recv_sem=recv_sems.at[7 * i + k], device_id=to, device_id_type=MESH)

        mine = [pltpu.make_async_copy(x_refs[i], out_refs[i].at[4 * x + 2 * y + c], local_sems.at[i]) for i in range(n)]
        for cp in mine:
            cp.start()
        first = [copy(i, 0, me, sibling, own=True) for i in range(n)]
        first += [copy(i, 1 + j, me, (*chip, c), own=True) for j, chip in enumerate(chips) for i in range(n)]
        for cp in first:
            cp.start()
        passed = []
        for j, chip in enumerate(chips):
            for i in range(n):
                copy(i, 1 + j, (*chip, c), me).wait_recv()
                passed.append(copy(i, 4 + j, (*chip, c), sibling))
                passed[-1].start()
        for i in range(n):
            copy(i, 0, sibling, me).wait_recv()
            for j, chip in enumerate(chips):
                copy(i, 4 + j, (*chip, 1 - c), me).wait_recv()
        for cp in first + passed:
            cp.wait_send()
        for cp in mine:
            cp.wait()

    return pl.pallas_call(
        body, name=name, out_shape=[jax.ShapeDtypeStruct((N_DEV,) + b.shape, b.dtype) for b in blocks],
        in_specs=[ANY] * n, out_specs=[ANY] * n,
        scratch_shapes=[pltpu.SemaphoreType.DMA((7 * n,)), pltpu.SemaphoreType.DMA((7 * n,)),
                        pltpu.SemaphoreType.DMA((n,))])(*blocks)


def _in_proj(x, w, b, cw, cb, tm, comms=None):
    T = x.shape[0]

    def body(x_ref, w_ref, b_ref, cw_ref, cb_ref, o_ref, pre_ref, act_ref, halo_sc):
        @pl.when(pl.program_id(0) == 0)
        def _():
            halo_sc[...] = jnp.zeros_like(halo_sc)

        o = _dot(x_ref[...], w_ref[...]) + b_ref[...]
        o_ref[...] = o
        xc = o[:, 128 * C_MQK:128 * C_MQK + HALF]
        halo = halo_sc[...]
        rowi = _iota((8, HALF), 0)
        acc = xc * cw_ref[3:4, :] + cb_ref[...]
        for j in (1, 2, 3):
            acc = acc + _shift_rows(xc, halo, j, rowi) * cw_ref[3 - j:4 - j, :]
        pre_ref[...] = acc
        act_ref[...] = acc * _sig(acc)
        halo_sc[...] = xc[tm - 8:]

    return _hosted_call(
        body, comms, name="in_proj", grid=(T // tm,),
        in_specs=[_row(tm, D_MODEL), _full(w.shape), _full(b.shape), _full(cw.shape), _full(cb.shape)],
        out_specs=[_row(tm, PROJ_WP), _row(tm, HALF), _row(tm, HALF)],
        out_shape=[_sds((T, PROJ_WP)), _sds((T, HALF)), _sds((T, HALF))],
        scratch_shapes=[pltpu.VMEM((8, HALF), F32)], args=(x, w, b, cw, cb))


def _shift_rows(x, halo, j, rowi):
    r = pltpu.roll(x, j, 0)
    top = jnp.where(rowi < j, pltpu.roll(halo, j, 0), r[:8])
    return jnp.concatenate([top, r[8:]], axis=0)


def _shift_rows_up(x, halo, j, rowi):
    n = x.shape[0]
    r = pltpu.roll(x, n - j, 0)
    bot = jnp.where(rowi >= 8 - j, pltpu.roll(halo, 8 - j, 0), r[n - 8:])
    return jnp.concatenate([r[:n - 8], bot], axis=0)


def _bdot(a, b, dims):
    return lax.dot_general(a.astype(_MXU), b.astype(_MXU), dims, preferred_element_type=F32)


def _bdotx(a, b, dims):
    return lax.dot_general(a, b, dims, precision=lax.Precision.HIGHEST, preferred_element_type=F32)


def _heads_to_batch(x, w):
    G = x.shape[0] // CHUNK
    x3 = x.reshape(G, CHUNK, HEADS * w)
    return jnp.stack([x3[:, :, w * h:w * (h + 1)] for h in range(HEADS)], axis=1).reshape(G * HEADS, CHUNK, w)


def _batch_to_heads(x3):
    B, _, w = x3.shape
    x4 = x3.reshape(B // HEADS, HEADS, CHUNK, w)
    return jnp.concatenate([x4[:, h] for h in range(HEADS)], axis=-1).reshape(B // HEADS * CHUNK, HEADS * w)


def _chunk_cumsum(x, rowmod, reverse=False):
    R = x.shape[0]
    for sh in (1, 2, 4, 8, 16, 32):
        if reverse:
            x = x + jnp.where(rowmod < CHUNK - sh, pltpu.roll(x, R - sh, 0), 0.0)
        else:
            x = x + jnp.where(rowmod >= sh, pltpu.roll(x, sh, 0), 0.0)
    return x


def _lane_col(x, c, lane):
    return _rowsum(jnp.where(lane == c, x, 0.0))


def _hg_gates(hq, hf, lb):
    sg = _sig(hf)
    nsg = _sig(-hf)
    f = lb + (1.0 - lb) * sg
    g = jnp.log(f)
    k = (1.0 - lb) * nsg
    sq = _sig(hq)
    return hq * sq, g, k, f, sg, nsg, sq


def _hg_prep(hq_ref, hf_ref, lg_ref, b_sc, k_sc):
    R = hq_ref.shape[0]
    G = R // CHUNK
    lb = _sig(lg_ref[0:1, :] - lg_ref[1:2, :])
    hq = hq_ref[...]
    q, g, k, f, sg, nsg, sq = _hg_gates(hq, hf_ref[...], lb)
    rowmod = _iota((R, HALF), 0) & (CHUNK - 1)
    b = _chunk_cumsum(g, rowmod) * LOG2E
    last8 = _iota((8, HALF), 0) == 7
    bl_rows = [_colsum(jnp.where(last8, b[CHUNK * c + CHUNK - 8:CHUNK * (c + 1)], 0.0)) for c in range(G)]
    bl3 = jnp.stack([r[:, 128 * h:128 * (h + 1)] for r in bl_rows for h in range(HEADS)], axis=0)
    b3, k3 = _heads_to_batch(b, 128), _heads_to_batch(k, 128)
    b_sc[...] = b3
    k_sc[...] = k3
    return dict(G=G, lb=lb, hq=hq, f=f, sg=sg, nsg=nsg, sq=sq, rowmod=rowmod, q3=_heads_to_batch(q, 128), k3=k3, b3=b3,
                bl3=bl3)


HSUB = 8


def _lo(j):
    return HSUB * (j // HSUB)


def _hg_diag_tiles(b_sc, b3, r0, rowi):
    bi = b3[:, r0:r0 + SUB]
    return [jnp.exp2(jnp.where(rowi[:, _lo(s):] >= s, bi[:, _lo(s):] - b_sc[:, r0 + s:r0 + s + 1, :], NEG))
            for s in range(SUB)]


def _hg_diag_tiles_t(b_sc, b3, r0, rowi):
    bi = b3[:, r0:r0 + SUB]
    return [jnp.exp2(jnp.where(rowi[:, :_lo(t) + HSUB] <= t, b_sc[:, r0 + t:r0 + t + 1, :] - bi[:, :_lo(t) + HSUB], NEG))
            for t in range(SUB)]


def _lane_sums(pieces, ones):
    B = pieces[0].shape[0]
    hs = [p.shape[1] for p in pieces]
    R = _dot(jnp.concatenate(pieces, axis=1).reshape(B * sum(hs), 128), ones).reshape(B, sum(hs), 128)
    out, o = [], 0
    for h in hs:
        out.append(R[:, o:o + h])
        o += h
    return out


def _sum_tri(terms, low_rows):
    if SUB == HSUB:
        return sum(terms)
    full = sum(t for t in terms if t.shape[1] == SUB)
    half = sum(t for t in terms if t.shape[1] == HSUB)
    lo, hi = full[:, :HSUB], full[:, HSUB:]
    return jnp.concatenate([lo + half, hi] if low_rows else [lo, hi + half], axis=1)


def _hgrn2_fwd(u, lb_logits, comms=None):
    T = u.shape[0]
    G = min(GC, T // CHUNK)
    R, B, N = G * CHUNK, G * HEADS, T // CHUNK

    def body(hq_ref, hf_ref, hv_ref, lg_ref, o_ref, st_ref, S_ref, b_sc, k_sc, v_sc):
        @pl.when(pl.program_id(0) == 0)
        def _():
            S_ref[...] = jnp.zeros_like(S_ref)

        pz = _hg_prep(hq_ref, hf_ref, lg_ref, b_sc, k_sc)
        q3, k3, b3, bl3 = pz["q3"], pz["k3"], pz["b3"], pz["bl3"]
        v3 = _heads_to_batch(hv_ref[...], 128)
        v_sc[...] = v3
        stloc = _bdot(v3, k3 * jnp.exp2(bl3 - b3), BTN).reshape(G, HEADS, 128, 128)
        dec = jnp.exp2(bl3).reshape(G, HEADS, 1, 128)
        ST = S_ref[...]
        sts = []
        for c in range(G):
            sts.append(ST)
            ST = ST * dec[c] + stloc[c]
        S_ref[...] = ST
        st4 = jnp.stack(sts, axis=0)
        st_ref[...] = st4
        o = _bdot(q3 * jnp.exp2(b3), st4.reshape(B, 128, 128), BNT)
        ones = jnp.ones((128, 128), F32)
        rowi = _iota((1, SUB, 128), 1)
        outs = []
        for i in range(CHUNK // SUB):
            r0 = SUB * i
            qi = q3[:, r0:r0 + SUB]
            oi = o[:, r0:r0 + SUB]
            if i > 0:
                r = b_sc[:, r0 - 1:r0, :]
                qe = qi * jnp.exp2(b3[:, r0:r0 + SUB] - r)
                ke = k3[:, :r0] * jnp.exp2(r - b3[:, :r0])
                oi = oi + _bdot(_bdot(qe, ke, BNT), v3[:, :r0], BNN)
            tiles = _hg_diag_tiles(b_sc, b3, r0, rowi)
            a_b = _lane_sums([qi[:, _lo(s):] * (k_sc[:, r0 + s:r0 + s + 1, :] * tiles[s]) for s in range(SUB)], ones)
            outs.append(oi + _sum_tri([a_b[s] * v_sc[:, r0 + s:r0 + s + 1, :] for s in range(SUB)], False))
        o_ref[...] = _batch_to_heads(jnp.concatenate(outs, axis=1))

    blk = lambda c: pl.BlockSpec((R, HALF), lambda n, c=c: (n, c // 4))
    return _hosted_call(
        body, comms, name="hgrn2_fwd", grid=(N // G,),
        in_specs=[blk(C_HQ), blk(C_HF), blk(C_HV), _full(lb_logits.shape)],
        out_specs=[pl.BlockSpec((R, HALF), lambda n: (n, 0)),
                   pl.BlockSpec((G, HEADS, 128, 128), lambda n: (n, 0, 0, 0))],
        out_shape=[_sds((T, HALF)), _sds((N, HEADS, 128, 128))],
        scratch_shapes=[pltpu.VMEM((HEADS, 128, 128), F32)] + [pltpu.VMEM((B, CHUNK, 128), F32)] * 3,
        args=(u, u, u, lb_logits))


def _hgrn2_bwd(u, lb_logits, do, states, comms=None):
    T = u.shape[0]
    G = min(GC, T // CHUNK)
    R, B, NG = G * CHUNK, G * HEADS, T // (G * CHUNK)

    def body(hq_ref, hf_ref, hv_ref, lg_ref, do_ref, st_ref, dhq_ref, dhf_ref, dhv_ref, dlb_ref,
             dS_ref, b_sc, k_sc, v_sc, q_sc, do_sc):
        @pl.when(pl.program_id(0) == 0)
        def _():
            dS_ref[...] = jnp.zeros_like(dS_ref)
            dlb_ref[...] = jnp.zeros_like(dlb_ref)

        pz = _hg_prep(hq_ref, hf_ref, lg_ref, b_sc, k_sc)
        q3, k3, b3, bl3, lb = pz["q3"], pz["k3"], pz["b3"], pz["bl3"], pz["lb"]
        v3 = _heads_to_batch(hv_ref[...], 128)
        v_sc[...] = v3
        do3 = _heads_to_batch(do_ref[...], 128)
        q_sc[...] = q3
        do_sc[...] = do3
        st3 = st_ref[...].reshape(B, 128, 128)
        eb = jnp.exp2(b3)
        ebl = jnp.exp2(bl3 - b3)
        qt = q3 * eb
        kl = k3 * ebl
        dstloc = _bdot(do3, qt, BTN).reshape(G, HEADS, 128, 128)
        dec = jnp.exp2(bl3).reshape(G, HEADS, 1, 128)
        dST = dS_ref[...]
        dsts = [None] * G
        for c in reversed(range(G)):
            dsts[c] = dST
            dST = dST * dec[c] + dstloc[c]
        dS_ref[...] = dST
        dst3 = jnp.stack(dsts, axis=0).reshape(B, 128, 128)
        dqt = _bdot(do3, st3, BNN)
        dkl = _bdot(v3, dst3, BNN)
        dv_acc = _bdot(kl, dst3, BNT)
        ones = jnp.ones((128, 128), F32)
        rowi = _iota((1, SUB, 128), 1)
        dq_parts, dk_parts, dv_parts = [], [], []
        dk_in = jnp.zeros((B, CHUNK, 128), F32)
        for i_s in range(CHUNK // SUB):
            r0 = SUB * i_s
            qi = q3[:, r0:r0 + SUB]
            doi = do3[:, r0:r0 + SUB]
            dqi = jnp.zeros((B, SUB, 128), F32)
            if i_s > 0:
                r = b_sc[:, r0 - 1:r0, :]
                eq = jnp.exp2(b3[:, r0:r0 + SUB] - r)
                ek = jnp.exp2(r - b3[:, :r0])
                qe = qi * eq
                ke = k3[:, :r0] * ek
                a_off = _bdot(qe, ke, BNT)
                p_off = _bdot(doi, v3[:, :r0], BNT)
                pad = jnp.zeros((B, CHUNK - r0, 128), F32)
                dv_acc = dv_acc + jnp.concatenate([_bdot(a_off, doi, BTN), pad], axis=1)
                dqi = dqi + _bdot(p_off, ke, BNN) * eq
                dk_in = dk_in + jnp.concatenate([_bdot(p_off, qe, BTN) * ek, pad], axis=1)
            ki, vi = k3[:, r0:r0 + SUB], v3[:, r0:r0 + SUB]
            rng = range(SUB)
            tiles = _hg_diag_tiles(b_sc, b3, r0, rowi)
            tiles_t = _hg_diag_tiles_t(b_sc, b3, r0, rowi)
            do_rows = [do_sc[:, r0 + t:r0 + t + 1, :] for t in rng]
            kts = [k_sc[:, r0 + s:r0 + s + 1, :] * tiles[s] for s in rng]
            qts = [q_sc[:, r0 + t:r0 + t + 1, :] * tiles_t[t] for t in rng]
            ps = [doi[:, _lo(s):] * v_sc[:, r0 + s:r0 + s + 1, :] for s in rng]
            mst = [ki[:, :_lo(t) + HSUB] * qts[t] for t in rng]
            pst = [vi[:, :_lo(t) + HSUB] * do_rows[t] for t in rng]
            sums = _lane_sums(ps + mst + pst, ones)
            p_b, a_t, p_t = sums[:SUB], sums[SUB:2 * SUB], sums[2 * SUB:]
            dq_parts.append(dqi + _sum_tri([p_b[s] * kts[s] for s in rng], False))
            dv_parts.append(_sum_tri([a_t[t] * do_rows[t] for t in rng], True))
            dk_parts.append(_sum_tri([p_t[t] * qts[t] for t in rng], True))
        dq_in = jnp.concatenate(dq_parts, axis=1)
        dk_in = dk_in + jnp.concatenate(dk_parts, axis=1)
        dv_acc = dv_acc + jnp.concatenate(dv_parts, axis=1)
        db = qt * dqt + q3 * dq_in - k3 * dk_in - kl * dkl
        last = jnp.sum(kl * dkl, axis=1, keepdims=True) + jnp.exp2(bl3) * jnp.sum(st3 * dst3, axis=1, keepdims=True)
        db = db + jnp.where(_iota((1, CHUNK, 1), 1) == CHUNK - 1, last, 0.0)
        dg = _chunk_cumsum(_batch_to_heads(db), pz["rowmod"], reverse=True)
        dq_tot = _batch_to_heads(dqt * eb + dq_in)
        dk_tot = _batch_to_heads(dkl * ebl + dk_in)
        common = dg / pz["f"] - dk_tot
        dhf_ref[...] = ((1.0 - lb) * pz["sg"] * pz["nsg"] * common).astype(dhf_ref.dtype)
        dl0 = _colsum(pz["nsg"] * common) * lb * (1.0 - lb)
        dlb_ref[0:1, :] += dl0
        dlb_ref[1:2, :] -= dl0
        dhq_ref[...] = (dq_tot * _dsilu(pz["hq"], pz["sq"])).astype(dhq_ref.dtype)
        dhv_ref[...] = _batch_to_heads(dv_acc).astype(dhv_ref.dtype)

    rev = lambda c: pl.BlockSpec((R, HALF), lambda i, c=c: (NG - 1 - i, c // 4))
    rev0 = pl.BlockSpec((R, HALF), lambda i: (NG - 1 - i, 0))
    return _hosted_call(
        body, comms, name="hgrn2_bwd", grid=(NG,),
        in_specs=[rev(C_HQ), rev(C_HF), rev(C_HV), _full(lb_logits.shape), rev0,
                  pl.BlockSpec((G, HEADS, 128, 128), lambda i: (NG - 1 - i, 0, 0, 0))],
        out_specs=[rev0, rev0, rev0, _full((2, HALF))],
        out_shape=[_sds((T, HALF), _MXU)] * 3 + [_sds((2, HALF))],
        scratch_shapes=[pltpu.VMEM((HEADS, 128, 128), F32)] + [pltpu.VMEM((B, CHUNK, 128), F32)] * 5,
        args=(u, u, u, lb_logits, do, states))


def _lanes_to_batch_cols(x, lane):
    G = x.shape[0] // CHUNK
    cols = [_lane_col(x, 4 + h, lane).reshape(G, CHUNK, 1) for h in range(HEADS)]
    return jnp.stack(cols, axis=1).reshape(G * HEADS, CHUNK, 1)


def _row_scalars(rows):
    lane = _iota((1, 128), 1)
    return jnp.stack([_rowsum(jnp.where(lane == 4 + h, r, 0.0)) for r in rows for h in range(HEADS)], axis=0)


def _ml_gates(gates):
    R = gates.shape[0]
    lane = _iota((R, 128), 1)
    rowmod = _iota((R, 128), 0) & (CHUNK - 1)
    lf = jnp.minimum(gates, 0.0) - jnp.log(1.0 + jnp.exp(-jnp.abs(gates)))
    g_all = _chunk_cumsum(lf, rowmod)
    x_all = pltpu.roll(gates, 4, 1) - g_all
    return g_all, x_all, lane, rowmod


def _ml_chunk_rows(g_all, x_all, mprev, g):
    gl = g_all[CHUNK * g + CHUNK - 8:CHUNK * (g + 1)]
    gl = _colsum(jnp.where(_iota((8, 128), 0) == 7, gl, 0.0))
    a = gl + x_all[CHUNK * g:CHUNK * (g + 1)]
    m_new = jnp.maximum(gl + mprev, jnp.max(a, axis=0, keepdims=True))
    return m_new, jnp.exp(gl + mprev - m_new), jnp.exp(a - m_new)


def _ml_batched(q3, k3, v3, g_all, x_all, lane, C3, n3, mprev3):
    G = g_all.shape[0] // CHUNK
    gcol3 = _lanes_to_batch_cols(g_all, lane)
    onehot = jnp.where(_iota((G, 8, 128), 1) + 4 == _iota((G, 8, 128), 2), 1.0, 0.0).astype(F32)
    rows = _bdotx(onehot, x_all.reshape(G, CHUNK, 128), BNT)
    sub = _iota((G, 8, CHUNK), 1)
    row3 = jnp.stack([jnp.sum(jnp.where(sub == h, rows, 0.0), axis=1, keepdims=True) for h in range(HEADS)],
                     axis=1).reshape(G * HEADS, 1, CHUNK)
    causal = _iota((1, CHUNK, CHUNK), 1) >= _iota((1, CHUNK, CHUNK), 2)
    dmat = jnp.where(causal, gcol3 + row3, NEG)
    m_inter = gcol3 + mprev3
    m_t = jnp.maximum(m_inter, jnp.max(dmat, axis=2, keepdims=True))
    wi = jnp.exp(dmat - m_t)
    wn = jnp.exp(m_inter - m_t)
    s3 = _bdot(q3, k3, BNT) * wi
    qc = _bdot(q3, C3, BNN)
    qn = jnp.sum(q3 * n3, axis=2, keepdims=True)
    num = _bdot(s3, v3, BNN) + wn * qc
    den = jnp.sum(s3, axis=2, keepdims=True) + wn * qn
    floor = jnp.exp(-m_t)
    return dict(wi=wi, wn=wn, s=s3, qc=qc, qn=qn, num=num, den=den, floor=floor, nrm=jnp.maximum(jnp.abs(den), floor))


def _mlstm_fwd(qkc, u, comms=None):
    T = u.shape[0]
    G = min(GC, T // CHUNK)
    R = G * CHUNK
    N = T // CHUNK

    def body(qk_ref, v_ref, g_ref, h_ref, cst_ref, nst_ref, mst_ref, C_ref, n_ref, m_ref):
        @pl.when(pl.program_id(0) == 0)
        def _():
            C_ref[...] = jnp.zeros_like(C_ref)
            n_ref[...] = jnp.zeros_like(n_ref)
            m_ref[...] = jnp.zeros_like(m_ref)

        g_all, x_all, lane, _ = _ml_gates(g_ref[...])
        m_row = m_ref[...]
        mprev_rows, wo_rows, ws_parts = [], [], []
        for g in range(G):
            mprev_rows.append(m_row)
            m_row, wo, ws = _ml_chunk_rows(g_all, x_all, m_row, g)
            wo_rows.append(wo)
            ws_parts.append(ws)
        m_ref[...] = m_row
        mst_ref[...] = jnp.stack(mprev_rows, axis=0)
        ws3 = _lanes_to_batch_cols(jnp.concatenate(ws_parts, axis=0), lane)
        wo4 = _row_scalars(wo_rows).reshape(G, HEADS, 1, 1)
        q3 = _heads_to_batch(qk_ref[:, :256] * ML_SCALE, ML_DQK)
        k3 = _heads_to_batch(qk_ref[:, 256:], ML_DQK)
        v3 = _heads_to_batch(v_ref[...], 128)
        kw = k3 * ws3
        cloc = _bdot(kw, v3, BTN).reshape(G, HEADS, ML_DQK, 128)
        nloc = jnp.sum(kw, axis=1, keepdims=True).reshape(G, HEADS, 1, ML_DQK)
        C, nn = C_ref[...], n_ref[...]
        cs, ns = [], []
        for g in range(G):
            cs.append(C)
            ns.append(nn)
            C = wo4[g] * C + cloc[g]
            nn = wo4[g] * nn + nloc[g]
        C_ref[...] = C
        n_ref[...] = nn
        c4, n4 = jnp.stack(cs, axis=0), jnp.stack(ns, axis=0)
        cst_ref[...] = c4
        nst_ref[...] = n4
        r = _ml_batched(q3, k3, v3, g_all, x_all, lane, c4.reshape(G * HEADS, ML_DQK, 128),
                        n4.reshape(G * HEADS, 1, ML_DQK), _row_scalars(mprev_rows))
        h_ref[...] = _batch_to_heads(r["num"] / r["nrm"])

    return _hosted_call(
        body, comms, name="mlstm_fwd", grid=(N // G,),
        in_specs=[pl.BlockSpec((R, HALF), lambda n: (n, 0)), pl.BlockSpec((R, HALF), lambda n: (n, C_MV // 4)),
                  pl.BlockSpec((R, 128), lambda n: (n, C_GATES))],
        out_specs=[pl.BlockSpec((R, HALF), lambda n: (n, 0)),
                   pl.BlockSpec((G, HEADS, ML_DQK, 128), lambda n: (n, 0, 0, 0)),
                   pl.BlockSpec((G, HEADS, 1, ML_DQK), lambda n: (n, 0, 0, 0)),
                   pl.BlockSpec((G, 1, 128), lambda n: (n, 0, 0))],
        out_shape=[_sds((T, HALF)), _sds((N, HEADS, ML_DQK, 128)), _sds((N, HEADS, 1, ML_DQK)), _sds((N, 1, 128))],
        scratch_shapes=[pltpu.VMEM((HEADS, ML_DQK, 128), F32), pltpu.VMEM((HEADS, 1, ML_DQK), F32),
                        pltpu.VMEM((1, 128), F32)],
        args=(qkc, u, u))


def _mlstm_bwd(qkc, u, dh, cst, nst, mst, pre, cw, comms=None):
    T = u.shape[0]
    G = min(GC, T // CHUNK)
    R = G * CHUNK
    NG = T // R

    def body(qk_ref, v_ref, g_ref, dh_ref, cst_ref, nst_ref, mst_ref, pre_ref, x_ref, xh_ref, cw_ref,
             dmqk_ref, dv_ref, dgt_ref, dcw_ref, dcb_ref, dC_ref, dn_ref, next_sc):
        @pl.when(pl.program_id(0) == 0)
        def _():
            for r in (dC_ref, dn_ref, next_sc, dcw_ref, dcb_ref):
                r[...] = jnp.zeros_like(r)

        B = G * HEADS
        gates = g_ref[...]
        g_all, x_all, lane, rowmod = _ml_gates(gates)
        mprev_rows = [mst_ref[g] for g in range(G)]
        wo_rows, ws_parts = [], []
        for g in range(G):
            _, wo, ws = _ml_chunk_rows(g_all, x_all, mprev_rows[g], g)
            wo_rows.append(wo)
            ws_parts.append(ws)
        ws3 = _lanes_to_batch_cols(jnp.concatenate(ws_parts, axis=0), lane)
        wo3 = _row_scalars(wo_rows)
        wo4 = wo3.reshape(G, HEADS, 1, 1)
        q3 = _heads_to_batch(qk_ref[:, :256] * ML_SCALE, ML_DQK)
        k3 = _heads_to_batch(qk_ref[:, 256:], ML_DQK)
        v3 = _heads_to_batch(v_ref[...], 128)
        dh3 = _heads_to_batch(dh_ref[...], 128)
        C3 = cst_ref[...].reshape(B, ML_DQK, 128)
        n3 = nst_ref[...].reshape(B, 1, ML_DQK)
        r = _ml_batched(q3, k3, v3, g_all, x_all, lane, C3, n3, _row_scalars(mprev_rows))
        wn, s3 = r["wn"], r["s"]
        inv = 1.0 / r["nrm"]
        dnum = dh3 * inv
        dnrm = -jnp.sum(dh3 * (r["num"] * inv), axis=2, keepdims=True) * inv
        dden = jnp.where(jnp.abs(r["den"]) > r["floor"], dnrm * jnp.sign(r["den"]), 0.0)
        ds = _bdot(dnum, v3, BNT) + dden
        dqk = ds * r["wi"]
        dd = ds * s3
        qw = q3 * wn
        dcloc = _bdot(qw, dnum, BTN).reshape(G, HEADS, ML_DQK, 128)
        dnloc = jnp.sum(qw * dden, axis=1, keepdims=True).reshape(G, HEADS, 1, ML_DQK)
        dC, dn = dC_ref[...], dn_ref[...]
        dcs, dns = [None] * G, [None] * G
        for g in reversed(range(G)):
            dcs[g], dns[g] = dC, dn
            dC = wo4[g] * dC + dcloc[g]
            dn = wo4[g] * dn + dnloc[g]
        dC_ref[...] = dC
        dn_ref[...] = dn
        dC3 = jnp.stack(dcs, axis=0).reshape(B, ML_DQK, 128)
        dn3 = jnp.stack(dns, axis=0).reshape(B, 1, ML_DQK)
        dk_st = ws3 * (_bdot(v3, dC3, BNT) + dn3)
        dq = _bdot(dqk, k3, BNN) + wn * (_bdot(dnum, C3, BNT) + dden * n3)
        dk = _bdot(dqk, q3, BTN) + dk_st
        dv = _bdot(s3, dnum, BTN) + ws3 * _bdot(k3, dC3, BNN)
        dv_ref[...] = _batch_to_heads(dv).astype(dv_ref.dtype)
        dqk = jnp.concatenate([_batch_to_heads(dq * ML_SCALE), _batch_to_heads(dk)], axis=1)
        pre = pre_ref[...]
        dpre = dqk * _dsilu(pre, _sig(pre))
        x = x_ref[...]
        xprev = jnp.where(pl.program_id(0) < NG - 1, xh_ref[...], 0.0)
        nxt = next_sc[...]
        row8 = _iota((8, HALF), 0)
        dx = dpre * cw_ref[3:4, :]
        dws = [None, None, None, _colsum(dpre * x)]
        for j in (1, 2, 3):
            dx = dx + _shift_rows_up(dpre, nxt, j, row8) * cw_ref[3 - j:4 - j, :]
            dws[3 - j] = _colsum(dpre * _shift_rows(x, xprev, j, row8))
        dmqk_ref[...] = dx.astype(dmqk_ref.dtype)
        dcw_ref[...] += jnp.concatenate(dws, axis=0)
        dcb_ref[...] += _colsum(dpre)
        next_sc[...] = dpre[:8]
        e_col = wn * (jnp.sum(dnum * r["qc"], axis=2, keepdims=True) + dden * r["qn"])
        c_col = jnp.sum(k3 * dk_st, axis=2, keepdims=True)
        z = wo3 * (jnp.sum(dC3 * C3, axis=(1, 2), keepdims=True) + jnp.sum(dn3 * n3, axis=(1, 2), keepdims=True))
        dd_hi = dd.astype(_MXU).astype(F32)
        ones = jnp.ones((B, CHUNK, 128), F32)
        dd_cols = (_bdot(dd_hi, ones, BTN) + _bdot(dd - dd_hi, ones, BTN))[:, :, 0:1]
        last = _iota((1, CHUNK, 1), 1) == CHUNK - 1
        dg3 = jnp.sum(dd, axis=2, keepdims=True) - dd_cols + e_col - c_col
        dg3 = dg3 + jnp.where(last, jnp.sum(c_col, axis=1, keepdims=True) + z, 0.0)
        di3 = dd_cols + c_col

        def to_lanes(x3, first):
            x4 = x3.reshape(G, HEADS, CHUNK, 1)
            return sum(jnp.where(lane == first + h, x4[:, h].reshape(R, 1), 0.0) for h in range(HEADS))

        dlf = _chunk_cumsum(to_lanes(dg3, 4), rowmod, reverse=True)
        dgt_ref[...] = (to_lanes(di3, 0) + dlf * _sig(-gates)).astype(dgt_ref.dtype)

    rev = lambda w, c: pl.BlockSpec((R, w), lambda i, c=c: (NG - 1 - i, c))
    st = lambda *s: pl.BlockSpec((G,) + s, lambda i: (NG - 1 - i,) + (0,) * len(s))
    halo = pl.BlockSpec((8, HALF), lambda i: (jnp.maximum((NG - 1 - i) * (R // 8) - 1, 0), C_MQK // 4))
    return _hosted_call(
        body, comms, name="mlstm_bwd", grid=(NG,),
        in_specs=[rev(HALF, 0), rev(HALF, C_MV // 4), rev(128, C_GATES), rev(HALF, 0),
                  st(HEADS, ML_DQK, 128), st(HEADS, 1, ML_DQK), st(1, 128),
                  rev(HALF, 0), rev(HALF, C_MQK // 4), halo, _full(cw.shape)],
        out_specs=[rev(HALF, 0), rev(HALF, 0), rev(128, 0), _full((4, HALF)), _full((1, HALF))],
        out_shape=[_sds((T, HALF), _MXU), _sds((T, HALF), _MXU), _sds((T, 128), _MXU), _sds((4, HALF)), _sds((1, HALF))],
        scratch_shapes=[pltpu.VMEM((HEADS, ML_DQK, 128), F32), pltpu.VMEM((HEADS, 1, ML_DQK), F32),
                        pltpu.VMEM((8, HALF), F32)],
        args=(qkc, u, u, dh, cst, nst, mst, pre, u, u, cw))


def _head_norm(o):
    rs_parts, r_parts = [], []
    for h in range(HEADS):
        oh = o[:, 128 * h:128 * (h + 1)]
        rs = lax.rsqrt(jnp.mean(oh * oh, axis=-1, keepdims=True) + RMS_EPS)
        rs_parts.append(rs)
        r_parts.append(oh * rs)
    return jnp.concatenate(r_parts, axis=1), rs_parts


def _out_proj(x, u, o_hg, h_ml, g_hg, g_ml, w_out, tm):
    T = x.shape[0]

    def body(x_ref, hgate_ref, mo_ref, ohg_ref, hml_ref, ghg_ref, gml_ref, w_ref, m_ref, z_ref):
        hgate = hgate_ref[...]
        a = _head_norm(ohg_ref[...])[0] * ghg_ref[...] * (hgate * _sig(hgate))
        b = _head_norm(hml_ref[...])[0] * gml_ref[...] * _sig(mo_ref[...])
        m = jnp.concatenate([a, b], axis=1)
        m_ref[...] = m.astype(m_ref.dtype)
        z_ref[...] = ALPHA * x_ref[...] + _dot(m, w_ref[...])

    return pl.pallas_call(
        body, name="out_proj", grid=(T // tm,),
        in_specs=[_row(tm, D_MODEL), _row(tm, HALF, C_HGATE // 4), _row(tm, HALF, C_MO // 4),
                  _row(tm, HALF), _row(tm, HALF), _full(g_hg.shape), _full(g_ml.shape), _full(w_out.shape)],
        out_specs=[_row(tm, D_MODEL)] * 2,
        out_shape=[_sds((T, D_MODEL), _MXU), _sds((T, D_MODEL))],
        compiler_params=_cp("parallel"))(x, u, u, o_hg, h_ml, g_hg, g_ml, w_out)


def _ffn_ln(z1, ln1_g, ln1_b, wg, wu, wd, ln_g, ln_b, tm):
    T = z1.shape[0]

    def body(z1_ref, g1_ref, b1_ref, wg_ref, wu_ref, wd_ref, g_ref, b_ref, z_ref, x1_ref, x2_ref, a_ref, bb_ref, h_ref):
        x = _ln_fwd(z1_ref[...], g1_ref[...], b1_ref[...])[0]
        x1_ref[...] = x.astype(x1_ref.dtype)
        a = _dot(x, wg_ref[...], NT)
        bb = _dot(x, wu_ref[...], NT)
        hh = a * _sig(a) * bb
        a_ref[...] = a.astype(a_ref.dtype)
        bb_ref[...] = bb.astype(bb_ref.dtype)
        h_ref[...] = hh.astype(h_ref.dtype)
        z = ALPHA * x + _dot(hh, wd_ref[...])
        z_ref[...] = z
        x2_ref[...] = _ln_fwd(z, g_ref[...], b_ref[...])[0].astype(x2_ref.dtype)

    vec = _full((1, D_MODEL))
    return pl.pallas_call(
        body, name="ffn_ln2", grid=(T // tm,),
        in_specs=[_row(tm, D_MODEL), vec, vec, _full(wg.shape), _full(wu.shape), _full(wd.shape), vec, vec],
        out_specs=[_row(tm, D_MODEL)] * 3 + [_row(tm, D_FF)] * 3,
        out_shape=[_sds((T, D_MODEL))] + [_sds((T, D_MODEL), _MXU)] * 2 + [_sds((T, D_FF), _MXU)] * 3,
        compiler_params=_cp("parallel"))(z1, ln1_g, ln1_b, wg, wu, wd, ln_g, ln_b)


def _ple_loss_ln2_bwd(z2, p, tgt, wpg, bpg, wpp, ln_g, ln_b, tm):
    T = z2.shape[0]

    def body(z_ref, p_ref, t_ref, wpg_ref, bpg_ref, wpp_ref, g_ref, b_ref,
             de_ref, dgp_ref, dz_ref, loss_ref, dbpg_ref, dg_ref, db_ref):
        @pl.when(pl.program_id(0) == 0)
        def _():
            for r in (loss_ref, dbpg_ref, dg_ref, db_ref):
                r[...] = jnp.zeros_like(r)

        x2, xhat, rstd = _ln_fwd(z_ref[...], g_ref[...], b_ref[...])
        gate = _sig(_dot(x2, wpg_ref[...]) + bpg_ref[...])
        e = _dot(p_ref[...], wpp_ref[...])
        err = x2 + gate * e - t_ref[...]
        loss_ref[...] += _colsum(err * err)
        dy = err * (1.0 / D_MODEL)
        de_ref[...] = (dy * gate).astype(de_ref.dtype)
        dgp = dy * e * gate * (1.0 - gate)
        dgp_ref[...] = dgp.astype(dgp_ref.dtype)
        dbpg_ref[...] += _colsum(dgp)
        dx2 = dy + _dot(dgp, wpg_ref[...], NT)
        dg_ref[...] += _colsum(dx2 * xhat)
        db_ref[...] += _colsum(dx2)
        dz_ref[...] = _ln_bwd(dx2, xhat, rstd, g_ref[...])

    vec = _full((1, D_MODEL))
    return pl.pallas_call(
        body, name="ple_loss_ln2_bwd", grid=(T // tm,),
        in_specs=[_row(tm, D_MODEL), _row(tm, PLE_DIM), _row(tm, D_MODEL),
                  _full(wpg.shape), vec, _full(wpp.shape), vec, vec],
        out_specs=[_row(tm, D_MODEL)] * 3 + [vec] * 4,
        out_shape=[_sds((T, D_MODEL), _MXU)] * 2 + [_sds((T, D_MODEL))] + [_sds((1, D_MODEL))] * 4,
        compiler_params=_cp("arbitrary"))(z2, p, tgt, wpg, bpg, wpp, ln_g, ln_b)


def _ffn_bwd_ln1_bwd(a_pre, b_pre, z1, dz2, wg, wu, wd, ln_g, ln_b, tm, comms=None):
    T = z1.shape[0]

    def body(a_ref, bb_ref, z_ref, dz2_ref, wg_ref, wu_ref, wd_ref, g_ref, b_ref,
             da_ref, dbb_ref, dz1_ref, dg_ref, db_ref):
        @pl.when(pl.program_id(0) == 0)
        def _():
            dg_ref[...] = jnp.zeros_like(dg_ref)
            db_ref[...] = jnp.zeros_like(db_ref)

        dz2 = dz2_ref[...]
        a = a_ref[...].astype(F32)
        bb = bb_ref[...].astype(F32)
        sa = _sig(a)
        act = a * sa
        dh = _dot(dz2, wd_ref[...], NT)
        da = (dh * bb * _dsilu(a, sa)).astype(da_ref.dtype)
        dbb = (dh * act).astype(dbb_ref.dtype)
        da_ref[...] = da
        dbb_ref[...] = dbb
        dx1 = ALPHA * dz2 + _dot(da, wg_ref[...]) + _dot(dbb, wu_ref[...])
        _, xhat, rstd = _ln_fwd(z_ref[...], g_ref[...], b_ref[...])
        dg_ref[...] += _colsum(dx1 * xhat)
        db_ref[...] += _colsum(dx1)
        dz1_ref[...] = _ln_bwd(dx1, xhat, rstd, g_ref[...])

    vec = _full((1, D_MODEL))
    return _hosted_call(
        body, comms, name="ffn_bwd_ln1_bwd", grid=(T // tm,),
        in_specs=[_row(tm, D_FF)] * 2 + [_row(tm, D_MODEL)] * 2 + [_full(wg.shape), _full(wu.shape), _full(wd.shape), vec, vec],
        out_specs=[_row(tm, D_FF)] * 2 + [_row(tm, D_MODEL), vec, vec],
        out_shape=[_sds((T, D_FF), _MXU)] * 2 + [_sds((T, D_MODEL)), _sds((1, D_MODEL)), _sds((1, D_MODEL))],
        scratch_shapes=[], args=(a_pre, b_pre, z1, dz2, wg, wu, wd, ln_g, ln_b))


def _out_proj_bwd(dz1, u, o_hg, h_ml, g_hg, g_ml, w_out, tm):
    T = dz1.shape[0]

    def body(dz_ref, hgate_ref, mo_ref, ohg_ref, hml_ref, ghg_ref, gml_ref, w_ref,
             dohg_ref, dhml_ref, dhgate_ref, dmo_ref, dghg_ref, dgml_ref):
        @pl.when(pl.program_id(0) == 0)
        def _():
            dghg_ref[...] = jnp.zeros_like(dghg_ref)
            dgml_ref[...] = jnp.zeros_like(dgml_ref)

        dm = _dot(dz_ref[...], w_ref[...], NT)

        def half(dmh, o, gvec, gate_val, dgate_fac, do_ref, dgate_ref, dgvec_ref):
            r, rs = _head_norm(o)
            dgate_ref[...] = (dmh * r * gvec * dgate_fac).astype(dgate_ref.dtype)
            dn = dmh * gate_val
            dgvec_ref[...] += _colsum(dn * r)
            dr = dn * gvec
            parts = []
            for h in range(HEADS):
                sl = slice(128 * h, 128 * (h + 1))
                parts.append(rs[h] * (dr[:, sl] - r[:, sl] * jnp.mean(dr[:, sl] * r[:, sl], axis=-1, keepdims=True)))
            do_ref[...] = jnp.concatenate(parts, axis=1)

        hg = hgate_ref[...]
        shg = _sig(hg)
        half(dm[:, :HALF], ohg_ref[...], ghg_ref[...], hg * shg, _dsilu(hg, shg), dohg_ref, dhgate_ref, dghg_ref)
        smo = _sig(mo_ref[...])
        half(dm[:, HALF:], hml_ref[...], gml_ref[...], smo, smo * (1.0 - smo), dhml_ref, dmo_ref, dgml_ref)

    vec = _full((1, HALF))
    return pl.pallas_call(
        body, name="out_proj_bwd", grid=(T // tm,),
        in_specs=[_row(tm, D_MODEL), _row(tm, HALF, C_HGATE // 4), _row(tm, HALF, C_MO // 4),
                  _row(tm, HALF), _row(tm, HALF), vec, vec, _full(w_out.shape)],
        out_specs=[_row(tm, HALF)] * 4 + [vec, vec],
        out_shape=[_sds((T, HALF))] * 2 + [_sds((T, HALF), _MXU)] * 2 + [_sds((1, HALF))] * 2,
        compiler_params=_cp("arbitrary"))(dz1, u, u, o_hg, h_ml, g_hg, g_ml, w_out)


def _du_specs(rows):
    return [pl.BlockSpec((rows, w), lambda i: (i, 0)) for w in DU_WIDTHS]


def _in_proj_bwd(dz1, du_parts, w, tm, comms=None):
    T = dz1.shape[0]

    def body(dz_ref, *refs):
        du = jnp.concatenate([r[...] for r in refs[:8]], axis=1)
        refs[9][...] = ALPHA * dz_ref[...] + _dot(du, refs[8][...], NT)

    (dx,), got = _hosted_call(
        body, comms, name="in_proj_bwd", grid=(T // tm,),
        in_specs=[_row(tm, D_MODEL)] + _du_specs(tm) + [_full(w.shape)],
        out_specs=[_row(tm, D_MODEL)], out_shape=[_sds((T, D_MODEL))], scratch_shapes=[], args=(dz1, *du_parts, w))
    return dx, got


def _wgrad(a, b, name, tm, tn, tk):
    T, M = a.shape
    N = b.shape[1]
    tm, tn, tk = min(tm, M), min(tn, N), min(tk, T)
    nk = T // tk

    def body(a_ref, b_ref, o_ref, acc_ref):
        kk = pl.program_id(2)

        @pl.when(kk == 0)
        def _():
            acc_ref[...] = jnp.zeros_like(acc_ref)

        acc_ref[...] += _dot(a_ref[...], b_ref[...], TN)

        @pl.when(kk == nk - 1)
        def _():
            o_ref[...] = acc_ref[...].astype(o_ref.dtype)

    return pl.pallas_call(
        body, name=name, grid=(M // tm, N // tn, nk),
        in_specs=[pl.BlockSpec((tk, tm), lambda i, j, kk: (kk, i)), pl.BlockSpec((tk, tn), lambda i, j, kk: (kk, j))],
        out_specs=pl.BlockSpec((tm, tn), lambda i, j, kk: (i, j)), out_shape=_sds((M, N), _MXU),
        scratch_shapes=[pltpu.VMEM((tm, tn), F32)],
        compiler_params=_cp("parallel", "parallel", "arbitrary"))(a, b)


W_IN_PARTS = 2


def _wgrad_w_in(x, du_parts, tk, part, comms=None):
    T = x.shape[0]
    M = D_MODEL // W_IN_PARTS
    tk = min(tk, T)
    nk = T // tk

    def body(a_ref, *refs):
        o_ref, cs_ref, acc_ref = refs[8:]
        kk = pl.program_id(0)

        @pl.when(kk == 0)
        def _():
            acc_ref[...] = jnp.zeros_like(acc_ref)
            cs_ref[...] = jnp.zeros_like(cs_ref)

        du = jnp.concatenate([r[...] for r in refs[:8]], axis=1)
        acc_ref[...] += _dot(a_ref[...], du, TN)
        cs_ref[...] += _colsum(du.astype(F32))

        @pl.when(kk == nk - 1)
        def _():
            o_ref[...] = acc_ref[...].astype(o_ref.dtype)

    return _hosted_call(
        body, comms, name="wgrad_w_in_%d" % part, grid=(nk,),
        in_specs=[pl.BlockSpec((tk, M), lambda kk: (kk, part))] + _du_specs(tk),
        out_specs=[_full((M, PROJ_WP)), _full((1, PROJ_WP))],
        out_shape=[_sds((M, PROJ_WP), _MXU), _sds((1, PROJ_WP))],
        scratch_shapes=[pltpu.VMEM((M, PROJ_WP), F32)], args=(x, *du_parts))


W_IN_S, FF_S, OUT_S, PP_S = PROJ_W // N_DEV, D_FF // N_DEV, D_MODEL // N_DEV, D_MODEL // N_DEV
LATE = ("w_ffn_gate", "w_ffn_up", "w_out", "w_ffn_down", "ple_w_gate", "ple_w_proj")
BIG = ("w_in",) + LATE
TRANSPOSED = ("w_ffn_gate", "w_ffn_up")


def _split_cols(a, n):
    return a.reshape(a.shape[0], N_DEV, n).transpose(1, 0, 2)


def _join_cols(a):
    return a.transpose(1, 0, 2).reshape(a.shape[1], -1)


def _step(x, p, tgt, w_in, b_in, lb_logits, conv_w, conv_b, g_hg, g_ml, ln1_g, ln1_b, ln2_g, ln2_b, bpg, late,
          distributed):
    T = x.shape[0]
    tm, tf = min(ROWS, T), min(ROWS_FFN, T)
    gather = lambda *names: [_GatherTwoLevel([late[n] for n in names])] if distributed else None
    scatter = lambda *arrs: [_Comm("scatter", list(arrs))] if distributed else None
    rows = lambda a, n: a.reshape(N_DEV, n, D_MODEL)
    (u, pre, qkc), got1 = _in_proj(x, w_in, b_in, conv_w, conv_b, tm, gather("w_out", "ple_w_gate", "ple_w_proj"))
    (o_hg, hg_states), got2 = _hgrn2_fwd(u, lb_logits, gather("w_ffn_gate", "w_ffn_up"))
    (h_ml, cst, nst, mst), got3 = _mlstm_fwd(qkc, u, gather("w_ffn_down"))
    if distributed:
        w_out, wpg, wpp = got1[0][0].reshape(D_MODEL, D_MODEL), got1[0][1].reshape(D_MODEL, D_MODEL), _join_cols(got1[0][2])
        wg, wu, wd = (a.reshape(D_FF, D_MODEL) for a in (got2[0][0], got2[0][1], got3[0][0]))
    else:
        w_out, wg, wu, wd, wpg, wpp = (late[n] for n in ("w_out", "w_ffn_gate", "w_ffn_up", "w_ffn_down", "ple_w_gate", "ple_w_proj"))
    m_in, z1 = _out_proj(x, u, o_hg, h_ml, g_hg, g_ml, w_out, tm)
    z2, x1, x2, a_pre, b_pre, hh = _ffn_ln(z1, ln1_g, ln1_b, wg, wu, wd, ln2_g, ln2_b, tf)
    de, dgp, dz2, loss_vec, d_bpg, d_ln2g, d_ln2b = _ple_loss_ln2_bwd(z2, p, tgt, wpg, bpg, wpp, ln2_g, ln2_b, tm)
    big = dict(ple_w_gate=_wgrad(x2, dgp, "wgrad_ple_gate", D_MODEL, D_MODEL, 2048),
               ple_w_proj=_wgrad(p, de, "wgrad_ple_proj", 512, D_MODEL, 2048))
    (da, dbb, dz1, d_ln1g, d_ln1b), r1 = _ffn_bwd_ln1_bwd(
        a_pre, b_pre, z1, dz2, wg, wu, wd, ln1_g, ln1_b, tf, scatter(rows(big["ple_w_gate"], OUT_S), _split_cols(big["ple_w_proj"], PP_S)))
    big.update(
        w_ffn_gate=_wgrad(da, x1, "wgrad_ffn_gate", D_FF, D_MODEL, 1024),
        w_ffn_up=_wgrad(dbb, x1, "wgrad_ffn_up", D_FF, D_MODEL, 1024),
        w_ffn_down=_wgrad(hh, dz2, "wgrad_ffn_down", D_FF, D_MODEL, 1024),
        w_out=_wgrad(m_in, dz1, "wgrad_w_out", D_MODEL, D_MODEL, 2048))
    d_ohg, d_hml, d_hgate, d_mo, d_ghg, d_gml = _out_proj_bwd(dz1, u, o_hg, h_ml, g_hg, g_ml, w_out, tm)
    (d_hq, d_hf, d_hv, d_lb), r2 = _hgrn2_bwd(
        u, lb_logits, d_ohg, hg_states,
        scatter(rows(big["w_ffn_gate"], FF_S), rows(big["w_ffn_up"], FF_S)))
    (d_mqk, d_mv, d_gates, d_convw, d_convb), r2b = _mlstm_bwd(
        qkc, u, d_hml, cst, nst, mst, pre, conv_w, scatter(rows(big["w_ffn_down"], FF_S), rows(big["w_out"], OUT_S)))
    du_parts = [d_hq, d_hf, d_hv, d_hgate, d_mqk, d_mv, d_mo, d_gates]
    own = lambda g: _split_cols(g[:, :PROJ_W], W_IN_S)
    (g_in0, d_bin), _ = _wgrad_w_in(x, du_parts, 512, 0)
    small = dict(b_in=d_bin[:, :PROJ_W], hg_lb_logits=d_lb, ml_conv_w=d_convw, ml_conv_b=d_convb, hg_norm_g=d_ghg,
                 ml_norm_g=d_gml, ln1_g=d_ln1g, ln1_b=d_ln1b, ln2_g=d_ln2g, ln2_b=d_ln2b, ple_b_gate=d_bpg)
    carried = [_Comm("scatter", [own(g_in0)]), _Comm("gather", [loss_vec] + [small[n] for n in SMALL])] if distributed else None
    (g_in1, _), r_in0 = _wgrad_w_in(x, du_parts, 512, 1, carried)
    big["w_in"] = jnp.concatenate([g_in0, g_in1], axis=0)
    dx, r3 = _in_proj_bwd(dz1, du_parts, w_in, tm, scatter(own(g_in1)))
    gathered_small = None
    if distributed:
        big = dict(ple_w_gate=r1[0][0], ple_w_proj=r1[0][1], w_ffn_gate=r2[0][0], w_ffn_up=r2[0][1],
                   w_ffn_down=r2b[0][0], w_out=r2b[0][1], w_in=[r_in0[0][0], r3[0][0]])
        gathered_small = r_in0[1]
    return loss_vec, dx, big, small, gathered_small


SMALL = ("b_in", "hg_lb_logits", "ml_conv_w", "ml_conv_b", "hg_norm_g", "ml_norm_g", "ln1_g", "ln1_b", "ln2_g", "ln2_b",
         "ple_b_gate")


def _padc(a, n):
    return jnp.pad(a, [(0, 0)] * (a.ndim - 1) + [(0, n - a.shape[-1])])


def _adamw(w, g, m, v):
    m = B1 * m + (1.0 - B1) * g
    v = B2 * v + (1.0 - B2) * jnp.square(g)
    m_hat = m / (1.0 - B1 ** STEP)
    v_hat = v / (1.0 - B2 ** STEP)
    return -LR * (m_hat / (jnp.sqrt(v_hat) + EPS) + WD * w), m, v


def _sum_slabs(ref):
    g = ref[0].astype(F32)
    for j in range(1, N_DEV):
        g = g + ref[j].astype(F32)
    return g


def _adamw_matrix(rbs, w, m, v, name):
    rbs = list(rbs) if isinstance(rbs, (list, tuple)) else [rbs]
    nb = len(rbs)
    R, C = w.shape
    rows = R // nb
    tr = 256 if rows % 256 == 0 else rows // 2 if rows % 32 == 0 else rows
    per = rows // tr

    def body(*refs):
        w_ref, m_ref, v_ref, g_ref, d_ref, m2_ref, v2_ref = refs[nb:]
        i = pl.program_id(0)
        g = _sum_slabs(refs[0])
        for k in range(1, nb):
            g = jnp.where(i >= k * per, _sum_slabs(refs[k]), g)
        g_ref[...] = g
        d_ref[...], m2_ref[...], v2_ref[...] = _adamw(w_ref[...], g, m_ref[...], v_ref[...])

    blk = pl.BlockSpec((tr, C), lambda i: (i, 0))
    part = lambda k: pl.BlockSpec((N_DEV, tr, C), lambda i, k=k: (0, jnp.clip(i - k * per, 0, per - 1), 0))
    return pl.pallas_call(
        body, name=name, grid=(R // tr,),
        in_specs=[part(k) for k in range(nb)] + [blk, blk, blk],
        out_specs=[blk] * 4, out_shape=[_sds((R, C))] * 4, compiler_params=_cp("parallel"))(*rbs, w, m, v)


def _adamw_small(loss_g, gs, ws, ms, vs):
    n = len(ws)

    def body(*refs):
        loss_ref, g_refs, w_refs, m_refs, v_refs = refs[0], refs[1:1 + n], refs[1 + n:1 + 2 * n], refs[1 + 2 * n:1 + 3 * n], refs[1 + 3 * n:1 + 4 * n]
        outs = refs[1 + 4 * n:]
        outs[0][...] = (0.5 / D_MODEL) * jnp.sum(_sum_slabs(loss_ref), keepdims=True)
        for i in range(n):
            g = _sum_slabs(g_refs[i])
            outs[1 + i][...] = g
            outs[1 + n + i][...], outs[1 + 2 * n + i][...], outs[1 + 3 * n + i][...] = _adamw(
                w_refs[i][...], g, m_refs[i][...], v_refs[i][...])

    res = pl.pallas_call(
        body, name="adamw_small", out_shape=[_sds((1, 1))] + [_sds(w.shape) for w in ws] * 4)(loss_g, *gs, *ws, *ms, *vs)
    return res[0], [res[1 + k * n:1 + (k + 1) * n] for k in range(4)]


WEIGHTS = ("w_in", "b_in", "hg_lb_logits", "ml_conv_w", "ml_conv_b", "hg_norm_g", "ml_norm_g", "w_out", "ln1_g", "ln1_b",
           "w_ffn_gate", "w_ffn_up", "w_ffn_down", "ln2_g", "ln2_b", "ple_w_proj", "ple_w_gate", "ple_b_gate")
CONV_S = HALF // N_DEV


def kernel(x, p, w_in, b_in, hg_lb_logits, ml_conv_w, ml_conv_b, hg_norm_g, ml_norm_g, w_out, ln1_g, ln1_b, w_ffn_gate, w_ffn_up, w_ffn_down, ln2_g, ln2_b, ple_w_proj, ple_w_gate, ple_b_gate, loss_target, m_w_in, m_b_in, m_hg_lb_logits, m_ml_conv_w, m_ml_conv_b, m_hg_norm_g, m_ml_norm_g, m_w_out, m_ln1_g, m_ln1_b, m_w_ffn_gate, m_w_ffn_up, m_w_ffn_down, m_ln2_g, m_ln2_b, m_ple_w_proj, m_ple_w_gate, m_ple_b_gate, v_w_in, v_b_in, v_hg_lb_logits, v_ml_conv_w, v_ml_conv_b, v_hg_norm_g, v_ml_norm_g, v_w_out, v_ln1_g, v_ln1_b, v_w_ffn_gate, v_w_ffn_up, v_w_ffn_down, v_ln2_g, v_ln2_b, v_ple_w_proj, v_ple_w_gate, v_ple_b_gate):
    args = locals()
    me = 4 * lax.axis_index("x") + 2 * lax.axis_index("y") + lax.axis_index("c")
    shapes = {n: args[n].shape for n in WEIGHTS}
    def drop(n, a):
        a = a[0] if n in BIG or n == "ml_conv_w" else a
        return a.T if n in TRANSPOSED else a

    W = {n: drop(n, args[n]) for n in WEIGHTS}
    M = {n: drop(n, args["m_" + n]) for n in WEIGHTS}
    V = {n: drop(n, args["v_" + n]) for n in WEIGHTS}

    g_in, g_conv = _gather_two_level(
        [W["w_in"].astype(_MXU), jnp.pad(W["ml_conv_w"], ((0, 4), (0, 128 - CONV_S)))], "gather_w_in")
    w_in_full = _padc(_join_cols(g_in), PROJ_WP)
    conv_full = _join_cols(g_conv[:, :4, :CONV_S])

    _, dx, big, _, sg = _step(
        x[0], p[0, 0], loss_target[0], w_in_full, _padc(b_in, PROJ_WP), hg_lb_logits, conv_full, ml_conv_b,
        hg_norm_g, ml_norm_g, ln1_g, ln1_b, ln2_g, ln2_b, ple_b_gate, {n: W[n].astype(_MXU) for n in LATE}, True)

    upd = {n: _adamw_matrix(big[n], W[n], M[n], V[n], "adamw_" + n) for n in BIG}
    sg = dict(zip(SMALL, sg[1:]), loss=sg[0])
    sg["ml_conv_w"] = lax.dynamic_slice(sg["ml_conv_w"], (0, 0, me * CONV_S), (N_DEV, 4, CONV_S))
    loss, small_upd = _adamw_small(sg["loss"], *[[d[n] for n in SMALL] for d in (sg, W, M, V)])

    outs = []
    for kind in range(4):
        smalls = dict(zip(SMALL, small_upd[kind]))
        for n in WEIGHTS:
            o = upd[n][kind] if n in BIG else smalls[n]
            outs.append((o.T if n in TRANSPOSED else o).reshape(shapes[n]))
    return (loss.reshape(()), dx.reshape(x.shape), *outs)
```
